```python
import jax, jax.numpy as jnp
from jax import lax
import numpy as np

D_MODEL = 1024
BATCH = 8
SEQ = 4096
DEPTH = 4

CHUNK = 64
Q_BLOCK = 128
PLE_DIM = 256
D_FF = 2816
CONV_DIM = 512
CONV_GROUPS = 8
CONV_K = 3
N_HEADS = 8
NOPE_DIM = 128
ROPE_DIM = 64
V_DIM = 128
Q_LORA = 384
KV_LORA = 256
ROPE_THETA = 10000.0
EPS = 1e-6
QK_DIM = NOPE_DIM + ROPE_DIM
ATTN_SCALE = QK_DIM ** -0.5
IN_SPLITS = (CONV_DIM, CONV_DIM, CONV_DIM, Q_LORA, KV_LORA, ROPE_DIM, D_MODEL, D_MODEL)
IN_COLS = sum(IN_SPLITS)

kernel_name = "hybrid_conv_mla_macaron_ple_trunk"


def rmsnorm(x, g):
    xf = x.astype(jnp.float32)
    y = xf * lax.rsqrt(jnp.mean(xf * xf, axis=-1, keepdims=True) + EPS)
    return (y * g.astype(jnp.float32)).astype(x.dtype)


def swiglu(x, w_gu, w_down):
    g, u = jnp.split(x @ w_gu, 2, axis=-1)
    return (jax.nn.silu(g) * u) @ w_down


def rope_tables(positions):
    inv_freq = ROPE_THETA ** (-jnp.arange(0, ROPE_DIM, 2, dtype=jnp.float32) / ROPE_DIM)
    ang = positions.astype(jnp.float32)[..., None] * inv_freq
    return jnp.cos(ang), jnp.sin(ang)


def apply_rope(x, cos, sin):
    half = ROPE_DIM // 2
    xf = x.astype(jnp.float32)
    x1, x2 = xf[..., :half], xf[..., half:]
    return jnp.concatenate([x1 * cos - x2 * sin, x2 * cos + x1 * sin], axis=-1).astype(x.dtype)


def short_conv_branch(b_gate, c_gate, v, conv_w, w_conv_out):
    seq = v.shape[1]
    z = c_gate * v
    zp = jnp.pad(z, ((0, 0), (CONV_K - 1, 0), (0, 0)))
    y = conv_w[0] * zp[:, 0:seq]
    for j in range(1, CONV_K):
        y = y + conv_w[j] * zp[:, j:j + seq]
    return (b_gate * y) @ w_conv_out


def block_causal_attention(q_nope, q_rope, k_nope, k_rope, v):
    seq = q_nope.shape[1]
    outs = []
    for i in range(seq // Q_BLOCK):
        q0, q1 = i * Q_BLOCK, (i + 1) * Q_BLOCK
        kn, kr, vb = k_nope[:, :q1], k_rope[:, :q1], v[:, :q1]
        s = (jnp.einsum('bqhd,bkhd->bhqk', q_nope[:, q0:q1], kn)
             + jnp.einsum('bqhd,bkd->bhqk', q_rope[:, q0:q1], kr)).astype(jnp.float32) * ATTN_SCALE
        q_chunk = (q0 + jnp.arange(Q_BLOCK)) // CHUNK
        k_chunk = jnp.arange(q1) // CHUNK
        mask = k_chunk[None, :] <= q_chunk[:, None]
        s = jnp.where(mask, s, -1e30)
        pr = jax.nn.softmax(s, axis=-1).astype(v.dtype)
        outs.append(jnp.einsum('bhqk,bkhd->bqhd', pr, vb))
    return jnp.concatenate(outs, axis=1)


def mla_branch(q_c, kv_c, k_r, q_norm_g, kv_norm_g, w_uq, w_ukv, w_mla_out, cos, sin):
    b, s, _ = q_c.shape
    q = (rmsnorm(q_c, q_norm_g) @ w_uq).reshape(b, s, N_HEADS, QK_DIM)
    q_nope = q[..., :NOPE_DIM]
    q_rope = apply_rope(q[..., NOPE_DIM:], cos[:, :, None, :], sin[:, :, None, :])
    kv = (rmsnorm(kv_c, kv_norm_g) @ w_ukv).reshape(b, s, N_HEADS, NOPE_DIM + V_DIM)
    k_nope, v = kv[..., :NOPE_DIM], kv[..., NOPE_DIM:]
    k_rope = apply_rope(k_r, cos, sin)
    o = block_causal_attention(q_nope, q_rope, k_nope, k_rope, v)
    return o.reshape(b, s, N_HEADS * V_DIM) @ w_mla_out


def _fwd_setup_inputs(seed: int = 0) -> dict:
    key = jax.random.key(seed)
    ks = jax.random.split(key, 24)
    f32 = jnp.float32

    def w(k, shape, fan_in):
        return jax.random.normal(k, shape, f32) * (fan_in ** -0.5)

    def gain(k, shape):
        return 1.0 + 0.05 * jax.random.normal(k, shape, f32)

    x = jax.random.normal(ks[0], (BATCH, SEQ, D_MODEL), f32)
    p = jax.random.normal(ks[1], (DEPTH, BATCH, SEQ, PLE_DIM), f32)
    offset = jax.random.randint(ks[2], (BATCH, 1), 0, 4096, dtype=jnp.int32)
    positions = offset + jnp.arange(SEQ, dtype=jnp.int32)[None, :]
    return {
        "x": x,
        "p": p,
        "positions": positions,
        "ffn1_norm": gain(ks[3], (DEPTH, D_MODEL)),
        "ffn1_w_gu": w(ks[4], (DEPTH, D_MODEL, 2 * D_FF), D_MODEL),
        "ffn1_w_down": w(ks[5], (DEPTH, D_FF, D_MODEL), D_FF),
        "mix_norm": gain(ks[6], (DEPTH, D_MODEL)),
        "w_in": w(ks[7], (DEPTH, D_MODEL, IN_COLS), D_MODEL),
        "conv_w": w(ks[8], (DEPTH, CONV_K, CONV_DIM), CONV_K),
        "w_conv_out": w(ks[9], (DEPTH, CONV_DIM, D_MODEL), CONV_DIM),
        "q_norm": gain(ks[10], (DEPTH, Q_LORA)),
        "kv_norm": gain(ks[11], (DEPTH, KV_LORA)),
        "w_uq": w(ks[12], (DEPTH, Q_LORA, N_HEADS * QK_DIM), Q_LORA),
        "w_ukv": w(ks[13], (DEPTH, KV_LORA, N_HEADS * (NOPE_DIM + V_DIM)), KV_LORA),
        "w_mla_out": w(ks[14], (DEPTH, N_HEADS * V_DIM, D_MODEL), N_HEADS * V_DIM),
        "w_o": w(ks[15], (DEPTH, D_MODEL, D_MODEL), D_MODEL),
        "ffn2_norm": gain(ks[16], (DEPTH, D_MODEL)),
        "ffn2_w_gu": w(ks[17], (DEPTH, D_MODEL, 2 * D_FF), D_MODEL),
        "ffn2_w_down": w(ks[18], (DEPTH, D_FF, D_MODEL), D_FF),
        "ple_norm": gain(ks[19], (DEPTH, D_MODEL)),
        "w_ple_gate": w(ks[20], (DEPTH, D_MODEL, D_MODEL), D_MODEL),
        "w_ple_proj": w(ks[21], (DEPTH, PLE_DIM, D_MODEL), PLE_DIM),
        "final_norm": gain(ks[22], (D_MODEL,)),
    }


def _fwd_reference(x, p, positions, ffn1_norm, ffn1_w_gu, ffn1_w_down, mix_norm, w_in, conv_w,
              w_conv_out, q_norm, kv_norm, w_uq, w_ukv, w_mla_out, w_o, ffn2_norm, ffn2_w_gu,
              ffn2_w_down, ple_norm, w_ple_gate, w_ple_proj, final_norm):
    cos, sin = rope_tables(positions)
    split_pts = list(np.cumsum(IN_SPLITS)[:-1])
    h = x
    for i in range(DEPTH):
        h = h + 0.5 * swiglu(rmsnorm(h, ffn1_norm[i]), ffn1_w_gu[i], ffn1_w_down[i])

        u = rmsnorm(h, mix_norm[i])
        b_g, c_g, v_c, q_c, kv_c, k_r, g_conv, g_mla = jnp.split(u @ w_in[i], split_pts, axis=-1)
        y_conv = short_conv_branch(b_g, c_g, v_c, conv_w[i], w_conv_out[i])
        y_mla = mla_branch(q_c, kv_c, k_r, q_norm[i], kv_norm[i], w_uq[i], w_ukv[i],
                           w_mla_out[i], cos, sin)
        merged = jax.nn.sigmoid(g_conv) * y_conv + jax.nn.sigmoid(g_mla) * y_mla
        h = h + merged @ w_o[i]

        h = h + 0.5 * swiglu(rmsnorm(h, ffn2_norm[i]), ffn2_w_gu[i], ffn2_w_down[i])

        gate = jax.nn.sigmoid(rmsnorm(h, ple_norm[i]) @ w_ple_gate[i])
        h = h + gate * (p[i] @ w_ple_proj[i])
    return rmsnorm(h, final_norm)


import jax as _jax
import jax.numpy as _jnp

TWIN_FORMAT = 'train_step'
FWD_PARAMS = ['x', 'p', 'positions', 'ffn1_norm', 'ffn1_w_gu', 'ffn1_w_down', 'mix_norm', 'w_in', 'conv_w', 'w_conv_out', 'q_norm', 'kv_norm', 'w_uq', 'w_ukv', 'w_mla_out', 'w_o', 'ffn2_norm', 'ffn2_w_gu', 'ffn2_w_down', 'ple_norm', 'w_ple_gate', 'w_ple_proj', 'final_norm']
TWIN_WEIGHTS = ['ffn1_norm', 'ffn1_w_gu', 'ffn1_w_down', 'mix_norm', 'w_in', 'conv_w', 'w_conv_out', 'q_norm', 'kv_norm', 'w_uq', 'w_ukv', 'w_mla_out', 'w_o', 'ffn2_norm', 'ffn2_w_gu', 'ffn2_w_down', 'ple_norm', 'w_ple_gate', 'w_ple_proj', 'final_norm']
TWIN_DIFF_INPUT = 'x'
TWIN_INPUTS = ['x', 'p', 'positions', 'ffn1_norm', 'ffn1_w_gu', 'ffn1_w_down', 'mix_norm', 'w_in', 'conv_w', 'w_conv_out', 'q_norm', 'kv_norm', 'w_uq', 'w_ukv', 'w_mla_out', 'w_o', 'ffn2_norm', 'ffn2_w_gu', 'ffn2_w_down', 'ple_norm', 'w_ple_gate', 'w_ple_proj', 'final_norm', 'loss_target', 'm_ffn1_norm', 'm_ffn1_w_gu', 'm_ffn1_w_down', 'm_mix_norm', 'm_w_in', 'm_conv_w', 'm_w_conv_out', 'm_q_norm', 'm_kv_norm', 'm_w_uq', 'm_w_ukv', 'm_w_mla_out', 'm_w_o', 'm_ffn2_norm', 'm_ffn2_w_gu', 'm_ffn2_w_down', 'm_ple_norm', 'm_w_ple_gate', 'm_w_ple_proj', 'm_final_norm', 'v_ffn1_norm', 'v_ffn1_w_gu', 'v_ffn1_w_down', 'v_mix_norm', 'v_w_in', 'v_conv_w', 'v_w_conv_out', 'v_q_norm', 'v_kv_norm', 'v_w_uq', 'v_w_ukv', 'v_w_mla_out', 'v_w_o', 'v_ffn2_norm', 'v_ffn2_w_gu', 'v_ffn2_w_down', 'v_ple_norm', 'v_w_ple_gate', 'v_w_ple_proj', 'v_final_norm']
TWIN_OUTPUTS = ['loss', 'grad_x', 'grad_ffn1_norm', 'grad_ffn1_w_gu', 'grad_ffn1_w_down', 'grad_mix_norm', 'grad_w_in', 'grad_conv_w', 'grad_w_conv_out', 'grad_q_norm', 'grad_kv_norm', 'grad_w_uq', 'grad_w_ukv', 'grad_w_mla_out', 'grad_w_o', 'grad_ffn2_norm', 'grad_ffn2_w_gu', 'grad_ffn2_w_down', 'grad_ple_norm', 'grad_w_ple_gate', 'grad_w_ple_proj', 'grad_final_norm', 'delta_ffn1_norm', 'delta_ffn1_w_gu', 'delta_ffn1_w_down', 'delta_mix_norm', 'delta_w_in', 'delta_conv_w', 'delta_w_conv_out', 'delta_q_norm', 'delta_kv_norm', 'delta_w_uq', 'delta_w_ukv', 'delta_w_mla_out', 'delta_w_o', 'delta_ffn2_norm', 'delta_ffn2_w_gu', 'delta_ffn2_w_down', 'delta_ple_norm', 'delta_w_ple_gate', 'delta_w_ple_proj', 'delta_final_norm', 'new_m_ffn1_norm', 'new_m_ffn1_w_gu', 'new_m_ffn1_w_down', 'new_m_mix_norm', 'new_m_w_in', 'new_m_conv_w', 'new_m_w_conv_out', 'new_m_q_norm', 'new_m_kv_norm', 'new_m_w_uq', 'new_m_w_ukv', 'new_m_w_mla_out', 'new_m_w_o', 'new_m_ffn2_norm', 'new_m_ffn2_w_gu', 'new_m_ffn2_w_down', 'new_m_ple_norm', 'new_m_w_ple_gate', 'new_m_w_ple_proj', 'new_m_final_norm', 'new_v_ffn1_norm', 'new_v_ffn1_w_gu', 'new_v_ffn1_w_down', 'new_v_mix_norm', 'new_v_w_in', 'new_v_conv_w', 'new_v_w_conv_out', 'new_v_q_norm', 'new_v_kv_norm', 'new_v_w_uq', 'new_v_w_ukv', 'new_v_w_mla_out', 'new_v_w_o', 'new_v_ffn2_norm', 'new_v_ffn2_w_gu', 'new_v_ffn2_w_down', 'new_v_ple_norm', 'new_v_w_ple_gate', 'new_v_w_ple_proj', 'new_v_final_norm']
TWIN_LEAF_KINDS = {'loss': 'loss', 'grad_x': 'grad_x', 'grad_ffn1_norm': 'grad_w', 'grad_ffn1_w_gu': 'grad_w', 'grad_ffn1_w_down': 'grad_w', 'grad_mix_norm': 'grad_w', 'grad_w_in': 'grad_w', 'grad_conv_w': 'grad_w', 'grad_w_conv_out': 'grad_w', 'grad_q_norm': 'grad_w', 'grad_kv_norm': 'grad_w', 'grad_w_uq': 'grad_w', 'grad_w_ukv': 'grad_w', 'grad_w_mla_out': 'grad_w', 'grad_w_o': 'grad_w', 'grad_ffn2_norm': 'grad_w', 'grad_ffn2_w_gu': 'grad_w', 'grad_ffn2_w_down': 'grad_w', 'grad_ple_norm': 'grad_w', 'grad_w_ple_gate': 'grad_w', 'grad_w_ple_proj': 'grad_w', 'grad_final_norm': 'grad_w', 'delta_ffn1_norm': 'delta_w', 'delta_ffn1_w_gu': 'delta_w', 'delta_ffn1_w_down': 'delta_w', 'delta_mix_norm': 'delta_w', 'delta_w_in': 'delta_w', 'delta_conv_w': 'delta_w', 'delta_w_conv_out': 'delta_w', 'delta_q_norm': 'delta_w', 'delta_kv_norm': 'delta_w', 'delta_w_uq': 'delta_w', 'delta_w_ukv': 'delta_w', 'delta_w_mla_out': 'delta_w', 'delta_w_o': 'delta_w', 'delta_ffn2_norm': 'delta_w', 'delta_ffn2_w_gu': 'delta_w', 'delta_ffn2_w_down': 'delta_w', 'delta_ple_norm': 'delta_w', 'delta_w_ple_gate': 'delta_w', 'delta_w_ple_proj': 'delta_w', 'delta_final_norm': 'delta_w', 'new_m_ffn1_norm': 'new_m', 'new_m_ffn1_w_gu': 'new_m', 'new_m_ffn1_w_down': 'new_m', 'new_m_mix_norm': 'new_m', 'new_m_w_in': 'new_m', 'new_m_conv_w': 'new_m', 'new_m_w_conv_out': 'new_m', 'new_m_q_norm': 'new_m', 'new_m_kv_norm': 'new_m', 'new_m_w_uq': 'new_m', 'new_m_w_ukv': 'new_m', 'new_m_w_mla_out': 'new_m', 'new_m_w_o': 'new_m', 'new_m_ffn2_norm': 'new_m', 'new_m_ffn2_w_gu': 'new_m', 'new_m_ffn2_w_down': 'new_m', 'new_m_ple_norm': 'new_m', 'new_m_w_ple_gate': 'new_m', 'new_m_w_ple_proj': 'new_m', 'new_m_final_norm': 'new_m', 'new_v_ffn1_norm': 'new_v', 'new_v_ffn1_w_gu': 'new_v', 'new_v_ffn1_w_down': 'new_v', 'new_v_mix_norm': 'new_v', 'new_v_w_in': 'new_v', 'new_v_conv_w': 'new_v', 'new_v_w_conv_out': 'new_v', 'new_v_q_norm': 'new_v', 'new_v_kv_norm': 'new_v', 'new_v_w_uq': 'new_v', 'new_v_w_ukv': 'new_v', 'new_v_w_mla_out': 'new_v', 'new_v_w_o': 'new_v', 'new_v_ffn2_norm': 'new_v', 'new_v_ffn2_w_gu': 'new_v', 'new_v_ffn2_w_down': 'new_v', 'new_v_ple_norm': 'new_v', 'new_v_w_ple_gate': 'new_v', 'new_v_w_ple_proj': 'new_v', 'new_v_final_norm': 'new_v'}


def _forward(args):
    return _fwd_reference(*[args[k] for k in FWD_PARAMS])


def _output_shape():
    out = _jax.eval_shape(lambda: _forward(_fwd_setup_inputs(0)))
    return out.shape, out.dtype

N_MICROBATCH = 1
ADAM_LR = 0.001
ADAM_B1 = 0.9
ADAM_B2 = 0.999
ADAM_EPS = 1e-08
ADAM_WD = 0.01
ADAM_STEP = 10
PER_EXAMPLE_BATCH_AXIS = {'x': 0, 'p': 1, 'positions': 0, 'loss_target': 0}
SHARED_INPUTS = []
_WEIGHT_DTYPES = {'ffn1_norm': _jnp.float32, 'ffn1_w_gu': _jnp.float32, 'ffn1_w_down': _jnp.float32, 'mix_norm': _jnp.float32, 'w_in': _jnp.float32, 'conv_w': _jnp.float32, 'w_conv_out': _jnp.float32, 'q_norm': _jnp.float32, 'kv_norm': _jnp.float32, 'w_uq': _jnp.float32, 'w_ukv': _jnp.float32, 'w_mla_out': _jnp.float32, 'w_o': _jnp.float32, 'ffn2_norm': _jnp.float32, 'ffn2_w_gu': _jnp.float32, 'ffn2_w_down': _jnp.float32, 'ple_norm': _jnp.float32, 'w_ple_gate': _jnp.float32, 'w_ple_proj': _jnp.float32, 'final_norm': _jnp.float32}
MOMENT_SCALE = {'ffn1_norm': 7.608709e-02, 'ffn1_w_gu': 3.245791e-02, 'ffn1_w_down': 5.297854e-02, 'mix_norm': 1.364348e-01, 'w_in': 6.559536e-02, 'conv_w': 1.100283e-01, 'w_conv_out': 7.459760e-02, 'q_norm': 1.788479e-02, 'kv_norm': 3.480480e-02, 'w_uq': 9.184868e-03, 'w_ukv': 1.103111e-02, 'w_mla_out': 1.241447e-02, 'w_o': 7.545500e-02, 'ffn2_norm': 5.775428e-02, 'ffn2_w_gu': 2.438159e-02, 'ffn2_w_down': 3.981393e-02, 'ple_norm': 2.838181e-02, 'w_ple_gate': 2.781470e-02, 'w_ple_proj': 7.127286e-02, 'final_norm': 3.199368e+01}


def _to_microbatches(a, axis):
    t = _jnp.moveaxis(a, axis, 0)
    t = t.reshape((N_MICROBATCH, t.shape[0] // N_MICROBATCH) + t.shape[1:])
    return _jnp.moveaxis(t, 1, axis + 1)


def setup_inputs(seed: int = 0) -> dict:
    inp = _fwd_setup_inputs(seed)
    key = _jax.random.fold_in(_jax.random.key(seed), 7919)
    shape, _ = _output_shape()
    out = dict(inp)
    out["loss_target"] = _jax.random.normal(_jax.random.fold_in(key, 0), shape, _jnp.float32)
    for i, name in enumerate(TWIN_WEIGHTS):
        w = inp[name].astype(_jnp.float32)
        if MOMENT_SCALE is None:
            s = _jnp.sqrt(_jnp.mean(_jnp.square(w)) + 1e-30)
        else:
            s = MOMENT_SCALE[name]
        km, kv = _jax.random.split(_jax.random.fold_in(key, i + 1))
        out[name] = w
        out["m_" + name] = s * _jax.random.normal(km, w.shape, _jnp.float32)
        out["v_" + name] = (s * s) * _jax.random.uniform(kv, w.shape, _jnp.float32, 0.5, 1.5)
    if N_MICROBATCH > 1:
        for name, axis in PER_EXAMPLE_BATCH_AXIS.items():
            out[name] = _to_microbatches(out[name], axis)
    return {'x': out['x'], 'p': out['p'], 'positions': out['positions'], 'ffn1_norm': out['ffn1_norm'], 'ffn1_w_gu': out['ffn1_w_gu'], 'ffn1_w_down': out['ffn1_w_down'], 'mix_norm': out['mix_norm'], 'w_in': out['w_in'], 'conv_w': out['conv_w'], 'w_conv_out': out['w_conv_out'], 'q_norm': out['q_norm'], 'kv_norm': out['kv_norm'], 'w_uq': out['w_uq'], 'w_ukv': out['w_ukv'], 'w_mla_out': out['w_mla_out'], 'w_o': out['w_o'], 'ffn2_norm': out['ffn2_norm'], 'ffn2_w_gu': out['ffn2_w_gu'], 'ffn2_w_down': out['ffn2_w_down'], 'ple_norm': out['ple_norm'], 'w_ple_gate': out['w_ple_gate'], 'w_ple_proj': out['w_ple_proj'], 'final_norm': out['final_norm'], 'loss_target': out['loss_target'], 'm_ffn1_norm': out['m_ffn1_norm'], 'm_ffn1_w_gu': out['m_ffn1_w_gu'], 'm_ffn1_w_down': out['m_ffn1_w_down'], 'm_mix_norm': out['m_mix_norm'], 'm_w_in': out['m_w_in'], 'm_conv_w': out['m_conv_w'], 'm_w_conv_out': out['m_w_conv_out'], 'm_q_norm': out['m_q_norm'], 'm_kv_norm': out['m_kv_norm'], 'm_w_uq': out['m_w_uq'], 'm_w_ukv': out['m_w_ukv'], 'm_w_mla_out': out['m_w_mla_out'], 'm_w_o': out['m_w_o'], 'm_ffn2_norm': out['m_ffn2_norm'], 'm_ffn2_w_gu': out['m_ffn2_w_gu'], 'm_ffn2_w_down': out['m_ffn2_w_down'], 'm_ple_norm': out['m_ple_norm'], 'm_w_ple_gate': out['m_w_ple_gate'], 'm_w_ple_proj': out['m_w_ple_proj'], 'm_final_norm': out['m_final_norm'], 'v_ffn1_norm': out['v_ffn1_norm'], 'v_ffn1_w_gu': out['v_ffn1_w_gu'], 'v_ffn1_w_down': out['v_ffn1_w_down'], 'v_mix_norm': out['v_mix_norm'], 'v_w_in': out['v_w_in'], 'v_conv_w': out['v_conv_w'], 'v_w_conv_out': out['v_w_conv_out'], 'v_q_norm': out['v_q_norm'], 'v_kv_norm': out['v_kv_norm'], 'v_w_uq': out['v_w_uq'], 'v_w_ukv': out['v_w_ukv'], 'v_w_mla_out': out['v_w_mla_out'], 'v_w_o': out['v_w_o'], 'v_ffn2_norm': out['v_ffn2_norm'], 'v_ffn2_w_gu': out['v_ffn2_w_gu'], 'v_ffn2_w_down': out['v_ffn2_w_down'], 'v_ple_norm': out['v_ple_norm'], 'v_w_ple_gate': out['v_w_ple_gate'], 'v_w_ple_proj': out['v_w_ple_proj'], 'v_final_norm': out['v_final_norm']}


def _loss(weights, diff, rest, loss_target):
    with _jax.named_scope("forward"):
        args = {**rest, TWIN_DIFF_INPUT: diff, **{k: w.astype(_WEIGHT_DTYPES[k]) for k, w in weights.items()}}
        y = _forward(args)
    with _jax.named_scope("loss_head"):
        err = _jnp.square(y.astype(_jnp.float32) - loss_target)
        return 0.5 * _jnp.sum(_jnp.mean(err, axis=-1)) if err.ndim else 0.5 * err


def _adamw(w, g, m, v):
    m = ADAM_B1 * m + (1.0 - ADAM_B1) * g
    v = ADAM_B2 * v + (1.0 - ADAM_B2) * _jnp.square(g)
    m_hat = m / (1.0 - ADAM_B1 ** ADAM_STEP)
    v_hat = v / (1.0 - ADAM_B2 ** ADAM_STEP)
    delta = -ADAM_LR * (m_hat / (_jnp.sqrt(v_hat) + ADAM_EPS) + ADAM_WD * w)
    return delta, m, v


def reference(x, p, positions, ffn1_norm, ffn1_w_gu, ffn1_w_down, mix_norm, w_in, conv_w, w_conv_out, q_norm, kv_norm, w_uq, w_ukv, w_mla_out, w_o, ffn2_norm, ffn2_w_gu, ffn2_w_down, ple_norm, w_ple_gate, w_ple_proj, final_norm, loss_target, m_ffn1_norm, m_ffn1_w_gu, m_ffn1_w_down, m_mix_norm, m_w_in, m_conv_w, m_w_conv_out, m_q_norm, m_kv_norm, m_w_uq, m_w_ukv, m_w_mla_out, m_w_o, m_ffn2_norm, m_ffn2_w_gu, m_ffn2_w_down, m_ple_norm, m_w_ple_gate, m_w_ple_proj, m_final_norm, v_ffn1_norm, v_ffn1_w_gu, v_ffn1_w_down, v_mix_norm, v_w_in, v_conv_w, v_w_conv_out, v_q_norm, v_kv_norm, v_w_uq, v_w_ukv, v_w_mla_out, v_w_o, v_ffn2_norm, v_ffn2_w_gu, v_ffn2_w_down, v_ple_norm, v_w_ple_gate, v_w_ple_proj, v_final_norm):
    given = dict(x=x, p=p, positions=positions, ffn1_norm=ffn1_norm, ffn1_w_gu=ffn1_w_gu, ffn1_w_down=ffn1_w_down, mix_norm=mix_norm, w_in=w_in, conv_w=conv_w, w_conv_out=w_conv_out, q_norm=q_norm, kv_norm=kv_norm, w_uq=w_uq, w_ukv=w_ukv, w_mla_out=w_mla_out, w_o=w_o, ffn2_norm=ffn2_norm, ffn2_w_gu=ffn2_w_gu, ffn2_w_down=ffn2_w_down, ple_norm=ple_norm, w_ple_gate=w_ple_gate, w_ple_proj=w_ple_proj, final_norm=final_norm, loss_target=loss_target, m_ffn1_norm=m_ffn1_norm, m_ffn1_w_gu=m_ffn1_w_gu, m_ffn1_w_down=m_ffn1_w_down, m_mix_norm=m_mix_norm, m_w_in=m_w_in, m_conv_w=m_conv_w, m_w_conv_out=m_w_conv_out, m_q_norm=m_q_norm, m_kv_norm=m_kv_norm, m_w_uq=m_w_uq, m_w_ukv=m_w_ukv, m_w_mla_out=m_w_mla_out, m_w_o=m_w_o, m_ffn2_norm=m_ffn2_norm, m_ffn2_w_gu=m_ffn2_w_gu, m_ffn2_w_down=m_ffn2_w_down, m_ple_norm=m_ple_norm, m_w_ple_gate=m_w_ple_gate, m_w_ple_proj=m_w_ple_proj, m_final_norm=m_final_norm, v_ffn1_norm=v_ffn1_norm, v_ffn1_w_gu=v_ffn1_w_gu, v_ffn1_w_down=v_ffn1_w_down, v_mix_norm=v_mix_norm, v_w_in=v_w_in, v_conv_w=v_conv_w, v_w_conv_out=v_w_conv_out, v_q_norm=v_q_norm, v_kv_norm=v_kv_norm, v_w_uq=v_w_uq, v_w_ukv=v_w_ukv, v_w_mla_out=v_w_mla_out, v_w_o=v_w_o, v_ffn2_norm=v_ffn2_norm, v_ffn2_w_gu=v_ffn2_w_gu, v_ffn2_w_down=v_ffn2_w_down, v_ple_norm=v_ple_norm, v_w_ple_gate=v_w_ple_gate, v_w_ple_proj=v_w_ple_proj, v_final_norm=v_final_norm)
    weights = {n: given[n] for n in TWIN_WEIGHTS}
    shared = {n: given[n] for n in SHARED_INPUTS}
    per_example = {n: given[n] for n in ['x', 'p', 'positions']}
    grad_fn = _jax.value_and_grad(_loss, argnums=(0, 1))

    def one_microbatch(ex, loss_target):
        ex = dict(ex)
        diff = ex.pop(TWIN_DIFF_INPUT)
        return grad_fn(weights, diff, {**shared, **ex}, loss_target)

    if N_MICROBATCH == 1:
        loss, (grad_w, grad_x) = one_microbatch(per_example, given["loss_target"])
    else:
        def body(carry, xs):
            loss_sum, grad_sum = carry
            l_k, (gw_k, gx_k) = one_microbatch(xs[0], xs[1])
            with _jax.named_scope("update"):
                return (loss_sum + l_k, _jax.tree.map(_jnp.add, grad_sum, gw_k)), gx_k

        init = (_jnp.zeros((), _jnp.float32), _jax.tree.map(_jnp.zeros_like, weights))
        (loss, grad_w), grad_x = _jax.lax.scan(body, init, (per_example, given["loss_target"]))
    with _jax.named_scope("update"):
        delta_w, new_m, new_v = {}, {}, {}
        for n in TWIN_WEIGHTS:
            delta_w[n], new_m[n], new_v[n] = _adamw(weights[n], grad_w[n], given["m_" + n], given["v_" + n])
    return (loss, grad_x, *[grad_w[n] for n in TWIN_WEIGHTS], *[delta_w[n] for n in TWIN_WEIGHTS],
            *[new_m[n] for n in TWIN_WEIGHTS], *[new_v[n] for n in TWIN_WEIGHTS])
```

```python
import functools

import jax
import jax.numpy as jnp
from jax import lax
from jax.experimental import pallas as pl
from jax.experimental.pallas import tpu as pltpu

F32 = jnp.float32
BF16 = jnp.bfloat16

CHUNK = 64
NOPE = 128
ROPE = 64
VDIM = 128
ROPE_THETA = 10000.0
EPS = 1e-6
ATTN_SCALE = (NOPE + ROPE) ** -0.5
ADAM_LR = 0.001
ADAM_B1 = 0.9
ADAM_B2 = 0.999
ADAM_EPS = 1e-08
ADAM_WD = 0.01
ADAM_STEP = 10

LANES = 128
SUBLANES = 8
V7X_VMEM_BYTES = 64 * 1024 * 1024
VMEM_LIMIT = V7X_VMEM_BYTES * 7 // 8
HEAD_SLOT = 2 * LANES
N_DEV = 8
FLAT_COLS = 1024

NT = (((1,), (1,)), ((), ()))
TN = (((0,), (0,)), ((), ()))
MESH = pl.DeviceIdType.MESH


def _dot(a, b):
    return jnp.dot(a, b, preferred_element_type=F32)


def _dot_nt(a, b):
    return lax.dot_general(a, b, NT, preferred_element_type=F32)


def _dot_tn(a, b):
    return lax.dot_general(a, b, TN, preferred_element_type=F32)


def _sig(x):
    return 1.0 / (1.0 + jnp.exp(-x))


def _tile(n, pref, unit):
    if n <= pref:
        return n
    t = (pref // unit) * unit
    while t >= unit:
        if n % t == 0:
            return t
        t -= unit
    return n


def _call(body, name, grid, in_specs, out_specs, out_shape, scratch=()):
    return pl.pallas_call(
        body,
        name=name,
        grid=grid,
        in_specs=in_specs,
        out_specs=out_specs,
        out_shape=out_shape,
        scratch_shapes=list(scratch),
        compiler_params=pltpu.CompilerParams(
            dimension_semantics=("arbitrary",) * len(grid), vmem_limit_bytes=VMEM_LIMIT
        ),
    )


def _sds(shape, dtype):
    return jax.ShapeDtypeStruct(shape, dtype)


def _rms_fwd(x, gain):
    rstd = lax.rsqrt(jnp.mean(x * x, axis=-1, keepdims=True) + EPS)
    return x * rstd * gain, rstd


def _rms_bwd(dn, x, rstd, gain):
    xhat = x * rstd
    dgy = dn * gain
    dx = rstd * (dgy - xhat * jnp.mean(dgy * xhat, axis=-1, keepdims=True))
    return dx, jnp.sum(dn * xhat, axis=0, keepdims=True)


def _ffn_up(h, gain, wgu):
    S, D = h.shape
    F = wgu.shape[1] // 2
    tm = _tile(S, 256, SUBLANES)

    def body(h_ref, gain_ref, wg_ref, wu_ref, g_ref, u_ref, a_ref, n_ref, r_ref):
        n32, rstd = _rms_fwd(h_ref[...], gain_ref[...])
        n = n32.astype(BF16)
        n_ref[...] = n
        r_ref[...] = rstd
        g = _dot(n, wg_ref[...])
        u = _dot(n, wu_ref[...])
        g_ref[...] = g.astype(BF16)
        u_ref[...] = u.astype(BF16)
        a_ref[...] = (g * _sig(g) * u).astype(BF16)

    row = lambda w: pl.BlockSpec((tm, w), lambda i: (i, 0))
    return _call(
        body, "ffn_up", (S // tm,),
        [row(D), pl.BlockSpec((1, D), lambda i: (0, 0)),
         pl.BlockSpec((D, F), lambda i: (0, 0)), pl.BlockSpec((D, F), lambda i: (0, 1))],
        [row(F), row(F), row(F), row(D), row(1)],
        [_sds((S, F), BF16)] * 3 + [_sds((S, D), BF16), _sds((S, 1), F32)],
    )(h, gain, wgu, wgu)


def _ffn_down(a, wd, h):
    S, F = a.shape
    D = wd.shape[1]
    tm = _tile(S, 512, SUBLANES)
    tn = _tile(D, 512, LANES)

    def body(a_ref, w_ref, h_ref, o_ref):
        o_ref[...] = h_ref[...] + 0.5 * _dot(a_ref[...], w_ref[...])

    return _call(
        body, "ffn_down", (D // tn, S // tm),
        [pl.BlockSpec((tm, F), lambda j, i: (i, 0)), pl.BlockSpec((F, tn), lambda j, i: (0, j)),
         pl.BlockSpec((tm, tn), lambda j, i: (i, j))],
        pl.BlockSpec((tm, tn), lambda j, i: (i, j)),
        _sds((S, D), F32),
    )(a, wd, h)


def _mix_in(h, gain, w_bcv, w_qkr, w_gg):
    S, D = h.shape
    widths = (w_bcv.shape[1], w_qkr.shape[1], w_gg.shape[1])
    tm = _tile(S, 256, SUBLANES)

    def body(h_ref, gain_ref, w1, w2, w3, o1, o2, o3, n_ref, r_ref):
        n32, rstd = _rms_fwd(h_ref[...], gain_ref[...])
        n = n32.astype(BF16)
        n_ref[...] = n
        r_ref[...] = rstd
        o1[...] = _dot(n, w1[...])
        o2[...] = _dot(n, w2[...])
        o3[...] = _dot(n, w3[...])

    row = lambda w: pl.BlockSpec((tm, w), lambda i: (i, 0))
    full = lambda w: pl.BlockSpec((D, w), lambda i: (0, 0))
    return _call(
        body, "mix_in", (S // tm,),
        [row(D), pl.BlockSpec((1, D), lambda i: (0, 0))] + [full(w) for w in widths],
        [row(w) for w in widths] + [row(D), row(1)],
        [_sds((S, w), F32) for w in widths] + [_sds((S, D), BF16), _sds((S, 1), F32)],
    )(h, gain, w_bcv, w_qkr, w_gg)


def _conv_taps(zc):
    rows = lax.broadcasted_iota(jnp.int32, zc.shape, 0)
    z1 = jnp.where(rows >= 1, pltpu.roll(zc, 1, 0), 0.0)
    z2 = jnp.where(rows >= 2, pltpu.roll(zc, 2, 0), 0.0)
    return z1, z2


def _conv_fwd(z_bcv, conv_w):
    S = z_bcv.shape[0]
    C = conv_w.shape[1]
    nc = C // LANES

    def body(b_ref, c_ref, v_ref, w_ref, o_ref):
        w = w_ref[...]
        zc = c_ref[...] * v_ref[...]
        z1, z2 = _conv_taps(zc)
        y = w[0:1] * z2 + w[1:2] * z1 + w[2:3] * zc
        o_ref[...] = (b_ref[...] * y).astype(BF16)

    col = lambda k: pl.BlockSpec((S, LANES), lambda j: (0, j + k * nc))
    return _call(
        body, "conv_fwd", (nc,),
        [col(0), col(1), col(2), pl.BlockSpec((3, LANES), lambda j: (0, j))],
        pl.BlockSpec((S, LANES), lambda j: (0, j)),
        _sds((S, C), BF16),
    )(z_bcv, z_bcv, z_bcv, conv_w)


def _rope(x, cs, half):
    c, s1, s2 = cs[:, :LANES], cs[:, LANES:2 * LANES], cs[:, 2 * LANES:]
    return x * c + pltpu.roll(x, LANES - half, 1) * s1 + pltpu.roll(x, half, 1) * s2


def _unrope(d, cs, half):
    c, s1, s2 = cs[:, :LANES], cs[:, LANES:2 * LANES], cs[:, 2 * LANES:]
    return d * c + pltpu.roll(d * s1, half, 1) + pltpu.roll(d * s2, LANES - half, 1)


def _mla_prep(z_qkr, gq, gkv, cs, wuq, wk, wv):
    S = z_qkr.shape[0]
    QL, KVL = gq.shape[1], gkv.shape[1]
    H = wv.shape[1] // VDIM
    tm = _tile(S, 256, SUBLANES)
    half = ROPE // 2

    def body(z_ref, gq_ref, gkv_ref, cs_ref, wuq_ref, wk_ref, wv_ref,
             q_ref, k_ref, v_ref, qn_ref, kvn_ref, rq_ref, rkv_ref):
        z = z_ref[...]
        cs_t = cs_ref[...]
        qn32, rq = _rms_fwd(z[:, :QL], gq_ref[...])
        kvn32, rkv = _rms_fwd(z[:, QL:QL + KVL], gkv_ref[...])
        qn = qn32.astype(BF16)
        kvn = kvn32.astype(BF16)
        qn_ref[...] = qn
        kvn_ref[...] = kvn
        rq_ref[...] = rq
        rkv_ref[...] = rkv
        q = _dot(qn, wuq_ref[...])
        k = _dot(kvn, wk_ref[...])
        v_ref[...] = _dot(kvn, wv_ref[...]).astype(BF16)
        krope = _rope(z[:, QL + KVL:], cs_t, half).astype(BF16)
        for h in range(H):
            lo, mid, hi = h * HEAD_SLOT, h * HEAD_SLOT + LANES, (h + 1) * HEAD_SLOT
            q_ref[:, lo:mid] = q[:, lo:mid].astype(BF16)
            q_ref[:, mid:hi] = _rope(q[:, mid:hi], cs_t, half).astype(BF16)
            k_ref[:, lo:mid] = k[:, lo:mid].astype(BF16)
            k_ref[:, mid:hi] = krope

    row = lambda w: pl.BlockSpec((tm, w), lambda i: (i, 0))
    full = lambda a: pl.BlockSpec(a.shape, lambda i: (0, 0))
    return _call(
        body, "mla_prep", (S // tm,),
        [row(z_qkr.shape[1]), full(gq), full(gkv), row(3 * LANES), full(wuq), full(wk), full(wv)],
        [row(H * HEAD_SLOT), row(H * HEAD_SLOT), row(H * VDIM), row(QL), row(KVL), row(1), row(1)],
        [_sds((S, H * HEAD_SLOT), BF16), _sds((S, H * HEAD_SLOT), BF16), _sds((S, H * VDIM), BF16),
         _sds((S, QL), BF16), _sds((S, KVL), BF16), _sds((S, 1), F32), _sds((S, 1), F32)],
    )(z_qkr, gq, gkv, cs, wuq, wk, wv)


def _chunk_mask(t):
    shift = CHUNK.bit_length() - 1
    row = lax.broadcasted_iota(jnp.int32, (t, t), 0) >> shift
    col = lax.broadcasted_iota(jnp.int32, (t, t), 1) >> shift
    return col <= row


def _attn_fwd(q, k, v, H):
    S = q.shape[0]
    t = _tile(S, 512, CHUNK)
    nq = S // t

    def body(q_ref, k_ref, v_ref, o_ref, lse_ref):
        qi = pl.program_id(1)
        qv = q_ref[...]

        def block(kj, carry, masked):
            m, l, acc = carry
            off = pl.multiple_of(kj * t, t)
            s = _dot_nt(qv, k_ref[pl.ds(off, t), :]) * ATTN_SCALE
            if masked:
                s = jnp.where(_chunk_mask(t), s, -1e30)
            m_new = jnp.maximum(m, jnp.max(s, axis=-1, keepdims=True))
            alpha = jnp.exp(m - m_new)
            p = jnp.exp(s - m_new)
            l = alpha * l + jnp.sum(p, axis=-1, keepdims=True)
            acc = alpha * acc + _dot(p.astype(BF16), v_ref[pl.ds(off, t), :])
            return m_new, l, acc

        init = (jnp.full((t, 1), -1e30, F32), jnp.zeros((t, 1), F32), jnp.zeros((t, VDIM), F32))
        carry = lax.fori_loop(0, qi, lambda kj, c: block(kj, c, False), init)
        m, l, acc = block(qi, carry, True)
        o_ref[...] = (acc / l).astype(BF16)
        lse_ref[...] = jnp.broadcast_to(m + jnp.log(l), (t, LANES))

    return _call(
        body, "attn_fwd", (H, nq),
        [pl.BlockSpec((t, HEAD_SLOT), lambda h, i: (i, h)), pl.BlockSpec((S, HEAD_SLOT), lambda h, i: (0, h)),
         pl.BlockSpec((S, VDIM), lambda h, i: (0, h))],
        [pl.BlockSpec((t, VDIM), lambda h, i: (i, h)), pl.BlockSpec((t, LANES), lambda h, i: (i, h))],
        [_sds((S, H * VDIM), BF16), _sds((S, H * LANES), F32)],
    )(q, k, v)


def _merge_wo(o, by, z_gg, h, wmo, wco, wo):
    S, D = h.shape
    tm = _tile(S, 512, SUBLANES)

    def body(o_ref, by_ref, gg_ref, h_ref, wmo_ref, wco_ref, wo_ref, h2_ref, mg_ref, yc_ref, ym_ref):
        ymla = _dot(o_ref[...], wmo_ref[...])
        yconv = _dot(by_ref[...], wco_ref[...])
        gg = gg_ref[...]
        merged = (_sig(gg[:, :D]) * yconv + _sig(gg[:, D:]) * ymla).astype(BF16)
        mg_ref[...] = merged
        yc_ref[...] = yconv.astype(BF16)
        ym_ref[...] = ymla.astype(BF16)
        h2_ref[...] = h_ref[...] + _dot(merged, wo_ref[...])

    row = lambda w: pl.BlockSpec((tm, w), lambda i: (i, 0))
    full = lambda a: pl.BlockSpec(a.shape, lambda i: (0, 0))
    return _call(
        body, "merge_wo", (S // tm,),
        [row(o.shape[1]), row(by.shape[1]), row(2 * D), row(D), full(wmo), full(wco), full(wo)],
        [row(D)] * 4,
        [_sds((S, D), F32)] + [_sds((S, D), BF16)] * 3,
    )(o, by, z_gg, h, wmo, wco, wo)


def _ple_fwd(h, gain, p, wpg, wpp):
    S, D = h.shape
    P = p.shape[1]
    tm = _tile(S, 512, SUBLANES)

    def body(h_ref, gain_ref, p_ref, wpg_ref, wpp_ref, o_ref, pre_ref, pp_ref, n_ref, r_ref):
        x = h_ref[...]
        n32, rstd = _rms_fwd(x, gain_ref[...])
        n = n32.astype(BF16)
        n_ref[...] = n
        r_ref[...] = rstd
        pre = _dot(n, wpg_ref[...])
        pp = _dot(p_ref[...].astype(BF16), wpp_ref[...])
        pre_ref[...] = pre
        pp_ref[...] = pp
        o_ref[...] = x + _sig(pre) * pp

    row = lambda w: pl.BlockSpec((tm, w), lambda i: (i, 0))
    full = lambda a: pl.BlockSpec(a.shape, lambda i: (0, 0))
    return _call(
        body, "ple_fwd", (S // tm,),
        [row(D), full(gain), row(P), full(wpg), full(wpp)],
        [row(D), row(D), row(D), row(D), row(1)],
        [_sds((S, D), F32)] * 3 + [_sds((S, D), BF16), _sds((S, 1), F32)],
    )(h, gain, p, wpg, wpp)


def _final_loss(h, gain, target):
    S, D = h.shape
    tm = _tile(S, 512, SUBLANES)

    def body(h_ref, gain_ref, t_ref, dh_ref, loss_ref, dg_ref):
        @pl.when(pl.program_id(0) == 0)
        def _():
            loss_ref[...] = jnp.zeros_like(loss_ref)
            dg_ref[...] = jnp.zeros_like(dg_ref)

        x = h_ref[...]
        gain_v = gain_ref[...]
        y, rstd = _rms_fwd(x, gain_v)
        err = y - t_ref[...]
        loss_ref[...] += 0.5 * jnp.sum(jnp.mean(err * err, axis=-1, keepdims=True))
        dx, dgain = _rms_bwd(err * (1.0 / D), x, rstd, gain_v)
        dh_ref[...] = dx
        dg_ref[...] += dgain

    row = pl.BlockSpec((tm, D), lambda i: (i, 0))
    return _call(
        body, "final_loss", (S // tm,),
        [row, pl.BlockSpec((1, D), lambda i: (0, 0)), row],
        [row, pl.BlockSpec((1, LANES), lambda i: (0, 0)), pl.BlockSpec((1, D), lambda i: (0, 0))],
        [_sds((S, D), F32), _sds((1, LANES), F32), _sds((1, D), F32)],
    )(h, gain, target)


def _mm_tn(x, dy, scale=1.0):
    S, K = x.shape
    N = dy.shape[1]
    tk = _tile(K, 512, LANES)
    budget = 12 * 1024 * 1024
    tn = _tile(N, max(LANES, budget // (S * dy.dtype.itemsize) // LANES * LANES), LANES)

    def body(x_ref, dy_ref, o_ref):
        r = _dot_tn(x_ref[...].astype(BF16), dy_ref[...].astype(BF16))
        o_ref[...] = r if scale == 1.0 else scale * r

    return _call(
        body, "mm_tn", (K // tk, N // tn),
        [pl.BlockSpec((S, tk), lambda i, j: (0, i)), pl.BlockSpec((S, tn), lambda i, j: (0, j))],
        pl.BlockSpec((tk, tn), lambda i, j: (i, j)),
        _sds((K, N), F32),
    )(x, dy)


def _ple_bwd(dh, pre, pp, h, rstd, gain, wpg):
    S, D = h.shape
    tm = _tile(S, 512, SUBLANES)

    def body(dh_ref, pre_ref, pp_ref, h_ref, r_ref, gain_ref, wpg_ref, o_ref, dpre_ref, dpp_ref, dg_ref):
        @pl.when(pl.program_id(0) == 0)
        def _():
            dg_ref[...] = jnp.zeros_like(dg_ref)

        d = dh_ref[...]
        gate = _sig(pre_ref[...])
        dpre = (d * pp_ref[...] * gate * (1.0 - gate)).astype(BF16)
        dpre_ref[...] = dpre
        dpp_ref[...] = (d * gate).astype(BF16)
        dx, dgain = _rms_bwd(_dot_nt(dpre, wpg_ref[...]), h_ref[...], r_ref[...], gain_ref[...])
        o_ref[...] = d + dx
        dg_ref[...] += dgain

    row = lambda w: pl.BlockSpec((tm, w), lambda i: (i, 0))
    full = lambda a: pl.BlockSpec(a.shape, lambda i: (0, 0))
    return _call(
        body, "ple_bwd", (S // tm,),
        [row(D), row(D), row(D), row(D), row(1), full(gain), full(wpg)],
        [row(D), row(D), row(D), pl.BlockSpec((1, D), lambda i: (0, 0))],
        [_sds((S, D), F32), _sds((S, D), BF16), _sds((S, D), BF16), _sds((1, D), F32)],
    )(dh, pre, pp, h, rstd, gain, wpg)


def _ffn_bwd_act(dh, wd, g, u):
    S, D = dh.shape
    F = wd.shape[0]
    tm = _tile(S, 512, SUBLANES)
    tn = _tile(F, 1408, LANES)

    def body(dh_ref, wd_ref, g_ref, u_ref, dg_ref, du_ref):
        da = 0.5 * _dot_nt(dh_ref[...].astype(BF16), wd_ref[...])
        g = g_ref[...].astype(F32)
        u = u_ref[...].astype(F32)
        sg = _sig(g)
        dg_ref[...] = (da * u * sg * (1.0 + g * (1.0 - sg))).astype(BF16)
        du_ref[...] = (da * g * sg).astype(BF16)

    act = pl.BlockSpec((tm, tn), lambda j, i: (i, j))
    return _call(
        body, "ffn_bwd_act", (F // tn, S // tm),
        [pl.BlockSpec((tm, D), lambda j, i: (i, 0)), pl.BlockSpec((tn, D), lambda j, i: (j, 0)), act, act],
        [act, act],
        [_sds((S, F), BF16)] * 2,
    )(dh, wd, g, u)


def _ffn_bwd_in(dg, du, wgu, h, rstd, gain, dh):
    S, D = h.shape
    F = dg.shape[1]
    tm = _tile(S, 512, SUBLANES)

    def body(dg_ref, du_ref, wg_ref, wu_ref, h_ref, r_ref, gain_ref, dh_ref, o_ref, dgain_ref):
        @pl.when(pl.program_id(0) == 0)
        def _():
            dgain_ref[...] = jnp.zeros_like(dgain_ref)

        dn = _dot_nt(dg_ref[...], wg_ref[...]) + _dot_nt(du_ref[...], wu_ref[...])
        dx, dgain = _rms_bwd(dn, h_ref[...], r_ref[...], gain_ref[...])
        o_ref[...] = dh_ref[...] + dx
        dgain_ref[...] += dgain

    row = lambda w: pl.BlockSpec((tm, w), lambda i: (i, 0))
    return _call(
        body, "ffn_bwd_in", (S // tm,),
        [row(F), row(F), pl.BlockSpec((D, F), lambda i: (0, 0)), pl.BlockSpec((D, F), lambda i: (0, 1)),
         row(D), row(1), pl.BlockSpec((1, D), lambda i: (0, 0)), row(D)],
        [row(D), pl.BlockSpec((1, D), lambda i: (0, 0))],
        [_sds((S, D), F32), _sds((1, D), F32)],
    )(dg, du, wgu, wgu, h, rstd, gain, dh)


def _merge_bwd(dh, wo, z_gg, yconv, ymla, wco, wmo):
    S, D = dh.shape
    C = wco.shape[0]
    HV = wmo.shape[0]
    tm = _tile(S, 512, SUBLANES)

    def body(dh_ref, wo_ref, gg_ref, yc_ref, ym_ref, wco_ref, wmo_ref, dgg_ref, dby_ref, do_ref, dyc_ref, dym_ref):
        dm = _dot_nt(dh_ref[...].astype(BF16), wo_ref[...])
        gg = gg_ref[...]
        sgc = _sig(gg[:, :D])
        sgm = _sig(gg[:, D:])
        dyc = (dm * sgc).astype(BF16)
        dym = (dm * sgm).astype(BF16)
        dyc_ref[...] = dyc
        dym_ref[...] = dym
        dgg_ref[:, :D] = (dm * yc_ref[...].astype(F32) * sgc * (1.0 - sgc)).astype(BF16)
        dgg_ref[:, D:] = (dm * ym_ref[...].astype(F32) * sgm * (1.0 - sgm)).astype(BF16)
        dby_ref[...] = _dot_nt(dyc, wco_ref[...])
        do_ref[...] = _dot_nt(dym, wmo_ref[...]).astype(BF16)

    row = lambda w: pl.BlockSpec((tm, w), lambda i: (i, 0))
    full = lambda a: pl.BlockSpec(a.shape, lambda i: (0, 0))
    return _call(
        body, "merge_bwd", (S // tm,),
        [row(D), full(wo), row(2 * D), row(D), row(D), full(wco), full(wmo)],
        [row(2 * D), row(C), row(HV), row(D), row(D)],
        [_sds((S, 2 * D), BF16), _sds((S, C), F32), _sds((S, HV), BF16), _sds((S, D), BF16), _sds((S, D), BF16)],
    )(dh, wo, z_gg, yconv, ymla, wco, wmo)


def _conv_bwd(z_bcv, conv_w, dby):
    S = z_bcv.shape[0]
    C = conv_w.shape[1]
    nc = C // LANES

    def body(b_ref, c_ref, v_ref, w_ref, dby_ref, db_ref, dc_ref, dv_ref, dw_ref):
        w = w_ref[...]
        c = c_ref[...]
        v = v_ref[...]
        d = dby_ref[...]
        zc = c * v
        z1, z2 = _conv_taps(zc)
        y = w[0:1] * z2 + w[1:2] * z1 + w[2:3] * zc
        db_ref[...] = (d * y).astype(BF16)
        dy = d * b_ref[...]
        rows = lax.broadcasted_iota(jnp.int32, dy.shape, 0)
        dy1 = jnp.where(rows < S - 1, pltpu.roll(dy, S - 1, 0), 0.0)
        dy2 = jnp.where(rows < S - 2, pltpu.roll(dy, S - 2, 0), 0.0)
        dzc = w[2:3] * dy + w[1:2] * dy1 + w[0:1] * dy2
        dc_ref[...] = (dzc * v).astype(BF16)
        dv_ref[...] = (dzc * c).astype(BF16)
        dw_ref[0:1, :] = jnp.sum(dy * z2, axis=0, keepdims=True)
        dw_ref[1:2, :] = jnp.sum(dy * z1, axis=0, keepdims=True)
        dw_ref[2:3, :] = jnp.sum(dy * zc, axis=0, keepdims=True)

    col = lambda k: pl.BlockSpec((S, LANES), lambda j: (0, j + k * nc))
    one = pl.BlockSpec((S, LANES), lambda j: (0, j))
    wspec = pl.BlockSpec((3, LANES), lambda j: (0, j))
    return _call(
        body, "conv_bwd", (nc,),
        [col(0), col(1), col(2), wspec, one],
        [one, one, one, wspec],
        [_sds((S, C), BF16)] * 3 + [_sds((3, C), F32)],
    )(z_bcv, z_bcv, z_bcv, conv_w, dby)


def _attn_bwd_q(q, k, v, do, o, lse, H):
    S = q.shape[0]
    t = _tile(S, 512, CHUNK)
    nq = S // t

    def body(q_ref, k_ref, v_ref, do_ref, o_ref, lse_ref, dq_ref, dl_ref):
        qi = pl.program_id(1)
        qv = q_ref[...]
        dov = do_ref[...]
        lse_v = lse_ref[...][:, :1]
        delta = jnp.sum(dov.astype(F32) * o_ref[...].astype(F32), axis=-1, keepdims=True)
        dl_ref[...] = jnp.broadcast_to(delta, (t, LANES))

        def block(kj, dq, masked):
            off = pl.multiple_of(kj * t, t)
            kv = k_ref[pl.ds(off, t), :]
            s = _dot_nt(qv, kv) * ATTN_SCALE
            if masked:
                s = jnp.where(_chunk_mask(t), s, -1e30)
            p = jnp.exp(s - lse_v)
            dp = _dot_nt(dov, v_ref[pl.ds(off, t), :])
            ds = (p * (dp - delta) * ATTN_SCALE).astype(BF16)
            return dq + _dot(ds, kv)

        dq = lax.fori_loop(0, qi, lambda kj, c: block(kj, c, False), jnp.zeros((t, HEAD_SLOT), F32))
        dq_ref[...] = block(qi, dq, True)

    qspec = lambda w: pl.BlockSpec((t, w), lambda h, i: (i, h))
    kspec = lambda w: pl.BlockSpec((S, w), lambda h, i: (0, h))
    return _call(
        body, "attn_bwd_q", (H, nq),
        [qspec(HEAD_SLOT), kspec(HEAD_SLOT), kspec(VDIM), qspec(VDIM), qspec(VDIM), qspec(LANES)],
        [qspec(HEAD_SLOT), qspec(LANES)],
        [_sds((S, H * HEAD_SLOT), F32), _sds((S, H * LANES), F32)],
    )(q, k, v, do, o, lse)


def _attn_bwd_kv(q, k, v, do, lse, delta, H):
    S = q.shape[0]
    t = _tile(S, 512, CHUNK)
    nk = S // t

    def body(q_ref, k_ref, v_ref, do_ref, lse_ref, dl_ref, dk_ref, dv_ref):
        kj = pl.program_id(1)
        kv = k_ref[...]
        vv = v_ref[...]

        def block(qi, carry, masked):
            dk, dv = carry
            off = pl.multiple_of(qi * t, t)
            qv = q_ref[pl.ds(off, t), :]
            dov = do_ref[pl.ds(off, t), :]
            s = _dot_nt(qv, kv) * ATTN_SCALE
            if masked:
                s = jnp.where(_chunk_mask(t), s, -1e30)
            p = jnp.exp(s - lse_ref[pl.ds(off, t), :][:, :1])
            dp = _dot_nt(dov, vv)
            ds = (p * (dp - dl_ref[pl.ds(off, t), :][:, :1]) * ATTN_SCALE).astype(BF16)
            return dk + _dot_tn(ds, qv), dv + _dot_tn(p.astype(BF16), dov)

        init = (jnp.zeros((t, HEAD_SLOT), F32), jnp.zeros((t, VDIM), F32))
        carry = block(kj, init, True)
        dk, dv = lax.fori_loop(kj + 1, nk, lambda qi, c: block(qi, c, False), carry)
        dk_ref[...] = dk
        dv_ref[...] = dv.astype(BF16)

    kspec = lambda w: pl.BlockSpec((t, w), lambda h, j: (j, h))
    qspec = lambda w: pl.BlockSpec((S, w), lambda h, j: (0, h))
    return _call(
        body, "attn_bwd_kv", (H, nk),
        [qspec(HEAD_SLOT), kspec(HEAD_SLOT), kspec(VDIM), qspec(VDIM), qspec(LANES), qspec(LANES)],
        [kspec(HEAD_SLOT), kspec(VDIM)],
        [_sds((S, H * HEAD_SLOT), F32), _sds((S, H * VDIM), BF16)],
    )(q, k, v, do, lse, delta)


def _mla_prep_bwd(dq, dk, dv, z_qkr, rq, rkv, gq, gkv, cs, wuq, wk, wv):
    S = z_qkr.shape[0]
    QL, KVL = gq.shape[1], gkv.shape[1]
    H = wv.shape[1] // VDIM
    tm = _tile(S, 256, SUBLANES)
    half = ROPE // 2

    def body(dq_ref, dk_ref, dv_ref, z_ref, rq_ref, rkv_ref, gq_ref, gkv_ref, cs_ref, wuq_ref, wk_ref, wv_ref,
             dz_ref, dqp_ref, dkp_ref, dgq_ref, dgkv_ref):
        @pl.when(pl.program_id(0) == 0)
        def _():
            dgq_ref[...] = jnp.zeros_like(dgq_ref)
            dgkv_ref[...] = jnp.zeros_like(dgkv_ref)

        cs_t = cs_ref[...]
        dkr = jnp.zeros((tm, LANES), F32)
        for h in range(H):
            lo, mid, hi = h * HEAD_SLOT, h * HEAD_SLOT + LANES, (h + 1) * HEAD_SLOT
            dqp_ref[:, lo:mid] = dq_ref[:, lo:mid].astype(BF16)
            dqp_ref[:, mid:hi] = _unrope(dq_ref[:, mid:hi], cs_t, half).astype(BF16)
            dkp_ref[:, lo:mid] = dk_ref[:, lo:mid].astype(BF16)
            dkp_ref[:, mid:hi] = jnp.zeros((tm, LANES), BF16)
            dkr = dkr + dk_ref[:, mid:hi]
        dqn = _dot_nt(dqp_ref[...], wuq_ref[...])
        dkvn = _dot_nt(dkp_ref[...], wk_ref[...]) + _dot_nt(dv_ref[...], wv_ref[...])
        z = z_ref[...]
        dqc, dgq = _rms_bwd(dqn, z[:, :QL], rq_ref[...], gq_ref[...])
        dkvc, dgkv = _rms_bwd(dkvn, z[:, QL:QL + KVL], rkv_ref[...], gkv_ref[...])
        dz_ref[:, :QL] = dqc.astype(BF16)
        dz_ref[:, QL:QL + KVL] = dkvc.astype(BF16)
        dz_ref[:, QL + KVL:] = _unrope(dkr, cs_t, half).astype(BF16)
        dgq_ref[...] += dgq
        dgkv_ref[...] += dgkv

    row = lambda w: pl.BlockSpec((tm, w), lambda i: (i, 0))
    full = lambda a: pl.BlockSpec(a.shape, lambda i: (0, 0))
    W = z_qkr.shape[1]
    return _call(
        body, "mla_prep_bwd", (S // tm,),
        [row(H * HEAD_SLOT), row(H * HEAD_SLOT), row(H * VDIM), row(W), row(1), row(1), full(gq), full(gkv),
         row(3 * LANES), full(wuq), full(wk), full(wv)],
        [row(W), row(H * HEAD_SLOT), row(H * HEAD_SLOT), full(gq), full(gkv)],
        [_sds((S, W), BF16), _sds((S, H * HEAD_SLOT), BF16), _sds((S, H * HEAD_SLOT), BF16),
         _sds((1, QL), F32), _sds((1, KVL), F32)],
    )(dq, dk, dv, z_qkr, rq, rkv, gq, gkv, cs, wuq, wk, wv)


def _mix_in_bwd(db, dc, dv, dz_qkr, dgg, w_bcv, w_qkr, w_gg, h, rstd, gain, dh):
    S, D = h.shape
    C = db.shape[1]
    tm = _tile(S, 512, SUBLANES)

    def body(db_ref, dc_ref, dv_ref, dq_ref, dgg_ref, wb_ref, wc_ref, wv_ref, wq_ref, wg_ref,
             h_ref, r_ref, gain_ref, dh_ref, o_ref, dgain_ref):
        @pl.when(pl.program_id(0) == 0)
        def _():
            dgain_ref[...] = jnp.zeros_like(dgain_ref)

        dn = (_dot_nt(db_ref[...], wb_ref[...]) + _dot_nt(dc_ref[...], wc_ref[...])
              + _dot_nt(dv_ref[...], wv_ref[...]) + _dot_nt(dq_ref[...], wq_ref[...])
              + _dot_nt(dgg_ref[...], wg_ref[...]))
        dx, dgain = _rms_bwd(dn, h_ref[...], r_ref[...], gain_ref[...])
        o_ref[...] = dh_ref[...] + dx
        dgain_ref[...] += dgain

    row = lambda w: pl.BlockSpec((tm, w), lambda i: (i, 0))
    full = lambda a: pl.BlockSpec(a.shape, lambda i: (0, 0))
    wcol = lambda k: pl.BlockSpec((D, C), lambda i: (0, k))
    return _call(
        body, "mix_in_bwd", (S // tm,),
        [row(C), row(C), row(C), row(dz_qkr.shape[1]), row(dgg.shape[1]), wcol(0), wcol(1), wcol(2),
         full(w_qkr), full(w_gg), row(D), row(1), full(gain), row(D)],
        [row(D), pl.BlockSpec((1, D), lambda i: (0, 0))],
        [_sds((S, D), F32), _sds((1, D), F32)],
    )(db, dc, dv, dz_qkr, dgg, w_bcv, w_bcv, w_bcv, w_qkr, w_gg, h, rstd, gain, dh)


def _rope_tables(positions):
    half = ROPE // 2
    inv_freq = ROPE_THETA ** (-jnp.arange(0, ROPE, 2, dtype=F32) / ROPE)
    ang = positions.astype(F32)[:, None] * inv_freq
    cos, sin = jnp.cos(ang), jnp.sin(ang)
    z = jnp.zeros_like(cos)
    pad = jnp.zeros((positions.shape[0], LANES - 2 * half), F32)
    return jnp.concatenate([cos, cos, pad, -sin, z, pad, z, sin, pad], axis=1)


def _layer_fwd(h0, p_l, cs, w):
    H = w["wv"].shape[1] // VDIM
    g1, u1, a1, n1, r1 = _ffn_up(h0, w["ffn1_norm"], w["ffn1_w_gu"])
    h1 = _ffn_down(a1, w["ffn1_w_down"], h0)
    z_bcv, z_qkr, z_gg, un, rm = _mix_in(h1, w["mix_norm"], w["w_bcv"], w["w_qkr"], w["w_gg"])
    by = _conv_fwd(z_bcv, w["conv_w"])
    q, k, v, qn, kvn, rq, rkv = _mla_prep(z_qkr, w["q_norm"], w["kv_norm"], cs, w["wuq"], w["wk"], w["wv"])
    o, lse = _attn_fwd(q, k, v, H)
    h2, merged, yconv, ymla = _merge_wo(o, by, z_gg, h1, w["w_mla_out"], w["w_conv_out"], w["w_o"])
    g2, u2, a2, n2, r2 = _ffn_up(h2, w["ffn2_norm"], w["ffn2_w_gu"])
    h3 = _ffn_down(a2, w["ffn2_w_down"], h2)
    h4, pre, pp, pn, rp = _ple_fwd(h3, w["ple_norm"], p_l, w["w_ple_gate"], w["w_ple_proj"])
    saved = dict(h0=h0, g1=g1, u1=u1, a1=a1, n1=n1, r1=r1, h1=h1, z_bcv=z_bcv, z_qkr=z_qkr, z_gg=z_gg, un=un,
                 rm=rm, by=by, q=q, k=k, v=v, qn=qn, kvn=kvn, rq=rq, rkv=rkv, o=o, lse=lse, h2=h2, merged=merged,
                 yconv=yconv, ymla=ymla, g2=g2, u2=u2, a2=a2, n2=n2, r2=r2, h3=h3, pre=pre, pp=pp, pn=pn, rp=rp,
                 p=p_l)
    return h4, saved


def _ffn_bwd(dh, s, w, tag):
    hin = s["h0"] if tag == 1 else s["h2"]
    g, u, a, n, r = (s[f"{x}{tag}"] for x in ("g", "u", "a", "n", "r"))
    wgu, wd, gain = w[f"ffn{tag}_w_gu"], w[f"ffn{tag}_w_down"], w[f"ffn{tag}_norm"]
    dg, du = _ffn_bwd_act(dh, wd, g, u)
    d_wd = _mm_tn(a, dh, 0.5)
    d_wgu = jnp.concatenate([_mm_tn(n, dg), _mm_tn(n, du)], axis=1)
    dh_in, d_gain = _ffn_bwd_in(dg, du, wgu, hin, r, gain, dh)
    return dh_in, {f"ffn{tag}_w_gu": d_wgu, f"ffn{tag}_w_down": d_wd, f"ffn{tag}_norm": d_gain}


def _layer_bwd(dh4, s, cs, w):
    H = w["wv"].shape[1] // VDIM
    grads = {}
    dh3, dpre, dpp, grads["ple_norm"] = _ple_bwd(dh4, s["pre"], s["pp"], s["h3"], s["rp"], w["ple_norm"],
                                                 w["w_ple_gate"])
    grads["w_ple_gate"] = _mm_tn(s["pn"], dpre)
    grads["w_ple_proj"] = _mm_tn(s["p"], dpp)
    dh2, g2 = _ffn_bwd(dh3, s, w, 2)
    grads.update(g2)
    dgg, dby, do, dyc, dym = _merge_bwd(dh2, w["w_o"], s["z_gg"], s["yconv"], s["ymla"], w["w_conv_out"],
                                        w["w_mla_out"])
    grads["w_o"] = _mm_tn(s["merged"], dh2)
    grads["w_conv_out"] = _mm_tn(s["by"], dyc)
    grads["w_mla_out"] = _mm_tn(s["o"], dym)
    db, dc, dv_c, grads["conv_w"] = _conv_bwd(s["z_bcv"], w["conv_w"], dby)
    dq, delta = _attn_bwd_q(s["q"], s["k"], s["v"], do, s["o"], s["lse"], H)
    dk, dv = _attn_bwd_kv(s["q"], s["k"], s["v"], do, s["lse"], delta, H)
    dz_qkr, dqp, dkp, grads["q_norm"], grads["kv_norm"] = _mla_prep_bwd(
        dq, dk, dv, s["z_qkr"], s["rq"], s["rkv"], w["q_norm"], w["kv_norm"], cs, w["wuq"], w["wk"], w["wv"])
    grads["wuq"] = _mm_tn(s["qn"], dqp)
    grads["wk"] = _mm_tn(s["kvn"], dkp)
    grads["wv"] = _mm_tn(s["kvn"], dv)
    un = s["un"]
    grads["w_bcv"] = jnp.concatenate([_mm_tn(un, db), _mm_tn(un, dc), _mm_tn(un, dv_c)], axis=1)
    grads["w_qkr"] = _mm_tn(un, dz_qkr)
    grads["w_gg"] = _mm_tn(un, dgg)
    dh1, grads["mix_norm"] = _mix_in_bwd(db, dc, dv_c, dz_qkr, dgg, w["w_bcv"], w["w_qkr"], w["w_gg"], s["h1"],
                                         s["rm"], w["mix_norm"], dh2)
    dh0, g1 = _ffn_bwd(dh1, s, w, 1)
    grads.update(g1)
    return dh0, grads


def _fwd_bwd(x, p, positions, target, layers, final_norm):
    cs = _rope_tables(positions)
    h = x
    saved = []
    for l, w in enumerate(layers):
        h, s = _layer_fwd(h, p[l], cs, w)
        saved.append(s)
    dh, loss, d_final = _final_loss(h, final_norm, target)
    grads = [None] * len(layers)
    for l in reversed(range(len(layers))):
        dh, grads[l] = _layer_bwd(dh, saved[l], cs, layers[l])
    return loss[0, 0], dh, grads, d_final


def _to_compute(nat):
    w_in = nat["w_in"]
    D = w_in.shape[0]
    C = nat["w_conv_out"].shape[0]
    QL, KVL = nat["w_uq"].shape[0], nat["w_ukv"].shape[0]
    H = nat["w_uq"].shape[1] // (NOPE + ROPE)
    o1, o2 = 3 * C, 3 * C + QL + KVL + ROPE
    wuq = nat["w_uq"].reshape(QL, H, NOPE + ROPE)
    wuq = jnp.pad(wuq, ((0, 0), (0, 0), (0, HEAD_SLOT - NOPE - ROPE))).reshape(QL, H * HEAD_SLOT)
    wukv = nat["w_ukv"].reshape(KVL, H, NOPE + VDIM)
    wk = jnp.pad(wukv[:, :, :NOPE], ((0, 0), (0, 0), (0, HEAD_SLOT - NOPE))).reshape(KVL, H * HEAD_SLOT)
    wv = wukv[:, :, NOPE:].reshape(KVL, H * VDIM)
    out = {k: nat[k] for k in ("ffn1_w_gu", "ffn1_w_down", "w_conv_out", "w_mla_out", "w_o", "ffn2_w_gu",
                               "ffn2_w_down", "w_ple_gate", "w_ple_proj")}
    out.update(
        w_bcv=w_in[:, :o1],
        w_qkr=jnp.pad(w_in[:, o1:o2], ((0, 0), (0, LANES - ROPE))),
        w_gg=w_in[:, o2:],
        wuq=wuq, wk=wk, wv=wv,
    )
    return out


def _grads_to_natural(g):
    D = g["w_bcv"].shape[0]
    QL, KVL = g["wuq"].shape[0], g["wk"].shape[0]
    H = g["wv"].shape[1] // VDIM
    w_in = jnp.concatenate([g["w_bcv"], g["w_qkr"][:, :QL + KVL + ROPE], g["w_gg"]], axis=1)
    w_uq = g["wuq"].reshape(QL, H, HEAD_SLOT)[:, :, :NOPE + ROPE].reshape(QL, H * (NOPE + ROPE))
    w_ukv = jnp.concatenate([g["wk"].reshape(KVL, H, HEAD_SLOT)[:, :, :NOPE], g["wv"].reshape(KVL, H, VDIM)],
                            axis=2).reshape(KVL, H * (NOPE + VDIM))
    out = {k: g[k] for k in ("ffn1_w_gu", "ffn1_w_down", "w_conv_out", "w_mla_out", "w_o", "ffn2_w_gu",
                             "ffn2_w_down", "w_ple_gate", "w_ple_proj")}
    out.update(w_in=w_in, w_uq=w_uq, w_ukv=w_ukv)
    return out


ANY = pl.BlockSpec(memory_space=pl.ANY)


def _mesh_pos():
    return lax.axis_index("x"), lax.axis_index("y"), lax.axis_index("c")


def _other_chips(x, y):
    return [(1 - x, y), (x, 1 - y), (1 - x, 1 - y)]


def _all_gather(flat):
    R, W = flat.shape

    def body(x_ref, out_ref, send_sems, recv_sems, local_sem):
        x, y, c = _mesh_pos()
        me, sibling = (x, y, c), (x, y, 1 - c)
        chips = _other_chips(x, y)

        def slot(px, py, pc):
            return out_ref.at[4 * px + 2 * py + pc]

        def copy(k, block, to, src=None):
            return pltpu.make_async_remote_copy(
                src_ref=slot(*block) if src is None else src, dst_ref=slot(*block),
                send_sem=send_sems.at[k], recv_sem=recv_sems.at[k], device_id=to, device_id_type=MESH)

        mine = pltpu.make_async_copy(x_ref, slot(*me), local_sem)
        mine.start()
        first = [copy(0, me, sibling, src=x_ref)]
        first += [copy(1 + j, me, (*chip, c), src=x_ref) for j, chip in enumerate(chips)]
        for cp in first:
            cp.start()
        passed = [copy(4 + j, (*chip, c), sibling) for j, chip in enumerate(chips)]
        for j, chip in enumerate(chips):
            copy(1 + j, (*chip, c), me).wait_recv()
            passed[j].start()
        copy(0, sibling, me).wait_recv()
        for j, chip in enumerate(chips):
            copy(4 + j, (*chip, 1 - c), me).wait_recv()
        for cp in first + passed:
            cp.wait_send()
        mine.wait()

    return pl.pallas_call(
        body, name="all_gather",
        out_shape=_sds((N_DEV, R, W), flat.dtype),
        in_specs=[ANY], out_specs=ANY,
        scratch_shapes=[pltpu.SemaphoreType.DMA((7,)), pltpu.SemaphoreType.DMA((7,)), pltpu.SemaphoreType.DMA],
    )(flat)


def _rs_d2d(g):
    _, R, W = g.shape

    def body(g_ref, out_ref, send_sems, recv_sems):
        x, y, c = _mesh_pos()
        sibling = (x, y, 1 - c)
        copies = []
        for j in range(4):
            copies.append(pltpu.make_async_remote_copy(
                src_ref=g_ref.at[2 * j + (1 - c)], dst_ref=out_ref.at[j],
                send_sem=send_sems.at[j], recv_sem=recv_sems.at[j], device_id=sibling, device_id_type=MESH))
        for cp in copies:
            cp.start()
        for cp in copies:
            cp.wait()

    return pl.pallas_call(
        body, name="rs_d2d",
        out_shape=_sds((4, R, W), g.dtype),
        in_specs=[ANY], out_specs=ANY,
        scratch_shapes=[pltpu.SemaphoreType.DMA((4,)), pltpu.SemaphoreType.DMA((4,))],
    )(g)


def _rs_add_chip(g, a):
    _, R, W = g.shape
    tr = _tile(R, 512, 16)

    def chip(k):
        x, y, _ = _mesh_pos()
        return ([(x, y)] + _other_chips(x, y))[k]

    def body(g0, g1, g2, g3, a0, a1, a2, a3, own_ref, out_ref):
        own_ref[...] = g0[0].astype(F32) + a0[0].astype(F32)
        for k, (gk, ak) in enumerate(((g1, a1), (g2, a2), (g3, a3))):
            out_ref[k] = (gk[0].astype(F32) + ak[0].astype(F32)).astype(BF16)

    def gspec(k):
        def index(i):
            px, py = chip(k)
            return 4 * px + 2 * py + lax.axis_index("c"), i, 0
        return pl.BlockSpec((1, tr, W), index)

    def aspec(k):
        def index(i):
            px, py = chip(k)
            return 2 * px + py, i, 0
        return pl.BlockSpec((1, tr, W), index)

    return _call(
        body, "rs_add_chip", (R // tr,),
        [gspec(k) for k in range(4)] + [aspec(k) for k in range(4)],
        [pl.BlockSpec((tr, W), lambda i: (i, 0)), pl.BlockSpec((3, tr, W), lambda i: (0, i, 0))],
        [_sds((R, W), F32), _sds((3, R, W), BF16)],
    )(g, g, g, g, a, a, a, a)


def _rs_ici(t3):
    _, R, W = t3.shape

    def body(t_ref, out_ref, send_sems, recv_sems):
        x, y, c = _mesh_pos()
        copies = []
        for k, chip in enumerate(_other_chips(x, y)):
            copies.append(pltpu.make_async_remote_copy(
                src_ref=t_ref.at[k], dst_ref=out_ref.at[k],
                send_sem=send_sems.at[k], recv_sem=recv_sems.at[k], device_id=(*chip, c), device_id_type=MESH))
        for cp in copies:
            cp.start()
        for cp in copies:
            cp.wait()

    return pl.pallas_call(
        body, name="rs_ici",
        out_shape=_sds((3, R, W), t3.dtype),
        in_specs=[ANY], out_specs=ANY,
        scratch_shapes=[pltpu.SemaphoreType.DMA((3,)), pltpu.SemaphoreType.DMA((3,))],
    )(t3)


def _rs_add_final(own, b):
    R, W = own.shape
    tr = _tile(R, 512, 16)

    def body(own_ref, b_ref, o_ref):
        o_ref[...] = ((own_ref[...] + b_ref[0].astype(F32)) + b_ref[1].astype(F32)) + b_ref[2].astype(F32)

    return _call(
        body, "rs_add_final", (R // tr,),
        [pl.BlockSpec((tr, W), lambda i: (i, 0)), pl.BlockSpec((3, tr, W), lambda i: (0, i, 0))],
        pl.BlockSpec((tr, W), lambda i: (i, 0)),
        _sds((R, W), F32),
    )(own, b)


def _reduce_scatter(g):
    own, t3 = _rs_add_chip(g, _rs_d2d(g))
    return _rs_add_final(own, _rs_ici(t3))


def _all_reduce_small(v):
    n, W = v.shape

    def body(v_ref, out_ref, slots, send_sems, recv_sems):
        x, y, c = _mesh_pos()
        me = 4 * x + 2 * y + c
        slots[me] = v_ref[...]
        copies = []
        for k in range(1, N_DEV):
            kx, ky, kc = (k >> 2) & 1, (k >> 1) & 1, k & 1
            peer = (1 - x if kx else x, 1 - y if ky else y, 1 - c if kc else c)
            copies.append(pltpu.make_async_remote_copy(
                src_ref=v_ref, dst_ref=slots.at[me], send_sem=send_sems.at[k - 1], recv_sem=recv_sems.at[k - 1],
                device_id=peer, device_id_type=MESH))
        for cp in copies:
            cp.start()
        for cp in copies:
            cp.wait()
        acc = slots[0]
        for d in range(1, N_DEV):
            acc = acc + slots[d]
        out_ref[...] = acc

    vm = pl.BlockSpec(memory_space=pltpu.VMEM)
    return pl.pallas_call(
        body, name="all_reduce_small",
        out_shape=_sds((n, W), F32),
        in_specs=[vm], out_specs=vm,
        scratch_shapes=[pltpu.VMEM((N_DEV, n, W), F32), pltpu.SemaphoreType.DMA((7,)), pltpu.SemaphoreType.DMA((7,))],
    )(v)


def _adamw(w, g, m, v):
    L, r, c = w.shape
    tr = _tile(r, max(SUBLANES, (256 * 1024 // c) // SUBLANES * SUBLANES), SUBLANES)

    def body(w_ref, g_ref, m_ref, v_ref, d_ref, nm_ref, nv_ref):
        gv = g_ref[...]
        m2 = ADAM_B1 * m_ref[...] + (1.0 - ADAM_B1) * gv
        v2 = ADAM_B2 * v_ref[...] + (1.0 - ADAM_B2) * (gv * gv)
        m_hat = m2 / (1.0 - ADAM_B1 ** ADAM_STEP)
        v_hat = v2 / (1.0 - ADAM_B2 ** ADAM_STEP)
        d_ref[...] = -ADAM_LR * (m_hat / (jnp.sqrt(v_hat) + ADAM_EPS) + ADAM_WD * w_ref[...])
        nm_ref[...] = m2
        nv_ref[...] = v2

    spec = pl.BlockSpec((1, tr, c), lambda l, i: (l, i, 0))
    return _call(body, "adamw", (L, r // tr), [spec] * 4, [spec] * 3, [_sds((L, r, c), F32)] * 3)(w, g, m, v)


_BIG = ("ffn1_w_gu", "ffn1_w_down", "w_in", "w_conv_out", "w_uq", "w_ukv", "w_mla_out", "w_o", "ffn2_w_gu",
        "ffn2_w_down", "w_ple_gate", "w_ple_proj")
_COL_SHARDED = ("ffn1_w_gu", "w_in", "w_conv_out", "w_uq", "w_ukv", "ffn2_w_gu", "w_ple_proj")
_SMALL = ("ffn1_norm", "mix_norm", "q_norm", "kv_norm", "ffn2_norm", "ple_norm")
_ORDER = ("ffn1_norm", "ffn1_w_gu", "ffn1_w_down", "mix_norm", "w_in", "conv_w", "w_conv_out", "q_norm", "kv_norm",
          "w_uq", "w_ukv", "w_mla_out", "w_o", "ffn2_norm", "ffn2_w_gu", "ffn2_w_down", "ple_norm", "w_ple_gate",
          "w_ple_proj", "final_norm")


def _pack_rows(vecs, width):
    flat = jnp.concatenate([a.reshape(-1) for a in vecs])
    n = flat.shape[0]
    rows = -(-n // width)
    rows = -(-rows // SUBLANES) * SUBLANES
    flat = jnp.pad(flat, (0, rows * width - n))
    offs, o = [], 0
    for a in vecs:
        offs.append(o)
        o += a.size
    return flat.reshape(rows, width), offs


def _unpack_rows(packed, vecs, offs):
    flat = packed.reshape(-1)
    return [flat[o:o + a.size].reshape(a.shape) for a, o in zip(vecs, offs)]


def kernel(x, p, positions, ffn1_norm, ffn1_w_gu, ffn1_w_down, mix_norm, w_in, conv_w, w_conv_out, q_norm, kv_norm, w_uq, w_ukv, w_mla_out, w_o, ffn2_norm, ffn2_w_gu, ffn2_w_down, ple_norm, w_ple_gate, w_ple_proj, final_norm, loss_target, m_ffn1_norm, m_ffn1_w_gu, m_ffn1_w_down, m_mix_norm, m_w_in, m_conv_w, m_w_conv_out, m_q_norm, m_kv_norm, m_w_uq, m_w_ukv, m_w_mla_out, m_w_o, m_ffn2_norm, m_ffn2_w_gu, m_ffn2_w_down, m_ple_norm, m_w_ple_gate, m_w_ple_proj, m_final_norm, v_ffn1_norm, v_ffn1_w_gu, v_ffn1_w_down, v_mix_norm, v_w_in, v_conv_w, v_w_conv_out, v_q_norm, v_kv_norm, v_w_uq, v_w_ukv, v_w_mla_out, v_w_o, v_ffn2_norm, v_ffn2_w_gu, v_ffn2_w_down, v_ple_norm, v_w_ple_gate, v_w_ple_proj, v_final_norm):
    args = dict(locals())
    wts = {n: args[n] for n in _ORDER}
    L = w_in.shape[0]
    dev = 4 * lax.axis_index("x") + 2 * lax.axis_index("y") + lax.axis_index("c")

    flat = jnp.concatenate([wts[n][l].astype(BF16).reshape(-1, FLAT_COLS) for l in range(L) for n in _BIG], axis=0)
    gathered = _all_gather(flat)
    cw = conv_w.shape[2]
    conv_full = lax.dynamic_update_slice(jnp.zeros((L, 3, N_DEV * cw), F32), conv_w, (0, 0, dev * cw))
    conv_packed, conv_offs = _pack_rows([conv_full], FLAT_COLS)
    conv_full = _unpack_rows(_all_reduce_small(conv_packed), [conv_full], conv_offs)[0]

    layers, row = [], 0
    for l in range(L):
        nat = {}
        for n in _BIG:
            r, c = wts[n].shape[1:]
            rows = r * c // FLAT_COLS
            blk = gathered[:, row:row + rows].reshape(N_DEV, r, c)
            row += rows
            nat[n] = (blk.transpose(1, 0, 2).reshape(r, N_DEV * c) if n in _COL_SHARDED
                      else blk.reshape(N_DEV * r, c))
        w = _to_compute(nat)
        w["conv_w"] = conv_full[l]
        for n in _SMALL:
            w[n] = wts[n][l][None, :]
        layers.append(w)

    loss_dev, grad_x, grads, d_final = _fwd_bwd(x[0], p[:, 0], positions[0], loss_target[0], layers,
                                                final_norm[None, :])
    loss = lax.psum(loss_dev, ("x", "y", "c"))

    parts = []
    for l in range(L):
        nat = _grads_to_natural(grads[l])
        for n in _BIG:
            r, c = wts[n].shape[1:]
            g = nat[n]
            blk = (g.reshape(r, N_DEV, c).transpose(1, 0, 2) if n in _COL_SHARDED else g.reshape(N_DEV, r, c))
            parts.append(blk.astype(BF16).reshape(N_DEV, r * c // FLAT_COLS, FLAT_COLS))
    reduced = _reduce_scatter(jnp.concatenate(parts, axis=1))
    grad = {n: [] for n in _BIG}
    row = 0
    for l in range(L):
        for n in _BIG:
            r, c = wts[n].shape[1:]
            rows = r * c // FLAT_COLS
            grad[n].append(reduced[row:row + rows].reshape(r, c))
            row += rows
    grad = {n: jnp.stack(v) for n, v in grad.items()}

    small = [jnp.stack([grads[l][n][0] for l in range(L)]) for n in _SMALL]
    small += [jnp.stack([grads[l]["conv_w"] for l in range(L)]), d_final[0]]
    packed, offs = _pack_rows(small, FLAT_COLS)
    small = _unpack_rows(_all_reduce_small(packed), small, offs)
    for n, g in zip(_SMALL, small):
        grad[n] = g
    grad["conv_w"] = lax.dynamic_slice(small[len(_SMALL)], (0, 0, dev * cw), (L, 3, cw))
    grad["final_norm"] = small[-1]

    deltas, new_m, new_v = {}, {}, {}
    for n in _ORDER:
        w3 = wts[n].reshape((1,) * (3 - wts[n].ndim) + wts[n].shape)
        d, nm, nv = _adamw(w3, grad[n].reshape(w3.shape), args["m_" + n].reshape(w3.shape),
                           args["v_" + n].reshape(w3.shape))
        deltas[n], new_m[n], new_v[n] = (a.reshape(wts[n].shape) for a in (d, nm, nv))
    return (loss, grad_x[None], *[grad[n] for n in _ORDER], *[deltas[n] for n in _ORDER],
            *[new_m[n] for n in _ORDER], *[new_v[n] for n in _ORDER])
```

```python
import math

import jax
import jax.numpy as jnp
from jax import lax
from jax.experimental import pallas as pl
from jax.experimental.pallas import tpu as pltpu

F32 = jnp.float32
BF16 = jnp.bfloat16

CHUNK = 64
NOPE = 128
ROPE = 64
VDIM = 128
ROPE_THETA = 10000.0
EPS = 1e-6
ATTN_SCALE = (NOPE + ROPE) ** -0.5
ADAM_LR = 0.001
ADAM_B1 = 0.9
ADAM_B2 = 0.999
ADAM_EPS = 1e-08
ADAM_WD = 0.01
ADAM_STEP = 10

LANES = 128
SUBLANES = 8
BF16_ROWS = 16
V7X_VMEM_BYTES = 64 * 1024 * 1024
VMEM_LIMIT = V7X_VMEM_BYTES * 7 // 8
HEAD_SLOT = 2 * LANES
N_DEV = 8
FLAT_COLS = 1024
CLASSES = ("gu", "dn", "sq", "win", "c128", "c256")

NT = (((1,), (1,)), ((), ()))
MESH = pl.DeviceIdType.MESH
ANY = pl.BlockSpec(memory_space=pl.ANY)


def _dot(a, b):
    return jnp.dot(a, b, preferred_element_type=F32)


def _dot_nt(a, b):
    return lax.dot_general(a, b, NT, preferred_element_type=F32)


def _sig(x):
    return 1.0 / (1.0 + jnp.exp(-x))


def _tile(n, pref, unit):
    if n <= pref:
        return n
    t = (pref // unit) * unit
    while t >= unit:
        if n % t == 0:
            return t
        t -= unit
    return n


def _call(body, name, grid, in_specs, out_specs, out_shape, scratch=(), aliases=None):
    return pl.pallas_call(
        body,
        name=name,
        grid=grid,
        in_specs=in_specs,
        out_specs=out_specs,
        out_shape=out_shape,
        scratch_shapes=list(scratch),
        input_output_aliases=aliases or {},
        compiler_params=pltpu.CompilerParams(
            dimension_semantics=("arbitrary",) * len(grid), vmem_limit_bytes=VMEM_LIMIT
        ),
    )


def _sds(shape, dtype):
    return jax.ShapeDtypeStruct(shape, dtype)


def _rms_fwd(x, gain):
    rstd = lax.rsqrt(jnp.mean(x * x, axis=-1, keepdims=True) + EPS)
    return x * rstd * gain, rstd


def _rms_bwd(dn, x, rstd, gain):
    xhat = x * rstd
    dgy = dn * gain
    dx = rstd * (dgy - xhat * jnp.mean(dgy * xhat, axis=-1, keepdims=True))
    return dx, jnp.sum(dn * xhat, axis=0, keepdims=True)


def _rows(tm, w):
    return pl.BlockSpec((tm, w), lambda i: (i, 0))


def _whole(a):
    nd = a.ndim
    return pl.BlockSpec(a.shape, lambda i: (0,) * nd)


def _slab(buf, rows, index):
    return pl.BlockSpec((N_DEV, rows, buf.shape[2]), lambda i: (0, index, 0))


def _cat_slots(w):
    return jnp.concatenate([w[d] for d in range(N_DEV)], axis=1)


def _ffn_up(h, gain, gu_w, which):
    S, D = h.shape
    c = gu_w.shape[2]
    tm = _tile(S, 256, SUBLANES)
    nb = N_DEV // 2

    def body(h_ref, gain_ref, w_ref, gu_ref, a_ref, n_ref, r_ref):
        n32, rstd = _rms_fwd(h_ref[...], gain_ref[...])
        n = n32.astype(BF16)
        n_ref[...] = n
        r_ref[...] = rstd
        for d in range(nb):
            g = _dot(n, w_ref[d])
            u = _dot(n, w_ref[nb + d])
            gu_ref[d] = g.astype(BF16)
            gu_ref[nb + d] = u.astype(BF16)
            a_ref[d] = (g * _sig(g) * u).astype(BF16)

    return _call(
        body, "ffn_up", (S // tm,),
        [_rows(tm, D), _whole(gain), _slab(gu_w, D, which)],
        [pl.BlockSpec((N_DEV, tm, c), lambda i: (0, i, 0)), pl.BlockSpec((nb, tm, c), lambda i: (0, i, 0)),
         _rows(tm, D), _rows(tm, 1)],
        [_sds((N_DEV, S, c), BF16), _sds((nb, S, c), BF16), _sds((S, D), BF16), _sds((S, 1), F32)],
    )(h, gain, gu_w)


def _down_weight(w_ref, d, c):
    return w_ref[2 * d:2 * d + 2].reshape(c, w_ref.shape[2])


def _ffn_down(a, dn_w, which, h):
    nb, S, c = a.shape
    D = h.shape[1]
    tm = _tile(S, 512, SUBLANES)

    def body(a_ref, w_ref, h_ref, o_ref):
        acc = _dot(a_ref[0], _down_weight(w_ref, 0, c))
        for d in range(1, nb):
            acc = acc + _dot(a_ref[d], _down_weight(w_ref, d, c))
        o_ref[...] = h_ref[...] + 0.5 * acc

    return _call(
        body, "ffn_down", (S // tm,),
        [pl.BlockSpec((nb, tm, c), lambda i: (0, i, 0)), _slab(dn_w, c // 2, which), _rows(tm, D)],
        _rows(tm, D),
        _sds((S, D), F32),
    )(a, dn_w, h)


def _win_segments(C, QL, KVL, D):
    o1, o2 = 3 * C, 3 * C + QL + KVL + ROPE
    return [("bcv", k, k * C, (k + 1) * C) for k in range(3)] + [("qkr", None, o1, o2), ("gg", None, o2, o2 + 2 * D)]


def _win_pieces(segments, cw):
    out = []
    for tgt, lead, a, b in segments:
        for d in range(N_DEV):
            lo, hi = max(a, d * cw), min(b, (d + 1) * cw)
            if lo < hi:
                out.append((tgt, lead, d, (lo - d * cw, hi - d * cw), (lo - a, hi - a)))
    return out


def _win_split(win_w, C, QL, KVL):
    _, D, cw = win_w.shape
    WQ = QL + KVL + LANES
    pieces = _win_pieces(_win_segments(C, QL, KVL, D), cw)
    tr = _tile(D, 256, BF16_ROWS)

    def body(w_ref, bcv_ref, qkr_ref, gg_ref):
        tgt = dict(bcv=bcv_ref, qkr=qkr_ref, gg=gg_ref)
        qkr_ref[:, QL + KVL + ROPE:] = jnp.zeros((tr, LANES - ROPE), BF16)
        for name, lead, d, (s0, s1), (t0, t1) in pieces:
            v = w_ref[d, :, s0:s1]
            if lead is None:
                tgt[name][:, t0:t1] = v
            else:
                tgt[name][lead, :, t0:t1] = v

    return _call(
        body, "win_split", (D // tr,),
        [pl.BlockSpec((N_DEV, tr, cw), lambda i: (0, i, 0))],
        [pl.BlockSpec((3, tr, C), lambda i: (0, i, 0)), _rows(tr, WQ), _rows(tr, 2 * D)],
        [_sds((3, D, C), BF16), _sds((D, WQ), BF16), _sds((D, 2 * D), BF16)],
    )(win_w)


def _win_merge(d_bcv, d_qkr, d_gg, cw):
    _, D, C = d_bcv.shape
    WQ = d_qkr.shape[1]
    QL_KVL = WQ - LANES
    o1 = 3 * C
    segments = [("bcv", k, k * C, (k + 1) * C) for k in range(3)]
    segments += [("qkr", None, o1, o1 + QL_KVL + ROPE), ("gg", None, o1 + QL_KVL + ROPE, o1 + QL_KVL + ROPE + 2 * D)]
    pieces = _win_pieces(segments, cw)
    tr = _tile(D, 256, BF16_ROWS)

    def body(bcv_ref, qkr_ref, gg_ref, o_ref):
        src = dict(bcv=bcv_ref, qkr=qkr_ref, gg=gg_ref)
        for name, lead, d, (s0, s1), (t0, t1) in pieces:
            v = src[name][:, t0:t1] if lead is None else src[name][lead, :, t0:t1]
            o_ref[d, :, s0:s1] = v.astype(BF16)

    return _call(
        body, "win_merge", (D // tr,),
        [pl.BlockSpec((3, tr, C), lambda i: (0, i, 0)), _rows(tr, WQ), _rows(tr, 2 * D)],
        pl.BlockSpec((N_DEV, tr, cw), lambda i: (0, i, 0)),
        _sds((N_DEV, D, cw), BF16),
    )(d_bcv, d_qkr, d_gg)


def _mix_in(h, gain, w_bcv, w_qkr, w_gg):
    S, D = h.shape
    C = w_bcv.shape[2]
    tm = _tile(S, 256, SUBLANES)

    def body(h_ref, gain_ref, w1, w2, w3, o1, o2, o3, n_ref, r_ref):
        n32, rstd = _rms_fwd(h_ref[...], gain_ref[...])
        n = n32.astype(BF16)
        n_ref[...] = n
        r_ref[...] = rstd
        for k in range(3):
            o1[k] = _dot(n, w1[k])
        o2[...] = _dot(n, w2[...])
        o3[...] = _dot(n, w3[...])

    return _call(
        body, "mix_in", (S // tm,),
        [_rows(tm, D), _whole(gain), _whole(w_bcv), _whole(w_qkr), _whole(w_gg)],
        [pl.BlockSpec((3, tm, C), lambda i: (0, i, 0)), _rows(tm, w_qkr.shape[1]), _rows(tm, 2 * D),
         _rows(tm, D), _rows(tm, 1)],
        [_sds((3, S, C), F32), _sds((S, w_qkr.shape[1]), F32), _sds((S, 2 * D), F32), _sds((S, D), BF16),
         _sds((S, 1), F32)],
    )(h, gain, w_bcv, w_qkr, w_gg)


def _conv_taps(zc):
    rows = lax.broadcasted_iota(jnp.int32, zc.shape, 0)
    z1 = jnp.where(rows >= 1, pltpu.roll(zc, 1, 0), 0.0)
    z2 = jnp.where(rows >= 2, pltpu.roll(zc, 2, 0), 0.0)
    return z1, z2


def _conv_fwd(z_bcv, conv_w):
    _, S, C = z_bcv.shape

    def body(z_ref, w_ref, o_ref):
        w = w_ref[...]
        zc = z_ref[1] * z_ref[2]
        z1, z2 = _conv_taps(zc)
        y = w[0:1] * z2 + w[1:2] * z1 + w[2:3] * zc
        o_ref[...] = (z_ref[0] * y).astype(BF16)

    return _call(
        body, "conv_fwd", (C // LANES,),
        [pl.BlockSpec((3, S, LANES), lambda j: (0, 0, j)), pl.BlockSpec((3, LANES), lambda j: (0, j))],
        pl.BlockSpec((S, LANES), lambda j: (0, j)),
        _sds((S, C), BF16),
    )(z_bcv, conv_w)


def _rope(x, cs, half):
    c, s1, s2 = cs[:, :LANES], cs[:, LANES:2 * LANES], cs[:, 2 * LANES:]
    return x * c + pltpu.roll(x, LANES - half, 1) * s1 + pltpu.roll(x, half, 1) * s2


def _unrope(d, cs, half):
    c, s1, s2 = cs[:, :LANES], cs[:, LANES:2 * LANES], cs[:, 2 * LANES:]
    return d * c + pltpu.roll(d * s1, half, 1) + pltpu.roll(d * s2, LANES - half, 1)


def _mla_prep(z_qkr, gq, gkv, cs, c256_w):
    S = z_qkr.shape[0]
    QL, KVL = gq.shape[1], gkv.shape[1]
    H = N_DEV
    tm = _tile(S, 256, SUBLANES)
    half = ROPE // 2

    def body(z_ref, gq_ref, gkv_ref, cs_ref, w_ref, q_ref, k_ref, v_ref, qn_ref, kvn_ref, rq_ref, rkv_ref):
        z = z_ref[...]
        cs_t = cs_ref[...]
        qn32, rq = _rms_fwd(z[:, :QL], gq_ref[...])
        kvn32, rkv = _rms_fwd(z[:, QL:QL + KVL], gkv_ref[...])
        qn = qn32.astype(BF16)
        kvn = kvn32.astype(BF16)
        qn_ref[...] = qn
        kvn_ref[...] = kvn
        rq_ref[...] = rq
        rkv_ref[...] = rkv
        krope = _rope(z[:, QL + KVL:], cs_t, half).astype(BF16)
        for h in range(H):
            lo, mid, hi = h * HEAD_SLOT, h * HEAD_SLOT + LANES, (h + 1) * HEAD_SLOT
            q = _dot(qn, w_ref[h, KVL:KVL + QL, :])
            kv = _dot(kvn, w_ref[h, 0:KVL, :])
            q_ref[:, lo:mid] = q[:, :LANES].astype(BF16)
            q_ref[:, mid:hi] = _rope(q[:, LANES:], cs_t, half).astype(BF16)
            k_ref[:, lo:mid] = kv[:, :LANES].astype(BF16)
            k_ref[:, mid:hi] = krope
            v_ref[:, h * VDIM:(h + 1) * VDIM] = kv[:, LANES:].astype(BF16)

    return _call(
        body, "mla_prep", (S // tm,),
        [_rows(tm, z_qkr.shape[1]), _whole(gq), _whole(gkv), _rows(tm, 3 * LANES), _whole(c256_w)],
        [_rows(tm, H * HEAD_SLOT), _rows(tm, H * HEAD_SLOT), _rows(tm, H * VDIM), _rows(tm, QL), _rows(tm, KVL),
         _rows(tm, 1), _rows(tm, 1)],
        [_sds((S, H * HEAD_SLOT), BF16), _sds((S, H * HEAD_SLOT), BF16), _sds((S, H * VDIM), BF16),
         _sds((S, QL), BF16), _sds((S, KVL), BF16), _sds((S, 1), F32), _sds((S, 1), F32)],
    )(z_qkr, gq, gkv, cs, c256_w)


def _chunk_mask(t):
    shift = CHUNK.bit_length() - 1
    row = lax.broadcasted_iota(jnp.int32, (t, t), 0) >> shift
    col = lax.broadcasted_iota(jnp.int32, (t, t), 1) >> shift
    return col <= row


def _attn_fwd(q, k, v, H):
    S = q.shape[0]
    t = _tile(S, 512, CHUNK)
    nq = S // t

    def body(q_ref, k_ref, v_ref, o_ref, lse_ref):
        qi = pl.program_id(1)
        qv = q_ref[...]

        def block(kj, carry, masked):
            m, l, acc = carry
            off = pl.multiple_of(kj * t, t)
            s = _dot_nt(qv, k_ref[pl.ds(off, t), :]) * ATTN_SCALE
            if masked:
                s = jnp.where(_chunk_mask(t), s, -1e30)
            m_new = jnp.maximum(m, jnp.max(s, axis=-1, keepdims=True))
            alpha = jnp.exp(m - m_new)
            p = jnp.exp(s - m_new)
            l = alpha * l + jnp.sum(p, axis=-1, keepdims=True)
            acc = alpha * acc + _dot(p.astype(BF16), v_ref[pl.ds(off, t), :])
            return m_new, l, acc

        init = (jnp.full((t, 1), -1e30, F32), jnp.zeros((t, 1), F32), jnp.zeros((t, VDIM), F32))
        carry = lax.fori_loop(0, qi, lambda kj, c: block(kj, c, False), init)
        m, l, acc = block(qi, carry, True)
        o_ref[...] = (acc / l).astype(BF16)
        lse_ref[...] = jnp.broadcast_to(m + jnp.log(l), (t, LANES))

    return _call(
        body, "attn_fwd", (H, nq),
        [pl.BlockSpec((t, HEAD_SLOT), lambda h, i: (i, h)), pl.BlockSpec((S, HEAD_SLOT), lambda h, i: (0, h)),
         pl.BlockSpec((S, VDIM), lambda h, i: (0, h))],
        [pl.BlockSpec((t, VDIM), lambda h, i: (i, h)), pl.BlockSpec((t, LANES), lambda h, i: (i, h))],
        [_sds((S, H * VDIM), BF16), _sds((S, H * LANES), F32)],
    )(q, k, v)


def _merge_wo(o, by, z_gg, h, sq_w, c128_w):
    S, D = h.shape
    C = by.shape[1]
    r = sq_w.shape[1] // 3
    tm = _tile(S, 512, SUBLANES)

    def body(o_ref, by_ref, gg_ref, h_ref, wmo_ref, wo_ref, wco_ref, h2_ref, mg_ref, yc_ref, ym_ref):
        ymla = _dot(o_ref[...], wmo_ref[...].reshape(N_DEV * r, D))
        yconv = _dot(by_ref[...], _cat_slots(wco_ref))
        gg = gg_ref[...]
        merged = (_sig(gg[:, :D]) * yconv + _sig(gg[:, D:]) * ymla).astype(BF16)
        mg_ref[...] = merged
        yc_ref[...] = yconv.astype(BF16)
        ym_ref[...] = ymla.astype(BF16)
        h2_ref[...] = h_ref[...] + _dot(merged, wo_ref[...].reshape(N_DEV * r, D))

    return _call(
        body, "merge_wo", (S // tm,),
        [_rows(tm, o.shape[1]), _rows(tm, C), _rows(tm, 2 * D), _rows(tm, D), _slab(sq_w, r, 0), _slab(sq_w, r, 1),
         _slab(c128_w, C, 0)],
        [_rows(tm, D)] * 4,
        [_sds((S, D), F32)] + [_sds((S, D), BF16)] * 3,
    )(o, by, z_gg, h, sq_w, sq_w, c128_w)


def _ple_fwd(h, gain, p, sq_w, c128_w, C):
    S, D = h.shape
    P = p.shape[1]
    r = sq_w.shape[1] // 3
    tm = _tile(S, 512, SUBLANES)

    def body(h_ref, gain_ref, p_ref, wpg_ref, wpp_ref, o_ref, pre_ref, pp_ref, n_ref, r_ref):
        x = h_ref[...]
        n32, rstd = _rms_fwd(x, gain_ref[...])
        n = n32.astype(BF16)
        n_ref[...] = n
        r_ref[...] = rstd
        pre = _dot(n, wpg_ref[...].reshape(N_DEV * r, D))
        pp = _dot(p_ref[...].astype(BF16), _cat_slots(wpp_ref))
        pre_ref[...] = pre
        pp_ref[...] = pp
        o_ref[...] = x + _sig(pre) * pp

    return _call(
        body, "ple_fwd", (S // tm,),
        [_rows(tm, D), _whole(gain), _rows(tm, P), _slab(sq_w, r, 2), _slab(c128_w, P, C // P)],
        [_rows(tm, D), _rows(tm, D), _rows(tm, D), _rows(tm, D), _rows(tm, 1)],
        [_sds((S, D), F32)] * 3 + [_sds((S, D), BF16), _sds((S, 1), F32)],
    )(h, gain, p, sq_w, c128_w)


def _final_loss(h, gain, target):
    S, D = h.shape
    tm = _tile(S, 512, SUBLANES)

    def body(h_ref, gain_ref, t_ref, dh_ref, loss_ref, dg_ref):
        @pl.when(pl.program_id(0) == 0)
        def _():
            loss_ref[...] = jnp.zeros_like(loss_ref)
            dg_ref[...] = jnp.zeros_like(dg_ref)

        x = h_ref[...]
        gain_v = gain_ref[...]
        y, rstd = _rms_fwd(x, gain_v)
        err = y - t_ref[...]
        loss_ref[...] += 0.5 * jnp.sum(jnp.mean(err * err, axis=-1, keepdims=True))
        dx, dgain = _rms_bwd(err * (1.0 / D), x, rstd, gain_v)
        dh_ref[...] = dx
        dg_ref[...] += dgain

    return _call(
        body, "final_loss", (S // tm,),
        [_rows(tm, D), _whole(gain), _rows(tm, D)],
        [_rows(tm, D), pl.BlockSpec((1, LANES), lambda i: (0, 0)), pl.BlockSpec((1, D), lambda i: (0, 0))],
        [_sds((S, D), F32), _sds((1, LANES), F32), _sds((1, D), F32)],
    )(h, gain, target)


def _tn_call(body, name, grid, in_specs, out_spec, out_shape, scratch, operands, prev):
    n = len(operands)
    if prev is None:
        return _call(body, name, grid, in_specs, out_spec, out_shape, scratch)(*operands)
    assert prev.shape == out_shape.shape and prev.dtype == out_shape.dtype

    def wrapped(*refs):
        body(*refs[:n], *refs[n + 1:])

    return _call(wrapped, name, grid, in_specs + [ANY], out_spec, out_shape, scratch, {n: 0})(*operands, prev)


def _transposed(x_ref, xt_ref, first):
    @pl.when(first)
    def _():
        xt_ref[...] = x_ref[...].astype(BF16).T


def _tn_slots(x, dy, prev, rows_total, row_off):
    S, K = x.shape
    B, _, c = dy.shape
    tk = _tile(K, 512, LANES)

    def body(x_ref, dy_ref, o_ref, xt_ref):
        _transposed(x_ref, xt_ref, pl.program_id(1) == 0)
        o_ref[0] = _dot(xt_ref[...], dy_ref[0]).astype(BF16)

    return _tn_call(
        body, "tn_slots", (K // tk, B),
        [pl.BlockSpec((S, tk), lambda i, b: (0, i)), pl.BlockSpec((1, S, c), lambda i, b: (b, 0, 0))],
        pl.BlockSpec((1, tk, c), lambda i, b: (b, row_off // tk + i, 0)),
        _sds((B, rows_total, c), BF16), [pltpu.VMEM((tk, S), BF16)], [x, dy], prev)


def _tn_plain(x, dy, out_dtype=F32):
    S, K = x.shape
    B, _, c = dy.shape
    tk = _tile(K, 512, LANES)
    tn = _tile(c, 1024, LANES)

    def body(x_ref, dy_ref, o_ref, xt_ref):
        _transposed(x_ref, xt_ref, (pl.program_id(1) == 0) & (pl.program_id(2) == 0))
        o_ref[0] = _dot(xt_ref[...], dy_ref[0]).astype(out_dtype)

    return _call(
        body, "tn_plain", (K // tk, B, c // tn),
        [pl.BlockSpec((S, tk), lambda i, b, j: (0, i)), pl.BlockSpec((1, S, tn), lambda i, b, j: (b, 0, j))],
        pl.BlockSpec((1, tk, tn), lambda i, b, j: (b, i, j)),
        _sds((B, K, c), out_dtype), [pltpu.VMEM((tk, S), BF16)],
    )(x, dy)


def _tn_down(a, dh, prev, rows_total, which):
    nb, S, c = a.shape
    D = dh.shape[1]
    r = c // 2
    tn = _tile(D, 512, LANES)

    def body(a_ref, dh_ref, o_ref, xt_ref):
        _transposed(a_ref.at[0], xt_ref, pl.program_id(1) == 0)
        g = 0.5 * _dot(xt_ref[...], dh_ref[...].astype(BF16))
        o_ref[...] = g.astype(BF16).reshape(2, r, tn)

    return _tn_call(
        body, "tn_down", (nb, D // tn),
        [pl.BlockSpec((1, S, c), lambda i, j: (i, 0, 0)), pl.BlockSpec((S, tn), lambda i, j: (0, j))],
        pl.BlockSpec((2, r, tn), lambda i, j: (i, which, j)),
        _sds((N_DEV, rows_total, D), BF16), [pltpu.VMEM((c, S), BF16)], [a, dh], prev)


def _tn_square(x, dy, prev, rows_total, member):
    S, K = x.shape
    N = dy.shape[1]
    r = K // N_DEV
    tk = _tile(K, 512, r)
    tn = _tile(N, 512, LANES)

    def body(x_ref, dy_ref, o_ref, xt_ref):
        _transposed(x_ref, xt_ref, pl.program_id(1) == 0)
        g = _dot(xt_ref[...], dy_ref[...].astype(BF16))
        o_ref[...] = g.astype(BF16).reshape(tk // r, r, tn)

    return _tn_call(
        body, "tn_square", (K // tk, N // tn),
        [pl.BlockSpec((S, tk), lambda i, j: (0, i)), pl.BlockSpec((S, tn), lambda i, j: (0, j))],
        pl.BlockSpec((tk // r, r, tn), lambda i, j: (i, member, j)),
        _sds((N_DEV, rows_total, N), BF16), [pltpu.VMEM((tk, S), BF16)], [x, dy], prev)


def _tn_cols(x, dy, prev, rows_total, row_block):
    S, K = x.shape
    N = dy.shape[1]
    cw = N // N_DEV

    def body(x_ref, dy_ref, o_ref):
        g = _dot(x_ref[...].astype(BF16).T, dy_ref[...])
        for d in range(N_DEV):
            o_ref[d] = g[:, d * cw:(d + 1) * cw].astype(BF16)

    return _tn_call(
        body, "tn_cols", (1,),
        [pl.BlockSpec((S, K), lambda i: (0, 0)), pl.BlockSpec((S, N), lambda i: (0, 0))],
        pl.BlockSpec((N_DEV, K, cw), lambda i: (0, row_block, 0)),
        _sds((N_DEV, rows_total, cw), BF16), [], [x, dy], prev)


def _tn_heads(qn, kvn, dqp, dkv):
    S, QL = qn.shape
    KVL = kvn.shape[1]

    def body(qn_ref, kvn_ref, dq_ref, dkv_ref, o_ref):
        o_ref[0, 0:KVL, :] = _dot(kvn_ref[...].T, dkv_ref[...]).astype(BF16)
        o_ref[0, KVL:KVL + QL, :] = _dot(qn_ref[...].T, dq_ref[...]).astype(BF16)

    head = pl.BlockSpec((S, HEAD_SLOT), lambda h: (0, h))
    return _call(
        body, "tn_heads", (N_DEV,),
        [pl.BlockSpec((S, QL), lambda h: (0, 0)), pl.BlockSpec((S, KVL), lambda h: (0, 0)), head, head],
        pl.BlockSpec((1, KVL + QL, HEAD_SLOT), lambda h: (h, 0, 0)),
        _sds((N_DEV, KVL + QL, HEAD_SLOT), BF16),
    )(qn, kvn, dqp, dkv)


def _ple_bwd(dh, pre, pp, h, rstd, gain, sq_w):
    S, D = h.shape
    r = sq_w.shape[1] // 3
    tm = _tile(S, 512, SUBLANES)

    def body(dh_ref, pre_ref, pp_ref, h_ref, r_ref, gain_ref, wpg_ref, o_ref, dpre_ref, dpp_ref, dg_ref):
        @pl.when(pl.program_id(0) == 0)
        def _():
            dg_ref[...] = jnp.zeros_like(dg_ref)

        d = dh_ref[...]
        gate = _sig(pre_ref[...])
        dpre = (d * pp_ref[...] * gate * (1.0 - gate)).astype(BF16)
        dpre_ref[...] = dpre
        dpp_ref[...] = (d * gate).astype(BF16)
        dn = _dot_nt(dpre, wpg_ref[...].reshape(N_DEV * r, D))
        dx, dgain = _rms_bwd(dn, h_ref[...], r_ref[...], gain_ref[...])
        o_ref[...] = d + dx
        dg_ref[...] += dgain

    return _call(
        body, "ple_bwd", (S // tm,),
        [_rows(tm, D), _rows(tm, D), _rows(tm, D), _rows(tm, D), _rows(tm, 1), _whole(gain), _slab(sq_w, r, 2)],
        [_rows(tm, D), _rows(tm, D), _rows(tm, D), pl.BlockSpec((1, D), lambda i: (0, 0))],
        [_sds((S, D), F32), _sds((S, D), BF16), _sds((S, D), BF16), _sds((1, D), F32)],
    )(dh, pre, pp, h, rstd, gain, sq_w)


def _ffn_bwd_act(dh, dn_w, which, gu):
    S, D = dh.shape
    _, _, c = gu.shape
    nb = N_DEV // 2
    tm = _tile(S, 256, SUBLANES)

    def body(dh_ref, w_ref, gu_ref, dgu_ref):
        dhb = dh_ref[...].astype(BF16)
        for d in range(nb):
            da = 0.5 * _dot_nt(dhb, _down_weight(w_ref, d, c))
            g = gu_ref[d].astype(F32)
            u = gu_ref[nb + d].astype(F32)
            sg = _sig(g)
            dgu_ref[d] = (da * u * sg * (1.0 + g * (1.0 - sg))).astype(BF16)
            dgu_ref[nb + d] = (da * g * sg).astype(BF16)

    act = pl.BlockSpec((N_DEV, tm, c), lambda i: (0, i, 0))
    return _call(
        body, "ffn_bwd_act", (S // tm,),
        [_rows(tm, D), _slab(dn_w, c // 2, which), act],
        act,
        _sds((N_DEV, S, c), BF16),
    )(dh, dn_w, gu)


def _ffn_bwd_in(dgu, gu_w, which, h, rstd, gain, dh):
    S, D = h.shape
    c = dgu.shape[2]
    tm = _tile(S, 256, SUBLANES)

    def body(dgu_ref, w_ref, h_ref, r_ref, gain_ref, dh_ref, o_ref, dgain_ref):
        @pl.when(pl.program_id(0) == 0)
        def _():
            dgain_ref[...] = jnp.zeros_like(dgain_ref)

        dn = _dot_nt(dgu_ref[0], w_ref[0])
        for d in range(1, N_DEV):
            dn = dn + _dot_nt(dgu_ref[d], w_ref[d])
        dx, dgain = _rms_bwd(dn, h_ref[...], r_ref[...], gain_ref[...])
        o_ref[...] = dh_ref[...] + dx
        dgain_ref[...] += dgain

    return _call(
        body, "ffn_bwd_in", (S // tm,),
        [pl.BlockSpec((N_DEV, tm, c), lambda i: (0, i, 0)), _slab(gu_w, D, which), _rows(tm, D), _rows(tm, 1),
         _whole(gain), _rows(tm, D)],
        [_rows(tm, D), pl.BlockSpec((1, D), lambda i: (0, 0))],
        [_sds((S, D), F32), _sds((1, D), F32)],
    )(dgu, gu_w, h, rstd, gain, dh)


def _merge_bwd(dh, z_gg, yconv, ymla, sq_w, c128_w, C):
    S, D = dh.shape
    r = sq_w.shape[1] // 3
    HV = N_DEV * r
    tm = _tile(S, 512, SUBLANES)

    def body(dh_ref, gg_ref, yc_ref, ym_ref, wmo_ref, wo_ref, wco_ref, dgg_ref, dby_ref, do_ref, dyc_ref, dym_ref):
        dm = _dot_nt(dh_ref[...].astype(BF16), wo_ref[...].reshape(HV, D))
        gg = gg_ref[...]
        sgc = _sig(gg[:, :D])
        sgm = _sig(gg[:, D:])
        dyc = (dm * sgc).astype(BF16)
        dym = (dm * sgm).astype(BF16)
        dyc_ref[...] = dyc
        dym_ref[...] = dym
        dgg_ref[:, :D] = (dm * yc_ref[...].astype(F32) * sgc * (1.0 - sgc)).astype(BF16)
        dgg_ref[:, D:] = (dm * ym_ref[...].astype(F32) * sgm * (1.0 - sgm)).astype(BF16)
        dby_ref[...] = _dot_nt(dyc, _cat_slots(wco_ref))
        do_ref[...] = _dot_nt(dym, wmo_ref[...].reshape(HV, D)).astype(BF16)

    return _call(
        body, "merge_bwd", (S // tm,),
        [_rows(tm, D), _rows(tm, 2 * D), _rows(tm, D), _rows(tm, D), _slab(sq_w, r, 0), _slab(sq_w, r, 1),
         _slab(c128_w, C, 0)],
        [_rows(tm, 2 * D), _rows(tm, C), _rows(tm, HV), _rows(tm, D), _rows(tm, D)],
        [_sds((S, 2 * D), BF16), _sds((S, C), F32), _sds((S, HV), BF16), _sds((S, D), BF16), _sds((S, D), BF16)],
    )(dh, z_gg, yconv, ymla, sq_w, sq_w, c128_w)


def _conv_bwd(z_bcv, conv_w, dby):
    _, S, C = z_bcv.shape

    def body(z_ref, w_ref, dby_ref, dz_ref, dw_ref):
        w = w_ref[...]
        c = z_ref[1]
        v = z_ref[2]
        d = dby_ref[...]
        zc = c * v
        z1, z2 = _conv_taps(zc)
        y = w[0:1] * z2 + w[1:2] * z1 + w[2:3] * zc
        dz_ref[0] = (d * y).astype(BF16)
        dy = d * z_ref[0]
        rows = lax.broadcasted_iota(jnp.int32, dy.shape, 0)
        dy1 = jnp.where(rows < S - 1, pltpu.roll(dy, S - 1, 0), 0.0)
        dy2 = jnp.where(rows < S - 2, pltpu.roll(dy, S - 2, 0), 0.0)
        dzc = w[2:3] * dy + w[1:2] * dy1 + w[0:1] * dy2
        dz_ref[1] = (dzc * v).astype(BF16)
        dz_ref[2] = (dzc * c).astype(BF16)
        dw_ref[0:1, :] = jnp.sum(dy * z2, axis=0, keepdims=True)
        dw_ref[1:2, :] = jnp.sum(dy * z1, axis=0, keepdims=True)
        dw_ref[2:3, :] = jnp.sum(dy * zc, axis=0, keepdims=True)

    three = pl.BlockSpec((3, S, LANES), lambda j: (0, 0, j))
    wspec = pl.BlockSpec((3, LANES), lambda j: (0, j))
    return _call(
        body, "conv_bwd", (C // LANES,),
        [three, wspec, pl.BlockSpec((S, LANES), lambda j: (0, j))],
        [three, wspec],
        [_sds((3, S, C), BF16), _sds((3, C), F32)],
    )(z_bcv, conv_w, dby)


def _attn_bwd_q(q, k, v, do, o, lse, H):
    S = q.shape[0]
    t = _tile(S, 512, CHUNK)
    nq = S // t

    def body(q_ref, k_ref, v_ref, do_ref, o_ref, lse_ref, dq_ref, dl_ref):
        qi = pl.program_id(1)
        qv = q_ref[...]
        dov = do_ref[...]
        lse_v = lse_ref[...][:, :1]
        delta = jnp.sum(dov.astype(F32) * o_ref[...].astype(F32), axis=-1, keepdims=True)
        dl_ref[...] = jnp.broadcast_to(delta, (t, LANES))

        def block(kj, dq, masked):
            off = pl.multiple_of(kj * t, t)
            kv = k_ref[pl.ds(off, t), :]
            s = _dot_nt(qv, kv) * ATTN_SCALE
            if masked:
                s = jnp.where(_chunk_mask(t), s, -1e30)
            p = jnp.exp(s - lse_v)
            dp = _dot_nt(dov, v_ref[pl.ds(off, t), :])
            ds = (p * (dp - delta) * ATTN_SCALE).astype(BF16)
            return dq + _dot(ds, kv)

        dq = lax.fori_loop(0, qi, lambda kj, c: block(kj, c, False), jnp.zeros((t, HEAD_SLOT), F32))
        dq_ref[...] = block(qi, dq, True)

    qspec = lambda w: pl.BlockSpec((t, w), lambda h, i: (i, h))
    kspec = lambda w: pl.BlockSpec((S, w), lambda h, i: (0, h))
    return _call(
        body, "attn_bwd_q", (H, nq),
        [qspec(HEAD_SLOT), kspec(HEAD_SLOT), kspec(VDIM), qspec(VDIM), qspec(VDIM), qspec(LANES)],
        [qspec(HEAD_SLOT), qspec(LANES)],
        [_sds((S, H * HEAD_SLOT), F32), _sds((S, H * LANES), F32)],
    )(q, k, v, do, o, lse)


def _attn_bwd_kv(q, k, v, do, lse, delta, H):
    S = q.shape[0]
    t = _tile(S, 512, CHUNK)
    nk = S // t
    TN = (((0,), (0,)), ((), ()))

    def body(q_ref, k_ref, v_ref, do_ref, lse_ref, dl_ref, dk_ref, dv_ref):
        kj = pl.program_id(1)
        kv = k_ref[...]
        vv = v_ref[...]

        def block(qi, carry, masked):
            dk, dv = carry
            off = pl.multiple_of(qi * t, t)
            qv = q_ref[pl.ds(off, t), :]
            dov = do_ref[pl.ds(off, t), :]
            s = _dot_nt(qv, kv) * ATTN_SCALE
            if masked:
                s = jnp.where(_chunk_mask(t), s, -1e30)
            p = jnp.exp(s - lse_ref[pl.ds(off, t), :][:, :1])
            dp = _dot_nt(dov, vv)
            ds = (p * (dp - dl_ref[pl.ds(off, t), :][:, :1]) * ATTN_SCALE).astype(BF16)
            dk = dk + lax.dot_general(ds, qv, TN, preferred_element_type=F32)
            dv = dv + lax.dot_general(p.astype(BF16), dov, TN, preferred_element_type=F32)
            return dk, dv

        init = (jnp.zeros((t, HEAD_SLOT), F32), jnp.zeros((t, VDIM), F32))
        carry = block(kj, init, True)
        dk, dv = lax.fori_loop(kj + 1, nk, lambda qi, c: block(qi, c, False), carry)
        dk_ref[...] = dk
        dv_ref[...] = dv.astype(BF16)

    kspec = lambda w: pl.BlockSpec((t, w), lambda h, j: (j, h))
    qspec = lambda w: pl.BlockSpec((S, w), lambda h, j: (0, h))
    return _call(
        body, "attn_bwd_kv", (H, nk),
        [qspec(HEAD_SLOT), kspec(HEAD_SLOT), kspec(VDIM), qspec(VDIM), qspec(LANES), qspec(LANES)],
        [kspec(HEAD_SLOT), kspec(VDIM)],
        [_sds((S, H * HEAD_SLOT), F32), _sds((S, H * VDIM), BF16)],
    )(q, k, v, do, lse, delta)


def _mla_prep_bwd(dq, dk, dv, z_qkr, rq, rkv, gq, gkv, cs, c256_w):
    S = z_qkr.shape[0]
    QL, KVL = gq.shape[1], gkv.shape[1]
    H = N_DEV
    tm = _tile(S, 256, SUBLANES)
    half = ROPE // 2

    def body(dq_ref, dk_ref, dv_ref, z_ref, rq_ref, rkv_ref, gq_ref, gkv_ref, cs_ref, w_ref,
             dz_ref, dqp_ref, dkv_ref, dgq_ref, dgkv_ref):
        @pl.when(pl.program_id(0) == 0)
        def _():
            dgq_ref[...] = jnp.zeros_like(dgq_ref)
            dgkv_ref[...] = jnp.zeros_like(dgkv_ref)

        cs_t = cs_ref[...]
        dkr = jnp.zeros((tm, LANES), F32)
        dqn = jnp.zeros((tm, QL), F32)
        dkvn = jnp.zeros((tm, KVL), F32)
        for h in range(H):
            lo, mid, hi = h * HEAD_SLOT, h * HEAD_SLOT + LANES, (h + 1) * HEAD_SLOT
            dqp_ref[:, lo:mid] = dq_ref[:, lo:mid].astype(BF16)
            dqp_ref[:, mid:hi] = _unrope(dq_ref[:, mid:hi], cs_t, half).astype(BF16)
            dkv_ref[:, lo:mid] = dk_ref[:, lo:mid].astype(BF16)
            dkv_ref[:, mid:hi] = dv_ref[:, h * VDIM:(h + 1) * VDIM]
            dkr = dkr + dk_ref[:, mid:hi]
            dqn = dqn + _dot_nt(dqp_ref[:, lo:hi], w_ref[h, KVL:KVL + QL, :])
            dkvn = dkvn + _dot_nt(dkv_ref[:, lo:hi], w_ref[h, 0:KVL, :])
        z = z_ref[...]
        dqc, dgq = _rms_bwd(dqn, z[:, :QL], rq_ref[...], gq_ref[...])
        dkvc, dgkv = _rms_bwd(dkvn, z[:, QL:QL + KVL], rkv_ref[...], gkv_ref[...])
        dz_ref[:, :QL] = dqc.astype(BF16)
        dz_ref[:, QL:QL + KVL] = dkvc.astype(BF16)
        dz_ref[:, QL + KVL:] = _unrope(dkr, cs_t, half).astype(BF16)
        dgq_ref[...] += dgq
        dgkv_ref[...] += dgkv

    W = z_qkr.shape[1]
    return _call(
        body, "mla_prep_bwd", (S // tm,),
        [_rows(tm, H * HEAD_SLOT), _rows(tm, H * HEAD_SLOT), _rows(tm, H * VDIM), _rows(tm, W), _rows(tm, 1),
         _rows(tm, 1), _whole(gq), _whole(gkv), _rows(tm, 3 * LANES), _whole(c256_w)],
        [_rows(tm, W), _rows(tm, H * HEAD_SLOT), _rows(tm, H * HEAD_SLOT), _whole(gq), _whole(gkv)],
        [_sds((S, W), BF16), _sds((S, H * HEAD_SLOT), BF16), _sds((S, H * HEAD_SLOT), BF16),
         _sds((1, QL), F32), _sds((1, KVL), F32)],
    )(dq, dk, dv, z_qkr, rq, rkv, gq, gkv, cs, c256_w)


def _mix_in_bwd(d_bcv, dz_qkr, dgg, w_bcv, w_qkr, w_gg, h, rstd, gain, dh):
    S, D = h.shape
    C = d_bcv.shape[2]
    tm = _tile(S, 512, SUBLANES)

    def body(db_ref, dq_ref, dgg_ref, wb_ref, wq_ref, wg_ref, h_ref, r_ref, gain_ref, dh_ref, o_ref, dgain_ref):
        @pl.when(pl.program_id(0) == 0)
        def _():
            dgain_ref[...] = jnp.zeros_like(dgain_ref)

        dn = _dot_nt(dq_ref[...], wq_ref[...]) + _dot_nt(dgg_ref[...], wg_ref[...])
        for k in range(3):
            dn = dn + _dot_nt(db_ref[k], wb_ref[k])
        dx, dgain = _rms_bwd(dn, h_ref[...], r_ref[...], gain_ref[...])
        o_ref[...] = dh_ref[...] + dx
        dgain_ref[...] += dgain

    return _call(
        body, "mix_in_bwd", (S // tm,),
        [pl.BlockSpec((3, tm, C), lambda i: (0, i, 0)), _rows(tm, dz_qkr.shape[1]), _rows(tm, dgg.shape[1]),
         _whole(w_bcv), _whole(w_qkr), _whole(w_gg), _rows(tm, D), _rows(tm, 1), _whole(gain), _rows(tm, D)],
        [_rows(tm, D), pl.BlockSpec((1, D), lambda i: (0, 0))],
        [_sds((S, D), F32), _sds((1, D), F32)],
    )(d_bcv, dz_qkr, dgg, w_bcv, w_qkr, w_gg, h, rstd, gain, dh)


def _rope_tables(positions):
    half = ROPE // 2
    inv_freq = ROPE_THETA ** (-jnp.arange(0, ROPE, 2, dtype=F32) / ROPE)
    ang = positions.astype(F32)[:, None] * inv_freq
    cos, sin = jnp.cos(ang), jnp.sin(ang)
    z = jnp.zeros_like(cos)
    pad = jnp.zeros((positions.shape[0], LANES - 2 * half), F32)
    return jnp.concatenate([cos, cos, pad, -sin, z, pad, z, sin, pad], axis=1)


def _layer_fwd(h0, p_l, cs, w, sm):
    C = sm["conv_w"].shape[1]
    QL, KVL = sm["q_norm"].shape[1], sm["kv_norm"].shape[1]
    gu1, a1, n1, r1 = _ffn_up(h0, sm["ffn1_norm"], w["gu"], 0)
    h1 = _ffn_down(a1, w["dn"], 0, h0)
    w_bcv, w_qkr, w_gg = _win_split(w["win"], C, QL, KVL)
    z_bcv, z_qkr, z_gg, un, rm = _mix_in(h1, sm["mix_norm"], w_bcv, w_qkr, w_gg)
    by = _conv_fwd(z_bcv, sm["conv_w"])
    q, k, v, qn, kvn, rq, rkv = _mla_prep(z_qkr, sm["q_norm"], sm["kv_norm"], cs, w["c256"])
    o, lse = _attn_fwd(q, k, v, N_DEV)
    h2, merged, yconv, ymla = _merge_wo(o, by, z_gg, h1, w["sq"], w["c128"])
    gu2, a2, n2, r2 = _ffn_up(h2, sm["ffn2_norm"], w["gu"], 1)
    h3 = _ffn_down(a2, w["dn"], 1, h2)
    h4, pre, pp, pn, rp = _ple_fwd(h3, sm["ple_norm"], p_l, w["sq"], w["c128"], C)
    saved = dict(h0=h0, gu1=gu1, a1=a1, n1=n1, r1=r1, h1=h1, w_bcv=w_bcv, w_qkr=w_qkr, w_gg=w_gg, z_bcv=z_bcv,
                 z_qkr=z_qkr, z_gg=z_gg, un=un, rm=rm, by=by, q=q, k=k, v=v, qn=qn, kvn=kvn, rq=rq, rkv=rkv, o=o,
                 lse=lse, h2=h2, merged=merged, yconv=yconv, ymla=ymla, gu2=gu2, a2=a2, n2=n2, r2=r2, h3=h3,
                 pre=pre, pp=pp, pn=pn, rp=rp, p=p_l)
    return h4, saved


def _layer_bwd(dh4, s, cs, w, sm):
    D = dh4.shape[1]
    C = sm["conv_w"].shape[1]
    P = s["p"].shape[1]
    rows = {n: w[n].shape[1] for n in CLASSES}
    small = {}
    dh3, dpre, dpp, small["ple_norm"] = _ple_bwd(dh4, s["pre"], s["pp"], s["h3"], s["rp"], sm["ple_norm"], w["sq"])
    g_sq = _tn_square(s["pn"], dpre, None, rows["sq"], 2)
    g_c128 = _tn_cols(s["p"], dpp, None, rows["c128"], C // P)

    dgu2 = _ffn_bwd_act(dh3, w["dn"], 1, s["gu2"])
    g_dn = _tn_down(s["a2"], dh3, None, rows["dn"], 1)
    g_gu = _tn_slots(s["n2"], dgu2, None, rows["gu"], D)
    dh2, small["ffn2_norm"] = _ffn_bwd_in(dgu2, w["gu"], 1, s["h2"], s["r2"], sm["ffn2_norm"], dh3)

    dgg, dby, do, dyc, dym = _merge_bwd(dh2, s["z_gg"], s["yconv"], s["ymla"], w["sq"], w["c128"], C)
    g_sq = _tn_square(s["merged"], dh2, g_sq, rows["sq"], 1)
    g_sq = _tn_square(s["o"], dym, g_sq, rows["sq"], 0)
    g_c128 = _tn_cols(s["by"], dyc, g_c128, rows["c128"], 0)
    d_bcv, small["conv_w"] = _conv_bwd(s["z_bcv"], sm["conv_w"], dby)
    dq, delta = _attn_bwd_q(s["q"], s["k"], s["v"], do, s["o"], s["lse"], N_DEV)
    dk, dv = _attn_bwd_kv(s["q"], s["k"], s["v"], do, s["lse"], delta, N_DEV)
    dz_qkr, dqp, dkv, small["q_norm"], small["kv_norm"] = _mla_prep_bwd(
        dq, dk, dv, s["z_qkr"], s["rq"], s["rkv"], sm["q_norm"], sm["kv_norm"], cs, w["c256"])
    g_c256 = _tn_heads(s["qn"], s["kvn"], dqp, dkv)
    un = s["un"]
    g_win = _win_merge(_tn_plain(un, d_bcv), _tn_plain(un, dz_qkr[None])[0], _tn_plain(un, dgg[None])[0],
                       w["win"].shape[2])
    dh1, small["mix_norm"] = _mix_in_bwd(d_bcv, dz_qkr, dgg, s["w_bcv"], s["w_qkr"], s["w_gg"], s["h1"], s["rm"],
                                         sm["mix_norm"], dh2)

    dgu1 = _ffn_bwd_act(dh1, w["dn"], 0, s["gu1"])
    g_dn = _tn_down(s["a1"], dh1, g_dn, rows["dn"], 0)
    g_gu = _tn_slots(s["n1"], dgu1, g_gu, rows["gu"], 0)
    dh0, small["ffn1_norm"] = _ffn_bwd_in(dgu1, w["gu"], 0, s["h0"], s["r1"], sm["ffn1_norm"], dh1)
    return dh0, dict(gu=g_gu, dn=g_dn, sq=g_sq, win=g_win, c128=g_c128, c256=g_c256), small


def _mesh_pos():
    return lax.axis_index("x"), lax.axis_index("y"), lax.axis_index("c")


def _other_chips(x, y):
    return [(1 - x, y), (x, 1 - y), (1 - x, 1 - y)]


def _pack(arrs, width):
    L = arrs[0].shape[0]
    shapes = [a.shape[1:] for a in arrs]
    R = sum(r for r, _ in shapes)

    def body(*refs):
        o_ref = refs[-1]
        off = 0
        for a_ref, (r, c) in zip(refs[:-1], shapes):
            o_ref[0, off:off + r, 0:c] = a_ref[0].astype(BF16)
            if c < width:
                o_ref[0, off:off + r, c:width] = jnp.zeros((r, width - c), BF16)
            off += r

    return _call(
        body, "pack", (L,),
        [pl.BlockSpec((1, r, c), lambda l: (l, 0, 0)) for r, c in shapes],
        pl.BlockSpec((1, R, width), lambda l: (l, 0, 0)),
        _sds((L, R, width), BF16),
    )(*arrs)


def _all_gather(packs, l):
    n = len(packs)

    def body(*refs):
        ins, outs = refs[:n], refs[n:2 * n]
        send_sems, recv_sems, local_sems = refs[2 * n:]
        x, y, c = _mesh_pos()
        me, sibling = (x, y, c), (x, y, 1 - c)
        chips = _other_chips(x, y)

        def copy(q, k, block, to, src=None):
            slot = outs[q].at[4 * block[0] + 2 * block[1] + block[2]]
            return pltpu.make_async_remote_copy(
                src_ref=slot if src is None else src, dst_ref=slot,
                send_sem=send_sems.at[7 * q + k], recv_sem=recv_sems.at[7 * q + k], device_id=to, device_id_type=MESH)

        started = []
        for q in range(n):
            src = ins[q].at[l]
            mine = pltpu.make_async_copy(src, outs[q].at[4 * x + 2 * y + c], local_sems.at[q])
            mine.start()
            started.append(mine)
        sends = []
        for q in range(n):
            src = ins[q].at[l]
            sends.append(copy(q, 0, me, sibling, src=src))
            sends += [copy(q, 1 + j, me, (*chip, c), src=src) for j, chip in enumerate(chips)]
        for cp in sends:
            cp.start()
        for q in range(n):
            for j, chip in enumerate(chips):
                copy(q, 1 + j, (*chip, c), me).wait_recv()
                fwd = copy(q, 4 + j, (*chip, c), sibling)
                fwd.start()
                sends.append(fwd)
        for q in range(n):
            copy(q, 0, sibling, me).wait_recv()
            for j, chip in enumerate(chips):
                copy(q, 4 + j, (*chip, 1 - c), me).wait_recv()
        for cp in sends:
            cp.wait_send()
        for mine in started:
            mine.wait()

    return pl.pallas_call(
        body, name="all_gather",
        out_shape=[_sds((N_DEV,) + p.shape[1:], p.dtype) for p in packs],
        in_specs=[ANY] * n, out_specs=[ANY] * n,
        scratch_shapes=[pltpu.SemaphoreType.DMA((7 * n,)), pltpu.SemaphoreType.DMA((7 * n,)),
                        pltpu.SemaphoreType.DMA((n,))],
    )(*packs)


def _rs_d2d(gs):
    n = len(gs)

    def body(*refs):
        ins, outs = refs[:n], refs[n:2 * n]
        send_sems, recv_sems = refs[2 * n:]
        x, y, c = _mesh_pos()
        copies = []
        for q in range(n):
            for j in range(4):
                copies.append(pltpu.make_async_remote_copy(
                    src_ref=ins[q].at[2 * j + (1 - c)], dst_ref=outs[q].at[j], send_sem=send_sems.at[4 * q + j],
                    recv_sem=recv_sems.at[4 * q + j], device_id=(x, y, 1 - c), device_id_type=MESH))
        for cp in copies:
            cp.start()
        for cp in copies:
            cp.wait()

    return pl.pallas_call(
        body, name="rs_d2d",
        out_shape=[_sds((4,) + g.shape[1:], g.dtype) for g in gs],
        in_specs=[ANY] * n, out_specs=[ANY] * n,
        scratch_shapes=[pltpu.SemaphoreType.DMA((4 * n,)), pltpu.SemaphoreType.DMA((4 * n,))],
    )(*gs)


def _rs_add_chip(gs, as_, owns, l):
    n = len(gs)
    steps = 4
    tiles = [g.shape[1] // steps for g in gs]

    def chip(k):
        x, y, _ = _mesh_pos()
        return ([(x, y)] + _other_chips(x, y))[k]

    def body(*refs):
        g_refs, a_refs = refs[:4 * n], refs[4 * n:8 * n]
        own_refs, t_refs = refs[9 * n:10 * n], refs[10 * n:]
        for q in range(n):
            g, a = g_refs[4 * q:4 * q + 4], a_refs[4 * q:4 * q + 4]
            own_refs[q][0] = g[0][0].astype(F32) + a[0][0].astype(F32)
            for k in range(1, 4):
                t_refs[q][k - 1] = (g[k][0].astype(F32) + a[k][0].astype(F32)).astype(BF16)

    def gspec(q, k):
        def index(i):
            px, py = chip(k)
            return 4 * px + 2 * py + lax.axis_index("c"), i, 0
        return pl.BlockSpec((1, tiles[q], gs[q].shape[2]), index)

    def aspec(q, k):
        def index(i):
            px, py = chip(k)
            return 2 * px + py, i, 0
        return pl.BlockSpec((1, tiles[q], gs[q].shape[2]), index)

    in_specs = [gspec(q, k) for q in range(n) for k in range(4)] + [aspec(q, k) for q in range(n) for k in range(4)]
    operands = [g for g in gs for _ in range(4)] + [a for a in as_ for _ in range(4)]
    out_specs = [pl.BlockSpec((1, tiles[q], gs[q].shape[2]), lambda i: (l, i, 0)) for q in range(n)]
    out_specs += [pl.BlockSpec((3, tiles[q], gs[q].shape[2]), lambda i: (0, i, 0)) for q in range(n)]
    out_shape = [_sds(o.shape, F32) for o in owns] + [_sds((3,) + g.shape[1:], BF16) for g in gs]
    res = _call(body, "rs_add_chip", (steps,), in_specs + [ANY] * n, out_specs, out_shape,
                aliases={8 * n + q: q for q in range(n)})(*operands, *owns)
    return res[:n], res[n:]


def _rs_ici(ts, bs, l):
    n = len(ts)

    def body(*refs):
        ins, outs = refs[:n], refs[2 * n:3 * n]
        send_sems, recv_sems = refs[3 * n:]
        x, y, c = _mesh_pos()
        copies = []
        for q in range(n):
            for k, chip in enumerate(_other_chips(x, y)):
                copies.append(pltpu.make_async_remote_copy(
                    src_ref=ins[q].at[k], dst_ref=outs[q].at[l, k], send_sem=send_sems.at[3 * q + k],
                    recv_sem=recv_sems.at[3 * q + k], device_id=(*chip, c), device_id_type=MESH))
        for cp in copies:
            cp.start()
        for cp in copies:
            cp.wait()

    return pl.pallas_call(
        body, name="rs_ici",
        out_shape=[_sds(b.shape, b.dtype) for b in bs],
        in_specs=[ANY] * (2 * n), out_specs=[ANY] * n,
        input_output_aliases={n + q: q for q in range(n)},
        scratch_shapes=[pltpu.SemaphoreType.DMA((3 * n,)), pltpu.SemaphoreType.DMA((3 * n,))],
    )(*ts, *bs)


def _all_reduce_small(v):
    n, W = v.shape

    def body(v_ref, out_ref, slots, send_sems, recv_sems):
        x, y, c = _mesh_pos()
        me = 4 * x + 2 * y + c
        slots[me] = v_ref[...]
        copies = []
        for k in range(1, N_DEV):
            kx, ky, kc = (k >> 2) & 1, (k >> 1) & 1, k & 1
            peer = (1 - x if kx else x, 1 - y if ky else y, 1 - c if kc else c)
            copies.append(pltpu.make_async_remote_copy(
                src_ref=v_ref, dst_ref=slots.at[me], send_sem=send_sems.at[k - 1], recv_sem=recv_sems.at[k - 1],
                device_id=peer, device_id_type=MESH))
        for cp in copies:
            cp.start()
        for cp in copies:
            cp.wait()
        acc = slots[0]
        for d in range(1, N_DEV):
            acc = acc + slots[d]
        out_ref[...] = acc

    vm = pl.BlockSpec(memory_space=pltpu.VMEM)
    return pl.pallas_call(
        body, name="all_reduce_small",
        out_shape=_sds((n, W), F32),
        in_specs=[vm], out_specs=vm,
        scratch_shapes=[pltpu.VMEM((N_DEV, n, W), F32), pltpu.SemaphoreType.DMA((7,)), pltpu.SemaphoreType.DMA((7,))],
    )(v)


def _adamw_math(w, g, m, v):
    m2 = ADAM_B1 * m + (1.0 - ADAM_B1) * g
    v2 = ADAM_B2 * v + (1.0 - ADAM_B2) * (g * g)
    m_hat = m2 / (1.0 - ADAM_B1 ** ADAM_STEP)
    v_hat = v2 / (1.0 - ADAM_B2 ** ADAM_STEP)
    return -ADAM_LR * (m_hat / (jnp.sqrt(v_hat) + ADAM_EPS) + ADAM_WD * w), m2, v2


def _adamw(w, g, m, v):
    L, r, c = w.shape
    tr = _tile(r, max(SUBLANES, (256 * 1024 // c) // SUBLANES * SUBLANES), SUBLANES)

    def body(w_ref, g_ref, m_ref, v_ref, d_ref, nm_ref, nv_ref):
        d_ref[...], nm_ref[...], nv_ref[...] = _adamw_math(w_ref[...], g_ref[...], m_ref[...], v_ref[...])

    spec = pl.BlockSpec((1, tr, c), lambda l, i: (l, i, 0))
    return _call(body, "adamw", (L, r // tr), [spec] * 4, [spec] * 3, [_sds((L, r, c), F32)] * 3)(w, g, m, v)


def _adamw_reduced(w, m, v, own, b, row_off, tr):
    L, r, c = w.shape
    W = own.shape[2]
    ob = row_off // tr

    def body(w_ref, m_ref, v_ref, own_ref, b_ref, g_ref, d_ref, nm_ref, nv_ref):
        g = ((own_ref[0] + b_ref[0, 0].astype(F32)) + b_ref[0, 1].astype(F32)) + b_ref[0, 2].astype(F32)
        g = g[:, :c]
        g_ref[0] = g
        d_ref[0], nm_ref[0], nv_ref[0] = _adamw_math(w_ref[0], g, m_ref[0], v_ref[0])

    spec = pl.BlockSpec((1, tr, c), lambda l, i: (l, i, 0))
    return _call(
        body, "adamw_reduced", (L, r // tr),
        [spec] * 3 + [pl.BlockSpec((1, tr, W), lambda l, i: (l, ob + i, 0)),
                      pl.BlockSpec((1, 3, tr, W), lambda l, i: (l, 0, ob + i, 0))],
        [spec] * 4, [_sds((L, r, c), F32)] * 4,
    )(w, m, v, own, b)


_MEMBERS = dict(gu=("ffn1_w_gu", "ffn2_w_gu"), dn=("ffn1_w_down", "ffn2_w_down"),
                sq=("w_mla_out", "w_o", "w_ple_gate"), win=("w_in",), c128=("w_conv_out", "w_ple_proj"),
                c256=("w_ukv", "w_uq"))
_SMALL = ("ffn1_norm", "mix_norm", "q_norm", "kv_norm", "ffn2_norm", "ple_norm")
_ORDER = ("ffn1_norm", "ffn1_w_gu", "ffn1_w_down", "mix_norm", "w_in", "conv_w", "w_conv_out", "q_norm", "kv_norm",
          "w_uq", "w_ukv", "w_mla_out", "w_o", "ffn2_norm", "ffn2_w_gu", "ffn2_w_down", "ple_norm", "w_ple_gate",
          "w_ple_proj", "final_norm")


def _class_width(wts, cls):
    return HEAD_SLOT if cls == "c256" else wts[_MEMBERS[cls][0]].shape[2]


def _pack_rows(vecs, width):
    flat = jnp.concatenate([a.reshape(-1) for a in vecs])
    n = flat.shape[0]
    rows = -(-n // width)
    rows = -(-rows // SUBLANES) * SUBLANES
    flat = jnp.pad(flat, (0, rows * width - n))
    offs, o = [], 0
    for a in vecs:
        offs.append(o)
        o += a.size
    return flat.reshape(rows, width), offs


def _unpack_rows(packed, vecs, offs):
    flat = packed.reshape(-1)
    return [flat[o:o + a.size].reshape(a.shape) for a, o in zip(vecs, offs)]


def _train(x, p, positions, target, gathered, small_w, final_norm):
    cs = _rope_tables(positions)
    L = len(gathered)
    h = x
    saved = []
    for l in range(L):
        h, s = _layer_fwd(h, p[l], cs, gathered[l], small_w[l])
        saved.append(s)
    dh, loss, d_final = _final_loss(h, final_norm, target)
    grads, smalls = [None] * L, [None] * L
    for l in reversed(range(L)):
        dh, grads[l], smalls[l] = _layer_bwd(dh, saved[l], cs, gathered[l], small_w[l])
    return loss[0, 0], dh, grads, smalls, d_final


def kernel(x, p, positions, ffn1_norm, ffn1_w_gu, ffn1_w_down, mix_norm, w_in, conv_w, w_conv_out, q_norm, kv_norm, w_uq, w_ukv, w_mla_out, w_o, ffn2_norm, ffn2_w_gu, ffn2_w_down, ple_norm, w_ple_gate, w_ple_proj, final_norm, loss_target, m_ffn1_norm, m_ffn1_w_gu, m_ffn1_w_down, m_mix_norm, m_w_in, m_conv_w, m_w_conv_out, m_q_norm, m_kv_norm, m_w_uq, m_w_ukv, m_w_mla_out, m_w_o, m_ffn2_norm, m_ffn2_w_gu, m_ffn2_w_down, m_ple_norm, m_w_ple_gate, m_w_ple_proj, m_final_norm, v_ffn1_norm, v_ffn1_w_gu, v_ffn1_w_down, v_mix_norm, v_w_in, v_conv_w, v_w_conv_out, v_q_norm, v_kv_norm, v_w_uq, v_w_ukv, v_w_mla_out, v_w_o, v_ffn2_norm, v_ffn2_w_gu, v_ffn2_w_down, v_ple_norm, v_w_ple_gate, v_w_ple_proj, v_final_norm):
    args = dict(locals())
    wts = {n: args[n] for n in _ORDER}
    L = w_in.shape[0]
    dev = 4 * lax.axis_index("x") + 2 * lax.axis_index("y") + lax.axis_index("c")

    packs = [_pack([wts[n] for n in _MEMBERS[cls]], _class_width(wts, cls)) for cls in CLASSES]
    gathered = [dict(zip(CLASSES, _all_gather(packs, l))) for l in range(L)]
    cw = conv_w.shape[2]
    conv_full = lax.dynamic_update_slice(jnp.zeros((L, 3, N_DEV * cw), F32), conv_w, (0, 0, dev * cw))
    conv_packed, conv_offs = _pack_rows([conv_full], FLAT_COLS)
    conv_full = _unpack_rows(_all_reduce_small(conv_packed), [conv_full], conv_offs)[0]
    small_w = [dict({n: wts[n][l][None, :] for n in _SMALL}, conv_w=conv_full[l]) for l in range(L)]

    loss_dev, grad_x, grads, smalls, d_final = _train(x[0], p[:, 0], positions[0], loss_target[0], gathered, small_w,
                                                      final_norm[None, :])
    loss = lax.psum(loss_dev, ("x", "y", "c"))

    owns = [lax.empty((L,) + packs[q].shape[1:], F32) for q in range(len(CLASSES))]
    bs = [lax.empty((L, 3) + packs[q].shape[1:], BF16) for q in range(len(CLASSES))]
    for l in reversed(range(L)):
        gs = [grads[l][cls] for cls in CLASSES]
        owns, ts = _rs_add_chip(gs, _rs_d2d(gs), owns, l)
        bs = _rs_ici(ts, bs, l)

    small = [jnp.stack([smalls[l][n][0] for l in range(L)]) for n in _SMALL]
    small += [jnp.stack([smalls[l]["conv_w"] for l in range(L)]), d_final[0]]
    packed, offs = _pack_rows(small, FLAT_COLS)
    small = _unpack_rows(_all_reduce_small(packed), small, offs)
    grad = dict(zip(_SMALL, small))
    grad["conv_w"] = lax.dynamic_slice(small[len(_SMALL)], (0, 0, dev * cw), (L, 3, cw))
    grad["final_norm"] = small[-1]

    deltas, new_m, new_v = {}, {}, {}
    for q, cls in enumerate(CLASSES):
        off = 0
        rows = [wts[n].shape[1] for n in _MEMBERS[cls]]
        tr = _tile(math.gcd(*rows), 256, BF16_ROWS)
        for n, r in zip(_MEMBERS[cls], rows):
            grad[n], deltas[n], new_m[n], new_v[n] = _adamw_reduced(wts[n], args["m_" + n], args["v_" + n], owns[q],
                                                                    bs[q], off, tr)
            off += r
    for n in _SMALL + ("conv_w", "final_norm"):
        w3 = wts[n].reshape((1,) * (3 - wts[n].ndim) + wts[n].shape)
        d, nm, nv = _adamw(w3, grad[n].reshape(w3.shape), args["m_" + n].reshape(w3.shape),
                           args["v_" + n].reshape(w3.shape))
        deltas[n], new_m[n], new_v[n] = (a.reshape(wts[n].shape) for a in (d, nm, nv))
    return (loss, grad_x[None], *[grad[n] for n in _ORDER], *[deltas[n] for n in _ORDER],
            *[new_m[n] for n in _ORDER], *[new_v[n] for n in _ORDER])
```

```python
import math

import jax
import jax.numpy as jnp
from jax import lax
from jax.experimental import pallas as pl
from jax.experimental.pallas import tpu as pltpu
from jax.experimental.pallas import tpu_sc as plsc

F32 = jnp.float32
BF16 = jnp.bfloat16

CHUNK = 64
NOPE = 128
ROPE = 64
VDIM = 128
ROPE_THETA = 10000.0
EPS = 1e-6
ATTN_SCALE = (NOPE + ROPE) ** -0.5
ADAM_LR = 0.001
ADAM_B1 = 0.9
ADAM_B2 = 0.999
ADAM_EPS = 1e-08
ADAM_WD = 0.01
ADAM_STEP = 10

LANES = 128
SUBLANES = 8
BF16_ROWS = 16
V7X_VMEM_BYTES = 64 * 1024 * 1024
VMEM_LIMIT = V7X_VMEM_BYTES * 7 // 8
HEAD_SLOT = 2 * LANES
N_DEV = 8
FLAT_COLS = 1024
CLASSES = ("gu", "dn", "sq", "win", "c128", "c256")

NT = (((1,), (1,)), ((), ()))
MESH = pl.DeviceIdType.MESH
ANY = pl.BlockSpec(memory_space=pl.ANY)


def _dot(a, b):
    return jnp.dot(a, b, preferred_element_type=F32)


def _dot_nt(a, b):
    return lax.dot_general(a, b, NT, preferred_element_type=F32)


def _sig(x):
    return 1.0 / (1.0 + jnp.exp(-x))


def _tile(n, pref, unit):
    if n <= pref:
        return n
    t = (pref // unit) * unit
    while t >= unit:
        if n % t == 0:
            return t
        t -= unit
    return n


def _call(body, name, grid, in_specs, out_specs, out_shape, scratch=(), aliases=None):
    return pl.pallas_call(
        body,
        name=name,
        grid=grid,
        in_specs=in_specs,
        out_specs=out_specs,
        out_shape=out_shape,
        scratch_shapes=list(scratch),
        input_output_aliases=aliases or {},
        compiler_params=pltpu.CompilerParams(
            dimension_semantics=("arbitrary",) * len(grid), vmem_limit_bytes=VMEM_LIMIT
        ),
    )


def _sds(shape, dtype):
    return jax.ShapeDtypeStruct(shape, dtype)


def _rms_fwd(x, gain):
    rstd = lax.rsqrt(jnp.mean(x * x, axis=-1, keepdims=True) + EPS)
    return x * rstd * gain, rstd


def _rms_bwd(dn, x, rstd, gain):
    xhat = x * rstd
    dgy = dn * gain
    dx = rstd * (dgy - xhat * jnp.mean(dgy * xhat, axis=-1, keepdims=True))
    return dx, jnp.sum(dn * xhat, axis=0, keepdims=True)


def _rows(tm, w):
    return pl.BlockSpec((tm, w), lambda i: (i, 0))


def _whole(a):
    nd = a.ndim
    return pl.BlockSpec(a.shape, lambda i: (0,) * nd)


def _slab(buf, rows, index):
    return pl.BlockSpec((N_DEV, rows, buf.shape[2]), lambda i: (0, index, 0))


def _cat_slots(w):
    return jnp.concatenate([w[d] for d in range(N_DEV)], axis=1)


def _ffn_up(h, gain, gu_w, which):
    S, D = h.shape
    c = gu_w.shape[2]
    tm = _tile(S, 256, SUBLANES)
    nb = N_DEV // 2

    def body(h_ref, gain_ref, w_ref, gu_ref, a_ref, n_ref, r_ref):
        n32, rstd = _rms_fwd(h_ref[...], gain_ref[...])
        n = n32.astype(BF16)
        n_ref[...] = n
        r_ref[...] = rstd
        for d in range(nb):
            g = _dot(n, w_ref[d])
            u = _dot(n, w_ref[nb + d])
            gu_ref[d] = g.astype(BF16)
            gu_ref[nb + d] = u.astype(BF16)
            a_ref[d] = (g * _sig(g) * u).astype(BF16)

    return _call(
        body, "ffn_up", (S // tm,),
        [_rows(tm, D), _whole(gain), _slab(gu_w, D, which)],
        [pl.BlockSpec((N_DEV, tm, c), lambda i: (0, i, 0)), pl.BlockSpec((nb, tm, c), lambda i: (0, i, 0)),
         _rows(tm, D), _rows(tm, 1)],
        [_sds((N_DEV, S, c), BF16), _sds((nb, S, c), BF16), _sds((S, D), BF16), _sds((S, 1), F32)],
    )(h, gain, gu_w)


def _down_weight(w_ref, d, c):
    return w_ref[2 * d:2 * d + 2].reshape(c, w_ref.shape[2])


def _ffn_down(a, dn_w, which, h):
    nb, S, c = a.shape
    D = h.shape[1]
    tm = _tile(S, 512, SUBLANES)

    def body(a_ref, w_ref, h_ref, o_ref):
        acc = _dot(a_ref[0], _down_weight(w_ref, 0, c))
        for d in range(1, nb):
            acc = acc + _dot(a_ref[d], _down_weight(w_ref, d, c))
        o_ref[...] = h_ref[...] + 0.5 * acc

    return _call(
        body, "ffn_down", (S // tm,),
        [pl.BlockSpec((nb, tm, c), lambda i: (0, i, 0)), _slab(dn_w, c // 2, which), _rows(tm, D)],
        _rows(tm, D),
        _sds((S, D), F32),
    )(a, dn_w, h)


def _win_segments(C, QL, KVL, D):
    o1, o2 = 3 * C, 3 * C + QL + KVL + ROPE
    return [("bcv", k, k * C, (k + 1) * C) for k in range(3)] + [("qkr", None, o1, o2), ("gg", None, o2, o2 + 2 * D)]


def _win_pieces(segments, cw):
    out = []
    for tgt, lead, a, b in segments:
        for d in range(N_DEV):
            lo, hi = max(a, d * cw), min(b, (d + 1) * cw)
            if lo < hi:
                out.append((tgt, lead, d, (lo - d * cw, hi - d * cw), (lo - a, hi - a)))
    return out


def _win_split(win_w, C, QL, KVL):
    _, D, cw = win_w.shape
    WQ = QL + KVL + LANES
    pieces = _win_pieces(_win_segments(C, QL, KVL, D), cw)
    tr = _tile(D, 256, BF16_ROWS)

    def body(w_ref, bcv_ref, qkr_ref, gg_ref):
        tgt = dict(bcv=bcv_ref, qkr=qkr_ref, gg=gg_ref)
        qkr_ref[:, QL + KVL + ROPE:] = jnp.zeros((tr, LANES - ROPE), BF16)
        for name, lead, d, (s0, s1), (t0, t1) in pieces:
            v = w_ref[d, :, s0:s1]
            if lead is None:
                tgt[name][:, t0:t1] = v
            else:
                tgt[name][lead, :, t0:t1] = v

    return _call(
        body, "win_split", (D // tr,),
        [pl.BlockSpec((N_DEV, tr, cw), lambda i: (0, i, 0))],
        [pl.BlockSpec((3, tr, C), lambda i: (0, i, 0)), _rows(tr, WQ), _rows(tr, 2 * D)],
        [_sds((3, D, C), BF16), _sds((D, WQ), BF16), _sds((D, 2 * D), BF16)],
    )(win_w)


def _win_merge(d_bcv, d_qkr, d_gg, cw):
    _, D, C = d_bcv.shape
    WQ = d_qkr.shape[1]
    QL_KVL = WQ - LANES
    o1 = 3 * C
    segments = [("bcv", k, k * C, (k + 1) * C) for k in range(3)]
    segments += [("qkr", None, o1, o1 + QL_KVL + ROPE), ("gg", None, o1 + QL_KVL + ROPE, o1 + QL_KVL + ROPE + 2 * D)]
    pieces = _win_pieces(segments, cw)
    tr = _tile(D, 256, BF16_ROWS)

    def body(bcv_ref, qkr_ref, gg_ref, o_ref):
        src = dict(bcv=bcv_ref, qkr=qkr_ref, gg=gg_ref)
        for name, lead, d, (s0, s1), (t0, t1) in pieces:
            v = src[name][:, t0:t1] if lead is None else src[name][lead, :, t0:t1]
            o_ref[d, :, s0:s1] = v.astype(BF16)

    return _call(
        body, "win_merge", (D // tr,),
        [pl.BlockSpec((3, tr, C), lambda i: (0, i, 0)), _rows(tr, WQ), _rows(tr, 2 * D)],
        pl.BlockSpec((N_DEV, tr, cw), lambda i: (0, i, 0)),
        _sds((N_DEV, D, cw), BF16),
    )(d_bcv, d_qkr, d_gg)


def _mix_in(h, gain, w_bcv, w_qkr, w_gg):
    S, D = h.shape
    C = w_bcv.shape[2]
    tm = _tile(S, 256, SUBLANES)

    def body(h_ref, gain_ref, w1, w2, w3, o1, o2, o3, n_ref, r_ref):
        n32, rstd = _rms_fwd(h_ref[...], gain_ref[...])
        n = n32.astype(BF16)
        n_ref[...] = n
        r_ref[...] = rstd
        for k in range(3):
            o1[k] = _dot(n, w1[k])
        o2[...] = _dot(n, w2[...])
        o3[...] = _dot(n, w3[...])

    return _call(
        body, "mix_in", (S // tm,),
        [_rows(tm, D), _whole(gain), _whole(w_bcv), _whole(w_qkr), _whole(w_gg)],
        [pl.BlockSpec((3, tm, C), lambda i: (0, i, 0)), _rows(tm, w_qkr.shape[1]), _rows(tm, 2 * D),
         _rows(tm, D), _rows(tm, 1)],
        [_sds((3, S, C), F32), _sds((S, w_qkr.shape[1]), F32), _sds((S, 2 * D), F32), _sds((S, D), BF16),
         _sds((S, 1), F32)],
    )(h, gain, w_bcv, w_qkr, w_gg)


def _conv_taps(zc):
    rows = lax.broadcasted_iota(jnp.int32, zc.shape, 0)
    z1 = jnp.where(rows >= 1, pltpu.roll(zc, 1, 0), 0.0)
    z2 = jnp.where(rows >= 2, pltpu.roll(zc, 2, 0), 0.0)
    return z1, z2


def _conv_fwd(z_bcv, conv_w):
    _, S, C = z_bcv.shape

    def body(z_ref, w_ref, o_ref):
        w = w_ref[...]
        zc = z_ref[1] * z_ref[2]
        z1, z2 = _conv_taps(zc)
        y = w[0:1] * z2 + w[1:2] * z1 + w[2:3] * zc
        o_ref[...] = (z_ref[0] * y).astype(BF16)

    return _call(
        body, "conv_fwd", (C // LANES,),
        [pl.BlockSpec((3, S, LANES), lambda j: (0, 0, j)), pl.BlockSpec((3, LANES), lambda j: (0, j))],
        pl.BlockSpec((S, LANES), lambda j: (0, j)),
        _sds((S, C), BF16),
    )(z_bcv, conv_w)


def _rope(x, cs, half):
    c, s1, s2 = cs[:, :LANES], cs[:, LANES:2 * LANES], cs[:, 2 * LANES:]
    return x * c + pltpu.roll(x, LANES - half, 1) * s1 + pltpu.roll(x, half, 1) * s2


def _unrope(d, cs, half):
    c, s1, s2 = cs[:, :LANES], cs[:, LANES:2 * LANES], cs[:, 2 * LANES:]
    return d * c + pltpu.roll(d * s1, half, 1) + pltpu.roll(d * s2, LANES - half, 1)


def _mla_prep(z_qkr, gq, gkv, cs, c256_w):
    S = z_qkr.shape[0]
    QL, KVL = gq.shape[1], gkv.shape[1]
    H = N_DEV
    tm = _tile(S, 256, SUBLANES)
    half = ROPE // 2

    def body(z_ref, gq_ref, gkv_ref, cs_ref, w_ref, q_ref, k_ref, v_ref, qn_ref, kvn_ref, rq_ref, rkv_ref):
        z = z_ref[...]
        cs_t = cs_ref[...]
        qn32, rq = _rms_fwd(z[:, :QL], gq_ref[...])
        kvn32, rkv = _rms_fwd(z[:, QL:QL + KVL], gkv_ref[...])
        qn = qn32.astype(BF16)
        kvn = kvn32.astype(BF16)
        qn_ref[...] = qn
        kvn_ref[...] = kvn
        rq_ref[...] = rq
        rkv_ref[...] = rkv
        krope = _rope(z[:, QL + KVL:], cs_t, half).astype(BF16)
        for h in range(H):
            lo, mid, hi = h * HEAD_SLOT, h * HEAD_SLOT + LANES, (h + 1) * HEAD_SLOT
            q = _dot(qn, w_ref[h, KVL:KVL + QL, :])
            kv = _dot(kvn, w_ref[h, 0:KVL, :])
            q_ref[:, lo:mid] = q[:, :LANES].astype(BF16)
            q_ref[:, mid:hi] = _rope(q[:, LANES:], cs_t, half).astype(BF16)
            k_ref[:, lo:mid] = kv[:, :LANES].astype(BF16)
            k_ref[:, mid:hi] = krope
            v_ref[:, h * VDIM:(h + 1) * VDIM] = kv[:, LANES:].astype(BF16)

    return _call(
        body, "mla_prep", (S // tm,),
        [_rows(tm, z_qkr.shape[1]), _whole(gq), _whole(gkv), _rows(tm, 3 * LANES), _whole(c256_w)],
        [_rows(tm, H * HEAD_SLOT), _rows(tm, H * HEAD_SLOT), _rows(tm, H * VDIM), _rows(tm, QL), _rows(tm, KVL),
         _rows(tm, 1), _rows(tm, 1)],
        [_sds((S, H * HEAD_SLOT), BF16), _sds((S, H * HEAD_SLOT), BF16), _sds((S, H * VDIM), BF16),
         _sds((S, QL), BF16), _sds((S, KVL), BF16), _sds((S, 1), F32), _sds((S, 1), F32)],
    )(z_qkr, gq, gkv, cs, c256_w)


def _chunk_mask(t):
    shift = CHUNK.bit_length() - 1
    row = lax.broadcasted_iota(jnp.int32, (t, t), 0) >> shift
    col = lax.broadcasted_iota(jnp.int32, (t, t), 1) >> shift
    return col <= row


def _attn_fwd(q, k, v, H):
    S = q.shape[0]
    t = _tile(S, 512, CHUNK)
    nq = S // t

    def body(q_ref, k_ref, v_ref, o_ref, lse_ref):
        qi = pl.program_id(1)
        qv = q_ref[...]

        def block(kj, carry, masked):
            m, l, acc = carry
            off = pl.multiple_of(kj * t, t)
            s = _dot_nt(qv, k_ref[pl.ds(off, t), :]) * ATTN_SCALE
            if masked:
                s = jnp.where(_chunk_mask(t), s, -1e30)
            m_new = jnp.maximum(m, jnp.max(s, axis=-1, keepdims=True))
            alpha = jnp.exp(m - m_new)
            p = jnp.exp(s - m_new)
            l = alpha * l + jnp.sum(p, axis=-1, keepdims=True)
            acc = alpha * acc + _dot(p.astype(BF16), v_ref[pl.ds(off, t), :])
            return m_new, l, acc

        init = (jnp.full((t, 1), -1e30, F32), jnp.zeros((t, 1), F32), jnp.zeros((t, VDIM), F32))
        carry = lax.fori_loop(0, qi, lambda kj, c: block(kj, c, False), init)
        m, l, acc = block(qi, carry, True)
        o_ref[...] = (acc / l).astype(BF16)
        lse_ref[...] = jnp.broadcast_to(m + jnp.log(l), (t, LANES))

    return _call(
        body, "attn_fwd", (H, nq),
        [pl.BlockSpec((t, HEAD_SLOT), lambda h, i: (i, h)), pl.BlockSpec((S, HEAD_SLOT), lambda h, i: (0, h)),
         pl.BlockSpec((S, VDIM), lambda h, i: (0, h))],
        [pl.BlockSpec((t, VDIM), lambda h, i: (i, h)), pl.BlockSpec((t, LANES), lambda h, i: (i, h))],
        [_sds((S, H * VDIM), BF16), _sds((S, H * LANES), F32)],
    )(q, k, v)


def _merge_wo(o, by, z_gg, h, sq_w, c128_w):
    S, D = h.shape
    C = by.shape[1]
    r = sq_w.shape[1] // 3
    tm = _tile(S, 512, SUBLANES)

    def body(o_ref, by_ref, gg_ref, h_ref, wmo_ref, wo_ref, wco_ref, h2_ref, mg_ref, yc_ref, ym_ref):
        ymla = _dot(o_ref[...], wmo_ref[...].reshape(N_DEV * r, D))
        yconv = _dot(by_ref[...], _cat_slots(wco_ref))
        gg = gg_ref[...]
        merged = (_sig(gg[:, :D]) * yconv + _sig(gg[:, D:]) * ymla).astype(BF16)
        mg_ref[...] = merged
        yc_ref[...] = yconv.astype(BF16)
        ym_ref[...] = ymla.astype(BF16)
        h2_ref[...] = h_ref[...] + _dot(merged, wo_ref[...].reshape(N_DEV * r, D))

    return _call(
        body, "merge_wo", (S // tm,),
        [_rows(tm, o.shape[1]), _rows(tm, C), _rows(tm, 2 * D), _rows(tm, D), _slab(sq_w, r, 0), _slab(sq_w, r, 1),
         _slab(c128_w, C, 0)],
        [_rows(tm, D)] * 4,
        [_sds((S, D), F32)] + [_sds((S, D), BF16)] * 3,
    )(o, by, z_gg, h, sq_w, sq_w, c128_w)


def _ple_fwd(h, gain, p, sq_w, c128_w, C):
    S, D = h.shape
    P = p.shape[1]
    r = sq_w.shape[1] // 3
    tm = _tile(S, 512, SUBLANES)

    def body(h_ref, gain_ref, p_ref, wpg_ref, wpp_ref, o_ref, pre_ref, pp_ref, n_ref, r_ref):
        x = h_ref[...]
        n32, rstd = _rms_fwd(x, gain_ref[...])
        n = n32.astype(BF16)
        n_ref[...] = n
        r_ref[...] = rstd
        pre = _dot(n, wpg_ref[...].reshape(N_DEV * r, D))
        pp = _dot(p_ref[...].astype(BF16), _cat_slots(wpp_ref))
        pre_ref[...] = pre
        pp_ref[...] = pp
        o_ref[...] = x + _sig(pre) * pp

    return _call(
        body, "ple_fwd", (S // tm,),
        [_rows(tm, D), _whole(gain), _rows(tm, P), _slab(sq_w, r, 2), _slab(c128_w, P, C // P)],
        [_rows(tm, D), _rows(tm, D), _rows(tm, D), _rows(tm, D), _rows(tm, 1)],
        [_sds((S, D), F32)] * 3 + [_sds((S, D), BF16), _sds((S, 1), F32)],
    )(h, gain, p, sq_w, c128_w)


def _final_loss(h, gain, target):
    S, D = h.shape
    tm = _tile(S, 512, SUBLANES)

    def body(h_ref, gain_ref, t_ref, dh_ref, loss_ref, dg_ref):
        @pl.when(pl.program_id(0) == 0)
        def _():
            loss_ref[...] = jnp.zeros_like(loss_ref)
            dg_ref[...] = jnp.zeros_like(dg_ref)

        x = h_ref[...]
        gain_v = gain_ref[...]
        y, rstd = _rms_fwd(x, gain_v)
        err = y - t_ref[...]
        loss_ref[...] += 0.5 * jnp.sum(jnp.mean(err * err, axis=-1, keepdims=True))
        dx, dgain = _rms_bwd(err * (1.0 / D), x, rstd, gain_v)
        dh_ref[...] = dx
        dg_ref[...] += dgain

    return _call(
        body, "final_loss", (S // tm,),
        [_rows(tm, D), _whole(gain), _rows(tm, D)],
        [_rows(tm, D), pl.BlockSpec((1, LANES), lambda i: (0, 0)), pl.BlockSpec((1, D), lambda i: (0, 0))],
        [_sds((S, D), F32), _sds((1, LANES), F32), _sds((1, D), F32)],
    )(h, gain, target)


def _tn_call(body, name, grid, in_specs, out_spec, out_shape, scratch, operands, prev):
    n = len(operands)
    if prev is None:
        return _call(body, name, grid, in_specs, out_spec, out_shape, scratch)(*operands)
    assert prev.shape == out_shape.shape and prev.dtype == out_shape.dtype

    def wrapped(*refs):
        body(*refs[:n], *refs[n + 1:])

    return _call(wrapped, name, grid, in_specs + [ANY], out_spec, out_shape, scratch, {n: 0})(*operands, prev)


def _transposed(x_ref, xt_ref, first):
    @pl.when(first)
    def _():
        xt_ref[...] = x_ref[...].astype(BF16).T


def _tn_slots(x, dy, prev, rows_total, row_off):
    S, K = x.shape
    B, _, c = dy.shape
    tk = _tile(K, 512, LANES)

    def body(x_ref, dy_ref, o_ref, xt_ref):
        _transposed(x_ref, xt_ref, pl.program_id(1) == 0)
        o_ref[0] = _dot(xt_ref[...], dy_ref[0]).astype(BF16)

    return _tn_call(
        body, "tn_slots", (K // tk, B),
        [pl.BlockSpec((S, tk), lambda i, b: (0, i)), pl.BlockSpec((1, S, c), lambda i, b: (b, 0, 0))],
        pl.BlockSpec((1, tk, c), lambda i, b: (b, row_off // tk + i, 0)),
        _sds((B, rows_total, c), BF16), [pltpu.VMEM((tk, S), BF16)], [x, dy], prev)


def _tn_plain(x, dy, out_dtype=F32):
    S, K = x.shape
    B, _, c = dy.shape
    tk = _tile(K, 512, LANES)
    tn = _tile(c, 1024, LANES)

    def body(x_ref, dy_ref, o_ref, xt_ref):
        _transposed(x_ref, xt_ref, (pl.program_id(1) == 0) & (pl.program_id(2) == 0))
        o_ref[0] = _dot(xt_ref[...], dy_ref[0]).astype(out_dtype)

    return _call(
        body, "tn_plain", (K // tk, B, c // tn),
        [pl.BlockSpec((S, tk), lambda i, b, j: (0, i)), pl.BlockSpec((1, S, tn), lambda i, b, j: (b, 0, j))],
        pl.BlockSpec((1, tk, tn), lambda i, b, j: (b, i, j)),
        _sds((B, K, c), out_dtype), [pltpu.VMEM((tk, S), BF16)],
    )(x, dy)


def _tn_down(a, dh, prev, rows_total, which):
    nb, S, c = a.shape
    D = dh.shape[1]
    r = c // 2
    tn = _tile(D, 512, LANES)

    def body(a_ref, dh_ref, o_ref, xt_ref):
        _transposed(a_ref.at[0], xt_ref, pl.program_id(1) == 0)
        g = 0.5 * _dot(xt_ref[...], dh_ref[...].astype(BF16))
        o_ref[...] = g.astype(BF16).reshape(2, r, tn)

    return _tn_call(
        body, "tn_down", (nb, D // tn),
        [pl.BlockSpec((1, S, c), lambda i, j: (i, 0, 0)), pl.BlockSpec((S, tn), lambda i, j: (0, j))],
        pl.BlockSpec((2, r, tn), lambda i, j: (i, which, j)),
        _sds((N_DEV, rows_total, D), BF16), [pltpu.VMEM((c, S), BF16)], [a, dh], prev)


def _tn_square(x, dy, prev, rows_total, member):
    S, K = x.shape
    N = dy.shape[1]
    r = K // N_DEV
    tk = _tile(K, 512, r)
    tn = _tile(N, 512, LANES)

    def body(x_ref, dy_ref, o_ref, xt_ref):
        _transposed(x_ref, xt_ref, pl.program_id(1) == 0)
        g = _dot(xt_ref[...], dy_ref[...].astype(BF16))
        o_ref[...] = g.astype(BF16).reshape(tk // r, r, tn)

    return _tn_call(
        body, "tn_square", (K // tk, N // tn),
        [pl.BlockSpec((S, tk), lambda i, j: (0, i)), pl.BlockSpec((S, tn), lambda i, j: (0, j))],
        pl.BlockSpec((tk // r, r, tn), lambda i, j: (i, member, j)),
        _sds((N_DEV, rows_total, N), BF16), [pltpu.VMEM((tk, S), BF16)], [x, dy], prev)


def _tn_cols(x, dy, prev, rows_total, row_block):
    S, K = x.shape
    N = dy.shape[1]
    cw = N // N_DEV

    def body(x_ref, dy_ref, o_ref):
        g = _dot(x_ref[...].astype(BF16).T, dy_ref[...])
        for d in range(N_DEV):
            o_ref[d] = g[:, d * cw:(d + 1) * cw].astype(BF16)

    return _tn_call(
        body, "tn_cols", (1,),
        [pl.BlockSpec((S, K), lambda i: (0, 0)), pl.BlockSpec((S, N), lambda i: (0, 0))],
        pl.BlockSpec((N_DEV, K, cw), lambda i: (0, row_block, 0)),
        _sds((N_DEV, rows_total, cw), BF16), [], [x, dy], prev)


def _tn_heads(qn, kvn, dqp, dkv):
    S, QL = qn.shape
    KVL = kvn.shape[1]

    def body(qn_ref, kvn_ref, dq_ref, dkv_ref, o_ref):
        o_ref[0, 0:KVL, :] = _dot(kvn_ref[...].T, dkv_ref[...]).astype(BF16)
        o_ref[0, KVL:KVL + QL, :] = _dot(qn_ref[...].T, dq_ref[...]).astype(BF16)

    head = pl.BlockSpec((S, HEAD_SLOT), lambda h: (0, h))
    return _call(
        body, "tn_heads", (N_DEV,),
        [pl.BlockSpec((S, QL), lambda h: (0, 0)), pl.BlockSpec((S, KVL), lambda h: (0, 0)), head, head],
        pl.BlockSpec((1, KVL + QL, HEAD_SLOT), lambda h: (h, 0, 0)),
        _sds((N_DEV, KVL + QL, HEAD_SLOT), BF16),
    )(qn, kvn, dqp, dkv)


def _ple_bwd(dh, pre, pp, h, rstd, gain, sq_w, after):
    S, D = h.shape
    r = sq_w.shape[1] // 3
    tm = _tile(S, 512, SUBLANES)

    def body(dh_ref, pre_ref, pp_ref, h_ref, r_ref, gain_ref, wpg_ref, *rest):
        o_ref, dpre_ref, dpp_ref, dg_ref = rest[len(after):]

        @pl.when(pl.program_id(0) == 0)
        def _():
            dg_ref[...] = jnp.zeros_like(dg_ref)

        d = dh_ref[...]
        gate = _sig(pre_ref[...])
        dpre = (d * pp_ref[...] * gate * (1.0 - gate)).astype(BF16)
        dpre_ref[...] = dpre
        dpp_ref[...] = (d * gate).astype(BF16)
        dn = _dot_nt(dpre, wpg_ref[...].reshape(N_DEV * r, D))
        dx, dgain = _rms_bwd(dn, h_ref[...], r_ref[...], gain_ref[...])
        o_ref[...] = d + dx
        dg_ref[...] += dgain

    return _call(
        body, "ple_bwd", (S // tm,),
        [_rows(tm, D), _rows(tm, D), _rows(tm, D), _rows(tm, D), _rows(tm, 1), _whole(gain), _slab(sq_w, r, 2)]
        + [ANY] * len(after),
        [_rows(tm, D), _rows(tm, D), _rows(tm, D), pl.BlockSpec((1, D), lambda i: (0, 0))],
        [_sds((S, D), F32), _sds((S, D), BF16), _sds((S, D), BF16), _sds((1, D), F32)],
    )(dh, pre, pp, h, rstd, gain, sq_w, *after)


def _ffn_bwd_act(dh, dn_w, which, gu, after=()):
    S, D = dh.shape
    _, _, c = gu.shape
    nb = N_DEV // 2
    tm = _tile(S, 256, SUBLANES)

    def body(dh_ref, w_ref, gu_ref, *rest):
        dgu_ref = rest[len(after)]
        dhb = dh_ref[...].astype(BF16)
        for d in range(nb):
            da = 0.5 * _dot_nt(dhb, _down_weight(w_ref, d, c))
            g = gu_ref[d].astype(F32)
            u = gu_ref[nb + d].astype(F32)
            sg = _sig(g)
            dgu_ref[d] = (da * u * sg * (1.0 + g * (1.0 - sg))).astype(BF16)
            dgu_ref[nb + d] = (da * g * sg).astype(BF16)

    act = pl.BlockSpec((N_DEV, tm, c), lambda i: (0, i, 0))
    return _call(
        body, "ffn_bwd_act", (S // tm,),
        [_rows(tm, D), _slab(dn_w, c // 2, which), act] + [ANY] * len(after),
        act,
        _sds((N_DEV, S, c), BF16),
    )(dh, dn_w, gu, *after)


def _ffn_bwd_in(dgu, gu_w, which, h, rstd, gain, dh):
    S, D = h.shape
    c = dgu.shape[2]
    tm = _tile(S, 256, SUBLANES)

    def body(dgu_ref, w_ref, h_ref, r_ref, gain_ref, dh_ref, o_ref, dgain_ref):
        @pl.when(pl.program_id(0) == 0)
        def _():
            dgain_ref[...] = jnp.zeros_like(dgain_ref)

        dn = _dot_nt(dgu_ref[0], w_ref[0])
        for d in range(1, N_DEV):
            dn = dn + _dot_nt(dgu_ref[d], w_ref[d])
        dx, dgain = _rms_bwd(dn, h_ref[...], r_ref[...], gain_ref[...])
        o_ref[...] = dh_ref[...] + dx
        dgain_ref[...] += dgain

    return _call(
        body, "ffn_bwd_in", (S // tm,),
        [pl.BlockSpec((N_DEV, tm, c), lambda i: (0, i, 0)), _slab(gu_w, D, which), _rows(tm, D), _rows(tm, 1),
         _whole(gain), _rows(tm, D)],
        [_rows(tm, D), pl.BlockSpec((1, D), lambda i: (0, 0))],
        [_sds((S, D), F32), _sds((1, D), F32)],
    )(dgu, gu_w, h, rstd, gain, dh)


def _merge_bwd(dh, z_gg, yconv, ymla, sq_w, c128_w, C, after):
    S, D = dh.shape
    r = sq_w.shape[1] // 3
    HV = N_DEV * r
    tm = _tile(S, 512, SUBLANES)

    def body(dh_ref, gg_ref, yc_ref, ym_ref, wmo_ref, wo_ref, wco_ref, *rest):
        dgg_ref, dby_ref, do_ref, dyc_ref, dym_ref = rest[len(after):]
        dm = _dot_nt(dh_ref[...].astype(BF16), wo_ref[...].reshape(HV, D))
        gg = gg_ref[...]
        sgc = _sig(gg[:, :D])
        sgm = _sig(gg[:, D:])
        dyc = (dm * sgc).astype(BF16)
        dym = (dm * sgm).astype(BF16)
        dyc_ref[...] = dyc
        dym_ref[...] = dym
        dgg_ref[:, :D] = (dm * yc_ref[...].astype(F32) * sgc * (1.0 - sgc)).astype(BF16)
        dgg_ref[:, D:] = (dm * ym_ref[...].astype(F32) * sgm * (1.0 - sgm)).astype(BF16)
        dby_ref[...] = _dot_nt(dyc, _cat_slots(wco_ref))
        do_ref[...] = _dot_nt(dym, wmo_ref[...].reshape(HV, D)).astype(BF16)

    return _call(
        body, "merge_bwd", (S // tm,),
        [_rows(tm, D), _rows(tm, 2 * D), _rows(tm, D), _rows(tm, D), _slab(sq_w, r, 0), _slab(sq_w, r, 1),
         _slab(c128_w, C, 0)] + [ANY] * len(after),
        [_rows(tm, 2 * D), _rows(tm, C), _rows(tm, HV), _rows(tm, D), _rows(tm, D)],
        [_sds((S, 2 * D), BF16), _sds((S, C), F32), _sds((S, HV), BF16), _sds((S, D), BF16), _sds((S, D), BF16)],
    )(dh, z_gg, yconv, ymla, sq_w, sq_w, c128_w, *after)


def _conv_bwd(z_bcv, conv_w, dby):
    _, S, C = z_bcv.shape

    def body(z_ref, w_ref, dby_ref, dz_ref, dw_ref):
        w = w_ref[...]
        c = z_ref[1]
        v = z_ref[2]
        d = dby_ref[...]
        zc = c * v
        z1, z2 = _conv_taps(zc)
        y = w[0:1] * z2 + w[1:2] * z1 + w[2:3] * zc
        dz_ref[0] = (d * y).astype(BF16)
        dy = d * z_ref[0]
        rows = lax.broadcasted_iota(jnp.int32, dy.shape, 0)
        dy1 = jnp.where(rows < S - 1, pltpu.roll(dy, S - 1, 0), 0.0)
        dy2 = jnp.where(rows < S - 2, pltpu.roll(dy, S - 2, 0), 0.0)
        dzc = w[2:3] * dy + w[1:2] * dy1 + w[0:1] * dy2
        dz_ref[1] = (dzc * v).astype(BF16)
        dz_ref[2] = (dzc * c).astype(BF16)
        dw_ref[0:1, :] = jnp.sum(dy * z2, axis=0, keepdims=True)
        dw_ref[1:2, :] = jnp.sum(dy * z1, axis=0, keepdims=True)
        dw_ref[2:3, :] = jnp.sum(dy * zc, axis=0, keepdims=True)

    three = pl.BlockSpec((3, S, LANES), lambda j: (0, 0, j))
    wspec = pl.BlockSpec((3, LANES), lambda j: (0, j))
    return _call(
        body, "conv_bwd", (C // LANES,),
        [three, wspec, pl.BlockSpec((S, LANES), lambda j: (0, j))],
        [three, wspec],
        [_sds((3, S, C), BF16), _sds((3, C), F32)],
    )(z_bcv, conv_w, dby)


def _attn_bwd_q(q, k, v, do, o, lse, H):
    S = q.shape[0]
    t = _tile(S, 512, CHUNK)
    nq = S // t

    def body(q_ref, k_ref, v_ref, do_ref, o_ref, lse_ref, dq_ref, dl_ref):
        qi = pl.program_id(1)
        qv = q_ref[...]
        dov = do_ref[...]
        lse_v = lse_ref[...][:, :1]
        delta = jnp.sum(dov.astype(F32) * o_ref[...].astype(F32), axis=-1, keepdims=True)
        dl_ref[...] = jnp.broadcast_to(delta, (t, LANES))

        def block(kj, dq, masked):
            off = pl.multiple_of(kj * t, t)
            kv = k_ref[pl.ds(off, t), :]
            s = _dot_nt(qv, kv) * ATTN_SCALE
            if masked:
                s = jnp.where(_chunk_mask(t), s, -1e30)
            p = jnp.exp(s - lse_v)
            dp = _dot_nt(dov, v_ref[pl.ds(off, t), :])
            ds = (p * (dp - delta) * ATTN_SCALE).astype(BF16)
            return dq + _dot(ds, kv)

        dq = lax.fori_loop(0, qi, lambda kj, c: block(kj, c, False), jnp.zeros((t, HEAD_SLOT), F32))
        dq_ref[...] = block(qi, dq, True)

    qspec = lambda w: pl.BlockSpec((t, w), lambda h, i: (i, h))
    kspec = lambda w: pl.BlockSpec((S, w), lambda h, i: (0, h))
    return _call(
        body, "attn_bwd_q", (H, nq),
        [qspec(HEAD_SLOT), kspec(HEAD_SLOT), kspec(VDIM), qspec(VDIM), qspec(VDIM), qspec(LANES)],
        [qspec(HEAD_SLOT), qspec(LANES)],
        [_sds((S, H * HEAD_SLOT), F32), _sds((S, H * LANES), F32)],
    )(q, k, v, do, o, lse)


def _attn_bwd_kv(q, k, v, do, lse, delta, H):
    S = q.shape[0]
    t = _tile(S, 512, CHUNK)
    nk = S // t
    TN = (((0,), (0,)), ((), ()))

    def body(q_ref, k_ref, v_ref, do_ref, lse_ref, dl_ref, dk_ref, dv_ref):
        kj = pl.program_id(1)
        kv = k_ref[...]
        vv = v_ref[...]

        def block(qi, carry, masked):
            dk, dv = carry
            off = pl.multiple_of(qi * t, t)
            qv = q_ref[pl.ds(off, t), :]
            dov = do_ref[pl.ds(off, t), :]
            s = _dot_nt(qv, kv) * ATTN_SCALE
            if masked:
                s = jnp.where(_chunk_mask(t), s, -1e30)
            p = jnp.exp(s - lse_ref[pl.ds(off, t), :][:, :1])
            dp = _dot_nt(dov, vv)
            ds = (p * (dp - dl_ref[pl.ds(off, t), :][:, :1]) * ATTN_SCALE).astype(BF16)
            dk = dk + lax.dot_general(ds, qv, TN, preferred_element_type=F32)
            dv = dv + lax.dot_general(p.astype(BF16), dov, TN, preferred_element_type=F32)
            return dk, dv

        init = (jnp.zeros((t, HEAD_SLOT), F32), jnp.zeros((t, VDIM), F32))
        carry = block(kj, init, True)
        dk, dv = lax.fori_loop(kj + 1, nk, lambda qi, c: block(qi, c, False), carry)
        dk_ref[...] = dk
        dv_ref[...] = dv.astype(BF16)

    kspec = lambda w: pl.BlockSpec((t, w), lambda h, j: (j, h))
    qspec = lambda w: pl.BlockSpec((S, w), lambda h, j: (0, h))
    return _call(
        body, "attn_bwd_kv", (H, nk),
        [qspec(HEAD_SLOT), kspec(HEAD_SLOT), kspec(VDIM), qspec(VDIM), qspec(LANES), qspec(LANES)],
        [kspec(HEAD_SLOT), kspec(VDIM)],
        [_sds((S, H * HEAD_SLOT), F32), _sds((S, H * VDIM), BF16)],
    )(q, k, v, do, lse, delta)


def _mla_prep_bwd(dq, dk, dv, z_qkr, rq, rkv, gq, gkv, cs, c256_w):
    S = z_qkr.shape[0]
    QL, KVL = gq.shape[1], gkv.shape[1]
    H = N_DEV
    tm = _tile(S, 256, SUBLANES)
    half = ROPE // 2

    def body(dq_ref, dk_ref, dv_ref, z_ref, rq_ref, rkv_ref, gq_ref, gkv_ref, cs_ref, w_ref,
             dz_ref, dqp_ref, dkv_ref, dgq_ref, dgkv_ref):
        @pl.when(pl.program_id(0) == 0)
        def _():
            dgq_ref[...] = jnp.zeros_like(dgq_ref)
            dgkv_ref[...] = jnp.zeros_like(dgkv_ref)

        cs_t = cs_ref[...]
        dkr = jnp.zeros((tm, LANES), F32)
        dqn = jnp.zeros((tm, QL), F32)
        dkvn = jnp.zeros((tm, KVL), F32)
        for h in range(H):
            lo, mid, hi = h * HEAD_SLOT, h * HEAD_SLOT + LANES, (h + 1) * HEAD_SLOT
            dqp_ref[:, lo:mid] = dq_ref[:, lo:mid].astype(BF16)
            dqp_ref[:, mid:hi] = _unrope(dq_ref[:, mid:hi], cs_t, half).astype(BF16)
            dkv_ref[:, lo:mid] = dk_ref[:, lo:mid].astype(BF16)
            dkv_ref[:, mid:hi] = dv_ref[:, h * VDIM:(h + 1) * VDIM]
            dkr = dkr + dk_ref[:, mid:hi]
            dqn = dqn + _dot_nt(dqp_ref[:, lo:hi], w_ref[h, KVL:KVL + QL, :])
            dkvn = dkvn + _dot_nt(dkv_ref[:, lo:hi], w_ref[h, 0:KVL, :])
        z = z_ref[...]
        dqc, dgq = _rms_bwd(dqn, z[:, :QL], rq_ref[...], gq_ref[...])
        dkvc, dgkv = _rms_bwd(dkvn, z[:, QL:QL + KVL], rkv_ref[...], gkv_ref[...])
        dz_ref[:, :QL] = dqc.astype(BF16)
        dz_ref[:, QL:QL + KVL] = dkvc.astype(BF16)
        dz_ref[:, QL + KVL:] = _unrope(dkr, cs_t, half).astype(BF16)
        dgq_ref[...] += dgq
        dgkv_ref[...] += dgkv

    W = z_qkr.shape[1]
    return _call(
        body, "mla_prep_bwd", (S // tm,),
        [_rows(tm, H * HEAD_SLOT), _rows(tm, H * HEAD_SLOT), _rows(tm, H * VDIM), _rows(tm, W), _rows(tm, 1),
         _rows(tm, 1), _whole(gq), _whole(gkv), _rows(tm, 3 * LANES), _whole(c256_w)],
        [_rows(tm, W), _rows(tm, H * HEAD_SLOT), _rows(tm, H * HEAD_SLOT), _whole(gq), _whole(gkv)],
        [_sds((S, W), BF16), _sds((S, H * HEAD_SLOT), BF16), _sds((S, H * HEAD_SLOT), BF16),
         _sds((1, QL), F32), _sds((1, KVL), F32)],
    )(dq, dk, dv, z_qkr, rq, rkv, gq, gkv, cs, c256_w)


def _mix_in_bwd(d_bcv, dz_qkr, dgg, w_bcv, w_qkr, w_gg, h, rstd, gain, dh):
    S, D = h.shape
    C = d_bcv.shape[2]
    tm = _tile(S, 512, SUBLANES)

    def body(db_ref, dq_ref, dgg_ref, wb_ref, wq_ref, wg_ref, h_ref, r_ref, gain_ref, dh_ref, o_ref, dgain_ref):
        @pl.when(pl.program_id(0) == 0)
        def _():
            dgain_ref[...] = jnp.zeros_like(dgain_ref)

        dn = _dot_nt(dq_ref[...], wq_ref[...]) + _dot_nt(dgg_ref[...], wg_ref[...])
        for k in range(3):
            dn = dn + _dot_nt(db_ref[k], wb_ref[k])
        dx, dgain = _rms_bwd(dn, h_ref[...], r_ref[...], gain_ref[...])
        o_ref[...] = dh_ref[...] + dx
        dgain_ref[...] += dgain

    return _call(
        body, "mix_in_bwd", (S // tm,),
        [pl.BlockSpec((3, tm, C), lambda i: (0, i, 0)), _rows(tm, dz_qkr.shape[1]), _rows(tm, dgg.shape[1]),
         _whole(w_bcv), _whole(w_qkr), _whole(w_gg), _rows(tm, D), _rows(tm, 1), _whole(gain), _rows(tm, D)],
        [_rows(tm, D), pl.BlockSpec((1, D), lambda i: (0, 0))],
        [_sds((S, D), F32), _sds((1, D), F32)],
    )(d_bcv, dz_qkr, dgg, w_bcv, w_qkr, w_gg, h, rstd, gain, dh)


def _rope_tables(positions):
    half = ROPE // 2
    inv_freq = ROPE_THETA ** (-jnp.arange(0, ROPE, 2, dtype=F32) / ROPE)
    ang = positions.astype(F32)[:, None] * inv_freq
    cos, sin = jnp.cos(ang), jnp.sin(ang)
    z = jnp.zeros_like(cos)
    pad = jnp.zeros((positions.shape[0], LANES - 2 * half), F32)
    return jnp.concatenate([cos, cos, pad, -sin, z, pad, z, sin, pad], axis=1)


def _layer_fwd(h0, p_l, cs, w, sm):
    C = sm["conv_w"].shape[1]
    QL, KVL = sm["q_norm"].shape[1], sm["kv_norm"].shape[1]
    gu1, a1, n1, r1 = _ffn_up(h0, sm["ffn1_norm"], w["gu"], 0)
    h1 = _ffn_down(a1, w["dn"], 0, h0)
    w_bcv, w_qkr, w_gg = _win_split(w["win"], C, QL, KVL)
    z_bcv, z_qkr, z_gg, un, rm = _mix_in(h1, sm["mix_norm"], w_bcv, w_qkr, w_gg)
    by = _conv_fwd(z_bcv, sm["conv_w"])
    q, k, v, qn, kvn, rq, rkv = _mla_prep(z_qkr, sm["q_norm"], sm["kv_norm"], cs, w["c256"])
    o, lse = _attn_fwd(q, k, v, N_DEV)
    h2, merged, yconv, ymla = _merge_wo(o, by, z_gg, h1, w["sq"], w["c128"])
    gu2, a2, n2, r2 = _ffn_up(h2, sm["ffn2_norm"], w["gu"], 1)
    h3 = _ffn_down(a2, w["dn"], 1, h2)
    h4, pre, pp, pn, rp = _ple_fwd(h3, sm["ple_norm"], p_l, w["sq"], w["c128"], C)
    saved = dict(h0=h0, gu1=gu1, a1=a1, n1=n1, r1=r1, h1=h1, w_bcv=w_bcv, w_qkr=w_qkr, w_gg=w_gg, z_bcv=z_bcv,
                 z_qkr=z_qkr, z_gg=z_gg, un=un, rm=rm, by=by, q=q, k=k, v=v, qn=qn, kvn=kvn, rq=rq, rkv=rkv, o=o,
                 lse=lse, h2=h2, merged=merged, yconv=yconv, ymla=ymla, gu2=gu2, a2=a2, n2=n2, r2=r2, h3=h3,
                 pre=pre, pp=pp, pn=pn, rp=rp, p=p_l)
    return h4, saved


def _layer_bwd_late(dh4, s, w, sm, after):
    D = dh4.shape[1]
    C = sm["conv_w"].shape[1]
    P = s["p"].shape[1]
    rows = {n: w[n].shape[1] for n in CLASSES}
    small = {}
    dh3, dpre, dpp, small["ple_norm"] = _ple_bwd(dh4, s["pre"], s["pp"], s["h3"], s["rp"], sm["ple_norm"], w["sq"],
                                                 after)
    g_sq = _tn_square(s["pn"], dpre, None, rows["sq"], 2)
    g_c128 = _tn_cols(s["p"], dpp, None, rows["c128"], C // P)

    dgu2 = _ffn_bwd_act(dh3, w["dn"], 1, s["gu2"])
    g_dn = _tn_down(s["a2"], dh3, None, rows["dn"], 1)
    g_gu = _tn_slots(s["n2"], dgu2, None, rows["gu"], D)
    dh2, small["ffn2_norm"] = _ffn_bwd_in(dgu2, w["gu"], 1, s["h2"], s["r2"], sm["ffn2_norm"], dh3)
    return dh2, dict(gu=g_gu, dn=g_dn, sq=g_sq, c128=g_c128), small


def _layer_bwd_mixer(dh2, part, small, s, cs, w, sm, after):
    C = sm["conv_w"].shape[1]
    rows = {n: w[n].shape[1] for n in CLASSES}
    g_gu, g_dn, g_sq, g_c128 = part["gu"], part["dn"], part["sq"], part["c128"]

    dgg, dby, do, dyc, dym = _merge_bwd(dh2, s["z_gg"], s["yconv"], s["ymla"], w["sq"], w["c128"], C, after)
    g_sq = _tn_square(s["merged"], dh2, g_sq, rows["sq"], 1)
    g_sq = _tn_square(s["o"], dym, g_sq, rows["sq"], 0)
    g_c128 = _tn_cols(s["by"], dyc, g_c128, rows["c128"], 0)
    d_bcv, small["conv_w"] = _conv_bwd(s["z_bcv"], sm["conv_w"], dby)
    dq, delta = _attn_bwd_q(s["q"], s["k"], s["v"], do, s["o"], s["lse"], N_DEV)
    dk, dv = _attn_bwd_kv(s["q"], s["k"], s["v"], do, s["lse"], delta, N_DEV)
    dz_qkr, dqp, dkv, small["q_norm"], small["kv_norm"] = _mla_prep_bwd(
        dq, dk, dv, s["z_qkr"], s["rq"], s["rkv"], sm["q_norm"], sm["kv_norm"], cs, w["c256"])
    g_c256 = _tn_heads(s["qn"], s["kvn"], dqp, dkv)
    un = s["un"]
    g_win = _win_merge(_tn_plain(un, d_bcv), _tn_plain(un, dz_qkr[None])[0], _tn_plain(un, dgg[None])[0],
                       w["win"].shape[2])
    dh1, small["mix_norm"] = _mix_in_bwd(d_bcv, dz_qkr, dgg, s["w_bcv"], s["w_qkr"], s["w_gg"], s["h1"], s["rm"],
                                         sm["mix_norm"], dh2)
    return dh1, dict(gu=g_gu, dn=g_dn, sq=g_sq, win=g_win, c128=g_c128, c256=g_c256), small


def _layer_bwd_first(dh1, part, small, s, w, sm, after):
    rows = {n: w[n].shape[1] for n in CLASSES}
    dgu1 = _ffn_bwd_act(dh1, w["dn"], 0, s["gu1"], after)
    g_dn = _tn_down(s["a1"], dh1, part["dn"], rows["dn"], 0)
    g_gu = _tn_slots(s["n1"], dgu1, part["gu"], rows["gu"], 0)
    dh0, small["ffn1_norm"] = _ffn_bwd_in(dgu1, w["gu"], 0, s["h0"], s["r1"], sm["ffn1_norm"], dh1)
    return dh0, dict(part, gu=g_gu, dn=g_dn), small


def _mesh_pos():
    return lax.axis_index("x"), lax.axis_index("y"), lax.axis_index("c")


def _other_chips(x, y):
    return [(1 - x, y), (x, 1 - y), (1 - x, 1 - y)]


def _pack(arrs, width):
    L = arrs[0].shape[0]
    shapes = [a.shape[1:] for a in arrs]
    R = sum(r for r, _ in shapes)

    def body(*refs):
        o_ref = refs[-1]
        off = 0
        for a_ref, (r, c) in zip(refs[:-1], shapes):
            o_ref[0, off:off + r, 0:c] = a_ref[0].astype(BF16)
            if c < width:
                o_ref[0, off:off + r, c:width] = jnp.zeros((r, width - c), BF16)
            off += r

    return _call(
        body, "pack", (L,),
        [pl.BlockSpec((1, r, c), lambda l: (l, 0, 0)) for r, c in shapes],
        pl.BlockSpec((1, R, width), lambda l: (l, 0, 0)),
        _sds((L, R, width), BF16),
    )(*arrs)


def _handshake(peers):
    barrier = pltpu.get_barrier_semaphore()
    for peer in peers:
        pl.semaphore_signal(barrier, inc=1, device_id=peer, device_id_type=MESH)
    pl.semaphore_wait(barrier, len(peers))


def _sequencer_call(body, name, out_types, sems, collective_id, operands):
    return pl.kernel(
        body, name=name, out_type=out_types,
        mesh=plsc.ScalarSubcoreMesh(axis_name="seq", num_cores=1),
        scratch_types=tuple(pltpu.SemaphoreType.DMA((k,)) for k in sems),
        compiler_params=pltpu.CompilerParams(collective_id=collective_id),
    )(*operands)


def _all_gather(packs, l, after, collective_id):
    n = len(packs)

    def body(*refs):
        ins, outs = refs[:n], refs[n + len(after):2 * n + len(after)]
        send_sems, recv_sems, local_sems = refs[2 * n + len(after):]
        x, y, c = _mesh_pos()
        me, sibling = (x, y, c), (x, y, 1 - c)
        chips = _other_chips(x, y)
        _handshake([sibling] + [(*chip, c) for chip in chips])

        def copy(q, k, block, to, src=None):
            slot = outs[q].at[4 * block[0] + 2 * block[1] + block[2]]
            return pltpu.make_async_remote_copy(
                src_ref=slot if src is None else src, dst_ref=slot,
                send_sem=send_sems.at[7 * q + k], recv_sem=recv_sems.at[7 * q + k], device_id=to, device_id_type=MESH)

        started = []
        for q in range(n):
            src = ins[q].at[l]
            mine = pltpu.make_async_copy(src, outs[q].at[4 * x + 2 * y + c], local_sems.at[q])
            mine.start()
            started.append(mine)
        sends = []
        for q in range(n):
            src = ins[q].at[l]
            sends.append(copy(q, 0, me, sibling, src=src))
            sends += [copy(q, 1 + j, me, (*chip, c), src=src) for j, chip in enumerate(chips)]
        for cp in sends:
            cp.start()
        for q in range(n):
            for j, chip in enumerate(chips):
                copy(q, 1 + j, (*chip, c), me).wait_recv()
                fwd = copy(q, 4 + j, (*chip, c), sibling)
                fwd.start()
                sends.append(fwd)
        for q in range(n):
            copy(q, 0, sibling, me).wait_recv()
            for j, chip in enumerate(chips):
                copy(q, 4 + j, (*chip, 1 - c), me).wait_recv()
        for cp in sends:
            cp.wait_send()
        for mine in started:
            mine.wait()

    return _sequencer_call(
        body, f"all_gather_{l}", [_sds((N_DEV,) + p.shape[1:], p.dtype) for p in packs], (7 * n, 7 * n, n),
        collective_id, list(packs) + list(after))


def _rs_d2d(gs, l, collective_id):
    n = len(gs)

    def body(*refs):
        ins, outs = refs[:n], refs[n:2 * n]
        send_sems, recv_sems = refs[2 * n:]
        x, y, c = _mesh_pos()
        _handshake([(x, y, 1 - c)])
        copies = []
        for q in range(n):
            for j in range(4):
                copies.append(pltpu.make_async_remote_copy(
                    src_ref=ins[q].at[2 * j + (1 - c)], dst_ref=outs[q].at[j], send_sem=send_sems.at[4 * q + j],
                    recv_sem=recv_sems.at[4 * q + j], device_id=(x, y, 1 - c), device_id_type=MESH))
        for cp in copies:
            cp.start()
        for cp in copies:
            cp.wait()

    return _sequencer_call(
        body, f"rs_d2d_{l}", [_sds((4,) + g.shape[1:], g.dtype) for g in gs], (4 * n, 4 * n), collective_id, gs)


def _rs_add_chip(gs, as_, after):
    n = len(gs)
    steps = 4
    tiles = [g.shape[1] // steps for g in gs]

    def chip(k):
        x, y, _ = _mesh_pos()
        return ([(x, y)] + _other_chips(x, y))[k]

    def body(*refs):
        g_refs, a_refs = refs[:4 * n], refs[4 * n:8 * n]
        own_refs, t_refs = refs[8 * n + len(after):9 * n + len(after)], refs[9 * n + len(after):]
        for q in range(n):
            g, a = g_refs[4 * q:4 * q + 4], a_refs[4 * q:4 * q + 4]
            own_refs[q][...] = g[0][0].astype(F32) + a[0][0].astype(F32)
            for k in range(1, 4):
                t_refs[q][k - 1] = (g[k][0].astype(F32) + a[k][0].astype(F32)).astype(BF16)

    def gspec(q, k):
        def index(i):
            px, py = chip(k)
            return 4 * px + 2 * py + lax.axis_index("c"), i, 0
        return pl.BlockSpec((1, tiles[q], gs[q].shape[2]), index)

    def aspec(q, k):
        def index(i):
            px, py = chip(k)
            return 2 * px + py, i, 0
        return pl.BlockSpec((1, tiles[q], gs[q].shape[2]), index)

    in_specs = [gspec(q, k) for q in range(n) for k in range(4)] + [aspec(q, k) for q in range(n) for k in range(4)]
    operands = [g for g in gs for _ in range(4)] + [a for a in as_ for _ in range(4)]
    out_specs = [pl.BlockSpec((tiles[q], gs[q].shape[2]), lambda i: (i, 0)) for q in range(n)]
    out_specs += [pl.BlockSpec((3, tiles[q], gs[q].shape[2]), lambda i: (0, i, 0)) for q in range(n)]
    out_shape = [_sds(g.shape[1:], F32) for g in gs] + [_sds((3,) + g.shape[1:], BF16) for g in gs]
    res = _call(body, "rs_add_chip", (steps,), in_specs + [ANY] * len(after), out_specs, out_shape)(*operands, *after)
    return res[:n], res[n:]


def _rs_ici(ts, l, collective_id):
    n = len(ts)

    def body(*refs):
        ins, outs = refs[:n], refs[n:2 * n]
        send_sems, recv_sems = refs[2 * n:]
        x, y, c = _mesh_pos()
        chips = _other_chips(x, y)
        _handshake([(*chip, c) for chip in chips])
        copies = []
        for q in range(n):
            for k, chip in enumerate(chips):
                copies.append(pltpu.make_async_remote_copy(
                    src_ref=ins[q].at[k], dst_ref=outs[q].at[k], send_sem=send_sems.at[3 * q + k],
                    recv_sem=recv_sems.at[3 * q + k], device_id=(*chip, c), device_id_type=MESH))
        for cp in copies:
            cp.start()
        for cp in copies:
            cp.wait()

    return _sequencer_call(
        body, f"rs_ici_{l}", [_sds(t.shape, t.dtype) for t in ts], (3 * n, 3 * n), collective_id, ts)


def _all_reduce_small(v):
    n, W = v.shape

    def body(v_ref, out_ref, slots, send_sems, recv_sems):
        x, y, c = _mesh_pos()
        me = 4 * x + 2 * y + c
        slots[me] = v_ref[...]
        copies = []
        for k in range(1, N_DEV):
            kx, ky, kc = (k >> 2) & 1, (k >> 1) & 1, k & 1
            peer = (1 - x if kx else x, 1 - y if ky else y, 1 - c if kc else c)
            copies.append(pltpu.make_async_remote_copy(
                src_ref=v_ref, dst_ref=slots.at[me], send_sem=send_sems.at[k - 1], recv_sem=recv_sems.at[k - 1],
                device_id=peer, device_id_type=MESH))
        for cp in copies:
            cp.start()
        for cp in copies:
            cp.wait()
        acc = slots[0]
        for d in range(1, N_DEV):
            acc = acc + slots[d]
        out_ref[...] = acc

    vm = pl.BlockSpec(memory_space=pltpu.VMEM)
    return pl.pallas_call(
        body, name="all_reduce_small",
        out_shape=_sds((n, W), F32),
        in_specs=[vm], out_specs=vm,
        scratch_shapes=[pltpu.VMEM((N_DEV, n, W), F32), pltpu.SemaphoreType.DMA((7,)), pltpu.SemaphoreType.DMA((7,))],
    )(v)


def _adamw_math(w, g, m, v):
    m2 = ADAM_B1 * m + (1.0 - ADAM_B1) * g
    v2 = ADAM_B2 * v + (1.0 - ADAM_B2) * (g * g)
    m_hat = m2 / (1.0 - ADAM_B1 ** ADAM_STEP)
    v_hat = v2 / (1.0 - ADAM_B2 ** ADAM_STEP)
    return -ADAM_LR * (m_hat / (jnp.sqrt(v_hat) + ADAM_EPS) + ADAM_WD * w), m2, v2


def _adamw(w, g, m, v):
    L, r, c = w.shape
    tr = _tile(r, max(SUBLANES, (256 * 1024 // c) // SUBLANES * SUBLANES), SUBLANES)

    def body(w_ref, g_ref, m_ref, v_ref, d_ref, nm_ref, nv_ref):
        d_ref[...], nm_ref[...], nv_ref[...] = _adamw_math(w_ref[...], g_ref[...], m_ref[...], v_ref[...])

    spec = pl.BlockSpec((1, tr, c), lambda l, i: (l, i, 0))
    return _call(body, "adamw", (L, r // tr), [spec] * 4, [spec] * 3, [_sds((L, r, c), F32)] * 3)(w, g, m, v)


def _adamw_reduced(w, m, v, own, b, row_off, tr, l, prev, after):
    L, r, c = w.shape
    W = own.shape[1]
    ob = row_off // tr
    extra = list(prev or ()) + list(after)

    def body(w_ref, m_ref, v_ref, own_ref, b_ref, *rest):
        g_ref, d_ref, nm_ref, nv_ref = rest[len(extra):]
        g = ((own_ref[...] + b_ref[0].astype(F32)) + b_ref[1].astype(F32)) + b_ref[2].astype(F32)
        g = g[:, :c]
        g_ref[0] = g
        d_ref[0], nm_ref[0], nv_ref[0] = _adamw_math(w_ref[0], g, m_ref[0], v_ref[0])

    spec = pl.BlockSpec((1, tr, c), lambda i: (l, i, 0))
    return _call(
        body, "adamw_reduced", (r // tr,),
        [spec] * 3 + [pl.BlockSpec((tr, W), lambda i: (ob + i, 0)), pl.BlockSpec((3, tr, W), lambda i: (0, ob + i, 0))]
        + [ANY] * len(extra),
        [spec] * 4, [_sds((L, r, c), F32)] * 4,
        aliases={5 + k: k for k in range(4)} if prev else None,
    )(w, m, v, own, b, *extra)


_MEMBERS = dict(gu=("ffn1_w_gu", "ffn2_w_gu"), dn=("ffn1_w_down", "ffn2_w_down"),
                sq=("w_mla_out", "w_o", "w_ple_gate"), win=("w_in",), c128=("w_conv_out", "w_ple_proj"),
                c256=("w_ukv", "w_uq"))
_SMALL = ("ffn1_norm", "mix_norm", "q_norm", "kv_norm", "ffn2_norm", "ple_norm")
_ORDER = ("ffn1_norm", "ffn1_w_gu", "ffn1_w_down", "mix_norm", "w_in", "conv_w", "w_conv_out", "q_norm", "kv_norm",
          "w_uq", "w_ukv", "w_mla_out", "w_o", "ffn2_norm", "ffn2_w_gu", "ffn2_w_down", "ple_norm", "w_ple_gate",
          "w_ple_proj", "final_norm")


def _class_width(wts, cls):
    return HEAD_SLOT if cls == "c256" else wts[_MEMBERS[cls][0]].shape[2]


def _pack_rows(vecs, width):
    flat = jnp.concatenate([a.reshape(-1) for a in vecs])
    n = flat.shape[0]
    rows = -(-n // width)
    rows = -(-rows // SUBLANES) * SUBLANES
    flat = jnp.pad(flat, (0, rows * width - n))
    offs, o = [], 0
    for a in vecs:
        offs.append(o)
        o += a.size
    return flat.reshape(rows, width), offs


def _unpack_rows(packed, vecs, offs):
    flat = packed.reshape(-1)
    return [flat[o:o + a.size].reshape(a.shape) for a, o in zip(vecs, offs)]


def _train(x, p, positions, target, gathered, packs, small_w, final_norm, update):
    cs = _rope_tables(positions)
    L = len(small_w)
    h = x
    saved = []
    if packs is not None:
        gathered = [dict(zip(CLASSES, _all_gather(packs, 0, [], 0)))]
    for l in range(L):
        h, s = _layer_fwd(h, p[l], cs, gathered[l], small_w[l])
        saved.append(s)
        if packs is not None and l + 1 < L:
            gathered.append(dict(zip(CLASSES, _all_gather(packs, l + 1, [s["n1"]], l + 1))))
    dh, loss, d_final = _final_loss(h, final_norm, target)
    grads, smalls = [None] * L, [None] * L
    exchanged = None
    landing = None

    def second_stage(after):
        l, gs, as_ = exchanged
        owns, ts = _rs_add_chip(gs, as_, [after])
        return l, owns, _rs_ici(ts, l, 2 * L + l)

    for l in reversed(range(L)):
        dh, part, small = _layer_bwd_late(dh, saved[l], gathered[l], small_w[l], [])
        pin = []
        if exchanged is not None:
            landing = second_stage(dh)
            pin = [landing[1][0]]
        dh, part, small = _layer_bwd_mixer(dh, part, small, saved[l], cs, gathered[l], small_w[l], pin)
        pin = [update(*landing)] if exchanged is not None else []
        dh, g, smalls[l] = _layer_bwd_first(dh, part, small, saved[l], gathered[l], small_w[l], pin)
        if update is not None:
            gs = [g[cls] for cls in CLASSES]
            exchanged = (l, gs, _rs_d2d(gs, l, L + l))
        else:
            grads[l] = g
    if update is not None:
        update(*second_stage(dh))
    return loss[0, 0], dh, grads, smalls, d_final


def kernel(x, p, positions, ffn1_norm, ffn1_w_gu, ffn1_w_down, mix_norm, w_in, conv_w, w_conv_out, q_norm, kv_norm, w_uq, w_ukv, w_mla_out, w_o, ffn2_norm, ffn2_w_gu, ffn2_w_down, ple_norm, w_ple_gate, w_ple_proj, final_norm, loss_target, m_ffn1_norm, m_ffn1_w_gu, m_ffn1_w_down, m_mix_norm, m_w_in, m_conv_w, m_w_conv_out, m_q_norm, m_kv_norm, m_w_uq, m_w_ukv, m_w_mla_out, m_w_o, m_ffn2_norm, m_ffn2_w_gu, m_ffn2_w_down, m_ple_norm, m_w_ple_gate, m_w_ple_proj, m_final_norm, v_ffn1_norm, v_ffn1_w_gu, v_ffn1_w_down, v_mix_norm, v_w_in, v_conv_w, v_w_conv_out, v_q_norm, v_kv_norm, v_w_uq, v_w_ukv, v_w_mla_out, v_w_o, v_ffn2_norm, v_ffn2_w_gu, v_ffn2_w_down, v_ple_norm, v_w_ple_gate, v_w_ple_proj, v_final_norm):
    args = dict(locals())
    wts = {n: args[n] for n in _ORDER}
    L = w_in.shape[0]
    dev = 4 * lax.axis_index("x") + 2 * lax.axis_index("y") + lax.axis_index("c")

    packs = [_pack([wts[n] for n in _MEMBERS[cls]], _class_width(wts, cls)) for cls in CLASSES]
    cw = conv_w.shape[2]
    conv_full = lax.dynamic_update_slice(jnp.zeros((L, 3, N_DEV * cw), F32), conv_w, (0, 0, dev * cw))
    conv_packed, conv_offs = _pack_rows([conv_full], FLAT_COLS)
    conv_full = _unpack_rows(_all_reduce_small(conv_packed), [conv_full], conv_offs)[0]
    small_w = [dict({n: wts[n][l][None, :] for n in _SMALL}, conv_w=conv_full[l]) for l in range(L)]

    done = {}

    def update(l, owns, bs):
        for q, cls in enumerate(CLASSES):
            off = 0
            rows = [wts[n].shape[1] for n in _MEMBERS[cls]]
            tr = _tile(math.gcd(*rows), 256, BF16_ROWS)
            for n, r in zip(_MEMBERS[cls], rows):
                done[n] = _adamw_reduced(wts[n], args["m_" + n], args["v_" + n], owns[q], bs[q], off, tr, l,
                                         done.get(n), [])
                off += r
        return done[_MEMBERS[CLASSES[-1]][-1]][0]

    loss_dev, grad_x, _, smalls, d_final = _train(x[0], p[:, 0], positions[0], loss_target[0], None, packs, small_w,
                                                  final_norm[None, :], update)
    loss = lax.psum(loss_dev, ("x", "y", "c"))

    small = [jnp.stack([smalls[l][n][0] for l in range(L)]) for n in _SMALL]
    small += [jnp.stack([smalls[l]["conv_w"] for l in range(L)]), d_final[0]]
    packed, offs = _pack_rows(small, FLAT_COLS)
    small = _unpack_rows(_all_reduce_small(packed), small, offs)
    grad = dict(zip(_SMALL, small))
    grad["conv_w"] = lax.dynamic_slice(small[len(_SMALL)], (0, 0, dev * cw), (L, 3, cw))
    grad["final_norm"] = small[-1]

    deltas, new_m, new_v = {}, {}, {}
    for n, (g, d, nm, nv) in done.items():
        grad[n], deltas[n], new_m[n], new_v[n] = g, d, nm, nv
    for n in _SMALL + ("conv_w", "final_norm"):
        w3 = wts[n].reshape((1,) * (3 - wts[n].ndim) + wts[n].shape)
        d, nm, nv = _adamw(w3, grad[n].reshape(w3.shape), args["m_" + n].reshape(w3.shape),
                           args["v_" + n].reshape(w3.shape))
        deltas[n], new_m[n], new_v[n] = (a.reshape(wts[n].shape) for a in (d, nm, nv))
    return (loss, grad_x[None], *[grad[n] for n in _ORDER], *[deltas[n] for n in _ORDER],
            *[new_m[n] for n in _ORDER], *[new_v[n] for n in _ORDER])
```

```python
import math

import jax
import jax.numpy as jnp
from jax import lax
from jax.experimental import pallas as pl
from jax.experimental.pallas import tpu as pltpu
from jax.experimental.pallas import tpu_sc as plsc

F32 = jnp.float32
BF16 = jnp.bfloat16

CHUNK = 64
NOPE = 128
ROPE = 64
VDIM = 128
ROPE_THETA = 10000.0
EPS = 1e-6
ATTN_SCALE = (NOPE + ROPE) ** -0.5
SCORE_SCALE = ATTN_SCALE * math.log2(math.e)
ADAM_LR = 0.001
ADAM_B1 = 0.9
ADAM_B2 = 0.999
ADAM_EPS = 1e-08
ADAM_WD = 0.01
ADAM_STEP = 10

LANES = 128
SUBLANES = 8
BF16_ROWS = 16
V7X_VMEM_BYTES = 64 * 1024 * 1024
VMEM_LIMIT = V7X_VMEM_BYTES * 7 // 8
HEAD_SLOT = 2 * LANES
N_DEV = 8
FLAT_COLS = 1024
CLASSES = ("gu", "dn", "sq", "win", "c128", "c256")

NT = (((1,), (1,)), ((), ()))
MESH = pl.DeviceIdType.MESH
ANY = pl.BlockSpec(memory_space=pl.ANY)


def _dot(a, b):
    return jnp.dot(a, b, preferred_element_type=F32)


def _dot_nt(a, b):
    return lax.dot_general(a, b, NT, preferred_element_type=F32)


def _sig(x):
    return 1.0 / (1.0 + jnp.exp(-x))


def _tile(n, pref, unit):
    if n <= pref:
        return n
    t = (pref // unit) * unit
    while t >= unit:
        if n % t == 0:
            return t
        t -= unit
    return n


def _call(body, name, grid, in_specs, out_specs, out_shape, scratch=(), aliases=None):
    return pl.pallas_call(
        body,
        name=name,
        grid=grid,
        in_specs=in_specs,
        out_specs=out_specs,
        out_shape=out_shape,
        scratch_shapes=list(scratch),
        input_output_aliases=aliases or {},
        compiler_params=pltpu.CompilerParams(
            dimension_semantics=("arbitrary",) * len(grid), vmem_limit_bytes=VMEM_LIMIT
        ),
    )


def _sds(shape, dtype):
    return jax.ShapeDtypeStruct(shape, dtype)


def _rms_fwd(x, gain):
    rstd = lax.rsqrt(jnp.mean(x * x, axis=-1, keepdims=True) + EPS)
    return x * rstd * gain, rstd


def _rms_bwd(dn, x, rstd, gain):
    xhat = x * rstd
    dgy = dn * gain
    dx = rstd * (dgy - xhat * jnp.mean(dgy * xhat, axis=-1, keepdims=True))
    return dx, jnp.sum(dn * xhat, axis=0, keepdims=True)


def _rows(tm, w):
    return pl.BlockSpec((tm, w), lambda i: (i, 0))


def _whole(a):
    nd = a.ndim
    return pl.BlockSpec(a.shape, lambda i: (0,) * nd)


def _slab(buf, rows, index):
    return pl.BlockSpec((N_DEV, rows, buf.shape[2]), lambda i: (0, index, 0))


def _cat_slots(w):
    return jnp.concatenate([w[d] for d in range(N_DEV)], axis=1)


def _ffn_up(h, gain, gu_w, which):
    S, D = h.shape
    c = gu_w.shape[2]
    tm = _tile(S, 256, SUBLANES)
    nb = N_DEV // 2

    def body(h_ref, gain_ref, w_ref, gu_ref, a_ref, n_ref, r_ref):
        n32, rstd = _rms_fwd(h_ref[...], gain_ref[...])
        n = n32.astype(BF16)
        n_ref[...] = n
        r_ref[...] = rstd
        for d in range(nb):
            g = _dot(n, w_ref[d])
            u = _dot(n, w_ref[nb + d])
            gu_ref[d] = g.astype(BF16)
            gu_ref[nb + d] = u.astype(BF16)
            a_ref[d] = (g * _sig(g) * u).astype(BF16)

    return _call(
        body, "ffn_up", (S // tm,),
        [_rows(tm, D), _whole(gain), _slab(gu_w, D, which)],
        [pl.BlockSpec((N_DEV, tm, c), lambda i: (0, i, 0)), pl.BlockSpec((nb, tm, c), lambda i: (0, i, 0)),
         _rows(tm, D), _rows(tm, 1)],
        [_sds((N_DEV, S, c), BF16), _sds((nb, S, c), BF16), _sds((S, D), BF16), _sds((S, 1), F32)],
    )(h, gain, gu_w)


def _down_weight(w_ref, d, c):
    return w_ref[2 * d:2 * d + 2].reshape(c, w_ref.shape[2])


def _ffn_down(a, dn_w, which, h):
    nb, S, c = a.shape
    D = h.shape[1]
    tm = _tile(S, 512, SUBLANES)

    def body(a_ref, w_ref, h_ref, o_ref):
        acc = _dot(a_ref[0], _down_weight(w_ref, 0, c))
        for d in range(1, nb):
            acc = acc + _dot(a_ref[d], _down_weight(w_ref, d, c))
        o_ref[...] = h_ref[...] + 0.5 * acc

    return _call(
        body, "ffn_down", (S // tm,),
        [pl.BlockSpec((nb, tm, c), lambda i: (0, i, 0)), _slab(dn_w, c // 2, which), _rows(tm, D)],
        _rows(tm, D),
        _sds((S, D), F32),
    )(a, dn_w, h)


def _win_segments(C, QL, KVL, D):
    o1, o2 = 3 * C, 3 * C + QL + KVL + ROPE
    return [("bcv", k, k * C, (k + 1) * C) for k in range(3)] + [("qkr", None, o1, o2), ("gg", None, o2, o2 + 2 * D)]


def _win_pieces(segments, cw):
    out = []
    for tgt, lead, a, b in segments:
        for d in range(N_DEV):
            lo, hi = max(a, d * cw), min(b, (d + 1) * cw)
            if lo < hi:
                out.append((tgt, lead, d, (lo - d * cw, hi - d * cw), (lo - a, hi - a)))
    return out


def _win_split(win_w, C, QL, KVL):
    _, D, cw = win_w.shape
    WQ = QL + KVL + LANES
    pieces = _win_pieces(_win_segments(C, QL, KVL, D), cw)
    tr = _tile(D, 256, BF16_ROWS)

    def body(w_ref, bcv_ref, qkr_ref, gg_ref):
        tgt = dict(bcv=bcv_ref, qkr=qkr_ref, gg=gg_ref)
        qkr_ref[:, QL + KVL + ROPE:] = jnp.zeros((tr, LANES - ROPE), BF16)
        for name, lead, d, (s0, s1), (t0, t1) in pieces:
            v = w_ref[d, :, s0:s1]
            if lead is None:
                tgt[name][:, t0:t1] = v
            else:
                tgt[name][lead, :, t0:t1] = v

    return _call(
        body, "win_split", (D // tr,),
        [pl.BlockSpec((N_DEV, tr, cw), lambda i: (0, i, 0))],
        [pl.BlockSpec((3, tr, C), lambda i: (0, i, 0)), _rows(tr, WQ), _rows(tr, 2 * D)],
        [_sds((3, D, C), BF16), _sds((D, WQ), BF16), _sds((D, 2 * D), BF16)],
    )(win_w)


def _win_merge(d_bcv, d_qkr, d_gg, cw):
    _, D, C = d_bcv.shape
    WQ = d_qkr.shape[1]
    QL_KVL = WQ - LANES
    o1 = 3 * C
    segments = [("bcv", k, k * C, (k + 1) * C) for k in range(3)]
    segments += [("qkr", None, o1, o1 + QL_KVL + ROPE), ("gg", None, o1 + QL_KVL + ROPE, o1 + QL_KVL + ROPE + 2 * D)]
    pieces = _win_pieces(segments, cw)
    tr = _tile(D, 256, BF16_ROWS)

    def body(bcv_ref, qkr_ref, gg_ref, o_ref):
        src = dict(bcv=bcv_ref, qkr=qkr_ref, gg=gg_ref)
        for name, lead, d, (s0, s1), (t0, t1) in pieces:
            v = src[name][:, t0:t1] if lead is None else src[name][lead, :, t0:t1]
            o_ref[d, :, s0:s1] = v.astype(BF16)

    return _call(
        body, "win_merge", (D // tr,),
        [pl.BlockSpec((3, tr, C), lambda i: (0, i, 0)), _rows(tr, WQ), _rows(tr, 2 * D)],
        pl.BlockSpec((N_DEV, tr, cw), lambda i: (0, i, 0)),
        _sds((N_DEV, D, cw), BF16),
    )(d_bcv, d_qkr, d_gg)


def _mix_in(h, gain, w_bcv, w_qkr, w_gg):
    S, D = h.shape
    C = w_bcv.shape[2]
    tm = _tile(S, 256, SUBLANES)

    def body(h_ref, gain_ref, w1, w2, w3, o1, o2, o3, n_ref, r_ref):
        n32, rstd = _rms_fwd(h_ref[...], gain_ref[...])
        n = n32.astype(BF16)
        n_ref[...] = n
        r_ref[...] = rstd
        for k in range(3):
            o1[k] = _dot(n, w1[k])
        o2[...] = _dot(n, w2[...])
        o3[...] = _dot(n, w3[...])

    return _call(
        body, "mix_in", (S // tm,),
        [_rows(tm, D), _whole(gain), _whole(w_bcv), _whole(w_qkr), _whole(w_gg)],
        [pl.BlockSpec((3, tm, C), lambda i: (0, i, 0)), _rows(tm, w_qkr.shape[1]), _rows(tm, 2 * D),
         _rows(tm, D), _rows(tm, 1)],
        [_sds((3, S, C), F32), _sds((S, w_qkr.shape[1]), F32), _sds((S, 2 * D), F32), _sds((S, D), BF16),
         _sds((S, 1), F32)],
    )(h, gain, w_bcv, w_qkr, w_gg)


def _conv_taps(zc):
    rows = lax.broadcasted_iota(jnp.int32, zc.shape, 0)
    z1 = jnp.where(rows >= 1, pltpu.roll(zc, 1, 0), 0.0)
    z2 = jnp.where(rows >= 2, pltpu.roll(zc, 2, 0), 0.0)
    return z1, z2


def _conv_fwd(z_bcv, conv_w):
    _, S, C = z_bcv.shape

    def body(z_ref, w_ref, o_ref):
        w = w_ref[...]
        zc = z_ref[1] * z_ref[2]
        z1, z2 = _conv_taps(zc)
        y = w[0:1] * z2 + w[1:2] * z1 + w[2:3] * zc
        o_ref[...] = (z_ref[0] * y).astype(BF16)

    return _call(
        body, "conv_fwd", (C // LANES,),
        [pl.BlockSpec((3, S, LANES), lambda j: (0, 0, j)), pl.BlockSpec((3, LANES), lambda j: (0, j))],
        pl.BlockSpec((S, LANES), lambda j: (0, j)),
        _sds((S, C), BF16),
    )(z_bcv, conv_w)


def _rope(x, cs, half):
    c, s1, s2 = cs[:, :LANES], cs[:, LANES:2 * LANES], cs[:, 2 * LANES:]
    return x * c + pltpu.roll(x, LANES - half, 1) * s1 + pltpu.roll(x, half, 1) * s2


def _unrope(d, cs, half):
    c, s1, s2 = cs[:, :LANES], cs[:, LANES:2 * LANES], cs[:, 2 * LANES:]
    return d * c + pltpu.roll(d * s1, half, 1) + pltpu.roll(d * s2, LANES - half, 1)


def _mla_prep(z_qkr, gq, gkv, cs, c256_w):
    S = z_qkr.shape[0]
    QL, KVL = gq.shape[1], gkv.shape[1]
    H = N_DEV
    tm = _tile(S, 256, SUBLANES)
    half = ROPE // 2

    def body(z_ref, gq_ref, gkv_ref, cs_ref, w_ref, q_ref, k_ref, v_ref, qn_ref, kvn_ref, rq_ref, rkv_ref):
        z = z_ref[...]
        cs_t = cs_ref[...]
        qn32, rq = _rms_fwd(z[:, :QL], gq_ref[...])
        kvn32, rkv = _rms_fwd(z[:, QL:QL + KVL], gkv_ref[...])
        qn = qn32.astype(BF16)
        kvn = kvn32.astype(BF16)
        qn_ref[...] = qn
        kvn_ref[...] = kvn
        rq_ref[...] = rq
        rkv_ref[...] = rkv
        krope = _rope(z[:, QL + KVL:], cs_t, half).astype(BF16)
        for h in range(H):
            lo, mid, hi = h * HEAD_SLOT, h * HEAD_SLOT + LANES, (h + 1) * HEAD_SLOT
            q = _dot(qn, w_ref[h, KVL:KVL + QL, :])
            kv = _dot(kvn, w_ref[h, 0:KVL, :])
            q_ref[:, lo:mid] = q[:, :LANES].astype(BF16)
            q_ref[:, mid:hi] = _rope(q[:, LANES:], cs_t, half).astype(BF16)
            k_ref[:, lo:mid] = kv[:, :LANES].astype(BF16)
            k_ref[:, mid:hi] = krope
            v_ref[:, h * VDIM:(h + 1) * VDIM] = kv[:, LANES:].astype(BF16)

    return _call(
        body, "mla_prep", (S // tm,),
        [_rows(tm, z_qkr.shape[1]), _whole(gq), _whole(gkv), _rows(tm, 3 * LANES), _whole(c256_w)],
        [_rows(tm, H * HEAD_SLOT), _rows(tm, H * HEAD_SLOT), _rows(tm, H * VDIM), _rows(tm, QL), _rows(tm, KVL),
         _rows(tm, 1), _rows(tm, 1)],
        [_sds((S, H * HEAD_SLOT), BF16), _sds((S, H * HEAD_SLOT), BF16), _sds((S, H * VDIM), BF16),
         _sds((S, QL), BF16), _sds((S, KVL), BF16), _sds((S, 1), F32), _sds((S, 1), F32)],
    )(z_qkr, gq, gkv, cs, c256_w)


def _chunk_mask(t):
    shift = CHUNK.bit_length() - 1
    krow = lax.broadcasted_iota(jnp.int32, (t, t), 0) >> shift
    qcol = lax.broadcasted_iota(jnp.int32, (t, t), 1) >> shift
    return krow <= qcol


def _attn_fwd(q, k, v, H):
    S = q.shape[0]
    t = _tile(S, 512, CHUNK)
    nq = S // t

    def body(q_ref, k_ref, v_ref, o_ref, lse_ref, vt_ref):
        qi = pl.program_id(1)

        @pl.when(qi == 0)
        def _():
            vt_ref[...] = v_ref[...].T

        qv = q_ref[...]

        def block(kj, carry, masked):
            m, l, acc = carry
            off = pl.multiple_of(kj * t, t)
            s = _dot_nt(k_ref[pl.ds(off, t), :], qv) * SCORE_SCALE
            if masked:
                s = jnp.where(_chunk_mask(t), s, -1e30)
            m_new = jnp.maximum(m, jnp.max(s, axis=0, keepdims=True))
            alpha = jnp.exp2(m - m_new)
            p = jnp.exp2(s - m_new)
            l = alpha * l + jnp.sum(p, axis=0, keepdims=True)
            acc = alpha * acc + _dot(vt_ref[:, pl.ds(off, t)], p.astype(BF16))
            return m_new, l, acc

        init = (jnp.full((1, t), -1e30, F32), jnp.zeros((1, t), F32), jnp.zeros((VDIM, t), F32))
        carry = lax.fori_loop(0, qi, lambda kj, c: block(kj, c, False), init)
        m, l, acc = block(qi, carry, True)
        o_ref[...] = (acc / l).T.astype(BF16)
        lse_ref[0] = jnp.broadcast_to(m + jnp.log2(l), (SUBLANES, t))

    return _call(
        body, "attn_fwd", (H, nq),
        [pl.BlockSpec((t, HEAD_SLOT), lambda h, i: (i, h)), pl.BlockSpec((S, HEAD_SLOT), lambda h, i: (0, h)),
         pl.BlockSpec((S, VDIM), lambda h, i: (0, h))],
        [pl.BlockSpec((t, VDIM), lambda h, i: (i, h)), pl.BlockSpec((1, SUBLANES, t), lambda h, i: (h, 0, i))],
        [_sds((S, H * VDIM), BF16), _sds((H, SUBLANES, S), F32)],
        [pltpu.VMEM((VDIM, S), BF16)],
    )(q, k, v)


def _merge_wo(o, by, z_gg, h, sq_w, c128_w):
    S, D = h.shape
    C = by.shape[1]
    r = sq_w.shape[1] // 3
    tm = _tile(S, 512, SUBLANES)

    def body(o_ref, by_ref, gg_ref, h_ref, wmo_ref, wo_ref, wco_ref, h2_ref, mg_ref, yc_ref, ym_ref):
        ymla = _dot(o_ref[...], wmo_ref[...].reshape(N_DEV * r, D))
        yconv = _dot(by_ref[...], _cat_slots(wco_ref))
        gg = gg_ref[...]
        merged = (_sig(gg[:, :D]) * yconv + _sig(gg[:, D:]) * ymla).astype(BF16)
        mg_ref[...] = merged
        yc_ref[...] = yconv.astype(BF16)
        ym_ref[...] = ymla.astype(BF16)
        h2_ref[...] = h_ref[...] + _dot(merged, wo_ref[...].reshape(N_DEV * r, D))

    return _call(
        body, "merge_wo", (S // tm,),
        [_rows(tm, o.shape[1]), _rows(tm, C), _rows(tm, 2 * D), _rows(tm, D), _slab(sq_w, r, 0), _slab(sq_w, r, 1),
         _slab(c128_w, C, 0)],
        [_rows(tm, D)] * 4,
        [_sds((S, D), F32)] + [_sds((S, D), BF16)] * 3,
    )(o, by, z_gg, h, sq_w, sq_w, c128_w)


def _ple_fwd(h, gain, p, sq_w, c128_w, C):
    S, D = h.shape
    P = p.shape[1]
    r = sq_w.shape[1] // 3
    tm = _tile(S, 512, SUBLANES)

    def body(h_ref, gain_ref, p_ref, wpg_ref, wpp_ref, o_ref, pre_ref, pp_ref, n_ref, r_ref):
        x = h_ref[...]
        n32, rstd = _rms_fwd(x, gain_ref[...])
        n = n32.astype(BF16)
        n_ref[...] = n
        r_ref[...] = rstd
        pre = _dot(n, wpg_ref[...].reshape(N_DEV * r, D))
        pp = _dot(p_ref[...].astype(BF16), _cat_slots(wpp_ref))
        pre_ref[...] = pre
        pp_ref[...] = pp
        o_ref[...] = x + _sig(pre) * pp

    return _call(
        body, "ple_fwd", (S // tm,),
        [_rows(tm, D), _whole(gain), _rows(tm, P), _slab(sq_w, r, 2), _slab(c128_w, P, C // P)],
        [_rows(tm, D), _rows(tm, D), _rows(tm, D), _rows(tm, D), _rows(tm, 1)],
        [_sds((S, D), F32)] * 3 + [_sds((S, D), BF16), _sds((S, 1), F32)],
    )(h, gain, p, sq_w, c128_w)


def _final_loss(h, gain, target):
    S, D = h.shape
    tm = _tile(S, 512, SUBLANES)

    def body(h_ref, gain_ref, t_ref, dh_ref, loss_ref, dg_ref):
        @pl.when(pl.program_id(0) == 0)
        def _():
            loss_ref[...] = jnp.zeros_like(loss_ref)
            dg_ref[...] = jnp.zeros_like(dg_ref)

        x = h_ref[...]
        gain_v = gain_ref[...]
        y, rstd = _rms_fwd(x, gain_v)
        err = y - t_ref[...]
        loss_ref[...] += 0.5 * jnp.sum(jnp.mean(err * err, axis=-1, keepdims=True))
        dx, dgain = _rms_bwd(err * (1.0 / D), x, rstd, gain_v)
        dh_ref[...] = dx
        dg_ref[...] += dgain

    return _call(
        body, "final_loss", (S // tm,),
        [_rows(tm, D), _whole(gain), _rows(tm, D)],
        [_rows(tm, D), pl.BlockSpec((1, LANES), lambda i: (0, 0)), pl.BlockSpec((1, D), lambda i: (0, 0))],
        [_sds((S, D), F32), _sds((1, LANES), F32), _sds((1, D), F32)],
    )(h, gain, target)


def _tn_call(body, name, grid, in_specs, out_spec, out_shape, scratch, operands, prev):
    n = len(operands)
    if prev is None:
        return _call(body, name, grid, in_specs, out_spec, out_shape, scratch)(*operands)
    assert prev.shape == out_shape.shape and prev.dtype == out_shape.dtype

    def wrapped(*refs):
        body(*refs[:n], *refs[n + 1:])

    return _call(wrapped, name, grid, in_specs + [ANY], out_spec, out_shape, scratch, {n: 0})(*operands, prev)


def _transposed(x_ref, xt_ref, first):
    @pl.when(first)
    def _():
        xt_ref[...] = x_ref[...].astype(BF16).T


def _tn_slots(x, dy, prev, rows_total, row_off):
    S, K = x.shape
    B, _, c = dy.shape
    tk = _tile(K, 512, LANES)

    def body(x_ref, dy_ref, o_ref, xt_ref):
        _transposed(x_ref, xt_ref, pl.program_id(1) == 0)
        o_ref[0] = _dot(xt_ref[...], dy_ref[0]).astype(BF16)

    return _tn_call(
        body, "tn_slots", (K // tk, B),
        [pl.BlockSpec((S, tk), lambda i, b: (0, i)), pl.BlockSpec((1, S, c), lambda i, b: (b, 0, 0))],
        pl.BlockSpec((1, tk, c), lambda i, b: (b, row_off // tk + i, 0)),
        _sds((B, rows_total, c), BF16), [pltpu.VMEM((tk, S), BF16)], [x, dy], prev)


def _tn_plain(x, dy, out_dtype=F32):
    S, K = x.shape
    B, _, c = dy.shape
    tk = _tile(K, 512, LANES)
    tn = _tile(c, 1024, LANES)

    def body(x_ref, dy_ref, o_ref, xt_ref):
        _transposed(x_ref, xt_ref, (pl.program_id(1) == 0) & (pl.program_id(2) == 0))
        o_ref[0] = _dot(xt_ref[...], dy_ref[0]).astype(out_dtype)

    return _call(
        body, "tn_plain", (K // tk, B, c // tn),
        [pl.BlockSpec((S, tk), lambda i, b, j: (0, i)), pl.BlockSpec((1, S, tn), lambda i, b, j: (b, 0, j))],
        pl.BlockSpec((1, tk, tn), lambda i, b, j: (b, i, j)),
        _sds((B, K, c), out_dtype), [pltpu.VMEM((tk, S), BF16)],
    )(x, dy)


def _tn_down(a, dh, prev, rows_total, which):
    nb, S, c = a.shape
    D = dh.shape[1]
    r = c // 2
    tn = _tile(D, 512, LANES)

    def body(a_ref, dh_ref, o_ref, xt_ref):
        _transposed(a_ref.at[0], xt_ref, pl.program_id(1) == 0)
        g = 0.5 * _dot(xt_ref[...], dh_ref[...].astype(BF16))
        o_ref[...] = g.astype(BF16).reshape(2, r, tn)

    return _tn_call(
        body, "tn_down", (nb, D // tn),
        [pl.BlockSpec((1, S, c), lambda i, j: (i, 0, 0)), pl.BlockSpec((S, tn), lambda i, j: (0, j))],
        pl.BlockSpec((2, r, tn), lambda i, j: (i, which, j)),
        _sds((N_DEV, rows_total, D), BF16), [pltpu.VMEM((c, S), BF16)], [a, dh], prev)


def _tn_square(x, dy, prev, rows_total, member):
    S, K = x.shape
    N = dy.shape[1]
    r = K // N_DEV
    tk = _tile(K, 512, r)
    tn = _tile(N, 512, LANES)

    def body(x_ref, dy_ref, o_ref, xt_ref):
        _transposed(x_ref, xt_ref, pl.program_id(1) == 0)
        g = _dot(xt_ref[...], dy_ref[...].astype(BF16))
        o_ref[...] = g.astype(BF16).reshape(tk // r, r, tn)

    return _tn_call(
        body, "tn_square", (K // tk, N // tn),
        [pl.BlockSpec((S, tk), lambda i, j: (0, i)), pl.BlockSpec((S, tn), lambda i, j: (0, j))],
        pl.BlockSpec((tk // r, r, tn), lambda i, j: (i, member, j)),
        _sds((N_DEV, rows_total, N), BF16), [pltpu.VMEM((tk, S), BF16)], [x, dy], prev)


def _tn_cols(x, dy, prev, rows_total, row_block):
    S, K = x.shape
    N = dy.shape[1]
    cw = N // N_DEV

    def body(x_ref, dy_ref, o_ref):
        g = _dot(x_ref[...].astype(BF16).T, dy_ref[...])
        for d in range(N_DEV):
            o_ref[d] = g[:, d * cw:(d + 1) * cw].astype(BF16)

    return _tn_call(
        body, "tn_cols", (1,),
        [pl.BlockSpec((S, K), lambda i: (0, 0)), pl.BlockSpec((S, N), lambda i: (0, 0))],
        pl.BlockSpec((N_DEV, K, cw), lambda i: (0, row_block, 0)),
        _sds((N_DEV, rows_total, cw), BF16), [], [x, dy], prev)


def _tn_heads(qn, kvn, dqp, dkv):
    S, QL = qn.shape
    KVL = kvn.shape[1]

    def body(qn_ref, kvn_ref, dq_ref, dkv_ref, o_ref):
        o_ref[0, 0:KVL, :] = _dot(kvn_ref[...].T, dkv_ref[...]).astype(BF16)
        o_ref[0, KVL:KVL + QL, :] = _dot(qn_ref[...].T, dq_ref[...]).astype(BF16)

    head = pl.BlockSpec((S, HEAD_SLOT), lambda h: (0, h))
    return _call(
        body, "tn_heads", (N_DEV,),
        [pl.BlockSpec((S, QL), lambda h: (0, 0)), pl.BlockSpec((S, KVL), lambda h: (0, 0)), head, head],
        pl.BlockSpec((1, KVL + QL, HEAD_SLOT), lambda h: (h, 0, 0)),
        _sds((N_DEV, KVL + QL, HEAD_SLOT), BF16),
    )(qn, kvn, dqp, dkv)


def _ple_bwd(dh, pre, pp, h, rstd, gain, sq_w, after):
    S, D = h.shape
    r = sq_w.shape[1] // 3
    tm = _tile(S, 512, SUBLANES)

    def body(dh_ref, pre_ref, pp_ref, h_ref, r_ref, gain_ref, wpg_ref, *rest):
        o_ref, dpre_ref, dpp_ref, dg_ref = rest[len(after):]

        @pl.when(pl.program_id(0) == 0)
        def _():
            dg_ref[...] = jnp.zeros_like(dg_ref)

        d = dh_ref[...]
        gate = _sig(pre_ref[...])
        dpre = (d * pp_ref[...] * gate * (1.0 - gate)).astype(BF16)
        dpre_ref[...] = dpre
        dpp_ref[...] = (d * gate).astype(BF16)
        dn = _dot_nt(dpre, wpg_ref[...].reshape(N_DEV * r, D))
        dx, dgain = _rms_bwd(dn, h_ref[...], r_ref[...], gain_ref[...])
        o_ref[...] = d + dx
        dg_ref[...] += dgain

    return _call(
        body, "ple_bwd", (S // tm,),
        [_rows(tm, D), _rows(tm, D), _rows(tm, D), _rows(tm, D), _rows(tm, 1), _whole(gain), _slab(sq_w, r, 2)]
        + [ANY] * len(after),
        [_rows(tm, D), _rows(tm, D), _rows(tm, D), pl.BlockSpec((1, D), lambda i: (0, 0))],
        [_sds((S, D), F32), _sds((S, D), BF16), _sds((S, D), BF16), _sds((1, D), F32)],
    )(dh, pre, pp, h, rstd, gain, sq_w, *after)


def _ffn_bwd_act(dh, dn_w, which, gu, after=()):
    S, D = dh.shape
    _, _, c = gu.shape
    nb = N_DEV // 2
    tm = _tile(S, 256, SUBLANES)

    def body(dh_ref, w_ref, gu_ref, *rest):
        dgu_ref = rest[len(after)]
        dhb = dh_ref[...].astype(BF16)
        for d in range(nb):
            da = 0.5 * _dot_nt(dhb, _down_weight(w_ref, d, c))
            g = gu_ref[d].astype(F32)
            u = gu_ref[nb + d].astype(F32)
            sg = _sig(g)
            dgu_ref[d] = (da * u * sg * (1.0 + g * (1.0 - sg))).astype(BF16)
            dgu_ref[nb + d] = (da * g * sg).astype(BF16)

    act = pl.BlockSpec((N_DEV, tm, c), lambda i: (0, i, 0))
    return _call(
        body, "ffn_bwd_act", (S // tm,),
        [_rows(tm, D), _slab(dn_w, c // 2, which), act] + [ANY] * len(after),
        act,
        _sds((N_DEV, S, c), BF16),
    )(dh, dn_w, gu, *after)


def _ffn_bwd_in(dgu, gu_w, which, h, rstd, gain, dh):
    S, D = h.shape
    c = dgu.shape[2]
    tm = _tile(S, 256, SUBLANES)

    def body(dgu_ref, w_ref, h_ref, r_ref, gain_ref, dh_ref, o_ref, dgain_ref):
        @pl.when(pl.program_id(0) == 0)
        def _():
            dgain_ref[...] = jnp.zeros_like(dgain_ref)

        dn = _dot_nt(dgu_ref[0], w_ref[0])
        for d in range(1, N_DEV):
            dn = dn + _dot_nt(dgu_ref[d], w_ref[d])
        dx, dgain = _rms_bwd(dn, h_ref[...], r_ref[...], gain_ref[...])
        o_ref[...] = dh_ref[...] + dx
        dgain_ref[...] += dgain

    return _call(
        body, "ffn_bwd_in", (S // tm,),
        [pl.BlockSpec((N_DEV, tm, c), lambda i: (0, i, 0)), _slab(gu_w, D, which), _rows(tm, D), _rows(tm, 1),
         _whole(gain), _rows(tm, D)],
        [_rows(tm, D), pl.BlockSpec((1, D), lambda i: (0, 0))],
        [_sds((S, D), F32), _sds((1, D), F32)],
    )(dgu, gu_w, h, rstd, gain, dh)


def _merge_bwd(dh, z_gg, yconv, ymla, sq_w, c128_w, C, after):
    S, D = dh.shape
    r = sq_w.shape[1] // 3
    HV = N_DEV * r
    tm = _tile(S, 512, SUBLANES)

    def body(dh_ref, gg_ref, yc_ref, ym_ref, wmo_ref, wo_ref, wco_ref, *rest):
        dgg_ref, dby_ref, do_ref, dyc_ref, dym_ref = rest[len(after):]
        dm = _dot_nt(dh_ref[...].astype(BF16), wo_ref[...].reshape(HV, D))
        gg = gg_ref[...]
        sgc = _sig(gg[:, :D])
        sgm = _sig(gg[:, D:])
        dyc = (dm * sgc).astype(BF16)
        dym = (dm * sgm).astype(BF16)
        dyc_ref[...] = dyc
        dym_ref[...] = dym
        dgg_ref[:, :D] = (dm * yc_ref[...].astype(F32) * sgc * (1.0 - sgc)).astype(BF16)
        dgg_ref[:, D:] = (dm * ym_ref[...].astype(F32) * sgm * (1.0 - sgm)).astype(BF16)
        dby_ref[...] = _dot_nt(dyc, _cat_slots(wco_ref))
        do_ref[...] = _dot_nt(dym, wmo_ref[...].reshape(HV, D)).astype(BF16)

    return _call(
        body, "merge_bwd", (S // tm,),
        [_rows(tm, D), _rows(tm, 2 * D), _rows(tm, D), _rows(tm, D), _slab(sq_w, r, 0), _slab(sq_w, r, 1),
         _slab(c128_w, C, 0)] + [ANY] * len(after),
        [_rows(tm, 2 * D), _rows(tm, C), _rows(tm, HV), _rows(tm, D), _rows(tm, D)],
        [_sds((S, 2 * D), BF16), _sds((S, C), F32), _sds((S, HV), BF16), _sds((S, D), BF16), _sds((S, D), BF16)],
    )(dh, z_gg, yconv, ymla, sq_w, sq_w, c128_w, *after)


def _conv_bwd(z_bcv, conv_w, dby):
    _, S, C = z_bcv.shape

    def body(z_ref, w_ref, dby_ref, dz_ref, dw_ref):
        w = w_ref[...]
        c = z_ref[1]
        v = z_ref[2]
        d = dby_ref[...]
        zc = c * v
        z1, z2 = _conv_taps(zc)
        y = w[0:1] * z2 + w[1:2] * z1 + w[2:3] * zc
        dz_ref[0] = (d * y).astype(BF16)
        dy = d * z_ref[0]
        rows = lax.broadcasted_iota(jnp.int32, dy.shape, 0)
        dy1 = jnp.where(rows < S - 1, pltpu.roll(dy, S - 1, 0), 0.0)
        dy2 = jnp.where(rows < S - 2, pltpu.roll(dy, S - 2, 0), 0.0)
        dzc = w[2:3] * dy + w[1:2] * dy1 + w[0:1] * dy2
        dz_ref[1] = (dzc * v).astype(BF16)
        dz_ref[2] = (dzc * c).astype(BF16)
        dw_ref[0:1, :] = jnp.sum(dy * z2, axis=0, keepdims=True)
        dw_ref[1:2, :] = jnp.sum(dy * z1, axis=0, keepdims=True)
        dw_ref[2:3, :] = jnp.sum(dy * zc, axis=0, keepdims=True)

    three = pl.BlockSpec((3, S, LANES), lambda j: (0, 0, j))
    wspec = pl.BlockSpec((3, LANES), lambda j: (0, j))
    return _call(
        body, "conv_bwd", (C // LANES,),
        [three, wspec, pl.BlockSpec((S, LANES), lambda j: (0, j))],
        [three, wspec],
        [_sds((3, S, C), BF16), _sds((3, C), F32)],
    )(z_bcv, conv_w, dby)


def _attn_delta(do, o, H):
    S = do.shape[0]
    t = _tile(S, 512, LANES)

    def body(do_ref, o_ref, dl_ref):
        prod = do_ref[...].astype(F32) * o_ref[...].astype(F32)
        dl_ref[0] = jnp.broadcast_to(jnp.sum(prod.T, axis=0, keepdims=True), (SUBLANES, t))

    blk = pl.BlockSpec((t, VDIM), lambda h, i: (i, h))
    return _call(body, "attn_delta", (H, S // t), [blk, blk],
                 pl.BlockSpec((1, SUBLANES, t), lambda h, i: (h, 0, i)), _sds((H, SUBLANES, S), F32))(do, o)


def _attn_bwd(q, k, v, do, lse, delta, H):
    S = q.shape[0]
    t = _tile(S, 512, CHUNK)
    nk = S // t

    def body(q_ref, k_ref, v_ref, do_ref, lse_ref, dl_ref, dq_ref, dk_ref, dv_ref, dqt_ref):
        kj = pl.program_id(1)

        @pl.when(kj == 0)
        def _():
            dqt_ref[...] = jnp.zeros_like(dqt_ref)

        kv = k_ref[...]
        vv = v_ref[...]
        kt = kv.T

        def block(qi, carry, masked):
            dk, dv = carry
            off = pl.multiple_of(qi * t, t)
            qv = q_ref[pl.ds(off, t), :]
            dov = do_ref[pl.ds(off, t), :]
            s = _dot_nt(kv, qv) * SCORE_SCALE
            if masked:
                s = jnp.where(_chunk_mask(t), s, -1e30)
            p = jnp.exp2(s - lse_ref[0, 0:1, pl.ds(off, t)])
            dp = _dot_nt(vv, dov)
            ds = (p * (dp - dl_ref[0, 0:1, pl.ds(off, t)]) * ATTN_SCALE).astype(BF16)
            dqt_ref[:, pl.ds(off, t)] += _dot(kt, ds)
            return dk + _dot(ds, qv), dv + _dot(p.astype(BF16), dov)

        init = (jnp.zeros((t, HEAD_SLOT), F32), jnp.zeros((t, VDIM), F32))
        carry = block(kj, init, True)
        dk, dv = lax.fori_loop(kj + 1, nk, lambda qi, c: block(qi, c, False), carry)
        dk_ref[...] = dk
        dv_ref[...] = dv.astype(BF16)

        @pl.when(kj == nk - 1)
        def _():
            dq_ref[...] = dqt_ref[...].T

    kspec = lambda w: pl.BlockSpec((t, w), lambda h, j: (j, h))
    qspec = lambda w: pl.BlockSpec((S, w), lambda h, j: (0, h))
    stat = pl.BlockSpec((1, SUBLANES, S), lambda h, j: (h, 0, 0))
    return _call(
        body, "attn_bwd", (H, nk),
        [qspec(HEAD_SLOT), kspec(HEAD_SLOT), kspec(VDIM), qspec(VDIM), stat, stat],
        [qspec(HEAD_SLOT), kspec(HEAD_SLOT), kspec(VDIM)],
        [_sds((S, H * HEAD_SLOT), F32), _sds((S, H * HEAD_SLOT), F32), _sds((S, H * VDIM), BF16)],
        [pltpu.VMEM((HEAD_SLOT, S), F32)],
    )(q, k, v, do, lse, delta)


def _mla_prep_bwd(dq, dk, dv, z_qkr, rq, rkv, gq, gkv, cs, c256_w):
    S = z_qkr.shape[0]
    QL, KVL = gq.shape[1], gkv.shape[1]
    H = N_DEV
    tm = _tile(S, 256, SUBLANES)
    half = ROPE // 2

    def body(dq_ref, dk_ref, dv_ref, z_ref, rq_ref, rkv_ref, gq_ref, gkv_ref, cs_ref, w_ref,
             dz_ref, dqp_ref, dkv_ref, dgq_ref, dgkv_ref):
        @pl.when(pl.program_id(0) == 0)
        def _():
            dgq_ref[...] = jnp.zeros_like(dgq_ref)
            dgkv_ref[...] = jnp.zeros_like(dgkv_ref)

        cs_t = cs_ref[...]
        dkr = jnp.zeros((tm, LANES), F32)
        dqn = jnp.zeros((tm, QL), F32)
        dkvn = jnp.zeros((tm, KVL), F32)
        for h in range(H):
            lo, mid, hi = h * HEAD_SLOT, h * HEAD_SLOT + LANES, (h + 1) * HEAD_SLOT
            dqp_ref[:, lo:mid] = dq_ref[:, lo:mid].astype(BF16)
            dqp_ref[:, mid:hi] = _unrope(dq_ref[:, mid:hi], cs_t, half).astype(BF16)
            dkv_ref[:, lo:mid] = dk_ref[:, lo:mid].astype(BF16)
            dkv_ref[:, mid:hi] = dv_ref[:, h * VDIM:(h + 1) * VDIM]
            dkr = dkr + dk_ref[:, mid:hi]
            dqn = dqn + _dot_nt(dqp_ref[:, lo:hi], w_ref[h, KVL:KVL + QL, :])
            dkvn = dkvn + _dot_nt(dkv_ref[:, lo:hi], w_ref[h, 0:KVL, :])
        z = z_ref[...]
        dqc, dgq = _rms_bwd(dqn, z[:, :QL], rq_ref[...], gq_ref[...])
        dkvc, dgkv = _rms_bwd(dkvn, z[:, QL:QL + KVL], rkv_ref[...], gkv_ref[...])
        dz_ref[:, :QL] = dqc.astype(BF16)
        dz_ref[:, QL:QL + KVL] = dkvc.astype(BF16)
        dz_ref[:, QL + KVL:] = _unrope(dkr, cs_t, half).astype(BF16)
        dgq_ref[...] += dgq
        dgkv_ref[...] += dgkv

    W = z_qkr.shape[1]
    return _call(
        body, "mla_prep_bwd", (S // tm,),
        [_rows(tm, H * HEAD_SLOT), _rows(tm, H * HEAD_SLOT), _rows(tm, H * VDIM), _rows(tm, W), _rows(tm, 1),
         _rows(tm, 1), _whole(gq), _whole(gkv), _rows(tm, 3 * LANES), _whole(c256_w)],
        [_rows(tm, W), _rows(tm, H * HEAD_SLOT), _rows(tm, H * HEAD_SLOT), _whole(gq), _whole(gkv)],
        [_sds((S, W), BF16), _sds((S, H * HEAD_SLOT), BF16), _sds((S, H * HEAD_SLOT), BF16),
         _sds((1, QL), F32), _sds((1, KVL), F32)],
    )(dq, dk, dv, z_qkr, rq, rkv, gq, gkv, cs, c256_w)


def _mix_in_bwd(d_bcv, dz_qkr, dgg, w_bcv, w_qkr, w_gg, h, rstd, gain, dh):
    S, D = h.shape
    C = d_bcv.shape[2]
    tm = _tile(S, 512, SUBLANES)

    def body(db_ref, dq_ref, dgg_ref, wb_ref, wq_ref, wg_ref, h_ref, r_ref, gain_ref, dh_ref, o_ref, dgain_ref):
        @pl.when(pl.program_id(0) == 0)
        def _():
            dgain_ref[...] = jnp.zeros_like(dgain_ref)

        dn = _dot_nt(dq_ref[...], wq_ref[...]) + _dot_nt(dgg_ref[...], wg_ref[...])
        for k in range(3):
            dn = dn + _dot_nt(db_ref[k], wb_ref[k])
        dx, dgain = _rms_bwd(dn, h_ref[...], r_ref[...], gain_ref[...])
        o_ref[...] = dh_ref[...] + dx
        dgain_ref[...] += dgain

    return _call(
        body, "mix_in_bwd", (S // tm,),
        [pl.BlockSpec((3, tm, C), lambda i: (0, i, 0)), _rows(tm, dz_qkr.shape[1]), _rows(tm, dgg.shape[1]),
         _whole(w_bcv), _whole(w_qkr), _whole(w_gg), _rows(tm, D), _rows(tm, 1), _whole(gain), _rows(tm, D)],
        [_rows(tm, D), pl.BlockSpec((1, D), lambda i: (0, 0))],
        [_sds((S, D), F32), _sds((1, D), F32)],
    )(d_bcv, dz_qkr, dgg, w_bcv, w_qkr, w_gg, h, rstd, gain, dh)


def _rope_tables(positions):
    half = ROPE // 2
    inv_freq = ROPE_THETA ** (-jnp.arange(0, ROPE, 2, dtype=F32) / ROPE)
    ang = positions.astype(F32)[:, None] * inv_freq
    cos, sin = jnp.cos(ang), jnp.sin(ang)
    z = jnp.zeros_like(cos)
    pad = jnp.zeros((positions.shape[0], LANES - 2 * half), F32)
    return jnp.concatenate([cos, cos, pad, -sin, z, pad, z, sin, pad], axis=1)


def _layer_fwd(h0, p_l, cs, w, sm):
    C = sm["conv_w"].shape[1]
    QL, KVL = sm["q_norm"].shape[1], sm["kv_norm"].shape[1]
    gu1, a1, n1, r1 = _ffn_up(h0, sm["ffn1_norm"], w["gu"], 0)
    h1 = _ffn_down(a1, w["dn"], 0, h0)
    w_bcv, w_qkr, w_gg = _win_split(w["win"], C, QL, KVL)
    z_bcv, z_qkr, z_gg, un, rm = _mix_in(h1, sm["mix_norm"], w_bcv, w_qkr, w_gg)
    by = _conv_fwd(z_bcv, sm["conv_w"])
    q, k, v, qn, kvn, rq, rkv = _mla_prep(z_qkr, sm["q_norm"], sm["kv_norm"], cs, w["c256"])
    o, lse = _attn_fwd(q, k, v, N_DEV)
    h2, merged, yconv, ymla = _merge_wo(o, by, z_gg, h1, w["sq"], w["c128"])
    gu2, a2, n2, r2 = _ffn_up(h2, sm["ffn2_norm"], w["gu"], 1)
    h3 = _ffn_down(a2, w["dn"], 1, h2)
    h4, pre, pp, pn, rp = _ple_fwd(h3, sm["ple_norm"], p_l, w["sq"], w["c128"], C)
    saved = dict(h0=h0, gu1=gu1, a1=a1, n1=n1, r1=r1, h1=h1, w_bcv=w_bcv, w_qkr=w_qkr, w_gg=w_gg, z_bcv=z_bcv,
                 z_qkr=z_qkr, z_gg=z_gg, un=un, rm=rm, by=by, q=q, k=k, v=v, qn=qn, kvn=kvn, rq=rq, rkv=rkv, o=o,
                 lse=lse, h2=h2, merged=merged, yconv=yconv, ymla=ymla, gu2=gu2, a2=a2, n2=n2, r2=r2, h3=h3,
                 pre=pre, pp=pp, pn=pn, rp=rp, p=p_l)
    return h4, saved


def _layer_bwd_late(dh4, s, w, sm, after):
    D = dh4.shape[1]
    C = sm["conv_w"].shape[1]
    P = s["p"].shape[1]
    rows = {n: w[n].shape[1] for n in CLASSES}
    small = {}
    dh3, dpre, dpp, small["ple_norm"] = _ple_bwd(dh4, s["pre"], s["pp"], s["h3"], s["rp"], sm["ple_norm"], w["sq"],
                                                 after)
    g_sq = _tn_square(s["pn"], dpre, None, rows["sq"], 2)
    g_c128 = _tn_cols(s["p"], dpp, None, rows["c128"], C // P)

    dgu2 = _ffn_bwd_act(dh3, w["dn"], 1, s["gu2"])
    g_dn = _tn_down(s["a2"], dh3, None, rows["dn"], 1)
    g_gu = _tn_slots(s["n2"], dgu2, None, rows["gu"], D)
    dh2, small["ffn2_norm"] = _ffn_bwd_in(dgu2, w["gu"], 1, s["h2"], s["r2"], sm["ffn2_norm"], dh3)
    return dh2, dict(gu=g_gu, dn=g_dn, sq=g_sq, c128=g_c128), small


def _layer_bwd_mixer(dh2, part, small, s, cs, w, sm, after):
    C = sm["conv_w"].shape[1]
    rows = {n: w[n].shape[1] for n in CLASSES}
    g_gu, g_dn, g_sq, g_c128 = part["gu"], part["dn"], part["sq"], part["c128"]

    dgg, dby, do, dyc, dym = _merge_bwd(dh2, s["z_gg"], s["yconv"], s["ymla"], w["sq"], w["c128"], C, after)
    g_sq = _tn_square(s["merged"], dh2, g_sq, rows["sq"], 1)
    g_sq = _tn_square(s["o"], dym, g_sq, rows["sq"], 0)
    g_c128 = _tn_cols(s["by"], dyc, g_c128, rows["c128"], 0)
    d_bcv, small["conv_w"] = _conv_bwd(s["z_bcv"], sm["conv_w"], dby)
    dq, dk, dv = _attn_bwd(s["q"], s["k"], s["v"], do, s["lse"], _attn_delta(do, s["o"], N_DEV), N_DEV)
    dz_qkr, dqp, dkv, small["q_norm"], small["kv_norm"] = _mla_prep_bwd(
        dq, dk, dv, s["z_qkr"], s["rq"], s["rkv"], sm["q_norm"], sm["kv_norm"], cs, w["c256"])
    g_c256 = _tn_heads(s["qn"], s["kvn"], dqp, dkv)
    un = s["un"]
    g_win = _win_merge(_tn_plain(un, d_bcv), _tn_plain(un, dz_qkr[None])[0], _tn_plain(un, dgg[None])[0],
                       w["win"].shape[2])
    dh1, small["mix_norm"] = _mix_in_bwd(d_bcv, dz_qkr, dgg, s["w_bcv"], s["w_qkr"], s["w_gg"], s["h1"], s["rm"],
                                         sm["mix_norm"], dh2)
    return dh1, dict(gu=g_gu, dn=g_dn, sq=g_sq, win=g_win, c128=g_c128, c256=g_c256), small


def _layer_bwd_first(dh1, part, small, s, w, sm, after):
    rows = {n: w[n].shape[1] for n in CLASSES}
    dgu1 = _ffn_bwd_act(dh1, w["dn"], 0, s["gu1"], after)
    g_dn = _tn_down(s["a1"], dh1, part["dn"], rows["dn"], 0)
    g_gu = _tn_slots(s["n1"], dgu1, part["gu"], rows["gu"], 0)
    dh0, small["ffn1_norm"] = _ffn_bwd_in(dgu1, w["gu"], 0, s["h0"], s["r1"], sm["ffn1_norm"], dh1)
    return dh0, dict(part, gu=g_gu, dn=g_dn), small


def _mesh_pos():
    return lax.axis_index("x"), lax.axis_index("y"), lax.axis_index("c")


def _other_chips(x, y):
    return [(1 - x, y), (x, 1 - y), (1 - x, 1 - y)]


def _pack(arrs, width):
    L = arrs[0].shape[0]
    shapes = [a.shape[1:] for a in arrs]
    R = sum(r for r, _ in shapes)

    def body(*refs):
        o_ref = refs[-1]
        off = 0
        for a_ref, (r, c) in zip(refs[:-1], shapes):
            o_ref[0, off:off + r, 0:c] = a_ref[0].astype(BF16)
            if c < width:
                o_ref[0, off:off + r, c:width] = jnp.zeros((r, width - c), BF16)
            off += r

    return _call(
        body, "pack", (L,),
        [pl.BlockSpec((1, r, c), lambda l: (l, 0, 0)) for r, c in shapes],
        pl.BlockSpec((1, R, width), lambda l: (l, 0, 0)),
        _sds((L, R, width), BF16),
    )(*arrs)


def _handshake(peers):
    barrier = pltpu.get_barrier_semaphore()
    for peer in peers:
        pl.semaphore_signal(barrier, inc=1, device_id=peer, device_id_type=MESH)
    pl.semaphore_wait(barrier, len(peers))


def _sequencer_call(body, name, out_types, sems, collective_id, operands):
    return pl.kernel(
        body, name=name, out_type=out_types,
        mesh=plsc.ScalarSubcoreMesh(axis_name="seq", num_cores=1),
        scratch_types=tuple(pltpu.SemaphoreType.DMA((k,)) for k in sems),
        compiler_params=pltpu.CompilerParams(collective_id=collective_id),
    )(*operands)


def _all_gather(packs, l, after, collective_id):
    n = len(packs)

    def body(*refs):
        ins, outs = refs[:n], refs[n + len(after):2 * n + len(after)]
        send_sems, recv_sems, local_sems = refs[2 * n + len(after):]
        x, y, c = _mesh_pos()
        me, sibling = (x, y, c), (x, y, 1 - c)
        chips = _other_chips(x, y)
        _handshake([sibling] + [(*chip, c) for chip in chips])

        def copy(q, k, block, to, src=None):
            slot = outs[q].at[4 * block[0] + 2 * block[1] + block[2]]
            return pltpu.make_async_remote_copy(
                src_ref=slot if src is None else src, dst_ref=slot,
                send_sem=send_sems.at[7 * q + k], recv_sem=recv_sems.at[7 * q + k], device_id=to, device_id_type=MESH)

        started = []
        for q in range(n):
            src = ins[q].at[l]
            mine = pltpu.make_async_copy(src, outs[q].at[4 * x + 2 * y + c], local_sems.at[q])
            mine.start()
            started.append(mine)
        sends = []
        for q in range(n):
            src = ins[q].at[l]
            sends.append(copy(q, 0, me, sibling, src=src))
            sends += [copy(q, 1 + j, me, (*chip, c), src=src) for j, chip in enumerate(chips)]
        for cp in sends:
            cp.start()
        for q in range(n):
            for j, chip in enumerate(chips):
                copy(q, 1 + j, (*chip, c), me).wait_recv()
                fwd = copy(q, 4 + j, (*chip, c), sibling)
                fwd.start()
                sends.append(fwd)
        for q in range(n):
            copy(q, 0, sibling, me).wait_recv()
            for j, chip in enumerate(chips):
                copy(q, 4 + j, (*chip, 1 - c), me).wait_recv()
        for cp in sends:
            cp.wait_send()
        for mine in started:
            mine.wait()

    return _sequencer_call(
        body, f"all_gather_{l}", [_sds((N_DEV,) + p.shape[1:], p.dtype) for p in packs], (7 * n, 7 * n, n),
        collective_id, list(packs) + list(after))


def _rs_d2d(gs, l, collective_id):
    n = len(gs)

    def body(*refs):
        ins, outs = refs[:n], refs[n:2 * n]
        send_sems, recv_sems = refs[2 * n:]
        x, y, c = _mesh_pos()
        _handshake([(x, y, 1 - c)])
        copies = []
        for q in range(n):
            for j in range(4):
                copies.append(pltpu.make_async_remote_copy(
                    src_ref=ins[q].at[2 * j + (1 - c)], dst_ref=outs[q].at[j], send_sem=send_sems.at[4 * q + j],
                    recv_sem=recv_sems.at[4 * q + j], device_id=(x, y, 1 - c), device_id_type=MESH))
        for cp in copies:
            cp.start()
        for cp in copies:
            cp.wait()

    return _sequencer_call(
        body, f"rs_d2d_{l}", [_sds((4,) + g.shape[1:], g.dtype) for g in gs], (4 * n, 4 * n), collective_id, gs)


def _rs_add_chip(gs, as_, after):
    n = len(gs)
    steps = 4
    tiles = [g.shape[1] // steps for g in gs]

    def chip(k):
        x, y, _ = _mesh_pos()
        return ([(x, y)] + _other_chips(x, y))[k]

    def body(*refs):
        g_refs, a_refs = refs[:4 * n], refs[4 * n:8 * n]
        own_refs, t_refs = refs[8 * n + len(after):9 * n + len(after)], refs[9 * n + len(after):]
        for q in range(n):
            g, a = g_refs[4 * q:4 * q + 4], a_refs[4 * q:4 * q + 4]
            own_refs[q][...] = g[0][0].astype(F32) + a[0][0].astype(F32)
            for k in range(1, 4):
                t_refs[q][k - 1] = (g[k][0].astype(F32) + a[k][0].astype(F32)).astype(BF16)

    def gspec(q, k):
        def index(i):
            px, py = chip(k)
            return 4 * px + 2 * py + lax.axis_index("c"), i, 0
        return pl.BlockSpec((1, tiles[q], gs[q].shape[2]), index)

    def aspec(q, k):
        def index(i):
            px, py = chip(k)
            return 2 * px + py, i, 0
        return pl.BlockSpec((1, tiles[q], gs[q].shape[2]), index)

    in_specs = [gspec(q, k) for q in range(n) for k in range(4)] + [aspec(q, k) for q in range(n) for k in range(4)]
    operands = [g for g in gs for _ in range(4)] + [a for a in as_ for _ in range(4)]
    out_specs = [pl.BlockSpec((tiles[q], gs[q].shape[2]), lambda i: (i, 0)) for q in range(n)]
    out_specs += [pl.BlockSpec((3, tiles[q], gs[q].shape[2]), lambda i: (0, i, 0)) for q in range(n)]
    out_shape = [_sds(g.shape[1:], F32) for g in gs] + [_sds((3,) + g.shape[1:], BF16) for g in gs]
    res = _call(body, "rs_add_chip", (steps,), in_specs + [ANY] * len(after), out_specs, out_shape)(*operands, *after)
    return res[:n], res[n:]


def _rs_ici(ts, l, collective_id):
    n = len(ts)

    def body(*refs):
        ins, outs = refs[:n], refs[n:2 * n]
        send_sems, recv_sems = refs[2 * n:]
        x, y, c = _mesh_pos()
        chips = _other_chips(x, y)
        _handshake([(*chip, c) for chip in chips])
        copies = []
        for q in range(n):
            for k, chip in enumerate(chips):
                copies.append(pltpu.make_async_remote_copy(
                    src_ref=ins[q].at[k], dst_ref=outs[q].at[k], send_sem=send_sems.at[3 * q + k],
                    recv_sem=recv_sems.at[3 * q + k], device_id=(*chip, c), device_id_type=MESH))
        for cp in copies:
            cp.start()
        for cp in copies:
            cp.wait()

    return _sequencer_call(
        body, f"rs_ici_{l}", [_sds(t.shape, t.dtype) for t in ts], (3 * n, 3 * n), collective_id, ts)


def _all_reduce_small(v):
    n, W = v.shape

    def body(v_ref, out_ref, slots, send_sems, recv_sems):
        x, y, c = _mesh_pos()
        me = 4 * x + 2 * y + c
        slots[me] = v_ref[...]
        copies = []
        for k in range(1, N_DEV):
            kx, ky, kc = (k >> 2) & 1, (k >> 1) & 1, k & 1
            peer = (1 - x if kx else x, 1 - y if ky else y, 1 - c if kc else c)
            copies.append(pltpu.make_async_remote_copy(
                src_ref=v_ref, dst_ref=slots.at[me], send_sem=send_sems.at[k - 1], recv_sem=recv_sems.at[k - 1],
                device_id=peer, device_id_type=MESH))
        for cp in copies:
            cp.start()
        for cp in copies:
            cp.wait()
        acc = slots[0]
        for d in range(1, N_DEV):
            acc = acc + slots[d]
        out_ref[...] = acc

    vm = pl.BlockSpec(memory_space=pltpu.VMEM)
    return pl.pallas_call(
        body, name="all_reduce_small",
        out_shape=_sds((n, W), F32),
        in_specs=[vm], out_specs=vm,
        scratch_shapes=[pltpu.VMEM((N_DEV, n, W), F32), pltpu.SemaphoreType.DMA((7,)), pltpu.SemaphoreType.DMA((7,))],
    )(v)


def _adamw_math(w, g, m, v):
    m2 = ADAM_B1 * m + (1.0 - ADAM_B1) * g
    v2 = ADAM_B2 * v + (1.0 - ADAM_B2) * (g * g)
    m_hat = m2 / (1.0 - ADAM_B1 ** ADAM_STEP)
    v_hat = v2 / (1.0 - ADAM_B2 ** ADAM_STEP)
    return -ADAM_LR * (m_hat / (jnp.sqrt(v_hat) + ADAM_EPS) + ADAM_WD * w), m2, v2


def _adamw(w, g, m, v):
    L, r, c = w.shape
    tr = _tile(r, max(SUBLANES, (256 * 1024 // c) // SUBLANES * SUBLANES), SUBLANES)

    def body(w_ref, g_ref, m_ref, v_ref, d_ref, nm_ref, nv_ref):
        d_ref[...], nm_ref[...], nv_ref[...] = _adamw_math(w_ref[...], g_ref[...], m_ref[...], v_ref[...])

    spec = pl.BlockSpec((1, tr, c), lambda l, i: (l, i, 0))
    return _call(body, "adamw", (L, r // tr), [spec] * 4, [spec] * 3, [_sds((L, r, c), F32)] * 3)(w, g, m, v)


def _adamw_reduced(w, m, v, own, b, row_off, tr, l, prev, after):
    L, r, c = w.shape
    W = own.shape[1]
    ob = row_off // tr
    extra = list(prev or ()) + list(after)

    def body(w_ref, m_ref, v_ref, own_ref, b_ref, *rest):
        g_ref, d_ref, nm_ref, nv_ref = rest[len(extra):]
        g = ((own_ref[...] + b_ref[0].astype(F32)) + b_ref[1].astype(F32)) + b_ref[2].astype(F32)
        g = g[:, :c]
        g_ref[0] = g
        d_ref[0], nm_ref[0], nv_ref[0] = _adamw_math(w_ref[0], g, m_ref[0], v_ref[0])

    spec = pl.BlockSpec((1, tr, c), lambda i: (l, i, 0))
    return _call(
        body, "adamw_reduced", (r // tr,),
        [spec] * 3 + [pl.BlockSpec((tr, W), lambda i: (ob + i, 0)), pl.BlockSpec((3, tr, W), lambda i: (0, ob + i, 0))]
        + [ANY] * len(extra),
        [spec] * 4, [_sds((L, r, c), F32)] * 4,
        aliases={5 + k: k for k in range(4)} if prev else None,
    )(w, m, v, own, b, *extra)


_MEMBERS = dict(gu=("ffn1_w_gu", "ffn2_w_gu"), dn=("ffn1_w_down", "ffn2_w_down"),
                sq=("w_mla_out", "w_o", "w_ple_gate"), win=("w_in",), c128=("w_conv_out", "w_ple_proj"),
                c256=("w_ukv", "w_uq"))
_SMALL = ("ffn1_norm", "mix_norm", "q_norm", "kv_norm", "ffn2_norm", "ple_norm")
_ORDER = ("ffn1_norm", "ffn1_w_gu", "ffn1_w_down", "mix_norm", "w_in", "conv_w", "w_conv_out", "q_norm", "kv_norm",
          "w_uq", "w_ukv", "w_mla_out", "w_o", "ffn2_norm", "ffn2_w_gu", "ffn2_w_down", "ple_norm", "w_ple_gate",
          "w_ple_proj", "final_norm")


def _class_width(wts, cls):
    return HEAD_SLOT if cls == "c256" else wts[_MEMBERS[cls][0]].shape[2]


def _pack_rows(vecs, width):
    flat = jnp.concatenate([a.reshape(-1) for a in vecs])
    n = flat.shape[0]
    rows = -(-n // width)
    rows = -(-rows // SUBLANES) * SUBLANES
    flat = jnp.pad(flat, (0, rows * width - n))
    offs, o = [], 0
    for a in vecs:
        offs.append(o)
        o += a.size
    return flat.reshape(rows, width), offs


def _unpack_rows(packed, vecs, offs):
    flat = packed.reshape(-1)
    return [flat[o:o + a.size].reshape(a.shape) for a, o in zip(vecs, offs)]


def _train(x, p, positions, target, gathered, packs, small_w, final_norm, update):
    cs = _rope_tables(positions)
    L = len(small_w)
    h = x
    saved = []
    if packs is not None:
        gathered = [dict(zip(CLASSES, _all_gather(packs, 0, [], 0)))]
    for l in range(L):
        h, s = _layer_fwd(h, p[l], cs, gathered[l], small_w[l])
        saved.append(s)
        if packs is not None and l + 1 < L:
            gathered.append(dict(zip(CLASSES, _all_gather(packs, l + 1, [s["n1"]], l + 1))))
    dh, loss, d_final = _final_loss(h, final_norm, target)
    grads, smalls = [None] * L, [None] * L
    exchanged = None
    landing = None

    def second_stage(after):
        l, gs, as_ = exchanged
        owns, ts = _rs_add_chip(gs, as_, [after])
        return l, owns, _rs_ici(ts, l, 2 * L + l)

    for l in reversed(range(L)):
        dh, part, small = _layer_bwd_late(dh, saved[l], gathered[l], small_w[l], [])
        pin = []
        if exchanged is not None:
            landing = second_stage(dh)
            pin = [landing[1][0]]
        dh, part, small = _layer_bwd_mixer(dh, part, small, saved[l], cs, gathered[l], small_w[l], pin)
        pin = [update(*landing)] if exchanged is not None else []
        dh, g, smalls[l] = _layer_bwd_first(dh, part, small, saved[l], gathered[l], small_w[l], pin)
        if update is not None:
            gs = [g[cls] for cls in CLASSES]
            exchanged = (l, gs, _rs_d2d(gs, l, L + l))
        else:
            grads[l] = g
    if update is not None:
        update(*second_stage(dh))
    return loss[0, 0], dh, grads, smalls, d_final


def kernel(x, p, positions, ffn1_norm, ffn1_w_gu, ffn1_w_down, mix_norm, w_in, conv_w, w_conv_out, q_norm, kv_norm, w_uq, w_ukv, w_mla_out, w_o, ffn2_norm, ffn2_w_gu, ffn2_w_down, ple_norm, w_ple_gate, w_ple_proj, final_norm, loss_target, m_ffn1_norm, m_ffn1_w_gu, m_ffn1_w_down, m_mix_norm, m_w_in, m_conv_w, m_w_conv_out, m_q_norm, m_kv_norm, m_w_uq, m_w_ukv, m_w_mla_out, m_w_o, m_ffn2_norm, m_ffn2_w_gu, m_ffn2_w_down, m_ple_norm, m_w_ple_gate, m_w_ple_proj, m_final_norm, v_ffn1_norm, v_ffn1_w_gu, v_ffn1_w_down, v_mix_norm, v_w_in, v_conv_w, v_w_conv_out, v_q_norm, v_kv_norm, v_w_uq, v_w_ukv, v_w_mla_out, v_w_o, v_ffn2_norm, v_ffn2_w_gu, v_ffn2_w_down, v_ple_norm, v_w_ple_gate, v_w_ple_proj, v_final_norm):
    args = dict(locals())
    wts = {n: args[n] for n in _ORDER}
    L = w_in.shape[0]
    dev = 4 * lax.axis_index("x") + 2 * lax.axis_index("y") + lax.axis_index("c")

    packs = [_pack([wts[n] for n in _MEMBERS[cls]], _class_width(wts, cls)) for cls in CLASSES]
    cw = conv_w.shape[2]
    conv_full = lax.dynamic_update_slice(jnp.zeros((L, 3, N_DEV * cw), F32), conv_w, (0, 0, dev * cw))
    conv_packed, conv_offs = _pack_rows([conv_full], FLAT_COLS)
    conv_full = _unpack_rows(_all_reduce_small(conv_packed), [conv_full], conv_offs)[0]
    small_w = [dict({n: wts[n][l][None, :] for n in _SMALL}, conv_w=conv_full[l]) for l in range(L)]

    done = {}

    def update(l, owns, bs):
        for q, cls in enumerate(CLASSES):
            off = 0
            rows = [wts[n].shape[1] for n in _MEMBERS[cls]]
            tr = _tile(math.gcd(*rows), 256, BF16_ROWS)
            for n, r in zip(_MEMBERS[cls], rows):
                done[n] = _adamw_reduced(wts[n], args["m_" + n], args["v_" + n], owns[q], bs[q], off, tr, l,
                                         done.get(n), [])
                off += r
        return done[_MEMBERS[CLASSES[-1]][-1]][0]

    loss_dev, grad_x, _, smalls, d_final = _train(x[0], p[:, 0], positions[0], loss_target[0], None, packs, small_w,
                                                  final_norm[None, :], update)
    loss = lax.psum(loss_dev, ("x", "y", "c"))

    small = [jnp.stack([smalls[l][n][0] for l in range(L)]) for n in _SMALL]
    small += [jnp.stack([smalls[l]["conv_w"] for l in range(L)]), d_final[0]]
    packed, offs = _pack_rows(small, FLAT_COLS)
    small = _unpack_rows(_all_reduce_small(packed), small, offs)
    grad = dict(zip(_SMALL, small))
    grad["conv_w"] = lax.dynamic_slice(small[len(_SMALL)], (0, 0, dev * cw), (L, 3, cw))
    grad["final_norm"] = small[-1]

    deltas, new_m, new_v = {}, {}, {}
    for n, (g, d, nm, nv) in done.items():
        grad[n], deltas[n], new_m[n], new_v[n] = g, d, nm, nv
    for n in _SMALL + ("conv_w", "final_norm"):
        w3 = wts[n].reshape((1,) * (3 - wts[n].ndim) + wts[n].shape)
        d, nm, nv = _adamw(w3, grad[n].reshape(w3.shape), args["m_" + n].reshape(w3.shape),
                           args["v_" + n].reshape(w3.shape))
        deltas[n], new_m[n], new_v[n] = (a.reshape(wts[n].shape) for a in (d, nm, nv))
    return (loss, grad_x[None], *[grad[n] for n in _ORDER], *[deltas[n] for n in _ORDER],
            *[new_m[n] for n in _ORDER], *[new_v[n] for n in _ORDER])
```

```python
import math

import jax
import jax.numpy as jnp
from jax import lax
from jax.experimental import pallas as pl
from jax.experimental.pallas import tpu as pltpu
from jax.experimental.pallas import tpu_sc as plsc

F32 = jnp.float32
BF16 = jnp.bfloat16

CHUNK = 64
NOPE = 128
ROPE = 64
VDIM = 128
ROPE_THETA = 10000.0
EPS = 1e-6
ATTN_SCALE = (NOPE + ROPE) ** -0.5
SCORE_SCALE = ATTN_SCALE * math.log2(math.e)
ADAM_LR = 0.001
ADAM_B1 = 0.9
ADAM_B2 = 0.999
ADAM_EPS = 1e-08
ADAM_WD = 0.01
ADAM_STEP = 10

LANES = 128
SUBLANES = 8
BF16_ROWS = 16
V7X_VMEM_BYTES = 64 * 1024 * 1024
VMEM_LIMIT = V7X_VMEM_BYTES * 7 // 8
HEAD_SLOT = 2 * LANES
N_DEV = 8
FLAT_COLS = 1024
CLASSES = ("gu", "dn", "sq", "win", "c128", "c256")

NT = (((1,), (1,)), ((), ()))
MESH = pl.DeviceIdType.MESH
ANY = pl.BlockSpec(memory_space=pl.ANY)


def _dot(a, b):
    return jnp.dot(a, b, preferred_element_type=F32)


def _dot_nt(a, b):
    return lax.dot_general(a, b, NT, preferred_element_type=F32)


def _sig(x):
    return 1.0 / (1.0 + jnp.exp(-x))


def _tile(n, pref, unit):
    if n <= pref:
        return n
    t = (pref // unit) * unit
    while t >= unit:
        if n % t == 0:
            return t
        t -= unit
    return n


def _call(body, name, grid, in_specs, out_specs, out_shape, scratch=(), aliases=None):
    return pl.pallas_call(
        body,
        name=name,
        grid=grid,
        in_specs=in_specs,
        out_specs=out_specs,
        out_shape=out_shape,
        scratch_shapes=list(scratch),
        input_output_aliases=aliases or {},
        compiler_params=pltpu.CompilerParams(
            dimension_semantics=("arbitrary",) * len(grid), vmem_limit_bytes=VMEM_LIMIT
        ),
    )


def _sds(shape, dtype):
    return jax.ShapeDtypeStruct(shape, dtype)


def _rms_fwd(x, gain):
    rstd = lax.rsqrt(jnp.mean(x * x, axis=-1, keepdims=True) + EPS)
    return x * rstd * gain, rstd


def _rms_bwd(dn, x, rstd, gain):
    xhat = x * rstd
    dgy = dn * gain
    dx = rstd * (dgy - xhat * jnp.mean(dgy * xhat, axis=-1, keepdims=True))
    return dx, jnp.sum(dn * xhat, axis=0, keepdims=True)


def _rows(tm, w):
    return pl.BlockSpec((tm, w), lambda i: (i, 0))


def _whole(a):
    nd = a.ndim
    return pl.BlockSpec(a.shape, lambda i: (0,) * nd)


def _slab(buf, rows, index):
    return pl.BlockSpec((N_DEV, rows, buf.shape[2]), lambda i: (0, index, 0))


def _cat_slots(w):
    return jnp.concatenate([w[d] for d in range(N_DEV)], axis=1)


def _ffn_up(h, gain, gu_w, which):
    S, D = h.shape
    c = gu_w.shape[2]
    tm = _tile(S, 256, SUBLANES)
    nb = N_DEV // 2

    def body(h_ref, gain_ref, w_ref, jac_ref, a_ref, at_ref, n_ref, r_ref):
        n32, rstd = _rms_fwd(h_ref[...], gain_ref[...])
        n = n32.astype(BF16)
        n_ref[...] = n
        r_ref[...] = rstd
        for d in range(nb):
            g = _dot(n, w_ref[d])
            u = _dot(n, w_ref[nb + d])
            sg = _sig(g)
            silu = g * sg
            a = (silu * u).astype(BF16)
            a_ref[d] = a
            at_ref[d] = a.T
            jac_ref[d] = (0.5 * u * (sg + silu * (1.0 - sg))).astype(BF16)
            jac_ref[nb + d] = (0.5 * silu).astype(BF16)

    return _call(
        body, "ffn_up", (S // tm,),
        [_rows(tm, D), _whole(gain), _slab(gu_w, D, which)],
        [pl.BlockSpec((N_DEV, tm, c), lambda i: (0, i, 0)), pl.BlockSpec((nb, tm, c), lambda i: (0, i, 0)),
         pl.BlockSpec((nb, c, tm), lambda i: (0, 0, i)), _rows(tm, D), _rows(tm, 1)],
        [_sds((N_DEV, S, c), BF16), _sds((nb, S, c), BF16), _sds((nb, c, S), BF16), _sds((S, D), BF16),
         _sds((S, 1), F32)],
    )(h, gain, gu_w)


def _down_weight(w_ref, d, c):
    return w_ref[2 * d:2 * d + 2].reshape(c, w_ref.shape[2])


def _ffn_down(a, dn_w, which, h):
    nb, S, c = a.shape
    D = h.shape[1]
    tm = _tile(S, 512, SUBLANES)

    def body(a_ref, w_ref, h_ref, o_ref):
        acc = _dot(a_ref[0], _down_weight(w_ref, 0, c))
        for d in range(1, nb):
            acc = acc + _dot(a_ref[d], _down_weight(w_ref, d, c))
        o_ref[...] = h_ref[...] + 0.5 * acc

    return _call(
        body, "ffn_down", (S // tm,),
        [pl.BlockSpec((nb, tm, c), lambda i: (0, i, 0)), _slab(dn_w, c // 2, which), _rows(tm, D)],
        _rows(tm, D),
        _sds((S, D), F32),
    )(a, dn_w, h)


def _win_segments(C, QL, KVL, D):
    o1, o2 = 3 * C, 3 * C + QL + KVL + ROPE
    return [("bcv", k, k * C, (k + 1) * C) for k in range(3)] + [("qkr", None, o1, o2), ("gg", None, o2, o2 + 2 * D)]


def _win_pieces(segments, cw):
    out = []
    for tgt, lead, a, b in segments:
        for d in range(N_DEV):
            lo, hi = max(a, d * cw), min(b, (d + 1) * cw)
            if lo < hi:
                out.append((tgt, lead, d, (lo - d * cw, hi - d * cw), (lo - a, hi - a)))
    return out


def _win_split(win_w, C, QL, KVL):
    _, D, cw = win_w.shape
    WQ = QL + KVL + LANES
    pieces = _win_pieces(_win_segments(C, QL, KVL, D), cw)
    tr = _tile(D, 256, BF16_ROWS)

    def body(w_ref, bcv_ref, qkr_ref, gg_ref):
        tgt = dict(bcv=bcv_ref, qkr=qkr_ref, gg=gg_ref)
        qkr_ref[:, QL + KVL + ROPE:] = jnp.zeros((tr, LANES - ROPE), BF16)
        for name, lead, d, (s0, s1), (t0, t1) in pieces:
            v = w_ref[d, :, s0:s1]
            if lead is None:
                tgt[name][:, t0:t1] = v
            else:
                tgt[name][lead, :, t0:t1] = v

    return _call(
        body, "win_split", (D // tr,),
        [pl.BlockSpec((N_DEV, tr, cw), lambda i: (0, i, 0))],
        [pl.BlockSpec((3, tr, C), lambda i: (0, i, 0)), _rows(tr, WQ), _rows(tr, 2 * D)],
        [_sds((3, D, C), BF16), _sds((D, WQ), BF16), _sds((D, 2 * D), BF16)],
    )(win_w)


def _win_merge(d_bcv, d_qkr, d_gg, cw):
    _, D, C = d_bcv.shape
    WQ = d_qkr.shape[1]
    QL_KVL = WQ - LANES
    o1 = 3 * C
    segments = [("bcv", k, k * C, (k + 1) * C) for k in range(3)]
    segments += [("qkr", None, o1, o1 + QL_KVL + ROPE), ("gg", None, o1 + QL_KVL + ROPE, o1 + QL_KVL + ROPE + 2 * D)]
    pieces = _win_pieces(segments, cw)
    tr = _tile(D, 256, BF16_ROWS)

    def body(bcv_ref, qkr_ref, gg_ref, o_ref):
        src = dict(bcv=bcv_ref, qkr=qkr_ref, gg=gg_ref)
        for name, lead, d, (s0, s1), (t0, t1) in pieces:
            v = src[name][:, t0:t1] if lead is None else src[name][lead, :, t0:t1]
            o_ref[d, :, s0:s1] = v.astype(BF16)

    return _call(
        body, "win_merge", (D // tr,),
        [pl.BlockSpec((3, tr, C), lambda i: (0, i, 0)), _rows(tr, WQ), _rows(tr, 2 * D)],
        pl.BlockSpec((N_DEV, tr, cw), lambda i: (0, i, 0)),
        _sds((N_DEV, D, cw), BF16),
    )(d_bcv, d_qkr, d_gg)


def _mix_in(h, gain, w_bcv, w_qkr, w_gg):
    S, D = h.shape
    C = w_bcv.shape[2]
    tm = _tile(S, 256, SUBLANES)

    def body(h_ref, gain_ref, w1, w2, w3, o1, o2, o3, n_ref, r_ref):
        n32, rstd = _rms_fwd(h_ref[...], gain_ref[...])
        n = n32.astype(BF16)
        n_ref[...] = n
        r_ref[...] = rstd
        for k in range(3):
            o1[k] = _dot(n, w1[k])
        o2[...] = _dot(n, w2[...])
        o3[...] = _dot(n, w3[...])

    return _call(
        body, "mix_in", (S // tm,),
        [_rows(tm, D), _whole(gain), _whole(w_bcv), _whole(w_qkr), _whole(w_gg)],
        [pl.BlockSpec((3, tm, C), lambda i: (0, i, 0)), _rows(tm, w_qkr.shape[1]), _rows(tm, 2 * D),
         _rows(tm, D), _rows(tm, 1)],
        [_sds((3, S, C), F32), _sds((S, w_qkr.shape[1]), F32), _sds((S, 2 * D), F32), _sds((S, D), BF16),
         _sds((S, 1), F32)],
    )(h, gain, w_bcv, w_qkr, w_gg)


def _conv_taps(zc):
    rows = lax.broadcasted_iota(jnp.int32, zc.shape, 0)
    z1 = jnp.where(rows >= 1, pltpu.roll(zc, 1, 0), 0.0)
    z2 = jnp.where(rows >= 2, pltpu.roll(zc, 2, 0), 0.0)
    return z1, z2


def _conv_fwd(z_bcv, conv_w):
    _, S, C = z_bcv.shape

    def body(z_ref, w_ref, o_ref):
        w = w_ref[...]
        zc = z_ref[1] * z_ref[2]
        z1, z2 = _conv_taps(zc)
        y = w[0:1] * z2 + w[1:2] * z1 + w[2:3] * zc
        o_ref[...] = (z_ref[0] * y).astype(BF16)

    return _call(
        body, "conv_fwd", (C // LANES,),
        [pl.BlockSpec((3, S, LANES), lambda j: (0, 0, j)), pl.BlockSpec((3, LANES), lambda j: (0, j))],
        pl.BlockSpec((S, LANES), lambda j: (0, j)),
        _sds((S, C), BF16),
    )(z_bcv, conv_w)


def _rope(x, cs, half):
    c, s1, s2 = cs[:, :LANES], cs[:, LANES:2 * LANES], cs[:, 2 * LANES:]
    return x * c + pltpu.roll(x, LANES - half, 1) * s1 + pltpu.roll(x, half, 1) * s2


def _unrope(d, cs, half):
    c, s1, s2 = cs[:, :LANES], cs[:, LANES:2 * LANES], cs[:, 2 * LANES:]
    return d * c + pltpu.roll(d * s1, half, 1) + pltpu.roll(d * s2, LANES - half, 1)


def _mla_prep(z_qkr, gq, gkv, cs, c256_w):
    S = z_qkr.shape[0]
    QL, KVL = gq.shape[1], gkv.shape[1]
    H = N_DEV
    tm = _tile(S, 256, SUBLANES)
    half = ROPE // 2

    def body(z_ref, gq_ref, gkv_ref, cs_ref, w_ref, q_ref, k_ref, v_ref, qn_ref, kvn_ref, rq_ref, rkv_ref):
        z = z_ref[...]
        cs_t = cs_ref[...]
        qn32, rq = _rms_fwd(z[:, :QL], gq_ref[...])
        kvn32, rkv = _rms_fwd(z[:, QL:QL + KVL], gkv_ref[...])
        qn = qn32.astype(BF16)
        kvn = kvn32.astype(BF16)
        qn_ref[...] = qn
        kvn_ref[...] = kvn
        rq_ref[...] = rq
        rkv_ref[...] = rkv
        krope = _rope(z[:, QL + KVL:], cs_t, half).astype(BF16)
        for h in range(H):
            lo, mid, hi = h * HEAD_SLOT, h * HEAD_SLOT + LANES, (h + 1) * HEAD_SLOT
            q = _dot(qn, w_ref[h, KVL:KVL + QL, :])
            kv = _dot(kvn, w_ref[h, 0:KVL, :])
            q_ref[:, lo:mid] = q[:, :LANES].astype(BF16)
            q_ref[:, mid:hi] = _rope(q[:, LANES:], cs_t, half).astype(BF16)
            k_ref[:, lo:mid] = kv[:, :LANES].astype(BF16)
            k_ref[:, mid:hi] = krope
            v_ref[:, h * VDIM:(h + 1) * VDIM] = kv[:, LANES:].astype(BF16)

    return _call(
        body, "mla_prep", (S // tm,),
        [_rows(tm, z_qkr.shape[1]), _whole(gq), _whole(gkv), _rows(tm, 3 * LANES), _whole(c256_w)],
        [_rows(tm, H * HEAD_SLOT), _rows(tm, H * HEAD_SLOT), _rows(tm, H * VDIM), _rows(tm, QL), _rows(tm, KVL),
         _rows(tm, 1), _rows(tm, 1)],
        [_sds((S, H * HEAD_SLOT), BF16), _sds((S, H * HEAD_SLOT), BF16), _sds((S, H * VDIM), BF16),
         _sds((S, QL), BF16), _sds((S, KVL), BF16), _sds((S, 1), F32), _sds((S, 1), F32)],
    )(z_qkr, gq, gkv, cs, c256_w)


def _chunk_mask(t):
    shift = CHUNK.bit_length() - 1
    krow = lax.broadcasted_iota(jnp.int32, (t, t), 0) >> shift
    qcol = lax.broadcasted_iota(jnp.int32, (t, t), 1) >> shift
    return krow <= qcol


def _attn_fwd(q, k, v, H):
    S = q.shape[0]
    t = _tile(S, 512, CHUNK)
    nq = S // t

    def body(q_ref, k_ref, v_ref, o_ref, lse_ref, vt_ref):
        qi = pl.program_id(1)

        @pl.when(qi == 0)
        def _():
            vt_ref[...] = v_ref[...].T

        qv = q_ref[...]

        def block(kj, carry, masked):
            m, l, acc = carry
            off = pl.multiple_of(kj * t, t)
            s = _dot_nt(k_ref[pl.ds(off, t), :], qv) * SCORE_SCALE
            if masked:
                s = jnp.where(_chunk_mask(t), s, -1e30)
            m_new = jnp.maximum(m, jnp.max(s, axis=0, keepdims=True))
            alpha = jnp.exp2(m - m_new)
            p = jnp.exp2(s - m_new)
            l = alpha * l + jnp.sum(p, axis=0, keepdims=True)
            acc = alpha * acc + _dot(vt_ref[:, pl.ds(off, t)], p.astype(BF16))
            return m_new, l, acc

        init = (jnp.full((1, t), -1e30, F32), jnp.zeros((1, t), F32), jnp.zeros((VDIM, t), F32))
        carry = lax.fori_loop(0, qi, lambda kj, c: block(kj, c, False), init)
        m, l, acc = block(qi, carry, True)
        o_ref[...] = (acc / l).T.astype(BF16)
        lse_ref[0] = jnp.broadcast_to(m + jnp.log2(l), (SUBLANES, t))

    return _call(
        body, "attn_fwd", (H, nq),
        [pl.BlockSpec((t, HEAD_SLOT), lambda h, i: (i, h)), pl.BlockSpec((S, HEAD_SLOT), lambda h, i: (0, h)),
         pl.BlockSpec((S, VDIM), lambda h, i: (0, h))],
        [pl.BlockSpec((t, VDIM), lambda h, i: (i, h)), pl.BlockSpec((1, SUBLANES, t), lambda h, i: (h, 0, i))],
        [_sds((S, H * VDIM), BF16), _sds((H, SUBLANES, S), F32)],
        [pltpu.VMEM((VDIM, S), BF16)],
    )(q, k, v)


def _merge_wo(o, by, z_gg, h, sq_w, c128_w):
    S, D = h.shape
    C = by.shape[1]
    r = sq_w.shape[1] // 3
    tm = _tile(S, 512, SUBLANES)

    def body(o_ref, by_ref, gg_ref, h_ref, wmo_ref, wo_ref, wco_ref, h2_ref, mg_ref, yc_ref, ym_ref):
        ymla = _dot(o_ref[...], wmo_ref[...].reshape(N_DEV * r, D))
        yconv = _dot(by_ref[...], _cat_slots(wco_ref))
        gg = gg_ref[...]
        merged = (_sig(gg[:, :D]) * yconv + _sig(gg[:, D:]) * ymla).astype(BF16)
        mg_ref[...] = merged
        yc_ref[...] = yconv.astype(BF16)
        ym_ref[...] = ymla.astype(BF16)
        h2_ref[...] = h_ref[...] + _dot(merged, wo_ref[...].reshape(N_DEV * r, D))

    return _call(
        body, "merge_wo", (S // tm,),
        [_rows(tm, o.shape[1]), _rows(tm, C), _rows(tm, 2 * D), _rows(tm, D), _slab(sq_w, r, 0), _slab(sq_w, r, 1),
         _slab(c128_w, C, 0)],
        [_rows(tm, D)] * 4,
        [_sds((S, D), F32)] + [_sds((S, D), BF16)] * 3,
    )(o, by, z_gg, h, sq_w, sq_w, c128_w)


def _ple_fwd(h, gain, p, sq_w, c128_w, C):
    S, D = h.shape
    P = p.shape[1]
    r = sq_w.shape[1] // 3
    tm = _tile(S, 512, SUBLANES)

    def body(h_ref, gain_ref, p_ref, wpg_ref, wpp_ref, o_ref, pre_ref, pp_ref, n_ref, r_ref):
        x = h_ref[...]
        n32, rstd = _rms_fwd(x, gain_ref[...])
        n = n32.astype(BF16)
        n_ref[...] = n
        r_ref[...] = rstd
        pre = _dot(n, wpg_ref[...].reshape(N_DEV * r, D))
        pp = _dot(p_ref[...].astype(BF16), _cat_slots(wpp_ref))
        pre_ref[...] = pre
        pp_ref[...] = pp
        o_ref[...] = x + _sig(pre) * pp

    return _call(
        body, "ple_fwd", (S // tm,),
        [_rows(tm, D), _whole(gain), _rows(tm, P), _slab(sq_w, r, 2), _slab(c128_w, P, C // P)],
        [_rows(tm, D), _rows(tm, D), _rows(tm, D), _rows(tm, D), _rows(tm, 1)],
        [_sds((S, D), F32)] * 3 + [_sds((S, D), BF16), _sds((S, 1), F32)],
    )(h, gain, p, sq_w, c128_w)


def _final_loss(h, gain, target):
    S, D = h.shape
    tm = _tile(S, 512, SUBLANES)

    def body(h_ref, gain_ref, t_ref, dh_ref, loss_ref, dg_ref):
        @pl.when(pl.program_id(0) == 0)
        def _():
            loss_ref[...] = jnp.zeros_like(loss_ref)
            dg_ref[...] = jnp.zeros_like(dg_ref)

        x = h_ref[...]
        gain_v = gain_ref[...]
        y, rstd = _rms_fwd(x, gain_v)
        err = y - t_ref[...]
        loss_ref[...] += 0.5 * jnp.sum(jnp.mean(err * err, axis=-1, keepdims=True))
        dx, dgain = _rms_bwd(err * (1.0 / D), x, rstd, gain_v)
        dh_ref[...] = dx
        dg_ref[...] += dgain

    return _call(
        body, "final_loss", (S // tm,),
        [_rows(tm, D), _whole(gain), _rows(tm, D)],
        [_rows(tm, D), pl.BlockSpec((1, LANES), lambda i: (0, 0)), pl.BlockSpec((1, D), lambda i: (0, 0))],
        [_sds((S, D), F32), _sds((1, LANES), F32), _sds((1, D), F32)],
    )(h, gain, target)


def _tn_call(body, name, grid, in_specs, out_spec, out_shape, scratch, operands, prev):
    n = len(operands)
    if prev is None:
        return _call(body, name, grid, in_specs, out_spec, out_shape, scratch)(*operands)
    assert prev.shape == out_shape.shape and prev.dtype == out_shape.dtype

    def wrapped(*refs):
        body(*refs[:n], *refs[n + 1:])

    return _call(wrapped, name, grid, in_specs + [ANY], out_spec, out_shape, scratch, {n: 0})(*operands, prev)


def _transposed(x_ref, xt_ref, first):
    @pl.when(first)
    def _():
        xt_ref[...] = x_ref[...].astype(BF16).T


def _tn_slots(x, dy, prev, rows_total, row_off):
    S, K = x.shape
    B, _, c = dy.shape
    tk = _tile(K, 512, LANES)

    def body(x_ref, dy_ref, o_ref, xt_ref):
        _transposed(x_ref, xt_ref, pl.program_id(1) == 0)
        o_ref[0] = _dot(xt_ref[...], dy_ref[0]).astype(BF16)

    return _tn_call(
        body, "tn_slots", (K // tk, B),
        [pl.BlockSpec((S, tk), lambda i, b: (0, i)), pl.BlockSpec((1, S, c), lambda i, b: (b, 0, 0))],
        pl.BlockSpec((1, tk, c), lambda i, b: (b, row_off // tk + i, 0)),
        _sds((B, rows_total, c), BF16), [pltpu.VMEM((tk, S), BF16)], [x, dy], prev)


def _tn_plain(x, dy, out_dtype=F32):
    S, K = x.shape
    B, _, c = dy.shape
    tk = _tile(K, 512, LANES)
    tn = _tile(c, 1024, LANES)

    def body(x_ref, dy_ref, o_ref, xt_ref):
        _transposed(x_ref, xt_ref, (pl.program_id(1) == 0) & (pl.program_id(2) == 0))
        o_ref[0] = _dot(xt_ref[...], dy_ref[0]).astype(out_dtype)

    return _call(
        body, "tn_plain", (K // tk, B, c // tn),
        [pl.BlockSpec((S, tk), lambda i, b, j: (0, i)), pl.BlockSpec((1, S, tn), lambda i, b, j: (b, 0, j))],
        pl.BlockSpec((1, tk, tn), lambda i, b, j: (b, i, j)),
        _sds((B, K, c), out_dtype), [pltpu.VMEM((tk, S), BF16)],
    )(x, dy)


def _tn_down(at, dh, prev, rows_total, which):
    nb, c, S = at.shape
    D = dh.shape[1]
    r = c // 2
    tn = _tile(D, 512, LANES)

    def body(at_ref, dh_ref, o_ref):
        g = 0.5 * _dot(at_ref[0], dh_ref[...].astype(BF16))
        o_ref[...] = g.astype(BF16).reshape(2, r, tn)

    return _tn_call(
        body, "tn_down", (nb, D // tn),
        [pl.BlockSpec((1, c, S), lambda i, j: (i, 0, 0)), pl.BlockSpec((S, tn), lambda i, j: (0, j))],
        pl.BlockSpec((2, r, tn), lambda i, j: (i, which, j)),
        _sds((N_DEV, rows_total, D), BF16), [], [at, dh], prev)


def _tn_square(x, dy, prev, rows_total, member):
    S, K = x.shape
    N = dy.shape[1]
    r = K // N_DEV
    tk = _tile(K, 512, r)
    tn = _tile(N, 512, LANES)

    def body(x_ref, dy_ref, o_ref, xt_ref):
        _transposed(x_ref, xt_ref, pl.program_id(1) == 0)
        g = _dot(xt_ref[...], dy_ref[...].astype(BF16))
        o_ref[...] = g.astype(BF16).reshape(tk // r, r, tn)

    return _tn_call(
        body, "tn_square", (K // tk, N // tn),
        [pl.BlockSpec((S, tk), lambda i, j: (0, i)), pl.BlockSpec((S, tn), lambda i, j: (0, j))],
        pl.BlockSpec((tk // r, r, tn), lambda i, j: (i, member, j)),
        _sds((N_DEV, rows_total, N), BF16), [pltpu.VMEM((tk, S), BF16)], [x, dy], prev)


def _tn_cols(x, dy, prev, rows_total, row_block):
    S, K = x.shape
    N = dy.shape[1]
    cw = N // N_DEV

    def body(x_ref, dy_ref, o_ref):
        g = _dot(x_ref[...].astype(BF16).T, dy_ref[...])
        for d in range(N_DEV):
            o_ref[d] = g[:, d * cw:(d + 1) * cw].astype(BF16)

    return _tn_call(
        body, "tn_cols", (1,),
        [pl.BlockSpec((S, K), lambda i: (0, 0)), pl.BlockSpec((S, N), lambda i: (0, 0))],
        pl.BlockSpec((N_DEV, K, cw), lambda i: (0, row_block, 0)),
        _sds((N_DEV, rows_total, cw), BF16), [], [x, dy], prev)


def _tn_heads(qn, kvn, dqp, dkv):
    S, QL = qn.shape
    KVL = kvn.shape[1]

    def body(qn_ref, kvn_ref, dq_ref, dkv_ref, o_ref):
        o_ref[0, 0:KVL, :] = _dot(kvn_ref[...].T, dkv_ref[...]).astype(BF16)
        o_ref[0, KVL:KVL + QL, :] = _dot(qn_ref[...].T, dq_ref[...]).astype(BF16)

    head = pl.BlockSpec((S, HEAD_SLOT), lambda h: (0, h))
    return _call(
        body, "tn_heads", (N_DEV,),
        [pl.BlockSpec((S, QL), lambda h: (0, 0)), pl.BlockSpec((S, KVL), lambda h: (0, 0)), head, head],
        pl.BlockSpec((1, KVL + QL, HEAD_SLOT), lambda h: (h, 0, 0)),
        _sds((N_DEV, KVL + QL, HEAD_SLOT), BF16),
    )(qn, kvn, dqp, dkv)


def _ple_bwd(dh, pre, pp, h, rstd, gain, sq_w, after):
    S, D = h.shape
    r = sq_w.shape[1] // 3
    tm = _tile(S, 512, SUBLANES)

    def body(dh_ref, pre_ref, pp_ref, h_ref, r_ref, gain_ref, wpg_ref, *rest):
        o_ref, dpre_ref, dpp_ref, dg_ref = rest[len(after):]

        @pl.when(pl.program_id(0) == 0)
        def _():
            dg_ref[...] = jnp.zeros_like(dg_ref)

        d = dh_ref[...]
        gate = _sig(pre_ref[...])
        dpre = (d * pp_ref[...] * gate * (1.0 - gate)).astype(BF16)
        dpre_ref[...] = dpre
        dpp_ref[...] = (d * gate).astype(BF16)
        dn = _dot_nt(dpre, wpg_ref[...].reshape(N_DEV * r, D))
        dx, dgain = _rms_bwd(dn, h_ref[...], r_ref[...], gain_ref[...])
        o_ref[...] = d + dx
        dg_ref[...] += dgain

    return _call(
        body, "ple_bwd", (S // tm,),
        [_rows(tm, D), _rows(tm, D), _rows(tm, D), _rows(tm, D), _rows(tm, 1), _whole(gain), _slab(sq_w, r, 2)]
        + [ANY] * len(after),
        [_rows(tm, D), _rows(tm, D), _rows(tm, D), pl.BlockSpec((1, D), lambda i: (0, 0))],
        [_sds((S, D), F32), _sds((S, D), BF16), _sds((S, D), BF16), _sds((1, D), F32)],
    )(dh, pre, pp, h, rstd, gain, sq_w, *after)


def _ffn_bwd_act(dh, dn_w, which, jac, after=()):
    S, D = dh.shape
    _, _, c = jac.shape
    nb = N_DEV // 2
    tm = _tile(S, 256, SUBLANES)

    def body(dh_ref, w_ref, jac_ref, *rest):
        dgu_ref = rest[len(after)]
        dhb = dh_ref[...].astype(BF16)
        for d in range(nb):
            da = _dot_nt(dhb, _down_weight(w_ref, d, c))
            dgu_ref[d] = (da * jac_ref[d].astype(F32)).astype(BF16)
            dgu_ref[nb + d] = (da * jac_ref[nb + d].astype(F32)).astype(BF16)

    act = pl.BlockSpec((N_DEV, tm, c), lambda i: (0, i, 0))
    return _call(
        body, "ffn_bwd_act", (S // tm,),
        [_rows(tm, D), _slab(dn_w, c // 2, which), act] + [ANY] * len(after),
        act,
        _sds((N_DEV, S, c), BF16),
    )(dh, dn_w, jac, *after)


def _ffn_bwd_in(dgu, gu_w, which, h, rstd, gain, dh):
    S, D = h.shape
    c = dgu.shape[2]
    tm = _tile(S, 256, SUBLANES)

    def body(dgu_ref, w_ref, h_ref, r_ref, gain_ref, dh_ref, o_ref, dgain_ref):
        @pl.when(pl.program_id(0) == 0)
        def _():
            dgain_ref[...] = jnp.zeros_like(dgain_ref)

        dn = _dot_nt(dgu_ref[0], w_ref[0])
        for d in range(1, N_DEV):
            dn = dn + _dot_nt(dgu_ref[d], w_ref[d])
        dx, dgain = _rms_bwd(dn, h_ref[...], r_ref[...], gain_ref[...])
        o_ref[...] = dh_ref[...] + dx
        dgain_ref[...] += dgain

    return _call(
        body, "ffn_bwd_in", (S // tm,),
        [pl.BlockSpec((N_DEV, tm, c), lambda i: (0, i, 0)), _slab(gu_w, D, which), _rows(tm, D), _rows(tm, 1),
         _whole(gain), _rows(tm, D)],
        [_rows(tm, D), pl.BlockSpec((1, D), lambda i: (0, 0))],
        [_sds((S, D), F32), _sds((1, D), F32)],
    )(dgu, gu_w, h, rstd, gain, dh)


def _merge_bwd(dh, z_gg, yconv, ymla, sq_w, c128_w, C, after):
    S, D = dh.shape
    r = sq_w.shape[1] // 3
    HV = N_DEV * r
    tm = _tile(S, 512, SUBLANES)

    def body(dh_ref, gg_ref, yc_ref, ym_ref, wmo_ref, wo_ref, wco_ref, *rest):
        dgg_ref, dby_ref, do_ref, dyc_ref, dym_ref = rest[len(after):]
        dm = _dot_nt(dh_ref[...].astype(BF16), wo_ref[...].reshape(HV, D))
        gg = gg_ref[...]
        sgc = _sig(gg[:, :D])
        sgm = _sig(gg[:, D:])
        dyc = (dm * sgc).astype(BF16)
        dym = (dm * sgm).astype(BF16)
        dyc_ref[...] = dyc
        dym_ref[...] = dym
        dgg_ref[:, :D] = (dm * yc_ref[...].astype(F32) * sgc * (1.0 - sgc)).astype(BF16)
        dgg_ref[:, D:] = (dm * ym_ref[...].astype(F32) * sgm * (1.0 - sgm)).astype(BF16)
        dby_ref[...] = _dot_nt(dyc, _cat_slots(wco_ref))
        do_ref[...] = _dot_nt(dym, wmo_ref[...].reshape(HV, D)).astype(BF16)

    return _call(
        body, "merge_bwd", (S // tm,),
        [_rows(tm, D), _rows(tm, 2 * D), _rows(tm, D), _rows(tm, D), _slab(sq_w, r, 0), _slab(sq_w, r, 1),
         _slab(c128_w, C, 0)] + [ANY] * len(after),
        [_rows(tm, 2 * D), _rows(tm, C), _rows(tm, HV), _rows(tm, D), _rows(tm, D)],
        [_sds((S, 2 * D), BF16), _sds((S, C), F32), _sds((S, HV), BF16), _sds((S, D), BF16), _sds((S, D), BF16)],
    )(dh, z_gg, yconv, ymla, sq_w, sq_w, c128_w, *after)


def _conv_bwd(z_bcv, conv_w, dby):
    _, S, C = z_bcv.shape

    def body(z_ref, w_ref, dby_ref, dz_ref, dw_ref):
        w = w_ref[...]
        c = z_ref[1]
        v = z_ref[2]
        d = dby_ref[...]
        zc = c * v
        z1, z2 = _conv_taps(zc)
        y = w[0:1] * z2 + w[1:2] * z1 + w[2:3] * zc
        dz_ref[0] = (d * y).astype(BF16)
        dy = d * z_ref[0]
        rows = lax.broadcasted_iota(jnp.int32, dy.shape, 0)
        dy1 = jnp.where(rows < S - 1, pltpu.roll(dy, S - 1, 0), 0.0)
        dy2 = jnp.where(rows < S - 2, pltpu.roll(dy, S - 2, 0), 0.0)
        dzc = w[2:3] * dy + w[1:2] * dy1 + w[0:1] * dy2
        dz_ref[1] = (dzc * v).astype(BF16)
        dz_ref[2] = (dzc * c).astype(BF16)
        dw_ref[0:1, :] = jnp.sum(dy * z2, axis=0, keepdims=True)
        dw_ref[1:2, :] = jnp.sum(dy * z1, axis=0, keepdims=True)
        dw_ref[2:3, :] = jnp.sum(dy * zc, axis=0, keepdims=True)

    three = pl.BlockSpec((3, S, LANES), lambda j: (0, 0, j))
    wspec = pl.BlockSpec((3, LANES), lambda j: (0, j))
    return _call(
        body, "conv_bwd", (C // LANES,),
        [three, wspec, pl.BlockSpec((S, LANES), lambda j: (0, j))],
        [three, wspec],
        [_sds((3, S, C), BF16), _sds((3, C), F32)],
    )(z_bcv, conv_w, dby)


def _attn_delta(do, o, H):
    S = do.shape[0]
    t = _tile(S, 512, LANES)

    def body(do_ref, o_ref, dl_ref):
        prod = do_ref[...].astype(F32) * o_ref[...].astype(F32)
        dl_ref[0] = jnp.broadcast_to(jnp.sum(prod.T, axis=0, keepdims=True), (SUBLANES, t))

    blk = pl.BlockSpec((t, VDIM), lambda h, i: (i, h))
    return _call(body, "attn_delta", (H, S // t), [blk, blk],
                 pl.BlockSpec((1, SUBLANES, t), lambda h, i: (h, 0, i)), _sds((H, SUBLANES, S), F32))(do, o)


def _attn_bwd(q, k, v, do, lse, delta, H):
    S = q.shape[0]
    t = _tile(S, 512, CHUNK)
    nk = S // t

    def body(q_ref, k_ref, v_ref, do_ref, lse_ref, dl_ref, dq_ref, dk_ref, dv_ref, dqt_ref):
        kj = pl.program_id(1)

        @pl.when(kj == 0)
        def _():
            dqt_ref[...] = jnp.zeros_like(dqt_ref)

        kv = k_ref[...]
        vv = v_ref[...]
        kt = kv.T

        def block(qi, carry, masked):
            dk, dv = carry
            off = pl.multiple_of(qi * t, t)
            qv = q_ref[pl.ds(off, t), :]
            dov = do_ref[pl.ds(off, t), :]
            s = _dot_nt(kv, qv) * SCORE_SCALE
            if masked:
                s = jnp.where(_chunk_mask(t), s, -1e30)
            p = jnp.exp2(s - lse_ref[0, 0:1, pl.ds(off, t)])
            dp = _dot_nt(vv, dov)
            ds = (p * (dp - dl_ref[0, 0:1, pl.ds(off, t)]) * ATTN_SCALE).astype(BF16)
            dqt_ref[:, pl.ds(off, t)] += _dot(kt, ds)
            return dk + _dot(ds, qv), dv + _dot(p.astype(BF16), dov)

        init = (jnp.zeros((t, HEAD_SLOT), F32), jnp.zeros((t, VDIM), F32))
        carry = block(kj, init, True)
        dk, dv = lax.fori_loop(kj + 1, nk, lambda qi, c: block(qi, c, False), carry)
        dk_ref[...] = dk
        dv_ref[...] = dv.astype(BF16)

        @pl.when(kj == nk - 1)
        def _():
            dq_ref[...] = dqt_ref[...].T

    kspec = lambda w: pl.BlockSpec((t, w), lambda h, j: (j, h))
    qspec = lambda w: pl.BlockSpec((S, w), lambda h, j: (0, h))
    stat = pl.BlockSpec((1, SUBLANES, S), lambda h, j: (h, 0, 0))
    return _call(
        body, "attn_bwd", (H, nk),
        [qspec(HEAD_SLOT), kspec(HEAD_SLOT), kspec(VDIM), qspec(VDIM), stat, stat],
        [qspec(HEAD_SLOT), kspec(HEAD_SLOT), kspec(VDIM)],
        [_sds((S, H * HEAD_SLOT), F32), _sds((S, H * HEAD_SLOT), F32), _sds((S, H * VDIM), BF16)],
        [pltpu.VMEM((HEAD_SLOT, S), F32)],
    )(q, k, v, do, lse, delta)


def _mla_prep_bwd(dq, dk, dv, z_qkr, rq, rkv, gq, gkv, cs, c256_w):
    S = z_qkr.shape[0]
    QL, KVL = gq.shape[1], gkv.shape[1]
    H = N_DEV
    tm = _tile(S, 256, SUBLANES)
    half = ROPE // 2

    def body(dq_ref, dk_ref, dv_ref, z_ref, rq_ref, rkv_ref, gq_ref, gkv_ref, cs_ref, w_ref,
             dz_ref, dqp_ref, dkv_ref, dgq_ref, dgkv_ref):
        @pl.when(pl.program_id(0) == 0)
        def _():
            dgq_ref[...] = jnp.zeros_like(dgq_ref)
            dgkv_ref[...] = jnp.zeros_like(dgkv_ref)

        cs_t = cs_ref[...]
        dkr = jnp.zeros((tm, LANES), F32)
        dqn = jnp.zeros((tm, QL), F32)
        dkvn = jnp.zeros((tm, KVL), F32)
        for h in range(H):
            lo, mid, hi = h * HEAD_SLOT, h * HEAD_SLOT + LANES, (h + 1) * HEAD_SLOT
            dqp_ref[:, lo:mid] = dq_ref[:, lo:mid].astype(BF16)
            dqp_ref[:, mid:hi] = _unrope(dq_ref[:, mid:hi], cs_t, half).astype(BF16)
            dkv_ref[:, lo:mid] = dk_ref[:, lo:mid].astype(BF16)
            dkv_ref[:, mid:hi] = dv_ref[:, h * VDIM:(h + 1) * VDIM]
            dkr = dkr + dk_ref[:, mid:hi]
            dqn = dqn + _dot_nt(dqp_ref[:, lo:hi], w_ref[h, KVL:KVL + QL, :])
            dkvn = dkvn + _dot_nt(dkv_ref[:, lo:hi], w_ref[h, 0:KVL, :])
        z = z_ref[...]
        dqc, dgq = _rms_bwd(dqn, z[:, :QL], rq_ref[...], gq_ref[...])
        dkvc, dgkv = _rms_bwd(dkvn, z[:, QL:QL + KVL], rkv_ref[...], gkv_ref[...])
        dz_ref[:, :QL] = dqc.astype(BF16)
        dz_ref[:, QL:QL + KVL] = dkvc.astype(BF16)
        dz_ref[:, QL + KVL:] = _unrope(dkr, cs_t, half).astype(BF16)
        dgq_ref[...] += dgq
        dgkv_ref[...] += dgkv

    W = z_qkr.shape[1]
    return _call(
        body, "mla_prep_bwd", (S // tm,),
        [_rows(tm, H * HEAD_SLOT), _rows(tm, H * HEAD_SLOT), _rows(tm, H * VDIM), _rows(tm, W), _rows(tm, 1),
         _rows(tm, 1), _whole(gq), _whole(gkv), _rows(tm, 3 * LANES), _whole(c256_w)],
        [_rows(tm, W), _rows(tm, H * HEAD_SLOT), _rows(tm, H * HEAD_SLOT), _whole(gq), _whole(gkv)],
        [_sds((S, W), BF16), _sds((S, H * HEAD_SLOT), BF16), _sds((S, H * HEAD_SLOT), BF16),
         _sds((1, QL), F32), _sds((1, KVL), F32)],
    )(dq, dk, dv, z_qkr, rq, rkv, gq, gkv, cs, c256_w)


def _mix_in_bwd(d_bcv, dz_qkr, dgg, w_bcv, w_qkr, w_gg, h, rstd, gain, dh):
    S, D = h.shape
    C = d_bcv.shape[2]
    tm = _tile(S, 512, SUBLANES)

    def body(db_ref, dq_ref, dgg_ref, wb_ref, wq_ref, wg_ref, h_ref, r_ref, gain_ref, dh_ref, o_ref, dgain_ref):
        @pl.when(pl.program_id(0) == 0)
        def _():
            dgain_ref[...] = jnp.zeros_like(dgain_ref)

        dn = _dot_nt(dq_ref[...], wq_ref[...]) + _dot_nt(dgg_ref[...], wg_ref[...])
        for k in range(3):
            dn = dn + _dot_nt(db_ref[k], wb_ref[k])
        dx, dgain = _rms_bwd(dn, h_ref[...], r_ref[...], gain_ref[...])
        o_ref[...] = dh_ref[...] + dx
        dgain_ref[...] += dgain

    return _call(
        body, "mix_in_bwd", (S // tm,),
        [pl.BlockSpec((3, tm, C), lambda i: (0, i, 0)), _rows(tm, dz_qkr.shape[1]), _rows(tm, dgg.shape[1]),
         _whole(w_bcv), _whole(w_qkr), _whole(w_gg), _rows(tm, D), _rows(tm, 1), _whole(gain), _rows(tm, D)],
        [_rows(tm, D), pl.BlockSpec((1, D), lambda i: (0, 0))],
        [_sds((S, D), F32), _sds((1, D), F32)],
    )(d_bcv, dz_qkr, dgg, w_bcv, w_qkr, w_gg, h, rstd, gain, dh)


def _rope_tables(positions):
    half = ROPE // 2
    inv_freq = ROPE_THETA ** (-jnp.arange(0, ROPE, 2, dtype=F32) / ROPE)
    ang = positions.astype(F32)[:, None] * inv_freq
    cos, sin = jnp.cos(ang), jnp.sin(ang)
    z = jnp.zeros_like(cos)
    pad = jnp.zeros((positions.shape[0], LANES - 2 * half), F32)
    return jnp.concatenate([cos, cos, pad, -sin, z, pad, z, sin, pad], axis=1)


def _grad_rows(w):
    return dict(gu=2 * w["gu1"].shape[1], dn=2 * w["dn1"].shape[1], sq=w["sq"].shape[1], win=w["win"].shape[1],
                c128=w["c128"].shape[1], c256=w["c256"].shape[1])


def _layer_fwd(h0, p_l, cs, w, sm, late):
    C = sm["conv_w"].shape[1]
    QL, KVL = sm["q_norm"].shape[1], sm["kv_norm"].shape[1]
    jac1, a1, at1, n1, r1 = _ffn_up(h0, sm["ffn1_norm"], w["gu1"], 0)
    h1 = _ffn_down(a1, w["dn1"], 0, h0)
    if late is not None:
        w.update(late(h1))
    w_bcv, w_qkr, w_gg = _win_split(w["win"], C, QL, KVL)
    z_bcv, z_qkr, z_gg, un, rm = _mix_in(h1, sm["mix_norm"], w_bcv, w_qkr, w_gg)
    by = _conv_fwd(z_bcv, sm["conv_w"])
    q, k, v, qn, kvn, rq, rkv = _mla_prep(z_qkr, sm["q_norm"], sm["kv_norm"], cs, w["c256"])
    o, lse = _attn_fwd(q, k, v, N_DEV)
    h2, merged, yconv, ymla = _merge_wo(o, by, z_gg, h1, w["sq"], w["c128"])
    jac2, a2, at2, n2, r2 = _ffn_up(h2, sm["ffn2_norm"], w["gu2"], 0)
    h3 = _ffn_down(a2, w["dn2"], 0, h2)
    h4, pre, pp, pn, rp = _ple_fwd(h3, sm["ple_norm"], p_l, w["sq"], w["c128"], C)
    saved = dict(h0=h0, jac1=jac1, at1=at1, n1=n1, r1=r1, h1=h1, w_bcv=w_bcv, w_qkr=w_qkr, w_gg=w_gg, z_bcv=z_bcv,
                 z_qkr=z_qkr, z_gg=z_gg, un=un, rm=rm, by=by, q=q, k=k, v=v, qn=qn, kvn=kvn, rq=rq, rkv=rkv, o=o,
                 lse=lse, h2=h2, merged=merged, yconv=yconv, ymla=ymla, jac2=jac2, at2=at2, n2=n2, r2=r2, h3=h3,
                 pre=pre, pp=pp, pn=pn, rp=rp, p=p_l)
    return h4, saved


def _layer_bwd_late(dh4, s, w, sm, after):
    D = dh4.shape[1]
    C = sm["conv_w"].shape[1]
    P = s["p"].shape[1]
    rows = _grad_rows(w)
    small = {}
    dh3, dpre, dpp, small["ple_norm"] = _ple_bwd(dh4, s["pre"], s["pp"], s["h3"], s["rp"], sm["ple_norm"], w["sq"],
                                                 after)
    g_sq = _tn_square(s["pn"], dpre, None, rows["sq"], 2)
    g_c128 = _tn_cols(s["p"], dpp, None, rows["c128"], C // P)

    dgu2 = _ffn_bwd_act(dh3, w["dn2"], 0, s["jac2"])
    g_dn = _tn_down(s["at2"], dh3, None, rows["dn"], 1)
    g_gu = _tn_slots(s["n2"], dgu2, None, rows["gu"], D)
    dh2, small["ffn2_norm"] = _ffn_bwd_in(dgu2, w["gu2"], 0, s["h2"], s["r2"], sm["ffn2_norm"], dh3)
    return dh2, dict(gu=g_gu, dn=g_dn, sq=g_sq, c128=g_c128), small


def _layer_bwd_mixer(dh2, part, small, s, cs, w, sm, after):
    C = sm["conv_w"].shape[1]
    rows = _grad_rows(w)
    g_gu, g_dn, g_sq, g_c128 = part["gu"], part["dn"], part["sq"], part["c128"]

    dgg, dby, do, dyc, dym = _merge_bwd(dh2, s["z_gg"], s["yconv"], s["ymla"], w["sq"], w["c128"], C, after)
    g_sq = _tn_square(s["merged"], dh2, g_sq, rows["sq"], 1)
    g_sq = _tn_square(s["o"], dym, g_sq, rows["sq"], 0)
    g_c128 = _tn_cols(s["by"], dyc, g_c128, rows["c128"], 0)
    d_bcv, small["conv_w"] = _conv_bwd(s["z_bcv"], sm["conv_w"], dby)
    dq, dk, dv = _attn_bwd(s["q"], s["k"], s["v"], do, s["lse"], _attn_delta(do, s["o"], N_DEV), N_DEV)
    dz_qkr, dqp, dkv, small["q_norm"], small["kv_norm"] = _mla_prep_bwd(
        dq, dk, dv, s["z_qkr"], s["rq"], s["rkv"], sm["q_norm"], sm["kv_norm"], cs, w["c256"])
    g_c256 = _tn_heads(s["qn"], s["kvn"], dqp, dkv)
    un = s["un"]
    g_win = _win_merge(_tn_plain(un, d_bcv), _tn_plain(un, dz_qkr[None])[0], _tn_plain(un, dgg[None])[0],
                       w["win"].shape[2])
    dh1, small["mix_norm"] = _mix_in_bwd(d_bcv, dz_qkr, dgg, s["w_bcv"], s["w_qkr"], s["w_gg"], s["h1"], s["rm"],
                                         sm["mix_norm"], dh2)
    return dh1, dict(gu=g_gu, dn=g_dn, sq=g_sq, win=g_win, c128=g_c128, c256=g_c256), small


def _layer_bwd_first(dh1, part, small, s, w, sm, after):
    rows = _grad_rows(w)
    dgu1 = _ffn_bwd_act(dh1, w["dn1"], 0, s["jac1"], after)
    g_dn = _tn_down(s["at1"], dh1, part["dn"], rows["dn"], 0)
    g_gu = _tn_slots(s["n1"], dgu1, part["gu"], rows["gu"], 0)
    dh0, small["ffn1_norm"] = _ffn_bwd_in(dgu1, w["gu1"], 0, s["h0"], s["r1"], sm["ffn1_norm"], dh1)
    return dh0, dict(part, gu=g_gu, dn=g_dn), small


def _mesh_pos():
    return lax.axis_index("x"), lax.axis_index("y"), lax.axis_index("c")


def _other_chips(x, y):
    return [(1 - x, y), (x, 1 - y), (1 - x, 1 - y)]


def _pack(arrs, width):
    L = arrs[0].shape[0]
    shapes = [a.shape[1:] for a in arrs]
    R = sum(r for r, _ in shapes)

    def body(*refs):
        o_ref = refs[-1]
        off = 0
        for a_ref, (r, c) in zip(refs[:-1], shapes):
            o_ref[0, off:off + r, 0:c] = a_ref[0].astype(BF16)
            if c < width:
                o_ref[0, off:off + r, c:width] = jnp.zeros((r, width - c), BF16)
            off += r

    return _call(
        body, "pack", (L,),
        [pl.BlockSpec((1, r, c), lambda l: (l, 0, 0)) for r, c in shapes],
        pl.BlockSpec((1, R, width), lambda l: (l, 0, 0)),
        _sds((L, R, width), BF16),
    )(*arrs)


def _handshake(peers):
    barrier = pltpu.get_barrier_semaphore()
    for peer in peers:
        pl.semaphore_signal(barrier, inc=1, device_id=peer, device_id_type=MESH)
    pl.semaphore_wait(barrier, len(peers))


def _sequencer_call(body, name, out_types, sems, collective_id, operands):
    return pl.kernel(
        body, name=name, out_type=out_types,
        mesh=plsc.ScalarSubcoreMesh(axis_name="seq", num_cores=1),
        scratch_types=tuple(pltpu.SemaphoreType.DMA((k,)) for k in sems),
        compiler_params=pltpu.CompilerParams(collective_id=collective_id),
    )(*operands)


def _all_gather(packs, l, after, collective_id):
    n = len(packs)

    def body(*refs):
        ins, outs = refs[:n], refs[n + len(after):2 * n + len(after)]
        send_sems, recv_sems, local_sems = refs[2 * n + len(after):]
        x, y, c = _mesh_pos()
        me, sibling = (x, y, c), (x, y, 1 - c)
        chips = _other_chips(x, y)
        _handshake([sibling] + [(*chip, c) for chip in chips])

        def copy(q, k, block, to, src=None):
            slot = outs[q].at[4 * block[0] + 2 * block[1] + block[2]]
            return pltpu.make_async_remote_copy(
                src_ref=slot if src is None else src, dst_ref=slot,
                send_sem=send_sems.at[7 * q + k], recv_sem=recv_sems.at[7 * q + k], device_id=to, device_id_type=MESH)

        started = []
        for q in range(n):
            src = ins[q].at[l]
            mine = pltpu.make_async_copy(src, outs[q].at[4 * x + 2 * y + c], local_sems.at[q])
            mine.start()
            started.append(mine)
        sends = []
        for q in range(n):
            src = ins[q].at[l]
            sends.append(copy(q, 0, me, sibling, src=src))
            sends += [copy(q, 1 + j, me, (*chip, c), src=src) for j, chip in enumerate(chips)]
        for cp in sends:
            cp.start()
        for q in range(n):
            for j, chip in enumerate(chips):
                copy(q, 1 + j, (*chip, c), me).wait_recv()
                fwd = copy(q, 4 + j, (*chip, c), sibling)
                fwd.start()
                sends.append(fwd)
        for q in range(n):
            copy(q, 0, sibling, me).wait_recv()
            for j, chip in enumerate(chips):
                copy(q, 4 + j, (*chip, 1 - c), me).wait_recv()
        for cp in sends:
            cp.wait_send()
        for mine in started:
            mine.wait()

    return _sequencer_call(
        body, f"all_gather_{collective_id}", [_sds((N_DEV,) + p.shape[1:], p.dtype) for p in packs], (7 * n, 7 * n, n),
        collective_id, list(packs) + list(after))


def _rs_d2d(gs, l, collective_id):
    n = len(gs)

    def body(*refs):
        ins, outs = refs[:n], refs[n:2 * n]
        send_sems, recv_sems = refs[2 * n:]
        x, y, c = _mesh_pos()
        _handshake([(x, y, 1 - c)])
        copies = []
        for q in range(n):
            for j in range(4):
                copies.append(pltpu.make_async_remote_copy(
                    src_ref=ins[q].at[2 * j + (1 - c)], dst_ref=outs[q].at[j], send_sem=send_sems.at[4 * q + j],
                    recv_sem=recv_sems.at[4 * q + j], device_id=(x, y, 1 - c), device_id_type=MESH))
        for cp in copies:
            cp.start()
        for cp in copies:
            cp.wait()

    return _sequencer_call(
        body, f"rs_d2d_{l}", [_sds((4,) + g.shape[1:], g.dtype) for g in gs], (4 * n, 4 * n), collective_id, gs)


def _rs_add_chip(gs, as_, after):
    n = len(gs)
    steps = 4
    tiles = [g.shape[1] // steps for g in gs]

    def chip(k):
        x, y, _ = _mesh_pos()
        return ([(x, y)] + _other_chips(x, y))[k]

    def body(*refs):
        g_refs, a_refs = refs[:4 * n], refs[4 * n:8 * n]
        own_refs, t_refs = refs[8 * n + len(after):9 * n + len(after)], refs[9 * n + len(after):]
        for q in range(n):
            g, a = g_refs[4 * q:4 * q + 4], a_refs[4 * q:4 * q + 4]
            own_refs[q][...] = g[0][0].astype(F32) + a[0][0].astype(F32)
            for k in range(1, 4):
                t_refs[q][k - 1] = (g[k][0].astype(F32) + a[k][0].astype(F32)).astype(BF16)

    def gspec(q, k):
        def index(i):
            px, py = chip(k)
            return 4 * px + 2 * py + lax.axis_index("c"), i, 0
        return pl.BlockSpec((1, tiles[q], gs[q].shape[2]), index)

    def aspec(q, k):
        def index(i):
            px, py = chip(k)
            return 2 * px + py, i, 0
        return pl.BlockSpec((1, tiles[q], gs[q].shape[2]), index)

    in_specs = [gspec(q, k) for q in range(n) for k in range(4)] + [aspec(q, k) for q in range(n) for k in range(4)]
    operands = [g for g in gs for _ in range(4)] + [a for a in as_ for _ in range(4)]
    out_specs = [pl.BlockSpec((tiles[q], gs[q].shape[2]), lambda i: (i, 0)) for q in range(n)]
    out_specs += [pl.BlockSpec((3, tiles[q], gs[q].shape[2]), lambda i: (0, i, 0)) for q in range(n)]
    out_shape = [_sds(g.shape[1:], F32) for g in gs] + [_sds((3,) + g.shape[1:], BF16) for g in gs]
    res = _call(body, "rs_add_chip", (steps,), in_specs + [ANY] * len(after), out_specs, out_shape)(*operands, *after)
    return res[:n], res[n:]


def _rs_ici(ts, l, collective_id):
    n = len(ts)

    def body(*refs):
        ins, outs = refs[:n], refs[n:2 * n]
        send_sems, recv_sems = refs[2 * n:]
        x, y, c = _mesh_pos()
        chips = _other_chips(x, y)
        _handshake([(*chip, c) for chip in chips])
        copies = []
        for q in range(n):
            for k, chip in enumerate(chips):
                copies.append(pltpu.make_async_remote_copy(
                    src_ref=ins[q].at[k], dst_ref=outs[q].at[k], send_sem=send_sems.at[3 * q + k],
                    recv_sem=recv_sems.at[3 * q + k], device_id=(*chip, c), device_id_type=MESH))
        for cp in copies:
            cp.start()
        for cp in copies:
            cp.wait()

    return _sequencer_call(
        body, f"rs_ici_{l}", [_sds(t.shape, t.dtype) for t in ts], (3 * n, 3 * n), collective_id, ts)


def _all_reduce_small(v):
    n, W = v.shape

    def body(v_ref, out_ref, slots, send_sems, recv_sems):
        x, y, c = _mesh_pos()
        me = 4 * x + 2 * y + c
        slots[me] = v_ref[...]
        copies = []
        for k in range(1, N_DEV):
            kx, ky, kc = (k >> 2) & 1, (k >> 1) & 1, k & 1
            peer = (1 - x if kx else x, 1 - y if ky else y, 1 - c if kc else c)
            copies.append(pltpu.make_async_remote_copy(
                src_ref=v_ref, dst_ref=slots.at[me], send_sem=send_sems.at[k - 1], recv_sem=recv_sems.at[k - 1],
                device_id=peer, device_id_type=MESH))
        for cp in copies:
            cp.start()
        for cp in copies:
            cp.wait()
        acc = slots[0]
        for d in range(1, N_DEV):
            acc = acc + slots[d]
        out_ref[...] = acc

    vm = pl.BlockSpec(memory_space=pltpu.VMEM)
    return pl.pallas_call(
        body, name="all_reduce_small",
        out_shape=_sds((n, W), F32),
        in_specs=[vm], out_specs=vm,
        scratch_shapes=[pltpu.VMEM((N_DEV, n, W), F32), pltpu.SemaphoreType.DMA((7,)), pltpu.SemaphoreType.DMA((7,))],
    )(v)


def _adamw_math(w, g, m, v):
    m2 = ADAM_B1 * m + (1.0 - ADAM_B1) * g
    v2 = ADAM_B2 * v + (1.0 - ADAM_B2) * (g * g)
    m_hat = m2 / (1.0 - ADAM_B1 ** ADAM_STEP)
    v_hat = v2 / (1.0 - ADAM_B2 ** ADAM_STEP)
    return -ADAM_LR * (m_hat / (jnp.sqrt(v_hat) + ADAM_EPS) + ADAM_WD * w), m2, v2


def _adamw(w, g, m, v):
    L, r, c = w.shape
    tr = _tile(r, max(SUBLANES, (256 * 1024 // c) // SUBLANES * SUBLANES), SUBLANES)

    def body(w_ref, g_ref, m_ref, v_ref, d_ref, nm_ref, nv_ref):
        d_ref[...], nm_ref[...], nv_ref[...] = _adamw_math(w_ref[...], g_ref[...], m_ref[...], v_ref[...])

    spec = pl.BlockSpec((1, tr, c), lambda l, i: (l, i, 0))
    return _call(body, "adamw", (L, r // tr), [spec] * 4, [spec] * 3, [_sds((L, r, c), F32)] * 3)(w, g, m, v)


def _adamw_reduced(w, m, v, own, b, row_off, tr, l, prev, after):
    L, r, c = w.shape
    W = own.shape[1]
    ob = row_off // tr
    extra = list(prev or ()) + list(after)

    def body(w_ref, m_ref, v_ref, own_ref, b_ref, *rest):
        g_ref, d_ref, nm_ref, nv_ref = rest[len(extra):]
        g = ((own_ref[...] + b_ref[0].astype(F32)) + b_ref[1].astype(F32)) + b_ref[2].astype(F32)
        g = g[:, :c]
        g_ref[0] = g
        d_ref[0], nm_ref[0], nv_ref[0] = _adamw_math(w_ref[0], g, m_ref[0], v_ref[0])

    spec = pl.BlockSpec((1, tr, c), lambda i: (l, i, 0))
    return _call(
        body, "adamw_reduced", (r // tr,),
        [spec] * 3 + [pl.BlockSpec((tr, W), lambda i: (ob + i, 0)), pl.BlockSpec((3, tr, W), lambda i: (0, ob + i, 0))]
        + [ANY] * len(extra),
        [spec] * 4, [_sds((L, r, c), F32)] * 4,
        aliases={5 + k: k for k in range(4)} if prev else None,
    )(w, m, v, own, b, *extra)


_MEMBERS = dict(gu=("ffn1_w_gu", "ffn2_w_gu"), dn=("ffn1_w_down", "ffn2_w_down"),
                sq=("w_mla_out", "w_o", "w_ple_gate"), win=("w_in",), c128=("w_conv_out", "w_ple_proj"),
                c256=("w_ukv", "w_uq"))
_GATHER_MEMBERS = dict(_MEMBERS, gu1=("ffn1_w_gu",), gu2=("ffn2_w_gu",), dn1=("ffn1_w_down",), dn2=("ffn2_w_down",))
GATHER_STAGES = (("gu1", "dn1"), ("win", "c256", "c128", "sq"), ("gu2", "dn2"))
_SMALL = ("ffn1_norm", "mix_norm", "q_norm", "kv_norm", "ffn2_norm", "ple_norm")
_ORDER = ("ffn1_norm", "ffn1_w_gu", "ffn1_w_down", "mix_norm", "w_in", "conv_w", "w_conv_out", "q_norm", "kv_norm",
          "w_uq", "w_ukv", "w_mla_out", "w_o", "ffn2_norm", "ffn2_w_gu", "ffn2_w_down", "ple_norm", "w_ple_gate",
          "w_ple_proj", "final_norm")


def _class_width(wts, cls):
    return HEAD_SLOT if cls == "c256" else wts[_GATHER_MEMBERS[cls][0]].shape[2]


def _pack_rows(vecs, width):
    flat = jnp.concatenate([a.reshape(-1) for a in vecs])
    n = flat.shape[0]
    rows = -(-n // width)
    rows = -(-rows // SUBLANES) * SUBLANES
    flat = jnp.pad(flat, (0, rows * width - n))
    offs, o = [], 0
    for a in vecs:
        offs.append(o)
        o += a.size
    return flat.reshape(rows, width), offs


def _unpack_rows(packed, vecs, offs):
    flat = packed.reshape(-1)
    return [flat[o:o + a.size].reshape(a.shape) for a, o in zip(vecs, offs)]


def _train(x, p, positions, target, gathered, packs, small_w, final_norm, update):
    cs = _rope_tables(positions)
    L = len(small_w)
    h = x
    saved = []
    def gather(l, names, after, collective_id):
        got = _all_gather([packs[n] for n in names], l, after, collective_id)
        return dict(zip(names, got))

    late = None
    if packs is not None:
        first, mixer, second = GATHER_STAGES
        w0 = gather(0, first, [], 0)
        w0.update(gather(0, mixer, [w0[first[0]]], 1))
        gathered = [w0]
        late = lambda h1: gather(0, second, [h1], 2)
    everything = sum(GATHER_STAGES, ())
    for l in range(L):
        h, s = _layer_fwd(h, p[l], cs, gathered[l], small_w[l], late)
        late = None
        saved.append(s)
        if packs is not None and l + 1 < L:
            gathered.append(gather(l + 1, everything, [s["by"]], 2 + l + 1))
    dh, loss, d_final = _final_loss(h, final_norm, target)
    grads, smalls = [None] * L, [None] * L
    exchanged = None
    landing = None

    def second_stage(after):
        l, gs, as_ = exchanged
        owns, ts = _rs_add_chip(gs, as_, [after])
        return l, owns, _rs_ici(ts, l, 2 * L + 2 + l)

    for l in reversed(range(L)):
        dh, part, small = _layer_bwd_late(dh, saved[l], gathered[l], small_w[l], [])
        pin = []
        if exchanged is not None:
            landing = second_stage(dh)
            pin = [landing[1][0]]
        dh, part, small = _layer_bwd_mixer(dh, part, small, saved[l], cs, gathered[l], small_w[l], pin)
        pin = [update(*landing)] if exchanged is not None else []
        dh, g, smalls[l] = _layer_bwd_first(dh, part, small, saved[l], gathered[l], small_w[l], pin)
        if update is not None:
            gs = [g[cls] for cls in CLASSES]
            exchanged = (l, gs, _rs_d2d(gs, l, L + 2 + l))
        else:
            grads[l] = g
    if update is not None:
        update(*second_stage(dh))
    return loss[0, 0], dh, grads, smalls, d_final


def kernel(x, p, positions, ffn1_norm, ffn1_w_gu, ffn1_w_down, mix_norm, w_in, conv_w, w_conv_out, q_norm, kv_norm, w_uq, w_ukv, w_mla_out, w_o, ffn2_norm, ffn2_w_gu, ffn2_w_down, ple_norm, w_ple_gate, w_ple_proj, final_norm, loss_target, m_ffn1_norm, m_ffn1_w_gu, m_ffn1_w_down, m_mix_norm, m_w_in, m_conv_w, m_w_conv_out, m_q_norm, m_kv_norm, m_w_uq, m_w_ukv, m_w_mla_out, m_w_o, m_ffn2_norm, m_ffn2_w_gu, m_ffn2_w_down, m_ple_norm, m_w_ple_gate, m_w_ple_proj, m_final_norm, v_ffn1_norm, v_ffn1_w_gu, v_ffn1_w_down, v_mix_norm, v_w_in, v_conv_w, v_w_conv_out, v_q_norm, v_kv_norm, v_w_uq, v_w_ukv, v_w_mla_out, v_w_o, v_ffn2_norm, v_ffn2_w_gu, v_ffn2_w_down, v_ple_norm, v_w_ple_gate, v_w_ple_proj, v_final_norm):
    args = dict(locals())
    wts = {n: args[n] for n in _ORDER}
    L = w_in.shape[0]
    dev = 4 * lax.axis_index("x") + 2 * lax.axis_index("y") + lax.axis_index("c")

    packs = {cls: _pack([wts[n] for n in _GATHER_MEMBERS[cls]], _class_width(wts, cls))
             for stage in GATHER_STAGES for cls in stage}
    cw = conv_w.shape[2]
    conv_full = lax.dynamic_update_slice(jnp.zeros((L, 3, N_DEV * cw), F32), conv_w, (0, 0, dev * cw))
    conv_packed, conv_offs = _pack_rows([conv_full], FLAT_COLS)
    conv_full = _unpack_rows(_all_reduce_small(conv_packed), [conv_full], conv_offs)[0]
    small_w = [dict({n: wts[n][l][None, :] for n in _SMALL}, conv_w=conv_full[l]) for l in range(L)]

    done = {}

    def update(l, owns, bs):
        for q, cls in enumerate(CLASSES):
            off = 0
            rows = [wts[n].shape[1] for n in _MEMBERS[cls]]
            tr = _tile(math.gcd(*rows), 256, BF16_ROWS)
            for n, r in zip(_MEMBERS[cls], rows):
                done[n] = _adamw_reduced(wts[n], args["m_" + n], args["v_" + n], owns[q], bs[q], off, tr, l,
                                         done.get(n), [])
                off += r
        return done[_MEMBERS[CLASSES[-1]][-1]][0]

    loss_dev, grad_x, _, smalls, d_final = _train(x[0], p[:, 0], positions[0], loss_target[0], None, packs, small_w,
                                                  final_norm[None, :], update)

    small = [jnp.stack([smalls[l][n][0] for l in range(L)]) for n in _SMALL]
    small += [jnp.stack([smalls[l]["conv_w"] for l in range(L)]), d_final[0], loss_dev[None]]
    packed, offs = _pack_rows(small, FLAT_COLS)
    small = _unpack_rows(_all_reduce_small(packed), small, offs)
    grad = dict(zip(_SMALL, small))
    grad["conv_w"] = lax.dynamic_slice(small[len(_SMALL)], (0, 0, dev * cw), (L, 3, cw))
    grad["final_norm"] = small[-2]
    loss = small[-1][0]

    deltas, new_m, new_v = {}, {}, {}
    for n, (g, d, nm, nv) in done.items():
        grad[n], deltas[n], new_m[n], new_v[n] = g, d, nm, nv
    for n in _SMALL + ("conv_w", "final_norm"):
        w3 = wts[n].reshape((1,) * (3 - wts[n].ndim) + wts[n].shape)
        d, nm, nv = _adamw(w3, grad[n].reshape(w3.shape), args["m_" + n].reshape(w3.shape),
                           args["v_" + n].reshape(w3.shape))
        deltas[n], new_m[n], new_v[n] = (a.reshape(wts[n].shape) for a in (d, nm, nv))
    return (loss, grad_x[None], *[grad[n] for n in _ORDER], *[deltas[n] for n in _ORDER],
            *[new_m[n] for n in _ORDER], *[new_v[n] for n in _ORDER])
```

```python
import math

import jax
import jax.numpy as jnp
from jax import lax
from jax.experimental import pallas as pl
from jax.experimental.pallas import tpu as pltpu
from jax.experimental.pallas import tpu_sc as plsc

F32 = jnp.float32
BF16 = jnp.bfloat16

CHUNK = 64
NOPE = 128
ROPE = 64
VDIM = 128
ROPE_THETA = 10000.0
EPS = 1e-6
ATTN_SCALE = (NOPE + ROPE) ** -0.5
SCORE_SCALE = ATTN_SCALE * math.log2(math.e)
ADAM_LR = 0.001
ADAM_B1 = 0.9
ADAM_B2 = 0.999
ADAM_EPS = 1e-08
ADAM_WD = 0.01
ADAM_STEP = 10

LANES = 128
SUBLANES = 8
BF16_ROWS = 16
V7X_VMEM_BYTES = 64 * 1024 * 1024
VMEM_LIMIT = V7X_VMEM_BYTES * 7 // 8
HEAD_SLOT = 2 * LANES
N_DEV = 8
FLAT_COLS = 1024
CLASSES = ("gu", "dn", "sq", "win", "c128", "c256")

NT = (((1,), (1,)), ((), ()))
MESH = pl.DeviceIdType.MESH
ANY = pl.BlockSpec(memory_space=pl.ANY)


def _dot(a, b):
    return jnp.dot(a, b, preferred_element_type=F32)


def _dot_nt(a, b):
    return lax.dot_general(a, b, NT, preferred_element_type=F32)


def _sig(x):
    return 1.0 / (1.0 + jnp.exp(-x))


def _tile(n, pref, unit):
    if n <= pref:
        return n
    t = (pref // unit) * unit
    while t >= unit:
        if n % t == 0:
            return t
        t -= unit
    return n


def _call(body, name, grid, in_specs, out_specs, out_shape, scratch=(), aliases=None):
    return pl.pallas_call(
        body,
        name=name,
        grid=grid,
        in_specs=in_specs,
        out_specs=out_specs,
        out_shape=out_shape,
        scratch_shapes=list(scratch),
        input_output_aliases=aliases or {},
        compiler_params=pltpu.CompilerParams(
            dimension_semantics=("arbitrary",) * len(grid), vmem_limit_bytes=VMEM_LIMIT
        ),
    )


def _sds(shape, dtype):
    return jax.ShapeDtypeStruct(shape, dtype)


def _rms_fwd(x, gain):
    rstd = lax.rsqrt(jnp.mean(x * x, axis=-1, keepdims=True) + EPS)
    return x * rstd * gain, rstd


def _rms_bwd(dn, x, rstd, gain):
    xhat = x * rstd
    dgy = dn * gain
    dx = rstd * (dgy - xhat * jnp.mean(dgy * xhat, axis=-1, keepdims=True))
    return dx, jnp.sum(dn * xhat, axis=0, keepdims=True)


def _rows(tm, w):
    return pl.BlockSpec((tm, w), lambda i: (i, 0))


def _whole(a):
    nd = a.ndim
    return pl.BlockSpec(a.shape, lambda i: (0,) * nd)


def _slab(buf, rows, index):
    return pl.BlockSpec((N_DEV, rows, buf.shape[2]), lambda i: (0, index, 0))


def _cat_slots(w):
    return jnp.concatenate([w[d] for d in range(N_DEV)], axis=1)


def _ffn_up(h, gain, gu_w, which):
    S, D = h.shape
    c = gu_w.shape[2]
    tm = _tile(S, 256, SUBLANES)
    nb = N_DEV // 2

    def body(h_ref, gain_ref, w_ref, jac_ref, a_ref, at_ref, n_ref, r_ref):
        n32, rstd = _rms_fwd(h_ref[...], gain_ref[...])
        n = n32.astype(BF16)
        n_ref[...] = n
        r_ref[...] = rstd
        for d in range(nb):
            g = _dot(n, w_ref[d])
            u = _dot(n, w_ref[nb + d])
            sg = _sig(g)
            silu = g * sg
            a = (silu * u).astype(BF16)
            a_ref[d] = a
            at_ref[d] = a.T
            jac_ref[d] = (0.5 * u * (sg + silu * (1.0 - sg))).astype(BF16)
            jac_ref[nb + d] = (0.5 * silu).astype(BF16)

    return _call(
        body, "ffn_up", (S // tm,),
        [_rows(tm, D), _whole(gain), _slab(gu_w, D, which)],
        [pl.BlockSpec((N_DEV, tm, c), lambda i: (0, i, 0)), pl.BlockSpec((nb, tm, c), lambda i: (0, i, 0)),
         pl.BlockSpec((nb, c, tm), lambda i: (0, 0, i)), _rows(tm, D), _rows(tm, 1)],
        [_sds((N_DEV, S, c), BF16), _sds((nb, S, c), BF16), _sds((nb, c, S), BF16), _sds((S, D), BF16),
         _sds((S, 1), F32)],
    )(h, gain, gu_w)


def _down_weight(w_ref, d, c):
    return w_ref[2 * d:2 * d + 2].reshape(c, w_ref.shape[2])


def _ffn_down(a, dn_w, which, h):
    nb, S, c = a.shape
    D = h.shape[1]
    tm = _tile(S, 512, SUBLANES)

    def body(a_ref, w_ref, h_ref, o_ref):
        acc = _dot(a_ref[0], _down_weight(w_ref, 0, c))
        for d in range(1, nb):
            acc = acc + _dot(a_ref[d], _down_weight(w_ref, d, c))
        o_ref[...] = h_ref[...] + 0.5 * acc

    return _call(
        body, "ffn_down", (S // tm,),
        [pl.BlockSpec((nb, tm, c), lambda i: (0, i, 0)), _slab(dn_w, c // 2, which), _rows(tm, D)],
        _rows(tm, D),
        _sds((S, D), F32),
    )(a, dn_w, h)


def _win_segments(C, QL, KVL, D):
    o1, o2 = 3 * C, 3 * C + QL + KVL + ROPE
    return [("bcv", k, k * C, (k + 1) * C) for k in range(3)] + [("qkr", None, o1, o2), ("gg", None, o2, o2 + 2 * D)]


def _win_pieces(segments, cw):
    out = []
    for tgt, lead, a, b in segments:
        for d in range(N_DEV):
            lo, hi = max(a, d * cw), min(b, (d + 1) * cw)
            if lo < hi:
                out.append((tgt, lead, d, (lo - d * cw, hi - d * cw), (lo - a, hi - a)))
    return out


def _win_split(win_w, C, QL, KVL):
    _, D, cw = win_w.shape
    WQ = QL + KVL + LANES
    pieces = _win_pieces(_win_segments(C, QL, KVL, D), cw)
    tr = _tile(D, 256, BF16_ROWS)

    def body(w_ref, bcv_ref, qkr_ref, gg_ref):
        tgt = dict(bcv=bcv_ref, qkr=qkr_ref, gg=gg_ref)
        qkr_ref[:, QL + KVL + ROPE:] = jnp.zeros((tr, LANES - ROPE), BF16)
        for name, lead, d, (s0, s1), (t0, t1) in pieces:
            v = w_ref[d, :, s0:s1]
            if lead is None:
                tgt[name][:, t0:t1] = v
            else:
                tgt[name][lead, :, t0:t1] = v

    return _call(
        body, "win_split", (D // tr,),
        [pl.BlockSpec((N_DEV, tr, cw), lambda i: (0, i, 0))],
        [pl.BlockSpec((3, tr, C), lambda i: (0, i, 0)), _rows(tr, WQ), _rows(tr, 2 * D)],
        [_sds((3, D, C), BF16), _sds((D, WQ), BF16), _sds((D, 2 * D), BF16)],
    )(win_w)


def _win_merge(d_bcv, d_qkr, d_gg, cw):
    _, D, C = d_bcv.shape
    WQ = d_qkr.shape[1]
    QL_KVL = WQ - LANES
    o1 = 3 * C
    segments = [("bcv", k, k * C, (k + 1) * C) for k in range(3)]
    segments += [("qkr", None, o1, o1 + QL_KVL + ROPE), ("gg", None, o1 + QL_KVL + ROPE, o1 + QL_KVL + ROPE + 2 * D)]
    pieces = _win_pieces(segments, cw)
    tr = _tile(D, 256, BF16_ROWS)

    def body(bcv_ref, qkr_ref, gg_ref, o_ref):
        src = dict(bcv=bcv_ref, qkr=qkr_ref, gg=gg_ref)
        for name, lead, d, (s0, s1), (t0, t1) in pieces:
            v = src[name][:, t0:t1] if lead is None else src[name][lead, :, t0:t1]
            o_ref[d, :, s0:s1] = v.astype(BF16)

    return _call(
        body, "win_merge", (D // tr,),
        [pl.BlockSpec((3, tr, C), lambda i: (0, i, 0)), _rows(tr, WQ), _rows(tr, 2 * D)],
        pl.BlockSpec((N_DEV, tr, cw), lambda i: (0, i, 0)),
        _sds((N_DEV, D, cw), BF16),
    )(d_bcv, d_qkr, d_gg)


def _mix_in(h, gain, w_bcv, w_qkr, w_gg):
    S, D = h.shape
    C = w_bcv.shape[2]
    tm = _tile(S, 256, SUBLANES)

    def body(h_ref, gain_ref, w1, w2, w3, o1, o2, o3, n_ref, r_ref):
        n32, rstd = _rms_fwd(h_ref[...], gain_ref[...])
        n = n32.astype(BF16)
        n_ref[...] = n
        r_ref[...] = rstd
        for k in range(3):
            o1[k] = _dot(n, w1[k])
        o2[...] = _dot(n, w2[...])
        o3[...] = _dot(n, w3[...])

    return _call(
        body, "mix_in", (S // tm,),
        [_rows(tm, D), _whole(gain), _whole(w_bcv), _whole(w_qkr), _whole(w_gg)],
        [pl.BlockSpec((3, tm, C), lambda i: (0, i, 0)), _rows(tm, w_qkr.shape[1]), _rows(tm, 2 * D),
         _rows(tm, D), _rows(tm, 1)],
        [_sds((3, S, C), F32), _sds((S, w_qkr.shape[1]), F32), _sds((S, 2 * D), F32), _sds((S, D), BF16),
         _sds((S, 1), F32)],
    )(h, gain, w_bcv, w_qkr, w_gg)


def _conv_taps(zc):
    rows = lax.broadcasted_iota(jnp.int32, zc.shape, 0)
    z1 = jnp.where(rows >= 1, pltpu.roll(zc, 1, 0), 0.0)
    z2 = jnp.where(rows >= 2, pltpu.roll(zc, 2, 0), 0.0)
    return z1, z2


def _conv_fwd(z_bcv, conv_w):
    _, S, C = z_bcv.shape

    def body(z_ref, w_ref, o_ref):
        w = w_ref[...]
        zc = z_ref[1] * z_ref[2]
        z1, z2 = _conv_taps(zc)
        y = w[0:1] * z2 + w[1:2] * z1 + w[2:3] * zc
        o_ref[...] = (z_ref[0] * y).astype(BF16)

    return _call(
        body, "conv_fwd", (C // LANES,),
        [pl.BlockSpec((3, S, LANES), lambda j: (0, 0, j)), pl.BlockSpec((3, LANES), lambda j: (0, j))],
        pl.BlockSpec((S, LANES), lambda j: (0, j)),
        _sds((S, C), BF16),
    )(z_bcv, conv_w)


def _rope(x, cs, half):
    c, s1, s2 = cs[:, :LANES], cs[:, LANES:2 * LANES], cs[:, 2 * LANES:]
    return x * c + pltpu.roll(x, LANES - half, 1) * s1 + pltpu.roll(x, half, 1) * s2


def _unrope(d, cs, half):
    c, s1, s2 = cs[:, :LANES], cs[:, LANES:2 * LANES], cs[:, 2 * LANES:]
    return d * c + pltpu.roll(d * s1, half, 1) + pltpu.roll(d * s2, LANES - half, 1)


def _mla_prep(z_qkr, gq, gkv, cs, c256_w):
    S = z_qkr.shape[0]
    QL, KVL = gq.shape[1], gkv.shape[1]
    H = N_DEV
    tm = _tile(S, 256, SUBLANES)
    half = ROPE // 2

    def body(z_ref, gq_ref, gkv_ref, cs_ref, w_ref, q_ref, k_ref, v_ref, qn_ref, kvn_ref, rq_ref, rkv_ref):
        z = z_ref[...]
        cs_t = cs_ref[...]
        qn32, rq = _rms_fwd(z[:, :QL], gq_ref[...])
        kvn32, rkv = _rms_fwd(z[:, QL:QL + KVL], gkv_ref[...])
        qn = qn32.astype(BF16)
        kvn = kvn32.astype(BF16)
        qn_ref[...] = qn
        kvn_ref[...] = kvn
        rq_ref[...] = rq
        rkv_ref[...] = rkv
        krope = _rope(z[:, QL + KVL:], cs_t, half).astype(BF16)
        for h in range(H):
            lo, mid, hi = h * HEAD_SLOT, h * HEAD_SLOT + LANES, (h + 1) * HEAD_SLOT
            q = _dot(qn, w_ref[h, KVL:KVL + QL, :])
            kv = _dot(kvn, w_ref[h, 0:KVL, :])
            q_ref[:, lo:mid] = q[:, :LANES].astype(BF16)
            q_ref[:, mid:hi] = _rope(q[:, LANES:], cs_t, half).astype(BF16)
            k_ref[:, lo:mid] = kv[:, :LANES].astype(BF16)
            k_ref[:, mid:hi] = krope
            v_ref[:, h * VDIM:(h + 1) * VDIM] = kv[:, LANES:].astype(BF16)

    return _call(
        body, "mla_prep", (S // tm,),
        [_rows(tm, z_qkr.shape[1]), _whole(gq), _whole(gkv), _rows(tm, 3 * LANES), _whole(c256_w)],
        [_rows(tm, H * HEAD_SLOT), _rows(tm, H * HEAD_SLOT), _rows(tm, H * VDIM), _rows(tm, QL), _rows(tm, KVL),
         _rows(tm, 1), _rows(tm, 1)],
        [_sds((S, H * HEAD_SLOT), BF16), _sds((S, H * HEAD_SLOT), BF16), _sds((S, H * VDIM), BF16),
         _sds((S, QL), BF16), _sds((S, KVL), BF16), _sds((S, 1), F32), _sds((S, 1), F32)],
    )(z_qkr, gq, gkv, cs, c256_w)


def _chunk_mask(t):
    shift = CHUNK.bit_length() - 1
    krow = lax.broadcasted_iota(jnp.int32, (t, t), 0) >> shift
    qcol = lax.broadcasted_iota(jnp.int32, (t, t), 1) >> shift
    return krow <= qcol


def _attn_fwd(q, k, v, H):
    S = q.shape[0]
    t = _tile(S, 512, CHUNK)
    nq = S // t

    def body(q_ref, k_ref, v_ref, o_ref, lse_ref, vt_ref):
        qi = pl.program_id(1)

        @pl.when(qi == 0)
        def _():
            vt_ref[...] = v_ref[...].T

        qv = q_ref[...]

        def block(kj, carry, masked):
            m, l, acc = carry
            off = pl.multiple_of(kj * t, t)
            s = _dot_nt(k_ref[pl.ds(off, t), :], qv) * SCORE_SCALE
            if masked:
                s = jnp.where(_chunk_mask(t), s, -1e30)
            m_new = jnp.maximum(m, jnp.max(s, axis=0, keepdims=True))
            alpha = jnp.exp2(m - m_new)
            p = jnp.exp2(s - m_new)
            l = alpha * l + jnp.sum(p, axis=0, keepdims=True)
            acc = alpha * acc + _dot(vt_ref[:, pl.ds(off, t)], p.astype(BF16))
            return m_new, l, acc

        init = (jnp.full((1, t), -1e30, F32), jnp.zeros((1, t), F32), jnp.zeros((VDIM, t), F32))
        carry = lax.fori_loop(0, qi, lambda kj, c: block(kj, c, False), init)
        m, l, acc = block(qi, carry, True)
        o_ref[...] = (acc / l).T.astype(BF16)
        lse_ref[0] = jnp.broadcast_to(m + jnp.log2(l), (SUBLANES, t))

    return _call(
        body, "attn_fwd", (H, nq),
        [pl.BlockSpec((t, HEAD_SLOT), lambda h, i: (i, h)), pl.BlockSpec((S, HEAD_SLOT), lambda h, i: (0, h)),
         pl.BlockSpec((S, VDIM), lambda h, i: (0, h))],
        [pl.BlockSpec((t, VDIM), lambda h, i: (i, h)), pl.BlockSpec((1, SUBLANES, t), lambda h, i: (h, 0, i))],
        [_sds((S, H * VDIM), BF16), _sds((H, SUBLANES, S), F32)],
        [pltpu.VMEM((VDIM, S), BF16)],
    )(q, k, v)


def _merge_wo(o, by, z_gg, h, sq_w, c128_w):
    S, D = h.shape
    C = by.shape[1]
    r = sq_w.shape[1] // 3
    tm = _tile(S, 512, SUBLANES)

    def body(o_ref, by_ref, gg_ref, h_ref, wmo_ref, wo_ref, wco_ref, h2_ref, mg_ref, yc_ref, ym_ref):
        ymla = _dot(o_ref[...], wmo_ref[...].reshape(N_DEV * r, D))
        yconv = _dot(by_ref[...], _cat_slots(wco_ref))
        gg = gg_ref[...]
        merged = (_sig(gg[:, :D]) * yconv + _sig(gg[:, D:]) * ymla).astype(BF16)
        mg_ref[...] = merged
        yc_ref[...] = yconv.astype(BF16)
        ym_ref[...] = ymla.astype(BF16)
        h2_ref[...] = h_ref[...] + _dot(merged, wo_ref[...].reshape(N_DEV * r, D))

    return _call(
        body, "merge_wo", (S // tm,),
        [_rows(tm, o.shape[1]), _rows(tm, C), _rows(tm, 2 * D), _rows(tm, D), _slab(sq_w, r, 0), _slab(sq_w, r, 1),
         _slab(c128_w, C, 0)],
        [_rows(tm, D)] * 4,
        [_sds((S, D), F32)] + [_sds((S, D), BF16)] * 3,
    )(o, by, z_gg, h, sq_w, sq_w, c128_w)


def _ple_fwd(h, gain, p, sq_w, c128_w, C):
    S, D = h.shape
    P = p.shape[1]
    r = sq_w.shape[1] // 3
    tm = _tile(S, 512, SUBLANES)

    def body(h_ref, gain_ref, p_ref, wpg_ref, wpp_ref, o_ref, pre_ref, pp_ref, n_ref, r_ref):
        x = h_ref[...]
        n32, rstd = _rms_fwd(x, gain_ref[...])
        n = n32.astype(BF16)
        n_ref[...] = n
        r_ref[...] = rstd
        pre = _dot(n, wpg_ref[...].reshape(N_DEV * r, D))
        pp = _dot(p_ref[...].astype(BF16), _cat_slots(wpp_ref))
        pre_ref[...] = pre
        pp_ref[...] = pp
        o_ref[...] = x + _sig(pre) * pp

    return _call(
        body, "ple_fwd", (S // tm,),
        [_rows(tm, D), _whole(gain), _rows(tm, P), _slab(sq_w, r, 2), _slab(c128_w, P, C // P)],
        [_rows(tm, D), _rows(tm, D), _rows(tm, D), _rows(tm, D), _rows(tm, 1)],
        [_sds((S, D), F32)] * 3 + [_sds((S, D), BF16), _sds((S, 1), F32)],
    )(h, gain, p, sq_w, c128_w)


def _final_loss(h, gain, target):
    S, D = h.shape
    tm = _tile(S, 512, SUBLANES)

    def body(h_ref, gain_ref, t_ref, dh_ref, loss_ref, dg_ref):
        @pl.when(pl.program_id(0) == 0)
        def _():
            loss_ref[...] = jnp.zeros_like(loss_ref)
            dg_ref[...] = jnp.zeros_like(dg_ref)

        x = h_ref[...]
        gain_v = gain_ref[...]
        y, rstd = _rms_fwd(x, gain_v)
        err = y - t_ref[...]
        loss_ref[...] += 0.5 * jnp.sum(jnp.mean(err * err, axis=-1, keepdims=True))
        dx, dgain = _rms_bwd(err * (1.0 / D), x, rstd, gain_v)
        dh_ref[...] = dx
        dg_ref[...] += dgain

    return _call(
        body, "final_loss", (S // tm,),
        [_rows(tm, D), _whole(gain), _rows(tm, D)],
        [_rows(tm, D), pl.BlockSpec((1, LANES), lambda i: (0, 0)), pl.BlockSpec((1, D), lambda i: (0, 0))],
        [_sds((S, D), F32), _sds((1, LANES), F32), _sds((1, D), F32)],
    )(h, gain, target)


def _tn_call(body, name, grid, in_specs, out_spec, out_shape, scratch, operands, prev):
    n = len(operands)
    if prev is None:
        return _call(body, name, grid, in_specs, out_spec, out_shape, scratch)(*operands)
    assert prev.shape == out_shape.shape and prev.dtype == out_shape.dtype

    def wrapped(*refs):
        body(*refs[:n], *refs[n + 1:])

    return _call(wrapped, name, grid, in_specs + [ANY], out_spec, out_shape, scratch, {n: 0})(*operands, prev)


def _transposed(x_ref, xt_ref, first):
    @pl.when(first)
    def _():
        xt_ref[...] = x_ref[...].astype(BF16).T


def _tn_slots(x, dy, prev, rows_total, row_off):
    S, K = x.shape
    B, _, c = dy.shape
    tk = _tile(K, 512, LANES)

    def body(x_ref, dy_ref, o_ref, xt_ref):
        _transposed(x_ref, xt_ref, pl.program_id(1) == 0)
        o_ref[0] = _dot(xt_ref[...], dy_ref[0]).astype(BF16)

    return _tn_call(
        body, "tn_slots", (K // tk, B),
        [pl.BlockSpec((S, tk), lambda i, b: (0, i)), pl.BlockSpec((1, S, c), lambda i, b: (b, 0, 0))],
        pl.BlockSpec((1, tk, c), lambda i, b: (b, row_off // tk + i, 0)),
        _sds((B, rows_total, c), BF16), [pltpu.VMEM((tk, S), BF16)], [x, dy], prev)


def _tn_plain(x, dy, out_dtype=F32):
    S, K = x.shape
    B, _, c = dy.shape
    tk = _tile(K, 512, LANES)
    tn = _tile(c, 1024, LANES)

    def body(x_ref, dy_ref, o_ref, xt_ref):
        _transposed(x_ref, xt_ref, (pl.program_id(1) == 0) & (pl.program_id(2) == 0))
        o_ref[0] = _dot(xt_ref[...], dy_ref[0]).astype(out_dtype)

    return _call(
        body, "tn_plain", (K // tk, B, c // tn),
        [pl.BlockSpec((S, tk), lambda i, b, j: (0, i)), pl.BlockSpec((1, S, tn), lambda i, b, j: (b, 0, j))],
        pl.BlockSpec((1, tk, tn), lambda i, b, j: (b, i, j)),
        _sds((B, K, c), out_dtype), [pltpu.VMEM((tk, S), BF16)],
    )(x, dy)


def _tn_down(at, dh, prev, rows_total, which):
    nb, c, S = at.shape
    D = dh.shape[1]
    r = c // 2
    tn = _tile(D, 512, LANES)

    def body(at_ref, dh_ref, o_ref):
        g = 0.5 * _dot(at_ref[0], dh_ref[...].astype(BF16))
        o_ref[...] = g.astype(BF16).reshape(2, r, tn)

    return _tn_call(
        body, "tn_down", (nb, D // tn),
        [pl.BlockSpec((1, c, S), lambda i, j: (i, 0, 0)), pl.BlockSpec((S, tn), lambda i, j: (0, j))],
        pl.BlockSpec((2, r, tn), lambda i, j: (i, which, j)),
        _sds((N_DEV, rows_total, D), BF16), [], [at, dh], prev)


def _tn_square(x, dy, prev, rows_total, member):
    S, K = x.shape
    N = dy.shape[1]
    r = K // N_DEV
    tk = _tile(K, 512, r)
    tn = _tile(N, 512, LANES)

    def body(x_ref, dy_ref, o_ref, xt_ref):
        _transposed(x_ref, xt_ref, pl.program_id(1) == 0)
        g = _dot(xt_ref[...], dy_ref[...].astype(BF16))
        o_ref[...] = g.astype(BF16).reshape(tk // r, r, tn)

    return _tn_call(
        body, "tn_square", (K // tk, N // tn),
        [pl.BlockSpec((S, tk), lambda i, j: (0, i)), pl.BlockSpec((S, tn), lambda i, j: (0, j))],
        pl.BlockSpec((tk // r, r, tn), lambda i, j: (i, member, j)),
        _sds((N_DEV, rows_total, N), BF16), [pltpu.VMEM((tk, S), BF16)], [x, dy], prev)


def _tn_cols(x, dy, prev, rows_total, row_block):
    S, K = x.shape
    N = dy.shape[1]
    cw = N // N_DEV

    def body(x_ref, dy_ref, o_ref):
        g = _dot(x_ref[...].astype(BF16).T, dy_ref[...])
        for d in range(N_DEV):
            o_ref[d] = g[:, d * cw:(d + 1) * cw].astype(BF16)

    return _tn_call(
        body, "tn_cols", (1,),
        [pl.BlockSpec((S, K), lambda i: (0, 0)), pl.BlockSpec((S, N), lambda i: (0, 0))],
        pl.BlockSpec((N_DEV, K, cw), lambda i: (0, row_block, 0)),
        _sds((N_DEV, rows_total, cw), BF16), [], [x, dy], prev)


def _tn_heads(qn, kvn, dqp, dkv):
    S, QL = qn.shape
    KVL = kvn.shape[1]

    def body(qn_ref, kvn_ref, dq_ref, dkv_ref, o_ref):
        o_ref[0, 0:KVL, :] = _dot(kvn_ref[...].T, dkv_ref[...]).astype(BF16)
        o_ref[0, KVL:KVL + QL, :] = _dot(qn_ref[...].T, dq_ref[...]).astype(BF16)

    head = pl.BlockSpec((S, HEAD_SLOT), lambda h: (0, h))
    return _call(
        body, "tn_heads", (N_DEV,),
        [pl.BlockSpec((S, QL), lambda h: (0, 0)), pl.BlockSpec((S, KVL), lambda h: (0, 0)), head, head],
        pl.BlockSpec((1, KVL + QL, HEAD_SLOT), lambda h: (h, 0, 0)),
        _sds((N_DEV, KVL + QL, HEAD_SLOT), BF16),
    )(qn, kvn, dqp, dkv)


def _ple_bwd(dh, pre, pp, h, rstd, gain, sq_w, after):
    S, D = h.shape
    r = sq_w.shape[1] // 3
    tm = _tile(S, 512, SUBLANES)

    def body(dh_ref, pre_ref, pp_ref, h_ref, r_ref, gain_ref, wpg_ref, *rest):
        o_ref, dpre_ref, dpp_ref, dg_ref = rest[len(after):]

        @pl.when(pl.program_id(0) == 0)
        def _():
            dg_ref[...] = jnp.zeros_like(dg_ref)

        d = dh_ref[...]
        gate = _sig(pre_ref[...])
        dpre = (d * pp_ref[...] * gate * (1.0 - gate)).astype(BF16)
        dpre_ref[...] = dpre
        dpp_ref[...] = (d * gate).astype(BF16)
        dn = _dot_nt(dpre, wpg_ref[...].reshape(N_DEV * r, D))
        dx, dgain = _rms_bwd(dn, h_ref[...], r_ref[...], gain_ref[...])
        o_ref[...] = d + dx
        dg_ref[...] += dgain

    return _call(
        body, "ple_bwd", (S // tm,),
        [_rows(tm, D), _rows(tm, D), _rows(tm, D), _rows(tm, D), _rows(tm, 1), _whole(gain), _slab(sq_w, r, 2)]
        + [ANY] * len(after),
        [_rows(tm, D), _rows(tm, D), _rows(tm, D), pl.BlockSpec((1, D), lambda i: (0, 0))],
        [_sds((S, D), F32), _sds((S, D), BF16), _sds((S, D), BF16), _sds((1, D), F32)],
    )(dh, pre, pp, h, rstd, gain, sq_w, *after)


def _ffn_bwd_act(dh, dn_w, which, jac, after=()):
    S, D = dh.shape
    _, _, c = jac.shape
    nb = N_DEV // 2
    tm = _tile(S, 256, SUBLANES)

    def body(dh_ref, w_ref, jac_ref, *rest):
        dgu_ref = rest[len(after)]
        dhb = dh_ref[...].astype(BF16)
        for d in range(nb):
            da = _dot_nt(dhb, _down_weight(w_ref, d, c))
            dgu_ref[d] = (da * jac_ref[d].astype(F32)).astype(BF16)
            dgu_ref[nb + d] = (da * jac_ref[nb + d].astype(F32)).astype(BF16)

    act = pl.BlockSpec((N_DEV, tm, c), lambda i: (0, i, 0))
    return _call(
        body, "ffn_bwd_act", (S // tm,),
        [_rows(tm, D), _slab(dn_w, c // 2, which), act] + [ANY] * len(after),
        act,
        _sds((N_DEV, S, c), BF16),
    )(dh, dn_w, jac, *after)


def _ffn_bwd_in(dgu, gu_w, which, h, rstd, gain, dh):
    S, D = h.shape
    c = dgu.shape[2]
    tm = _tile(S, 256, SUBLANES)

    def body(dgu_ref, w_ref, h_ref, r_ref, gain_ref, dh_ref, o_ref, dgain_ref):
        @pl.when(pl.program_id(0) == 0)
        def _():
            dgain_ref[...] = jnp.zeros_like(dgain_ref)

        dn = _dot_nt(dgu_ref[0], w_ref[0])
        for d in range(1, N_DEV):
            dn = dn + _dot_nt(dgu_ref[d], w_ref[d])
        dx, dgain = _rms_bwd(dn, h_ref[...], r_ref[...], gain_ref[...])
        o_ref[...] = dh_ref[...] + dx
        dgain_ref[...] += dgain

    return _call(
        body, "ffn_bwd_in", (S // tm,),
        [pl.BlockSpec((N_DEV, tm, c), lambda i: (0, i, 0)), _slab(gu_w, D, which), _rows(tm, D), _rows(tm, 1),
         _whole(gain), _rows(tm, D)],
        [_rows(tm, D), pl.BlockSpec((1, D), lambda i: (0, 0))],
        [_sds((S, D), F32), _sds((1, D), F32)],
    )(dgu, gu_w, h, rstd, gain, dh)


def _merge_bwd(dh, z_gg, yconv, ymla, o, sq_w, c128_w, C, after):
    S, D = dh.shape
    r = sq_w.shape[1] // 3
    HV = N_DEV * r
    H = HV // VDIM
    tm = _tile(S, 512, SUBLANES)

    def head_rows():
        row = lax.broadcasted_iota(jnp.int32, (SUBLANES * H, HV), 0) >> (SUBLANES.bit_length() - 1)
        col = lax.broadcasted_iota(jnp.int32, (SUBLANES * H, HV), 1) >> (VDIM.bit_length() - 1)
        return jnp.where(row == col, 1.0, 0.0).astype(BF16)

    def body(dh_ref, gg_ref, yc_ref, ym_ref, o_ref, wmo_ref, wo_ref, wco_ref, *rest):
        dgg_ref, dby_ref, do_ref, dyc_ref, dym_ref, dl_ref = rest[len(after):]
        dm = _dot_nt(dh_ref[...].astype(BF16), wo_ref[...].reshape(HV, D))
        gg = gg_ref[...]
        sgc = _sig(gg[:, :D])
        sgm = _sig(gg[:, D:])
        dyc = (dm * sgc).astype(BF16)
        dym = (dm * sgm).astype(BF16)
        dyc_ref[...] = dyc
        dym_ref[...] = dym
        dgg_ref[:, :D] = (dm * yc_ref[...].astype(F32) * sgc * (1.0 - sgc)).astype(BF16)
        dgg_ref[:, D:] = (dm * ym_ref[...].astype(F32) * sgm * (1.0 - sgm)).astype(BF16)
        dby_ref[...] = _dot_nt(dyc, _cat_slots(wco_ref))
        do = _dot_nt(dym, wmo_ref[...].reshape(HV, D)).astype(BF16)
        do_ref[...] = do
        prod = do.astype(F32) * o_ref[...].astype(F32)
        hi = prod.astype(BF16)
        lo = (prod - hi.astype(F32)).astype(BF16)
        pick = head_rows()
        dl_ref[...] = _dot_nt(pick, hi) + _dot_nt(pick, lo)

    return _call(
        body, "merge_bwd", (S // tm,),
        [_rows(tm, D), _rows(tm, 2 * D), _rows(tm, D), _rows(tm, D), _rows(tm, HV), _slab(sq_w, r, 0),
         _slab(sq_w, r, 1), _slab(c128_w, C, 0)] + [ANY] * len(after),
        [_rows(tm, 2 * D), _rows(tm, C), _rows(tm, HV), _rows(tm, D), _rows(tm, D),
         pl.BlockSpec((SUBLANES * H, tm), lambda i: (0, i))],
        [_sds((S, 2 * D), BF16), _sds((S, C), F32), _sds((S, HV), BF16), _sds((S, D), BF16), _sds((S, D), BF16),
         _sds((SUBLANES * H, S), F32)],
    )(dh, z_gg, yconv, ymla, o, sq_w, sq_w, c128_w, *after)


def _conv_bwd(z_bcv, conv_w, dby):
    _, S, C = z_bcv.shape

    def body(z_ref, w_ref, dby_ref, dz_ref, dw_ref):
        w = w_ref[...]
        c = z_ref[1]
        v = z_ref[2]
        d = dby_ref[...]
        zc = c * v
        z1, z2 = _conv_taps(zc)
        y = w[0:1] * z2 + w[1:2] * z1 + w[2:3] * zc
        dz_ref[0] = (d * y).astype(BF16)
        dy = d * z_ref[0]
        rows = lax.broadcasted_iota(jnp.int32, dy.shape, 0)
        dy1 = jnp.where(rows < S - 1, pltpu.roll(dy, S - 1, 0), 0.0)
        dy2 = jnp.where(rows < S - 2, pltpu.roll(dy, S - 2, 0), 0.0)
        dzc = w[2:3] * dy + w[1:2] * dy1 + w[0:1] * dy2
        dz_ref[1] = (dzc * v).astype(BF16)
        dz_ref[2] = (dzc * c).astype(BF16)
        dw_ref[0:1, :] = jnp.sum(dy * z2, axis=0, keepdims=True)
        dw_ref[1:2, :] = jnp.sum(dy * z1, axis=0, keepdims=True)
        dw_ref[2:3, :] = jnp.sum(dy * zc, axis=0, keepdims=True)

    three = pl.BlockSpec((3, S, LANES), lambda j: (0, 0, j))
    wspec = pl.BlockSpec((3, LANES), lambda j: (0, j))
    return _call(
        body, "conv_bwd", (C // LANES,),
        [three, wspec, pl.BlockSpec((S, LANES), lambda j: (0, j))],
        [three, wspec],
        [_sds((3, S, C), BF16), _sds((3, C), F32)],
    )(z_bcv, conv_w, dby)


def _attn_bwd(q, k, v, do, lse, delta, H):
    S = q.shape[0]
    t = _tile(S, 512, CHUNK)
    nk = S // t

    def body(q_ref, k_ref, v_ref, do_ref, lse_ref, dl_ref, dq_ref, dk_ref, dv_ref, dqt_ref):
        kj = pl.program_id(1)

        @pl.when(kj == 0)
        def _():
            dqt_ref[...] = jnp.zeros_like(dqt_ref)

        kv = k_ref[...]
        vv = v_ref[...]
        kt = kv.T

        def block(qi, carry, masked):
            dk, dv = carry
            off = pl.multiple_of(qi * t, t)
            qv = q_ref[pl.ds(off, t), :]
            dov = do_ref[pl.ds(off, t), :]
            s = _dot_nt(kv, qv) * SCORE_SCALE
            if masked:
                s = jnp.where(_chunk_mask(t), s, -1e30)
            p = jnp.exp2(s - lse_ref[0, 0:1, pl.ds(off, t)])
            dp = _dot_nt(vv, dov)
            ds = (p * (dp - dl_ref[0, 0:1, pl.ds(off, t)]) * ATTN_SCALE).astype(BF16)
            dqt_ref[:, pl.ds(off, t)] += _dot(kt, ds)
            return dk + _dot(ds, qv), dv + _dot(p.astype(BF16), dov)

        init = (jnp.zeros((t, HEAD_SLOT), F32), jnp.zeros((t, VDIM), F32))
        carry = block(kj, init, True)
        dk, dv = lax.fori_loop(kj + 1, nk, lambda qi, c: block(qi, c, False), carry)
        dk_ref[...] = dk
        dv_ref[...] = dv.astype(BF16)

        @pl.when(kj == nk - 1)
        def _():
            dq_ref[...] = dqt_ref[...].T

    kspec = lambda w: pl.BlockSpec((t, w), lambda h, j: (j, h))
    qspec = lambda w: pl.BlockSpec((S, w), lambda h, j: (0, h))
    stat = pl.BlockSpec((1, SUBLANES, S), lambda h, j: (h, 0, 0))
    return _call(
        body, "attn_bwd", (H, nk),
        [qspec(HEAD_SLOT), kspec(HEAD_SLOT), kspec(VDIM), qspec(VDIM), stat, stat],
        [qspec(HEAD_SLOT), kspec(HEAD_SLOT), kspec(VDIM)],
        [_sds((S, H * HEAD_SLOT), F32), _sds((S, H * HEAD_SLOT), F32), _sds((S, H * VDIM), BF16)],
        [pltpu.VMEM((HEAD_SLOT, S), F32)],
    )(q, k, v, do, lse, delta)


def _mla_prep_bwd(dq, dk, dv, z_qkr, rq, rkv, gq, gkv, cs, c256_w):
    S = z_qkr.shape[0]
    QL, KVL = gq.shape[1], gkv.shape[1]
    H = N_DEV
    tm = _tile(S, 256, SUBLANES)
    half = ROPE // 2

    def body(dq_ref, dk_ref, dv_ref, z_ref, rq_ref, rkv_ref, gq_ref, gkv_ref, cs_ref, w_ref,
             dz_ref, dqp_ref, dkv_ref, dgq_ref, dgkv_ref):
        @pl.when(pl.program_id(0) == 0)
        def _():
            dgq_ref[...] = jnp.zeros_like(dgq_ref)
            dgkv_ref[...] = jnp.zeros_like(dgkv_ref)

        cs_t = cs_ref[...]
        dkr = jnp.zeros((tm, LANES), F32)
        dqn = jnp.zeros((tm, QL), F32)
        dkvn = jnp.zeros((tm, KVL), F32)
        for h in range(H):
            lo, mid, hi = h * HEAD_SLOT, h * HEAD_SLOT + LANES, (h + 1) * HEAD_SLOT
            dqp_ref[:, lo:mid] = dq_ref[:, lo:mid].astype(BF16)
            dqp_ref[:, mid:hi] = _unrope(dq_ref[:, mid:hi], cs_t, half).astype(BF16)
            dkv_ref[:, lo:mid] = dk_ref[:, lo:mid].astype(BF16)
            dkv_ref[:, mid:hi] = dv_ref[:, h * VDIM:(h + 1) * VDIM]
            dkr = dkr + dk_ref[:, mid:hi]
            dqn = dqn + _dot_nt(dqp_ref[:, lo:hi], w_ref[h, KVL:KVL + QL, :])
            dkvn = dkvn + _dot_nt(dkv_ref[:, lo:hi], w_ref[h, 0:KVL, :])
        z = z_ref[...]
        dqc, dgq = _rms_bwd(dqn, z[:, :QL], rq_ref[...], gq_ref[...])
        dkvc, dgkv = _rms_bwd(dkvn, z[:, QL:QL + KVL], rkv_ref[...], gkv_ref[...])
        dz_ref[:, :QL] = dqc.astype(BF16)
        dz_ref[:, QL:QL + KVL] = dkvc.astype(BF16)
        dz_ref[:, QL + KVL:] = _unrope(dkr, cs_t, half).astype(BF16)
        dgq_ref[...] += dgq
        dgkv_ref[...] += dgkv

    W = z_qkr.shape[1]
    return _call(
        body, "mla_prep_bwd", (S // tm,),
        [_rows(tm, H * HEAD_SLOT), _rows(tm, H * HEAD_SLOT), _rows(tm, H * VDIM), _rows(tm, W), _rows(tm, 1),
         _rows(tm, 1), _whole(gq), _whole(gkv), _rows(tm, 3 * LANES), _whole(c256_w)],
        [_rows(tm, W), _rows(tm, H * HEAD_SLOT), _rows(tm, H * HEAD_SLOT), _whole(gq), _whole(gkv)],
        [_sds((S, W), BF16), _sds((S, H * HEAD_SLOT), BF16), _sds((S, H * HEAD_SLOT), BF16),
         _sds((1, QL), F32), _sds((1, KVL), F32)],
    )(dq, dk, dv, z_qkr, rq, rkv, gq, gkv, cs, c256_w)


def _mix_in_bwd(d_bcv, dz_qkr, dgg, w_bcv, w_qkr, w_gg, h, rstd, gain, dh):
    S, D = h.shape
    C = d_bcv.shape[2]
    tm = _tile(S, 512, SUBLANES)

    def body(db_ref, dq_ref, dgg_ref, wb_ref, wq_ref, wg_ref, h_ref, r_ref, gain_ref, dh_ref, o_ref, dgain_ref):
        @pl.when(pl.program_id(0) == 0)
        def _():
            dgain_ref[...] = jnp.zeros_like(dgain_ref)

        dn = _dot_nt(dq_ref[...], wq_ref[...]) + _dot_nt(dgg_ref[...], wg_ref[...])
        for k in range(3):
            dn = dn + _dot_nt(db_ref[k], wb_ref[k])
        dx, dgain = _rms_bwd(dn, h_ref[...], r_ref[...], gain_ref[...])
        o_ref[...] = dh_ref[...] + dx
        dgain_ref[...] += dgain

    return _call(
        body, "mix_in_bwd", (S // tm,),
        [pl.BlockSpec((3, tm, C), lambda i: (0, i, 0)), _rows(tm, dz_qkr.shape[1]), _rows(tm, dgg.shape[1]),
         _whole(w_bcv), _whole(w_qkr), _whole(w_gg), _rows(tm, D), _rows(tm, 1), _whole(gain), _rows(tm, D)],
        [_rows(tm, D), pl.BlockSpec((1, D), lambda i: (0, 0))],
        [_sds((S, D), F32), _sds((1, D), F32)],
    )(d_bcv, dz_qkr, dgg, w_bcv, w_qkr, w_gg, h, rstd, gain, dh)


def _rope_tables(positions):
    half = ROPE // 2
    inv_freq = ROPE_THETA ** (-jnp.arange(0, ROPE, 2, dtype=F32) / ROPE)
    ang = positions.astype(F32)[:, None] * inv_freq
    cos, sin = jnp.cos(ang), jnp.sin(ang)
    z = jnp.zeros_like(cos)
    pad = jnp.zeros((positions.shape[0], LANES - 2 * half), F32)
    return jnp.concatenate([cos, cos, pad, -sin, z, pad, z, sin, pad], axis=1)


def _grad_rows(w):
    return dict(gu=2 * w["gu1"].shape[1], dn=2 * w["dn1"].shape[1], sq=w["sq"].shape[1], win=w["win"].shape[1],
                c128=w["c128"].shape[1], c256=w["c256"].shape[1])


def _layer_fwd(h0, p_l, cs, w, sm, late):
    C = sm["conv_w"].shape[1]
    QL, KVL = sm["q_norm"].shape[1], sm["kv_norm"].shape[1]
    jac1, a1, at1, n1, r1 = _ffn_up(h0, sm["ffn1_norm"], w["gu1"], 0)
    h1 = _ffn_down(a1, w["dn1"], 0, h0)
    if late is not None:
        w.update(late(h1))
    w_bcv, w_qkr, w_gg = _win_split(w["win"], C, QL, KVL)
    z_bcv, z_qkr, z_gg, un, rm = _mix_in(h1, sm["mix_norm"], w_bcv, w_qkr, w_gg)
    by = _conv_fwd(z_bcv, sm["conv_w"])
    q, k, v, qn, kvn, rq, rkv = _mla_prep(z_qkr, sm["q_norm"], sm["kv_norm"], cs, w["c256"])
    o, lse = _attn_fwd(q, k, v, N_DEV)
    h2, merged, yconv, ymla = _merge_wo(o, by, z_gg, h1, w["sq"], w["c128"])
    jac2, a2, at2, n2, r2 = _ffn_up(h2, sm["ffn2_norm"], w["gu2"], 0)
    h3 = _ffn_down(a2, w["dn2"], 0, h2)
    h4, pre, pp, pn, rp = _ple_fwd(h3, sm["ple_norm"], p_l, w["sq"], w["c128"], C)
    saved = dict(h0=h0, jac1=jac1, at1=at1, n1=n1, r1=r1, h1=h1, w_bcv=w_bcv, w_qkr=w_qkr, w_gg=w_gg, z_bcv=z_bcv,
                 z_qkr=z_qkr, z_gg=z_gg, un=un, rm=rm, by=by, q=q, k=k, v=v, qn=qn, kvn=kvn, rq=rq, rkv=rkv, o=o,
                 lse=lse, h2=h2, merged=merged, yconv=yconv, ymla=ymla, jac2=jac2, at2=at2, n2=n2, r2=r2, h3=h3,
                 pre=pre, pp=pp, pn=pn, rp=rp, p=p_l)
    return h4, saved


def _layer_bwd_late(dh4, s, w, sm, after):
    D = dh4.shape[1]
    C = sm["conv_w"].shape[1]
    P = s["p"].shape[1]
    rows = _grad_rows(w)
    small = {}
    dh3, dpre, dpp, small["ple_norm"] = _ple_bwd(dh4, s["pre"], s["pp"], s["h3"], s["rp"], sm["ple_norm"], w["sq"],
                                                 after)
    g_sq = _tn_square(s["pn"], dpre, None, rows["sq"], 2)
    g_c128 = _tn_cols(s["p"], dpp, None, rows["c128"], C // P)

    dgu2 = _ffn_bwd_act(dh3, w["dn2"], 0, s["jac2"])
    g_dn = _tn_down(s["at2"], dh3, None, rows["dn"], 1)
    g_gu = _tn_slots(s["n2"], dgu2, None, rows["gu"], D)
    dh2, small["ffn2_norm"] = _ffn_bwd_in(dgu2, w["gu2"], 0, s["h2"], s["r2"], sm["ffn2_norm"], dh3)
    return dh2, dict(gu=g_gu, dn=g_dn, sq=g_sq, c128=g_c128), small


def _layer_bwd_mixer(dh2, part, small, s, cs, w, sm, after):
    C = sm["conv_w"].shape[1]
    rows = _grad_rows(w)
    g_gu, g_dn, g_sq, g_c128 = part["gu"], part["dn"], part["sq"], part["c128"]

    dgg, dby, do, dyc, dym, delta = _merge_bwd(dh2, s["z_gg"], s["yconv"], s["ymla"], s["o"], w["sq"], w["c128"], C,
                                               after)
    g_sq = _tn_square(s["merged"], dh2, g_sq, rows["sq"], 1)
    g_sq = _tn_square(s["o"], dym, g_sq, rows["sq"], 0)
    g_c128 = _tn_cols(s["by"], dyc, g_c128, rows["c128"], 0)
    d_bcv, small["conv_w"] = _conv_bwd(s["z_bcv"], sm["conv_w"], dby)
    delta = delta.reshape(N_DEV, SUBLANES, delta.shape[1])
    dq, dk, dv = _attn_bwd(s["q"], s["k"], s["v"], do, s["lse"], delta, N_DEV)
    dz_qkr, dqp, dkv, small["q_norm"], small["kv_norm"] = _mla_prep_bwd(
        dq, dk, dv, s["z_qkr"], s["rq"], s["rkv"], sm["q_norm"], sm["kv_norm"], cs, w["c256"])
    g_c256 = _tn_heads(s["qn"], s["kvn"], dqp, dkv)
    un = s["un"]
    g_win = _win_merge(_tn_plain(un, d_bcv), _tn_plain(un, dz_qkr[None])[0], _tn_plain(un, dgg[None])[0],
                       w["win"].shape[2])
    dh1, small["mix_norm"] = _mix_in_bwd(d_bcv, dz_qkr, dgg, s["w_bcv"], s["w_qkr"], s["w_gg"], s["h1"], s["rm"],
                                         sm["mix_norm"], dh2)
    return dh1, dict(gu=g_gu, dn=g_dn, sq=g_sq, win=g_win, c128=g_c128, c256=g_c256), small


def _layer_bwd_first(dh1, part, small, s, w, sm, after):
    rows = _grad_rows(w)
    dgu1 = _ffn_bwd_act(dh1, w["dn1"], 0, s["jac1"], after)
    g_dn = _tn_down(s["at1"], dh1, part["dn"], rows["dn"], 0)
    g_gu = _tn_slots(s["n1"], dgu1, part["gu"], rows["gu"], 0)
    dh0, small["ffn1_norm"] = _ffn_bwd_in(dgu1, w["gu1"], 0, s["h0"], s["r1"], sm["ffn1_norm"], dh1)
    return dh0, dict(part, gu=g_gu, dn=g_dn), small


def _mesh_pos():
    return lax.axis_index("x"), lax.axis_index("y"), lax.axis_index("c")


def _other_chips(x, y):
    return [(1 - x, y), (x, 1 - y), (1 - x, 1 - y)]


def _pack(arrs, flipped, width):
    L = arrs[0].shape[0]
    shapes = [a.shape[:0:-1] if f else a.shape[1:] for a, f in zip(arrs, flipped)]
    R = sum(r for r, _ in shapes)

    def body(*refs):
        o_ref = refs[-1]
        off = 0
        for a_ref, f, (r, c) in zip(refs[:-1], flipped, shapes):
            a = a_ref[0].T if f else a_ref[0]
            o_ref[0, off:off + r, 0:c] = a.astype(BF16)
            if c < width:
                o_ref[0, off:off + r, c:width] = jnp.zeros((r, width - c), BF16)
            off += r

    return _call(
        body, "pack", (L,),
        [pl.BlockSpec((1,) + a.shape[1:], lambda l: (l, 0, 0)) for a in arrs],
        pl.BlockSpec((1, R, width), lambda l: (l, 0, 0)),
        _sds((L, R, width), BF16),
    )(*arrs)


def _handshake(peers):
    barrier = pltpu.get_barrier_semaphore()
    for peer in peers:
        pl.semaphore_signal(barrier, inc=1, device_id=peer, device_id_type=MESH)
    pl.semaphore_wait(barrier, len(peers))


def _sequencer_call(body, name, out_types, sems, collective_id, operands):
    return pl.kernel(
        body, name=name, out_type=out_types,
        mesh=plsc.ScalarSubcoreMesh(axis_name="seq", num_cores=1),
        scratch_types=tuple(pltpu.SemaphoreType.DMA((k,)) for k in sems),
        compiler_params=pltpu.CompilerParams(collective_id=collective_id),
    )(*operands)


def _all_gather(packs, l, after, collective_id):
    n = len(packs)

    def body(*refs):
        ins, outs = refs[:n], refs[n + len(after):2 * n + len(after)]
        send_sems, recv_sems, local_sems = refs[2 * n + len(after):]
        x, y, c = _mesh_pos()
        me, sibling = (x, y, c), (x, y, 1 - c)
        chips = _other_chips(x, y)
        _handshake([sibling] + [(*chip, c) for chip in chips])

        def copy(q, k, block, to, src=None):
            slot = outs[q].at[4 * block[0] + 2 * block[1] + block[2]]
            return pltpu.make_async_remote_copy(
                src_ref=slot if src is None else src, dst_ref=slot,
                send_sem=send_sems.at[7 * q + k], recv_sem=recv_sems.at[7 * q + k], device_id=to, device_id_type=MESH)

        started = []
        for q in range(n):
            src = ins[q].at[l]
            mine = pltpu.make_async_copy(src, outs[q].at[4 * x + 2 * y + c], local_sems.at[q])
            mine.start()
            started.append(mine)
        sends = []
        for q in range(n):
            src = ins[q].at[l]
            sends.append(copy(q, 0, me, sibling, src=src))
            sends += [copy(q, 1 + j, me, (*chip, c), src=src) for j, chip in enumerate(chips)]
        for cp in sends:
            cp.start()
        for q in range(n):
            for j, chip in enumerate(chips):
                copy(q, 1 + j, (*chip, c), me).wait_recv()
                fwd = copy(q, 4 + j, (*chip, c), sibling)
                fwd.start()
                sends.append(fwd)
        for q in range(n):
            copy(q, 0, sibling, me).wait_recv()
            for j, chip in enumerate(chips):
                copy(q, 4 + j, (*chip, 1 - c), me).wait_recv()
        for cp in sends:
            cp.wait_send()
        for mine in started:
            mine.wait()

    return _sequencer_call(
        body, f"all_gather_{collective_id}", [_sds((N_DEV,) + p.shape[1:], p.dtype) for p in packs], (7 * n, 7 * n, n),
        collective_id, list(packs) + list(after))


def _rs_d2d(gs, l, collective_id):
    n = len(gs)

    def body(*refs):
        ins, outs = refs[:n], refs[n:2 * n]
        send_sems, recv_sems = refs[2 * n:]
        x, y, c = _mesh_pos()
        _handshake([(x, y, 1 - c)])
        copies = []
        for q in range(n):
            for j in range(4):
                copies.append(pltpu.make_async_remote_copy(
                    src_ref=ins[q].at[2 * j + (1 - c)], dst_ref=outs[q].at[j], send_sem=send_sems.at[4 * q + j],
                    recv_sem=recv_sems.at[4 * q + j], device_id=(x, y, 1 - c), device_id_type=MESH))
        for cp in copies:
            cp.start()
        for cp in copies:
            cp.wait()

    return _sequencer_call(
        body, f"rs_d2d_{l}", [_sds((4,) + g.shape[1:], g.dtype) for g in gs], (4 * n, 4 * n), collective_id, gs)


def _rs_add_chip(gs, as_, after):
    n = len(gs)
    steps = 4
    tiles = [g.shape[1] // steps for g in gs]

    def chip(k):
        x, y, _ = _mesh_pos()
        return ([(x, y)] + _other_chips(x, y))[k]

    def body(*refs):
        g_refs, a_refs = refs[:4 * n], refs[4 * n:8 * n]
        own_refs, t_refs = refs[8 * n + len(after):9 * n + len(after)], refs[9 * n + len(after):]
        for q in range(n):
            g, a = g_refs[4 * q:4 * q + 4], a_refs[4 * q:4 * q + 4]
            own_refs[q][...] = g[0][0].astype(F32) + a[0][0].astype(F32)
            for k in range(1, 4):
                t_refs[q][k - 1] = (g[k][0].astype(F32) + a[k][0].astype(F32)).astype(BF16)

    def gspec(q, k):
        def index(i):
            px, py = chip(k)
            return 4 * px + 2 * py + lax.axis_index("c"), i, 0
        return pl.BlockSpec((1, tiles[q], gs[q].shape[2]), index)

    def aspec(q, k):
        def index(i):
            px, py = chip(k)
            return 2 * px + py, i, 0
        return pl.BlockSpec((1, tiles[q], gs[q].shape[2]), index)

    in_specs = [gspec(q, k) for q in range(n) for k in range(4)] + [aspec(q, k) for q in range(n) for k in range(4)]
    operands = [g for g in gs for _ in range(4)] + [a for a in as_ for _ in range(4)]
    out_specs = [pl.BlockSpec((tiles[q], gs[q].shape[2]), lambda i: (i, 0)) for q in range(n)]
    out_specs += [pl.BlockSpec((3, tiles[q], gs[q].shape[2]), lambda i: (0, i, 0)) for q in range(n)]
    out_shape = [_sds(g.shape[1:], F32) for g in gs] + [_sds((3,) + g.shape[1:], BF16) for g in gs]
    res = _call(body, "rs_add_chip", (steps,), in_specs + [ANY] * len(after), out_specs, out_shape)(*operands, *after)
    return res[:n], res[n:]


def _rs_ici(ts, l, collective_id):
    n = len(ts)

    def body(*refs):
        ins, outs = refs[:n], refs[n:2 * n]
        send_sems, recv_sems = refs[2 * n:]
        x, y, c = _mesh_pos()
        chips = _other_chips(x, y)
        _handshake([(*chip, c) for chip in chips])
        copies = []
        for q in range(n):
            for k, chip in enumerate(chips):
                copies.append(pltpu.make_async_remote_copy(
                    src_ref=ins[q].at[k], dst_ref=outs[q].at[k], send_sem=send_sems.at[3 * q + k],
                    recv_sem=recv_sems.at[3 * q + k], device_id=(*chip, c), device_id_type=MESH))
        for cp in copies:
            cp.start()
        for cp in copies:
            cp.wait()

    return _sequencer_call(
        body, f"rs_ici_{l}", [_sds(t.shape, t.dtype) for t in ts], (3 * n, 3 * n), collective_id, ts)


def _all_reduce_small(v):
    n, W = v.shape

    def body(v_ref, out_ref, slots, send_sems, recv_sems):
        x, y, c = _mesh_pos()
        me = 4 * x + 2 * y + c
        slots[me] = v_ref[...]
        copies = []
        for k in range(1, N_DEV):
            kx, ky, kc = (k >> 2) & 1, (k >> 1) & 1, k & 1
            peer = (1 - x if kx else x, 1 - y if ky else y, 1 - c if kc else c)
            copies.append(pltpu.make_async_remote_copy(
                src_ref=v_ref, dst_ref=slots.at[me], send_sem=send_sems.at[k - 1], recv_sem=recv_sems.at[k - 1],
                device_id=peer, device_id_type=MESH))
        for cp in copies:
            cp.start()
        for cp in copies:
            cp.wait()
        acc = slots[0]
        for d in range(1, N_DEV):
            acc = acc + slots[d]
        out_ref[...] = acc

    vm = pl.BlockSpec(memory_space=pltpu.VMEM)
    return pl.pallas_call(
        body, name="all_reduce_small",
        out_shape=_sds((n, W), F32),
        in_specs=[vm], out_specs=vm,
        scratch_shapes=[pltpu.VMEM((N_DEV, n, W), F32), pltpu.SemaphoreType.DMA((7,)), pltpu.SemaphoreType.DMA((7,))],
    )(v)


def _adamw_math(w, g, m, v):
    m2 = ADAM_B1 * m + (1.0 - ADAM_B1) * g
    v2 = ADAM_B2 * v + (1.0 - ADAM_B2) * (g * g)
    m_hat = m2 / (1.0 - ADAM_B1 ** ADAM_STEP)
    v_hat = v2 / (1.0 - ADAM_B2 ** ADAM_STEP)
    return -ADAM_LR * (m_hat / (jnp.sqrt(v_hat) + ADAM_EPS) + ADAM_WD * w), m2, v2


def _adamw(w, g, m, v):
    L, r, c = w.shape
    tr = _tile(r, max(SUBLANES, (256 * 1024 // c) // SUBLANES * SUBLANES), SUBLANES)

    def body(w_ref, g_ref, m_ref, v_ref, d_ref, nm_ref, nv_ref):
        d_ref[...], nm_ref[...], nv_ref[...] = _adamw_math(w_ref[...], g_ref[...], m_ref[...], v_ref[...])

    spec = pl.BlockSpec((1, tr, c), lambda l, i: (l, i, 0))
    return _call(body, "adamw", (L, r // tr), [spec] * 4, [spec] * 3, [_sds((L, r, c), F32)] * 3)(w, g, m, v)


def _adamw_reduced(w, m, v, flipped, own, b, row_off, tr, l, prev, after):
    L = w.shape[0]
    c, r = w.shape[1:] if flipped else w.shape[:0:-1]
    W = own.shape[1]
    ob = row_off // tr
    extra = list(prev or ()) + list(after)

    def body(w_ref, m_ref, v_ref, own_ref, b_ref, *rest):
        g_ref, d_ref, nm_ref, nv_ref = rest[len(extra):]
        g = ((own_ref[...] + b_ref[0].astype(F32)) + b_ref[1].astype(F32)) + b_ref[2].astype(F32)
        g = g[:, :c].T if flipped else g[:, :c]
        g_ref[0] = g
        d_ref[0], nm_ref[0], nv_ref[0] = _adamw_math(w_ref[0], g, m_ref[0], v_ref[0])

    spec = pl.BlockSpec((1, c, tr), lambda i: (l, 0, i)) if flipped else pl.BlockSpec((1, tr, c), lambda i: (l, i, 0))
    return _call(
        body, "adamw_reduced", (r // tr,),
        [spec] * 3 + [pl.BlockSpec((tr, W), lambda i: (ob + i, 0)), pl.BlockSpec((3, tr, W), lambda i: (0, ob + i, 0))]
        + [ANY] * len(extra),
        [spec] * 4, [_sds(w.shape, F32)] * 4,
        aliases={5 + k: k for k in range(4)} if prev else None,
    )(w, m, v, own, b, *extra)


_MEMBERS = dict(gu=("ffn1_w_gu", "ffn2_w_gu"), dn=("ffn1_w_down", "ffn2_w_down"),
                sq=("w_mla_out", "w_o", "w_ple_gate"), win=("w_in",), c128=("w_conv_out", "w_ple_proj"),
                c256=("w_ukv", "w_uq"))
_GATHER_MEMBERS = dict(_MEMBERS, gu1=("ffn1_w_gu",), gu2=("ffn2_w_gu",), dn1=("ffn1_w_down",), dn2=("ffn2_w_down",))
GATHER_STAGES = (("gu1", "dn1"), ("win", "c256", "c128", "sq"), ("gu2", "dn2"))
_FLIPPED = ("ffn1_w_gu", "ffn2_w_gu", "w_in", "w_uq")
_SMALL = ("ffn1_norm", "mix_norm", "q_norm", "kv_norm", "ffn2_norm", "ple_norm")
_ORDER = ("ffn1_norm", "ffn1_w_gu", "ffn1_w_down", "mix_norm", "w_in", "conv_w", "w_conv_out", "q_norm", "kv_norm",
          "w_uq", "w_ukv", "w_mla_out", "w_o", "ffn2_norm", "ffn2_w_gu", "ffn2_w_down", "ple_norm", "w_ple_gate",
          "w_ple_proj", "final_norm")


def _class_width(wts, cls):
    return HEAD_SLOT if cls == "c256" else wts[_GATHER_MEMBERS[cls][0]].shape[2]


def _pack_rows(vecs, width):
    flat = jnp.concatenate([a.reshape(-1) for a in vecs])
    n = flat.shape[0]
    rows = -(-n // width)
    rows = -(-rows // SUBLANES) * SUBLANES
    flat = jnp.pad(flat, (0, rows * width - n))
    offs, o = [], 0
    for a in vecs:
        offs.append(o)
        o += a.size
    return flat.reshape(rows, width), offs


def _unpack_rows(packed, vecs, offs):
    flat = packed.reshape(-1)
    return [flat[o:o + a.size].reshape(a.shape) for a, o in zip(vecs, offs)]


def _train(x, p, positions, target, gathered, packs, small_w, final_norm, update):
    cs = _rope_tables(positions)
    L = len(small_w)
    h = x
    saved = []
    def gather(l, names, after, collective_id):
        got = _all_gather([packs[n] for n in names], l, after, collective_id)
        return dict(zip(names, got))

    late = None
    if packs is not None:
        first, mixer, second = GATHER_STAGES
        w0 = gather(0, first, [], 0)
        w0.update(gather(0, mixer, [w0[first[0]]], 1))
        gathered = [w0]
        late = lambda h1: gather(0, second, [h1], 2)
    everything = sum(GATHER_STAGES, ())
    for l in range(L):
        h, s = _layer_fwd(h, p[l], cs, gathered[l], small_w[l], late)
        late = None
        saved.append(s)
        if packs is not None and l + 1 < L:
            gathered.append(gather(l + 1, everything, [s["by"]], 2 + l + 1))
    dh, loss, d_final = _final_loss(h, final_norm, target)
    grads, smalls = [None] * L, [None] * L
    exchanged = None
    landing = None

    def second_stage(after):
        l, gs, as_ = exchanged
        owns, ts = _rs_add_chip(gs, as_, [after])
        return l, owns, _rs_ici(ts, l, 2 * L + 2 + l)

    for l in reversed(range(L)):
        dh, part, small = _layer_bwd_late(dh, saved[l], gathered[l], small_w[l], [])
        pin = []
        if exchanged is not None:
            landing = second_stage(dh)
            pin = [landing[1][0]]
        dh, part, small = _layer_bwd_mixer(dh, part, small, saved[l], cs, gathered[l], small_w[l], pin)
        pin = [update(*landing)] if exchanged is not None else []
        dh, g, smalls[l] = _layer_bwd_first(dh, part, small, saved[l], gathered[l], small_w[l], pin)
        if update is not None:
            gs = [g[cls] for cls in CLASSES]
            exchanged = (l, gs, _rs_d2d(gs, l, L + 2 + l))
        else:
            grads[l] = g
    if update is not None:
        update(*second_stage(dh))
    return loss[0, 0], dh, grads, smalls, d_final


def kernel(x, p, positions, ffn1_norm, ffn1_w_gu, ffn1_w_down, mix_norm, w_in, conv_w, w_conv_out, q_norm, kv_norm, w_uq, w_ukv, w_mla_out, w_o, ffn2_norm, ffn2_w_gu, ffn2_w_down, ple_norm, w_ple_gate, w_ple_proj, final_norm, loss_target, m_ffn1_norm, m_ffn1_w_gu, m_ffn1_w_down, m_mix_norm, m_w_in, m_conv_w, m_w_conv_out, m_q_norm, m_kv_norm, m_w_uq, m_w_ukv, m_w_mla_out, m_w_o, m_ffn2_norm, m_ffn2_w_gu, m_ffn2_w_down, m_ple_norm, m_w_ple_gate, m_w_ple_proj, m_final_norm, v_ffn1_norm, v_ffn1_w_gu, v_ffn1_w_down, v_mix_norm, v_w_in, v_conv_w, v_w_conv_out, v_q_norm, v_kv_norm, v_w_uq, v_w_ukv, v_w_mla_out, v_w_o, v_ffn2_norm, v_ffn2_w_gu, v_ffn2_w_down, v_ple_norm, v_w_ple_gate, v_w_ple_proj, v_final_norm):
    args = dict(locals())
    wts = {n: args[n] for n in _ORDER}
    L = w_in.shape[0]
    dev = 4 * lax.axis_index("x") + 2 * lax.axis_index("y") + lax.axis_index("c")

    view = lambda n, a: jnp.swapaxes(a, 1, 2) if n in _FLIPPED else a
    packs = {cls: _pack([view(n, wts[n]) for n in _GATHER_MEMBERS[cls]], [n in _FLIPPED for n in _GATHER_MEMBERS[cls]],
                        _class_width(wts, cls))
             for stage in GATHER_STAGES for cls in stage}
    cw = conv_w.shape[2]
    conv_full = lax.dynamic_update_slice(jnp.zeros((L, 3, N_DEV * cw), F32), conv_w, (0, 0, dev * cw))
    conv_packed, conv_offs = _pack_rows([conv_full], FLAT_COLS)
    conv_full = _unpack_rows(_all_reduce_small(conv_packed), [conv_full], conv_offs)[0]
    small_w = [dict({n: wts[n][l][None, :] for n in _SMALL}, conv_w=conv_full[l]) for l in range(L)]

    done = {}

    def update(l, owns, bs):
        for q, cls in enumerate(CLASSES):
            off = 0
            rows = [wts[n].shape[1] for n in _MEMBERS[cls]]
            tr = _tile(math.gcd(*rows), 256, BF16_ROWS)
            for n, r in zip(_MEMBERS[cls], rows):
                done[n] = _adamw_reduced(view(n, wts[n]), view(n, args["m_" + n]), view(n, args["v_" + n]),
                                         n in _FLIPPED, owns[q], bs[q], off, tr, l, done.get(n), [])
                off += r
        return done[_MEMBERS[CLASSES[-1]][-1]][0]

    loss_dev, grad_x, _, smalls, d_final = _train(x[0], p[:, 0], positions[0], loss_target[0], None, packs, small_w,
                                                  final_norm[None, :], update)

    small = [jnp.stack([smalls[l][n][0] for l in range(L)]) for n in _SMALL]
    small += [jnp.stack([smalls[l]["conv_w"] for l in range(L)]), d_final[0], loss_dev[None]]
    packed, offs = _pack_rows(small, FLAT_COLS)
    small = _unpack_rows(_all_reduce_small(packed), small, offs)
    grad = dict(zip(_SMALL, small))
    grad["conv_w"] = lax.dynamic_slice(small[len(_SMALL)], (0, 0, dev * cw), (L, 3, cw))
    grad["final_norm"] = small[-2]
    loss = small[-1][0]

    deltas, new_m, new_v = {}, {}, {}
    for n, outs in done.items():
        grad[n], deltas[n], new_m[n], new_v[n] = (view(n, a) for a in outs)
    for n in _SMALL + ("conv_w", "final_norm"):
        w3 = wts[n].reshape((1,) * (3 - wts[n].ndim) + wts[n].shape)
        d, nm, nv = _adamw(w3, grad[n].reshape(w3.shape), args["m_" + n].reshape(w3.shape),
                           args["v_" + n].reshape(w3.shape))
        deltas[n], new_m[n], new_v[n] = (a.reshape(wts[n].shape) for a in (d, nm, nv))
    return (loss, grad_x[None], *[grad[n] for n in _ORDER], *[deltas[n] for n in _ORDER],
            *[new_m[n] for n in _ORDER], *[new_v[n] for n in _ORDER])
```

```python
import math

import jax
import jax.numpy as jnp
from jax import lax
from jax.experimental import pallas as pl
from jax.experimental.pallas import tpu as pltpu
from jax.experimental.pallas import tpu_sc as plsc

F32 = jnp.float32
BF16 = jnp.bfloat16

CHUNK = 64
NOPE = 128
ROPE = 64
VDIM = 128
ROPE_THETA = 10000.0
EPS = 1e-6
ATTN_SCALE = (NOPE + ROPE) ** -0.5
SCORE_SCALE = ATTN_SCALE * math.log2(math.e)
ADAM_LR = 0.001
ADAM_B1 = 0.9
ADAM_B2 = 0.999
ADAM_EPS = 1e-08
ADAM_WD = 0.01
ADAM_STEP = 10

LANES = 128
SUBLANES = 8
BF16_ROWS = 16
V7X_VMEM_BYTES = 64 * 1024 * 1024
VMEM_LIMIT = V7X_VMEM_BYTES * 7 // 8
HEAD_SLOT = 2 * LANES
N_DEV = 8
ATTN_FWD_WIDTH = 4
ATTN_BWD_WIDTH = 2
FLAT_COLS = 1024
CLASSES = ("gu", "dn", "sq", "win", "c128", "c256")

NT = (((1,), (1,)), ((), ()))
MESH = pl.DeviceIdType.MESH
ANY = pl.BlockSpec(memory_space=pl.ANY)


def _dot(a, b):
    return jnp.dot(a, b, preferred_element_type=F32)


def _dot_nt(a, b):
    return lax.dot_general(a, b, NT, preferred_element_type=F32)


def _sig(x):
    return 1.0 / (1.0 + jnp.exp(-x))


def _tile(n, pref, unit):
    if n <= pref:
        return n
    t = (pref // unit) * unit
    while t >= unit:
        if n % t == 0:
            return t
        t -= unit
    return n


def _call(body, name, grid, in_specs, out_specs, out_shape, scratch=(), aliases=None):
    return pl.pallas_call(
        body,
        name=name,
        grid=grid,
        in_specs=in_specs,
        out_specs=out_specs,
        out_shape=out_shape,
        scratch_shapes=list(scratch),
        input_output_aliases=aliases or {},
        compiler_params=pltpu.CompilerParams(
            dimension_semantics=("arbitrary",) * len(grid), vmem_limit_bytes=VMEM_LIMIT
        ),
    )


def _sds(shape, dtype):
    return jax.ShapeDtypeStruct(shape, dtype)


def _rms_fwd(x, gain):
    rstd = lax.rsqrt(jnp.mean(x * x, axis=-1, keepdims=True) + EPS)
    return x * rstd * gain, rstd


def _rms_bwd(dn, x, rstd, gain):
    xhat = x * rstd
    dgy = dn * gain
    dx = rstd * (dgy - xhat * jnp.mean(dgy * xhat, axis=-1, keepdims=True))
    return dx, jnp.sum(dn * xhat, axis=0, keepdims=True)


def _rows(tm, w):
    return pl.BlockSpec((tm, w), lambda i: (i, 0))


def _whole(a):
    nd = a.ndim
    return pl.BlockSpec(a.shape, lambda i: (0,) * nd, pipeline_mode=pl.Buffered(1))


def _slab(buf, rows, index):
    return pl.BlockSpec((N_DEV, rows, buf.shape[2]), lambda i: (0, index, 0), pipeline_mode=pl.Buffered(1))


def _cat_slots(w):
    return jnp.concatenate([w[d] for d in range(N_DEV)], axis=1)


def _ffn_up(h, gain, gu_w, which):
    S, D = h.shape
    c = gu_w.shape[2]
    tm = _tile(S, 512, SUBLANES)
    nb = N_DEV // 2

    def body(h_ref, gain_ref, w_ref, jac_ref, a_ref, at_ref, n_ref, r_ref):
        n32, rstd = _rms_fwd(h_ref[...], gain_ref[...])
        n = n32.astype(BF16)
        n_ref[...] = n
        r_ref[...] = rstd
        for d in range(nb):
            g = _dot(n, w_ref[d])
            u = _dot(n, w_ref[nb + d])
            sg = _sig(g)
            silu = g * sg
            a = (silu * u).astype(BF16)
            a_ref[d] = a
            at_ref[d] = a.T
            jac_ref[d] = (0.5 * u * (sg + silu * (1.0 - sg))).astype(BF16)
            jac_ref[nb + d] = (0.5 * silu).astype(BF16)

    return _call(
        body, "ffn_up", (S // tm,),
        [_rows(tm, D), _whole(gain), _slab(gu_w, D, which)],
        [pl.BlockSpec((N_DEV, tm, c), lambda i: (0, i, 0)), pl.BlockSpec((nb, tm, c), lambda i: (0, i, 0)),
         pl.BlockSpec((nb, c, tm), lambda i: (0, 0, i)), _rows(tm, D), _rows(tm, 1)],
        [_sds((N_DEV, S, c), BF16), _sds((nb, S, c), BF16), _sds((nb, c, S), BF16), _sds((S, D), BF16),
         _sds((S, 1), F32)],
    )(h, gain, gu_w)


def _down_weight(w_ref, d, c):
    return w_ref[2 * d:2 * d + 2].reshape(c, w_ref.shape[2])


def _ffn_down(a, dn_w, which, h):
    nb, S, c = a.shape
    D = h.shape[1]
    tm = _tile(S, 1024, SUBLANES)

    def body(a_ref, w_ref, h_ref, o_ref):
        acc = _dot(a_ref[0], _down_weight(w_ref, 0, c))
        for d in range(1, nb):
            acc = acc + _dot(a_ref[d], _down_weight(w_ref, d, c))
        o_ref[...] = h_ref[...] + 0.5 * acc

    return _call(
        body, "ffn_down", (S // tm,),
        [pl.BlockSpec((nb, tm, c), lambda i: (0, i, 0)), _slab(dn_w, c // 2, which), _rows(tm, D)],
        _rows(tm, D),
        _sds((S, D), F32),
    )(a, dn_w, h)


def _win_segments(C, QL, KVL, D):
    o1, o2 = 3 * C, 3 * C + QL + KVL + ROPE
    return [("bcv", k, k * C, (k + 1) * C) for k in range(3)] + [("qkr", None, o1, o2), ("gg", None, o2, o2 + 2 * D)]


def _win_pieces(segments, cw):
    out = []
    for tgt, lead, a, b in segments:
        for d in range(N_DEV):
            lo, hi = max(a, d * cw), min(b, (d + 1) * cw)
            if lo < hi:
                out.append((tgt, lead, d, (lo - d * cw, hi - d * cw), (lo - a, hi - a)))
    return out


def _win_split(win_w, C, QL, KVL):
    _, D, cw = win_w.shape
    WQ = QL + KVL + LANES
    pieces = _win_pieces(_win_segments(C, QL, KVL, D), cw)
    tr = _tile(D, 256, BF16_ROWS)

    def body(w_ref, bcv_ref, qkr_ref, gg_ref):
        tgt = dict(bcv=bcv_ref, qkr=qkr_ref, gg=gg_ref)
        qkr_ref[:, QL + KVL + ROPE:] = jnp.zeros((tr, LANES - ROPE), BF16)
        for name, lead, d, (s0, s1), (t0, t1) in pieces:
            v = w_ref[d, :, s0:s1]
            if lead is None:
                tgt[name][:, t0:t1] = v
            else:
                tgt[name][lead, :, t0:t1] = v

    return _call(
        body, "win_split", (D // tr,),
        [pl.BlockSpec((N_DEV, tr, cw), lambda i: (0, i, 0))],
        [pl.BlockSpec((3, tr, C), lambda i: (0, i, 0)), _rows(tr, WQ), _rows(tr, 2 * D)],
        [_sds((3, D, C), BF16), _sds((D, WQ), BF16), _sds((D, 2 * D), BF16)],
    )(win_w)


def _win_merge(d_bcv, d_qkr, d_gg, cw):
    _, D, C = d_bcv.shape
    WQ = d_qkr.shape[1]
    QL_KVL = WQ - LANES
    o1 = 3 * C
    segments = [("bcv", k, k * C, (k + 1) * C) for k in range(3)]
    segments += [("qkr", None, o1, o1 + QL_KVL + ROPE), ("gg", None, o1 + QL_KVL + ROPE, o1 + QL_KVL + ROPE + 2 * D)]
    pieces = _win_pieces(segments, cw)
    tr = _tile(D, 256, BF16_ROWS)

    def body(bcv_ref, qkr_ref, gg_ref, o_ref):
        src = dict(bcv=bcv_ref, qkr=qkr_ref, gg=gg_ref)
        for name, lead, d, (s0, s1), (t0, t1) in pieces:
            v = src[name][:, t0:t1] if lead is None else src[name][lead, :, t0:t1]
            o_ref[d, :, s0:s1] = v.astype(BF16)

    return _call(
        body, "win_merge", (D // tr,),
        [pl.BlockSpec((3, tr, C), lambda i: (0, i, 0)), _rows(tr, WQ), _rows(tr, 2 * D)],
        pl.BlockSpec((N_DEV, tr, cw), lambda i: (0, i, 0)),
        _sds((N_DEV, D, cw), BF16),
    )(d_bcv, d_qkr, d_gg)


def _mix_in(h, gain, w_bcv, w_qkr, w_gg):
    S, D = h.shape
    C = w_bcv.shape[2]
    tm = _tile(S, 512, SUBLANES)

    def body(h_ref, gain_ref, w1, w2, w3, o1, o2, o3, n_ref, r_ref):
        n32, rstd = _rms_fwd(h_ref[...], gain_ref[...])
        n = n32.astype(BF16)
        n_ref[...] = n
        r_ref[...] = rstd
        for k in range(3):
            o1[k] = _dot(n, w1[k])
        o2[...] = _dot(n, w2[...])
        o3[...] = _dot(n, w3[...])

    return _call(
        body, "mix_in", (S // tm,),
        [_rows(tm, D), _whole(gain), _whole(w_bcv), _whole(w_qkr), _whole(w_gg)],
        [pl.BlockSpec((3, tm, C), lambda i: (0, i, 0)), _rows(tm, w_qkr.shape[1]), _rows(tm, 2 * D),
         _rows(tm, D), _rows(tm, 1)],
        [_sds((3, S, C), F32), _sds((S, w_qkr.shape[1]), F32), _sds((S, 2 * D), F32), _sds((S, D), BF16),
         _sds((S, 1), F32)],
    )(h, gain, w_bcv, w_qkr, w_gg)


def _conv_taps(zc):
    rows = lax.broadcasted_iota(jnp.int32, zc.shape, 0)
    z1 = jnp.where(rows >= 1, pltpu.roll(zc, 1, 0), 0.0)
    z2 = jnp.where(rows >= 2, pltpu.roll(zc, 2, 0), 0.0)
    return z1, z2


def _conv_fwd(z_bcv, conv_w):
    _, S, C = z_bcv.shape

    def body(z_ref, w_ref, o_ref):
        w = w_ref[...]
        zc = z_ref[1] * z_ref[2]
        z1, z2 = _conv_taps(zc)
        y = w[0:1] * z2 + w[1:2] * z1 + w[2:3] * zc
        o_ref[...] = (z_ref[0] * y).astype(BF16)

    return _call(
        body, "conv_fwd", (C // LANES,),
        [pl.BlockSpec((3, S, LANES), lambda j: (0, 0, j)), pl.BlockSpec((3, LANES), lambda j: (0, j))],
        pl.BlockSpec((S, LANES), lambda j: (0, j)),
        _sds((S, C), BF16),
    )(z_bcv, conv_w)


def _rope(x, cs, half):
    c, s1, s2 = cs[:, :LANES], cs[:, LANES:2 * LANES], cs[:, 2 * LANES:]
    return x * c + pltpu.roll(x, LANES - half, 1) * s1 + pltpu.roll(x, half, 1) * s2


def _unrope(d, cs, half):
    c, s1, s2 = cs[:, :LANES], cs[:, LANES:2 * LANES], cs[:, 2 * LANES:]
    return d * c + pltpu.roll(d * s1, half, 1) + pltpu.roll(d * s2, LANES - half, 1)


def _mla_prep(z_qkr, gq, gkv, cs, c256_w):
    S = z_qkr.shape[0]
    QL, KVL = gq.shape[1], gkv.shape[1]
    H = N_DEV
    tm = _tile(S, 512, SUBLANES)
    half = ROPE // 2

    def body(z_ref, gq_ref, gkv_ref, cs_ref, w_ref, q_ref, k_ref, v_ref, qn_ref, kvn_ref, rq_ref, rkv_ref):
        z = z_ref[...]
        cs_t = cs_ref[...]
        qn32, rq = _rms_fwd(z[:, :QL], gq_ref[...])
        kvn32, rkv = _rms_fwd(z[:, QL:QL + KVL], gkv_ref[...])
        qn = qn32.astype(BF16)
        kvn = kvn32.astype(BF16)
        qn_ref[...] = qn
        kvn_ref[...] = kvn
        rq_ref[...] = rq
        rkv_ref[...] = rkv
        krope = _rope(z[:, QL + KVL:], cs_t, half).astype(BF16)
        for h in range(H):
            lo, mid, hi = h * HEAD_SLOT, h * HEAD_SLOT + LANES, (h + 1) * HEAD_SLOT
            q = _dot(qn, w_ref[h, KVL:KVL + QL, :])
            kv = _dot(kvn, w_ref[h, 0:KVL, :])
            q_ref[:, lo:mid] = q[:, :LANES].astype(BF16)
            q_ref[:, mid:hi] = _rope(q[:, LANES:], cs_t, half).astype(BF16)
            k_ref[:, lo:mid] = kv[:, :LANES].astype(BF16)
            k_ref[:, mid:hi] = krope
            v_ref[:, h * VDIM:(h + 1) * VDIM] = kv[:, LANES:].astype(BF16)

    return _call(
        body, "mla_prep", (S // tm,),
        [_rows(tm, z_qkr.shape[1]), _whole(gq), _whole(gkv), _rows(tm, 3 * LANES), _whole(c256_w)],
        [_rows(tm, H * HEAD_SLOT), _rows(tm, H * HEAD_SLOT), _rows(tm, H * VDIM), _rows(tm, QL), _rows(tm, KVL),
         _rows(tm, 1), _rows(tm, 1)],
        [_sds((S, H * HEAD_SLOT), BF16), _sds((S, H * HEAD_SLOT), BF16), _sds((S, H * VDIM), BF16),
         _sds((S, QL), BF16), _sds((S, KVL), BF16), _sds((S, 1), F32), _sds((S, 1), F32)],
    )(z_qkr, gq, gkv, cs, c256_w)


def _chunk_mask(t):
    shift = CHUNK.bit_length() - 1
    krow = lax.broadcasted_iota(jnp.int32, (t, t), 0) >> shift
    qcol = lax.broadcasted_iota(jnp.int32, (t, t), 1) >> shift
    return krow <= qcol


def _attn_fwd(q, k, v, H):
    S = q.shape[0]
    t = _tile(S, 512, CHUNK)
    nq = S // t

    def body(q_ref, k_ref, v_ref, o_ref, lse_ref, vt_ref):
        qi = pl.program_id(1)

        @pl.when(qi == 0)
        def _():
            vt_ref[...] = v_ref[...].T

        qv = q_ref[...]

        def block(start, width, carry, masked):
            m, l, acc = carry
            off = pl.multiple_of(start * t, t)
            s = _dot_nt(k_ref[pl.ds(off, width * t), :], qv) * SCORE_SCALE
            if masked:
                s = jnp.where(_chunk_mask(t), s, -1e30)
            m_new = jnp.maximum(m, jnp.max(s, axis=0, keepdims=True))
            alpha = jnp.exp2(m - m_new)
            p = jnp.exp2(s - m_new)
            l = alpha * l + jnp.sum(p, axis=0, keepdims=True)
            acc = alpha * acc + _dot(vt_ref[:, pl.ds(off, width * t)], p.astype(BF16))
            return m_new, l, acc

        init = (jnp.full((1, t), -1e30, F32), jnp.zeros((1, t), F32), jnp.zeros((VDIM, t), F32))
        wide = lax.div(qi, ATTN_FWD_WIDTH)
        carry = lax.fori_loop(0, wide, lambda j, c: block(j * ATTN_FWD_WIDTH, ATTN_FWD_WIDTH, c, False), init)
        carry = lax.fori_loop(wide * ATTN_FWD_WIDTH, qi, lambda kj, c: block(kj, 1, c, False), carry)
        m, l, acc = block(qi, 1, carry, True)
        o_ref[...] = (acc / l).T.astype(BF16)
        lse_ref[0] = jnp.broadcast_to(m + jnp.log2(l), (SUBLANES, t))

    return _call(
        body, "attn_fwd", (H, nq),
        [pl.BlockSpec((t, HEAD_SLOT), lambda h, i: (i, h)), pl.BlockSpec((S, HEAD_SLOT), lambda h, i: (0, h)),
         pl.BlockSpec((S, VDIM), lambda h, i: (0, h))],
        [pl.BlockSpec((t, VDIM), lambda h, i: (i, h)), pl.BlockSpec((1, SUBLANES, t), lambda h, i: (h, 0, i))],
        [_sds((S, H * VDIM), BF16), _sds((H, SUBLANES, S), F32)],
        [pltpu.VMEM((VDIM, S), BF16)],
    )(q, k, v)


def _merge_wo(o, by, z_gg, h, sq_w, c128_w):
    S, D = h.shape
    C = by.shape[1]
    r = sq_w.shape[1] // 3
    tm = _tile(S, 512, SUBLANES)

    def body(o_ref, by_ref, gg_ref, h_ref, wmo_ref, wo_ref, wco_ref, h2_ref, mg_ref, yc_ref, ym_ref):
        ymla = _dot(o_ref[...], wmo_ref[...].reshape(N_DEV * r, D))
        yconv = _dot(by_ref[...], _cat_slots(wco_ref))
        gg = gg_ref[...]
        merged = (_sig(gg[:, :D]) * yconv + _sig(gg[:, D:]) * ymla).astype(BF16)
        mg_ref[...] = merged
        yc_ref[...] = yconv.astype(BF16)
        ym_ref[...] = ymla.astype(BF16)
        h2_ref[...] = h_ref[...] + _dot(merged, wo_ref[...].reshape(N_DEV * r, D))

    return _call(
        body, "merge_wo", (S // tm,),
        [_rows(tm, o.shape[1]), _rows(tm, C), _rows(tm, 2 * D), _rows(tm, D), _slab(sq_w, r, 0), _slab(sq_w, r, 1),
         _slab(c128_w, C, 0)],
        [_rows(tm, D)] * 4,
        [_sds((S, D), F32)] + [_sds((S, D), BF16)] * 3,
    )(o, by, z_gg, h, sq_w, sq_w, c128_w)


def _ple_fwd(h, gain, p, sq_w, c128_w, C):
    S, D = h.shape
    P = p.shape[1]
    r = sq_w.shape[1] // 3
    tm = _tile(S, 512, SUBLANES)

    def body(h_ref, gain_ref, p_ref, wpg_ref, wpp_ref, o_ref, pre_ref, pp_ref, n_ref, r_ref):
        x = h_ref[...]
        n32, rstd = _rms_fwd(x, gain_ref[...])
        n = n32.astype(BF16)
        n_ref[...] = n
        r_ref[...] = rstd
        pre = _dot(n, wpg_ref[...].reshape(N_DEV * r, D))
        pp = _dot(p_ref[...].astype(BF16), _cat_slots(wpp_ref))
        pre_ref[...] = pre
        pp_ref[...] = pp
        o_ref[...] = x + _sig(pre) * pp

    return _call(
        body, "ple_fwd", (S // tm,),
        [_rows(tm, D), _whole(gain), _rows(tm, P), _slab(sq_w, r, 2), _slab(c128_w, P, C // P)],
        [_rows(tm, D), _rows(tm, D), _rows(tm, D), _rows(tm, D), _rows(tm, 1)],
        [_sds((S, D), F32)] * 3 + [_sds((S, D), BF16), _sds((S, 1), F32)],
    )(h, gain, p, sq_w, c128_w)


def _final_loss(h, gain, target):
    S, D = h.shape
    tm = _tile(S, 512, SUBLANES)

    def body(h_ref, gain_ref, t_ref, dh_ref, loss_ref, dg_ref):
        @pl.when(pl.program_id(0) == 0)
        def _():
            loss_ref[...] = jnp.zeros_like(loss_ref)
            dg_ref[...] = jnp.zeros_like(dg_ref)

        x = h_ref[...]
        gain_v = gain_ref[...]
        y, rstd = _rms_fwd(x, gain_v)
        err = y - t_ref[...]
        loss_ref[...] += 0.5 * jnp.sum(jnp.mean(err * err, axis=-1, keepdims=True))
        dx, dgain = _rms_bwd(err * (1.0 / D), x, rstd, gain_v)
        dh_ref[...] = dx
        dg_ref[...] += dgain

    return _call(
        body, "final_loss", (S // tm,),
        [_rows(tm, D), _whole(gain), _rows(tm, D)],
        [_rows(tm, D), pl.BlockSpec((1, LANES), lambda i: (0, 0)), pl.BlockSpec((1, D), lambda i: (0, 0))],
        [_sds((S, D), F32), _sds((1, LANES), F32), _sds((1, D), F32)],
    )(h, gain, target)


def _tn_call(body, name, grid, in_specs, out_spec, out_shape, scratch, operands, prev):
    n = len(operands)
    if prev is None:
        return _call(body, name, grid, in_specs, out_spec, out_shape, scratch)(*operands)
    assert prev.shape == out_shape.shape and prev.dtype == out_shape.dtype

    def wrapped(*refs):
        body(*refs[:n], *refs[n + 1:])

    return _call(wrapped, name, grid, in_specs + [ANY], out_spec, out_shape, scratch, {n: 0})(*operands, prev)


def _transposed(x_ref, xt_ref, first):
    @pl.when(first)
    def _():
        xt_ref[...] = x_ref[...].astype(BF16).T


def _tn_slots(x, dy, prev, rows_total, row_off):
    S, K = x.shape
    B, _, c = dy.shape
    tk = _tile(K, 1024, LANES)

    def body(x_ref, dy_ref, o_ref, xt_ref):
        _transposed(x_ref, xt_ref, pl.program_id(1) == 0)
        o_ref[0] = _dot(xt_ref[...], dy_ref[0]).astype(BF16)

    return _tn_call(
        body, "tn_slots", (K // tk, B),
        [pl.BlockSpec((S, tk), lambda i, b: (0, i)), pl.BlockSpec((1, S, c), lambda i, b: (b, 0, 0))],
        pl.BlockSpec((1, tk, c), lambda i, b: (b, row_off // tk + i, 0)),
        _sds((B, rows_total, c), BF16), [pltpu.VMEM((tk, S), BF16)], [x, dy], prev)


def _tn_plain(x, dy, out_dtype=F32):
    S, K = x.shape
    B, _, c = dy.shape
    tk = _tile(K, 512, LANES)
    tn = _tile(c, 1024, LANES)

    def body(x_ref, dy_ref, o_ref, xt_ref):
        _transposed(x_ref, xt_ref, (pl.program_id(1) == 0) & (pl.program_id(2) == 0))
        o_ref[0] = _dot(xt_ref[...], dy_ref[0]).astype(out_dtype)

    return _call(
        body, "tn_plain", (K // tk, B, c // tn),
        [pl.BlockSpec((S, tk), lambda i, b, j: (0, i)), pl.BlockSpec((1, S, tn), lambda i, b, j: (b, 0, j))],
        pl.BlockSpec((1, tk, tn), lambda i, b, j: (b, i, j)),
        _sds((B, K, c), out_dtype), [pltpu.VMEM((tk, S), BF16)],
    )(x, dy)


def _tn_down(at, dh, prev, rows_total, which):
    nb, c, S = at.shape
    D = dh.shape[1]
    r = c // 2
    tn = _tile(D, 512, LANES)

    def body(at_ref, dh_ref, o_ref):
        g = 0.5 * _dot(at_ref[0], dh_ref[...].astype(BF16))
        o_ref[...] = g.astype(BF16).reshape(2, r, tn)

    return _tn_call(
        body, "tn_down", (nb, D // tn),
        [pl.BlockSpec((1, c, S), lambda i, j: (i, 0, 0)), pl.BlockSpec((S, tn), lambda i, j: (0, j))],
        pl.BlockSpec((2, r, tn), lambda i, j: (i, which, j)),
        _sds((N_DEV, rows_total, D), BF16), [], [at, dh], prev)


def _tn_square(x, dy, prev, rows_total, member):
    S, K = x.shape
    N = dy.shape[1]
    r = K // N_DEV
    tk = _tile(K, 512, r)
    tn = _tile(N, 512, LANES)

    def body(x_ref, dy_ref, o_ref, xt_ref):
        _transposed(x_ref, xt_ref, pl.program_id(1) == 0)
        g = _dot(xt_ref[...], dy_ref[...].astype(BF16))
        o_ref[...] = g.astype(BF16).reshape(tk // r, r, tn)

    return _tn_call(
        body, "tn_square", (K // tk, N // tn),
        [pl.BlockSpec((S, tk), lambda i, j: (0, i)), pl.BlockSpec((S, tn), lambda i, j: (0, j))],
        pl.BlockSpec((tk // r, r, tn), lambda i, j: (i, member, j)),
        _sds((N_DEV, rows_total, N), BF16), [pltpu.VMEM((tk, S), BF16)], [x, dy], prev)


def _tn_cols(x, dy, prev, rows_total, row_block):
    S, K = x.shape
    N = dy.shape[1]
    cw = N // N_DEV

    def body(x_ref, dy_ref, o_ref):
        g = _dot(x_ref[...].astype(BF16).T, dy_ref[...])
        for d in range(N_DEV):
            o_ref[d] = g[:, d * cw:(d + 1) * cw].astype(BF16)

    return _tn_call(
        body, "tn_cols", (1,),
        [pl.BlockSpec((S, K), lambda i: (0, 0)), pl.BlockSpec((S, N), lambda i: (0, 0))],
        pl.BlockSpec((N_DEV, K, cw), lambda i: (0, row_block, 0)),
        _sds((N_DEV, rows_total, cw), BF16), [], [x, dy], prev)


def _tn_heads(qn, kvn, dqp, dkv):
    S, QL = qn.shape
    KVL = kvn.shape[1]

    def body(qn_ref, kvn_ref, dq_ref, dkv_ref, o_ref):
        o_ref[0, 0:KVL, :] = _dot(kvn_ref[...].T, dkv_ref[...]).astype(BF16)
        o_ref[0, KVL:KVL + QL, :] = _dot(qn_ref[...].T, dq_ref[...]).astype(BF16)

    head = pl.BlockSpec((S, HEAD_SLOT), lambda h: (0, h))
    return _call(
        body, "tn_heads", (N_DEV,),
        [pl.BlockSpec((S, QL), lambda h: (0, 0)), pl.BlockSpec((S, KVL), lambda h: (0, 0)), head, head],
        pl.BlockSpec((1, KVL + QL, HEAD_SLOT), lambda h: (h, 0, 0)),
        _sds((N_DEV, KVL + QL, HEAD_SLOT), BF16),
    )(qn, kvn, dqp, dkv)


def _ple_bwd(dh, pre, pp, h, rstd, gain, sq_w, after):
    S, D = h.shape
    r = sq_w.shape[1] // 3
    tm = _tile(S, 512, SUBLANES)

    def body(dh_ref, pre_ref, pp_ref, h_ref, r_ref, gain_ref, wpg_ref, *rest):
        o_ref, dpre_ref, dpp_ref, dg_ref = rest[len(after):]

        @pl.when(pl.program_id(0) == 0)
        def _():
            dg_ref[...] = jnp.zeros_like(dg_ref)

        d = dh_ref[...]
        gate = _sig(pre_ref[...])
        dpre = (d * pp_ref[...] * gate * (1.0 - gate)).astype(BF16)
        dpre_ref[...] = dpre
        dpp_ref[...] = (d * gate).astype(BF16)
        dn = _dot_nt(dpre, wpg_ref[...].reshape(N_DEV * r, D))
        dx, dgain = _rms_bwd(dn, h_ref[...], r_ref[...], gain_ref[...])
        o_ref[...] = d + dx
        dg_ref[...] += dgain

    return _call(
        body, "ple_bwd", (S // tm,),
        [_rows(tm, D), _rows(tm, D), _rows(tm, D), _rows(tm, D), _rows(tm, 1), _whole(gain), _slab(sq_w, r, 2)]
        + [ANY] * len(after),
        [_rows(tm, D), _rows(tm, D), _rows(tm, D), pl.BlockSpec((1, D), lambda i: (0, 0))],
        [_sds((S, D), F32), _sds((S, D), BF16), _sds((S, D), BF16), _sds((1, D), F32)],
    )(dh, pre, pp, h, rstd, gain, sq_w, *after)


def _ffn_bwd_act(dh, dn_w, which, jac, after=()):
    S, D = dh.shape
    _, _, c = jac.shape
    nb = N_DEV // 2
    tm = _tile(S, 512, SUBLANES)

    def body(dh_ref, w_ref, jac_ref, *rest):
        dgu_ref = rest[len(after)]
        dhb = dh_ref[...].astype(BF16)
        for d in range(nb):
            da = _dot_nt(dhb, _down_weight(w_ref, d, c))
            dgu_ref[d] = (da * jac_ref[d].astype(F32)).astype(BF16)
            dgu_ref[nb + d] = (da * jac_ref[nb + d].astype(F32)).astype(BF16)

    act = pl.BlockSpec((N_DEV, tm, c), lambda i: (0, i, 0))
    return _call(
        body, "ffn_bwd_act", (S // tm,),
        [_rows(tm, D), _slab(dn_w, c // 2, which), act] + [ANY] * len(after),
        act,
        _sds((N_DEV, S, c), BF16),
    )(dh, dn_w, jac, *after)


def _ffn_bwd_in(dgu, gu_w, which, h, rstd, gain, dh):
    S, D = h.shape
    c = dgu.shape[2]
    tm = _tile(S, 512, SUBLANES)

    def body(dgu_ref, w_ref, h_ref, r_ref, gain_ref, dh_ref, o_ref, dgain_ref):
        @pl.when(pl.program_id(0) == 0)
        def _():
            dgain_ref[...] = jnp.zeros_like(dgain_ref)

        dn = _dot_nt(dgu_ref[0], w_ref[0])
        for d in range(1, N_DEV):
            dn = dn + _dot_nt(dgu_ref[d], w_ref[d])
        dx, dgain = _rms_bwd(dn, h_ref[...], r_ref[...], gain_ref[...])
        o_ref[...] = dh_ref[...] + dx
        dgain_ref[...] += dgain

    return _call(
        body, "ffn_bwd_in", (S // tm,),
        [pl.BlockSpec((N_DEV, tm, c), lambda i: (0, i, 0)), _slab(gu_w, D, which), _rows(tm, D), _rows(tm, 1),
         _whole(gain), _rows(tm, D)],
        [_rows(tm, D), pl.BlockSpec((1, D), lambda i: (0, 0))],
        [_sds((S, D), F32), _sds((1, D), F32)],
    )(dgu, gu_w, h, rstd, gain, dh)


def _merge_bwd(dh, z_gg, yconv, ymla, o, sq_w, c128_w, C, after):
    S, D = dh.shape
    r = sq_w.shape[1] // 3
    HV = N_DEV * r
    H = HV // VDIM
    tm = _tile(S, 512, SUBLANES)

    def head_rows():
        row = lax.broadcasted_iota(jnp.int32, (SUBLANES * H, HV), 0) >> (SUBLANES.bit_length() - 1)
        col = lax.broadcasted_iota(jnp.int32, (SUBLANES * H, HV), 1) >> (VDIM.bit_length() - 1)
        return jnp.where(row == col, 1.0, 0.0).astype(BF16)

    def body(dh_ref, gg_ref, yc_ref, ym_ref, o_ref, wmo_ref, wo_ref, wco_ref, *rest):
        dgg_ref, dby_ref, do_ref, dyc_ref, dym_ref, dl_ref = rest[len(after):]
        dm = _dot_nt(dh_ref[...].astype(BF16), wo_ref[...].reshape(HV, D))
        gg = gg_ref[...]
        sgc = _sig(gg[:, :D])
        sgm = _sig(gg[:, D:])
        dyc = (dm * sgc).astype(BF16)
        dym = (dm * sgm).astype(BF16)
        dyc_ref[...] = dyc
        dym_ref[...] = dym
        dgg_ref[:, :D] = (dm * yc_ref[...].astype(F32) * sgc * (1.0 - sgc)).astype(BF16)
        dgg_ref[:, D:] = (dm * ym_ref[...].astype(F32) * sgm * (1.0 - sgm)).astype(BF16)
        dby_ref[...] = _dot_nt(dyc, _cat_slots(wco_ref))
        do = _dot_nt(dym, wmo_ref[...].reshape(HV, D)).astype(BF16)
        do_ref[...] = do
        prod = do.astype(F32) * o_ref[...].astype(F32)
        hi = prod.astype(BF16)
        lo = (prod - hi.astype(F32)).astype(BF16)
        pick = head_rows()
        dl_ref[...] = _dot_nt(pick, hi) + _dot_nt(pick, lo)

    return _call(
        body, "merge_bwd", (S // tm,),
        [_rows(tm, D), _rows(tm, 2 * D), _rows(tm, D), _rows(tm, D), _rows(tm, HV), _slab(sq_w, r, 0),
         _slab(sq_w, r, 1), _slab(c128_w, C, 0)] + [ANY] * len(after),
        [_rows(tm, 2 * D), _rows(tm, C), _rows(tm, HV), _rows(tm, D), _rows(tm, D),
         pl.BlockSpec((SUBLANES * H, tm), lambda i: (0, i))],
        [_sds((S, 2 * D), BF16), _sds((S, C), F32), _sds((S, HV), BF16), _sds((S, D), BF16), _sds((S, D), BF16),
         _sds((SUBLANES * H, S), F32)],
    )(dh, z_gg, yconv, ymla, o, sq_w, sq_w, c128_w, *after)


def _conv_bwd(z_bcv, conv_w, dby):
    _, S, C = z_bcv.shape

    def body(z_ref, w_ref, dby_ref, dz_ref, dw_ref):
        w = w_ref[...]
        c = z_ref[1]
        v = z_ref[2]
        d = dby_ref[...]
        zc = c * v
        z1, z2 = _conv_taps(zc)
        y = w[0:1] * z2 + w[1:2] * z1 + w[2:3] * zc
        dz_ref[0] = (d * y).astype(BF16)
        dy = d * z_ref[0]
        rows = lax.broadcasted_iota(jnp.int32, dy.shape, 0)
        dy1 = jnp.where(rows < S - 1, pltpu.roll(dy, S - 1, 0), 0.0)
        dy2 = jnp.where(rows < S - 2, pltpu.roll(dy, S - 2, 0), 0.0)
        dzc = w[2:3] * dy + w[1:2] * dy1 + w[0:1] * dy2
        dz_ref[1] = (dzc * v).astype(BF16)
        dz_ref[2] = (dzc * c).astype(BF16)
        dw_ref[0:1, :] = jnp.sum(dy * z2, axis=0, keepdims=True)
        dw_ref[1:2, :] = jnp.sum(dy * z1, axis=0, keepdims=True)
        dw_ref[2:3, :] = jnp.sum(dy * zc, axis=0, keepdims=True)

    three = pl.BlockSpec((3, S, LANES), lambda j: (0, 0, j))
    wspec = pl.BlockSpec((3, LANES), lambda j: (0, j))
    return _call(
        body, "conv_bwd", (C // LANES,),
        [three, wspec, pl.BlockSpec((S, LANES), lambda j: (0, j))],
        [three, wspec],
        [_sds((3, S, C), BF16), _sds((3, C), F32)],
    )(z_bcv, conv_w, dby)


def _attn_bwd(q, k, v, do, lse, delta, H):
    S = q.shape[0]
    t = _tile(S, 512, CHUNK)
    nk = S // t

    def body(q_ref, k_ref, v_ref, do_ref, lse_ref, dl_ref, dq_ref, dk_ref, dv_ref, dqt_ref):
        kj = pl.program_id(1)

        @pl.when(kj == 0)
        def _():
            dqt_ref[...] = jnp.zeros_like(dqt_ref)

        kv = k_ref[...]
        vv = v_ref[...]
        kt = kv.T

        def block(start, width, carry, masked):
            dk, dv = carry
            off = pl.multiple_of(start * t, t)
            qv = q_ref[pl.ds(off, width * t), :]
            dov = do_ref[pl.ds(off, width * t), :]
            s = _dot_nt(kv, qv) * SCORE_SCALE
            if masked:
                s = jnp.where(_chunk_mask(t), s, -1e30)
            p = jnp.exp2(s - lse_ref[0, 0:1, pl.ds(off, width * t)])
            dp = _dot_nt(vv, dov)
            ds = (p * (dp - dl_ref[0, 0:1, pl.ds(off, width * t)]) * ATTN_SCALE).astype(BF16)
            dqt_ref[:, pl.ds(off, width * t)] += _dot(kt, ds)
            return dk + _dot(ds, qv), dv + _dot(p.astype(BF16), dov)

        init = (jnp.zeros((t, HEAD_SLOT), F32), jnp.zeros((t, VDIM), F32))
        carry = block(kj, 1, init, True)
        wide = lax.div(nk - 1 - kj, ATTN_BWD_WIDTH)
        carry = lax.fori_loop(
            0, wide, lambda j, c: block(kj + 1 + j * ATTN_BWD_WIDTH, ATTN_BWD_WIDTH, c, False), carry)
        dk, dv = lax.fori_loop(kj + 1 + wide * ATTN_BWD_WIDTH, nk, lambda qi, c: block(qi, 1, c, False), carry)
        dk_ref[...] = dk
        dv_ref[...] = dv.astype(BF16)

        @pl.when(kj == nk - 1)
        def _():
            dq_ref[...] = dqt_ref[...].T

    kspec = lambda w: pl.BlockSpec((t, w), lambda h, j: (j, h))
    qspec = lambda w: pl.BlockSpec((S, w), lambda h, j: (0, h))
    stat = pl.BlockSpec((1, SUBLANES, S), lambda h, j: (h, 0, 0))
    return _call(
        body, "attn_bwd", (H, nk),
        [qspec(HEAD_SLOT), kspec(HEAD_SLOT), kspec(VDIM), qspec(VDIM), stat, stat],
        [qspec(HEAD_SLOT), kspec(HEAD_SLOT), kspec(VDIM)],
        [_sds((S, H * HEAD_SLOT), F32), _sds((S, H * HEAD_SLOT), F32), _sds((S, H * VDIM), BF16)],
        [pltpu.VMEM((HEAD_SLOT, S), F32)],
    )(q, k, v, do, lse, delta)


def _mla_prep_bwd(dq, dk, dv, z_qkr, rq, rkv, gq, gkv, cs, c256_w):
    S = z_qkr.shape[0]
    QL, KVL = gq.shape[1], gkv.shape[1]
    H = N_DEV
    tm = _tile(S, 512, SUBLANES)
    half = ROPE // 2

    def body(dq_ref, dk_ref, dv_ref, z_ref, rq_ref, rkv_ref, gq_ref, gkv_ref, cs_ref, w_ref,
             dz_ref, dqp_ref, dkv_ref, dgq_ref, dgkv_ref):
        @pl.when(pl.program_id(0) == 0)
        def _():
            dgq_ref[...] = jnp.zeros_like(dgq_ref)
            dgkv_ref[...] = jnp.zeros_like(dgkv_ref)

        cs_t = cs_ref[...]
        dkr = jnp.zeros((tm, LANES), F32)
        dqn = jnp.zeros((tm, QL), F32)
        dkvn = jnp.zeros((tm, KVL), F32)
        for h in range(H):
            lo, mid, hi = h * HEAD_SLOT, h * HEAD_SLOT + LANES, (h + 1) * HEAD_SLOT
            dqp_ref[:, lo:mid] = dq_ref[:, lo:mid].astype(BF16)
            dqp_ref[:, mid:hi] = _unrope(dq_ref[:, mid:hi], cs_t, half).astype(BF16)
            dkv_ref[:, lo:mid] = dk_ref[:, lo:mid].astype(BF16)
            dkv_ref[:, mid:hi] = dv_ref[:, h * VDIM:(h + 1) * VDIM]
            dkr = dkr + dk_ref[:, mid:hi]
            dqn = dqn + _dot_nt(dqp_ref[:, lo:hi], w_ref[h, KVL:KVL + QL, :])
            dkvn = dkvn + _dot_nt(dkv_ref[:, lo:hi], w_ref[h, 0:KVL, :])
        z = z_ref[...]
        dqc, dgq = _rms_bwd(dqn, z[:, :QL], rq_ref[...], gq_ref[...])
        dkvc, dgkv = _rms_bwd(dkvn, z[:, QL:QL + KVL], rkv_ref[...], gkv_ref[...])
        dz_ref[:, :QL] = dqc.astype(BF16)
        dz_ref[:, QL:QL + KVL] = dkvc.astype(BF16)
        dz_ref[:, QL + KVL:] = _unrope(dkr, cs_t, half).astype(BF16)
        dgq_ref[...] += dgq
        dgkv_ref[...] += dgkv

    W = z_qkr.shape[1]
    return _call(
        body, "mla_prep_bwd", (S // tm,),
        [_rows(tm, H * HEAD_SLOT), _rows(tm, H * HEAD_SLOT), _rows(tm, H * VDIM), _rows(tm, W), _rows(tm, 1),
         _rows(tm, 1), _whole(gq), _whole(gkv), _rows(tm, 3 * LANES), _whole(c256_w)],
        [_rows(tm, W), _rows(tm, H * HEAD_SLOT), _rows(tm, H * HEAD_SLOT), _whole(gq), _whole(gkv)],
        [_sds((S, W), BF16), _sds((S, H * HEAD_SLOT), BF16), _sds((S, H * HEAD_SLOT), BF16),
         _sds((1, QL), F32), _sds((1, KVL), F32)],
    )(dq, dk, dv, z_qkr, rq, rkv, gq, gkv, cs, c256_w)


def _mix_in_bwd(d_bcv, dz_qkr, dgg, w_bcv, w_qkr, w_gg, h, rstd, gain, dh):
    S, D = h.shape
    C = d_bcv.shape[2]
    tm = _tile(S, 512, SUBLANES)

    def body(db_ref, dq_ref, dgg_ref, wb_ref, wq_ref, wg_ref, h_ref, r_ref, gain_ref, dh_ref, o_ref, dgain_ref):
        @pl.when(pl.program_id(0) == 0)
        def _():
            dgain_ref[...] = jnp.zeros_like(dgain_ref)

        dn = _dot_nt(dq_ref[...], wq_ref[...]) + _dot_nt(dgg_ref[...], wg_ref[...])
        for k in range(3):
            dn = dn + _dot_nt(db_ref[k], wb_ref[k])
        dx, dgain = _rms_bwd(dn, h_ref[...], r_ref[...], gain_ref[...])
        o_ref[...] = dh_ref[...] + dx
        dgain_ref[...] += dgain

    return _call(
        body, "mix_in_bwd", (S // tm,),
        [pl.BlockSpec((3, tm, C), lambda i: (0, i, 0)), _rows(tm, dz_qkr.shape[1]), _rows(tm, dgg.shape[1]),
         _whole(w_bcv), _whole(w_qkr), _whole(w_gg), _rows(tm, D), _rows(tm, 1), _whole(gain), _rows(tm, D)],
        [_rows(tm, D), pl.BlockSpec((1, D), lambda i: (0, 0))],
        [_sds((S, D), F32), _sds((1, D), F32)],
    )(d_bcv, dz_qkr, dgg, w_bcv, w_qkr, w_gg, h, rstd, gain, dh)


def _rope_tables(positions):
    half = ROPE // 2
    inv_freq = ROPE_THETA ** (-jnp.arange(0, ROPE, 2, dtype=F32) / ROPE)
    ang = positions.astype(F32)[:, None] * inv_freq
    cos, sin = jnp.cos(ang), jnp.sin(ang)
    z = jnp.zeros_like(cos)
    pad = jnp.zeros((positions.shape[0], LANES - 2 * half), F32)
    return jnp.concatenate([cos, cos, pad, -sin, z, pad, z, sin, pad], axis=1)


def _grad_rows(w):
    return dict(gu=2 * w["gu1"].shape[1], dn=2 * w["dn1"].shape[1], sq=w["sq"].shape[1], win=w["win"].shape[1],
                c128=w["c128"].shape[1], c256=w["c256"].shape[1])


def _layer_fwd(h0, p_l, cs, w, sm, late):
    C = sm["conv_w"].shape[1]
    QL, KVL = sm["q_norm"].shape[1], sm["kv_norm"].shape[1]
    jac1, a1, at1, n1, r1 = _ffn_up(h0, sm["ffn1_norm"], w["gu1"], 0)
    h1 = _ffn_down(a1, w["dn1"], 0, h0)
    if late is not None:
        w.update(late(h1))
    w_bcv, w_qkr, w_gg = _win_split(w["win"], C, QL, KVL)
    z_bcv, z_qkr, z_gg, un, rm = _mix_in(h1, sm["mix_norm"], w_bcv, w_qkr, w_gg)
    by = _conv_fwd(z_bcv, sm["conv_w"])
    q, k, v, qn, kvn, rq, rkv = _mla_prep(z_qkr, sm["q_norm"], sm["kv_norm"], cs, w["c256"])
    o, lse = _attn_fwd(q, k, v, N_DEV)
    h2, merged, yconv, ymla = _merge_wo(o, by, z_gg, h1, w["sq"], w["c128"])
    jac2, a2, at2, n2, r2 = _ffn_up(h2, sm["ffn2_norm"], w["gu2"], 0)
    h3 = _ffn_down(a2, w["dn2"], 0, h2)
    h4, pre, pp, pn, rp = _ple_fwd(h3, sm["ple_norm"], p_l, w["sq"], w["c128"], C)
    saved = dict(h0=h0, jac1=jac1, at1=at1, n1=n1, r1=r1, h1=h1, w_bcv=w_bcv, w_qkr=w_qkr, w_gg=w_gg, z_bcv=z_bcv,
                 z_qkr=z_qkr, z_gg=z_gg, un=un, rm=rm, by=by, q=q, k=k, v=v, qn=qn, kvn=kvn, rq=rq, rkv=rkv, o=o,
                 lse=lse, h2=h2, merged=merged, yconv=yconv, ymla=ymla, jac2=jac2, at2=at2, n2=n2, r2=r2, h3=h3,
                 pre=pre, pp=pp, pn=pn, rp=rp, p=p_l)
    return h4, saved


def _layer_bwd_late(dh4, s, w, sm, after):
    D = dh4.shape[1]
    C = sm["conv_w"].shape[1]
    P = s["p"].shape[1]
    rows = _grad_rows(w)
    small = {}
    dh3, dpre, dpp, small["ple_norm"] = _ple_bwd(dh4, s["pre"], s["pp"], s["h3"], s["rp"], sm["ple_norm"], w["sq"],
                                                 after)
    g_sq = _tn_square(s["pn"], dpre, None, rows["sq"], 2)
    g_c128 = _tn_cols(s["p"], dpp, None, rows["c128"], C // P)

    dgu2 = _ffn_bwd_act(dh3, w["dn2"], 0, s["jac2"])
    g_dn = _tn_down(s["at2"], dh3, None, rows["dn"], 1)
    g_gu = _tn_slots(s["n2"], dgu2, None, rows["gu"], D)
    dh2, small["ffn2_norm"] = _ffn_bwd_in(dgu2, w["gu2"], 0, s["h2"], s["r2"], sm["ffn2_norm"], dh3)
    return dh2, dict(gu=g_gu, dn=g_dn, sq=g_sq, c128=g_c128), small


def _layer_bwd_mixer(dh2, part, small, s, cs, w, sm, after):
    C = sm["conv_w"].shape[1]
    rows = _grad_rows(w)
    g_gu, g_dn, g_sq, g_c128 = part["gu"], part["dn"], part["sq"], part["c128"]

    dgg, dby, do, dyc, dym, delta = _merge_bwd(dh2, s["z_gg"], s["yconv"], s["ymla"], s["o"], w["sq"], w["c128"], C,
                                               after)
    g_sq = _tn_square(s["merged"], dh2, g_sq, rows["sq"], 1)
    g_sq = _tn_square(s["o"], dym, g_sq, rows["sq"], 0)
    g_c128 = _tn_cols(s["by"], dyc, g_c128, rows["c128"], 0)
    d_bcv, small["conv_w"] = _conv_bwd(s["z_bcv"], sm["conv_w"], dby)
    delta = delta.reshape(N_DEV, SUBLANES, delta.shape[1])
    dq, dk, dv = _attn_bwd(s["q"], s["k"], s["v"], do, s["lse"], delta, N_DEV)
    dz_qkr, dqp, dkv, small["q_norm"], small["kv_norm"] = _mla_prep_bwd(
        dq, dk, dv, s["z_qkr"], s["rq"], s["rkv"], sm["q_norm"], sm["kv_norm"], cs, w["c256"])
    g_c256 = _tn_heads(s["qn"], s["kvn"], dqp, dkv)
    un = s["un"]
    g_win = _win_merge(_tn_plain(un, d_bcv), _tn_plain(un, dz_qkr[None])[0], _tn_plain(un, dgg[None])[0],
                       w["win"].shape[2])
    dh1, small["mix_norm"] = _mix_in_bwd(d_bcv, dz_qkr, dgg, s["w_bcv"], s["w_qkr"], s["w_gg"], s["h1"], s["rm"],
                                         sm["mix_norm"], dh2)
    return dh1, dict(gu=g_gu, dn=g_dn, sq=g_sq, win=g_win, c128=g_c128, c256=g_c256), small


def _layer_bwd_first(dh1, part, small, s, w, sm, after):
    rows = _grad_rows(w)
    dgu1 = _ffn_bwd_act(dh1, w["dn1"], 0, s["jac1"], after)
    g_dn = _tn_down(s["at1"], dh1, part["dn"], rows["dn"], 0)
    g_gu = _tn_slots(s["n1"], dgu1, part["gu"], rows["gu"], 0)
    dh0, small["ffn1_norm"] = _ffn_bwd_in(dgu1, w["gu1"], 0, s["h0"], s["r1"], sm["ffn1_norm"], dh1)
    return dh0, dict(part, gu=g_gu, dn=g_dn), small


def _mesh_pos():
    return lax.axis_index("x"), lax.axis_index("y"), lax.axis_index("c")


def _other_chips(x, y):
    return [(1 - x, y), (x, 1 - y), (1 - x, 1 - y)]


def _pack(arrs, flipped, width):
    L = arrs[0].shape[0]
    shapes = [a.shape[:0:-1] if f else a.shape[1:] for a, f in zip(arrs, flipped)]
    R = sum(r for r, _ in shapes)

    def body(*refs):
        o_ref = refs[-1]
        off = 0
        for a_ref, f, (r, c) in zip(refs[:-1], flipped, shapes):
            a = a_ref[0].T if f else a_ref[0]
            o_ref[0, off:off + r, 0:c] = a.astype(BF16)
            if c < width:
                o_ref[0, off:off + r, c:width] = jnp.zeros((r, width - c), BF16)
            off += r

    return _call(
        body, "pack", (L,),
        [pl.BlockSpec((1,) + a.shape[1:], lambda l: (l, 0, 0)) for a in arrs],
        pl.BlockSpec((1, R, width), lambda l: (l, 0, 0)),
        _sds((L, R, width), BF16),
    )(*arrs)


def _handshake(peers):
    barrier = pltpu.get_barrier_semaphore()
    for peer in peers:
        pl.semaphore_signal(barrier, inc=1, device_id=peer, device_id_type=MESH)
    pl.semaphore_wait(barrier, len(peers))


def _sequencer_call(body, name, out_types, sems, collective_id, operands):
    return pl.kernel(
        body, name=name, out_type=out_types,
        mesh=plsc.ScalarSubcoreMesh(axis_name="seq", num_cores=1),
        scratch_types=tuple(pltpu.SemaphoreType.DMA((k,)) for k in sems),
        compiler_params=pltpu.CompilerParams(collective_id=collective_id),
    )(*operands)


def _all_gather(packs, l, after, collective_id):
    n = len(packs)

    def body(*refs):
        ins, outs = refs[:n], refs[n + len(after):2 * n + len(after)]
        send_sems, recv_sems, local_sems = refs[2 * n + len(after):]
        x, y, c = _mesh_pos()
        me, sibling = (x, y, c), (x, y, 1 - c)
        chips = _other_chips(x, y)
        _handshake([sibling] + [(*chip, c) for chip in chips])

        def copy(q, k, block, to, src=None):
            slot = outs[q].at[4 * block[0] + 2 * block[1] + block[2]]
            return pltpu.make_async_remote_copy(
                src_ref=slot if src is None else src, dst_ref=slot,
                send_sem=send_sems.at[7 * q + k], recv_sem=recv_sems.at[7 * q + k], device_id=to, device_id_type=MESH)

        started = []
        for q in range(n):
            src = ins[q].at[l]
            mine = pltpu.make_async_copy(src, outs[q].at[4 * x + 2 * y + c], local_sems.at[q])
            mine.start()
            started.append(mine)
        sends = []
        for q in range(n):
            src = ins[q].at[l]
            sends.append(copy(q, 0, me, sibling, src=src))
            sends += [copy(q, 1 + j, me, (*chip, c), src=src) for j, chip in enumerate(chips)]
        for cp in sends:
            cp.start()
        for q in range(n):
            for j, chip in enumerate(chips):
                copy(q, 1 + j, (*chip, c), me).wait_recv()
                fwd = copy(q, 4 + j, (*chip, c), sibling)
                fwd.start()
                sends.append(fwd)
        for q in range(n):
            copy(q, 0, sibling, me).wait_recv()
            for j, chip in enumerate(chips):
                copy(q, 4 + j, (*chip, 1 - c), me).wait_recv()
        for cp in sends:
            cp.wait_send()
        for mine in started:
            mine.wait()

    return _sequencer_call(
        body, f"all_gather_{collective_id}", [_sds((N_DEV,) + p.shape[1:], p.dtype) for p in packs], (7 * n, 7 * n, n),
        collective_id, list(packs) + list(after))


def _rs_d2d(gs, l, collective_id):
    n = len(gs)

    def body(*refs):
        ins, outs = refs[:n], refs[n:2 * n]
        send_sems, recv_sems = refs[2 * n:]
        x, y, c = _mesh_pos()
        _handshake([(x, y, 1 - c)])
        copies = []
        for q in range(n):
            for j in range(4):
                copies.append(pltpu.make_async_remote_copy(
                    src_ref=ins[q].at[2 * j + (1 - c)], dst_ref=outs[q].at[j], send_sem=send_sems.at[4 * q + j],
                    recv_sem=recv_sems.at[4 * q + j], device_id=(x, y, 1 - c), device_id_type=MESH))
        for cp in copies:
            cp.start()
        for cp in copies:
            cp.wait()

    return _sequencer_call(
        body, f"rs_d2d_{l}", [_sds((4,) + g.shape[1:], g.dtype) for g in gs], (4 * n, 4 * n), collective_id, gs)


def _rs_add_chip(gs, as_, after):
    n = len(gs)
    steps = 4
    tiles = [g.shape[1] // steps for g in gs]

    def chip(k):
        x, y, _ = _mesh_pos()
        return ([(x, y)] + _other_chips(x, y))[k]

    def body(*refs):
        g_refs, a_refs = refs[:4 * n], refs[4 * n:8 * n]
        own_refs, t_refs = refs[8 * n + len(after):9 * n + len(after)], refs[9 * n + len(after):]
        for q in range(n):
            g, a = g_refs[4 * q:4 * q + 4], a_refs[4 * q:4 * q + 4]
            own_refs[q][...] = g[0][0].astype(F32) + a[0][0].astype(F32)
            for k in range(1, 4):
                t_refs[q][k - 1] = (g[k][0].astype(F32) + a[k][0].astype(F32)).astype(BF16)

    def gspec(q, k):
        def index(i):
            px, py = chip(k)
            return 4 * px + 2 * py + lax.axis_index("c"), i, 0
        return pl.BlockSpec((1, tiles[q], gs[q].shape[2]), index)

    def aspec(q, k):
        def index(i):
            px, py = chip(k)
            return 2 * px + py, i, 0
        return pl.BlockSpec((1, tiles[q], gs[q].shape[2]), index)

    in_specs = [gspec(q, k) for q in range(n) for k in range(4)] + [aspec(q, k) for q in range(n) for k in range(4)]
    operands = [g for g in gs for _ in range(4)] + [a for a in as_ for _ in range(4)]
    out_specs = [pl.BlockSpec((tiles[q], gs[q].shape[2]), lambda i: (i, 0)) for q in range(n)]
    out_specs += [pl.BlockSpec((3, tiles[q], gs[q].shape[2]), lambda i: (0, i, 0)) for q in range(n)]
    out_shape = [_sds(g.shape[1:], F32) for g in gs] + [_sds((3,) + g.shape[1:], BF16) for g in gs]
    res = _call(body, "rs_add_chip", (steps,), in_specs + [ANY] * len(after), out_specs, out_shape)(*operands, *after)
    return res[:n], res[n:]


def _rs_ici(ts, l, collective_id):
    n = len(ts)

    def body(*refs):
        ins, outs = refs[:n], refs[n:2 * n]
        send_sems, recv_sems = refs[2 * n:]
        x, y, c = _mesh_pos()
        chips = _other_chips(x, y)
        _handshake([(*chip, c) for chip in chips])
        copies = []
        for q in range(n):
            for k, chip in enumerate(chips):
                copies.append(pltpu.make_async_remote_copy(
                    src_ref=ins[q].at[k], dst_ref=outs[q].at[k], send_sem=send_sems.at[3 * q + k],
                    recv_sem=recv_sems.at[3 * q + k], device_id=(*chip, c), device_id_type=MESH))
        for cp in copies:
            cp.start()
        for cp in copies:
            cp.wait()

    return _sequencer_call(
        body, f"rs_ici_{l}", [_sds(t.shape, t.dtype) for t in ts], (3 * n, 3 * n), collective_id, ts)


def _all_reduce_small(v):
    n, W = v.shape

    def body(v_ref, out_ref, slots, send_sems, recv_sems):
        x, y, c = _mesh_pos()
        me = 4 * x + 2 * y + c
        slots[me] = v_ref[...]
        copies = []
        for k in range(1, N_DEV):
            kx, ky, kc = (k >> 2) & 1, (k >> 1) & 1, k & 1
            peer = (1 - x if kx else x, 1 - y if ky else y, 1 - c if kc else c)
            copies.append(pltpu.make_async_remote_copy(
                src_ref=v_ref, dst_ref=slots.at[me], send_sem=send_sems.at[k - 1], recv_sem=recv_sems.at[k - 1],
                device_id=peer, device_id_type=MESH))
        for cp in copies:
            cp.start()
        for cp in copies:
            cp.wait()
        acc = slots[0]
        for d in range(1, N_DEV):
            acc = acc + slots[d]
        out_ref[...] = acc

    vm = pl.BlockSpec(memory_space=pltpu.VMEM)
    return pl.pallas_call(
        body, name="all_reduce_small",
        out_shape=_sds((n, W), F32),
        in_specs=[vm], out_specs=vm,
        scratch_shapes=[pltpu.VMEM((N_DEV, n, W), F32), pltpu.SemaphoreType.DMA((7,)), pltpu.SemaphoreType.DMA((7,))],
    )(v)


def _adamw_math(w, g, m, v):
    m2 = ADAM_B1 * m + (1.0 - ADAM_B1) * g
    v2 = ADAM_B2 * v + (1.0 - ADAM_B2) * (g * g)
    m_hat = m2 / (1.0 - ADAM_B1 ** ADAM_STEP)
    v_hat = v2 / (1.0 - ADAM_B2 ** ADAM_STEP)
    return -ADAM_LR * (m_hat / (jnp.sqrt(v_hat) + ADAM_EPS) + ADAM_WD * w), m2, v2


def _adamw(w, g, m, v):
    L, r, c = w.shape
    tr = _tile(r, max(SUBLANES, (256 * 1024 // c) // SUBLANES * SUBLANES), SUBLANES)

    def body(w_ref, g_ref, m_ref, v_ref, d_ref, nm_ref, nv_ref):
        d_ref[...], nm_ref[...], nv_ref[...] = _adamw_math(w_ref[...], g_ref[...], m_ref[...], v_ref[...])

    spec = pl.BlockSpec((1, tr, c), lambda l, i: (l, i, 0))
    return _call(body, "adamw", (L, r // tr), [spec] * 4, [spec] * 3, [_sds((L, r, c), F32)] * 3)(w, g, m, v)


def _adamw_reduced(w, m, v, flipped, own, b, row_off, tr, l, prev, after):
    L = w.shape[0]
    c, r = w.shape[1:] if flipped else w.shape[:0:-1]
    W = own.shape[1]
    ob = row_off // tr
    extra = list(prev or ()) + list(after)

    def body(w_ref, m_ref, v_ref, own_ref, b_ref, *rest):
        g_ref, d_ref, nm_ref, nv_ref = rest[len(extra):]
        g = ((own_ref[...] + b_ref[0].astype(F32)) + b_ref[1].astype(F32)) + b_ref[2].astype(F32)
        g = g[:, :c].T if flipped else g[:, :c]
        g_ref[0] = g
        d_ref[0], nm_ref[0], nv_ref[0] = _adamw_math(w_ref[0], g, m_ref[0], v_ref[0])

    spec = pl.BlockSpec((1, c, tr), lambda i: (l, 0, i)) if flipped else pl.BlockSpec((1, tr, c), lambda i: (l, i, 0))
    return _call(
        body, "adamw_reduced", (r // tr,),
        [spec] * 3 + [pl.BlockSpec((tr, W), lambda i: (ob + i, 0)), pl.BlockSpec((3, tr, W), lambda i: (0, ob + i, 0))]
        + [ANY] * len(extra),
        [spec] * 4, [_sds(w.shape, F32)] * 4,
        aliases={5 + k: k for k in range(4)} if prev else None,
    )(w, m, v, own, b, *extra)


_MEMBERS = dict(gu=("ffn1_w_gu", "ffn2_w_gu"), dn=("ffn1_w_down", "ffn2_w_down"),
                sq=("w_mla_out", "w_o", "w_ple_gate"), win=("w_in",), c128=("w_conv_out", "w_ple_proj"),
                c256=("w_ukv", "w_uq"))
_GATHER_MEMBERS = dict(_MEMBERS, gu1=("ffn1_w_gu",), gu2=("ffn2_w_gu",), dn1=("ffn1_w_down",), dn2=("ffn2_w_down",))
GATHER_STAGES = (("gu1", "dn1"), ("win", "c256", "c128", "sq"), ("gu2", "dn2"))
_FLIPPED = ("ffn1_w_gu", "ffn2_w_gu", "w_in", "w_uq")
_SMALL = ("ffn1_norm", "mix_norm", "q_norm", "kv_norm", "ffn2_norm", "ple_norm")
_ORDER = ("ffn1_norm", "ffn1_w_gu", "ffn1_w_down", "mix_norm", "w_in", "conv_w", "w_conv_out", "q_norm", "kv_norm",
          "w_uq", "w_ukv", "w_mla_out", "w_o", "ffn2_norm", "ffn2_w_gu", "ffn2_w_down", "ple_norm", "w_ple_gate",
          "w_ple_proj", "final_norm")


def _class_width(wts, cls):
    return HEAD_SLOT if cls == "c256" else wts[_GATHER_MEMBERS[cls][0]].shape[2]


def _pack_rows(vecs, width):
    flat = jnp.concatenate([a.reshape(-1) for a in vecs])
    n = flat.shape[0]
    rows = -(-n // width)
    rows = -(-rows // SUBLANES) * SUBLANES
    flat = jnp.pad(flat, (0, rows * width - n))
    offs, o = [], 0
    for a in vecs:
        offs.append(o)
        o += a.size
    return flat.reshape(rows, width), offs


def _unpack_rows(packed, vecs, offs):
    flat = packed.reshape(-1)
    return [flat[o:o + a.size].reshape(a.shape) for a, o in zip(vecs, offs)]


def _train(x, p, positions, target, gathered, packs, small_w, final_norm, update):
    cs = _rope_tables(positions)
    L = len(small_w)
    h = x
    saved = []
    def gather(l, names, after, collective_id):
        got = _all_gather([packs[n] for n in names], l, after, collective_id)
        return dict(zip(names, got))

    late = None
    if packs is not None:
        first, mixer, second = GATHER_STAGES
        w0 = gather(0, first, [], 0)
        w0.update(gather(0, mixer, [w0[first[0]]], 1))
        gathered = [w0]
        late = lambda h1: gather(0, second, [h1], 2)
    everything = sum(GATHER_STAGES, ())
    for l in range(L):
        h, s = _layer_fwd(h, p[l], cs, gathered[l], small_w[l], late)
        late = None
        saved.append(s)
        if packs is not None and l + 1 < L:
            gathered.append(gather(l + 1, everything, [s["by"]], 2 + l + 1))
    dh, loss, d_final = _final_loss(h, final_norm, target)
    grads, smalls = [None] * L, [None] * L
    exchanged = None
    landing = None

    def second_stage(after):
        l, gs, as_ = exchanged
        owns, ts = _rs_add_chip(gs, as_, [after])
        return l, owns, _rs_ici(ts, l, 2 * L + 2 + l)

    for l in reversed(range(L)):
        dh, part, small = _layer_bwd_late(dh, saved[l], gathered[l], small_w[l], [])
        pin = []
        if exchanged is not None:
            landing = second_stage(dh)
            pin = [landing[1][0]]
        dh, part, small = _layer_bwd_mixer(dh, part, small, saved[l], cs, gathered[l], small_w[l], pin)
        pin = [update(*landing)] if exchanged is not None else []
        dh, g, smalls[l] = _layer_bwd_first(dh, part, small, saved[l], gathered[l], small_w[l], pin)
        if update is not None:
            gs = [g[cls] for cls in CLASSES]
            exchanged = (l, gs, _rs_d2d(gs, l, L + 2 + l))
        else:
            grads[l] = g
    if update is not None:
        update(*second_stage(dh))
    return loss[0, 0], dh, grads, smalls, d_final


def kernel(x, p, positions, ffn1_norm, ffn1_w_gu, ffn1_w_down, mix_norm, w_in, conv_w, w_conv_out, q_norm, kv_norm, w_uq, w_ukv, w_mla_out, w_o, ffn2_norm, ffn2_w_gu, ffn2_w_down, ple_norm, w_ple_gate, w_ple_proj, final_norm, loss_target, m_ffn1_norm, m_ffn1_w_gu, m_ffn1_w_down, m_mix_norm, m_w_in, m_conv_w, m_w_conv_out, m_q_norm, m_kv_norm, m_w_uq, m_w_ukv, m_w_mla_out, m_w_o, m_ffn2_norm, m_ffn2_w_gu, m_ffn2_w_down, m_ple_norm, m_w_ple_gate, m_w_ple_proj, m_final_norm, v_ffn1_norm, v_ffn1_w_gu, v_ffn1_w_down, v_mix_norm, v_w_in, v_conv_w, v_w_conv_out, v_q_norm, v_kv_norm, v_w_uq, v_w_ukv, v_w_mla_out, v_w_o, v_ffn2_norm, v_ffn2_w_gu, v_ffn2_w_down, v_ple_norm, v_w_ple_gate, v_w_ple_proj, v_final_norm):
    args = dict(locals())
    wts = {n: args[n] for n in _ORDER}
    L = w_in.shape[0]
    dev = 4 * lax.axis_index("x") + 2 * lax.axis_index("y") + lax.axis_index("c")

    view = lambda n, a: jnp.swapaxes(a, 1, 2) if n in _FLIPPED else a
    packs = {cls: _pack([view(n, wts[n]) for n in _GATHER_MEMBERS[cls]], [n in _FLIPPED for n in _GATHER_MEMBERS[cls]],
                        _class_width(wts, cls))
             for stage in GATHER_STAGES for cls in stage}
    cw = conv_w.shape[2]
    conv_full = lax.dynamic_update_slice(jnp.zeros((L, 3, N_DEV * cw), F32), conv_w, (0, 0, dev * cw))
    conv_packed, conv_offs = _pack_rows([conv_full], FLAT_COLS)
    conv_full = _unpack_rows(_all_reduce_small(conv_packed), [conv_full], conv_offs)[0]
    small_w = [dict({n: wts[n][l][None, :] for n in _SMALL}, conv_w=conv_full[l]) for l in range(L)]

    done = {}

    def update(l, owns, bs):
        for q, cls in enumerate(CLASSES):
            off = 0
            rows = [wts[n].shape[1] for n in _MEMBERS[cls]]
            tr = _tile(math.gcd(*rows), 256, BF16_ROWS)
            for n, r in zip(_MEMBERS[cls], rows):
                done[n] = _adamw_reduced(view(n, wts[n]), view(n, args["m_" + n]), view(n, args["v_" + n]),
                                         n in _FLIPPED, owns[q], bs[q], off, tr, l, done.get(n), [])
                off += r
        return done[_MEMBERS[CLASSES[-1]][-1]][0]

    loss_dev, grad_x, _, smalls, d_final = _train(x[0], p[:, 0], positions[0], loss_target[0], None, packs, small_w,
                                                  final_norm[None, :], update)

    small = [jnp.stack([smalls[l][n][0] for l in range(L)]) for n in _SMALL]
    small += [jnp.stack([smalls[l]["conv_w"] for l in range(L)]), d_final[0], loss_dev[None]]
    packed, offs = _pack_rows(small, FLAT_COLS)
    small = _unpack_rows(_all_reduce_small(packed), small, offs)
    grad = dict(zip(_SMALL, small))
    grad["conv_w"] = lax.dynamic_slice(small[len(_SMALL)], (0, 0, dev * cw), (L, 3, cw))
    grad["final_norm"] = small[-2]
    loss = small[-1][0]

    deltas, new_m, new_v = {}, {}, {}
    for n, outs in done.items():
        grad[n], deltas[n], new_m[n], new_v[n] = (view(n, a) for a in outs)
    for n in _SMALL + ("conv_w", "final_norm"):
        w3 = wts[n].reshape((1,) * (3 - wts[n].ndim) + wts[n].shape)
        d, nm, nv = _adamw(w3, grad[n].reshape(w3.shape), args["m_" + n].reshape(w3.shape),
                           args["v_" + n].reshape(w3.shape))
        deltas[n], new_m[n], new_v[n] = (a.reshape(wts[n].shape) for a in (d, nm, nv))
    return (loss, grad_x[None], *[grad[n] for n in _ORDER], *[deltas[n] for n in _ORDER],
            *[new_m[n] for n in _ORDER], *[new_v[n] for n in _ORDER])
```

```python
import math

import jax
import jax.numpy as jnp
from jax import lax
from jax.experimental import pallas as pl
from jax.experimental.pallas import tpu as pltpu
from jax.experimental.pallas import tpu_sc as plsc

F32 = jnp.float32
BF16 = jnp.bfloat16

CHUNK = 64
NOPE = 128
ROPE = 64
VDIM = 128
ROPE_THETA = 10000.0
EPS = 1e-6
ATTN_SCALE = (NOPE + ROPE) ** -0.5
SCORE_SCALE = ATTN_SCALE * math.log2(math.e)
ADAM_LR = 0.001
ADAM_B1 = 0.9
ADAM_B2 = 0.999
ADAM_EPS = 1e-08
ADAM_WD = 0.01
ADAM_STEP = 10

LANES = 128
SUBLANES = 8
BF16_ROWS = 16
V7X_VMEM_BYTES = 64 * 1024 * 1024
VMEM_LIMIT = V7X_VMEM_BYTES * 7 // 8
HEAD_SLOT = 2 * LANES
N_DEV = 8
ATTN_FWD_WIDTH = 4
ATTN_BWD_WIDTH = 2
FLAT_COLS = 1024
CLASSES = ("gu", "dn", "sq", "win", "c128", "c256")

NT = (((1,), (1,)), ((), ()))
MESH = pl.DeviceIdType.MESH
ANY = pl.BlockSpec(memory_space=pl.ANY)


def _dot(a, b):
    return jnp.dot(a, b, preferred_element_type=F32)


def _dot_nt(a, b):
    return lax.dot_general(a, b, NT, preferred_element_type=F32)


def _sig(x):
    return 1.0 / (1.0 + jnp.exp(-x))


def _tile(n, pref, unit):
    if n <= pref:
        return n
    t = (pref // unit) * unit
    while t >= unit:
        if n % t == 0:
            return t
        t -= unit
    return n


def _call(body, name, grid, in_specs, out_specs, out_shape, scratch=(), aliases=None):
    return pl.pallas_call(
        body,
        name=name,
        grid=grid,
        in_specs=in_specs,
        out_specs=out_specs,
        out_shape=out_shape,
        scratch_shapes=list(scratch),
        input_output_aliases=aliases or {},
        compiler_params=pltpu.CompilerParams(
            dimension_semantics=("arbitrary",) * len(grid), vmem_limit_bytes=VMEM_LIMIT
        ),
    )


def _sds(shape, dtype):
    return jax.ShapeDtypeStruct(shape, dtype)


def _rms_fwd(x, gain):
    rstd = lax.rsqrt(jnp.mean(x * x, axis=-1, keepdims=True) + EPS)
    return x * rstd * gain, rstd


def _rms_bwd(dn, x, rstd, gain):
    xhat = x * rstd
    dgy = dn * gain
    dx = rstd * (dgy - xhat * jnp.mean(dgy * xhat, axis=-1, keepdims=True))
    return dx, jnp.sum(dn * xhat, axis=0, keepdims=True)


def _rows(tm, w):
    return pl.BlockSpec((tm, w), lambda i: (i, 0))


def _whole(a):
    nd = a.ndim
    return pl.BlockSpec(a.shape, lambda i: (0,) * nd, pipeline_mode=pl.Buffered(1))


def _slab(buf, rows, index):
    return pl.BlockSpec((N_DEV, rows, buf.shape[2]), lambda i: (0, index, 0), pipeline_mode=pl.Buffered(1))


def _cat_slots(w):
    return jnp.concatenate([w[d] for d in range(N_DEV)], axis=1)


def _ffn_up(h, gain, gu_w, which):
    S, D = h.shape
    c = gu_w.shape[2]
    tm = _tile(S, 512, SUBLANES)
    nb = N_DEV // 2

    def body(h_ref, gain_ref, w_ref, jac_ref, a_ref, at_ref, n_ref, r_ref):
        n32, rstd = _rms_fwd(h_ref[...], gain_ref[...])
        n = n32.astype(BF16)
        n_ref[...] = n
        r_ref[...] = rstd
        for d in range(nb):
            g = _dot(n, w_ref[d])
            u = _dot(n, w_ref[nb + d])
            sg = _sig(g)
            silu = g * sg
            a = (silu * u).astype(BF16)
            a_ref[d] = a
            at_ref[d] = a.T
            jac_ref[d] = (0.5 * u * (sg + silu * (1.0 - sg))).astype(BF16)
            jac_ref[nb + d] = (0.5 * silu).astype(BF16)

    return _call(
        body, "ffn_up", (S // tm,),
        [_rows(tm, D), _whole(gain), _slab(gu_w, D, which)],
        [pl.BlockSpec((N_DEV, tm, c), lambda i: (0, i, 0)), pl.BlockSpec((nb, tm, c), lambda i: (0, i, 0)),
         pl.BlockSpec((nb, c, tm), lambda i: (0, 0, i)), _rows(tm, D), _rows(tm, 1)],
        [_sds((N_DEV, S, c), BF16), _sds((nb, S, c), BF16), _sds((nb, c, S), BF16), _sds((S, D), BF16),
         _sds((S, 1), F32)],
    )(h, gain, gu_w)


def _down_weight(w_ref, d, c):
    return w_ref[2 * d:2 * d + 2].reshape(c, w_ref.shape[2])


def _ffn_down(a, dn_w, which, h):
    nb, S, c = a.shape
    D = h.shape[1]
    tm = _tile(S, 1024, SUBLANES)

    def body(a_ref, w_ref, h_ref, o_ref):
        acc = _dot(a_ref[0], _down_weight(w_ref, 0, c))
        for d in range(1, nb):
            acc = acc + _dot(a_ref[d], _down_weight(w_ref, d, c))
        o_ref[...] = h_ref[...] + 0.5 * acc

    return _call(
        body, "ffn_down", (S // tm,),
        [pl.BlockSpec((nb, tm, c), lambda i: (0, i, 0)), _slab(dn_w, c // 2, which), _rows(tm, D)],
        _rows(tm, D),
        _sds((S, D), F32),
    )(a, dn_w, h)


def _win_segments(C, QL, KVL, D):
    o1, o2 = 3 * C, 3 * C + QL + KVL + ROPE
    return [("bcv", k, k * C, (k + 1) * C) for k in range(3)] + [("qkr", None, o1, o2), ("gg", None, o2, o2 + 2 * D)]


def _win_pieces(segments, cw):
    out = []
    for tgt, lead, a, b in segments:
        for d in range(N_DEV):
            lo, hi = max(a, d * cw), min(b, (d + 1) * cw)
            if lo < hi:
                out.append((tgt, lead, d, (lo - d * cw, hi - d * cw), (lo - a, hi - a)))
    return out


def _win_split(win_w, C, QL, KVL):
    _, D, cw = win_w.shape
    WQ = QL + KVL + LANES
    pieces = _win_pieces(_win_segments(C, QL, KVL, D), cw)
    tr = _tile(D, 256, BF16_ROWS)

    def body(w_ref, bcv_ref, qkr_ref, gg_ref):
        tgt = dict(bcv=bcv_ref, qkr=qkr_ref, gg=gg_ref)
        qkr_ref[:, QL + KVL + ROPE:] = jnp.zeros((tr, LANES - ROPE), BF16)
        for name, lead, d, (s0, s1), (t0, t1) in pieces:
            v = w_ref[d, :, s0:s1]
            if lead is None:
                tgt[name][:, t0:t1] = v
            else:
                tgt[name][lead, :, t0:t1] = v

    return _call(
        body, "win_split", (D // tr,),
        [pl.BlockSpec((N_DEV, tr, cw), lambda i: (0, i, 0))],
        [pl.BlockSpec((3, tr, C), lambda i: (0, i, 0)), _rows(tr, WQ), _rows(tr, 2 * D)],
        [_sds((3, D, C), BF16), _sds((D, WQ), BF16), _sds((D, 2 * D), BF16)],
    )(win_w)


def _win_merge(d_bcv, d_qkr, d_gg, cw):
    _, D, C = d_bcv.shape
    WQ = d_qkr.shape[1]
    QL_KVL = WQ - LANES
    o1 = 3 * C
    segments = [("bcv", k, k * C, (k + 1) * C) for k in range(3)]
    segments += [("qkr", None, o1, o1 + QL_KVL + ROPE), ("gg", None, o1 + QL_KVL + ROPE, o1 + QL_KVL + ROPE + 2 * D)]
    pieces = _win_pieces(segments, cw)
    tr = _tile(D, 256, BF16_ROWS)

    def body(bcv_ref, qkr_ref, gg_ref, o_ref):
        src = dict(bcv=bcv_ref, qkr=qkr_ref, gg=gg_ref)
        for name, lead, d, (s0, s1), (t0, t1) in pieces:
            v = src[name][:, t0:t1] if lead is None else src[name][lead, :, t0:t1]
            o_ref[d, :, s0:s1] = v.astype(BF16)

    return _call(
        body, "win_merge", (D // tr,),
        [pl.BlockSpec((3, tr, C), lambda i: (0, i, 0)), _rows(tr, WQ), _rows(tr, 2 * D)],
        pl.BlockSpec((N_DEV, tr, cw), lambda i: (0, i, 0)),
        _sds((N_DEV, D, cw), BF16),
    )(d_bcv, d_qkr, d_gg)


def _mix_in(h, gain, w_bcv, w_qkr, w_gg):
    S, D = h.shape
    C = w_bcv.shape[2]
    tm = _tile(S, 512, SUBLANES)

    def body(h_ref, gain_ref, w1, w2, w3, o1, o2, o3, n_ref, r_ref):
        n32, rstd = _rms_fwd(h_ref[...], gain_ref[...])
        n = n32.astype(BF16)
        n_ref[...] = n
        r_ref[...] = rstd
        for k in range(3):
            o1[k] = _dot(n, w1[k]).astype(BF16)
        o2[...] = _dot(n, w2[...])
        o3[...] = _dot(n, w3[...]).astype(BF16)

    return _call(
        body, "mix_in", (S // tm,),
        [_rows(tm, D), _whole(gain), _whole(w_bcv), _whole(w_qkr), _whole(w_gg)],
        [pl.BlockSpec((3, tm, C), lambda i: (0, i, 0)), _rows(tm, w_qkr.shape[1]), _rows(tm, 2 * D),
         _rows(tm, D), _rows(tm, 1)],
        [_sds((3, S, C), BF16), _sds((S, w_qkr.shape[1]), F32), _sds((S, 2 * D), BF16), _sds((S, D), BF16),
         _sds((S, 1), F32)],
    )(h, gain, w_bcv, w_qkr, w_gg)


def _conv_taps(zc):
    rows = lax.broadcasted_iota(jnp.int32, zc.shape, 0)
    z1 = jnp.where(rows >= 1, pltpu.roll(zc, 1, 0), 0.0)
    z2 = jnp.where(rows >= 2, pltpu.roll(zc, 2, 0), 0.0)
    return z1, z2


def _conv_fwd(z_bcv, conv_w):
    _, S, C = z_bcv.shape

    def body(z_ref, w_ref, o_ref):
        w = w_ref[...]
        zc = z_ref[1].astype(F32) * z_ref[2].astype(F32)
        z1, z2 = _conv_taps(zc)
        y = w[0:1] * z2 + w[1:2] * z1 + w[2:3] * zc
        o_ref[...] = (z_ref[0].astype(F32) * y).astype(BF16)

    return _call(
        body, "conv_fwd", (C // LANES,),
        [pl.BlockSpec((3, S, LANES), lambda j: (0, 0, j)), pl.BlockSpec((3, LANES), lambda j: (0, j))],
        pl.BlockSpec((S, LANES), lambda j: (0, j)),
        _sds((S, C), BF16),
    )(z_bcv, conv_w)


def _rope(x, cs, half):
    c, s1, s2 = cs[:, :LANES], cs[:, LANES:2 * LANES], cs[:, 2 * LANES:]
    return x * c + pltpu.roll(x, LANES - half, 1) * s1 + pltpu.roll(x, half, 1) * s2


def _unrope(d, cs, half):
    c, s1, s2 = cs[:, :LANES], cs[:, LANES:2 * LANES], cs[:, 2 * LANES:]
    return d * c + pltpu.roll(d * s1, half, 1) + pltpu.roll(d * s2, LANES - half, 1)


def _mla_prep(z_qkr, gq, gkv, cs, c256_w):
    S = z_qkr.shape[0]
    QL, KVL = gq.shape[1], gkv.shape[1]
    H = N_DEV
    tm = _tile(S, 512, SUBLANES)
    half = ROPE // 2

    def body(z_ref, gq_ref, gkv_ref, cs_ref, w_ref, q_ref, k_ref, v_ref, qn_ref, kvn_ref, rq_ref, rkv_ref):
        z = z_ref[...]
        cs_t = cs_ref[...]
        qn32, rq = _rms_fwd(z[:, :QL], gq_ref[...])
        kvn32, rkv = _rms_fwd(z[:, QL:QL + KVL], gkv_ref[...])
        qn = qn32.astype(BF16)
        kvn = kvn32.astype(BF16)
        qn_ref[...] = qn
        kvn_ref[...] = kvn
        rq_ref[...] = rq
        rkv_ref[...] = rkv
        krope = _rope(z[:, QL + KVL:], cs_t, half).astype(BF16)
        for h in range(H):
            lo, mid, hi = h * HEAD_SLOT, h * HEAD_SLOT + LANES, (h + 1) * HEAD_SLOT
            q = _dot(qn, w_ref[h, KVL:KVL + QL, :])
            kv = _dot(kvn, w_ref[h, 0:KVL, :])
            q_ref[:, lo:mid] = q[:, :LANES].astype(BF16)
            q_ref[:, mid:hi] = _rope(q[:, LANES:], cs_t, half).astype(BF16)
            k_ref[:, lo:mid] = kv[:, :LANES].astype(BF16)
            k_ref[:, mid:hi] = krope
            v_ref[:, h * VDIM:(h + 1) * VDIM] = kv[:, LANES:].astype(BF16)

    return _call(
        body, "mla_prep", (S // tm,),
        [_rows(tm, z_qkr.shape[1]), _whole(gq), _whole(gkv), _rows(tm, 3 * LANES), _whole(c256_w)],
        [_rows(tm, H * HEAD_SLOT), _rows(tm, H * HEAD_SLOT), _rows(tm, H * VDIM), _rows(tm, QL), _rows(tm, KVL),
         _rows(tm, 1), _rows(tm, 1)],
        [_sds((S, H * HEAD_SLOT), BF16), _sds((S, H * HEAD_SLOT), BF16), _sds((S, H * VDIM), BF16),
         _sds((S, QL), BF16), _sds((S, KVL), BF16), _sds((S, 1), F32), _sds((S, 1), F32)],
    )(z_qkr, gq, gkv, cs, c256_w)


def _chunk_mask(t):
    shift = CHUNK.bit_length() - 1
    krow = lax.broadcasted_iota(jnp.int32, (t, t), 0) >> shift
    qcol = lax.broadcasted_iota(jnp.int32, (t, t), 1) >> shift
    return krow <= qcol


def _attn_fwd(q, k, v, H):
    S = q.shape[0]
    t = _tile(S, 512, CHUNK)
    nq = S // t

    def body(q_ref, k_ref, v_ref, o_ref, lse_ref, vt_ref):
        qi = pl.program_id(1)

        @pl.when(qi == 0)
        def _():
            vt_ref[...] = v_ref[...].T

        qv = q_ref[...]

        def block(start, width, carry, masked):
            m, l, acc = carry
            off = pl.multiple_of(start * t, t)
            s = _dot_nt(k_ref[pl.ds(off, width * t), :], qv) * SCORE_SCALE
            if masked:
                s = jnp.where(_chunk_mask(t), s, -1e30)
            m_new = jnp.maximum(m, jnp.max(s, axis=0, keepdims=True))
            alpha = jnp.exp2(m - m_new)
            p = jnp.exp2(s - m_new)
            l = alpha * l + jnp.sum(p, axis=0, keepdims=True)
            acc = alpha * acc + _dot(vt_ref[:, pl.ds(off, width * t)], p.astype(BF16))
            return m_new, l, acc

        init = (jnp.full((1, t), -1e30, F32), jnp.zeros((1, t), F32), jnp.zeros((VDIM, t), F32))
        wide = lax.div(qi, ATTN_FWD_WIDTH)
        carry = lax.fori_loop(0, wide, lambda j, c: block(j * ATTN_FWD_WIDTH, ATTN_FWD_WIDTH, c, False), init)
        carry = lax.fori_loop(wide * ATTN_FWD_WIDTH, qi, lambda kj, c: block(kj, 1, c, False), carry)
        m, l, acc = block(qi, 1, carry, True)
        o_ref[...] = (acc / l).T.astype(BF16)
        lse_ref[0] = jnp.broadcast_to(m + jnp.log2(l), (SUBLANES, t))

    return _call(
        body, "attn_fwd", (H, nq),
        [pl.BlockSpec((t, HEAD_SLOT), lambda h, i: (i, h)), pl.BlockSpec((S, HEAD_SLOT), lambda h, i: (0, h)),
         pl.BlockSpec((S, VDIM), lambda h, i: (0, h))],
        [pl.BlockSpec((t, VDIM), lambda h, i: (i, h)), pl.BlockSpec((1, SUBLANES, t), lambda h, i: (h, 0, i))],
        [_sds((S, H * VDIM), BF16), _sds((H, SUBLANES, S), F32)],
        [pltpu.VMEM((VDIM, S), BF16)],
    )(q, k, v)


def _merge_wo(o, by, z_gg, h, sq_w, c128_w):
    S, D = h.shape
    C = by.shape[1]
    r = sq_w.shape[1] // 3
    tm = _tile(S, 512, SUBLANES)

    def body(o_ref, by_ref, gg_ref, h_ref, wmo_ref, wo_ref, wco_ref, h2_ref, mg_ref, yc_ref, ym_ref):
        ymla = _dot(o_ref[...], wmo_ref[...].reshape(N_DEV * r, D))
        yconv = _dot(by_ref[...], _cat_slots(wco_ref))
        gg = gg_ref[...].astype(F32)
        merged = (_sig(gg[:, :D]) * yconv + _sig(gg[:, D:]) * ymla).astype(BF16)
        mg_ref[...] = merged
        yc_ref[...] = yconv.astype(BF16)
        ym_ref[...] = ymla.astype(BF16)
        h2_ref[...] = h_ref[...] + _dot(merged, wo_ref[...].reshape(N_DEV * r, D))

    return _call(
        body, "merge_wo", (S // tm,),
        [_rows(tm, o.shape[1]), _rows(tm, C), _rows(tm, 2 * D), _rows(tm, D), _slab(sq_w, r, 0), _slab(sq_w, r, 1),
         _slab(c128_w, C, 0)],
        [_rows(tm, D)] * 4,
        [_sds((S, D), F32)] + [_sds((S, D), BF16)] * 3,
    )(o, by, z_gg, h, sq_w, sq_w, c128_w)


def _ple_fwd(h, gain, p, sq_w, c128_w, C):
    S, D = h.shape
    P = p.shape[1]
    r = sq_w.shape[1] // 3
    tm = _tile(S, 512, SUBLANES)

    def body(h_ref, gain_ref, p_ref, wpg_ref, wpp_ref, o_ref, pre_ref, pp_ref, n_ref, r_ref):
        x = h_ref[...]
        n32, rstd = _rms_fwd(x, gain_ref[...])
        n = n32.astype(BF16)
        n_ref[...] = n
        r_ref[...] = rstd
        pre = _dot(n, wpg_ref[...].reshape(N_DEV * r, D))
        pp = _dot(p_ref[...].astype(BF16), _cat_slots(wpp_ref))
        pre_ref[...] = pre.astype(BF16)
        pp_ref[...] = pp.astype(BF16)
        o_ref[...] = x + _sig(pre) * pp

    return _call(
        body, "ple_fwd", (S // tm,),
        [_rows(tm, D), _whole(gain), _rows(tm, P), _slab(sq_w, r, 2), _slab(c128_w, P, C // P)],
        [_rows(tm, D), _rows(tm, D), _rows(tm, D), _rows(tm, D), _rows(tm, 1)],
        [_sds((S, D), F32)] + [_sds((S, D), BF16)] * 3 + [_sds((S, 1), F32)],
    )(h, gain, p, sq_w, c128_w)


def _final_loss(h, gain, target):
    S, D = h.shape
    tm = _tile(S, 512, SUBLANES)

    def body(h_ref, gain_ref, t_ref, dh_ref, loss_ref, dg_ref):
        @pl.when(pl.program_id(0) == 0)
        def _():
            loss_ref[...] = jnp.zeros_like(loss_ref)
            dg_ref[...] = jnp.zeros_like(dg_ref)

        x = h_ref[...]
        gain_v = gain_ref[...]
        y, rstd = _rms_fwd(x, gain_v)
        err = y - t_ref[...]
        loss_ref[...] += 0.5 * jnp.sum(jnp.mean(err * err, axis=-1, keepdims=True))
        dx, dgain = _rms_bwd(err * (1.0 / D), x, rstd, gain_v)
        dh_ref[...] = dx
        dg_ref[...] += dgain

    return _call(
        body, "final_loss", (S // tm,),
        [_rows(tm, D), _whole(gain), _rows(tm, D)],
        [_rows(tm, D), pl.BlockSpec((1, LANES), lambda i: (0, 0)), pl.BlockSpec((1, D), lambda i: (0, 0))],
        [_sds((S, D), F32), _sds((1, LANES), F32), _sds((1, D), F32)],
    )(h, gain, target)


def _tn_call(body, name, grid, in_specs, out_spec, out_shape, scratch, operands, prev):
    n = len(operands)
    if prev is None:
        return _call(body, name, grid, in_specs, out_spec, out_shape, scratch)(*operands)
    assert prev.shape == out_shape.shape and prev.dtype == out_shape.dtype

    def wrapped(*refs):
        body(*refs[:n], *refs[n + 1:])

    return _call(wrapped, name, grid, in_specs + [ANY], out_spec, out_shape, scratch, {n: 0})(*operands, prev)


def _transposed(x_ref, xt_ref, first):
    @pl.when(first)
    def _():
        xt_ref[...] = x_ref[...].astype(BF16).T


def _tn_slots(x, dy, prev, rows_total, row_off):
    S, K = x.shape
    B, _, c = dy.shape
    tk = _tile(K, 1024, LANES)

    def body(x_ref, dy_ref, o_ref, xt_ref):
        _transposed(x_ref, xt_ref, pl.program_id(1) == 0)
        o_ref[0] = _dot(xt_ref[...], dy_ref[0]).astype(BF16)

    return _tn_call(
        body, "tn_slots", (K // tk, B),
        [pl.BlockSpec((S, tk), lambda i, b: (0, i)), pl.BlockSpec((1, S, c), lambda i, b: (b, 0, 0))],
        pl.BlockSpec((1, tk, c), lambda i, b: (b, row_off // tk + i, 0)),
        _sds((B, rows_total, c), BF16), [pltpu.VMEM((tk, S), BF16)], [x, dy], prev)


def _tn_plain(x, dy, out_dtype=F32):
    S, K = x.shape
    B, _, c = dy.shape
    tk = _tile(K, 512, LANES)
    tn = _tile(c, 1024, LANES)

    def body(x_ref, dy_ref, o_ref, xt_ref):
        _transposed(x_ref, xt_ref, (pl.program_id(1) == 0) & (pl.program_id(2) == 0))
        o_ref[0] = _dot(xt_ref[...], dy_ref[0]).astype(out_dtype)

    return _call(
        body, "tn_plain", (K // tk, B, c // tn),
        [pl.BlockSpec((S, tk), lambda i, b, j: (0, i)), pl.BlockSpec((1, S, tn), lambda i, b, j: (b, 0, j))],
        pl.BlockSpec((1, tk, tn), lambda i, b, j: (b, i, j)),
        _sds((B, K, c), out_dtype), [pltpu.VMEM((tk, S), BF16)],
    )(x, dy)


def _tn_down(at, dh, prev, rows_total, which):
    nb, c, S = at.shape
    D = dh.shape[1]
    r = c // 2
    tn = _tile(D, 512, LANES)

    def body(at_ref, dh_ref, o_ref):
        g = 0.5 * _dot(at_ref[0], dh_ref[...].astype(BF16))
        o_ref[...] = g.astype(BF16).reshape(2, r, tn)

    return _tn_call(
        body, "tn_down", (nb, D // tn),
        [pl.BlockSpec((1, c, S), lambda i, j: (i, 0, 0)), pl.BlockSpec((S, tn), lambda i, j: (0, j))],
        pl.BlockSpec((2, r, tn), lambda i, j: (i, which, j)),
        _sds((N_DEV, rows_total, D), BF16), [], [at, dh], prev)


def _tn_square(x, dy, prev, rows_total, member):
    S, K = x.shape
    N = dy.shape[1]
    r = K // N_DEV
    tk = _tile(K, 512, r)
    tn = _tile(N, 512, LANES)

    def body(x_ref, dy_ref, o_ref, xt_ref):
        _transposed(x_ref, xt_ref, pl.program_id(1) == 0)
        g = _dot(xt_ref[...], dy_ref[...].astype(BF16))
        o_ref[...] = g.astype(BF16).reshape(tk // r, r, tn)

    return _tn_call(
        body, "tn_square", (K // tk, N // tn),
        [pl.BlockSpec((S, tk), lambda i, j: (0, i)), pl.BlockSpec((S, tn), lambda i, j: (0, j))],
        pl.BlockSpec((tk // r, r, tn), lambda i, j: (i, member, j)),
        _sds((N_DEV, rows_total, N), BF16), [pltpu.VMEM((tk, S), BF16)], [x, dy], prev)


def _tn_cols(x, dy, prev, rows_total, row_block):
    S, K = x.shape
    N = dy.shape[1]
    cw = N // N_DEV

    def body(x_ref, dy_ref, o_ref):
        g = _dot(x_ref[...].astype(BF16).T, dy_ref[...])
        for d in range(N_DEV):
            o_ref[d] = g[:, d * cw:(d + 1) * cw].astype(BF16)

    return _tn_call(
        body, "tn_cols", (1,),
        [pl.BlockSpec((S, K), lambda i: (0, 0)), pl.BlockSpec((S, N), lambda i: (0, 0))],
        pl.BlockSpec((N_DEV, K, cw), lambda i: (0, row_block, 0)),
        _sds((N_DEV, rows_total, cw), BF16), [], [x, dy], prev)


def _tn_heads(qn, kvn, dqp, dkv):
    S, QL = qn.shape
    KVL = kvn.shape[1]

    def body(qn_ref, kvn_ref, dq_ref, dkv_ref, o_ref):
        o_ref[0, 0:KVL, :] = _dot(kvn_ref[...].T, dkv_ref[...]).astype(BF16)
        o_ref[0, KVL:KVL + QL, :] = _dot(qn_ref[...].T, dq_ref[...]).astype(BF16)

    head = pl.BlockSpec((S, HEAD_SLOT), lambda h: (0, h))
    return _call(
        body, "tn_heads", (N_DEV,),
        [pl.BlockSpec((S, QL), lambda h: (0, 0)), pl.BlockSpec((S, KVL), lambda h: (0, 0)), head, head],
        pl.BlockSpec((1, KVL + QL, HEAD_SLOT), lambda h: (h, 0, 0)),
        _sds((N_DEV, KVL + QL, HEAD_SLOT), BF16),
    )(qn, kvn, dqp, dkv)


def _ple_bwd(dh, pre, pp, h, rstd, gain, sq_w, after):
    S, D = h.shape
    r = sq_w.shape[1] // 3
    tm = _tile(S, 512, SUBLANES)

    def body(dh_ref, pre_ref, pp_ref, h_ref, r_ref, gain_ref, wpg_ref, *rest):
        o_ref, dpre_ref, dpp_ref, dg_ref = rest[len(after):]

        @pl.when(pl.program_id(0) == 0)
        def _():
            dg_ref[...] = jnp.zeros_like(dg_ref)

        d = dh_ref[...]
        gate = _sig(pre_ref[...].astype(F32))
        dpre = (d * pp_ref[...].astype(F32) * gate * (1.0 - gate)).astype(BF16)
        dpre_ref[...] = dpre
        dpp_ref[...] = (d * gate).astype(BF16)
        dn = _dot_nt(dpre, wpg_ref[...].reshape(N_DEV * r, D))
        dx, dgain = _rms_bwd(dn, h_ref[...], r_ref[...], gain_ref[...])
        o_ref[...] = d + dx
        dg_ref[...] += dgain

    return _call(
        body, "ple_bwd", (S // tm,),
        [_rows(tm, D), _rows(tm, D), _rows(tm, D), _rows(tm, D), _rows(tm, 1), _whole(gain), _slab(sq_w, r, 2)]
        + [ANY] * len(after),
        [_rows(tm, D), _rows(tm, D), _rows(tm, D), pl.BlockSpec((1, D), lambda i: (0, 0))],
        [_sds((S, D), F32), _sds((S, D), BF16), _sds((S, D), BF16), _sds((1, D), F32)],
    )(dh, pre, pp, h, rstd, gain, sq_w, *after)


def _ffn_bwd_act(dh, dn_w, which, jac, after=()):
    S, D = dh.shape
    _, _, c = jac.shape
    nb = N_DEV // 2
    tm = _tile(S, 512, SUBLANES)

    def body(dh_ref, w_ref, jac_ref, *rest):
        dgu_ref = rest[len(after)]
        dhb = dh_ref[...].astype(BF16)
        for d in range(nb):
            da = _dot_nt(dhb, _down_weight(w_ref, d, c))
            dgu_ref[d] = (da * jac_ref[d].astype(F32)).astype(BF16)
            dgu_ref[nb + d] = (da * jac_ref[nb + d].astype(F32)).astype(BF16)

    act = pl.BlockSpec((N_DEV, tm, c), lambda i: (0, i, 0))
    return _call(
        body, "ffn_bwd_act", (S // tm,),
        [_rows(tm, D), _slab(dn_w, c // 2, which), act] + [ANY] * len(after),
        act,
        _sds((N_DEV, S, c), BF16),
    )(dh, dn_w, jac, *after)


def _ffn_bwd_in(dgu, gu_w, which, h, rstd, gain, dh):
    S, D = h.shape
    c = dgu.shape[2]
    tm = _tile(S, 512, SUBLANES)

    def body(dgu_ref, w_ref, h_ref, r_ref, gain_ref, dh_ref, o_ref, dgain_ref):
        @pl.when(pl.program_id(0) == 0)
        def _():
            dgain_ref[...] = jnp.zeros_like(dgain_ref)

        dn = _dot_nt(dgu_ref[0], w_ref[0])
        for d in range(1, N_DEV):
            dn = dn + _dot_nt(dgu_ref[d], w_ref[d])
        dx, dgain = _rms_bwd(dn, h_ref[...], r_ref[...], gain_ref[...])
        o_ref[...] = dh_ref[...] + dx
        dgain_ref[...] += dgain

    return _call(
        body, "ffn_bwd_in", (S // tm,),
        [pl.BlockSpec((N_DEV, tm, c), lambda i: (0, i, 0)), _slab(gu_w, D, which), _rows(tm, D), _rows(tm, 1),
         _whole(gain), _rows(tm, D)],
        [_rows(tm, D), pl.BlockSpec((1, D), lambda i: (0, 0))],
        [_sds((S, D), F32), _sds((1, D), F32)],
    )(dgu, gu_w, h, rstd, gain, dh)


def _merge_bwd(dh, z_gg, yconv, ymla, o, sq_w, c128_w, C, after):
    S, D = dh.shape
    r = sq_w.shape[1] // 3
    HV = N_DEV * r
    H = HV // VDIM
    tm = _tile(S, 512, SUBLANES)

    def head_rows():
        row = lax.broadcasted_iota(jnp.int32, (SUBLANES * H, HV), 0) >> (SUBLANES.bit_length() - 1)
        col = lax.broadcasted_iota(jnp.int32, (SUBLANES * H, HV), 1) >> (VDIM.bit_length() - 1)
        return jnp.where(row == col, 1.0, 0.0).astype(BF16)

    def body(dh_ref, gg_ref, yc_ref, ym_ref, o_ref, wmo_ref, wo_ref, wco_ref, *rest):
        dgg_ref, dby_ref, do_ref, dyc_ref, dym_ref, dl_ref = rest[len(after):]
        dm = _dot_nt(dh_ref[...].astype(BF16), wo_ref[...].reshape(HV, D))
        gg = gg_ref[...].astype(F32)
        sgc = _sig(gg[:, :D])
        sgm = _sig(gg[:, D:])
        dyc = (dm * sgc).astype(BF16)
        dym = (dm * sgm).astype(BF16)
        dyc_ref[...] = dyc
        dym_ref[...] = dym
        dgg_ref[:, :D] = (dm * yc_ref[...].astype(F32) * sgc * (1.0 - sgc)).astype(BF16)
        dgg_ref[:, D:] = (dm * ym_ref[...].astype(F32) * sgm * (1.0 - sgm)).astype(BF16)
        dby_ref[...] = _dot_nt(dyc, _cat_slots(wco_ref)).astype(BF16)
        do = _dot_nt(dym, wmo_ref[...].reshape(HV, D)).astype(BF16)
        do_ref[...] = do
        prod = do.astype(F32) * o_ref[...].astype(F32)
        hi = prod.astype(BF16)
        lo = (prod - hi.astype(F32)).astype(BF16)
        pick = head_rows()
        dl_ref[...] = _dot_nt(pick, hi) + _dot_nt(pick, lo)

    return _call(
        body, "merge_bwd", (S // tm,),
        [_rows(tm, D), _rows(tm, 2 * D), _rows(tm, D), _rows(tm, D), _rows(tm, HV), _slab(sq_w, r, 0),
         _slab(sq_w, r, 1), _slab(c128_w, C, 0)] + [ANY] * len(after),
        [_rows(tm, 2 * D), _rows(tm, C), _rows(tm, HV), _rows(tm, D), _rows(tm, D),
         pl.BlockSpec((SUBLANES * H, tm), lambda i: (0, i))],
        [_sds((S, 2 * D), BF16), _sds((S, C), BF16), _sds((S, HV), BF16), _sds((S, D), BF16), _sds((S, D), BF16),
         _sds((SUBLANES * H, S), F32)],
    )(dh, z_gg, yconv, ymla, o, sq_w, sq_w, c128_w, *after)


def _conv_bwd(z_bcv, conv_w, dby):
    _, S, C = z_bcv.shape

    def body(z_ref, w_ref, dby_ref, dz_ref, dw_ref):
        w = w_ref[...]
        c = z_ref[1].astype(F32)
        v = z_ref[2].astype(F32)
        d = dby_ref[...].astype(F32)
        zc = c * v
        z1, z2 = _conv_taps(zc)
        y = w[0:1] * z2 + w[1:2] * z1 + w[2:3] * zc
        dz_ref[0] = (d * y).astype(BF16)
        dy = d * z_ref[0].astype(F32)
        rows = lax.broadcasted_iota(jnp.int32, dy.shape, 0)
        dy1 = jnp.where(rows < S - 1, pltpu.roll(dy, S - 1, 0), 0.0)
        dy2 = jnp.where(rows < S - 2, pltpu.roll(dy, S - 2, 0), 0.0)
        dzc = w[2:3] * dy + w[1:2] * dy1 + w[0:1] * dy2
        dz_ref[1] = (dzc * v).astype(BF16)
        dz_ref[2] = (dzc * c).astype(BF16)
        dw_ref[0:1, :] = jnp.sum(dy * z2, axis=0, keepdims=True)
        dw_ref[1:2, :] = jnp.sum(dy * z1, axis=0, keepdims=True)
        dw_ref[2:3, :] = jnp.sum(dy * zc, axis=0, keepdims=True)

    three = pl.BlockSpec((3, S, LANES), lambda j: (0, 0, j))
    wspec = pl.BlockSpec((3, LANES), lambda j: (0, j))
    return _call(
        body, "conv_bwd", (C // LANES,),
        [three, wspec, pl.BlockSpec((S, LANES), lambda j: (0, j))],
        [three, wspec],
        [_sds((3, S, C), BF16), _sds((3, C), F32)],
    )(z_bcv, conv_w, dby)


def _attn_bwd(q, k, v, do, lse, delta, H):
    S = q.shape[0]
    t = _tile(S, 512, CHUNK)
    nk = S // t

    def body(q_ref, k_ref, v_ref, do_ref, lse_ref, dl_ref, dq_ref, dk_ref, dv_ref, dqt_ref):
        kj = pl.program_id(1)

        @pl.when(kj == 0)
        def _():
            dqt_ref[...] = jnp.zeros_like(dqt_ref)

        kv = k_ref[...]
        vv = v_ref[...]
        kt = kv.T

        def block(start, width, carry, masked):
            dk, dv = carry
            off = pl.multiple_of(start * t, t)
            qv = q_ref[pl.ds(off, width * t), :]
            dov = do_ref[pl.ds(off, width * t), :]
            s = _dot_nt(kv, qv) * SCORE_SCALE
            if masked:
                s = jnp.where(_chunk_mask(t), s, -1e30)
            p = jnp.exp2(s - lse_ref[0, 0:1, pl.ds(off, width * t)])
            dp = _dot_nt(vv, dov)
            ds = (p * (dp - dl_ref[0, 0:1, pl.ds(off, width * t)]) * ATTN_SCALE).astype(BF16)
            dqt_ref[:, pl.ds(off, width * t)] += _dot(kt, ds)
            return dk + _dot(ds, qv), dv + _dot(p.astype(BF16), dov)

        init = (jnp.zeros((t, HEAD_SLOT), F32), jnp.zeros((t, VDIM), F32))
        carry = block(kj, 1, init, True)
        wide = lax.div(nk - 1 - kj, ATTN_BWD_WIDTH)
        carry = lax.fori_loop(
            0, wide, lambda j, c: block(kj + 1 + j * ATTN_BWD_WIDTH, ATTN_BWD_WIDTH, c, False), carry)
        dk, dv = lax.fori_loop(kj + 1 + wide * ATTN_BWD_WIDTH, nk, lambda qi, c: block(qi, 1, c, False), carry)
        dk_ref[...] = dk.astype(BF16)
        dv_ref[...] = dv.astype(BF16)

        @pl.when(kj == nk - 1)
        def _():
            dq_ref[...] = dqt_ref[...].T.astype(BF16)

    kspec = lambda w: pl.BlockSpec((t, w), lambda h, j: (j, h))
    qspec = lambda w: pl.BlockSpec((S, w), lambda h, j: (0, h))
    stat = pl.BlockSpec((1, SUBLANES, S), lambda h, j: (h, 0, 0))
    return _call(
        body, "attn_bwd", (H, nk),
        [qspec(HEAD_SLOT), kspec(HEAD_SLOT), kspec(VDIM), qspec(VDIM), stat, stat],
        [qspec(HEAD_SLOT), kspec(HEAD_SLOT), kspec(VDIM)],
        [_sds((S, H * HEAD_SLOT), BF16), _sds((S, H * HEAD_SLOT), BF16), _sds((S, H * VDIM), BF16)],
        [pltpu.VMEM((HEAD_SLOT, S), F32)],
    )(q, k, v, do, lse, delta)


def _mla_prep_bwd(dq, dk, dv, z_qkr, rq, rkv, gq, gkv, cs, c256_w):
    S = z_qkr.shape[0]
    QL, KVL = gq.shape[1], gkv.shape[1]
    H = N_DEV
    tm = _tile(S, 512, SUBLANES)
    half = ROPE // 2

    def body(dq_ref, dk_ref, dv_ref, z_ref, rq_ref, rkv_ref, gq_ref, gkv_ref, cs_ref, w_ref,
             dz_ref, dqp_ref, dkv_ref, dgq_ref, dgkv_ref):
        @pl.when(pl.program_id(0) == 0)
        def _():
            dgq_ref[...] = jnp.zeros_like(dgq_ref)
            dgkv_ref[...] = jnp.zeros_like(dgkv_ref)

        cs_t = cs_ref[...]
        dkr = jnp.zeros((tm, LANES), F32)
        dqn = jnp.zeros((tm, QL), F32)
        dkvn = jnp.zeros((tm, KVL), F32)
        for h in range(H):
            lo, mid, hi = h * HEAD_SLOT, h * HEAD_SLOT + LANES, (h + 1) * HEAD_SLOT
            dqp_ref[:, lo:mid] = dq_ref[:, lo:mid]
            dqp_ref[:, mid:hi] = _unrope(dq_ref[:, mid:hi].astype(F32), cs_t, half).astype(BF16)
            dkv_ref[:, lo:mid] = dk_ref[:, lo:mid]
            dkv_ref[:, mid:hi] = dv_ref[:, h * VDIM:(h + 1) * VDIM]
            dkr = dkr + dk_ref[:, mid:hi].astype(F32)
            dqn = dqn + _dot_nt(dqp_ref[:, lo:hi], w_ref[h, KVL:KVL + QL, :])
            dkvn = dkvn + _dot_nt(dkv_ref[:, lo:hi], w_ref[h, 0:KVL, :])
        z = z_ref[...]
        dqc, dgq = _rms_bwd(dqn, z[:, :QL], rq_ref[...], gq_ref[...])
        dkvc, dgkv = _rms_bwd(dkvn, z[:, QL:QL + KVL], rkv_ref[...], gkv_ref[...])
        dz_ref[:, :QL] = dqc.astype(BF16)
        dz_ref[:, QL:QL + KVL] = dkvc.astype(BF16)
        dz_ref[:, QL + KVL:] = _unrope(dkr, cs_t, half).astype(BF16)
        dgq_ref[...] += dgq
        dgkv_ref[...] += dgkv

    W = z_qkr.shape[1]
    return _call(
        body, "mla_prep_bwd", (S // tm,),
        [_rows(tm, H * HEAD_SLOT), _rows(tm, H * HEAD_SLOT), _rows(tm, H * VDIM), _rows(tm, W), _rows(tm, 1),
         _rows(tm, 1), _whole(gq), _whole(gkv), _rows(tm, 3 * LANES), _whole(c256_w)],
        [_rows(tm, W), _rows(tm, H * HEAD_SLOT), _rows(tm, H * HEAD_SLOT), _whole(gq), _whole(gkv)],
        [_sds((S, W), BF16), _sds((S, H * HEAD_SLOT), BF16), _sds((S, H * HEAD_SLOT), BF16),
         _sds((1, QL), F32), _sds((1, KVL), F32)],
    )(dq, dk, dv, z_qkr, rq, rkv, gq, gkv, cs, c256_w)


def _mix_in_bwd(d_bcv, dz_qkr, dgg, w_bcv, w_qkr, w_gg, h, rstd, gain, dh):
    S, D = h.shape
    C = d_bcv.shape[2]
    tm = _tile(S, 512, SUBLANES)

    def body(db_ref, dq_ref, dgg_ref, wb_ref, wq_ref, wg_ref, h_ref, r_ref, gain_ref, dh_ref, o_ref, dgain_ref):
        @pl.when(pl.program_id(0) == 0)
        def _():
            dgain_ref[...] = jnp.zeros_like(dgain_ref)

        dn = _dot_nt(dq_ref[...], wq_ref[...]) + _dot_nt(dgg_ref[...], wg_ref[...])
        for k in range(3):
            dn = dn + _dot_nt(db_ref[k], wb_ref[k])
        dx, dgain = _rms_bwd(dn, h_ref[...], r_ref[...], gain_ref[...])
        o_ref[...] = dh_ref[...] + dx
        dgain_ref[...] += dgain

    return _call(
        body, "mix_in_bwd", (S // tm,),
        [pl.BlockSpec((3, tm, C), lambda i: (0, i, 0)), _rows(tm, dz_qkr.shape[1]), _rows(tm, dgg.shape[1]),
         _whole(w_bcv), _whole(w_qkr), _whole(w_gg), _rows(tm, D), _rows(tm, 1), _whole(gain), _rows(tm, D)],
        [_rows(tm, D), pl.BlockSpec((1, D), lambda i: (0, 0))],
        [_sds((S, D), F32), _sds((1, D), F32)],
    )(d_bcv, dz_qkr, dgg, w_bcv, w_qkr, w_gg, h, rstd, gain, dh)


def _rope_tables(positions):
    half = ROPE // 2
    inv_freq = ROPE_THETA ** (-jnp.arange(0, ROPE, 2, dtype=F32) / ROPE)
    ang = positions.astype(F32)[:, None] * inv_freq
    cos, sin = jnp.cos(ang), jnp.sin(ang)
    z = jnp.zeros_like(cos)
    pad = jnp.zeros((positions.shape[0], LANES - 2 * half), F32)
    return jnp.concatenate([cos, cos, pad, -sin, z, pad, z, sin, pad], axis=1)


def _grad_rows(w):
    return dict(gu=2 * w["gu1"].shape[1], dn=2 * w["dn1"].shape[1], sq=w["sq"].shape[1], win=w["win"].shape[1],
                c128=w["c128"].shape[1], c256=w["c256"].shape[1])


def _layer_fwd(h0, p_l, cs, w, sm, late):
    C = sm["conv_w"].shape[1]
    QL, KVL = sm["q_norm"].shape[1], sm["kv_norm"].shape[1]
    jac1, a1, at1, n1, r1 = _ffn_up(h0, sm["ffn1_norm"], w["gu1"], 0)
    h1 = _ffn_down(a1, w["dn1"], 0, h0)
    if late is not None:
        w.update(late(h1))
    w_bcv, w_qkr, w_gg = _win_split(w["win"], C, QL, KVL)
    z_bcv, z_qkr, z_gg, un, rm = _mix_in(h1, sm["mix_norm"], w_bcv, w_qkr, w_gg)
    by = _conv_fwd(z_bcv, sm["conv_w"])
    q, k, v, qn, kvn, rq, rkv = _mla_prep(z_qkr, sm["q_norm"], sm["kv_norm"], cs, w["c256"])
    o, lse = _attn_fwd(q, k, v, N_DEV)
    h2, merged, yconv, ymla = _merge_wo(o, by, z_gg, h1, w["sq"], w["c128"])
    jac2, a2, at2, n2, r2 = _ffn_up(h2, sm["ffn2_norm"], w["gu2"], 0)
    h3 = _ffn_down(a2, w["dn2"], 0, h2)
    h4, pre, pp, pn, rp = _ple_fwd(h3, sm["ple_norm"], p_l, w["sq"], w["c128"], C)
    saved = dict(h0=h0, jac1=jac1, at1=at1, n1=n1, r1=r1, h1=h1, w_bcv=w_bcv, w_qkr=w_qkr, w_gg=w_gg, z_bcv=z_bcv,
                 z_qkr=z_qkr, z_gg=z_gg, un=un, rm=rm, by=by, q=q, k=k, v=v, qn=qn, kvn=kvn, rq=rq, rkv=rkv, o=o,
                 lse=lse, h2=h2, merged=merged, yconv=yconv, ymla=ymla, jac2=jac2, at2=at2, n2=n2, r2=r2, h3=h3,
                 pre=pre, pp=pp, pn=pn, rp=rp, p=p_l)
    return h4, saved


def _layer_bwd_late(dh4, s, w, sm, after):
    D = dh4.shape[1]
    C = sm["conv_w"].shape[1]
    P = s["p"].shape[1]
    rows = _grad_rows(w)
    small = {}
    dh3, dpre, dpp, small["ple_norm"] = _ple_bwd(dh4, s["pre"], s["pp"], s["h3"], s["rp"], sm["ple_norm"], w["sq"],
                                                 after)
    g_sq = _tn_square(s["pn"], dpre, None, rows["sq"], 2)
    g_c128 = _tn_cols(s["p"], dpp, None, rows["c128"], C // P)

    dgu2 = _ffn_bwd_act(dh3, w["dn2"], 0, s["jac2"])
    g_dn = _tn_down(s["at2"], dh3, None, rows["dn"], 1)
    g_gu = _tn_slots(s["n2"], dgu2, None, rows["gu"], D)
    dh2, small["ffn2_norm"] = _ffn_bwd_in(dgu2, w["gu2"], 0, s["h2"], s["r2"], sm["ffn2_norm"], dh3)
    return dh2, dict(gu=g_gu, dn=g_dn, sq=g_sq, c128=g_c128), small


def _layer_bwd_mixer(dh2, part, small, s, cs, w, sm, after):
    C = sm["conv_w"].shape[1]
    rows = _grad_rows(w)
    g_gu, g_dn, g_sq, g_c128 = part["gu"], part["dn"], part["sq"], part["c128"]

    dgg, dby, do, dyc, dym, delta = _merge_bwd(dh2, s["z_gg"], s["yconv"], s["ymla"], s["o"], w["sq"], w["c128"], C,
                                               after)
    g_sq = _tn_square(s["merged"], dh2, g_sq, rows["sq"], 1)
    g_sq = _tn_square(s["o"], dym, g_sq, rows["sq"], 0)
    g_c128 = _tn_cols(s["by"], dyc, g_c128, rows["c128"], 0)
    d_bcv, small["conv_w"] = _conv_bwd(s["z_bcv"], sm["conv_w"], dby)
    delta = delta.reshape(N_DEV, SUBLANES, delta.shape[1])
    dq, dk, dv = _attn_bwd(s["q"], s["k"], s["v"], do, s["lse"], delta, N_DEV)
    dz_qkr, dqp, dkv, small["q_norm"], small["kv_norm"] = _mla_prep_bwd(
        dq, dk, dv, s["z_qkr"], s["rq"], s["rkv"], sm["q_norm"], sm["kv_norm"], cs, w["c256"])
    g_c256 = _tn_heads(s["qn"], s["kvn"], dqp, dkv)
    un = s["un"]
    g_win = _win_merge(_tn_plain(un, d_bcv), _tn_plain(un, dz_qkr[None])[0], _tn_plain(un, dgg[None])[0],
                       w["win"].shape[2])
    dh1, small["mix_norm"] = _mix_in_bwd(d_bcv, dz_qkr, dgg, s["w_bcv"], s["w_qkr"], s["w_gg"], s["h1"], s["rm"],
                                         sm["mix_norm"], dh2)
    return dh1, dict(gu=g_gu, dn=g_dn, sq=g_sq, win=g_win, c128=g_c128, c256=g_c256), small


def _layer_bwd_first(dh1, part, small, s, w, sm, after):
    rows = _grad_rows(w)
    dgu1 = _ffn_bwd_act(dh1, w["dn1"], 0, s["jac1"], after)
    g_dn = _tn_down(s["at1"], dh1, part["dn"], rows["dn"], 0)
    g_gu = _tn_slots(s["n1"], dgu1, part["gu"], rows["gu"], 0)
    dh0, small["ffn1_norm"] = _ffn_bwd_in(dgu1, w["gu1"], 0, s["h0"], s["r1"], sm["ffn1_norm"], dh1)
    return dh0, dict(part, gu=g_gu, dn=g_dn), small


def _mesh_pos():
    return lax.axis_index("x"), lax.axis_index("y"), lax.axis_index("c")


def _other_chips(x, y):
    return [(1 - x, y), (x, 1 - y), (1 - x, 1 - y)]


def _pack(arrs, flipped, width):
    L = arrs[0].shape[0]
    shapes = [a.shape[:0:-1] if f else a.shape[1:] for a, f in zip(arrs, flipped)]
    R = sum(r for r, _ in shapes)

    def body(*refs):
        o_ref = refs[-1]
        off = 0
        for a_ref, f, (r, c) in zip(refs[:-1], flipped, shapes):
            a = a_ref[0].T if f else a_ref[0]
            o_ref[0, off:off + r, 0:c] = a.astype(BF16)
            if c < width:
                o_ref[0, off:off + r, c:width] = jnp.zeros((r, width - c), BF16)
            off += r

    return _call(
        body, "pack", (L,),
        [pl.BlockSpec((1,) + a.shape[1:], lambda l: (l, 0, 0)) for a in arrs],
        pl.BlockSpec((1, R, width), lambda l: (l, 0, 0)),
        _sds((L, R, width), BF16),
    )(*arrs)


def _handshake(peers):
    barrier = pltpu.get_barrier_semaphore()
    for peer in peers:
        pl.semaphore_signal(barrier, inc=1, device_id=peer, device_id_type=MESH)
    pl.semaphore_wait(barrier, len(peers))


def _sequencer_call(body, name, out_types, sems, collective_id, operands):
    return pl.kernel(
        body, name=name, out_type=out_types,
        mesh=plsc.ScalarSubcoreMesh(axis_name="seq", num_cores=1),
        scratch_types=tuple(pltpu.SemaphoreType.DMA((k,)) for k in sems),
        compiler_params=pltpu.CompilerParams(collective_id=collective_id),
    )(*operands)


def _all_gather(packs, l, after, collective_id):
    n = len(packs)

    def body(*refs):
        ins, outs = refs[:n], refs[n + len(after):2 * n + len(after)]
        send_sems, recv_sems, local_sems = refs[2 * n + len(after):]
        x, y, c = _mesh_pos()
        me, sibling = (x, y, c), (x, y, 1 - c)
        chips = _other_chips(x, y)
        _handshake([sibling] + [(*chip, c) for chip in chips])

        def copy(q, k, block, to, src=None):
            slot = outs[q].at[4 * block[0] + 2 * block[1] + block[2]]
            return pltpu.make_async_remote_copy(
                src_ref=slot if src is None else src, dst_ref=slot,
                send_sem=send_sems.at[7 * q + k], recv_sem=recv_sems.at[7 * q + k], device_id=to, device_id_type=MESH)

        started = []
        for q in range(n):
            src = ins[q].at[l]
            mine = pltpu.make_async_copy(src, outs[q].at[4 * x + 2 * y + c], local_sems.at[q])
            mine.start()
            started.append(mine)
        sends = []
        for q in range(n):
            src = ins[q].at[l]
            sends.append(copy(q, 0, me, sibling, src=src))
            sends += [copy(q, 1 + j, me, (*chip, c), src=src) for j, chip in enumerate(chips)]
        for cp in sends:
            cp.start()
        for q in range(n):
            for j, chip in enumerate(chips):
                copy(q, 1 + j, (*chip, c), me).wait_recv()
                fwd = copy(q, 4 + j, (*chip, c), sibling)
                fwd.start()
                sends.append(fwd)
        for q in range(n):
            copy(q, 0, sibling, me).wait_recv()
            for j, chip in enumerate(chips):
                copy(q, 4 + j, (*chip, 1 - c), me).wait_recv()
        for cp in sends:
            cp.wait_send()
        for mine in started:
            mine.wait()

    return _sequencer_call(
        body, f"all_gather_{collective_id}", [_sds((N_DEV,) + p.shape[1:], p.dtype) for p in packs], (7 * n, 7 * n, n),
        collective_id, list(packs) + list(after))


def _rs_d2d(gs, l, collective_id):
    n = len(gs)

    def body(*refs):
        ins, outs = refs[:n], refs[n:2 * n]
        send_sems, recv_sems = refs[2 * n:]
        x, y, c = _mesh_pos()
        _handshake([(x, y, 1 - c)])
        copies = []
        for q in range(n):
            for j in range(4):
                copies.append(pltpu.make_async_remote_copy(
                    src_ref=ins[q].at[2 * j + (1 - c)], dst_ref=outs[q].at[j], send_sem=send_sems.at[4 * q + j],
                    recv_sem=recv_sems.at[4 * q + j], device_id=(x, y, 1 - c), device_id_type=MESH))
        for cp in copies:
            cp.start()
        for cp in copies:
            cp.wait()

    return _sequencer_call(
        body, f"rs_d2d_{l}", [_sds((4,) + g.shape[1:], g.dtype) for g in gs], (4 * n, 4 * n), collective_id, gs)


def _rs_add_chip(gs, as_, after):
    n = len(gs)
    steps = 4
    tiles = [g.shape[1] // steps for g in gs]

    def chip(k):
        x, y, _ = _mesh_pos()
        return ([(x, y)] + _other_chips(x, y))[k]

    def body(*refs):
        g_refs, a_refs = refs[:4 * n], refs[4 * n:8 * n]
        own_refs, t_refs = refs[8 * n + len(after):9 * n + len(after)], refs[9 * n + len(after):]
        for q in range(n):
            g, a = g_refs[4 * q:4 * q + 4], a_refs[4 * q:4 * q + 4]
            own_refs[q][...] = g[0][0].astype(F32) + a[0][0].astype(F32)
            for k in range(1, 4):
                t_refs[q][k - 1] = (g[k][0].astype(F32) + a[k][0].astype(F32)).astype(BF16)

    def gspec(q, k):
        def index(i):
            px, py = chip(k)
            return 4 * px + 2 * py + lax.axis_index("c"), i, 0
        return pl.BlockSpec((1, tiles[q], gs[q].shape[2]), index)

    def aspec(q, k):
        def index(i):
            px, py = chip(k)
            return 2 * px + py, i, 0
        return pl.BlockSpec((1, tiles[q], gs[q].shape[2]), index)

    in_specs = [gspec(q, k) for q in range(n) for k in range(4)] + [aspec(q, k) for q in range(n) for k in range(4)]
    operands = [g for g in gs for _ in range(4)] + [a for a in as_ for _ in range(4)]
    out_specs = [pl.BlockSpec((tiles[q], gs[q].shape[2]), lambda i: (i, 0)) for q in range(n)]
    out_specs += [pl.BlockSpec((3, tiles[q], gs[q].shape[2]), lambda i: (0, i, 0)) for q in range(n)]
    out_shape = [_sds(g.shape[1:], F32) for g in gs] + [_sds((3,) + g.shape[1:], BF16) for g in gs]
    res = _call(body, "rs_add_chip", (steps,), in_specs + [ANY] * len(after), out_specs, out_shape)(*operands, *after)
    return res[:n], res[n:]


def _rs_ici(ts, l, collective_id):
    n = len(ts)

    def body(*refs):
        ins, outs = refs[:n], refs[n:2 * n]
        send_sems, recv_sems = refs[2 * n:]
        x, y, c = _mesh_pos()
        chips = _other_chips(x, y)
        _handshake([(*chip, c) for chip in chips])
        copies = []
        for q in range(n):
            for k, chip in enumerate(chips):
                copies.append(pltpu.make_async_remote_copy(
                    src_ref=ins[q].at[k], dst_ref=outs[q].at[k], send_sem=send_sems.at[3 * q + k],
                    recv_sem=recv_sems.at[3 * q + k], device_id=(*chip, c), device_id_type=MESH))
        for cp in copies:
            cp.start()
        for cp in copies:
            cp.wait()

    return _sequencer_call(
        body, f"rs_ici_{l}", [_sds(t.shape, t.dtype) for t in ts], (3 * n, 3 * n), collective_id, ts)


def _all_reduce_small(v):
    n, W = v.shape

    def body(v_ref, out_ref, slots, send_sems, recv_sems):
        x, y, c = _mesh_pos()
        me = 4 * x + 2 * y + c
        slots[me] = v_ref[...]
        copies = []
        for k in range(1, N_DEV):
            kx, ky, kc = (k >> 2) & 1, (k >> 1) & 1, k & 1
            peer = (1 - x if kx else x, 1 - y if ky else y, 1 - c if kc else c)
            copies.append(pltpu.make_async_remote_copy(
                src_ref=v_ref, dst_ref=slots.at[me], send_sem=send_sems.at[k - 1], recv_sem=recv_sems.at[k - 1],
                device_id=peer, device_id_type=MESH))
        for cp in copies:
            cp.start()
        for cp in copies:
            cp.wait()
        acc = slots[0]
        for d in range(1, N_DEV):
            acc = acc + slots[d]
        out_ref[...] = acc

    vm = pl.BlockSpec(memory_space=pltpu.VMEM)
    return pl.pallas_call(
        body, name="all_reduce_small",
        out_shape=_sds((n, W), F32),
        in_specs=[vm], out_specs=vm,
        scratch_shapes=[pltpu.VMEM((N_DEV, n, W), F32), pltpu.SemaphoreType.DMA((7,)), pltpu.SemaphoreType.DMA((7,))],
    )(v)


def _adamw_math(w, g, m, v):
    m2 = ADAM_B1 * m + (1.0 - ADAM_B1) * g
    v2 = ADAM_B2 * v + (1.0 - ADAM_B2) * (g * g)
    m_hat = m2 / (1.0 - ADAM_B1 ** ADAM_STEP)
    v_hat = v2 / (1.0 - ADAM_B2 ** ADAM_STEP)
    return -ADAM_LR * (m_hat / (jnp.sqrt(v_hat) + ADAM_EPS) + ADAM_WD * w), m2, v2


def _adamw(w, g, m, v):
    L, r, c = w.shape
    tr = _tile(r, max(SUBLANES, (256 * 1024 // c) // SUBLANES * SUBLANES), SUBLANES)

    def body(w_ref, g_ref, m_ref, v_ref, d_ref, nm_ref, nv_ref):
        d_ref[...], nm_ref[...], nv_ref[...] = _adamw_math(w_ref[...], g_ref[...], m_ref[...], v_ref[...])

    spec = pl.BlockSpec((1, tr, c), lambda l, i: (l, i, 0))
    return _call(body, "adamw", (L, r // tr), [spec] * 4, [spec] * 3, [_sds((L, r, c), F32)] * 3)(w, g, m, v)


def _adamw_reduced(w, m, v, flipped, own, b, row_off, tr, l, prev, after):
    L = w.shape[0]
    c, r = w.shape[1:] if flipped else w.shape[:0:-1]
    W = own.shape[1]
    ob = row_off // tr
    extra = list(prev or ()) + list(after)

    def body(w_ref, m_ref, v_ref, own_ref, b_ref, *rest):
        g_ref, d_ref, nm_ref, nv_ref = rest[len(extra):]
        g = ((own_ref[...] + b_ref[0].astype(F32)) + b_ref[1].astype(F32)) + b_ref[2].astype(F32)
        g = g[:, :c].T if flipped else g[:, :c]
        g_ref[0] = g
        d_ref[0], nm_ref[0], nv_ref[0] = _adamw_math(w_ref[0], g, m_ref[0], v_ref[0])

    spec = pl.BlockSpec((1, c, tr), lambda i: (l, 0, i)) if flipped else pl.BlockSpec((1, tr, c), lambda i: (l, i, 0))
    return _call(
        body, "adamw_reduced", (r // tr,),
        [spec] * 3 + [pl.BlockSpec((tr, W), lambda i: (ob + i, 0)), pl.BlockSpec((3, tr, W), lambda i: (0, ob + i, 0))]
        + [ANY] * len(extra),
        [spec] * 4, [_sds(w.shape, F32)] * 4,
        aliases={5 + k: k for k in range(4)} if prev else None,
    )(w, m, v, own, b, *extra)


_MEMBERS = dict(gu=("ffn1_w_gu", "ffn2_w_gu"), dn=("ffn1_w_down", "ffn2_w_down"),
                sq=("w_mla_out", "w_o", "w_ple_gate"), win=("w_in",), c128=("w_conv_out", "w_ple_proj"),
                c256=("w_ukv", "w_uq"))
_GATHER_MEMBERS = dict(_MEMBERS, gu1=("ffn1_w_gu",), gu2=("ffn2_w_gu",), dn1=("ffn1_w_down",), dn2=("ffn2_w_down",))
GATHER_STAGES = (("gu1", "dn1"), ("win", "c256", "c128", "sq"), ("gu2", "dn2"))
_FLIPPED = ("ffn1_w_gu", "ffn2_w_gu", "w_in", "w_uq")
_SMALL = ("ffn1_norm", "mix_norm", "q_norm", "kv_norm", "ffn2_norm", "ple_norm")
_ORDER = ("ffn1_norm", "ffn1_w_gu", "ffn1_w_down", "mix_norm", "w_in", "conv_w", "w_conv_out", "q_norm", "kv_norm",
          "w_uq", "w_ukv", "w_mla_out", "w_o", "ffn2_norm", "ffn2_w_gu", "ffn2_w_down", "ple_norm", "w_ple_gate",
          "w_ple_proj", "final_norm")


def _class_width(wts, cls):
    return HEAD_SLOT if cls == "c256" else wts[_GATHER_MEMBERS[cls][0]].shape[2]


def _pack_rows(vecs, width):
    flat = jnp.concatenate([a.reshape(-1) for a in vecs])
    n = flat.shape[0]
    rows = -(-n // width)
    rows = -(-rows // SUBLANES) * SUBLANES
    flat = jnp.pad(flat, (0, rows * width - n))
    offs, o = [], 0
    for a in vecs:
        offs.append(o)
        o += a.size
    return flat.reshape(rows, width), offs


def _unpack_rows(packed, vecs, offs):
    flat = packed.reshape(-1)
    return [flat[o:o + a.size].reshape(a.shape) for a, o in zip(vecs, offs)]


def _train(x, p, positions, target, gathered, packs, small_w, final_norm, update):
    cs = _rope_tables(positions)
    L = len(small_w)
    h = x
    saved = []
    def gather(l, names, after, collective_id):
        got = _all_gather([packs[n] for n in names], l, after, collective_id)
        return dict(zip(names, got))

    late = None
    if packs is not None:
        first, mixer, second = GATHER_STAGES
        w0 = gather(0, first, [], 0)
        w0.update(gather(0, mixer, [w0[first[0]]], 1))
        gathered = [w0]
        late = lambda h1: gather(0, second, [h1], 2)
    everything = sum(GATHER_STAGES, ())
    for l in range(L):
        h, s = _layer_fwd(h, p[l], cs, gathered[l], small_w[l], late)
        late = None
        saved.append(s)
        if packs is not None and l + 1 < L:
            gathered.append(gather(l + 1, everything, [s["by"]], 2 + l + 1))
    dh, loss, d_final = _final_loss(h, final_norm, target)
    grads, smalls = [None] * L, [None] * L
    exchanged = None
    landing = None

    def second_stage(after):
        l, gs, as_ = exchanged
        owns, ts = _rs_add_chip(gs, as_, [after])
        return l, owns, _rs_ici(ts, l, 2 * L + 2 + l)

    for l in reversed(range(L)):
        dh, part, small = _layer_bwd_late(dh, saved[l], gathered[l], small_w[l], [])
        pin = []
        if exchanged is not None:
            landing = second_stage(dh)
            pin = [landing[1][0]]
        dh, part, small = _layer_bwd_mixer(dh, part, small, saved[l], cs, gathered[l], small_w[l], pin)
        pin = [update(*landing)] if exchanged is not None else []
        dh, g, smalls[l] = _layer_bwd_first(dh, part, small, saved[l], gathered[l], small_w[l], pin)
        if update is not None:
            gs = [g[cls] for cls in CLASSES]
            exchanged = (l, gs, _rs_d2d(gs, l, L + 2 + l))
        else:
            grads[l] = g
    if update is not None:
        update(*second_stage(dh))
    return loss[0, 0], dh, grads, smalls, d_final


def kernel(x, p, positions, ffn1_norm, ffn1_w_gu, ffn1_w_down, mix_norm, w_in, conv_w, w_conv_out, q_norm, kv_norm, w_uq, w_ukv, w_mla_out, w_o, ffn2_norm, ffn2_w_gu, ffn2_w_down, ple_norm, w_ple_gate, w_ple_proj, final_norm, loss_target, m_ffn1_norm, m_ffn1_w_gu, m_ffn1_w_down, m_mix_norm, m_w_in, m_conv_w, m_w_conv_out, m_q_norm, m_kv_norm, m_w_uq, m_w_ukv, m_w_mla_out, m_w_o, m_ffn2_norm, m_ffn2_w_gu, m_ffn2_w_down, m_ple_norm, m_w_ple_gate, m_w_ple_proj, m_final_norm, v_ffn1_norm, v_ffn1_w_gu, v_ffn1_w_down, v_mix_norm, v_w_in, v_conv_w, v_w_conv_out, v_q_norm, v_kv_norm, v_w_uq, v_w_ukv, v_w_mla_out, v_w_o, v_ffn2_norm, v_ffn2_w_gu, v_ffn2_w_down, v_ple_norm, v_w_ple_gate, v_w_ple_proj, v_final_norm):
    args = dict(locals())
    wts = {n: args[n] for n in _ORDER}
    L = w_in.shape[0]
    dev = 4 * lax.axis_index("x") + 2 * lax.axis_index("y") + lax.axis_index("c")

    view = lambda n, a: jnp.swapaxes(a, 1, 2) if n in _FLIPPED else a
    packs = {cls: _pack([view(n, wts[n]) for n in _GATHER_MEMBERS[cls]], [n in _FLIPPED for n in _GATHER_MEMBERS[cls]],
                        _class_width(wts, cls))
             for stage in GATHER_STAGES for cls in stage}
    cw = conv_w.shape[2]
    conv_full = lax.dynamic_update_slice(jnp.zeros((L, 3, N_DEV * cw), F32), conv_w, (0, 0, dev * cw))
    conv_packed, conv_offs = _pack_rows([conv_full], FLAT_COLS)
    conv_full = _unpack_rows(_all_reduce_small(conv_packed), [conv_full], conv_offs)[0]
    small_w = [dict({n: wts[n][l][None, :] for n in _SMALL}, conv_w=conv_full[l]) for l in range(L)]

    done = {}

    def update(l, owns, bs):
        for q, cls in enumerate(CLASSES):
            off = 0
            rows = [wts[n].shape[1] for n in _MEMBERS[cls]]
            tr = _tile(math.gcd(*rows), 256, BF16_ROWS)
            for n, r in zip(_MEMBERS[cls], rows):
                done[n] = _adamw_reduced(view(n, wts[n]), view(n, args["m_" + n]), view(n, args["v_" + n]),
                                         n in _FLIPPED, owns[q], bs[q], off, tr, l, done.get(n), [])
                off += r
        return done[_MEMBERS[CLASSES[-1]][-1]][0]

    loss_dev, grad_x, _, smalls, d_final = _train(x[0], p[:, 0], positions[0], loss_target[0], None, packs, small_w,
                                                  final_norm[None, :], update)

    small = [jnp.stack([smalls[l][n][0] for l in range(L)]) for n in _SMALL]
    small += [jnp.stack([smalls[l]["conv_w"] for l in range(L)]), d_final[0], loss_dev[None]]
    packed, offs = _pack_rows(small, FLAT_COLS)
    small = _unpack_rows(_all_reduce_small(packed), small, offs)
    grad = dict(zip(_SMALL, small))
    grad["conv_w"] = lax.dynamic_slice(small[len(_SMALL)], (0, 0, dev * cw), (L, 3, cw))
    grad["final_norm"] = small[-2]
    loss = small[-1][0]

    deltas, new_m, new_v = {}, {}, {}
    for n, outs in done.items():
        grad[n], deltas[n], new_m[n], new_v[n] = (view(n, a) for a in outs)
    for n in _SMALL + ("conv_w", "final_norm"):
        w3 = wts[n].reshape((1,) * (3 - wts[n].ndim) + wts[n].shape)
        d, nm, nv = _adamw(w3, grad[n].reshape(w3.shape), args["m_" + n].reshape(w3.shape),
                           args["v_" + n].reshape(w3.shape))
        deltas[n], new_m[n], new_v[n] = (a.reshape(wts[n].shape) for a in (d, nm, nv))
    return (loss, grad_x[None], *[grad[n] for n in _ORDER], *[deltas[n] for n in _ORDER],
            *[new_m[n] for n in _ORDER], *[new_v[n] for n in _ORDER])
```

```python
import math

import jax
import jax.numpy as jnp
from jax import lax
from jax.experimental import pallas as pl
from jax.experimental.pallas import tpu as pltpu
from jax.experimental.pallas import tpu_sc as plsc

F32 = jnp.float32
BF16 = jnp.bfloat16

CHUNK = 64
NOPE = 128
ROPE = 64
VDIM = 128
ROPE_THETA = 10000.0
EPS = 1e-6
ATTN_SCALE = (NOPE + ROPE) ** -0.5
SCORE_SCALE = ATTN_SCALE * math.log2(math.e)
LN2 = math.log(2.0)
ADAM_LR = 0.001
ADAM_B1 = 0.9
ADAM_B2 = 0.999
ADAM_EPS = 1e-08
ADAM_WD = 0.01
ADAM_STEP = 10

LANES = 128
SUBLANES = 8
BF16_ROWS = 16
V7X_VMEM_BYTES = 64 * 1024 * 1024
VMEM_LIMIT = V7X_VMEM_BYTES * 7 // 8
HEAD_SLOT = 2 * LANES
N_DEV = 8
ATTN_FWD_WIDTH = 4
ATTN_BWD_WIDTH = 2
FLAT_COLS = 1024
CLASSES = ("gu", "dn", "sq", "win", "c128", "c256")

NT = (((1,), (1,)), ((), ()))
MESH = pl.DeviceIdType.MESH
ANY = pl.BlockSpec(memory_space=pl.ANY)


def _dot(a, b):
    return jnp.dot(a, b, preferred_element_type=F32)


def _dot_nt(a, b):
    return lax.dot_general(a, b, NT, preferred_element_type=F32)


def _sig(x):
    return 1.0 / (1.0 + jnp.exp(-x))


def _tile(n, pref, unit):
    if n <= pref:
        return n
    t = (pref // unit) * unit
    while t >= unit:
        if n % t == 0:
            return t
        t -= unit
    return n


def _call(body, name, grid, in_specs, out_specs, out_shape, scratch=(), aliases=None):
    return pl.pallas_call(
        body,
        name=name,
        grid=grid,
        in_specs=in_specs,
        out_specs=out_specs,
        out_shape=out_shape,
        scratch_shapes=list(scratch),
        input_output_aliases=aliases or {},
        compiler_params=pltpu.CompilerParams(
            dimension_semantics=("arbitrary",) * len(grid), vmem_limit_bytes=VMEM_LIMIT
        ),
    )


def _sds(shape, dtype):
    return jax.ShapeDtypeStruct(shape, dtype)


def _rms_fwd(x, gain):
    rstd = lax.rsqrt(jnp.mean(x * x, axis=-1, keepdims=True) + EPS)
    return x * rstd * gain, rstd


def _rms_bwd(dn, x, rstd, gain):
    xhat = x * rstd
    dgy = dn * gain
    dx = rstd * (dgy - xhat * jnp.mean(dgy * xhat, axis=-1, keepdims=True))
    return dx, jnp.sum(dn * xhat, axis=0, keepdims=True)


def _rows(tm, w):
    return pl.BlockSpec((tm, w), lambda i: (i, 0))


def _whole(a):
    nd = a.ndim
    return pl.BlockSpec(a.shape, lambda i: (0,) * nd, pipeline_mode=pl.Buffered(1))


def _slab(buf, rows, index):
    return pl.BlockSpec((N_DEV, rows, buf.shape[2]), lambda i: (0, index, 0), pipeline_mode=pl.Buffered(1))


def _cat_slots(w):
    return jnp.concatenate([w[d] for d in range(N_DEV)], axis=1)


def _ffn_up(h, gain, gu_w, which):
    S, D = h.shape
    c = gu_w.shape[2]
    tm = _tile(S, 512, SUBLANES)
    nb = N_DEV // 2

    def body(h_ref, gain_ref, w_ref, jac_ref, a_ref, at_ref, n_ref, r_ref):
        n32, rstd = _rms_fwd(h_ref[...], gain_ref[...])
        n = n32.astype(BF16)
        n_ref[...] = n
        r_ref[...] = rstd
        for d in range(nb):
            g = _dot(n, w_ref[d])
            u = _dot(n, w_ref[nb + d])
            sg = _sig(g)
            silu = g * sg
            a = (silu * u).astype(BF16)
            a_ref[d] = a
            at_ref[d] = a.T
            jac_ref[d] = (0.5 * u * (sg + silu * (1.0 - sg))).astype(BF16)
            jac_ref[nb + d] = (0.5 * silu).astype(BF16)

    return _call(
        body, "ffn_up", (S // tm,),
        [_rows(tm, D), _whole(gain), _slab(gu_w, D, which)],
        [pl.BlockSpec((N_DEV, tm, c), lambda i: (0, i, 0)), pl.BlockSpec((nb, tm, c), lambda i: (0, i, 0)),
         pl.BlockSpec((nb, c, tm), lambda i: (0, 0, i)), _rows(tm, D), _rows(tm, 1)],
        [_sds((N_DEV, S, c), BF16), _sds((nb, S, c), BF16), _sds((nb, c, S), BF16), _sds((S, D), BF16),
         _sds((S, 1), F32)],
    )(h, gain, gu_w)


def _down_weight(w_ref, d, c):
    return w_ref[2 * d:2 * d + 2].reshape(c, w_ref.shape[2])


def _ffn_down(a, dn_w, which, h):
    nb, S, c = a.shape
    D = h.shape[1]
    tm = _tile(S, 1024, SUBLANES)

    def body(a_ref, w_ref, h_ref, o_ref):
        acc = _dot(a_ref[0], _down_weight(w_ref, 0, c))
        for d in range(1, nb):
            acc = acc + _dot(a_ref[d], _down_weight(w_ref, d, c))
        o_ref[...] = h_ref[...] + 0.5 * acc

    return _call(
        body, "ffn_down", (S // tm,),
        [pl.BlockSpec((nb, tm, c), lambda i: (0, i, 0)), _slab(dn_w, c // 2, which), _rows(tm, D)],
        _rows(tm, D),
        _sds((S, D), F32),
    )(a, dn_w, h)


def _win_segments(C, QL, KVL, D):
    o1, o2 = 3 * C, 3 * C + QL + KVL + ROPE
    return [("bcv", k, k * C, (k + 1) * C) for k in range(3)] + [("qkr", None, o1, o2), ("gg", None, o2, o2 + 2 * D)]


def _win_pieces(segments, cw):
    out = []
    for tgt, lead, a, b in segments:
        for d in range(N_DEV):
            lo, hi = max(a, d * cw), min(b, (d + 1) * cw)
            if lo < hi:
                out.append((tgt, lead, d, (lo - d * cw, hi - d * cw), (lo - a, hi - a)))
    return out


def _win_split(win_w, C, QL, KVL):
    _, D, cw = win_w.shape
    WQ = QL + KVL + LANES
    pieces = _win_pieces(_win_segments(C, QL, KVL, D), cw)
    tr = _tile(D, 256, BF16_ROWS)

    def body(w_ref, bcv_ref, qkr_ref, gg_ref):
        tgt = dict(bcv=bcv_ref, qkr=qkr_ref, gg=gg_ref)
        qkr_ref[:, QL + KVL + ROPE:] = jnp.zeros((tr, LANES - ROPE), BF16)
        for name, lead, d, (s0, s1), (t0, t1) in pieces:
            v = w_ref[d, :, s0:s1]
            if lead is None:
                tgt[name][:, t0:t1] = v
            else:
                tgt[name][lead, :, t0:t1] = v

    return _call(
        body, "win_split", (D // tr,),
        [pl.BlockSpec((N_DEV, tr, cw), lambda i: (0, i, 0))],
        [pl.BlockSpec((3, tr, C), lambda i: (0, i, 0)), _rows(tr, WQ), _rows(tr, 2 * D)],
        [_sds((3, D, C), BF16), _sds((D, WQ), BF16), _sds((D, 2 * D), BF16)],
    )(win_w)


def _win_merge(d_bcv, d_qkr, d_gg, cw):
    _, D, C = d_bcv.shape
    WQ = d_qkr.shape[1]
    QL_KVL = WQ - LANES
    o1 = 3 * C
    segments = [("bcv", k, k * C, (k + 1) * C) for k in range(3)]
    segments += [("qkr", None, o1, o1 + QL_KVL + ROPE), ("gg", None, o1 + QL_KVL + ROPE, o1 + QL_KVL + ROPE + 2 * D)]
    pieces = _win_pieces(segments, cw)
    tr = _tile(D, 256, BF16_ROWS)

    def body(bcv_ref, qkr_ref, gg_ref, o_ref):
        src = dict(bcv=bcv_ref, qkr=qkr_ref, gg=gg_ref)
        for name, lead, d, (s0, s1), (t0, t1) in pieces:
            v = src[name][:, t0:t1] if lead is None else src[name][lead, :, t0:t1]
            o_ref[d, :, s0:s1] = v.astype(BF16)

    return _call(
        body, "win_merge", (D // tr,),
        [pl.BlockSpec((3, tr, C), lambda i: (0, i, 0)), _rows(tr, WQ), _rows(tr, 2 * D)],
        pl.BlockSpec((N_DEV, tr, cw), lambda i: (0, i, 0)),
        _sds((N_DEV, D, cw), BF16),
    )(d_bcv, d_qkr, d_gg)


def _mix_in(h, gain, w_bcv, w_qkr, w_gg):
    S, D = h.shape
    C = w_bcv.shape[2]
    tm = _tile(S, 512, SUBLANES)

    def body(h_ref, gain_ref, w1, w2, w3, o1, o2, o3, n_ref, r_ref):
        n32, rstd = _rms_fwd(h_ref[...], gain_ref[...])
        n = n32.astype(BF16)
        n_ref[...] = n
        r_ref[...] = rstd
        for k in range(3):
            o1[k] = _dot(n, w1[k]).astype(BF16)
        o2[...] = _dot(n, w2[...])
        o3[...] = _dot(n, w3[...]).astype(BF16)

    return _call(
        body, "mix_in", (S // tm,),
        [_rows(tm, D), _whole(gain), _whole(w_bcv), _whole(w_qkr), _whole(w_gg)],
        [pl.BlockSpec((3, tm, C), lambda i: (0, i, 0)), _rows(tm, w_qkr.shape[1]), _rows(tm, 2 * D),
         _rows(tm, D), _rows(tm, 1)],
        [_sds((3, S, C), BF16), _sds((S, w_qkr.shape[1]), F32), _sds((S, 2 * D), BF16), _sds((S, D), BF16),
         _sds((S, 1), F32)],
    )(h, gain, w_bcv, w_qkr, w_gg)


def _conv_taps(zc):
    rows = lax.broadcasted_iota(jnp.int32, zc.shape, 0)
    z1 = jnp.where(rows >= 1, pltpu.roll(zc, 1, 0), 0.0)
    z2 = jnp.where(rows >= 2, pltpu.roll(zc, 2, 0), 0.0)
    return z1, z2


def _conv_fwd(z_bcv, conv_w):
    _, S, C = z_bcv.shape

    def body(z_ref, w_ref, o_ref):
        w = w_ref[...]
        zc = z_ref[1].astype(F32) * z_ref[2].astype(F32)
        z1, z2 = _conv_taps(zc)
        y = w[0:1] * z2 + w[1:2] * z1 + w[2:3] * zc
        o_ref[...] = (z_ref[0].astype(F32) * y).astype(BF16)

    return _call(
        body, "conv_fwd", (C // LANES,),
        [pl.BlockSpec((3, S, LANES), lambda j: (0, 0, j)), pl.BlockSpec((3, LANES), lambda j: (0, j))],
        pl.BlockSpec((S, LANES), lambda j: (0, j)),
        _sds((S, C), BF16),
    )(z_bcv, conv_w)


def _rope(x, cs, half):
    c, s1, s2 = cs[:, :LANES], cs[:, LANES:2 * LANES], cs[:, 2 * LANES:]
    return x * c + pltpu.roll(x, LANES - half, 1) * s1 + pltpu.roll(x, half, 1) * s2


def _unrope(d, cs, half):
    c, s1, s2 = cs[:, :LANES], cs[:, LANES:2 * LANES], cs[:, 2 * LANES:]
    return d * c + pltpu.roll(d * s1, half, 1) + pltpu.roll(d * s2, LANES - half, 1)


def _mla_prep(z_qkr, gq, gkv, cs, c256_w):
    S = z_qkr.shape[0]
    QL, KVL = gq.shape[1], gkv.shape[1]
    H = N_DEV
    tm = _tile(S, 512, SUBLANES)
    half = ROPE // 2

    def body(z_ref, gq_ref, gkv_ref, cs_ref, w_ref, q_ref, k_ref, v_ref, qn_ref, kvn_ref, rq_ref, rkv_ref):
        z = z_ref[...]
        cs_t = cs_ref[...]
        qn32, rq = _rms_fwd(z[:, :QL], gq_ref[...])
        kvn32, rkv = _rms_fwd(z[:, QL:QL + KVL], gkv_ref[...])
        qn = qn32.astype(BF16)
        kvn = kvn32.astype(BF16)
        qn_ref[...] = qn
        kvn_ref[...] = kvn
        rq_ref[...] = rq
        rkv_ref[...] = rkv
        krope = _rope(z[:, QL + KVL:], cs_t, half).astype(BF16)
        for h in range(H):
            lo, mid, hi = h * HEAD_SLOT, h * HEAD_SLOT + LANES, (h + 1) * HEAD_SLOT
            q = _dot(qn, w_ref[h, KVL:KVL + QL, :])
            kv = _dot(kvn, w_ref[h, 0:KVL, :])
            q_ref[:, lo:mid] = (q[:, :LANES] * SCORE_SCALE).astype(BF16)
            q_ref[:, mid:hi] = (_rope(q[:, LANES:], cs_t, half) * SCORE_SCALE).astype(BF16)
            k_ref[:, lo:mid] = kv[:, :LANES].astype(BF16)
            k_ref[:, mid:hi] = krope
            v_ref[:, h * VDIM:(h + 1) * VDIM] = kv[:, LANES:].astype(BF16)

    return _call(
        body, "mla_prep", (S // tm,),
        [_rows(tm, z_qkr.shape[1]), _whole(gq), _whole(gkv), _rows(tm, 3 * LANES), _whole(c256_w)],
        [_rows(tm, H * HEAD_SLOT), _rows(tm, H * HEAD_SLOT), _rows(tm, H * VDIM), _rows(tm, QL), _rows(tm, KVL),
         _rows(tm, 1), _rows(tm, 1)],
        [_sds((S, H * HEAD_SLOT), BF16), _sds((S, H * HEAD_SLOT), BF16), _sds((S, H * VDIM), BF16),
         _sds((S, QL), BF16), _sds((S, KVL), BF16), _sds((S, 1), F32), _sds((S, 1), F32)],
    )(z_qkr, gq, gkv, cs, c256_w)


def _chunk_mask(t):
    shift = CHUNK.bit_length() - 1
    krow = lax.broadcasted_iota(jnp.int32, (t, t), 0) >> shift
    qcol = lax.broadcasted_iota(jnp.int32, (t, t), 1) >> shift
    return krow <= qcol


def _attn_fwd(q, k, v, H):
    S = q.shape[0]
    t = _tile(S, 512, CHUNK)
    nq = S // t

    def body(q_ref, k_ref, v_ref, o_ref, lse_ref, vt_ref):
        qi = pl.program_id(1)

        @pl.when(qi == 0)
        def _():
            vt_ref[0:VDIM, :] = v_ref[...].T
            vt_ref[VDIM:, :] = jnp.ones((BF16_ROWS, S), BF16)

        qv = q_ref[...]

        def block(start, width, carry, masked):
            m, acc = carry
            off = pl.multiple_of(start * t, t)
            s = _dot_nt(k_ref[pl.ds(off, width * t), :], qv)
            if masked:
                s = jnp.where(_chunk_mask(t), s, -1e30)
            m_new = jnp.maximum(m, jnp.max(s, axis=0, keepdims=True))
            p = jnp.exp2(s - m_new).astype(BF16)
            acc = jnp.exp2(m - m_new) * acc + _dot(vt_ref[:, pl.ds(off, width * t)], p)
            return m_new, acc

        init = (jnp.full((1, t), -1e30, F32), jnp.zeros((VDIM + BF16_ROWS, t), F32))
        wide = lax.div(qi, ATTN_FWD_WIDTH)
        carry = lax.fori_loop(0, wide, lambda j, c: block(j * ATTN_FWD_WIDTH, ATTN_FWD_WIDTH, c, False), init)
        carry = lax.fori_loop(wide * ATTN_FWD_WIDTH, qi, lambda kj, c: block(kj, 1, c, False), carry)
        m, acc = block(qi, 1, carry, True)
        l = acc[VDIM:VDIM + 1]
        o_ref[...] = (acc[0:VDIM] * (1.0 / l)).T.astype(BF16)
        lse_ref[0] = jnp.broadcast_to(m + jnp.log2(l), (SUBLANES, t))

    return _call(
        body, "attn_fwd", (H, nq),
        [pl.BlockSpec((t, HEAD_SLOT), lambda h, i: (i, h)), pl.BlockSpec((S, HEAD_SLOT), lambda h, i: (0, h)),
         pl.BlockSpec((S, VDIM), lambda h, i: (0, h))],
        [pl.BlockSpec((t, VDIM), lambda h, i: (i, h)), pl.BlockSpec((1, SUBLANES, t), lambda h, i: (h, 0, i))],
        [_sds((S, H * VDIM), BF16), _sds((H, SUBLANES, S), F32)],
        [pltpu.VMEM((VDIM + BF16_ROWS, S), BF16)],
    )(q, k, v)


def _merge_wo(o, by, z_gg, h, sq_w, c128_w):
    S, D = h.shape
    C = by.shape[1]
    r = sq_w.shape[1] // 3
    tm = _tile(S, 512, SUBLANES)

    def body(o_ref, by_ref, gg_ref, h_ref, wmo_ref, wo_ref, wco_ref, h2_ref, mg_ref, yc_ref, ym_ref):
        ymla = _dot(o_ref[...], wmo_ref[...].reshape(N_DEV * r, D))
        yconv = _dot(by_ref[...], _cat_slots(wco_ref))
        gg = gg_ref[...].astype(F32)
        merged = (_sig(gg[:, :D]) * yconv + _sig(gg[:, D:]) * ymla).astype(BF16)
        mg_ref[...] = merged
        yc_ref[...] = yconv.astype(BF16)
        ym_ref[...] = ymla.astype(BF16)
        h2_ref[...] = h_ref[...] + _dot(merged, wo_ref[...].reshape(N_DEV * r, D))

    return _call(
        body, "merge_wo", (S // tm,),
        [_rows(tm, o.shape[1]), _rows(tm, C), _rows(tm, 2 * D), _rows(tm, D), _slab(sq_w, r, 0), _slab(sq_w, r, 1),
         _slab(c128_w, C, 0)],
        [_rows(tm, D)] * 4,
        [_sds((S, D), F32)] + [_sds((S, D), BF16)] * 3,
    )(o, by, z_gg, h, sq_w, sq_w, c128_w)


def _ple_fwd(h, gain, p, sq_w, c128_w, C):
    S, D = h.shape
    P = p.shape[1]
    r = sq_w.shape[1] // 3
    tm = _tile(S, 512, SUBLANES)

    def body(h_ref, gain_ref, p_ref, wpg_ref, wpp_ref, o_ref, pre_ref, pp_ref, n_ref, r_ref):
        x = h_ref[...]
        n32, rstd = _rms_fwd(x, gain_ref[...])
        n = n32.astype(BF16)
        n_ref[...] = n
        r_ref[...] = rstd
        pre = _dot(n, wpg_ref[...].reshape(N_DEV * r, D))
        pp = _dot(p_ref[...].astype(BF16), _cat_slots(wpp_ref))
        pre_ref[...] = pre.astype(BF16)
        pp_ref[...] = pp.astype(BF16)
        o_ref[...] = x + _sig(pre) * pp

    return _call(
        body, "ple_fwd", (S // tm,),
        [_rows(tm, D), _whole(gain), _rows(tm, P), _slab(sq_w, r, 2), _slab(c128_w, P, C // P)],
        [_rows(tm, D), _rows(tm, D), _rows(tm, D), _rows(tm, D), _rows(tm, 1)],
        [_sds((S, D), F32)] + [_sds((S, D), BF16)] * 3 + [_sds((S, 1), F32)],
    )(h, gain, p, sq_w, c128_w)


def _final_loss(h, gain, target):
    S, D = h.shape
    tm = _tile(S, 512, SUBLANES)

    def body(h_ref, gain_ref, t_ref, dh_ref, loss_ref, dg_ref):
        @pl.when(pl.program_id(0) == 0)
        def _():
            loss_ref[...] = jnp.zeros_like(loss_ref)
            dg_ref[...] = jnp.zeros_like(dg_ref)

        x = h_ref[...]
        gain_v = gain_ref[...]
        y, rstd = _rms_fwd(x, gain_v)
        err = y - t_ref[...]
        loss_ref[...] += 0.5 * jnp.sum(jnp.mean(err * err, axis=-1, keepdims=True))
        dx, dgain = _rms_bwd(err * (1.0 / D), x, rstd, gain_v)
        dh_ref[...] = dx
        dg_ref[...] += dgain

    return _call(
        body, "final_loss", (S // tm,),
        [_rows(tm, D), _whole(gain), _rows(tm, D)],
        [_rows(tm, D), pl.BlockSpec((1, LANES), lambda i: (0, 0)), pl.BlockSpec((1, D), lambda i: (0, 0))],
        [_sds((S, D), F32), _sds((1, LANES), F32), _sds((1, D), F32)],
    )(h, gain, target)


def _tn_call(body, name, grid, in_specs, out_spec, out_shape, scratch, operands, prev):
    n = len(operands)
    if prev is None:
        return _call(body, name, grid, in_specs, out_spec, out_shape, scratch)(*operands)
    assert prev.shape == out_shape.shape and prev.dtype == out_shape.dtype

    def wrapped(*refs):
        body(*refs[:n], *refs[n + 1:])

    return _call(wrapped, name, grid, in_specs + [ANY], out_spec, out_shape, scratch, {n: 0})(*operands, prev)


def _transposed(x_ref, xt_ref, first):
    @pl.when(first)
    def _():
        xt_ref[...] = x_ref[...].astype(BF16).T


def _tn_slots(x, dy, prev, rows_total, row_off):
    S, K = x.shape
    B, _, c = dy.shape
    tk = _tile(K, 1024, LANES)

    def body(x_ref, dy_ref, o_ref, xt_ref):
        _transposed(x_ref, xt_ref, pl.program_id(1) == 0)
        o_ref[0] = _dot(xt_ref[...], dy_ref[0]).astype(BF16)

    return _tn_call(
        body, "tn_slots", (K // tk, B),
        [pl.BlockSpec((S, tk), lambda i, b: (0, i)), pl.BlockSpec((1, S, c), lambda i, b: (b, 0, 0))],
        pl.BlockSpec((1, tk, c), lambda i, b: (b, row_off // tk + i, 0)),
        _sds((B, rows_total, c), BF16), [pltpu.VMEM((tk, S), BF16)], [x, dy], prev)


def _tn_plain(x, dy, out_dtype=F32):
    S, K = x.shape
    B, _, c = dy.shape
    tk = _tile(K, 512, LANES)
    tn = _tile(c, 1024, LANES)

    def body(x_ref, dy_ref, o_ref, xt_ref):
        _transposed(x_ref, xt_ref, (pl.program_id(1) == 0) & (pl.program_id(2) == 0))
        o_ref[0] = _dot(xt_ref[...], dy_ref[0]).astype(out_dtype)

    return _call(
        body, "tn_plain", (K // tk, B, c // tn),
        [pl.BlockSpec((S, tk), lambda i, b, j: (0, i)), pl.BlockSpec((1, S, tn), lambda i, b, j: (b, 0, j))],
        pl.BlockSpec((1, tk, tn), lambda i, b, j: (b, i, j)),
        _sds((B, K, c), out_dtype), [pltpu.VMEM((tk, S), BF16)],
    )(x, dy)


def _tn_down(at, dh, prev, rows_total, which):
    nb, c, S = at.shape
    D = dh.shape[1]
    r = c // 2
    tn = _tile(D, 512, LANES)

    def body(at_ref, dh_ref, o_ref):
        g = 0.5 * _dot(at_ref[0], dh_ref[...].astype(BF16))
        o_ref[...] = g.astype(BF16).reshape(2, r, tn)

    return _tn_call(
        body, "tn_down", (D // tn, nb),
        [pl.BlockSpec((1, c, S), lambda j, i: (i, 0, 0)), pl.BlockSpec((S, tn), lambda j, i: (0, j))],
        pl.BlockSpec((2, r, tn), lambda j, i: (i, which, j)),
        _sds((N_DEV, rows_total, D), BF16), [], [at, dh], prev)


def _tn_square(x, dy, prev, rows_total, member):
    S, K = x.shape
    N = dy.shape[1]
    r = K // N_DEV
    tk = _tile(K, 512, r)
    tn = _tile(N, 512, LANES)

    def body(x_ref, dy_ref, o_ref, xt_ref):
        _transposed(x_ref, xt_ref, pl.program_id(1) == 0)
        g = _dot(xt_ref[...], dy_ref[...].astype(BF16))
        o_ref[...] = g.astype(BF16).reshape(tk // r, r, tn)

    return _tn_call(
        body, "tn_square", (K // tk, N // tn),
        [pl.BlockSpec((S, tk), lambda i, j: (0, i)), pl.BlockSpec((S, tn), lambda i, j: (0, j))],
        pl.BlockSpec((tk // r, r, tn), lambda i, j: (i, member, j)),
        _sds((N_DEV, rows_total, N), BF16), [pltpu.VMEM((tk, S), BF16)], [x, dy], prev)


def _tn_cols(x, dy, prev, rows_total, row_block):
    S, K = x.shape
    N = dy.shape[1]
    cw = N // N_DEV

    def body(x_ref, dy_ref, o_ref):
        g = _dot(x_ref[...].astype(BF16).T, dy_ref[...])
        for d in range(N_DEV):
            o_ref[d] = g[:, d * cw:(d + 1) * cw].astype(BF16)

    return _tn_call(
        body, "tn_cols", (1,),
        [pl.BlockSpec((S, K), lambda i: (0, 0)), pl.BlockSpec((S, N), lambda i: (0, 0))],
        pl.BlockSpec((N_DEV, K, cw), lambda i: (0, row_block, 0)),
        _sds((N_DEV, rows_total, cw), BF16), [], [x, dy], prev)


def _tn_heads(qn, kvn, dqp, dkv):
    S, QL = qn.shape
    KVL = kvn.shape[1]

    def body(qn_ref, kvn_ref, dq_ref, dkv_ref, o_ref):
        o_ref[0, 0:KVL, :] = _dot(kvn_ref[...].T, dkv_ref[...]).astype(BF16)
        o_ref[0, KVL:KVL + QL, :] = _dot(qn_ref[...].T, dq_ref[...]).astype(BF16)

    head = pl.BlockSpec((S, HEAD_SLOT), lambda h: (0, h))
    return _call(
        body, "tn_heads", (N_DEV,),
        [pl.BlockSpec((S, QL), lambda h: (0, 0)), pl.BlockSpec((S, KVL), lambda h: (0, 0)), head, head],
        pl.BlockSpec((1, KVL + QL, HEAD_SLOT), lambda h: (h, 0, 0)),
        _sds((N_DEV, KVL + QL, HEAD_SLOT), BF16),
    )(qn, kvn, dqp, dkv)


def _ple_bwd(dh, pre, pp, h, rstd, gain, sq_w, after):
    S, D = h.shape
    r = sq_w.shape[1] // 3
    tm = _tile(S, 512, SUBLANES)

    def body(dh_ref, pre_ref, pp_ref, h_ref, r_ref, gain_ref, wpg_ref, *rest):
        o_ref, dpre_ref, dpp_ref, dg_ref = rest[len(after):]

        @pl.when(pl.program_id(0) == 0)
        def _():
            dg_ref[...] = jnp.zeros_like(dg_ref)

        d = dh_ref[...]
        gate = _sig(pre_ref[...].astype(F32))
        dpre = (d * pp_ref[...].astype(F32) * gate * (1.0 - gate)).astype(BF16)
        dpre_ref[...] = dpre
        dpp_ref[...] = (d * gate).astype(BF16)
        dn = _dot_nt(dpre, wpg_ref[...].reshape(N_DEV * r, D))
        dx, dgain = _rms_bwd(dn, h_ref[...], r_ref[...], gain_ref[...])
        o_ref[...] = d + dx
        dg_ref[...] += dgain

    return _call(
        body, "ple_bwd", (S // tm,),
        [_rows(tm, D), _rows(tm, D), _rows(tm, D), _rows(tm, D), _rows(tm, 1), _whole(gain), _slab(sq_w, r, 2)]
        + [ANY] * len(after),
        [_rows(tm, D), _rows(tm, D), _rows(tm, D), pl.BlockSpec((1, D), lambda i: (0, 0))],
        [_sds((S, D), F32), _sds((S, D), BF16), _sds((S, D), BF16), _sds((1, D), F32)],
    )(dh, pre, pp, h, rstd, gain, sq_w, *after)


def _ffn_bwd_act(dh, dn_w, which, jac, after=()):
    S, D = dh.shape
    _, _, c = jac.shape
    nb = N_DEV // 2
    tm = _tile(S, 512, SUBLANES)

    def body(dh_ref, w_ref, jac_ref, *rest):
        dgu_ref = rest[len(after)]
        dhb = dh_ref[...].astype(BF16)
        for d in range(nb):
            da = _dot_nt(dhb, _down_weight(w_ref, d, c))
            dgu_ref[d] = (da * jac_ref[d].astype(F32)).astype(BF16)
            dgu_ref[nb + d] = (da * jac_ref[nb + d].astype(F32)).astype(BF16)

    act = pl.BlockSpec((N_DEV, tm, c), lambda i: (0, i, 0))
    return _call(
        body, "ffn_bwd_act", (S // tm,),
        [_rows(tm, D), _slab(dn_w, c // 2, which), act] + [ANY] * len(after),
        act,
        _sds((N_DEV, S, c), BF16),
    )(dh, dn_w, jac, *after)


def _ffn_bwd_in(dgu, gu_w, which, h, rstd, gain, dh):
    S, D = h.shape
    c = dgu.shape[2]
    tm = _tile(S, 512, SUBLANES)

    def body(dgu_ref, w_ref, h_ref, r_ref, gain_ref, dh_ref, o_ref, dgain_ref):
        @pl.when(pl.program_id(0) == 0)
        def _():
            dgain_ref[...] = jnp.zeros_like(dgain_ref)

        dn = _dot_nt(dgu_ref[0], w_ref[0])
        for d in range(1, N_DEV):
            dn = dn + _dot_nt(dgu_ref[d], w_ref[d])
        dx, dgain = _rms_bwd(dn, h_ref[...], r_ref[...], gain_ref[...])
        o_ref[...] = dh_ref[...] + dx
        dgain_ref[...] += dgain

    return _call(
        body, "ffn_bwd_in", (S // tm,),
        [pl.BlockSpec((N_DEV, tm, c), lambda i: (0, i, 0)), _slab(gu_w, D, which), _rows(tm, D), _rows(tm, 1),
         _whole(gain), _rows(tm, D)],
        [_rows(tm, D), pl.BlockSpec((1, D), lambda i: (0, 0))],
        [_sds((S, D), F32), _sds((1, D), F32)],
    )(dgu, gu_w, h, rstd, gain, dh)


def _merge_bwd(dh, z_gg, yconv, ymla, o, sq_w, c128_w, C, after):
    S, D = dh.shape
    r = sq_w.shape[1] // 3
    HV = N_DEV * r
    H = HV // VDIM
    tm = _tile(S, 512, SUBLANES)

    def head_rows():
        row = lax.broadcasted_iota(jnp.int32, (SUBLANES * H, HV), 0) >> (SUBLANES.bit_length() - 1)
        col = lax.broadcasted_iota(jnp.int32, (SUBLANES * H, HV), 1) >> (VDIM.bit_length() - 1)
        return jnp.where(row == col, 1.0, 0.0).astype(BF16)

    def body(dh_ref, gg_ref, yc_ref, ym_ref, o_ref, wmo_ref, wo_ref, wco_ref, *rest):
        dgg_ref, dby_ref, do_ref, dyc_ref, dym_ref, dl_ref = rest[len(after):]
        dm = _dot_nt(dh_ref[...].astype(BF16), wo_ref[...].reshape(HV, D))
        gg = gg_ref[...].astype(F32)
        sgc = _sig(gg[:, :D])
        sgm = _sig(gg[:, D:])
        dyc = (dm * sgc).astype(BF16)
        dym = (dm * sgm).astype(BF16)
        dyc_ref[...] = dyc
        dym_ref[...] = dym
        dgg_ref[:, :D] = (dm * yc_ref[...].astype(F32) * sgc * (1.0 - sgc)).astype(BF16)
        dgg_ref[:, D:] = (dm * ym_ref[...].astype(F32) * sgm * (1.0 - sgm)).astype(BF16)
        dby_ref[...] = _dot_nt(dyc, _cat_slots(wco_ref)).astype(BF16)
        do = _dot_nt(dym, wmo_ref[...].reshape(HV, D)).astype(BF16)
        do_ref[...] = do
        prod = do.astype(F32) * o_ref[...].astype(F32)
        hi = prod.astype(BF16)
        lo = (prod - hi.astype(F32)).astype(BF16)
        pick = head_rows()
        dl_ref[...] = _dot_nt(pick, hi) + _dot_nt(pick, lo)

    return _call(
        body, "merge_bwd", (S // tm,),
        [_rows(tm, D), _rows(tm, 2 * D), _rows(tm, D), _rows(tm, D), _rows(tm, HV), _slab(sq_w, r, 0),
         _slab(sq_w, r, 1), _slab(c128_w, C, 0)] + [ANY] * len(after),
        [_rows(tm, 2 * D), _rows(tm, C), _rows(tm, HV), _rows(tm, D), _rows(tm, D),
         pl.BlockSpec((SUBLANES * H, tm), lambda i: (0, i))],
        [_sds((S, 2 * D), BF16), _sds((S, C), BF16), _sds((S, HV), BF16), _sds((S, D), BF16), _sds((S, D), BF16),
         _sds((SUBLANES * H, S), F32)],
    )(dh, z_gg, yconv, ymla, o, sq_w, sq_w, c128_w, *after)


def _conv_bwd(z_bcv, conv_w, dby):
    _, S, C = z_bcv.shape

    def body(z_ref, w_ref, dby_ref, dz_ref, dw_ref):
        w = w_ref[...]
        c = z_ref[1].astype(F32)
        v = z_ref[2].astype(F32)
        d = dby_ref[...].astype(F32)
        zc = c * v
        z1, z2 = _conv_taps(zc)
        y = w[0:1] * z2 + w[1:2] * z1 + w[2:3] * zc
        dz_ref[0] = (d * y).astype(BF16)
        dy = d * z_ref[0].astype(F32)
        rows = lax.broadcasted_iota(jnp.int32, dy.shape, 0)
        dy1 = jnp.where(rows < S - 1, pltpu.roll(dy, S - 1, 0), 0.0)
        dy2 = jnp.where(rows < S - 2, pltpu.roll(dy, S - 2, 0), 0.0)
        dzc = w[2:3] * dy + w[1:2] * dy1 + w[0:1] * dy2
        dz_ref[1] = (dzc * v).astype(BF16)
        dz_ref[2] = (dzc * c).astype(BF16)
        dw_ref[0:1, :] = jnp.sum(dy * z2, axis=0, keepdims=True)
        dw_ref[1:2, :] = jnp.sum(dy * z1, axis=0, keepdims=True)
        dw_ref[2:3, :] = jnp.sum(dy * zc, axis=0, keepdims=True)

    three = pl.BlockSpec((3, S, LANES), lambda j: (0, 0, j))
    wspec = pl.BlockSpec((3, LANES), lambda j: (0, j))
    return _call(
        body, "conv_bwd", (C // LANES,),
        [three, wspec, pl.BlockSpec((S, LANES), lambda j: (0, j))],
        [three, wspec],
        [_sds((3, S, C), BF16), _sds((3, C), F32)],
    )(z_bcv, conv_w, dby)


def _attn_bwd(q, k, v, do, lse, delta, H):
    S = q.shape[0]
    t = _tile(S, 512, CHUNK)
    nk = S // t

    def body(q_ref, k_ref, v_ref, do_ref, lse_ref, dl_ref, dq_ref, dk_ref, dv_ref, dqt_ref):
        kj = pl.program_id(1)

        @pl.when(kj == 0)
        def _():
            dqt_ref[...] = jnp.zeros_like(dqt_ref)

        kv = k_ref[...]
        vv = v_ref[...]
        kt = kv.T

        def block(start, width, carry, masked):
            dk, dv = carry
            off = pl.multiple_of(start * t, t)
            qv = q_ref[pl.ds(off, width * t), :]
            dov = do_ref[pl.ds(off, width * t), :]
            s = _dot_nt(kv, qv)
            if masked:
                s = jnp.where(_chunk_mask(t), s, -1e30)
            p = jnp.exp2(s - lse_ref[0, 0:1, pl.ds(off, width * t)])
            dp = _dot_nt(vv, dov)
            ds = (p * (dp - dl_ref[0, 0:1, pl.ds(off, width * t)]) * LN2).astype(BF16)
            dqt_ref[:, pl.ds(off, width * t)] += _dot(kt, ds)
            return dk + _dot(ds, qv), dv + _dot(p.astype(BF16), dov)

        init = (jnp.zeros((t, HEAD_SLOT), F32), jnp.zeros((t, VDIM), F32))
        carry = block(kj, 1, init, True)
        wide = lax.div(nk - 1 - kj, ATTN_BWD_WIDTH)
        carry = lax.fori_loop(
            0, wide, lambda j, c: block(kj + 1 + j * ATTN_BWD_WIDTH, ATTN_BWD_WIDTH, c, False), carry)
        dk, dv = lax.fori_loop(kj + 1 + wide * ATTN_BWD_WIDTH, nk, lambda qi, c: block(qi, 1, c, False), carry)
        dk_ref[...] = dk.astype(BF16)
        dv_ref[...] = dv.astype(BF16)

        @pl.when(kj == nk - 1)
        def _():
            dq_ref[...] = (dqt_ref[...] * SCORE_SCALE).T.astype(BF16)

    kspec = lambda w: pl.BlockSpec((t, w), lambda h, j: (j, h))
    qspec = lambda w: pl.BlockSpec((S, w), lambda h, j: (0, h))
    stat = pl.BlockSpec((1, SUBLANES, S), lambda h, j: (h, 0, 0))
    return _call(
        body, "attn_bwd", (H, nk),
        [qspec(HEAD_SLOT), kspec(HEAD_SLOT), kspec(VDIM), qspec(VDIM), stat, stat],
        [qspec(HEAD_SLOT), kspec(HEAD_SLOT), kspec(VDIM)],
        [_sds((S, H * HEAD_SLOT), BF16), _sds((S, H * HEAD_SLOT), BF16), _sds((S, H * VDIM), BF16)],
        [pltpu.VMEM((HEAD_SLOT, S), F32)],
    )(q, k, v, do, lse, delta)


def _mla_prep_bwd(dq, dk, dv, z_qkr, rq, rkv, gq, gkv, cs, c256_w):
    S = z_qkr.shape[0]
    QL, KVL = gq.shape[1], gkv.shape[1]
    H = N_DEV
    tm = _tile(S, 512, SUBLANES)
    half = ROPE // 2

    def body(dq_ref, dk_ref, dv_ref, z_ref, rq_ref, rkv_ref, gq_ref, gkv_ref, cs_ref, w_ref,
             dz_ref, dqp_ref, dkv_ref, dgq_ref, dgkv_ref):
        @pl.when(pl.program_id(0) == 0)
        def _():
            dgq_ref[...] = jnp.zeros_like(dgq_ref)
            dgkv_ref[...] = jnp.zeros_like(dgkv_ref)

        cs_t = cs_ref[...]
        dkr = jnp.zeros((tm, LANES), F32)
        dqn = jnp.zeros((tm, QL), F32)
        dkvn = jnp.zeros((tm, KVL), F32)
        for h in range(H):
            lo, mid, hi = h * HEAD_SLOT, h * HEAD_SLOT + LANES, (h + 1) * HEAD_SLOT
            dqp_ref[:, lo:mid] = dq_ref[:, lo:mid]
            dqp_ref[:, mid:hi] = _unrope(dq_ref[:, mid:hi].astype(F32), cs_t, half).astype(BF16)
            dkv_ref[:, lo:mid] = dk_ref[:, lo:mid]
            dkv_ref[:, mid:hi] = dv_ref[:, h * VDIM:(h + 1) * VDIM]
            dkr = dkr + dk_ref[:, mid:hi].astype(F32)
            dqn = dqn + _dot_nt(dqp_ref[:, lo:hi], w_ref[h, KVL:KVL + QL, :])
            dkvn = dkvn + _dot_nt(dkv_ref[:, lo:hi], w_ref[h, 0:KVL, :])
        z = z_ref[...]
        dqc, dgq = _rms_bwd(dqn, z[:, :QL], rq_ref[...], gq_ref[...])
        dkvc, dgkv = _rms_bwd(dkvn, z[:, QL:QL + KVL], rkv_ref[...], gkv_ref[...])
        dz_ref[:, :QL] = dqc.astype(BF16)
        dz_ref[:, QL:QL + KVL] = dkvc.astype(BF16)
        dz_ref[:, QL + KVL:] = _unrope(dkr, cs_t, half).astype(BF16)
        dgq_ref[...] += dgq
        dgkv_ref[...] += dgkv

    W = z_qkr.shape[1]
    return _call(
        body, "mla_prep_bwd", (S // tm,),
        [_rows(tm, H * HEAD_SLOT), _rows(tm, H * HEAD_SLOT), _rows(tm, H * VDIM), _rows(tm, W), _rows(tm, 1),
         _rows(tm, 1), _whole(gq), _whole(gkv), _rows(tm, 3 * LANES), _whole(c256_w)],
        [_rows(tm, W), _rows(tm, H * HEAD_SLOT), _rows(tm, H * HEAD_SLOT), _whole(gq), _whole(gkv)],
        [_sds((S, W), BF16), _sds((S, H * HEAD_SLOT), BF16), _sds((S, H * HEAD_SLOT), BF16),
         _sds((1, QL), F32), _sds((1, KVL), F32)],
    )(dq, dk, dv, z_qkr, rq, rkv, gq, gkv, cs, c256_w)


def _mix_in_bwd(d_bcv, dz_qkr, dgg, w_bcv, w_qkr, w_gg, h, rstd, gain, dh):
    S, D = h.shape
    C = d_bcv.shape[2]
    tm = _tile(S, 512, SUBLANES)

    def body(db_ref, dq_ref, dgg_ref, wb_ref, wq_ref, wg_ref, h_ref, r_ref, gain_ref, dh_ref, o_ref, dgain_ref):
        @pl.when(pl.program_id(0) == 0)
        def _():
            dgain_ref[...] = jnp.zeros_like(dgain_ref)

        dn = _dot_nt(dq_ref[...], wq_ref[...]) + _dot_nt(dgg_ref[...], wg_ref[...])
        for k in range(3):
            dn = dn + _dot_nt(db_ref[k], wb_ref[k])
        dx, dgain = _rms_bwd(dn, h_ref[...], r_ref[...], gain_ref[...])
        o_ref[...] = dh_ref[...] + dx
        dgain_ref[...] += dgain

    return _call(
        body, "mix_in_bwd", (S // tm,),
        [pl.BlockSpec((3, tm, C), lambda i: (0, i, 0)), _rows(tm, dz_qkr.shape[1]), _rows(tm, dgg.shape[1]),
         _whole(w_bcv), _whole(w_qkr), _whole(w_gg), _rows(tm, D), _rows(tm, 1), _whole(gain), _rows(tm, D)],
        [_rows(tm, D), pl.BlockSpec((1, D), lambda i: (0, 0))],
        [_sds((S, D), F32), _sds((1, D), F32)],
    )(d_bcv, dz_qkr, dgg, w_bcv, w_qkr, w_gg, h, rstd, gain, dh)


def _rope_tables(positions):
    half = ROPE // 2
    inv_freq = ROPE_THETA ** (-jnp.arange(0, ROPE, 2, dtype=F32) / ROPE)
    ang = positions.astype(F32)[:, None] * inv_freq
    cos, sin = jnp.cos(ang), jnp.sin(ang)
    z = jnp.zeros_like(cos)
    pad = jnp.zeros((positions.shape[0], LANES - 2 * half), F32)
    return jnp.concatenate([cos, cos, pad, -sin, z, pad, z, sin, pad], axis=1)


def _grad_rows(w):
    return dict(gu=2 * w["gu1"].shape[1], dn=2 * w["dn1"].shape[1], sq=w["sq"].shape[1], win=w["win"].shape[1],
                c128=w["c128"].shape[1], c256=w["c256"].shape[1])


def _layer_fwd(h0, p_l, cs, w, sm, late):
    C = sm["conv_w"].shape[1]
    QL, KVL = sm["q_norm"].shape[1], sm["kv_norm"].shape[1]
    jac1, a1, at1, n1, r1 = _ffn_up(h0, sm["ffn1_norm"], w["gu1"], 0)
    h1 = _ffn_down(a1, w["dn1"], 0, h0)
    if late is not None:
        w.update(late(h1))
    w_bcv, w_qkr, w_gg = _win_split(w["win"], C, QL, KVL)
    z_bcv, z_qkr, z_gg, un, rm = _mix_in(h1, sm["mix_norm"], w_bcv, w_qkr, w_gg)
    by = _conv_fwd(z_bcv, sm["conv_w"])
    q, k, v, qn, kvn, rq, rkv = _mla_prep(z_qkr, sm["q_norm"], sm["kv_norm"], cs, w["c256"])
    o, lse = _attn_fwd(q, k, v, N_DEV)
    h2, merged, yconv, ymla = _merge_wo(o, by, z_gg, h1, w["sq"], w["c128"])
    jac2, a2, at2, n2, r2 = _ffn_up(h2, sm["ffn2_norm"], w["gu2"], 0)
    h3 = _ffn_down(a2, w["dn2"], 0, h2)
    h4, pre, pp, pn, rp = _ple_fwd(h3, sm["ple_norm"], p_l, w["sq"], w["c128"], C)
    saved = dict(h0=h0, jac1=jac1, at1=at1, n1=n1, r1=r1, h1=h1, w_bcv=w_bcv, w_qkr=w_qkr, w_gg=w_gg, z_bcv=z_bcv,
                 z_qkr=z_qkr, z_gg=z_gg, un=un, rm=rm, by=by, q=q, k=k, v=v, qn=qn, kvn=kvn, rq=rq, rkv=rkv, o=o,
                 lse=lse, h2=h2, merged=merged, yconv=yconv, ymla=ymla, jac2=jac2, at2=at2, n2=n2, r2=r2, h3=h3,
                 pre=pre, pp=pp, pn=pn, rp=rp, p=p_l)
    return h4, saved


def _layer_bwd_late(dh4, s, w, sm, after):
    D = dh4.shape[1]
    C = sm["conv_w"].shape[1]
    P = s["p"].shape[1]
    rows = _grad_rows(w)
    small = {}
    dh3, dpre, dpp, small["ple_norm"] = _ple_bwd(dh4, s["pre"], s["pp"], s["h3"], s["rp"], sm["ple_norm"], w["sq"],
                                                 after)
    g_sq = _tn_square(s["pn"], dpre, None, rows["sq"], 2)
    g_c128 = _tn_cols(s["p"], dpp, None, rows["c128"], C // P)

    dgu2 = _ffn_bwd_act(dh3, w["dn2"], 0, s["jac2"])
    g_dn = _tn_down(s["at2"], dh3, None, rows["dn"], 1)
    g_gu = _tn_slots(s["n2"], dgu2, None, rows["gu"], D)
    dh2, small["ffn2_norm"] = _ffn_bwd_in(dgu2, w["gu2"], 0, s["h2"], s["r2"], sm["ffn2_norm"], dh3)
    return dh2, dict(gu=g_gu, dn=g_dn, sq=g_sq, c128=g_c128), small


def _layer_bwd_mixer(dh2, part, small, s, cs, w, sm, after):
    C = sm["conv_w"].shape[1]
    rows = _grad_rows(w)
    g_gu, g_dn, g_sq, g_c128 = part["gu"], part["dn"], part["sq"], part["c128"]

    dgg, dby, do, dyc, dym, delta = _merge_bwd(dh2, s["z_gg"], s["yconv"], s["ymla"], s["o"], w["sq"], w["c128"], C,
                                               after)
    g_sq = _tn_square(s["merged"], dh2, g_sq, rows["sq"], 1)
    g_sq = _tn_square(s["o"], dym, g_sq, rows["sq"], 0)
    g_c128 = _tn_cols(s["by"], dyc, g_c128, rows["c128"], 0)
    d_bcv, small["conv_w"] = _conv_bwd(s["z_bcv"], sm["conv_w"], dby)
    delta = delta.reshape(N_DEV, SUBLANES, delta.shape[1])
    dq, dk, dv = _attn_bwd(s["q"], s["k"], s["v"], do, s["lse"], delta, N_DEV)
    dz_qkr, dqp, dkv, small["q_norm"], small["kv_norm"] = _mla_prep_bwd(
        dq, dk, dv, s["z_qkr"], s["rq"], s["rkv"], sm["q_norm"], sm["kv_norm"], cs, w["c256"])
    g_c256 = _tn_heads(s["qn"], s["kvn"], dqp, dkv)
    un = s["un"]
    g_win = _win_merge(_tn_plain(un, d_bcv), _tn_plain(un, dz_qkr[None])[0], _tn_plain(un, dgg[None])[0],
                       w["win"].shape[2])
    dh1, small["mix_norm"] = _mix_in_bwd(d_bcv, dz_qkr, dgg, s["w_bcv"], s["w_qkr"], s["w_gg"], s["h1"], s["rm"],
                                         sm["mix_norm"], dh2)
    return dh1, dict(gu=g_gu, dn=g_dn, sq=g_sq, win=g_win, c128=g_c128, c256=g_c256), small


def _layer_bwd_first(dh1, part, small, s, w, sm, after):
    rows = _grad_rows(w)
    dgu1 = _ffn_bwd_act(dh1, w["dn1"], 0, s["jac1"], after)
    g_dn = _tn_down(s["at1"], dh1, part["dn"], rows["dn"], 0)
    g_gu = _tn_slots(s["n1"], dgu1, part["gu"], rows["gu"], 0)
    dh0, small["ffn1_norm"] = _ffn_bwd_in(dgu1, w["gu1"], 0, s["h0"], s["r1"], sm["ffn1_norm"], dh1)
    return dh0, dict(part, gu=g_gu, dn=g_dn), small


def _mesh_pos():
    return lax.axis_index("x"), lax.axis_index("y"), lax.axis_index("c")


def _other_chips(x, y):
    return [(1 - x, y), (x, 1 - y), (1 - x, 1 - y)]


def _pack(arrs, flipped, width):
    L = arrs[0].shape[0]
    shapes = [a.shape[:0:-1] if f else a.shape[1:] for a, f in zip(arrs, flipped)]
    R = sum(r for r, _ in shapes)

    def body(*refs):
        o_ref = refs[-1]
        off = 0
        for a_ref, f, (r, c) in zip(refs[:-1], flipped, shapes):
            a = a_ref[0].T if f else a_ref[0]
            o_ref[0, off:off + r, 0:c] = a.astype(BF16)
            if c < width:
                o_ref[0, off:off + r, c:width] = jnp.zeros((r, width - c), BF16)
            off += r

    return _call(
        body, "pack", (L,),
        [pl.BlockSpec((1,) + a.shape[1:], lambda l: (l, 0, 0)) for a in arrs],
        pl.BlockSpec((1, R, width), lambda l: (l, 0, 0)),
        _sds((L, R, width), BF16),
    )(*arrs)


def _handshake(peers):
    barrier = pltpu.get_barrier_semaphore()
    for peer in peers:
        pl.semaphore_signal(barrier, inc=1, device_id=peer, device_id_type=MESH)
    pl.semaphore_wait(barrier, len(peers))


def _sequencer_call(body, name, out_types, sems, collective_id, operands):
    return pl.kernel(
        body, name=name, out_type=out_types,
        mesh=plsc.ScalarSubcoreMesh(axis_name="seq", num_cores=1),
        scratch_types=tuple(pltpu.SemaphoreType.DMA((k,)) for k in sems),
        compiler_params=pltpu.CompilerParams(collective_id=collective_id),
    )(*operands)


def _all_gather(packs, l, after, collective_id):
    n = len(packs)

    def body(*refs):
        ins, outs = refs[:n], refs[n + len(after):2 * n + len(after)]
        send_sems, recv_sems, local_sems = refs[2 * n + len(after):]
        x, y, c = _mesh_pos()
        me, sibling = (x, y, c), (x, y, 1 - c)
        chips = _other_chips(x, y)
        _handshake([sibling] + [(*chip, c) for chip in chips])

        def copy(q, k, block, to, src=None):
            slot = outs[q].at[4 * block[0] + 2 * block[1] + block[2]]
            return pltpu.make_async_remote_copy(
                src_ref=slot if src is None else src, dst_ref=slot,
                send_sem=send_sems.at[7 * q + k], recv_sem=recv_sems.at[7 * q + k], device_id=to, device_id_type=MESH)

        started = []
        for q in range(n):
            src = ins[q].at[l]
            mine = pltpu.make_async_copy(src, outs[q].at[4 * x + 2 * y + c], local_sems.at[q])
            mine.start()
            started.append(mine)
        sends = []
        for q in range(n):
            src = ins[q].at[l]
            sends.append(copy(q, 0, me, sibling, src=src))
            sends += [copy(q, 1 + j, me, (*chip, c), src=src) for j, chip in enumerate(chips)]
        for cp in sends:
            cp.start()
        for q in range(n):
            for j, chip in enumerate(chips):
                copy(q, 1 + j, (*chip, c), me).wait_recv()
                fwd = copy(q, 4 + j, (*chip, c), sibling)
                fwd.start()
                sends.append(fwd)
        for q in range(n):
            copy(q, 0, sibling, me).wait_recv()
            for j, chip in enumerate(chips):
                copy(q, 4 + j, (*chip, 1 - c), me).wait_recv()
        for cp in sends:
            cp.wait_send()
        for mine in started:
            mine.wait()

    return _sequencer_call(
        body, f"all_gather_{collective_id}", [_sds((N_DEV,) + p.shape[1:], p.dtype) for p in packs], (7 * n, 7 * n, n),
        collective_id, list(packs) + list(after))


def _rs_d2d(gs, l, collective_id):
    n = len(gs)

    def body(*refs):
        ins, outs = refs[:n], refs[n:2 * n]
        send_sems, recv_sems = refs[2 * n:]
        x, y, c = _mesh_pos()
        _handshake([(x, y, 1 - c)])
        copies = []
        for q in range(n):
            for j in range(4):
                copies.append(pltpu.make_async_remote_copy(
                    src_ref=ins[q].at[2 * j + (1 - c)], dst_ref=outs[q].at[j], send_sem=send_sems.at[4 * q + j],
                    recv_sem=recv_sems.at[4 * q + j], device_id=(x, y, 1 - c), device_id_type=MESH))
        for cp in copies:
            cp.start()
        for cp in copies:
            cp.wait()

    return _sequencer_call(
        body, f"rs_d2d_{l}", [_sds((4,) + g.shape[1:], g.dtype) for g in gs], (4 * n, 4 * n), collective_id, gs)


def _rs_add_chip(gs, as_, after):
    n = len(gs)
    steps = 4
    tiles = [g.shape[1] // steps for g in gs]

    def chip(k):
        x, y, _ = _mesh_pos()
        return ([(x, y)] + _other_chips(x, y))[k]

    def body(*refs):
        g_refs, a_refs = refs[:4 * n], refs[4 * n:8 * n]
        own_refs, t_refs = refs[8 * n + len(after):9 * n + len(after)], refs[9 * n + len(after):]
        for q in range(n):
            g, a = g_refs[4 * q:4 * q + 4], a_refs[4 * q:4 * q + 4]
            own_refs[q][...] = g[0][0].astype(F32) + a[0][0].astype(F32)
            for k in range(1, 4):
                t_refs[q][k - 1] = (g[k][0].astype(F32) + a[k][0].astype(F32)).astype(BF16)

    def gspec(q, k):
        def index(i):
            px, py = chip(k)
            return 4 * px + 2 * py + lax.axis_index("c"), i, 0
        return pl.BlockSpec((1, tiles[q], gs[q].shape[2]), index)

    def aspec(q, k):
        def index(i):
            px, py = chip(k)
            return 2 * px + py, i, 0
        return pl.BlockSpec((1, tiles[q], gs[q].shape[2]), index)

    in_specs = [gspec(q, k) for q in range(n) for k in range(4)] + [aspec(q, k) for q in range(n) for k in range(4)]
    operands = [g for g in gs for _ in range(4)] + [a for a in as_ for _ in range(4)]
    out_specs = [pl.BlockSpec((tiles[q], gs[q].shape[2]), lambda i: (i, 0)) for q in range(n)]
    out_specs += [pl.BlockSpec((3, tiles[q], gs[q].shape[2]), lambda i: (0, i, 0)) for q in range(n)]
    out_shape = [_sds(g.shape[1:], F32) for g in gs] + [_sds((3,) + g.shape[1:], BF16) for g in gs]
    res = _call(body, "rs_add_chip", (steps,), in_specs + [ANY] * len(after), out_specs, out_shape)(*operands, *after)
    return res[:n], res[n:]


def _rs_ici(ts, l, collective_id):
    n = len(ts)

    def body(*refs):
        ins, outs = refs[:n], refs[n:2 * n]
        send_sems, recv_sems = refs[2 * n:]
        x, y, c = _mesh_pos()
        chips = _other_chips(x, y)
        _handshake([(*chip, c) for chip in chips])
        copies = []
        for q in range(n):
            for k, chip in enumerate(chips):
                copies.append(pltpu.make_async_remote_copy(
                    src_ref=ins[q].at[k], dst_ref=outs[q].at[k], send_sem=send_sems.at[3 * q + k],
                    recv_sem=recv_sems.at[3 * q + k], device_id=(*chip, c), device_id_type=MESH))
        for cp in copies:
            cp.start()
        for cp in copies:
            cp.wait()

    return _sequencer_call(
        body, f"rs_ici_{l}", [_sds(t.shape, t.dtype) for t in ts], (3 * n, 3 * n), collective_id, ts)


def _all_reduce_small(v):
    n, W = v.shape

    def body(v_ref, out_ref, slots, send_sems, recv_sems):
        x, y, c = _mesh_pos()
        me = 4 * x + 2 * y + c
        slots[me] = v_ref[...]
        copies = []
        for k in range(1, N_DEV):
            kx, ky, kc = (k >> 2) & 1, (k >> 1) & 1, k & 1
            peer = (1 - x if kx else x, 1 - y if ky else y, 1 - c if kc else c)
            copies.append(pltpu.make_async_remote_copy(
                src_ref=v_ref, dst_ref=slots.at[me], send_sem=send_sems.at[k - 1], recv_sem=recv_sems.at[k - 1],
                device_id=peer, device_id_type=MESH))
        for cp in copies:
            cp.start()
        for cp in copies:
            cp.wait()
        acc = slots[0]
        for d in range(1, N_DEV):
            acc = acc + slots[d]
        out_ref[...] = acc

    vm = pl.BlockSpec(memory_space=pltpu.VMEM)
    return pl.pallas_call(
        body, name="all_reduce_small",
        out_shape=_sds((n, W), F32),
        in_specs=[vm], out_specs=vm,
        scratch_shapes=[pltpu.VMEM((N_DEV, n, W), F32), pltpu.SemaphoreType.DMA((7,)), pltpu.SemaphoreType.DMA((7,))],
    )(v)


def _adamw_math(w, g, m, v):
    m2 = ADAM_B1 * m + (1.0 - ADAM_B1) * g
    v2 = ADAM_B2 * v + (1.0 - ADAM_B2) * (g * g)
    m_hat = m2 / (1.0 - ADAM_B1 ** ADAM_STEP)
    v_hat = v2 / (1.0 - ADAM_B2 ** ADAM_STEP)
    return -ADAM_LR * (m_hat / (jnp.sqrt(v_hat) + ADAM_EPS) + ADAM_WD * w), m2, v2


def _adamw(w, g, m, v):
    L, r, c = w.shape
    tr = _tile(r, max(SUBLANES, (256 * 1024 // c) // SUBLANES * SUBLANES), SUBLANES)

    def body(w_ref, g_ref, m_ref, v_ref, d_ref, nm_ref, nv_ref):
        d_ref[...], nm_ref[...], nv_ref[...] = _adamw_math(w_ref[...], g_ref[...], m_ref[...], v_ref[...])

    spec = pl.BlockSpec((1, tr, c), lambda l, i: (l, i, 0))
    return _call(body, "adamw", (L, r // tr), [spec] * 4, [spec] * 3, [_sds((L, r, c), F32)] * 3)(w, g, m, v)


def _adamw_reduced(w, m, v, flipped, own, b, row_off, tr, l, prev, after):
    L = w.shape[0]
    c, r = w.shape[1:] if flipped else w.shape[:0:-1]
    W = own.shape[1]
    ob = row_off // tr
    extra = list(prev or ()) + list(after)

    def body(w_ref, m_ref, v_ref, own_ref, b_ref, *rest):
        g_ref, d_ref, nm_ref, nv_ref = rest[len(extra):]
        g = ((own_ref[...] + b_ref[0].astype(F32)) + b_ref[1].astype(F32)) + b_ref[2].astype(F32)
        g = g[:, :c].T if flipped else g[:, :c]
        g_ref[0] = g
        d_ref[0], nm_ref[0], nv_ref[0] = _adamw_math(w_ref[0], g, m_ref[0], v_ref[0])

    spec = pl.BlockSpec((1, c, tr), lambda i: (l, 0, i)) if flipped else pl.BlockSpec((1, tr, c), lambda i: (l, i, 0))
    return _call(
        body, "adamw_reduced", (r // tr,),
        [spec] * 3 + [pl.BlockSpec((tr, W), lambda i: (ob + i, 0)), pl.BlockSpec((3, tr, W), lambda i: (0, ob + i, 0))]
        + [ANY] * len(extra),
        [spec] * 4, [_sds(w.shape, F32)] * 4,
        aliases={5 + k: k for k in range(4)} if prev else None,
    )(w, m, v, own, b, *extra)


_MEMBERS = dict(gu=("ffn1_w_gu", "ffn2_w_gu"), dn=("ffn1_w_down", "ffn2_w_down"),
                sq=("w_mla_out", "w_o", "w_ple_gate"), win=("w_in",), c128=("w_conv_out", "w_ple_proj"),
                c256=("w_ukv", "w_uq"))
_GATHER_MEMBERS = dict(_MEMBERS, gu1=("ffn1_w_gu",), gu2=("ffn2_w_gu",), dn1=("ffn1_w_down",), dn2=("ffn2_w_down",))
GATHER_STAGES = (("gu1", "dn1"), ("win", "c256", "c128", "sq"), ("gu2", "dn2"))
_FLIPPED = ("ffn1_w_gu", "ffn2_w_gu", "w_in", "w_uq")
_SMALL = ("ffn1_norm", "mix_norm", "q_norm", "kv_norm", "ffn2_norm", "ple_norm")
_ORDER = ("ffn1_norm", "ffn1_w_gu", "ffn1_w_down", "mix_norm", "w_in", "conv_w", "w_conv_out", "q_norm", "kv_norm",
          "w_uq", "w_ukv", "w_mla_out", "w_o", "ffn2_norm", "ffn2_w_gu", "ffn2_w_down", "ple_norm", "w_ple_gate",
          "w_ple_proj", "final_norm")


def _class_width(wts, cls):
    return HEAD_SLOT if cls == "c256" else wts[_GATHER_MEMBERS[cls][0]].shape[2]


def _pack_rows(vecs, width):
    flat = jnp.concatenate([a.reshape(-1) for a in vecs])
    n = flat.shape[0]
    rows = -(-n // width)
    rows = -(-rows // SUBLANES) * SUBLANES
    flat = jnp.pad(flat, (0, rows * width - n))
    offs, o = [], 0
    for a in vecs:
        offs.append(o)
        o += a.size
    return flat.reshape(rows, width), offs


def _unpack_rows(packed, vecs, offs):
    flat = packed.reshape(-1)
    return [flat[o:o + a.size].reshape(a.shape) for a, o in zip(vecs, offs)]


def _train(x, p, positions, target, gathered, packs, small_w, final_norm, update):
    cs = _rope_tables(positions)
    L = len(small_w)
    h = x
    saved = []
    def gather(l, names, after, collective_id):
        got = _all_gather([packs[n] for n in names], l, after, collective_id)
        return dict(zip(names, got))

    late = None
    if packs is not None:
        first, mixer, second = GATHER_STAGES
        w0 = gather(0, first, [], 0)
        w0.update(gather(0, mixer, [w0[first[0]]], 1))
        gathered = [w0]
        late = lambda h1: gather(0, second, [h1], 2)
    everything = sum(GATHER_STAGES, ())
    for l in range(L):
        h, s = _layer_fwd(h, p[l], cs, gathered[l], small_w[l], late)
        late = None
        saved.append(s)
        if packs is not None and l + 1 < L:
            gathered.append(gather(l + 1, everything, [s["by"]], 2 + l + 1))
    dh, loss, d_final = _final_loss(h, final_norm, target)
    grads, smalls = [None] * L, [None] * L
    exchanged = None
    landing = None

    def second_stage(after):
        l, gs, as_ = exchanged
        owns, ts = _rs_add_chip(gs, as_, [after])
        return l, owns, _rs_ici(ts, l, 2 * L + 2 + l)

    for l in reversed(range(L)):
        dh, part, small = _layer_bwd_late(dh, saved[l], gathered[l], small_w[l], [])
        pin = []
        if exchanged is not None:
            landing = second_stage(dh)
            pin = [landing[1][0]]
        dh, part, small = _layer_bwd_mixer(dh, part, small, saved[l], cs, gathered[l], small_w[l], pin)
        pin = [update(*landing)] if exchanged is not None else []
        dh, g, smalls[l] = _layer_bwd_first(dh, part, small, saved[l], gathered[l], small_w[l], pin)
        if update is not None:
            gs = [g[cls] for cls in CLASSES]
            exchanged = (l, gs, _rs_d2d(gs, l, L + 2 + l))
        else:
            grads[l] = g
    if update is not None:
        update(*second_stage(dh))
    return loss[0, 0], dh, grads, smalls, d_final


def kernel(x, p, positions, ffn1_norm, ffn1_w_gu, ffn1_w_down, mix_norm, w_in, conv_w, w_conv_out, q_norm, kv_norm, w_uq, w_ukv, w_mla_out, w_o, ffn2_norm, ffn2_w_gu, ffn2_w_down, ple_norm, w_ple_gate, w_ple_proj, final_norm, loss_target, m_ffn1_norm, m_ffn1_w_gu, m_ffn1_w_down, m_mix_norm, m_w_in, m_conv_w, m_w_conv_out, m_q_norm, m_kv_norm, m_w_uq, m_w_ukv, m_w_mla_out, m_w_o, m_ffn2_norm, m_ffn2_w_gu, m_ffn2_w_down, m_ple_norm, m_w_ple_gate, m_w_ple_proj, m_final_norm, v_ffn1_norm, v_ffn1_w_gu, v_ffn1_w_down, v_mix_norm, v_w_in, v_conv_w, v_w_conv_out, v_q_norm, v_kv_norm, v_w_uq, v_w_ukv, v_w_mla_out, v_w_o, v_ffn2_norm, v_ffn2_w_gu, v_ffn2_w_down, v_ple_norm, v_w_ple_gate, v_w_ple_proj, v_final_norm):
    args = dict(locals())
    wts = {n: args[n] for n in _ORDER}
    L = w_in.shape[0]
    dev = 4 * lax.axis_index("x") + 2 * lax.axis_index("y") + lax.axis_index("c")

    view = lambda n, a: jnp.swapaxes(a, 1, 2) if n in _FLIPPED else a
    packs = {cls: _pack([view(n, wts[n]) for n in _GATHER_MEMBERS[cls]], [n in _FLIPPED for n in _GATHER_MEMBERS[cls]],
                        _class_width(wts, cls))
             for stage in GATHER_STAGES for cls in stage}
    cw = conv_w.shape[2]
    conv_full = lax.dynamic_update_slice(jnp.zeros((L, 3, N_DEV * cw), F32), conv_w, (0, 0, dev * cw))
    conv_packed, conv_offs = _pack_rows([conv_full], FLAT_COLS)
    conv_full = _unpack_rows(_all_reduce_small(conv_packed), [conv_full], conv_offs)[0]
    small_w = [dict({n: wts[n][l][None, :] for n in _SMALL}, conv_w=conv_full[l]) for l in range(L)]

    done = {}

    def update(l, owns, bs):
        for q, cls in enumerate(CLASSES):
            off = 0
            rows = [wts[n].shape[1] for n in _MEMBERS[cls]]
            tr = _tile(math.gcd(*rows), 256, BF16_ROWS)
            for n, r in zip(_MEMBERS[cls], rows):
                done[n] = _adamw_reduced(view(n, wts[n]), view(n, args["m_" + n]), view(n, args["v_" + n]),
                                         n in _FLIPPED, owns[q], bs[q], off, tr, l, done.get(n), [])
                off += r
        return done[_MEMBERS[CLASSES[-1]][-1]][0]

    loss_dev, grad_x, _, smalls, d_final = _train(x[0], p[:, 0], positions[0], loss_target[0], None, packs, small_w,
                                                  final_norm[None, :], update)

    small = [jnp.stack([smalls[l][n][0] for l in range(L)]) for n in _SMALL]
    small += [jnp.stack([smalls[l]["conv_w"] for l in range(L)]), d_final[0], loss_dev[None]]
    packed, offs = _pack_rows(small, FLAT_COLS)
    small = _unpack_rows(_all_reduce_small(packed), small, offs)
    grad = dict(zip(_SMALL, small))
    grad["conv_w"] = lax.dynamic_slice(small[len(_SMALL)], (0, 0, dev * cw), (L, 3, cw))
    grad["final_norm"] = small[-2]
    loss = small[-1][0]

    deltas, new_m, new_v = {}, {}, {}
    for n, outs in done.items():
        grad[n], deltas[n], new_m[n], new_v[n] = (view(n, a) for a in outs)
    for n in _SMALL + ("conv_w", "final_norm"):
        w3 = wts[n].reshape((1,) * (3 - wts[n].ndim) + wts[n].shape)
        d, nm, nv = _adamw(w3, grad[n].reshape(w3.shape), args["m_" + n].reshape(w3.shape),
                           args["v_" + n].reshape(w3.shape))
        deltas[n], new_m[n], new_v[n] = (a.reshape(wts[n].shape) for a in (d, nm, nv))
    return (loss, grad_x[None], *[grad[n] for n in _ORDER], *[deltas[n] for n in _ORDER],
            *[new_m[n] for n in _ORDER], *[new_v[n] for n in _ORDER])
```

```python
import functools
import math

import jax
import jax.numpy as jnp
from jax import lax
from jax.experimental import pallas as pl
from jax.experimental.pallas import tpu as pltpu
from jax.experimental.pallas import tpu_sc as plsc

F32 = jnp.float32
BF16 = jnp.bfloat16

CHUNK = 64
NOPE = 128
ROPE = 64
VDIM = 128
ROPE_THETA = 10000.0
EPS = 1e-6
ATTN_SCALE = (NOPE + ROPE) ** -0.5
SCORE_SCALE = ATTN_SCALE * math.log2(math.e)
LN2 = math.log(2.0)
ADAM_LR = 0.001
ADAM_B1 = 0.9
ADAM_B2 = 0.999
ADAM_EPS = 1e-08
ADAM_WD = 0.01
ADAM_STEP = 10

LANES = 128
SUBLANES = 8
BF16_ROWS = 16
V7X_VMEM_BYTES = 64 * 1024 * 1024
VMEM_LIMIT = V7X_VMEM_BYTES * 7 // 8
HEAD_SLOT = 2 * LANES
N_DEV = 8
ATTN_FWD_WIDTH = 4
ATTN_BWD_WIDTH = 2
FLAT_COLS = 1024
CLASSES = ("gu", "dn", "sq", "win", "c128", "c256")

NT = (((1,), (1,)), ((), ()))
MESH = pl.DeviceIdType.MESH
ANY = pl.BlockSpec(memory_space=pl.ANY)


def _dot(a, b):
    return jnp.dot(a, b, preferred_element_type=F32)


def _dot_nt(a, b):
    return lax.dot_general(a, b, NT, preferred_element_type=F32)


def _sig(x):
    return 1.0 / (1.0 + jnp.exp(-x))


def _tile(n, pref, unit):
    if n <= pref:
        return n
    t = (pref // unit) * unit
    while t >= unit:
        if n % t == 0:
            return t
        t -= unit
    return n


def _call(body, name, grid, in_specs, out_specs, out_shape, scratch=(), aliases=None):
    return pl.pallas_call(
        body,
        name=name,
        grid=grid,
        in_specs=in_specs,
        out_specs=out_specs,
        out_shape=out_shape,
        scratch_shapes=list(scratch),
        input_output_aliases=aliases or {},
        compiler_params=pltpu.CompilerParams(
            dimension_semantics=("arbitrary",) * len(grid), vmem_limit_bytes=VMEM_LIMIT
        ),
    )


def _sds(shape, dtype):
    return jax.ShapeDtypeStruct(shape, dtype)


def _rms_fwd(x, gain):
    rstd = lax.rsqrt(jnp.mean(x * x, axis=-1, keepdims=True) + EPS)
    return x * rstd * gain, rstd


def _rms_bwd(dn, x, rstd, gain):
    xhat = x * rstd
    dgy = dn * gain
    dx = rstd * (dgy - xhat * jnp.mean(dgy * xhat, axis=-1, keepdims=True))
    return dx, jnp.sum(dn * xhat, axis=0, keepdims=True)


def _rows(tm, w):
    return pl.BlockSpec((tm, w), lambda i: (i, 0))


def _whole(a):
    nd = a.ndim
    return pl.BlockSpec(a.shape, lambda i: (0,) * nd, pipeline_mode=pl.Buffered(1))


def _slab(buf, rows, index):
    return pl.BlockSpec((N_DEV, rows, buf.shape[2]), lambda i: (0, index, 0), pipeline_mode=pl.Buffered(1))


def _cat_slots(w):
    return jnp.concatenate([w[d] for d in range(N_DEV)], axis=1)


def _ffn_up(h, gain, gu_w, which):
    S, D = h.shape
    c = gu_w.shape[2]
    tm = _tile(S, 512, SUBLANES)
    nb = N_DEV // 2

    def body(h_ref, gain_ref, w_ref, jac_ref, a_ref, at_ref, n_ref, r_ref):
        n32, rstd = _rms_fwd(h_ref[...], gain_ref[...])
        n = n32.astype(BF16)
        n_ref[...] = n
        r_ref[...] = rstd
        for d in range(nb):
            g = _dot(n, w_ref[d])
            u = _dot(n, w_ref[nb + d])
            sg = _sig(g)
            silu = g * sg
            a = (silu * u).astype(BF16)
            a_ref[d] = a
            at_ref[d] = a.T
            jac_ref[d] = (0.5 * u * (sg + silu * (1.0 - sg))).astype(BF16)
            jac_ref[nb + d] = (0.5 * silu).astype(BF16)

    return _call(
        body, "ffn_up", (S // tm,),
        [_rows(tm, D), _whole(gain), _slab(gu_w, D, which)],
        [pl.BlockSpec((N_DEV, tm, c), lambda i: (0, i, 0)), pl.BlockSpec((nb, tm, c), lambda i: (0, i, 0)),
         pl.BlockSpec((nb, c, tm), lambda i: (0, 0, i)), _rows(tm, D), _rows(tm, 1)],
        [_sds((N_DEV, S, c), BF16), _sds((nb, S, c), BF16), _sds((nb, c, S), BF16), _sds((S, D), BF16),
         _sds((S, 1), F32)],
    )(h, gain, gu_w)


def _down_weight(w_ref, d, c):
    return w_ref[2 * d:2 * d + 2].reshape(c, w_ref.shape[2])


def _ffn_down(a, dn_w, which, h):
    nb, S, c = a.shape
    D = h.shape[1]
    tm = _tile(S, 1024, SUBLANES)

    def body(a_ref, w_ref, h_ref, o_ref):
        acc = _dot(a_ref[0], _down_weight(w_ref, 0, c))
        for d in range(1, nb):
            acc = acc + _dot(a_ref[d], _down_weight(w_ref, d, c))
        o_ref[...] = h_ref[...] + 0.5 * acc

    return _call(
        body, "ffn_down", (S // tm,),
        [pl.BlockSpec((nb, tm, c), lambda i: (0, i, 0)), _slab(dn_w, c // 2, which), _rows(tm, D)],
        _rows(tm, D),
        _sds((S, D), F32),
    )(a, dn_w, h)


def _win_segments(C, QL, KVL, D):
    o1, o2 = 3 * C, 3 * C + QL + KVL + ROPE
    return [("bcv", k, k * C, (k + 1) * C) for k in range(3)] + [("qkr", None, o1, o2), ("gg", None, o2, o2 + 2 * D)]


def _win_pieces(segments, cw):
    out = []
    for tgt, lead, a, b in segments:
        for d in range(N_DEV):
            lo, hi = max(a, d * cw), min(b, (d + 1) * cw)
            if lo < hi:
                out.append((tgt, lead, d, (lo - d * cw, hi - d * cw), (lo - a, hi - a)))
    return out


def _win_split(win_w, C, QL, KVL):
    _, D, cw = win_w.shape
    WQ = QL + KVL + LANES
    pieces = _win_pieces(_win_segments(C, QL, KVL, D), cw)
    tr = _tile(D, 256, BF16_ROWS)

    def body(w_ref, bcv_ref, qkr_ref, gg_ref):
        tgt = dict(bcv=bcv_ref, qkr=qkr_ref, gg=gg_ref)
        qkr_ref[:, QL + KVL + ROPE:] = jnp.zeros((tr, LANES - ROPE), BF16)
        for name, lead, d, (s0, s1), (t0, t1) in pieces:
            v = w_ref[d, :, s0:s1]
            if lead is None:
                tgt[name][:, t0:t1] = v
            else:
                tgt[name][lead, :, t0:t1] = v

    return _call(
        body, "win_split", (D // tr,),
        [pl.BlockSpec((N_DEV, tr, cw), lambda i: (0, i, 0))],
        [pl.BlockSpec((3, tr, C), lambda i: (0, i, 0)), _rows(tr, WQ), _rows(tr, 2 * D)],
        [_sds((3, D, C), BF16), _sds((D, WQ), BF16), _sds((D, 2 * D), BF16)],
    )(win_w)


def _win_merge(d_bcv, d_qkr, d_gg, cw):
    _, D, C = d_bcv.shape
    WQ = d_qkr.shape[1]
    QL_KVL = WQ - LANES
    o1 = 3 * C
    segments = [("bcv", k, k * C, (k + 1) * C) for k in range(3)]
    segments += [("qkr", None, o1, o1 + QL_KVL + ROPE), ("gg", None, o1 + QL_KVL + ROPE, o1 + QL_KVL + ROPE + 2 * D)]
    pieces = _win_pieces(segments, cw)
    tr = _tile(D, 256, BF16_ROWS)

    def body(bcv_ref, qkr_ref, gg_ref, o_ref):
        src = dict(bcv=bcv_ref, qkr=qkr_ref, gg=gg_ref)
        for name, lead, d, (s0, s1), (t0, t1) in pieces:
            v = src[name][:, t0:t1] if lead is None else src[name][lead, :, t0:t1]
            o_ref[d, :, s0:s1] = v.astype(BF16)

    return _call(
        body, "win_merge", (D // tr,),
        [pl.BlockSpec((3, tr, C), lambda i: (0, i, 0)), _rows(tr, WQ), _rows(tr, 2 * D)],
        pl.BlockSpec((N_DEV, tr, cw), lambda i: (0, i, 0)),
        _sds((N_DEV, D, cw), BF16),
    )(d_bcv, d_qkr, d_gg)


def _mix_in(h, gain, w_bcv, w_qkr, w_gg):
    S, D = h.shape
    C = w_bcv.shape[2]
    tm = _tile(S, 512, SUBLANES)

    def body(h_ref, gain_ref, w1, w2, w3, o1, o2, o3, n_ref, r_ref):
        n32, rstd = _rms_fwd(h_ref[...], gain_ref[...])
        n = n32.astype(BF16)
        n_ref[...] = n
        r_ref[...] = rstd
        for k in range(3):
            o1[k] = _dot(n, w1[k]).astype(BF16)
        o2[...] = _dot(n, w2[...])
        o3[...] = _dot(n, w3[...]).astype(BF16)

    return _call(
        body, "mix_in", (S // tm,),
        [_rows(tm, D), _whole(gain), _whole(w_bcv), _whole(w_qkr), _whole(w_gg)],
        [pl.BlockSpec((3, tm, C), lambda i: (0, i, 0)), _rows(tm, w_qkr.shape[1]), _rows(tm, 2 * D),
         _rows(tm, D), _rows(tm, 1)],
        [_sds((3, S, C), BF16), _sds((S, w_qkr.shape[1]), F32), _sds((S, 2 * D), BF16), _sds((S, D), BF16),
         _sds((S, 1), F32)],
    )(h, gain, w_bcv, w_qkr, w_gg)


def _conv_taps(zc):
    rows = lax.broadcasted_iota(jnp.int32, zc.shape, 0)
    z1 = jnp.where(rows >= 1, pltpu.roll(zc, 1, 0), 0.0)
    z2 = jnp.where(rows >= 2, pltpu.roll(zc, 2, 0), 0.0)
    return z1, z2


def _conv_fwd(z_bcv, conv_w):
    _, S, C = z_bcv.shape

    def body(z_ref, w_ref, o_ref):
        w = w_ref[...]
        zc = z_ref[1].astype(F32) * z_ref[2].astype(F32)
        z1, z2 = _conv_taps(zc)
        y = w[0:1] * z2 + w[1:2] * z1 + w[2:3] * zc
        o_ref[...] = (z_ref[0].astype(F32) * y).astype(BF16)

    return _call(
        body, "conv_fwd", (C // LANES,),
        [pl.BlockSpec((3, S, LANES), lambda j: (0, 0, j)), pl.BlockSpec((3, LANES), lambda j: (0, j))],
        pl.BlockSpec((S, LANES), lambda j: (0, j)),
        _sds((S, C), BF16),
    )(z_bcv, conv_w)


def _rope(x, cs, half):
    c, s1, s2 = cs[:, :LANES], cs[:, LANES:2 * LANES], cs[:, 2 * LANES:]
    return x * c + pltpu.roll(x, LANES - half, 1) * s1 + pltpu.roll(x, half, 1) * s2


def _unrope(d, cs, half):
    c, s1, s2 = cs[:, :LANES], cs[:, LANES:2 * LANES], cs[:, 2 * LANES:]
    return d * c + pltpu.roll(d * s1, half, 1) + pltpu.roll(d * s2, LANES - half, 1)


def _mla_prep(z_qkr, gq, gkv, cs, c256_w):
    S = z_qkr.shape[0]
    QL, KVL = gq.shape[1], gkv.shape[1]
    H = N_DEV
    tm = _tile(S, 512, SUBLANES)
    half = ROPE // 2

    def body(z_ref, gq_ref, gkv_ref, cs_ref, w_ref, q_ref, k_ref, v_ref, qn_ref, kvn_ref, rq_ref, rkv_ref):
        z = z_ref[...]
        cs_t = cs_ref[...]
        qn32, rq = _rms_fwd(z[:, :QL], gq_ref[...])
        kvn32, rkv = _rms_fwd(z[:, QL:QL + KVL], gkv_ref[...])
        qn = qn32.astype(BF16)
        kvn = kvn32.astype(BF16)
        qn_ref[...] = qn
        kvn_ref[...] = kvn
        rq_ref[...] = rq
        rkv_ref[...] = rkv
        krope = _rope(z[:, QL + KVL:], cs_t, half).astype(BF16)
        for h in range(H):
            lo, mid, hi = h * HEAD_SLOT, h * HEAD_SLOT + LANES, (h + 1) * HEAD_SLOT
            q = _dot(qn, w_ref[h, KVL:KVL + QL, :])
            kv = _dot(kvn, w_ref[h, 0:KVL, :])
            q_ref[:, lo:mid] = (q[:, :LANES] * SCORE_SCALE).astype(BF16)
            q_ref[:, mid:hi] = (_rope(q[:, LANES:], cs_t, half) * SCORE_SCALE).astype(BF16)
            k_ref[:, lo:mid] = kv[:, :LANES].astype(BF16)
            k_ref[:, mid:hi] = krope
            v_ref[:, h * VDIM:(h + 1) * VDIM] = kv[:, LANES:].astype(BF16)

    return _call(
        body, "mla_prep", (S // tm,),
        [_rows(tm, z_qkr.shape[1]), _whole(gq), _whole(gkv), _rows(tm, 3 * LANES), _whole(c256_w)],
        [_rows(tm, H * HEAD_SLOT), _rows(tm, H * HEAD_SLOT), _rows(tm, H * VDIM), _rows(tm, QL), _rows(tm, KVL),
         _rows(tm, 1), _rows(tm, 1)],
        [_sds((S, H * HEAD_SLOT), BF16), _sds((S, H * HEAD_SLOT), BF16), _sds((S, H * VDIM), BF16),
         _sds((S, QL), BF16), _sds((S, KVL), BF16), _sds((S, 1), F32), _sds((S, 1), F32)],
    )(z_qkr, gq, gkv, cs, c256_w)


def _chunk_mask(rows, cols, diagonal_row):
    shift = CHUNK.bit_length() - 1
    krow = (lax.broadcasted_iota(jnp.int32, (rows, cols), 0) - diagonal_row) >> shift
    qcol = lax.broadcasted_iota(jnp.int32, (rows, cols), 1) >> shift
    return krow <= qcol


def _attn_fwd(q, k, v, H):
    S = q.shape[0]
    t = _tile(S, 512, CHUNK)
    nq = S // t

    def body(q_ref, k_ref, v_ref, o_ref, lse_ref, vt_ref):
        qi = pl.program_id(1)

        @pl.when(qi == 0)
        def _():
            vt_ref[0:VDIM, :] = v_ref[...].T
            vt_ref[VDIM:, :] = jnp.ones((BF16_ROWS, S), BF16)

        qv = q_ref[...]

        def block(start, width, carry, masked):
            m, acc = carry
            off = pl.multiple_of(start * t, t)
            s = _dot_nt(k_ref[pl.ds(off, width * t), :], qv)
            if masked:
                s = jnp.where(_chunk_mask(width * t, t, (width - 1) * t), s, -1e30)
            m_new = jnp.maximum(m, jnp.max(s, axis=0, keepdims=True))
            p = jnp.exp2(s - m_new).astype(BF16)
            acc = jnp.exp2(m - m_new) * acc + _dot(vt_ref[:, pl.ds(off, width * t)], p)
            return m_new, acc

        init = (jnp.full((1, t), -1e30, F32), jnp.zeros((VDIM + BF16_ROWS, t), F32))
        wide = lax.div(qi, ATTN_FWD_WIDTH)
        carry = lax.fori_loop(0, wide, lambda j, c: block(j * ATTN_FWD_WIDTH, ATTN_FWD_WIDTH, c, False), init)
        left = qi - wide * ATTN_FWD_WIDTH
        for extra in range(ATTN_FWD_WIDTH):
            @pl.when(left == extra)
            def _():
                m, acc = block(qi - extra, extra + 1, carry, True)
                l = acc[VDIM:VDIM + 1]
                o_ref[...] = (acc[0:VDIM] * (1.0 / l)).T.astype(BF16)
                lse_ref[0] = jnp.broadcast_to(m + jnp.log2(l), (SUBLANES, t))

    return _call(
        body, "attn_fwd", (H, nq),
        [pl.BlockSpec((t, HEAD_SLOT), lambda h, i: (i, h)), pl.BlockSpec((S, HEAD_SLOT), lambda h, i: (0, h)),
         pl.BlockSpec((S, VDIM), lambda h, i: (0, h))],
        [pl.BlockSpec((t, VDIM), lambda h, i: (i, h)), pl.BlockSpec((1, SUBLANES, t), lambda h, i: (h, 0, i))],
        [_sds((S, H * VDIM), BF16), _sds((H, SUBLANES, S), F32)],
        [pltpu.VMEM((VDIM + BF16_ROWS, S), BF16)],
    )(q, k, v)


def _merge_wo(o, by, z_gg, h, sq_w, c128_w):
    S, D = h.shape
    C = by.shape[1]
    r = sq_w.shape[1] // 3
    tm = _tile(S, 512, SUBLANES)

    def body(o_ref, by_ref, gg_ref, h_ref, wmo_ref, wo_ref, wco_ref, h2_ref, mg_ref, yc_ref, ym_ref):
        ymla = _dot(o_ref[...], wmo_ref[...].reshape(N_DEV * r, D))
        yconv = _dot(by_ref[...], _cat_slots(wco_ref))
        gg = gg_ref[...].astype(F32)
        merged = (_sig(gg[:, :D]) * yconv + _sig(gg[:, D:]) * ymla).astype(BF16)
        mg_ref[...] = merged
        yc_ref[...] = yconv.astype(BF16)
        ym_ref[...] = ymla.astype(BF16)
        h2_ref[...] = h_ref[...] + _dot(merged, wo_ref[...].reshape(N_DEV * r, D))

    return _call(
        body, "merge_wo", (S // tm,),
        [_rows(tm, o.shape[1]), _rows(tm, C), _rows(tm, 2 * D), _rows(tm, D), _slab(sq_w, r, 0), _slab(sq_w, r, 1),
         _slab(c128_w, C, 0)],
        [_rows(tm, D)] * 4,
        [_sds((S, D), F32)] + [_sds((S, D), BF16)] * 3,
    )(o, by, z_gg, h, sq_w, sq_w, c128_w)


def _ple_fwd(h, gain, p, sq_w, c128_w, C):
    S, D = h.shape
    P = p.shape[1]
    r = sq_w.shape[1] // 3
    tm = _tile(S, 512, SUBLANES)

    def body(h_ref, gain_ref, p_ref, wpg_ref, wpp_ref, o_ref, pre_ref, pp_ref, n_ref, r_ref):
        x = h_ref[...]
        n32, rstd = _rms_fwd(x, gain_ref[...])
        n = n32.astype(BF16)
        n_ref[...] = n
        r_ref[...] = rstd
        pre = _dot(n, wpg_ref[...].reshape(N_DEV * r, D))
        pp = _dot(p_ref[...].astype(BF16), _cat_slots(wpp_ref))
        pre_ref[...] = pre.astype(BF16)
        pp_ref[...] = pp.astype(BF16)
        o_ref[...] = x + _sig(pre) * pp

    return _call(
        body, "ple_fwd", (S // tm,),
        [_rows(tm, D), _whole(gain), _rows(tm, P), _slab(sq_w, r, 2), _slab(c128_w, P, C // P)],
        [_rows(tm, D), _rows(tm, D), _rows(tm, D), _rows(tm, D), _rows(tm, 1)],
        [_sds((S, D), F32)] + [_sds((S, D), BF16)] * 3 + [_sds((S, 1), F32)],
    )(h, gain, p, sq_w, c128_w)


def _final_loss(h, gain, target):
    S, D = h.shape
    tm = _tile(S, 512, SUBLANES)

    def body(h_ref, gain_ref, t_ref, dh_ref, loss_ref, dg_ref):
        @pl.when(pl.program_id(0) == 0)
        def _():
            loss_ref[...] = jnp.zeros_like(loss_ref)
            dg_ref[...] = jnp.zeros_like(dg_ref)

        x = h_ref[...]
        gain_v = gain_ref[...]
        y, rstd = _rms_fwd(x, gain_v)
        err = y - t_ref[...]
        loss_ref[...] += 0.5 * jnp.sum(jnp.mean(err * err, axis=-1, keepdims=True))
        dx, dgain = _rms_bwd(err * (1.0 / D), x, rstd, gain_v)
        dh_ref[...] = dx
        dg_ref[...] += dgain

    return _call(
        body, "final_loss", (S // tm,),
        [_rows(tm, D), _whole(gain), _rows(tm, D)],
        [_rows(tm, D), pl.BlockSpec((1, LANES), lambda i: (0, 0)), pl.BlockSpec((1, D), lambda i: (0, 0))],
        [_sds((S, D), F32), _sds((1, LANES), F32), _sds((1, D), F32)],
    )(h, gain, target)


def _tn_call(body, name, grid, in_specs, out_spec, out_shape, scratch, operands, prev):
    n = len(operands)
    if prev is None:
        return _call(body, name, grid, in_specs, out_spec, out_shape, scratch)(*operands)
    assert prev.shape == out_shape.shape and prev.dtype == out_shape.dtype

    def wrapped(*refs):
        body(*refs[:n], *refs[n + 1:])

    return _call(wrapped, name, grid, in_specs + [ANY], out_spec, out_shape, scratch, {n: 0})(*operands, prev)


def _transposed(x_ref, xt_ref, first):
    @pl.when(first)
    def _():
        xt_ref[...] = x_ref[...].astype(BF16).T


def _tn_slots(x, dy, prev, rows_total, row_off):
    S, K = x.shape
    B, _, c = dy.shape
    tk = _tile(K, 1024, LANES)

    def body(x_ref, dy_ref, o_ref, xt_ref):
        _transposed(x_ref, xt_ref, pl.program_id(1) == 0)
        o_ref[0] = _dot(xt_ref[...], dy_ref[0]).astype(BF16)

    return _tn_call(
        body, "tn_slots", (K // tk, B),
        [pl.BlockSpec((S, tk), lambda i, b: (0, i)), pl.BlockSpec((1, S, c), lambda i, b: (b, 0, 0))],
        pl.BlockSpec((1, tk, c), lambda i, b: (b, row_off // tk + i, 0)),
        _sds((B, rows_total, c), BF16), [pltpu.VMEM((tk, S), BF16)], [x, dy], prev)


def _tn_plain(x, dy, out_dtype=F32):
    S, K = x.shape
    B, _, c = dy.shape
    tk = _tile(K, 512, LANES)
    tn = _tile(c, 1024, LANES)

    def body(x_ref, dy_ref, o_ref, xt_ref):
        _transposed(x_ref, xt_ref, (pl.program_id(1) == 0) & (pl.program_id(2) == 0))
        o_ref[0] = _dot(xt_ref[...], dy_ref[0]).astype(out_dtype)

    return _call(
        body, "tn_plain", (K // tk, B, c // tn),
        [pl.BlockSpec((S, tk), lambda i, b, j: (0, i)), pl.BlockSpec((1, S, tn), lambda i, b, j: (b, 0, j))],
        pl.BlockSpec((1, tk, tn), lambda i, b, j: (b, i, j)),
        _sds((B, K, c), out_dtype), [pltpu.VMEM((tk, S), BF16)],
    )(x, dy)


def _tn_down(at, dh, prev, rows_total, which):
    nb, c, S = at.shape
    D = dh.shape[1]
    r = c // 2
    tn = _tile(D, 512, LANES)

    def body(at_ref, dh_ref, o_ref):
        g = 0.5 * _dot(at_ref[0], dh_ref[...].astype(BF16))
        o_ref[...] = g.astype(BF16).reshape(2, r, tn)

    return _tn_call(
        body, "tn_down", (D // tn, nb),
        [pl.BlockSpec((1, c, S), lambda j, i: (i, 0, 0)), pl.BlockSpec((S, tn), lambda j, i: (0, j))],
        pl.BlockSpec((2, r, tn), lambda j, i: (i, which, j)),
        _sds((N_DEV, rows_total, D), BF16), [], [at, dh], prev)


def _tn_square(x, dy, prev, rows_total, member):
    S, K = x.shape
    N = dy.shape[1]
    r = K // N_DEV
    tk = _tile(K, 512, r)
    tn = _tile(N, 512, LANES)

    def body(x_ref, dy_ref, o_ref, xt_ref):
        _transposed(x_ref, xt_ref, pl.program_id(1) == 0)
        g = _dot(xt_ref[...], dy_ref[...].astype(BF16))
        o_ref[...] = g.astype(BF16).reshape(tk // r, r, tn)

    return _tn_call(
        body, "tn_square", (K // tk, N // tn),
        [pl.BlockSpec((S, tk), lambda i, j: (0, i)), pl.BlockSpec((S, tn), lambda i, j: (0, j))],
        pl.BlockSpec((tk // r, r, tn), lambda i, j: (i, member, j)),
        _sds((N_DEV, rows_total, N), BF16), [pltpu.VMEM((tk, S), BF16)], [x, dy], prev)


def _tn_cols(x, dy, prev, rows_total, row_block):
    S, K = x.shape
    N = dy.shape[1]
    cw = N // N_DEV

    def body(x_ref, dy_ref, o_ref):
        g = _dot(x_ref[...].astype(BF16).T, dy_ref[...])
        for d in range(N_DEV):
            o_ref[d] = g[:, d * cw:(d + 1) * cw].astype(BF16)

    return _tn_call(
        body, "tn_cols", (1,),
        [pl.BlockSpec((S, K), lambda i: (0, 0)), pl.BlockSpec((S, N), lambda i: (0, 0))],
        pl.BlockSpec((N_DEV, K, cw), lambda i: (0, row_block, 0)),
        _sds((N_DEV, rows_total, cw), BF16), [], [x, dy], prev)


def _tn_heads(qn, kvn, dqp, dkv):
    S, QL = qn.shape
    KVL = kvn.shape[1]

    def body(qn_ref, kvn_ref, dq_ref, dkv_ref, o_ref):
        o_ref[0, 0:KVL, :] = _dot(kvn_ref[...].T, dkv_ref[...]).astype(BF16)
        o_ref[0, KVL:KVL + QL, :] = _dot(qn_ref[...].T, dq_ref[...]).astype(BF16)

    head = pl.BlockSpec((S, HEAD_SLOT), lambda h: (0, h))
    return _call(
        body, "tn_heads", (N_DEV,),
        [pl.BlockSpec((S, QL), lambda h: (0, 0)), pl.BlockSpec((S, KVL), lambda h: (0, 0)), head, head],
        pl.BlockSpec((1, KVL + QL, HEAD_SLOT), lambda h: (h, 0, 0)),
        _sds((N_DEV, KVL + QL, HEAD_SLOT), BF16),
    )(qn, kvn, dqp, dkv)


def _ple_bwd(dh, pre, pp, h, rstd, gain, sq_w, after):
    S, D = h.shape
    r = sq_w.shape[1] // 3
    tm = _tile(S, 512, SUBLANES)

    def body(dh_ref, pre_ref, pp_ref, h_ref, r_ref, gain_ref, wpg_ref, *rest):
        o_ref, dpre_ref, dpp_ref, dg_ref = rest[len(after):]

        @pl.when(pl.program_id(0) == 0)
        def _():
            dg_ref[...] = jnp.zeros_like(dg_ref)

        d = dh_ref[...]
        gate = _sig(pre_ref[...].astype(F32))
        dpre = (d * pp_ref[...].astype(F32) * gate * (1.0 - gate)).astype(BF16)
        dpre_ref[...] = dpre
        dpp_ref[...] = (d * gate).astype(BF16)
        dn = _dot_nt(dpre, wpg_ref[...].reshape(N_DEV * r, D))
        dx, dgain = _rms_bwd(dn, h_ref[...], r_ref[...], gain_ref[...])
        o_ref[...] = d + dx
        dg_ref[...] += dgain

    return _call(
        body, "ple_bwd", (S // tm,),
        [_rows(tm, D), _rows(tm, D), _rows(tm, D), _rows(tm, D), _rows(tm, 1), _whole(gain), _slab(sq_w, r, 2)]
        + [ANY] * len(after),
        [_rows(tm, D), _rows(tm, D), _rows(tm, D), pl.BlockSpec((1, D), lambda i: (0, 0))],
        [_sds((S, D), F32), _sds((S, D), BF16), _sds((S, D), BF16), _sds((1, D), F32)],
    )(dh, pre, pp, h, rstd, gain, sq_w, *after)


def _ffn_bwd_act(dh, dn_w, which, jac, after=()):
    S, D = dh.shape
    _, _, c = jac.shape
    nb = N_DEV // 2
    tm = _tile(S, 512, SUBLANES)

    def body(dh_ref, w_ref, jac_ref, *rest):
        dgu_ref = rest[len(after)]
        dhb = dh_ref[...].astype(BF16)
        for d in range(nb):
            da = _dot_nt(dhb, _down_weight(w_ref, d, c))
            dgu_ref[d] = (da * jac_ref[d].astype(F32)).astype(BF16)
            dgu_ref[nb + d] = (da * jac_ref[nb + d].astype(F32)).astype(BF16)

    act = pl.BlockSpec((N_DEV, tm, c), lambda i: (0, i, 0))
    return _call(
        body, "ffn_bwd_act", (S // tm,),
        [_rows(tm, D), _slab(dn_w, c // 2, which), act] + [ANY] * len(after),
        act,
        _sds((N_DEV, S, c), BF16),
    )(dh, dn_w, jac, *after)


def _ffn_bwd_in(dgu, gu_w, which, h, rstd, gain, dh):
    S, D = h.shape
    c = dgu.shape[2]
    tm = _tile(S, 512, SUBLANES)

    def body(dgu_ref, w_ref, h_ref, r_ref, gain_ref, dh_ref, o_ref, dgain_ref):
        @pl.when(pl.program_id(0) == 0)
        def _():
            dgain_ref[...] = jnp.zeros_like(dgain_ref)

        dn = _dot_nt(dgu_ref[0], w_ref[0])
        for d in range(1, N_DEV):
            dn = dn + _dot_nt(dgu_ref[d], w_ref[d])
        dx, dgain = _rms_bwd(dn, h_ref[...], r_ref[...], gain_ref[...])
        o_ref[...] = dh_ref[...] + dx
        dgain_ref[...] += dgain

    return _call(
        body, "ffn_bwd_in", (S // tm,),
        [pl.BlockSpec((N_DEV, tm, c), lambda i: (0, i, 0)), _slab(gu_w, D, which), _rows(tm, D), _rows(tm, 1),
         _whole(gain), _rows(tm, D)],
        [_rows(tm, D), pl.BlockSpec((1, D), lambda i: (0, 0))],
        [_sds((S, D), F32), _sds((1, D), F32)],
    )(dgu, gu_w, h, rstd, gain, dh)


def _merge_bwd(dh, z_gg, yconv, ymla, o, sq_w, c128_w, C, after):
    S, D = dh.shape
    r = sq_w.shape[1] // 3
    HV = N_DEV * r
    H = HV // VDIM
    tm = _tile(S, 512, SUBLANES)

    def head_rows():
        row = lax.broadcasted_iota(jnp.int32, (SUBLANES * H, HV), 0) >> (SUBLANES.bit_length() - 1)
        col = lax.broadcasted_iota(jnp.int32, (SUBLANES * H, HV), 1) >> (VDIM.bit_length() - 1)
        return jnp.where(row == col, 1.0, 0.0).astype(BF16)

    def body(dh_ref, gg_ref, yc_ref, ym_ref, o_ref, wmo_ref, wo_ref, wco_ref, *rest):
        dgg_ref, dby_ref, do_ref, dyc_ref, dym_ref, dl_ref = rest[len(after):]
        dm = _dot_nt(dh_ref[...].astype(BF16), wo_ref[...].reshape(HV, D))
        gg = gg_ref[...].astype(F32)
        sgc = _sig(gg[:, :D])
        sgm = _sig(gg[:, D:])
        dyc = (dm * sgc).astype(BF16)
        dym = (dm * sgm).astype(BF16)
        dyc_ref[...] = dyc
        dym_ref[...] = dym
        dgg_ref[:, :D] = (dm * yc_ref[...].astype(F32) * sgc * (1.0 - sgc)).astype(BF16)
        dgg_ref[:, D:] = (dm * ym_ref[...].astype(F32) * sgm * (1.0 - sgm)).astype(BF16)
        dby_ref[...] = _dot_nt(dyc, _cat_slots(wco_ref)).astype(BF16)
        do = _dot_nt(dym, wmo_ref[...].reshape(HV, D)).astype(BF16)
        do_ref[...] = do
        prod = do.astype(F32) * o_ref[...].astype(F32)
        hi = prod.astype(BF16)
        lo = (prod - hi.astype(F32)).astype(BF16)
        pick = head_rows()
        dl_ref[...] = _dot_nt(pick, hi) + _dot_nt(pick, lo)

    return _call(
        body, "merge_bwd", (S // tm,),
        [_rows(tm, D), _rows(tm, 2 * D), _rows(tm, D), _rows(tm, D), _rows(tm, HV), _slab(sq_w, r, 0),
         _slab(sq_w, r, 1), _slab(c128_w, C, 0)] + [ANY] * len(after),
        [_rows(tm, 2 * D), _rows(tm, C), _rows(tm, HV), _rows(tm, D), _rows(tm, D),
         pl.BlockSpec((SUBLANES * H, tm), lambda i: (0, i))],
        [_sds((S, 2 * D), BF16), _sds((S, C), BF16), _sds((S, HV), BF16), _sds((S, D), BF16), _sds((S, D), BF16),
         _sds((SUBLANES * H, S), F32)],
    )(dh, z_gg, yconv, ymla, o, sq_w, sq_w, c128_w, *after)


def _conv_bwd(z_bcv, conv_w, dby):
    _, S, C = z_bcv.shape

    def body(z_ref, w_ref, dby_ref, dz_ref, dw_ref):
        w = w_ref[...]
        c = z_ref[1].astype(F32)
        v = z_ref[2].astype(F32)
        d = dby_ref[...].astype(F32)
        zc = c * v
        z1, z2 = _conv_taps(zc)
        y = w[0:1] * z2 + w[1:2] * z1 + w[2:3] * zc
        dz_ref[0] = (d * y).astype(BF16)
        dy = d * z_ref[0].astype(F32)
        rows = lax.broadcasted_iota(jnp.int32, dy.shape, 0)
        dy1 = jnp.where(rows < S - 1, pltpu.roll(dy, S - 1, 0), 0.0)
        dy2 = jnp.where(rows < S - 2, pltpu.roll(dy, S - 2, 0), 0.0)
        dzc = w[2:3] * dy + w[1:2] * dy1 + w[0:1] * dy2
        dz_ref[1] = (dzc * v).astype(BF16)
        dz_ref[2] = (dzc * c).astype(BF16)
        dw_ref[0:1, :] = jnp.sum(dy * z2, axis=0, keepdims=True)
        dw_ref[1:2, :] = jnp.sum(dy * z1, axis=0, keepdims=True)
        dw_ref[2:3, :] = jnp.sum(dy * zc, axis=0, keepdims=True)

    three = pl.BlockSpec((3, S, LANES), lambda j: (0, 0, j))
    wspec = pl.BlockSpec((3, LANES), lambda j: (0, j))
    return _call(
        body, "conv_bwd", (C // LANES,),
        [three, wspec, pl.BlockSpec((S, LANES), lambda j: (0, j))],
        [three, wspec],
        [_sds((3, S, C), BF16), _sds((3, C), F32)],
    )(z_bcv, conv_w, dby)


def _attn_bwd(q, k, v, do, lse, delta, H):
    S = q.shape[0]
    t = _tile(S, 512, CHUNK)
    nk = S // t

    def body(q_ref, k_ref, v_ref, do_ref, lse_ref, dl_ref, dq_ref, dk_ref, dv_ref, dqt_ref):
        kj = pl.program_id(1)

        @pl.when(kj == 0)
        def _():
            dqt_ref[...] = jnp.zeros_like(dqt_ref)

        kv = k_ref[...]
        vv = v_ref[...]
        kt = kv.T

        def block(start, width, carry, masked):
            dk, dv = carry
            off = pl.multiple_of(start * t, t)
            qv = q_ref[pl.ds(off, width * t), :]
            dov = do_ref[pl.ds(off, width * t), :]
            s = _dot_nt(kv, qv)
            if masked:
                s = jnp.where(_chunk_mask(t, width * t, 0), s, -1e30)
            p = jnp.exp2(s - lse_ref[0, 0:1, pl.ds(off, width * t)])
            dp = _dot_nt(vv, dov)
            ds = (p * (dp - dl_ref[0, 0:1, pl.ds(off, width * t)]) * LN2).astype(BF16)
            dqt_ref[:, pl.ds(off, width * t)] += _dot(kt, ds)
            return dk + _dot(ds, qv), dv + _dot(p.astype(BF16), dov)

        init = (jnp.zeros((t, HEAD_SLOT), F32), jnp.zeros((t, VDIM), F32))
        wide = lax.div(nk - 1 - kj, ATTN_BWD_WIDTH)
        left = nk - 1 - kj - wide * ATTN_BWD_WIDTH
        carry = lax.switch(left, [functools.partial(block, kj, extra + 1, init, True)
                                  for extra in range(ATTN_BWD_WIDTH)])
        dk, dv = lax.fori_loop(
            0, wide, lambda j, c: block(kj + 1 + left + j * ATTN_BWD_WIDTH, ATTN_BWD_WIDTH, c, False), carry)
        dk_ref[...] = dk.astype(BF16)
        dv_ref[...] = dv.astype(BF16)

        @pl.when(kj == nk - 1)
        def _():
            dq_ref[...] = (dqt_ref[...] * SCORE_SCALE).T.astype(BF16)

    kspec = lambda w: pl.BlockSpec((t, w), lambda h, j: (j, h))
    qspec = lambda w: pl.BlockSpec((S, w), lambda h, j: (0, h))
    stat = pl.BlockSpec((1, SUBLANES, S), lambda h, j: (h, 0, 0))
    return _call(
        body, "attn_bwd", (H, nk),
        [qspec(HEAD_SLOT), kspec(HEAD_SLOT), kspec(VDIM), qspec(VDIM), stat, stat],
        [qspec(HEAD_SLOT), kspec(HEAD_SLOT), kspec(VDIM)],
        [_sds((S, H * HEAD_SLOT), BF16), _sds((S, H * HEAD_SLOT), BF16), _sds((S, H * VDIM), BF16)],
        [pltpu.VMEM((HEAD_SLOT, S), F32)],
    )(q, k, v, do, lse, delta)


def _mla_prep_bwd(dq, dk, dv, z_qkr, rq, rkv, gq, gkv, cs, c256_w):
    S = z_qkr.shape[0]
    QL, KVL = gq.shape[1], gkv.shape[1]
    H = N_DEV
    tm = _tile(S, 512, SUBLANES)
    half = ROPE // 2

    def body(dq_ref, dk_ref, dv_ref, z_ref, rq_ref, rkv_ref, gq_ref, gkv_ref, cs_ref, w_ref,
             dz_ref, dqp_ref, dkv_ref, dgq_ref, dgkv_ref):
        @pl.when(pl.program_id(0) == 0)
        def _():
            dgq_ref[...] = jnp.zeros_like(dgq_ref)
            dgkv_ref[...] = jnp.zeros_like(dgkv_ref)

        cs_t = cs_ref[...]
        dkr = jnp.zeros((tm, LANES), F32)
        dqn = jnp.zeros((tm, QL), F32)
        dkvn = jnp.zeros((tm, KVL), F32)
        for h in range(H):
            lo, mid, hi = h * HEAD_SLOT, h * HEAD_SLOT + LANES, (h + 1) * HEAD_SLOT
            dqp_ref[:, lo:mid] = dq_ref[:, lo:mid]
            dqp_ref[:, mid:hi] = _unrope(dq_ref[:, mid:hi].astype(F32), cs_t, half).astype(BF16)
            dkv_ref[:, lo:mid] = dk_ref[:, lo:mid]
            dkv_ref[:, mid:hi] = dv_ref[:, h * VDIM:(h + 1) * VDIM]
            dkr = dkr + dk_ref[:, mid:hi].astype(F32)
            dqn = dqn + _dot_nt(dqp_ref[:, lo:hi], w_ref[h, KVL:KVL + QL, :])
            dkvn = dkvn + _dot_nt(dkv_ref[:, lo:hi], w_ref[h, 0:KVL, :])
        z = z_ref[...]
        dqc, dgq = _rms_bwd(dqn, z[:, :QL], rq_ref[...], gq_ref[...])
        dkvc, dgkv = _rms_bwd(dkvn, z[:, QL:QL + KVL], rkv_ref[...], gkv_ref[...])
        dz_ref[:, :QL] = dqc.astype(BF16)
        dz_ref[:, QL:QL + KVL] = dkvc.astype(BF16)
        dz_ref[:, QL + KVL:] = _unrope(dkr, cs_t, half).astype(BF16)
        dgq_ref[...] += dgq
        dgkv_ref[...] += dgkv

    W = z_qkr.shape[1]
    return _call(
        body, "mla_prep_bwd", (S // tm,),
        [_rows(tm, H * HEAD_SLOT), _rows(tm, H * HEAD_SLOT), _rows(tm, H * VDIM), _rows(tm, W), _rows(tm, 1),
         _rows(tm, 1), _whole(gq), _whole(gkv), _rows(tm, 3 * LANES), _whole(c256_w)],
        [_rows(tm, W), _rows(tm, H * HEAD_SLOT), _rows(tm, H * HEAD_SLOT), _whole(gq), _whole(gkv)],
        [_sds((S, W), BF16), _sds((S, H * HEAD_SLOT), BF16), _sds((S, H * HEAD_SLOT), BF16),
         _sds((1, QL), F32), _sds((1, KVL), F32)],
    )(dq, dk, dv, z_qkr, rq, rkv, gq, gkv, cs, c256_w)


def _mix_in_bwd(d_bcv, dz_qkr, dgg, w_bcv, w_qkr, w_gg, h, rstd, gain, dh):
    S, D = h.shape
    C = d_bcv.shape[2]
    tm = _tile(S, 512, SUBLANES)

    def body(db_ref, dq_ref, dgg_ref, wb_ref, wq_ref, wg_ref, h_ref, r_ref, gain_ref, dh_ref, o_ref, dgain_ref):
        @pl.when(pl.program_id(0) == 0)
        def _():
            dgain_ref[...] = jnp.zeros_like(dgain_ref)

        dn = _dot_nt(dq_ref[...], wq_ref[...]) + _dot_nt(dgg_ref[...], wg_ref[...])
        for k in range(3):
            dn = dn + _dot_nt(db_ref[k], wb_ref[k])
        dx, dgain = _rms_bwd(dn, h_ref[...], r_ref[...], gain_ref[...])
        o_ref[...] = dh_ref[...] + dx
        dgain_ref[...] += dgain

    return _call(
        body, "mix_in_bwd", (S // tm,),
        [pl.BlockSpec((3, tm, C), lambda i: (0, i, 0)), _rows(tm, dz_qkr.shape[1]), _rows(tm, dgg.shape[1]),
         _whole(w_bcv), _whole(w_qkr), _whole(w_gg), _rows(tm, D), _rows(tm, 1), _whole(gain), _rows(tm, D)],
        [_rows(tm, D), pl.BlockSpec((1, D), lambda i: (0, 0))],
        [_sds((S, D), F32), _sds((1, D), F32)],
    )(d_bcv, dz_qkr, dgg, w_bcv, w_qkr, w_gg, h, rstd, gain, dh)


def _rope_tables(positions):
    half = ROPE // 2
    inv_freq = ROPE_THETA ** (-jnp.arange(0, ROPE, 2, dtype=F32) / ROPE)
    ang = positions.astype(F32)[:, None] * inv_freq
    cos, sin = jnp.cos(ang), jnp.sin(ang)
    z = jnp.zeros_like(cos)
    pad = jnp.zeros((positions.shape[0], LANES - 2 * half), F32)
    return jnp.concatenate([cos, cos, pad, -sin, z, pad, z, sin, pad], axis=1)


def _grad_rows(w):
    return dict(gu=2 * w["gu1"].shape[1], dn=2 * w["dn1"].shape[1], sq=w["sq"].shape[1], win=w["win"].shape[1],
                c128=w["c128"].shape[1], c256=w["c256"].shape[1])


def _layer_fwd(h0, p_l, cs, w, sm, late):
    C = sm["conv_w"].shape[1]
    QL, KVL = sm["q_norm"].shape[1], sm["kv_norm"].shape[1]
    jac1, a1, at1, n1, r1 = _ffn_up(h0, sm["ffn1_norm"], w["gu1"], 0)
    h1 = _ffn_down(a1, w["dn1"], 0, h0)
    if late is not None:
        w.update(late(h1))
    w_bcv, w_qkr, w_gg = _win_split(w["win"], C, QL, KVL)
    z_bcv, z_qkr, z_gg, un, rm = _mix_in(h1, sm["mix_norm"], w_bcv, w_qkr, w_gg)
    by = _conv_fwd(z_bcv, sm["conv_w"])
    q, k, v, qn, kvn, rq, rkv = _mla_prep(z_qkr, sm["q_norm"], sm["kv_norm"], cs, w["c256"])
    o, lse = _attn_fwd(q, k, v, N_DEV)
    h2, merged, yconv, ymla = _merge_wo(o, by, z_gg, h1, w["sq"], w["c128"])
    jac2, a2, at2, n2, r2 = _ffn_up(h2, sm["ffn2_norm"], w["gu2"], 0)
    h3 = _ffn_down(a2, w["dn2"], 0, h2)
    h4, pre, pp, pn, rp = _ple_fwd(h3, sm["ple_norm"], p_l, w["sq"], w["c128"], C)
    saved = dict(h0=h0, jac1=jac1, at1=at1, n1=n1, r1=r1, h1=h1, w_bcv=w_bcv, w_qkr=w_qkr, w_gg=w_gg, z_bcv=z_bcv,
                 z_qkr=z_qkr, z_gg=z_gg, un=un, rm=rm, by=by, q=q, k=k, v=v, qn=qn, kvn=kvn, rq=rq, rkv=rkv, o=o,
                 lse=lse, h2=h2, merged=merged, yconv=yconv, ymla=ymla, jac2=jac2, at2=at2, n2=n2, r2=r2, h3=h3,
                 pre=pre, pp=pp, pn=pn, rp=rp, p=p_l)
    return h4, saved


def _layer_bwd_late(dh4, s, w, sm, after):
    D = dh4.shape[1]
    C = sm["conv_w"].shape[1]
    P = s["p"].shape[1]
    rows = _grad_rows(w)
    small = {}
    dh3, dpre, dpp, small["ple_norm"] = _ple_bwd(dh4, s["pre"], s["pp"], s["h3"], s["rp"], sm["ple_norm"], w["sq"],
                                                 after)
    g_sq = _tn_square(s["pn"], dpre, None, rows["sq"], 2)
    g_c128 = _tn_cols(s["p"], dpp, None, rows["c128"], C // P)

    dgu2 = _ffn_bwd_act(dh3, w["dn2"], 0, s["jac2"])
    g_dn = _tn_down(s["at2"], dh3, None, rows["dn"], 1)
    g_gu = _tn_slots(s["n2"], dgu2, None, rows["gu"], D)
    dh2, small["ffn2_norm"] = _ffn_bwd_in(dgu2, w["gu2"], 0, s["h2"], s["r2"], sm["ffn2_norm"], dh3)
    return dh2, dict(gu=g_gu, dn=g_dn, sq=g_sq, c128=g_c128), small


def _layer_bwd_mixer(dh2, part, small, s, cs, w, sm, after):
    C = sm["conv_w"].shape[1]
    rows = _grad_rows(w)
    g_gu, g_dn, g_sq, g_c128 = part["gu"], part["dn"], part["sq"], part["c128"]

    dgg, dby, do, dyc, dym, delta = _merge_bwd(dh2, s["z_gg"], s["yconv"], s["ymla"], s["o"], w["sq"], w["c128"], C,
                                               after)
    g_sq = _tn_square(s["merged"], dh2, g_sq, rows["sq"], 1)
    g_sq = _tn_square(s["o"], dym, g_sq, rows["sq"], 0)
    g_c128 = _tn_cols(s["by"], dyc, g_c128, rows["c128"], 0)
    d_bcv, small["conv_w"] = _conv_bwd(s["z_bcv"], sm["conv_w"], dby)
    delta = delta.reshape(N_DEV, SUBLANES, delta.shape[1])
    dq, dk, dv = _attn_bwd(s["q"], s["k"], s["v"], do, s["lse"], delta, N_DEV)
    dz_qkr, dqp, dkv, small["q_norm"], small["kv_norm"] = _mla_prep_bwd(
        dq, dk, dv, s["z_qkr"], s["rq"], s["rkv"], sm["q_norm"], sm["kv_norm"], cs, w["c256"])
    g_c256 = _tn_heads(s["qn"], s["kvn"], dqp, dkv)
    un = s["un"]
    g_win = _win_merge(_tn_plain(un, d_bcv), _tn_plain(un, dz_qkr[None])[0], _tn_plain(un, dgg[None])[0],
                       w["win"].shape[2])
    dh1, small["mix_norm"] = _mix_in_bwd(d_bcv, dz_qkr, dgg, s["w_bcv"], s["w_qkr"], s["w_gg"], s["h1"], s["rm"],
                                         sm["mix_norm"], dh2)
    return dh1, dict(gu=g_gu, dn=g_dn, sq=g_sq, win=g_win, c128=g_c128, c256=g_c256), small


def _layer_bwd_first(dh1, part, small, s, w, sm, after):
    rows = _grad_rows(w)
    dgu1 = _ffn_bwd_act(dh1, w["dn1"], 0, s["jac1"], after)
    g_dn = _tn_down(s["at1"], dh1, part["dn"], rows["dn"], 0)
    g_gu = _tn_slots(s["n1"], dgu1, part["gu"], rows["gu"], 0)
    dh0, small["ffn1_norm"] = _ffn_bwd_in(dgu1, w["gu1"], 0, s["h0"], s["r1"], sm["ffn1_norm"], dh1)
    return dh0, dict(part, gu=g_gu, dn=g_dn), small


def _mesh_pos():
    return lax.axis_index("x"), lax.axis_index("y"), lax.axis_index("c")


def _other_chips(x, y):
    return [(1 - x, y), (x, 1 - y), (1 - x, 1 - y)]


def _pack(arrs, flipped, width):
    L = arrs[0].shape[0]
    shapes = [a.shape[:0:-1] if f else a.shape[1:] for a, f in zip(arrs, flipped)]
    R = sum(r for r, _ in shapes)

    def body(*refs):
        o_ref = refs[-1]
        off = 0
        for a_ref, f, (r, c) in zip(refs[:-1], flipped, shapes):
            a = a_ref[0].T if f else a_ref[0]
            o_ref[0, off:off + r, 0:c] = a.astype(BF16)
            if c < width:
                o_ref[0, off:off + r, c:width] = jnp.zeros((r, width - c), BF16)
            off += r

    return _call(
        body, "pack", (L,),
        [pl.BlockSpec((1,) + a.shape[1:], lambda l: (l, 0, 0)) for a in arrs],
        pl.BlockSpec((1, R, width), lambda l: (l, 0, 0)),
        _sds((L, R, width), BF16),
    )(*arrs)


def _handshake(peers):
    barrier = pltpu.get_barrier_semaphore()
    for peer in peers:
        pl.semaphore_signal(barrier, inc=1, device_id=peer, device_id_type=MESH)
    pl.semaphore_wait(barrier, len(peers))


def _sequencer_call(body, name, out_types, sems, collective_id, operands):
    return pl.kernel(
        body, name=name, out_type=out_types,
        mesh=plsc.ScalarSubcoreMesh(axis_name="seq", num_cores=1),
        scratch_types=tuple(pltpu.SemaphoreType.DMA((k,)) for k in sems),
        compiler_params=pltpu.CompilerParams(collective_id=collective_id),
    )(*operands)


def _all_gather(packs, l, after, collective_id):
    n = len(packs)

    def body(*refs):
        ins, outs = refs[:n], refs[n + len(after):2 * n + len(after)]
        send_sems, recv_sems, local_sems = refs[2 * n + len(after):]
        x, y, c = _mesh_pos()
        me, sibling = (x, y, c), (x, y, 1 - c)
        chips = _other_chips(x, y)
        _handshake([sibling] + [(*chip, c) for chip in chips])

        def copy(q, k, block, to, src=None):
            slot = outs[q].at[4 * block[0] + 2 * block[1] + block[2]]
            return pltpu.make_async_remote_copy(
                src_ref=slot if src is None else src, dst_ref=slot,
                send_sem=send_sems.at[7 * q + k], recv_sem=recv_sems.at[7 * q + k], device_id=to, device_id_type=MESH)

        started = []
        for q in range(n):
            src = ins[q].at[l]
            mine = pltpu.make_async_copy(src, outs[q].at[4 * x + 2 * y + c], local_sems.at[q])
            mine.start()
            started.append(mine)
        sends = []
        for q in range(n):
            src = ins[q].at[l]
            sends.append(copy(q, 0, me, sibling, src=src))
            sends += [copy(q, 1 + j, me, (*chip, c), src=src) for j, chip in enumerate(chips)]
        for cp in sends:
            cp.start()
        for q in range(n):
            for j, chip in enumerate(chips):
                copy(q, 1 + j, (*chip, c), me).wait_recv()
                fwd = copy(q, 4 + j, (*chip, c), sibling)
                fwd.start()
                sends.append(fwd)
        for q in range(n):
            copy(q, 0, sibling, me).wait_recv()
            for j, chip in enumerate(chips):
                copy(q, 4 + j, (*chip, 1 - c), me).wait_recv()
        for cp in sends:
            cp.wait_send()
        for mine in started:
            mine.wait()

    return _sequencer_call(
        body, f"all_gather_{collective_id}", [_sds((N_DEV,) + p.shape[1:], p.dtype) for p in packs], (7 * n, 7 * n, n),
        collective_id, list(packs) + list(after))


def _rs_d2d(gs, l, collective_id):
    n = len(gs)

    def body(*refs):
        ins, outs = refs[:n], refs[n:2 * n]
        send_sems, recv_sems = refs[2 * n:]
        x, y, c = _mesh_pos()
        _handshake([(x, y, 1 - c)])
        copies = []
        for q in range(n):
            for j in range(4):
                copies.append(pltpu.make_async_remote_copy(
                    src_ref=ins[q].at[2 * j + (1 - c)], dst_ref=outs[q].at[j], send_sem=send_sems.at[4 * q + j],
                    recv_sem=recv_sems.at[4 * q + j], device_id=(x, y, 1 - c), device_id_type=MESH))
        for cp in copies:
            cp.start()
        for cp in copies:
            cp.wait()

    return _sequencer_call(
        body, f"rs_d2d_{l}", [_sds((4,) + g.shape[1:], g.dtype) for g in gs], (4 * n, 4 * n), collective_id, gs)


def _rs_add_chip(gs, as_, after):
    n = len(gs)
    steps = 4
    tiles = [g.shape[1] // steps for g in gs]

    def chip(k):
        x, y, _ = _mesh_pos()
        return ([(x, y)] + _other_chips(x, y))[k]

    def body(*refs):
        g_refs, a_refs = refs[:4 * n], refs[4 * n:8 * n]
        own_refs, t_refs = refs[8 * n + len(after):9 * n + len(after)], refs[9 * n + len(after):]
        for q in range(n):
            g, a = g_refs[4 * q:4 * q + 4], a_refs[4 * q:4 * q + 4]
            own_refs[q][...] = g[0][0].astype(F32) + a[0][0].astype(F32)
            for k in range(1, 4):
                t_refs[q][k - 1] = (g[k][0].astype(F32) + a[k][0].astype(F32)).astype(BF16)

    def gspec(q, k):
        def index(i):
            px, py = chip(k)
            return 4 * px + 2 * py + lax.axis_index("c"), i, 0
        return pl.BlockSpec((1, tiles[q], gs[q].shape[2]), index)

    def aspec(q, k):
        def index(i):
            px, py = chip(k)
            return 2 * px + py, i, 0
        return pl.BlockSpec((1, tiles[q], gs[q].shape[2]), index)

    in_specs = [gspec(q, k) for q in range(n) for k in range(4)] + [aspec(q, k) for q in range(n) for k in range(4)]
    operands = [g for g in gs for _ in range(4)] + [a for a in as_ for _ in range(4)]
    out_specs = [pl.BlockSpec((tiles[q], gs[q].shape[2]), lambda i: (i, 0)) for q in range(n)]
    out_specs += [pl.BlockSpec((3, tiles[q], gs[q].shape[2]), lambda i: (0, i, 0)) for q in range(n)]
    out_shape = [_sds(g.shape[1:], F32) for g in gs] + [_sds((3,) + g.shape[1:], BF16) for g in gs]
    res = _call(body, "rs_add_chip", (steps,), in_specs + [ANY] * len(after), out_specs, out_shape)(*operands, *after)
    return res[:n], res[n:]


def _rs_ici(ts, l, collective_id):
    n = len(ts)

    def body(*refs):
        ins, outs = refs[:n], refs[n:2 * n]
        send_sems, recv_sems = refs[2 * n:]
        x, y, c = _mesh_pos()
        chips = _other_chips(x, y)
        _handshake([(*chip, c) for chip in chips])
        copies = []
        for q in range(n):
            for k, chip in enumerate(chips):
                copies.append(pltpu.make_async_remote_copy(
                    src_ref=ins[q].at[k], dst_ref=outs[q].at[k], send_sem=send_sems.at[3 * q + k],
                    recv_sem=recv_sems.at[3 * q + k], device_id=(*chip, c), device_id_type=MESH))
        for cp in copies:
            cp.start()
        for cp in copies:
            cp.wait()

    return _sequencer_call(
        body, f"rs_ici_{l}", [_sds(t.shape, t.dtype) for t in ts], (3 * n, 3 * n), collective_id, ts)


def _all_reduce_small(v):
    n, W = v.shape

    def body(v_ref, out_ref, slots, send_sems, recv_sems):
        x, y, c = _mesh_pos()
        me = 4 * x + 2 * y + c
        slots[me] = v_ref[...]
        copies = []
        for k in range(1, N_DEV):
            kx, ky, kc = (k >> 2) & 1, (k >> 1) & 1, k & 1
            peer = (1 - x if kx else x, 1 - y if ky else y, 1 - c if kc else c)
            copies.append(pltpu.make_async_remote_copy(
                src_ref=v_ref, dst_ref=slots.at[me], send_sem=send_sems.at[k - 1], recv_sem=recv_sems.at[k - 1],
                device_id=peer, device_id_type=MESH))
        for cp in copies:
            cp.start()
        for cp in copies:
            cp.wait()
        acc = slots[0]
        for d in range(1, N_DEV):
            acc = acc + slots[d]
        out_ref[...] = acc

    vm = pl.BlockSpec(memory_space=pltpu.VMEM)
    return pl.pallas_call(
        body, name="all_reduce_small",
        out_shape=_sds((n, W), F32),
        in_specs=[vm], out_specs=vm,
        scratch_shapes=[pltpu.VMEM((N_DEV, n, W), F32), pltpu.SemaphoreType.DMA((7,)), pltpu.SemaphoreType.DMA((7,))],
    )(v)


def _adamw_math(w, g, m, v):
    m2 = ADAM_B1 * m + (1.0 - ADAM_B1) * g
    v2 = ADAM_B2 * v + (1.0 - ADAM_B2) * (g * g)
    m_hat = m2 / (1.0 - ADAM_B1 ** ADAM_STEP)
    v_hat = v2 / (1.0 - ADAM_B2 ** ADAM_STEP)
    return -ADAM_LR * (m_hat / (jnp.sqrt(v_hat) + ADAM_EPS) + ADAM_WD * w), m2, v2


def _adamw(w, g, m, v):
    L, r, c = w.shape
    tr = _tile(r, max(SUBLANES, (256 * 1024 // c) // SUBLANES * SUBLANES), SUBLANES)

    def body(w_ref, g_ref, m_ref, v_ref, d_ref, nm_ref, nv_ref):
        d_ref[...], nm_ref[...], nv_ref[...] = _adamw_math(w_ref[...], g_ref[...], m_ref[...], v_ref[...])

    spec = pl.BlockSpec((1, tr, c), lambda l, i: (l, i, 0))
    return _call(body, "adamw", (L, r // tr), [spec] * 4, [spec] * 3, [_sds((L, r, c), F32)] * 3)(w, g, m, v)


def _adamw_reduced(w, m, v, flipped, own, b, row_off, tr, l, prev, after):
    L = w.shape[0]
    c, r = w.shape[1:] if flipped else w.shape[:0:-1]
    W = own.shape[1]
    ob = row_off // tr
    extra = list(prev or ()) + list(after)

    def body(w_ref, m_ref, v_ref, own_ref, b_ref, *rest):
        g_ref, d_ref, nm_ref, nv_ref = rest[len(extra):]
        g = ((own_ref[...] + b_ref[0].astype(F32)) + b_ref[1].astype(F32)) + b_ref[2].astype(F32)
        g = g[:, :c].T if flipped else g[:, :c]
        g_ref[0] = g
        d_ref[0], nm_ref[0], nv_ref[0] = _adamw_math(w_ref[0], g, m_ref[0], v_ref[0])

    spec = pl.BlockSpec((1, c, tr), lambda i: (l, 0, i)) if flipped else pl.BlockSpec((1, tr, c), lambda i: (l, i, 0))
    return _call(
        body, "adamw_reduced", (r // tr,),
        [spec] * 3 + [pl.BlockSpec((tr, W), lambda i: (ob + i, 0)), pl.BlockSpec((3, tr, W), lambda i: (0, ob + i, 0))]
        + [ANY] * len(extra),
        [spec] * 4, [_sds(w.shape, F32)] * 4,
        aliases={5 + k: k for k in range(4)} if prev else None,
    )(w, m, v, own, b, *extra)


_MEMBERS = dict(gu=("ffn1_w_gu", "ffn2_w_gu"), dn=("ffn1_w_down", "ffn2_w_down"),
                sq=("w_mla_out", "w_o", "w_ple_gate"), win=("w_in",), c128=("w_conv_out", "w_ple_proj"),
                c256=("w_ukv", "w_uq"))
_GATHER_MEMBERS = dict(_MEMBERS, gu1=("ffn1_w_gu",), gu2=("ffn2_w_gu",), dn1=("ffn1_w_down",), dn2=("ffn2_w_down",))
GATHER_STAGES = (("gu1", "dn1"), ("win", "c256", "c128", "sq"), ("gu2", "dn2"))
_FLIPPED = ("ffn1_w_gu", "ffn2_w_gu", "w_in", "w_uq")
_SMALL = ("ffn1_norm", "mix_norm", "q_norm", "kv_norm", "ffn2_norm", "ple_norm")
_ORDER = ("ffn1_norm", "ffn1_w_gu", "ffn1_w_down", "mix_norm", "w_in", "conv_w", "w_conv_out", "q_norm", "kv_norm",
          "w_uq", "w_ukv", "w_mla_out", "w_o", "ffn2_norm", "ffn2_w_gu", "ffn2_w_down", "ple_norm", "w_ple_gate",
          "w_ple_proj", "final_norm")


def _class_width(wts, cls):
    return HEAD_SLOT if cls == "c256" else wts[_GATHER_MEMBERS[cls][0]].shape[2]


def _pack_rows(vecs, width):
    flat = jnp.concatenate([a.reshape(-1) for a in vecs])
    n = flat.shape[0]
    rows = -(-n // width)
    rows = -(-rows // SUBLANES) * SUBLANES
    flat = jnp.pad(flat, (0, rows * width - n))
    offs, o = [], 0
    for a in vecs:
        offs.append(o)
        o += a.size
    return flat.reshape(rows, width), offs


def _unpack_rows(packed, vecs, offs):
    flat = packed.reshape(-1)
    return [flat[o:o + a.size].reshape(a.shape) for a, o in zip(vecs, offs)]


def _train(x, p, positions, target, gathered, packs, small_w, final_norm, update):
    cs = _rope_tables(positions)
    L = len(small_w)
    h = x
    saved = []
    def gather(l, names, after, collective_id):
        got = _all_gather([packs[n] for n in names], l, after, collective_id)
        return dict(zip(names, got))

    late = None
    if packs is not None:
        first, mixer, second = GATHER_STAGES
        w0 = gather(0, first, [], 0)
        w0.update(gather(0, mixer, [w0[first[0]]], 1))
        gathered = [w0]
        late = lambda h1: gather(0, second, [h1], 2)
    everything = sum(GATHER_STAGES, ())
    for l in range(L):
        h, s = _layer_fwd(h, p[l], cs, gathered[l], small_w[l], late)
        late = None
        saved.append(s)
        if packs is not None and l + 1 < L:
            gathered.append(gather(l + 1, everything, [s["by"]], 2 + l + 1))
    dh, loss, d_final = _final_loss(h, final_norm, target)
    grads, smalls = [None] * L, [None] * L
    exchanged = None
    landing = None

    def second_stage(after):
        l, gs, as_ = exchanged
        owns, ts = _rs_add_chip(gs, as_, [after])
        return l, owns, _rs_ici(ts, l, 2 * L + 2 + l)

    for l in reversed(range(L)):
        dh, part, small = _layer_bwd_late(dh, saved[l], gathered[l], small_w[l], [])
        pin = []
        if exchanged is not None:
            landing = second_stage(dh)
            pin = [landing[1][0]]
        dh, part, small = _layer_bwd_mixer(dh, part, small, saved[l], cs, gathered[l], small_w[l], pin)
        pin = [update(*landing)] if exchanged is not None else []
        dh, g, smalls[l] = _layer_bwd_first(dh, part, small, saved[l], gathered[l], small_w[l], pin)
        if update is not None:
            gs = [g[cls] for cls in CLASSES]
            exchanged = (l, gs, _rs_d2d(gs, l, L + 2 + l))
        else:
            grads[l] = g
    if update is not None:
        update(*second_stage(dh))
    return loss[0, 0], dh, grads, smalls, d_final


def kernel(x, p, positions, ffn1_norm, ffn1_w_gu, ffn1_w_down, mix_norm, w_in, conv_w, w_conv_out, q_norm, kv_norm, w_uq, w_ukv, w_mla_out, w_o, ffn2_norm, ffn2_w_gu, ffn2_w_down, ple_norm, w_ple_gate, w_ple_proj, final_norm, loss_target, m_ffn1_norm, m_ffn1_w_gu, m_ffn1_w_down, m_mix_norm, m_w_in, m_conv_w, m_w_conv_out, m_q_norm, m_kv_norm, m_w_uq, m_w_ukv, m_w_mla_out, m_w_o, m_ffn2_norm, m_ffn2_w_gu, m_ffn2_w_down, m_ple_norm, m_w_ple_gate, m_w_ple_proj, m_final_norm, v_ffn1_norm, v_ffn1_w_gu, v_ffn1_w_down, v_mix_norm, v_w_in, v_conv_w, v_w_conv_out, v_q_norm, v_kv_norm, v_w_uq, v_w_ukv, v_w_mla_out, v_w_o, v_ffn2_norm, v_ffn2_w_gu, v_ffn2_w_down, v_ple_norm, v_w_ple_gate, v_w_ple_proj, v_final_norm):
    args = dict(locals())
    wts = {n: args[n] for n in _ORDER}
    L = w_in.shape[0]
    dev = 4 * lax.axis_index("x") + 2 * lax.axis_index("y") + lax.axis_index("c")

    view = lambda n, a: jnp.swapaxes(a, 1, 2) if n in _FLIPPED else a
    packs = {cls: _pack([view(n, wts[n]) for n in _GATHER_MEMBERS[cls]], [n in _FLIPPED for n in _GATHER_MEMBERS[cls]],
                        _class_width(wts, cls))
             for stage in GATHER_STAGES for cls in stage}
    cw = conv_w.shape[2]
    conv_full = lax.dynamic_update_slice(jnp.zeros((L, 3, N_DEV * cw), F32), conv_w, (0, 0, dev * cw))
    conv_packed, conv_offs = _pack_rows([conv_full], FLAT_COLS)
    conv_full = _unpack_rows(_all_reduce_small(conv_packed), [conv_full], conv_offs)[0]
    small_w = [dict({n: wts[n][l][None, :] for n in _SMALL}, conv_w=conv_full[l]) for l in range(L)]

    done = {}

    def update(l, owns, bs):
        for q, cls in enumerate(CLASSES):
            off = 0
            rows = [wts[n].shape[1] for n in _MEMBERS[cls]]
            tr = _tile(math.gcd(*rows), 256, BF16_ROWS)
            for n, r in zip(_MEMBERS[cls], rows):
                done[n] = _adamw_reduced(view(n, wts[n]), view(n, args["m_" + n]), view(n, args["v_" + n]),
                                         n in _FLIPPED, owns[q], bs[q], off, tr, l, done.get(n), [])
                off += r
        return done[_MEMBERS[CLASSES[-1]][-1]][0]

    loss_dev, grad_x, _, smalls, d_final = _train(x[0], p[:, 0], positions[0], loss_target[0], None, packs, small_w,
                                                  final_norm[None, :], update)

    small = [jnp.stack([smalls[l][n][0] for l in range(L)]) for n in _SMALL]
    small += [jnp.stack([smalls[l]["conv_w"] for l in range(L)]), d_final[0], loss_dev[None]]
    packed, offs = _pack_rows(small, FLAT_COLS)
    small = _unpack_rows(_all_reduce_small(packed), small, offs)
    grad = dict(zip(_SMALL, small))
    grad["conv_w"] = lax.dynamic_slice(small[len(_SMALL)], (0, 0, dev * cw), (L, 3, cw))
    grad["final_norm"] = small[-2]
    loss = small[-1][0]

    deltas, new_m, new_v = {}, {}, {}
    for n, outs in done.items():
        grad[n], deltas[n], new_m[n], new_v[n] = (view(n, a) for a in outs)
    for n in _SMALL + ("conv_w", "final_norm"):
        w3 = wts[n].reshape((1,) * (3 - wts[n].ndim) + wts[n].shape)
        d, nm, nv = _adamw(w3, grad[n].reshape(w3.shape), args["m_" + n].reshape(w3.shape),
                           args["v_" + n].reshape(w3.shape))
        deltas[n], new_m[n], new_v[n] = (a.reshape(wts[n].shape) for a in (d, nm, nv))
    return (loss, grad_x[None], *[grad[n] for n in _ORDER], *[deltas[n] for n in _ORDER],
            *[new_m[n] for n in _ORDER], *[new_v[n] for n in _ORDER])
```

```python
import functools
import math

import jax
import jax.numpy as jnp
from jax import lax
from jax.experimental import pallas as pl
from jax.experimental.pallas import tpu as pltpu
from jax.experimental.pallas import tpu_sc as plsc

F32 = jnp.float32
BF16 = jnp.bfloat16

CHUNK = 64
NOPE = 128
ROPE = 64
VDIM = 128
ROPE_THETA = 10000.0
EPS = 1e-6
ATTN_SCALE = (NOPE + ROPE) ** -0.5
SCORE_SCALE = ATTN_SCALE * math.log2(math.e)
LN2 = math.log(2.0)
ADAM_LR = 0.001
ADAM_B1 = 0.9
ADAM_B2 = 0.999
ADAM_EPS = 1e-08
ADAM_WD = 0.01
ADAM_STEP = 10

LANES = 128
SUBLANES = 8
BF16_ROWS = 16
V7X_VMEM_BYTES = 64 * 1024 * 1024
VMEM_LIMIT = V7X_VMEM_BYTES * 7 // 8
HEAD_SLOT = 2 * LANES
N_DEV = 8
ATTN_FWD_WIDTH = 4
ATTN_BWD_WIDTH = 4
FLAT_COLS = 1024
CLASSES = ("gu", "dn", "sq", "win", "c128", "c256")

NT = (((1,), (1,)), ((), ()))
MESH = pl.DeviceIdType.MESH
ANY = pl.BlockSpec(memory_space=pl.ANY)


def _dot(a, b):
    return jnp.dot(a, b, preferred_element_type=F32)


def _dot_nt(a, b):
    return lax.dot_general(a, b, NT, preferred_element_type=F32)


def _sig(x):
    return 1.0 / (1.0 + jnp.exp(-x))


def _tile(n, pref, unit):
    if n <= pref:
        return n
    t = (pref // unit) * unit
    while t >= unit:
        if n % t == 0:
            return t
        t -= unit
    return n


def _call(body, name, grid, in_specs, out_specs, out_shape, scratch=(), aliases=None):
    return pl.pallas_call(
        body,
        name=name,
        grid=grid,
        in_specs=in_specs,
        out_specs=out_specs,
        out_shape=out_shape,
        scratch_shapes=list(scratch),
        input_output_aliases=aliases or {},
        compiler_params=pltpu.CompilerParams(
            dimension_semantics=("arbitrary",) * len(grid), vmem_limit_bytes=VMEM_LIMIT
        ),
    )


def _sds(shape, dtype):
    return jax.ShapeDtypeStruct(shape, dtype)


def _rms_fwd(x, gain):
    rstd = lax.rsqrt(jnp.mean(x * x, axis=-1, keepdims=True) + EPS)
    return x * rstd * gain, rstd


def _rms_bwd(dn, x, rstd, gain):
    xhat = x * rstd
    dgy = dn * gain
    dx = rstd * (dgy - xhat * jnp.mean(dgy * xhat, axis=-1, keepdims=True))
    return dx, jnp.sum(dn * xhat, axis=0, keepdims=True)


def _rows(tm, w):
    return pl.BlockSpec((tm, w), lambda i: (i, 0))


def _whole(a):
    nd = a.ndim
    return pl.BlockSpec(a.shape, lambda i: (0,) * nd, pipeline_mode=pl.Buffered(1))


def _slab(buf, rows, index):
    return pl.BlockSpec((N_DEV, rows, buf.shape[2]), lambda i: (0, index, 0), pipeline_mode=pl.Buffered(1))


def _cat_slots(w):
    return jnp.concatenate([w[d] for d in range(N_DEV)], axis=1)


def _ffn_up(h, gain, gu_w, which):
    S, D = h.shape
    c = gu_w.shape[2]
    tm = _tile(S, 512, SUBLANES)
    nb = N_DEV // 2

    def body(h_ref, gain_ref, w_ref, jac_ref, a_ref, at_ref, n_ref, r_ref):
        n32, rstd = _rms_fwd(h_ref[...], gain_ref[...])
        n = n32.astype(BF16)
        n_ref[...] = n.T
        r_ref[...] = rstd
        for d in range(nb):
            g = _dot(n, w_ref[d])
            u = _dot(n, w_ref[nb + d])
            sg = _sig(g)
            silu = g * sg
            a = (silu * u).astype(BF16)
            a_ref[d] = a
            at_ref[d] = a.T
            jac_ref[d] = (0.5 * u * (sg + silu * (1.0 - sg))).astype(BF16)
            jac_ref[nb + d] = (0.5 * silu).astype(BF16)

    return _call(
        body, "ffn_up", (S // tm,),
        [_rows(tm, D), _whole(gain), _slab(gu_w, D, which)],
        [pl.BlockSpec((N_DEV, tm, c), lambda i: (0, i, 0)), pl.BlockSpec((nb, tm, c), lambda i: (0, i, 0)),
         pl.BlockSpec((nb, c, tm), lambda i: (0, 0, i)), pl.BlockSpec((D, tm), lambda i: (0, i)), _rows(tm, 1)],
        [_sds((N_DEV, S, c), BF16), _sds((nb, S, c), BF16), _sds((nb, c, S), BF16), _sds((D, S), BF16),
         _sds((S, 1), F32)],
    )(h, gain, gu_w)


def _down_weight(w_ref, d, c):
    return w_ref[2 * d:2 * d + 2].reshape(c, w_ref.shape[2])


def _ffn_down(a, dn_w, which, h):
    nb, S, c = a.shape
    D = h.shape[1]
    tm = _tile(S, 1024, SUBLANES)

    def body(a_ref, w_ref, h_ref, o_ref):
        acc = _dot(a_ref[0], _down_weight(w_ref, 0, c))
        for d in range(1, nb):
            acc = acc + _dot(a_ref[d], _down_weight(w_ref, d, c))
        o_ref[...] = h_ref[...] + 0.5 * acc

    return _call(
        body, "ffn_down", (S // tm,),
        [pl.BlockSpec((nb, tm, c), lambda i: (0, i, 0)), _slab(dn_w, c // 2, which), _rows(tm, D)],
        _rows(tm, D),
        _sds((S, D), F32),
    )(a, dn_w, h)


def _win_segments(C, QL, KVL, D):
    o1, o2 = 3 * C, 3 * C + QL + KVL + ROPE
    return [("bcv", k, k * C, (k + 1) * C) for k in range(3)] + [("qkr", None, o1, o2), ("gg", None, o2, o2 + 2 * D)]


def _win_pieces(segments, cw):
    out = []
    for tgt, lead, a, b in segments:
        for d in range(N_DEV):
            lo, hi = max(a, d * cw), min(b, (d + 1) * cw)
            if lo < hi:
                out.append((tgt, lead, d, (lo - d * cw, hi - d * cw), (lo - a, hi - a)))
    return out


def _win_split(win_w, C, QL, KVL):
    _, D, cw = win_w.shape
    WQ = QL + KVL + LANES
    pieces = _win_pieces(_win_segments(C, QL, KVL, D), cw)
    tr = _tile(D, 256, BF16_ROWS)

    def body(w_ref, bcv_ref, qkr_ref, gg_ref):
        tgt = dict(bcv=bcv_ref, qkr=qkr_ref, gg=gg_ref)
        qkr_ref[:, QL + KVL + ROPE:] = jnp.zeros((tr, LANES - ROPE), BF16)
        for name, lead, d, (s0, s1), (t0, t1) in pieces:
            v = w_ref[d, :, s0:s1]
            if lead is None:
                tgt[name][:, t0:t1] = v
            else:
                tgt[name][lead, :, t0:t1] = v

    return _call(
        body, "win_split", (D // tr,),
        [pl.BlockSpec((N_DEV, tr, cw), lambda i: (0, i, 0))],
        [pl.BlockSpec((3, tr, C), lambda i: (0, i, 0)), _rows(tr, WQ), _rows(tr, 2 * D)],
        [_sds((3, D, C), BF16), _sds((D, WQ), BF16), _sds((D, 2 * D), BF16)],
    )(win_w)


def _win_merge(d_bcv, d_qkr, d_gg, cw):
    _, D, C = d_bcv.shape
    WQ = d_qkr.shape[1]
    QL_KVL = WQ - LANES
    o1 = 3 * C
    segments = [("bcv", k, k * C, (k + 1) * C) for k in range(3)]
    segments += [("qkr", None, o1, o1 + QL_KVL + ROPE), ("gg", None, o1 + QL_KVL + ROPE, o1 + QL_KVL + ROPE + 2 * D)]
    pieces = _win_pieces(segments, cw)
    tr = _tile(D, 256, BF16_ROWS)

    def body(bcv_ref, qkr_ref, gg_ref, o_ref):
        src = dict(bcv=bcv_ref, qkr=qkr_ref, gg=gg_ref)
        for name, lead, d, (s0, s1), (t0, t1) in pieces:
            v = src[name][:, t0:t1] if lead is None else src[name][lead, :, t0:t1]
            o_ref[d, :, s0:s1] = v.astype(BF16)

    return _call(
        body, "win_merge", (D // tr,),
        [pl.BlockSpec((3, tr, C), lambda i: (0, i, 0)), _rows(tr, WQ), _rows(tr, 2 * D)],
        pl.BlockSpec((N_DEV, tr, cw), lambda i: (0, i, 0)),
        _sds((N_DEV, D, cw), BF16),
    )(d_bcv, d_qkr, d_gg)


def _mix_in(h, gain, w_bcv, w_qkr, w_gg):
    S, D = h.shape
    C = w_bcv.shape[2]
    tm = _tile(S, 512, SUBLANES)

    def body(h_ref, gain_ref, w1, w2, w3, o1, o2, o3, n_ref, r_ref):
        n32, rstd = _rms_fwd(h_ref[...], gain_ref[...])
        n = n32.astype(BF16)
        n_ref[...] = n.T
        r_ref[...] = rstd
        for k in range(3):
            o1[k] = _dot(n, w1[k]).astype(BF16)
        o2[...] = _dot(n, w2[...])
        o3[...] = _dot(n, w3[...]).astype(BF16)

    return _call(
        body, "mix_in", (S // tm,),
        [_rows(tm, D), _whole(gain), _whole(w_bcv), _whole(w_qkr), _whole(w_gg)],
        [pl.BlockSpec((3, tm, C), lambda i: (0, i, 0)), _rows(tm, w_qkr.shape[1]), _rows(tm, 2 * D),
         pl.BlockSpec((D, tm), lambda i: (0, i)), _rows(tm, 1)],
        [_sds((3, S, C), BF16), _sds((S, w_qkr.shape[1]), F32), _sds((S, 2 * D), BF16), _sds((D, S), BF16),
         _sds((S, 1), F32)],
    )(h, gain, w_bcv, w_qkr, w_gg)


def _conv_taps(zc):
    rows = lax.broadcasted_iota(jnp.int32, zc.shape, 0)
    z1 = jnp.where(rows >= 1, pltpu.roll(zc, 1, 0), 0.0)
    z2 = jnp.where(rows >= 2, pltpu.roll(zc, 2, 0), 0.0)
    return z1, z2


def _conv_fwd(z_bcv, conv_w):
    _, S, C = z_bcv.shape

    def body(z_ref, w_ref, o_ref):
        w = w_ref[...]
        zc = z_ref[1].astype(F32) * z_ref[2].astype(F32)
        z1, z2 = _conv_taps(zc)
        y = w[0:1] * z2 + w[1:2] * z1 + w[2:3] * zc
        o_ref[...] = (z_ref[0].astype(F32) * y).astype(BF16)

    return _call(
        body, "conv_fwd", (C // LANES,),
        [pl.BlockSpec((3, S, LANES), lambda j: (0, 0, j)), pl.BlockSpec((3, LANES), lambda j: (0, j))],
        pl.BlockSpec((S, LANES), lambda j: (0, j)),
        _sds((S, C), BF16),
    )(z_bcv, conv_w)


def _rope(x, cs, half):
    c, s1, s2 = cs[:, :LANES], cs[:, LANES:2 * LANES], cs[:, 2 * LANES:]
    return x * c + pltpu.roll(x, LANES - half, 1) * s1 + pltpu.roll(x, half, 1) * s2


def _unrope(d, cs, half):
    c, s1, s2 = cs[:, :LANES], cs[:, LANES:2 * LANES], cs[:, 2 * LANES:]
    return d * c + pltpu.roll(d * s1, half, 1) + pltpu.roll(d * s2, LANES - half, 1)


def _mla_prep(z_qkr, gq, gkv, cs, c256_w):
    S = z_qkr.shape[0]
    QL, KVL = gq.shape[1], gkv.shape[1]
    H = N_DEV
    tm = _tile(S, 512, SUBLANES)
    half = ROPE // 2

    def body(z_ref, gq_ref, gkv_ref, cs_ref, w_ref, q_ref, k_ref, v_ref, qn_ref, kvn_ref, rq_ref, rkv_ref):
        z = z_ref[...]
        cs_t = cs_ref[...]
        qn32, rq = _rms_fwd(z[:, :QL], gq_ref[...])
        kvn32, rkv = _rms_fwd(z[:, QL:QL + KVL], gkv_ref[...])
        qn = qn32.astype(BF16)
        kvn = kvn32.astype(BF16)
        qn_ref[...] = qn
        kvn_ref[...] = kvn
        rq_ref[...] = rq
        rkv_ref[...] = rkv
        krope = _rope(z[:, QL + KVL:], cs_t, half).astype(BF16)
        for h in range(H):
            lo, mid, hi = h * HEAD_SLOT, h * HEAD_SLOT + LANES, (h + 1) * HEAD_SLOT
            q = _dot(qn, w_ref[h, KVL:KVL + QL, :])
            kv = _dot(kvn, w_ref[h, 0:KVL, :])
            q_ref[:, lo:mid] = (q[:, :LANES] * SCORE_SCALE).astype(BF16)
            q_ref[:, mid:hi] = (_rope(q[:, LANES:], cs_t, half) * SCORE_SCALE).astype(BF16)
            k_ref[:, lo:mid] = kv[:, :LANES].astype(BF16)
            k_ref[:, mid:hi] = krope
            v_ref[:, h * VDIM:(h + 1) * VDIM] = kv[:, LANES:].astype(BF16)

    return _call(
        body, "mla_prep", (S // tm,),
        [_rows(tm, z_qkr.shape[1]), _whole(gq), _whole(gkv), _rows(tm, 3 * LANES), _whole(c256_w)],
        [_rows(tm, H * HEAD_SLOT), _rows(tm, H * HEAD_SLOT), _rows(tm, H * VDIM), _rows(tm, QL), _rows(tm, KVL),
         _rows(tm, 1), _rows(tm, 1)],
        [_sds((S, H * HEAD_SLOT), BF16), _sds((S, H * HEAD_SLOT), BF16), _sds((S, H * VDIM), BF16),
         _sds((S, QL), BF16), _sds((S, KVL), BF16), _sds((S, 1), F32), _sds((S, 1), F32)],
    )(z_qkr, gq, gkv, cs, c256_w)


def _chunk_mask(rows, cols, diagonal_row):
    shift = CHUNK.bit_length() - 1
    krow = (lax.broadcasted_iota(jnp.int32, (rows, cols), 0) - diagonal_row) >> shift
    qcol = lax.broadcasted_iota(jnp.int32, (rows, cols), 1) >> shift
    return krow <= qcol


def _attn_fwd(q, k, v, H):
    S = q.shape[0]
    t = _tile(S, 512, CHUNK)
    nq = S // t

    def body(q_ref, k_ref, v_ref, o_ref, ot_ref, lse_ref, vt_ref):
        qi = pl.program_id(1)

        @pl.when(qi == 0)
        def _():
            vt_ref[0:VDIM, :] = v_ref[...].T
            vt_ref[VDIM:, :] = jnp.ones((BF16_ROWS, S), BF16)

        qv = q_ref[...]

        def block(start, width, carry, masked):
            m, acc = carry
            off = pl.multiple_of(start * t, t)
            s = _dot_nt(k_ref[pl.ds(off, width * t), :], qv)
            if masked:
                s = jnp.where(_chunk_mask(width * t, t, (width - 1) * t), s, -1e30)
            m_new = jnp.maximum(m, jnp.max(s, axis=0, keepdims=True))
            p = jnp.exp2(s - m_new).astype(BF16)
            acc = jnp.exp2(m - m_new) * acc + _dot(vt_ref[:, pl.ds(off, width * t)], p)
            return m_new, acc

        init = (jnp.full((1, t), -1e30, F32), jnp.zeros((VDIM + BF16_ROWS, t), F32))
        wide = lax.div(qi, ATTN_FWD_WIDTH)
        carry = lax.fori_loop(0, wide, lambda j, c: block(j * ATTN_FWD_WIDTH, ATTN_FWD_WIDTH, c, False), init)
        left = qi - wide * ATTN_FWD_WIDTH
        for extra in range(ATTN_FWD_WIDTH):
            @pl.when(left == extra)
            def _():
                m, acc = block(qi - extra, extra + 1, carry, True)
                l = acc[VDIM:VDIM + 1]
                out = (acc[0:VDIM] * (1.0 / l)).astype(BF16)
                ot_ref[...] = out
                o_ref[...] = out.T
                lse_ref[0] = jnp.broadcast_to(m + jnp.log2(l), (SUBLANES, t))

    return _call(
        body, "attn_fwd", (H, nq),
        [pl.BlockSpec((t, HEAD_SLOT), lambda h, i: (i, h)), pl.BlockSpec((S, HEAD_SLOT), lambda h, i: (0, h)),
         pl.BlockSpec((S, VDIM), lambda h, i: (0, h))],
        [pl.BlockSpec((t, VDIM), lambda h, i: (i, h)), pl.BlockSpec((VDIM, t), lambda h, i: (h, i)),
         pl.BlockSpec((1, SUBLANES, t), lambda h, i: (h, 0, i))],
        [_sds((S, H * VDIM), BF16), _sds((H * VDIM, S), BF16), _sds((H, SUBLANES, S), F32)],
        [pltpu.VMEM((VDIM + BF16_ROWS, S), BF16)],
    )(q, k, v)


def _merge_wo(o, by, z_gg, h, sq_w, c128_w):
    S, D = h.shape
    C = by.shape[1]
    r = sq_w.shape[1] // 3
    tm = _tile(S, 512, SUBLANES)

    def body(o_ref, by_ref, gg_ref, h_ref, wmo_ref, wo_ref, wco_ref, h2_ref, mg_ref, yc_ref, ym_ref):
        ymla = _dot(o_ref[...], wmo_ref[...].reshape(N_DEV * r, D))
        yconv = _dot(by_ref[...], _cat_slots(wco_ref))
        gg = gg_ref[...].astype(F32)
        merged = (_sig(gg[:, :D]) * yconv + _sig(gg[:, D:]) * ymla).astype(BF16)
        mg_ref[...] = merged.T
        yc_ref[...] = yconv.astype(BF16)
        ym_ref[...] = ymla.astype(BF16)
        h2_ref[...] = h_ref[...] + _dot(merged, wo_ref[...].reshape(N_DEV * r, D))

    return _call(
        body, "merge_wo", (S // tm,),
        [_rows(tm, o.shape[1]), _rows(tm, C), _rows(tm, 2 * D), _rows(tm, D), _slab(sq_w, r, 0), _slab(sq_w, r, 1),
         _slab(c128_w, C, 0)],
        [_rows(tm, D), pl.BlockSpec((D, tm), lambda i: (0, i)), _rows(tm, D), _rows(tm, D)],
        [_sds((S, D), F32), _sds((D, S), BF16), _sds((S, D), BF16), _sds((S, D), BF16)],
    )(o, by, z_gg, h, sq_w, sq_w, c128_w)


def _ple_fwd(h, gain, p, sq_w, c128_w, C):
    S, D = h.shape
    P = p.shape[1]
    r = sq_w.shape[1] // 3
    tm = _tile(S, 512, SUBLANES)

    def body(h_ref, gain_ref, p_ref, wpg_ref, wpp_ref, o_ref, pre_ref, pp_ref, n_ref, r_ref):
        x = h_ref[...]
        n32, rstd = _rms_fwd(x, gain_ref[...])
        n = n32.astype(BF16)
        n_ref[...] = n.T
        r_ref[...] = rstd
        pre = _dot(n, wpg_ref[...].reshape(N_DEV * r, D))
        pp = _dot(p_ref[...].astype(BF16), _cat_slots(wpp_ref))
        pre_ref[...] = pre.astype(BF16)
        pp_ref[...] = pp.astype(BF16)
        o_ref[...] = x + _sig(pre) * pp

    return _call(
        body, "ple_fwd", (S // tm,),
        [_rows(tm, D), _whole(gain), _rows(tm, P), _slab(sq_w, r, 2), _slab(c128_w, P, C // P)],
        [_rows(tm, D), _rows(tm, D), _rows(tm, D), pl.BlockSpec((D, tm), lambda i: (0, i)), _rows(tm, 1)],
        [_sds((S, D), F32), _sds((S, D), BF16), _sds((S, D), BF16), _sds((D, S), BF16), _sds((S, 1), F32)],
    )(h, gain, p, sq_w, c128_w)


def _final_loss(h, gain, target):
    S, D = h.shape
    tm = _tile(S, 512, SUBLANES)

    def body(h_ref, gain_ref, t_ref, dh_ref, loss_ref, dg_ref):
        @pl.when(pl.program_id(0) == 0)
        def _():
            loss_ref[...] = jnp.zeros_like(loss_ref)
            dg_ref[...] = jnp.zeros_like(dg_ref)

        x = h_ref[...]
        gain_v = gain_ref[...]
        y, rstd = _rms_fwd(x, gain_v)
        err = y - t_ref[...]
        loss_ref[...] += 0.5 * jnp.sum(jnp.mean(err * err, axis=-1, keepdims=True))
        dx, dgain = _rms_bwd(err * (1.0 / D), x, rstd, gain_v)
        dh_ref[...] = dx
        dg_ref[...] += dgain

    return _call(
        body, "final_loss", (S // tm,),
        [_rows(tm, D), _whole(gain), _rows(tm, D)],
        [_rows(tm, D), pl.BlockSpec((1, LANES), lambda i: (0, 0)), pl.BlockSpec((1, D), lambda i: (0, 0))],
        [_sds((S, D), F32), _sds((1, LANES), F32), _sds((1, D), F32)],
    )(h, gain, target)


def _tn_call(body, name, grid, in_specs, out_spec, out_shape, scratch, operands, prev):
    n = len(operands)
    if prev is None:
        return _call(body, name, grid, in_specs, out_spec, out_shape, scratch)(*operands)
    assert prev.shape == out_shape.shape and prev.dtype == out_shape.dtype

    def wrapped(*refs):
        body(*refs[:n], *refs[n + 1:])

    return _call(wrapped, name, grid, in_specs + [ANY], out_spec, out_shape, scratch, {n: 0})(*operands, prev)


def _tn_slots(xt, dy, prev, rows_total, row_off):
    K, S = xt.shape
    B, _, c = dy.shape
    tk = _tile(K, 1024, BF16_ROWS)

    def body(xt_ref, dy_ref, o_ref):
        o_ref[0] = _dot(xt_ref[...], dy_ref[0]).astype(BF16)

    return _tn_call(
        body, "tn_slots", (K // tk, B),
        [pl.BlockSpec((tk, S), lambda i, b: (i, 0)), pl.BlockSpec((1, S, c), lambda i, b: (b, 0, 0))],
        pl.BlockSpec((1, tk, c), lambda i, b: (b, row_off // tk + i, 0)),
        _sds((B, rows_total, c), BF16), [], [xt, dy], prev)


def _tn_plain(xt, dy):
    K, S = xt.shape
    B, _, c = dy.shape
    tk = _tile(K, 512, BF16_ROWS)
    tn = _tile(c, 1024, LANES)

    def body(xt_ref, dy_ref, o_ref):
        o_ref[0] = _dot(xt_ref[...], dy_ref[0])

    return _call(
        body, "tn_plain", (K // tk, B, c // tn),
        [pl.BlockSpec((tk, S), lambda i, b, j: (i, 0)), pl.BlockSpec((1, S, tn), lambda i, b, j: (b, 0, j))],
        pl.BlockSpec((1, tk, tn), lambda i, b, j: (b, i, j)),
        _sds((B, K, c), F32),
    )(xt, dy)


def _tn_down(at, dh, prev, rows_total, which):
    nb, c, S = at.shape
    D = dh.shape[1]
    r = c // 2
    tn = _tile(D, 512, LANES)

    def body(at_ref, dh_ref, o_ref):
        g = 0.5 * _dot(at_ref[0], dh_ref[...].astype(BF16))
        o_ref[...] = g.astype(BF16).reshape(2, r, tn)

    return _tn_call(
        body, "tn_down", (D // tn, nb),
        [pl.BlockSpec((1, c, S), lambda j, i: (i, 0, 0)), pl.BlockSpec((S, tn), lambda j, i: (0, j))],
        pl.BlockSpec((2, r, tn), lambda j, i: (i, which, j)),
        _sds((N_DEV, rows_total, D), BF16), [], [at, dh], prev)


def _tn_square(xt, dy, prev, rows_total, member):
    K, S = xt.shape
    N = dy.shape[1]
    r = K // N_DEV
    tk = _tile(K, 512, r)
    tn = _tile(N, 512, LANES)

    def body(xt_ref, dy_ref, o_ref):
        g = _dot(xt_ref[...], dy_ref[...].astype(BF16))
        o_ref[...] = g.astype(BF16).reshape(tk // r, r, tn)

    return _tn_call(
        body, "tn_square", (N // tn, K // tk),
        [pl.BlockSpec((tk, S), lambda j, i: (i, 0)), pl.BlockSpec((S, tn), lambda j, i: (0, j))],
        pl.BlockSpec((tk // r, r, tn), lambda j, i: (i, member, j)),
        _sds((N_DEV, rows_total, N), BF16), [], [xt, dy], prev)


def _tn_cols(x, dy, prev, rows_total, row_block):
    S, K = x.shape
    N = dy.shape[1]
    cw = N // N_DEV

    def body(x_ref, dy_ref, o_ref):
        g = _dot(x_ref[...].astype(BF16).T, dy_ref[...])
        for d in range(N_DEV):
            o_ref[d] = g[:, d * cw:(d + 1) * cw].astype(BF16)

    return _tn_call(
        body, "tn_cols", (1,),
        [pl.BlockSpec((S, K), lambda i: (0, 0)), pl.BlockSpec((S, N), lambda i: (0, 0))],
        pl.BlockSpec((N_DEV, K, cw), lambda i: (0, row_block, 0)),
        _sds((N_DEV, rows_total, cw), BF16), [], [x, dy], prev)


def _tn_heads(qn, kvn, dqp, dkv):
    S, QL = qn.shape
    KVL = kvn.shape[1]

    def body(qn_ref, kvn_ref, dq_ref, dkv_ref, o_ref):
        o_ref[0, 0:KVL, :] = _dot(kvn_ref[...].T, dkv_ref[...]).astype(BF16)
        o_ref[0, KVL:KVL + QL, :] = _dot(qn_ref[...].T, dq_ref[...]).astype(BF16)

    head = pl.BlockSpec((S, HEAD_SLOT), lambda h: (0, h))
    return _call(
        body, "tn_heads", (N_DEV,),
        [pl.BlockSpec((S, QL), lambda h: (0, 0)), pl.BlockSpec((S, KVL), lambda h: (0, 0)), head, head],
        pl.BlockSpec((1, KVL + QL, HEAD_SLOT), lambda h: (h, 0, 0)),
        _sds((N_DEV, KVL + QL, HEAD_SLOT), BF16),
    )(qn, kvn, dqp, dkv)


def _ple_bwd(dh, pre, pp, h, rstd, gain, sq_w, after):
    S, D = h.shape
    r = sq_w.shape[1] // 3
    tm = _tile(S, 512, SUBLANES)

    def body(dh_ref, pre_ref, pp_ref, h_ref, r_ref, gain_ref, wpg_ref, *rest):
        o_ref, dpre_ref, dpp_ref, dg_ref = rest[len(after):]

        @pl.when(pl.program_id(0) == 0)
        def _():
            dg_ref[...] = jnp.zeros_like(dg_ref)

        d = dh_ref[...]
        gate = _sig(pre_ref[...].astype(F32))
        dpre = (d * pp_ref[...].astype(F32) * gate * (1.0 - gate)).astype(BF16)
        dpre_ref[...] = dpre
        dpp_ref[...] = (d * gate).astype(BF16)
        dn = _dot_nt(dpre, wpg_ref[...].reshape(N_DEV * r, D))
        dx, dgain = _rms_bwd(dn, h_ref[...], r_ref[...], gain_ref[...])
        o_ref[...] = d + dx
        dg_ref[...] += dgain

    return _call(
        body, "ple_bwd", (S // tm,),
        [_rows(tm, D), _rows(tm, D), _rows(tm, D), _rows(tm, D), _rows(tm, 1), _whole(gain), _slab(sq_w, r, 2)]
        + [ANY] * len(after),
        [_rows(tm, D), _rows(tm, D), _rows(tm, D), pl.BlockSpec((1, D), lambda i: (0, 0))],
        [_sds((S, D), F32), _sds((S, D), BF16), _sds((S, D), BF16), _sds((1, D), F32)],
    )(dh, pre, pp, h, rstd, gain, sq_w, *after)


def _ffn_bwd_act(dh, dn_w, which, jac, after=()):
    S, D = dh.shape
    _, _, c = jac.shape
    nb = N_DEV // 2
    tm = _tile(S, 512, SUBLANES)

    def body(dh_ref, w_ref, jac_ref, *rest):
        dgu_ref = rest[len(after)]
        dhb = dh_ref[...].astype(BF16)
        for d in range(nb):
            da = _dot_nt(dhb, _down_weight(w_ref, d, c))
            dgu_ref[d] = (da * jac_ref[d].astype(F32)).astype(BF16)
            dgu_ref[nb + d] = (da * jac_ref[nb + d].astype(F32)).astype(BF16)

    act = pl.BlockSpec((N_DEV, tm, c), lambda i: (0, i, 0))
    return _call(
        body, "ffn_bwd_act", (S // tm,),
        [_rows(tm, D), _slab(dn_w, c // 2, which), act] + [ANY] * len(after),
        act,
        _sds((N_DEV, S, c), BF16),
    )(dh, dn_w, jac, *after)


def _ffn_bwd_in(dgu, gu_w, which, h, rstd, gain, dh):
    S, D = h.shape
    c = dgu.shape[2]
    tm = _tile(S, 512, SUBLANES)

    def body(dgu_ref, w_ref, h_ref, r_ref, gain_ref, dh_ref, o_ref, dgain_ref):
        @pl.when(pl.program_id(0) == 0)
        def _():
            dgain_ref[...] = jnp.zeros_like(dgain_ref)

        dn = _dot_nt(dgu_ref[0], w_ref[0])
        for d in range(1, N_DEV):
            dn = dn + _dot_nt(dgu_ref[d], w_ref[d])
        dx, dgain = _rms_bwd(dn, h_ref[...], r_ref[...], gain_ref[...])
        o_ref[...] = dh_ref[...] + dx
        dgain_ref[...] += dgain

    return _call(
        body, "ffn_bwd_in", (S // tm,),
        [pl.BlockSpec((N_DEV, tm, c), lambda i: (0, i, 0)), _slab(gu_w, D, which), _rows(tm, D), _rows(tm, 1),
         _whole(gain), _rows(tm, D)],
        [_rows(tm, D), pl.BlockSpec((1, D), lambda i: (0, 0))],
        [_sds((S, D), F32), _sds((1, D), F32)],
    )(dgu, gu_w, h, rstd, gain, dh)


def _merge_bwd(dh, z_gg, yconv, ymla, o, sq_w, c128_w, C, after):
    S, D = dh.shape
    r = sq_w.shape[1] // 3
    HV = N_DEV * r
    H = HV // VDIM
    tm = _tile(S, 512, SUBLANES)

    def head_rows():
        row = lax.broadcasted_iota(jnp.int32, (SUBLANES * H, HV), 0) >> (SUBLANES.bit_length() - 1)
        col = lax.broadcasted_iota(jnp.int32, (SUBLANES * H, HV), 1) >> (VDIM.bit_length() - 1)
        return jnp.where(row == col, 1.0, 0.0).astype(BF16)

    def body(dh_ref, gg_ref, yc_ref, ym_ref, o_ref, wmo_ref, wo_ref, wco_ref, *rest):
        dgg_ref, dby_ref, do_ref, dyc_ref, dym_ref, dl_ref = rest[len(after):]
        dm = _dot_nt(dh_ref[...].astype(BF16), wo_ref[...].reshape(HV, D))
        gg = gg_ref[...].astype(F32)
        sgc = _sig(gg[:, :D])
        sgm = _sig(gg[:, D:])
        dyc = (dm * sgc).astype(BF16)
        dym = (dm * sgm).astype(BF16)
        dyc_ref[...] = dyc
        dym_ref[...] = dym
        dgg_ref[:, :D] = (dm * yc_ref[...].astype(F32) * sgc * (1.0 - sgc)).astype(BF16)
        dgg_ref[:, D:] = (dm * ym_ref[...].astype(F32) * sgm * (1.0 - sgm)).astype(BF16)
        dby_ref[...] = _dot_nt(dyc, _cat_slots(wco_ref)).astype(BF16)
        do = _dot_nt(dym, wmo_ref[...].reshape(HV, D)).astype(BF16)
        do_ref[...] = do
        prod = do.astype(F32) * o_ref[...].astype(F32)
        hi = prod.astype(BF16)
        lo = (prod - hi.astype(F32)).astype(BF16)
        pick = head_rows()
        dl_ref[...] = _dot_nt(pick, hi) + _dot_nt(pick, lo)

    return _call(
        body, "merge_bwd", (S // tm,),
        [_rows(tm, D), _rows(tm, 2 * D), _rows(tm, D), _rows(tm, D), _rows(tm, HV), _slab(sq_w, r, 0),
         _slab(sq_w, r, 1), _slab(c128_w, C, 0)] + [ANY] * len(after),
        [_rows(tm, 2 * D), _rows(tm, C), _rows(tm, HV), _rows(tm, D), _rows(tm, D),
         pl.BlockSpec((SUBLANES * H, tm), lambda i: (0, i))],
        [_sds((S, 2 * D), BF16), _sds((S, C), BF16), _sds((S, HV), BF16), _sds((S, D), BF16), _sds((S, D), BF16),
         _sds((SUBLANES * H, S), F32)],
    )(dh, z_gg, yconv, ymla, o, sq_w, sq_w, c128_w, *after)


def _conv_bwd(z_bcv, conv_w, dby):
    _, S, C = z_bcv.shape

    def body(z_ref, w_ref, dby_ref, dz_ref, dw_ref):
        w = w_ref[...]
        c = z_ref[1].astype(F32)
        v = z_ref[2].astype(F32)
        d = dby_ref[...].astype(F32)
        zc = c * v
        z1, z2 = _conv_taps(zc)
        y = w[0:1] * z2 + w[1:2] * z1 + w[2:3] * zc
        dz_ref[0] = (d * y).astype(BF16)
        dy = d * z_ref[0].astype(F32)
        rows = lax.broadcasted_iota(jnp.int32, dy.shape, 0)
        dy1 = jnp.where(rows < S - 1, pltpu.roll(dy, S - 1, 0), 0.0)
        dy2 = jnp.where(rows < S - 2, pltpu.roll(dy, S - 2, 0), 0.0)
        dzc = w[2:3] * dy + w[1:2] * dy1 + w[0:1] * dy2
        dz_ref[1] = (dzc * v).astype(BF16)
        dz_ref[2] = (dzc * c).astype(BF16)
        dw_ref[0:1, :] = jnp.sum(dy * z2, axis=0, keepdims=True)
        dw_ref[1:2, :] = jnp.sum(dy * z1, axis=0, keepdims=True)
        dw_ref[2:3, :] = jnp.sum(dy * zc, axis=0, keepdims=True)

    three = pl.BlockSpec((3, S, LANES), lambda j: (0, 0, j))
    wspec = pl.BlockSpec((3, LANES), lambda j: (0, j))
    return _call(
        body, "conv_bwd", (C // LANES,),
        [three, wspec, pl.BlockSpec((S, LANES), lambda j: (0, j))],
        [three, wspec],
        [_sds((3, S, C), BF16), _sds((3, C), F32)],
    )(z_bcv, conv_w, dby)


def _attn_bwd(q, k, v, do, lse, delta, H):
    S = q.shape[0]
    t = _tile(S, 512, CHUNK)
    nk = S // t

    def body(q_ref, k_ref, v_ref, do_ref, lse_ref, dl_ref, dq_ref, dk_ref, dv_ref, dqt_ref):
        kj = pl.program_id(1)

        @pl.when(kj == 0)
        def _():
            dqt_ref[...] = jnp.zeros_like(dqt_ref)

        kv = k_ref[...]
        vv = v_ref[...]
        kt = kv.T

        def block(start, width, carry, masked):
            dk, dv = carry
            off = pl.multiple_of(start * t, t)
            qv = q_ref[pl.ds(off, width * t), :]
            dov = do_ref[pl.ds(off, width * t), :]
            s = _dot_nt(kv, qv)
            if masked:
                s = jnp.where(_chunk_mask(t, width * t, 0), s, -1e30)
            p = jnp.exp2(s - lse_ref[0, 0:1, pl.ds(off, width * t)])
            dp = _dot_nt(vv, dov)
            ds = (p * (dp - dl_ref[0, 0:1, pl.ds(off, width * t)]) * LN2).astype(BF16)
            dqt_ref[:, pl.ds(off, width * t)] += _dot(kt, ds)
            return dk + _dot(ds, qv), dv + _dot(p.astype(BF16), dov)

        init = (jnp.zeros((t, HEAD_SLOT), F32), jnp.zeros((t, VDIM), F32))
        wide = lax.div(nk - 1 - kj, ATTN_BWD_WIDTH)
        left = nk - 1 - kj - wide * ATTN_BWD_WIDTH
        carry = lax.switch(left, [functools.partial(block, kj, extra + 1, init, True)
                                  for extra in range(ATTN_BWD_WIDTH)])
        dk, dv = lax.fori_loop(
            0, wide, lambda j, c: block(kj + 1 + left + j * ATTN_BWD_WIDTH, ATTN_BWD_WIDTH, c, False), carry)
        dk_ref[...] = dk.astype(BF16)
        dv_ref[...] = dv.astype(BF16)

        @pl.when(kj == nk - 1)
        def _():
            dq_ref[...] = (dqt_ref[...] * SCORE_SCALE).T.astype(BF16)

    kspec = lambda w: pl.BlockSpec((t, w), lambda h, j: (j, h))
    qspec = lambda w: pl.BlockSpec((S, w), lambda h, j: (0, h))
    stat = pl.BlockSpec((1, SUBLANES, S), lambda h, j: (h, 0, 0))
    return _call(
        body, "attn_bwd", (H, nk),
        [qspec(HEAD_SLOT), kspec(HEAD_SLOT), kspec(VDIM), qspec(VDIM), stat, stat],
        [qspec(HEAD_SLOT), kspec(HEAD_SLOT), kspec(VDIM)],
        [_sds((S, H * HEAD_SLOT), BF16), _sds((S, H * HEAD_SLOT), BF16), _sds((S, H * VDIM), BF16)],
        [pltpu.VMEM((HEAD_SLOT, S), F32)],
    )(q, k, v, do, lse, delta)


def _mla_prep_bwd(dq, dk, dv, z_qkr, rq, rkv, gq, gkv, cs, c256_w):
    S = z_qkr.shape[0]
    QL, KVL = gq.shape[1], gkv.shape[1]
    H = N_DEV
    tm = _tile(S, 512, SUBLANES)
    half = ROPE // 2

    def body(dq_ref, dk_ref, dv_ref, z_ref, rq_ref, rkv_ref, gq_ref, gkv_ref, cs_ref, w_ref,
             dz_ref, dqp_ref, dkv_ref, dgq_ref, dgkv_ref):
        @pl.when(pl.program_id(0) == 0)
        def _():
            dgq_ref[...] = jnp.zeros_like(dgq_ref)
            dgkv_ref[...] = jnp.zeros_like(dgkv_ref)

        cs_t = cs_ref[...]
        dkr = jnp.zeros((tm, LANES), F32)
        dqn = jnp.zeros((tm, QL), F32)
        dkvn = jnp.zeros((tm, KVL), F32)
        for h in range(H):
            lo, mid, hi = h * HEAD_SLOT, h * HEAD_SLOT + LANES, (h + 1) * HEAD_SLOT
            dqp_ref[:, lo:mid] = dq_ref[:, lo:mid]
            dqp_ref[:, mid:hi] = _unrope(dq_ref[:, mid:hi].astype(F32), cs_t, half).astype(BF16)
            dkv_ref[:, lo:mid] = dk_ref[:, lo:mid]
            dkv_ref[:, mid:hi] = dv_ref[:, h * VDIM:(h + 1) * VDIM]
            dkr = dkr + dk_ref[:, mid:hi].astype(F32)
            dqn = dqn + _dot_nt(dqp_ref[:, lo:hi], w_ref[h, KVL:KVL + QL, :])
            dkvn = dkvn + _dot_nt(dkv_ref[:, lo:hi], w_ref[h, 0:KVL, :])
        z = z_ref[...]
        dqc, dgq = _rms_bwd(dqn, z[:, :QL], rq_ref[...], gq_ref[...])
        dkvc, dgkv = _rms_bwd(dkvn, z[:, QL:QL + KVL], rkv_ref[...], gkv_ref[...])
        dz_ref[:, :QL] = dqc.astype(BF16)
        dz_ref[:, QL:QL + KVL] = dkvc.astype(BF16)
        dz_ref[:, QL + KVL:] = _unrope(dkr, cs_t, half).astype(BF16)
        dgq_ref[...] += dgq
        dgkv_ref[...] += dgkv

    W = z_qkr.shape[1]
    return _call(
        body, "mla_prep_bwd", (S // tm,),
        [_rows(tm, H * HEAD_SLOT), _rows(tm, H * HEAD_SLOT), _rows(tm, H * VDIM), _rows(tm, W), _rows(tm, 1),
         _rows(tm, 1), _whole(gq), _whole(gkv), _rows(tm, 3 * LANES), _whole(c256_w)],
        [_rows(tm, W), _rows(tm, H * HEAD_SLOT), _rows(tm, H * HEAD_SLOT), _whole(gq), _whole(gkv)],
        [_sds((S, W), BF16), _sds((S, H * HEAD_SLOT), BF16), _sds((S, H * HEAD_SLOT), BF16),
         _sds((1, QL), F32), _sds((1, KVL), F32)],
    )(dq, dk, dv, z_qkr, rq, rkv, gq, gkv, cs, c256_w)


def _mix_in_bwd(d_bcv, dz_qkr, dgg, w_bcv, w_qkr, w_gg, h, rstd, gain, dh):
    S, D = h.shape
    C = d_bcv.shape[2]
    tm = _tile(S, 512, SUBLANES)

    def body(db_ref, dq_ref, dgg_ref, wb_ref, wq_ref, wg_ref, h_ref, r_ref, gain_ref, dh_ref, o_ref, dgain_ref):
        @pl.when(pl.program_id(0) == 0)
        def _():
            dgain_ref[...] = jnp.zeros_like(dgain_ref)

        dn = _dot_nt(dq_ref[...], wq_ref[...]) + _dot_nt(dgg_ref[...], wg_ref[...])
        for k in range(3):
            dn = dn + _dot_nt(db_ref[k], wb_ref[k])
        dx, dgain = _rms_bwd(dn, h_ref[...], r_ref[...], gain_ref[...])
        o_ref[...] = dh_ref[...] + dx
        dgain_ref[...] += dgain

    return _call(
        body, "mix_in_bwd", (S // tm,),
        [pl.BlockSpec((3, tm, C), lambda i: (0, i, 0)), _rows(tm, dz_qkr.shape[1]), _rows(tm, dgg.shape[1]),
         _whole(w_bcv), _whole(w_qkr), _whole(w_gg), _rows(tm, D), _rows(tm, 1), _whole(gain), _rows(tm, D)],
        [_rows(tm, D), pl.BlockSpec((1, D), lambda i: (0, 0))],
        [_sds((S, D), F32), _sds((1, D), F32)],
    )(d_bcv, dz_qkr, dgg, w_bcv, w_qkr, w_gg, h, rstd, gain, dh)


def _rope_tables(positions):
    half = ROPE // 2
    inv_freq = ROPE_THETA ** (-jnp.arange(0, ROPE, 2, dtype=F32) / ROPE)
    ang = positions.astype(F32)[:, None] * inv_freq
    cos, sin = jnp.cos(ang), jnp.sin(ang)
    z = jnp.zeros_like(cos)
    pad = jnp.zeros((positions.shape[0], LANES - 2 * half), F32)
    return jnp.concatenate([cos, cos, pad, -sin, z, pad, z, sin, pad], axis=1)


def _grad_rows(w):
    return dict(gu=2 * w["gu1"].shape[1], dn=2 * w["dn1"].shape[1], sq=w["sq"].shape[1], win=w["win"].shape[1],
                c128=w["c128"].shape[1], c256=w["c256"].shape[1])


def _layer_fwd(h0, p_l, cs, w, sm, late):
    C = sm["conv_w"].shape[1]
    QL, KVL = sm["q_norm"].shape[1], sm["kv_norm"].shape[1]
    jac1, a1, at1, n1, r1 = _ffn_up(h0, sm["ffn1_norm"], w["gu1"], 0)
    h1 = _ffn_down(a1, w["dn1"], 0, h0)
    if late is not None:
        w.update(late(h1))
    w_bcv, w_qkr, w_gg = _win_split(w["win"], C, QL, KVL)
    z_bcv, z_qkr, z_gg, un, rm = _mix_in(h1, sm["mix_norm"], w_bcv, w_qkr, w_gg)
    by = _conv_fwd(z_bcv, sm["conv_w"])
    q, k, v, qn, kvn, rq, rkv = _mla_prep(z_qkr, sm["q_norm"], sm["kv_norm"], cs, w["c256"])
    o, ot, lse = _attn_fwd(q, k, v, N_DEV)
    h2, merged, yconv, ymla = _merge_wo(o, by, z_gg, h1, w["sq"], w["c128"])
    jac2, a2, at2, n2, r2 = _ffn_up(h2, sm["ffn2_norm"], w["gu2"], 0)
    h3 = _ffn_down(a2, w["dn2"], 0, h2)
    h4, pre, pp, pn, rp = _ple_fwd(h3, sm["ple_norm"], p_l, w["sq"], w["c128"], C)
    saved = dict(h0=h0, jac1=jac1, at1=at1, n1=n1, r1=r1, h1=h1, w_bcv=w_bcv, w_qkr=w_qkr, w_gg=w_gg, z_bcv=z_bcv,
                 z_qkr=z_qkr, z_gg=z_gg, un=un, rm=rm, by=by, q=q, k=k, v=v, qn=qn, kvn=kvn, rq=rq, rkv=rkv, o=o, ot=ot,
                 lse=lse, h2=h2, merged=merged, yconv=yconv, ymla=ymla, jac2=jac2, at2=at2, n2=n2, r2=r2, h3=h3,
                 pre=pre, pp=pp, pn=pn, rp=rp, p=p_l)
    return h4, saved


def _layer_bwd_late(dh4, s, w, sm, after):
    D = dh4.shape[1]
    C = sm["conv_w"].shape[1]
    P = s["p"].shape[1]
    rows = _grad_rows(w)
    small = {}
    dh3, dpre, dpp, small["ple_norm"] = _ple_bwd(dh4, s["pre"], s["pp"], s["h3"], s["rp"], sm["ple_norm"], w["sq"],
                                                 after)
    g_sq = _tn_square(s["pn"], dpre, None, rows["sq"], 2)
    g_c128 = _tn_cols(s["p"], dpp, None, rows["c128"], C // P)

    dgu2 = _ffn_bwd_act(dh3, w["dn2"], 0, s["jac2"])
    g_dn = _tn_down(s["at2"], dh3, None, rows["dn"], 1)
    g_gu = _tn_slots(s["n2"], dgu2, None, rows["gu"], D)
    dh2, small["ffn2_norm"] = _ffn_bwd_in(dgu2, w["gu2"], 0, s["h2"], s["r2"], sm["ffn2_norm"], dh3)
    return dh2, dict(gu=g_gu, dn=g_dn, sq=g_sq, c128=g_c128), small


def _layer_bwd_mixer(dh2, part, small, s, cs, w, sm, after):
    C = sm["conv_w"].shape[1]
    rows = _grad_rows(w)
    g_gu, g_dn, g_sq, g_c128 = part["gu"], part["dn"], part["sq"], part["c128"]

    dgg, dby, do, dyc, dym, delta = _merge_bwd(dh2, s["z_gg"], s["yconv"], s["ymla"], s["o"], w["sq"], w["c128"], C,
                                               after)
    g_sq = _tn_square(s["merged"], dh2, g_sq, rows["sq"], 1)
    g_sq = _tn_square(s["ot"], dym, g_sq, rows["sq"], 0)
    g_c128 = _tn_cols(s["by"], dyc, g_c128, rows["c128"], 0)
    d_bcv, small["conv_w"] = _conv_bwd(s["z_bcv"], sm["conv_w"], dby)
    delta = delta.reshape(N_DEV, SUBLANES, delta.shape[1])
    dq, dk, dv = _attn_bwd(s["q"], s["k"], s["v"], do, s["lse"], delta, N_DEV)
    dz_qkr, dqp, dkv, small["q_norm"], small["kv_norm"] = _mla_prep_bwd(
        dq, dk, dv, s["z_qkr"], s["rq"], s["rkv"], sm["q_norm"], sm["kv_norm"], cs, w["c256"])
    g_c256 = _tn_heads(s["qn"], s["kvn"], dqp, dkv)
    un = s["un"]
    g_win = _win_merge(_tn_plain(un, d_bcv), _tn_plain(un, dz_qkr[None])[0], _tn_plain(un, dgg[None])[0],
                       w["win"].shape[2])
    dh1, small["mix_norm"] = _mix_in_bwd(d_bcv, dz_qkr, dgg, s["w_bcv"], s["w_qkr"], s["w_gg"], s["h1"], s["rm"],
                                         sm["mix_norm"], dh2)
    return dh1, dict(gu=g_gu, dn=g_dn, sq=g_sq, win=g_win, c128=g_c128, c256=g_c256), small


def _layer_bwd_first(dh1, part, small, s, w, sm, after):
    rows = _grad_rows(w)
    dgu1 = _ffn_bwd_act(dh1, w["dn1"], 0, s["jac1"], after)
    g_dn = _tn_down(s["at1"], dh1, part["dn"], rows["dn"], 0)
    g_gu = _tn_slots(s["n1"], dgu1, part["gu"], rows["gu"], 0)
    dh0, small["ffn1_norm"] = _ffn_bwd_in(dgu1, w["gu1"], 0, s["h0"], s["r1"], sm["ffn1_norm"], dh1)
    return dh0, dict(part, gu=g_gu, dn=g_dn), small


def _mesh_pos():
    return lax.axis_index("x"), lax.axis_index("y"), lax.axis_index("c")


def _other_chips(x, y):
    return [(1 - x, y), (x, 1 - y), (1 - x, 1 - y)]


def _pack(arrs, flipped, width):
    L = arrs[0].shape[0]
    shapes = [a.shape[:0:-1] if f else a.shape[1:] for a, f in zip(arrs, flipped)]
    R = sum(r for r, _ in shapes)

    def body(*refs):
        o_ref = refs[-1]
        off = 0
        for a_ref, f, (r, c) in zip(refs[:-1], flipped, shapes):
            a = a_ref[0].T if f else a_ref[0]
            o_ref[0, off:off + r, 0:c] = a.astype(BF16)
            if c < width:
                o_ref[0, off:off + r, c:width] = jnp.zeros((r, width - c), BF16)
            off += r

    return _call(
        body, "pack", (L,),
        [pl.BlockSpec((1,) + a.shape[1:], lambda l: (l, 0, 0)) for a in arrs],
        pl.BlockSpec((1, R, width), lambda l: (l, 0, 0)),
        _sds((L, R, width), BF16),
    )(*arrs)


def _handshake(peers):
    barrier = pltpu.get_barrier_semaphore()
    for peer in peers:
        pl.semaphore_signal(barrier, inc=1, device_id=peer, device_id_type=MESH)
    pl.semaphore_wait(barrier, len(peers))


def _sequencer_call(body, name, out_types, sems, collective_id, operands):
    return pl.kernel(
        body, name=name, out_type=out_types,
        mesh=plsc.ScalarSubcoreMesh(axis_name="seq", num_cores=1),
        scratch_types=tuple(pltpu.SemaphoreType.DMA((k,)) for k in sems),
        compiler_params=pltpu.CompilerParams(collective_id=collective_id),
    )(*operands)


def _all_gather(packs, l, after, collective_id):
    n = len(packs)

    def body(*refs):
        ins, outs = refs[:n], refs[n + len(after):2 * n + len(after)]
        send_sems, recv_sems, local_sems = refs[2 * n + len(after):]
        x, y, c = _mesh_pos()
        me, sibling = (x, y, c), (x, y, 1 - c)
        chips = _other_chips(x, y)
        _handshake([sibling] + [(*chip, c) for chip in chips])

        def copy(q, k, block, to, src=None):
            slot = outs[q].at[4 * block[0] + 2 * block[1] + block[2]]
            return pltpu.make_async_remote_copy(
                src_ref=slot if src is None else src, dst_ref=slot,
                send_sem=send_sems.at[7 * q + k], recv_sem=recv_sems.at[7 * q + k], device_id=to, device_id_type=MESH)

        started = []
        for q in range(n):
            src = ins[q].at[l]
            mine = pltpu.make_async_copy(src, outs[q].at[4 * x + 2 * y + c], local_sems.at[q])
            mine.start()
            started.append(mine)
        sends = []
        for q in range(n):
            src = ins[q].at[l]
            sends.append(copy(q, 0, me, sibling, src=src))
            sends += [copy(q, 1 + j, me, (*chip, c), src=src) for j, chip in enumerate(chips)]
        for cp in sends:
            cp.start()
        for q in range(n):
            for j, chip in enumerate(chips):
                copy(q, 1 + j, (*chip, c), me).wait_recv()
                fwd = copy(q, 4 + j, (*chip, c), sibling)
                fwd.start()
                sends.append(fwd)
        for q in range(n):
            copy(q, 0, sibling, me).wait_recv()
            for j, chip in enumerate(chips):
                copy(q, 4 + j, (*chip, 1 - c), me).wait_recv()
        for cp in sends:
            cp.wait_send()
        for mine in started:
            mine.wait()

    return _sequencer_call(
        body, f"all_gather_{collective_id}", [_sds((N_DEV,) + p.shape[1:], p.dtype) for p in packs], (7 * n, 7 * n, n),
        collective_id, list(packs) + list(after))


def _rs_d2d(gs, l, collective_id):
    n = len(gs)

    def body(*refs):
        ins, outs = refs[:n], refs[n:2 * n]
        send_sems, recv_sems = refs[2 * n:]
        x, y, c = _mesh_pos()
        _handshake([(x, y, 1 - c)])
        copies = []
        for q in range(n):
            for j in range(4):
                copies.append(pltpu.make_async_remote_copy(
                    src_ref=ins[q].at[2 * j + (1 - c)], dst_ref=outs[q].at[j], send_sem=send_sems.at[4 * q + j],
                    recv_sem=recv_sems.at[4 * q + j], device_id=(x, y, 1 - c), device_id_type=MESH))
        for cp in copies:
            cp.start()
        for cp in copies:
            cp.wait()

    return _sequencer_call(
        body, f"rs_d2d_{l}", [_sds((4,) + g.shape[1:], g.dtype) for g in gs], (4 * n, 4 * n), collective_id, gs)


def _rs_add_chip(gs, as_, after):
    n = len(gs)
    steps = 4
    tiles = [g.shape[1] // steps for g in gs]

    def chip(k):
        x, y, _ = _mesh_pos()
        return ([(x, y)] + _other_chips(x, y))[k]

    def body(*refs):
        g_refs, a_refs = refs[:4 * n], refs[4 * n:8 * n]
        own_refs, t_refs = refs[8 * n + len(after):9 * n + len(after)], refs[9 * n + len(after):]
        for q in range(n):
            g, a = g_refs[4 * q:4 * q + 4], a_refs[4 * q:4 * q + 4]
            own_refs[q][...] = g[0][0].astype(F32) + a[0][0].astype(F32)
            for k in range(1, 4):
                t_refs[q][k - 1] = (g[k][0].astype(F32) + a[k][0].astype(F32)).astype(BF16)

    def gspec(q, k):
        def index(i):
            px, py = chip(k)
            return 4 * px + 2 * py + lax.axis_index("c"), i, 0
        return pl.BlockSpec((1, tiles[q], gs[q].shape[2]), index)

    def aspec(q, k):
        def index(i):
            px, py = chip(k)
            return 2 * px + py, i, 0
        return pl.BlockSpec((1, tiles[q], gs[q].shape[2]), index)

    in_specs = [gspec(q, k) for q in range(n) for k in range(4)] + [aspec(q, k) for q in range(n) for k in range(4)]
    operands = [g for g in gs for _ in range(4)] + [a for a in as_ for _ in range(4)]
    out_specs = [pl.BlockSpec((tiles[q], gs[q].shape[2]), lambda i: (i, 0)) for q in range(n)]
    out_specs += [pl.BlockSpec((3, tiles[q], gs[q].shape[2]), lambda i: (0, i, 0)) for q in range(n)]
    out_shape = [_sds(g.shape[1:], F32) for g in gs] + [_sds((3,) + g.shape[1:], BF16) for g in gs]
    res = _call(body, "rs_add_chip", (steps,), in_specs + [ANY] * len(after), out_specs, out_shape)(*operands, *after)
    return res[:n], res[n:]


def _rs_ici(ts, l, collective_id):
    n = len(ts)

    def body(*refs):
        ins, outs = refs[:n], refs[n:2 * n]
        send_sems, recv_sems = refs[2 * n:]
        x, y, c = _mesh_pos()
        chips = _other_chips(x, y)
        _handshake([(*chip, c) for chip in chips])
        copies = []
        for q in range(n):
            for k, chip in enumerate(chips):
                copies.append(pltpu.make_async_remote_copy(
                    src_ref=ins[q].at[k], dst_ref=outs[q].at[k], send_sem=send_sems.at[3 * q + k],
                    recv_sem=recv_sems.at[3 * q + k], device_id=(*chip, c), device_id_type=MESH))
        for cp in copies:
            cp.start()
        for cp in copies:
            cp.wait()

    return _sequencer_call(
        body, f"rs_ici_{l}", [_sds(t.shape, t.dtype) for t in ts], (3 * n, 3 * n), collective_id, ts)


def _all_reduce_small(v):
    n, W = v.shape

    def body(v_ref, out_ref, slots, send_sems, recv_sems):
        x, y, c = _mesh_pos()
        me = 4 * x + 2 * y + c
        slots[me] = v_ref[...]
        copies = []
        for k in range(1, N_DEV):
            kx, ky, kc = (k >> 2) & 1, (k >> 1) & 1, k & 1
            peer = (1 - x if kx else x, 1 - y if ky else y, 1 - c if kc else c)
            copies.append(pltpu.make_async_remote_copy(
                src_ref=v_ref, dst_ref=slots.at[me], send_sem=send_sems.at[k - 1], recv_sem=recv_sems.at[k - 1],
                device_id=peer, device_id_type=MESH))
        for cp in copies:
            cp.start()
        for cp in copies:
            cp.wait()
        acc = slots[0]
        for d in range(1, N_DEV):
            acc = acc + slots[d]
        out_ref[...] = acc

    vm = pl.BlockSpec(memory_space=pltpu.VMEM)
    return pl.pallas_call(
        body, name="all_reduce_small",
        out_shape=_sds((n, W), F32),
        in_specs=[vm], out_specs=vm,
        scratch_shapes=[pltpu.VMEM((N_DEV, n, W), F32), pltpu.SemaphoreType.DMA((7,)), pltpu.SemaphoreType.DMA((7,))],
    )(v)


def _adamw_math(w, g, m, v):
    m2 = ADAM_B1 * m + (1.0 - ADAM_B1) * g
    v2 = ADAM_B2 * v + (1.0 - ADAM_B2) * (g * g)
    m_hat = m2 / (1.0 - ADAM_B1 ** ADAM_STEP)
    v_hat = v2 / (1.0 - ADAM_B2 ** ADAM_STEP)
    return -ADAM_LR * (m_hat / (jnp.sqrt(v_hat) + ADAM_EPS) + ADAM_WD * w), m2, v2


def _adamw(w, g, m, v):
    L, r, c = w.shape
    tr = _tile(r, max(SUBLANES, (256 * 1024 // c) // SUBLANES * SUBLANES), SUBLANES)

    def body(w_ref, g_ref, m_ref, v_ref, d_ref, nm_ref, nv_ref):
        d_ref[...], nm_ref[...], nv_ref[...] = _adamw_math(w_ref[...], g_ref[...], m_ref[...], v_ref[...])

    spec = pl.BlockSpec((1, tr, c), lambda l, i: (l, i, 0))
    return _call(body, "adamw", (L, r // tr), [spec] * 4, [spec] * 3, [_sds((L, r, c), F32)] * 3)(w, g, m, v)


def _adamw_reduced(w, m, v, flipped, own, b, row_off, tr, l, prev, after):
    L = w.shape[0]
    c, r = w.shape[1:] if flipped else w.shape[:0:-1]
    W = own.shape[1]
    ob = row_off // tr
    extra = list(prev or ()) + list(after)

    def body(w_ref, m_ref, v_ref, own_ref, b_ref, *rest):
        g_ref, d_ref, nm_ref, nv_ref = rest[len(extra):]
        g = ((own_ref[...] + b_ref[0].astype(F32)) + b_ref[1].astype(F32)) + b_ref[2].astype(F32)
        g = g[:, :c].T if flipped else g[:, :c]
        g_ref[0] = g
        d_ref[0], nm_ref[0], nv_ref[0] = _adamw_math(w_ref[0], g, m_ref[0], v_ref[0])

    spec = pl.BlockSpec((1, c, tr), lambda i: (l, 0, i)) if flipped else pl.BlockSpec((1, tr, c), lambda i: (l, i, 0))
    return _call(
        body, "adamw_reduced", (r // tr,),
        [spec] * 3 + [pl.BlockSpec((tr, W), lambda i: (ob + i, 0)), pl.BlockSpec((3, tr, W), lambda i: (0, ob + i, 0))]
        + [ANY] * len(extra),
        [spec] * 4, [_sds(w.shape, F32)] * 4,
        aliases={5 + k: k for k in range(4)} if prev else None,
    )(w, m, v, own, b, *extra)


_MEMBERS = dict(gu=("ffn1_w_gu", "ffn2_w_gu"), dn=("ffn1_w_down", "ffn2_w_down"),
                sq=("w_mla_out", "w_o", "w_ple_gate"), win=("w_in",), c128=("w_conv_out", "w_ple_proj"),
                c256=("w_ukv", "w_uq"))
_GATHER_MEMBERS = dict(_MEMBERS, gu1=("ffn1_w_gu",), gu2=("ffn2_w_gu",), dn1=("ffn1_w_down",), dn2=("ffn2_w_down",))
GATHER_STAGES = (("gu1", "dn1"), ("win", "c256", "c128", "sq"), ("gu2", "dn2"))
_FLIPPED = ("ffn1_w_gu", "ffn2_w_gu", "w_in", "w_uq")
_SMALL = ("ffn1_norm", "mix_norm", "q_norm", "kv_norm", "ffn2_norm", "ple_norm")
_ORDER = ("ffn1_norm", "ffn1_w_gu", "ffn1_w_down", "mix_norm", "w_in", "conv_w", "w_conv_out", "q_norm", "kv_norm",
          "w_uq", "w_ukv", "w_mla_out", "w_o", "ffn2_norm", "ffn2_w_gu", "ffn2_w_down", "ple_norm", "w_ple_gate",
          "w_ple_proj", "final_norm")


def _class_width(wts, cls):
    return HEAD_SLOT if cls == "c256" else wts[_GATHER_MEMBERS[cls][0]].shape[2]


def _pack_rows(vecs, width):
    flat = jnp.concatenate([a.reshape(-1) for a in vecs])
    n = flat.shape[0]
    rows = -(-n // width)
    rows = -(-rows // SUBLANES) * SUBLANES
    flat = jnp.pad(flat, (0, rows * width - n))
    offs, o = [], 0
    for a in vecs:
        offs.append(o)
        o += a.size
    return flat.reshape(rows, width), offs


def _unpack_rows(packed, vecs, offs):
    flat = packed.reshape(-1)
    return [flat[o:o + a.size].reshape(a.shape) for a, o in zip(vecs, offs)]


def _train(x, p, positions, target, gathered, packs, small_w, final_norm, update):
    cs = _rope_tables(positions)
    L = len(small_w)
    h = x
    saved = []
    def gather(l, names, after, collective_id):
        got = _all_gather([packs[n] for n in names], l, after, collective_id)
        return dict(zip(names, got))

    late = None
    if packs is not None:
        first, mixer, second = GATHER_STAGES
        w0 = gather(0, first, [], 0)
        w0.update(gather(0, mixer, [w0[first[0]]], 1))
        gathered = [w0]
        late = lambda h1: gather(0, second, [h1], 2)
    everything = sum(GATHER_STAGES, ())
    for l in range(L):
        h, s = _layer_fwd(h, p[l], cs, gathered[l], small_w[l], late)
        late = None
        saved.append(s)
        if packs is not None and l + 1 < L:
            gathered.append(gather(l + 1, everything, [s["by"]], 2 + l + 1))
    dh, loss, d_final = _final_loss(h, final_norm, target)
    grads, smalls = [None] * L, [None] * L
    exchanged = None
    landing = None

    def second_stage(after):
        l, gs, as_ = exchanged
        owns, ts = _rs_add_chip(gs, as_, [after])
        return l, owns, _rs_ici(ts, l, 2 * L + 2 + l)

    for l in reversed(range(L)):
        dh, part, small = _layer_bwd_late(dh, saved[l], gathered[l], small_w[l], [])
        pin = []
        if exchanged is not None:
            landing = second_stage(dh)
            pin = [landing[1][0]]
        dh, part, small = _layer_bwd_mixer(dh, part, small, saved[l], cs, gathered[l], small_w[l], pin)
        pin = [update(*landing)] if exchanged is not None else []
        dh, g, smalls[l] = _layer_bwd_first(dh, part, small, saved[l], gathered[l], small_w[l], pin)
        if update is not None:
            gs = [g[cls] for cls in CLASSES]
            exchanged = (l, gs, _rs_d2d(gs, l, L + 2 + l))
        else:
            grads[l] = g
    if update is not None:
        update(*second_stage(dh))
    return loss[0, 0], dh, grads, smalls, d_final


def kernel(x, p, positions, ffn1_norm, ffn1_w_gu, ffn1_w_down, mix_norm, w_in, conv_w, w_conv_out, q_norm, kv_norm, w_uq, w_ukv, w_mla_out, w_o, ffn2_norm, ffn2_w_gu, ffn2_w_down, ple_norm, w_ple_gate, w_ple_proj, final_norm, loss_target, m_ffn1_norm, m_ffn1_w_gu, m_ffn1_w_down, m_mix_norm, m_w_in, m_conv_w, m_w_conv_out, m_q_norm, m_kv_norm, m_w_uq, m_w_ukv, m_w_mla_out, m_w_o, m_ffn2_norm, m_ffn2_w_gu, m_ffn2_w_down, m_ple_norm, m_w_ple_gate, m_w_ple_proj, m_final_norm, v_ffn1_norm, v_ffn1_w_gu, v_ffn1_w_down, v_mix_norm, v_w_in, v_conv_w, v_w_conv_out, v_q_norm, v_kv_norm, v_w_uq, v_w_ukv, v_w_mla_out, v_w_o, v_ffn2_norm, v_ffn2_w_gu, v_ffn2_w_down, v_ple_norm, v_w_ple_gate, v_w_ple_proj, v_final_norm):
    args = dict(locals())
    wts = {n: args[n] for n in _ORDER}
    L = w_in.shape[0]
    dev = 4 * lax.axis_index("x") + 2 * lax.axis_index("y") + lax.axis_index("c")

    view = lambda n, a: jnp.swapaxes(a, 1, 2) if n in _FLIPPED else a
    packs = {cls: _pack([view(n, wts[n]) for n in _GATHER_MEMBERS[cls]], [n in _FLIPPED for n in _GATHER_MEMBERS[cls]],
                        _class_width(wts, cls))
             for stage in GATHER_STAGES for cls in stage}
    cw = conv_w.shape[2]
    conv_full = lax.dynamic_update_slice(jnp.zeros((L, 3, N_DEV * cw), F32), conv_w, (0, 0, dev * cw))
    conv_packed, conv_offs = _pack_rows([conv_full], FLAT_COLS)
    conv_full = _unpack_rows(_all_reduce_small(conv_packed), [conv_full], conv_offs)[0]
    small_w = [dict({n: wts[n][l][None, :] for n in _SMALL}, conv_w=conv_full[l]) for l in range(L)]

    done = {}

    def update(l, owns, bs):
        for q, cls in enumerate(CLASSES):
            off = 0
            rows = [wts[n].shape[1] for n in _MEMBERS[cls]]
            tr = _tile(math.gcd(*rows), 256, BF16_ROWS)
            for n, r in zip(_MEMBERS[cls], rows):
                done[n] = _adamw_reduced(view(n, wts[n]), view(n, args["m_" + n]), view(n, args["v_" + n]),
                                         n in _FLIPPED, owns[q], bs[q], off, tr, l, done.get(n), [])
                off += r
        return done[_MEMBERS[CLASSES[-1]][-1]][0]

    loss_dev, grad_x, _, smalls, d_final = _train(x[0], p[:, 0], positions[0], loss_target[0], None, packs, small_w,
                                                  final_norm[None, :], update)

    small = [jnp.stack([smalls[l][n][0] for l in range(L)]) for n in _SMALL]
    small += [jnp.stack([smalls[l]["conv_w"] for l in range(L)]), d_final[0], loss_dev[None]]
    packed, offs = _pack_rows(small, FLAT_COLS)
    small = _unpack_rows(_all_reduce_small(packed), small, offs)
    grad = dict(zip(_SMALL, small))
    grad["conv_w"] = lax.dynamic_slice(small[len(_SMALL)], (0, 0, dev * cw), (L, 3, cw))
    grad["final_norm"] = small[-2]
    loss = small[-1][0]

    deltas, new_m, new_v = {}, {}, {}
    for n, outs in done.items():
        grad[n], deltas[n], new_m[n], new_v[n] = (view(n, a) for a in outs)
    for n in _SMALL + ("conv_w", "final_norm"):
        w3 = wts[n].reshape((1,) * (3 - wts[n].ndim) + wts[n].shape)
        d, nm, nv = _adamw(w3, grad[n].reshape(w3.shape), args["m_" + n].reshape(w3.shape),
                           args["v_" + n].reshape(w3.shape))
        deltas[n], new_m[n], new_v[n] = (a.reshape(wts[n].shape) for a in (d, nm, nv))
    return (loss, grad_x[None], *[grad[n] for n in _ORDER], *[deltas[n] for n in _ORDER],
            *[new_m[n] for n in _ORDER], *[new_v[n] for n in _ORDER])
```

```python
import functools
import math

import jax
import jax.numpy as jnp
from jax import lax
from jax.experimental import pallas as pl
from jax.experimental.pallas import tpu as pltpu
from jax.experimental.pallas import tpu_sc as plsc

F32 = jnp.float32
BF16 = jnp.bfloat16

CHUNK = 64
NOPE = 128
ROPE = 64
VDIM = 128
ROPE_THETA = 10000.0
EPS = 1e-6
ATTN_SCALE = (NOPE + ROPE) ** -0.5
SCORE_SCALE = ATTN_SCALE * math.log2(math.e)
LN2 = math.log(2.0)
ADAM_LR = 0.001
ADAM_B1 = 0.9
ADAM_B2 = 0.999
ADAM_EPS = 1e-08
ADAM_WD = 0.01
ADAM_STEP = 10

LANES = 128
SUBLANES = 8
BF16_ROWS = 16
V7X_VMEM_BYTES = 64 * 1024 * 1024
VMEM_LIMIT = V7X_VMEM_BYTES * 7 // 8
HEAD_SLOT = 2 * LANES
N_DEV = 8
ATTN_FWD_WIDTH = 4
ATTN_BWD_WIDTH = 4
FLAT_COLS = 1024
CLASSES = ("gu", "dn", "sq", "win", "c128", "c256")

NT = (((1,), (1,)), ((), ()))
MESH = pl.DeviceIdType.MESH
ANY = pl.BlockSpec(memory_space=pl.ANY)


def _dot(a, b):
    return jnp.dot(a, b, preferred_element_type=F32)


def _dot_nt(a, b):
    return lax.dot_general(a, b, NT, preferred_element_type=F32)


def _sig(x):
    return 1.0 / (1.0 + jnp.exp(-x))


def _tile(n, pref, unit):
    if n <= pref:
        return n
    t = (pref // unit) * unit
    while t >= unit:
        if n % t == 0:
            return t
        t -= unit
    return n


def _call(body, name, grid, in_specs, out_specs, out_shape, scratch=(), aliases=None):
    return pl.pallas_call(
        body,
        name=name,
        grid=grid,
        in_specs=in_specs,
        out_specs=out_specs,
        out_shape=out_shape,
        scratch_shapes=list(scratch),
        input_output_aliases=aliases or {},
        compiler_params=pltpu.CompilerParams(
            dimension_semantics=("arbitrary",) * len(grid), vmem_limit_bytes=VMEM_LIMIT
        ),
    )


def _sds(shape, dtype):
    return jax.ShapeDtypeStruct(shape, dtype)


def _rms_fwd(x, gain):
    rstd = lax.rsqrt(jnp.mean(x * x, axis=-1, keepdims=True) + EPS)
    return x * rstd * gain, rstd


def _rms_bwd(dn, x, rstd, gain):
    xhat = x * rstd
    dgy = dn * gain
    dx = rstd * (dgy - xhat * jnp.mean(dgy * xhat, axis=-1, keepdims=True))
    return dx, jnp.sum(dn * xhat, axis=0, keepdims=True)


def _rows(tm, w):
    return pl.BlockSpec((tm, w), lambda i: (i, 0))


def _whole(a):
    nd = a.ndim
    return pl.BlockSpec(a.shape, lambda i: (0,) * nd, pipeline_mode=pl.Buffered(1))


def _slab(buf, rows, index):
    return pl.BlockSpec((N_DEV, rows, buf.shape[2]), lambda i: (0, index, 0), pipeline_mode=pl.Buffered(1))


def _cat_slots(w):
    return jnp.concatenate([w[d] for d in range(N_DEV)], axis=1)


def _down_weight(w_ref, d, c):
    return w_ref[2 * d:2 * d + 2].reshape(c, w_ref.shape[2])


def _ffn_fwd(h, gain, gu_w, dn_w):
    S, D = h.shape
    c = gu_w.shape[2]
    tm = _tile(S, 512, SUBLANES)
    nb = N_DEV // 2

    def body(h_ref, gain_ref, w_ref, wd_ref, o_ref, jac_ref, at_ref, n_ref, r_ref):
        x = h_ref[...]
        n32, rstd = _rms_fwd(x, gain_ref[...])
        n = n32.astype(BF16)
        n_ref[...] = n.T
        r_ref[...] = rstd
        acc = jnp.zeros((tm, D), F32)
        for d in range(nb):
            g = _dot(n, w_ref[d])
            u = _dot(n, w_ref[nb + d])
            sg = _sig(g)
            silu = g * sg
            a = (silu * u).astype(BF16)
            at_ref[d] = a.T
            jac_ref[d] = (0.5 * u * (sg + silu * (1.0 - sg))).astype(BF16)
            jac_ref[nb + d] = (0.5 * silu).astype(BF16)
            acc = acc + _dot(a, _down_weight(wd_ref, d, c))
        o_ref[...] = x + 0.5 * acc

    return _call(
        body, "ffn_fwd", (S // tm,),
        [_rows(tm, D), _whole(gain), _slab(gu_w, D, 0), _slab(dn_w, c // 2, 0)],
        [_rows(tm, D), pl.BlockSpec((N_DEV, tm, c), lambda i: (0, i, 0)),
         pl.BlockSpec((nb, c, tm), lambda i: (0, 0, i)), pl.BlockSpec((D, tm), lambda i: (0, i)), _rows(tm, 1)],
        [_sds((S, D), F32), _sds((N_DEV, S, c), BF16), _sds((nb, c, S), BF16), _sds((D, S), BF16),
         _sds((S, 1), F32)],
    )(h, gain, gu_w, dn_w)


def _win_segments(C, QL, KVL, D):
    o1, o2 = 3 * C, 3 * C + QL + KVL + ROPE
    return [("bcv", k, k * C, (k + 1) * C) for k in range(3)] + [("qkr", None, o1, o2), ("gg", None, o2, o2 + 2 * D)]


def _win_pieces(segments, cw):
    out = []
    for tgt, lead, a, b in segments:
        for d in range(N_DEV):
            lo, hi = max(a, d * cw), min(b, (d + 1) * cw)
            if lo < hi:
                out.append((tgt, lead, d, (lo - d * cw, hi - d * cw), (lo - a, hi - a)))
    return out


def _win_split(win_w, C, QL, KVL):
    _, D, cw = win_w.shape
    WQ = QL + KVL + LANES
    pieces = _win_pieces(_win_segments(C, QL, KVL, D), cw)
    tr = _tile(D, 256, BF16_ROWS)

    def body(w_ref, bcv_ref, qkr_ref, gg_ref):
        tgt = dict(bcv=bcv_ref, qkr=qkr_ref, gg=gg_ref)
        qkr_ref[:, QL + KVL + ROPE:] = jnp.zeros((tr, LANES - ROPE), BF16)
        for name, lead, d, (s0, s1), (t0, t1) in pieces:
            v = w_ref[d, :, s0:s1]
            if lead is None:
                tgt[name][:, t0:t1] = v
            else:
                tgt[name][lead, :, t0:t1] = v

    return _call(
        body, "win_split", (D // tr,),
        [pl.BlockSpec((N_DEV, tr, cw), lambda i: (0, i, 0))],
        [pl.BlockSpec((3, tr, C), lambda i: (0, i, 0)), _rows(tr, WQ), _rows(tr, 2 * D)],
        [_sds((3, D, C), BF16), _sds((D, WQ), BF16), _sds((D, 2 * D), BF16)],
    )(win_w)


def _win_merge(d_bcv, d_qkr, d_gg, cw):
    _, D, C = d_bcv.shape
    WQ = d_qkr.shape[1]
    QL_KVL = WQ - LANES
    o1 = 3 * C
    segments = [("bcv", k, k * C, (k + 1) * C) for k in range(3)]
    segments += [("qkr", None, o1, o1 + QL_KVL + ROPE), ("gg", None, o1 + QL_KVL + ROPE, o1 + QL_KVL + ROPE + 2 * D)]
    pieces = _win_pieces(segments, cw)
    tr = _tile(D, 256, BF16_ROWS)

    def body(bcv_ref, qkr_ref, gg_ref, o_ref):
        src = dict(bcv=bcv_ref, qkr=qkr_ref, gg=gg_ref)
        for name, lead, d, (s0, s1), (t0, t1) in pieces:
            v = src[name][:, t0:t1] if lead is None else src[name][lead, :, t0:t1]
            o_ref[d, :, s0:s1] = v.astype(BF16)

    return _call(
        body, "win_merge", (D // tr,),
        [pl.BlockSpec((3, tr, C), lambda i: (0, i, 0)), _rows(tr, WQ), _rows(tr, 2 * D)],
        pl.BlockSpec((N_DEV, tr, cw), lambda i: (0, i, 0)),
        _sds((N_DEV, D, cw), BF16),
    )(d_bcv, d_qkr, d_gg)


def _mix_in(h, gain, w_bcv, w_qkr, w_gg):
    S, D = h.shape
    C = w_bcv.shape[2]
    tm = _tile(S, 512, SUBLANES)

    def body(h_ref, gain_ref, w1, w2, w3, o1, o2, o3, n_ref, r_ref):
        n32, rstd = _rms_fwd(h_ref[...], gain_ref[...])
        n = n32.astype(BF16)
        n_ref[...] = n.T
        r_ref[...] = rstd
        for k in range(3):
            o1[k] = _dot(n, w1[k]).astype(BF16)
        o2[...] = _dot(n, w2[...])
        o3[...] = _dot(n, w3[...]).astype(BF16)

    return _call(
        body, "mix_in", (S // tm,),
        [_rows(tm, D), _whole(gain), _whole(w_bcv), _whole(w_qkr), _whole(w_gg)],
        [pl.BlockSpec((3, tm, C), lambda i: (0, i, 0)), _rows(tm, w_qkr.shape[1]), _rows(tm, 2 * D),
         pl.BlockSpec((D, tm), lambda i: (0, i)), _rows(tm, 1)],
        [_sds((3, S, C), BF16), _sds((S, w_qkr.shape[1]), F32), _sds((S, 2 * D), BF16), _sds((D, S), BF16),
         _sds((S, 1), F32)],
    )(h, gain, w_bcv, w_qkr, w_gg)


def _conv_taps(zc):
    rows = lax.broadcasted_iota(jnp.int32, zc.shape, 0)
    z1 = jnp.where(rows >= 1, pltpu.roll(zc, 1, 0), 0.0)
    z2 = jnp.where(rows >= 2, pltpu.roll(zc, 2, 0), 0.0)
    return z1, z2


def _conv_fwd(z_bcv, conv_w):
    _, S, C = z_bcv.shape

    def body(z_ref, w_ref, o_ref):
        w = w_ref[...]
        zc = z_ref[1].astype(F32) * z_ref[2].astype(F32)
        z1, z2 = _conv_taps(zc)
        y = w[0:1] * z2 + w[1:2] * z1 + w[2:3] * zc
        o_ref[...] = (z_ref[0].astype(F32) * y).astype(BF16)

    return _call(
        body, "conv_fwd", (C // LANES,),
        [pl.BlockSpec((3, S, LANES), lambda j: (0, 0, j)), pl.BlockSpec((3, LANES), lambda j: (0, j))],
        pl.BlockSpec((S, LANES), lambda j: (0, j)),
        _sds((S, C), BF16),
    )(z_bcv, conv_w)


def _rope(x, cs, half):
    c, s1, s2 = cs[:, :LANES], cs[:, LANES:2 * LANES], cs[:, 2 * LANES:]
    return x * c + pltpu.roll(x, LANES - half, 1) * s1 + pltpu.roll(x, half, 1) * s2


def _unrope(d, cs, half):
    c, s1, s2 = cs[:, :LANES], cs[:, LANES:2 * LANES], cs[:, 2 * LANES:]
    return d * c + pltpu.roll(d * s1, half, 1) + pltpu.roll(d * s2, LANES - half, 1)


def _mla_prep(z_qkr, gq, gkv, cs, c256_w):
    S = z_qkr.shape[0]
    QL, KVL = gq.shape[1], gkv.shape[1]
    H = N_DEV
    tm = _tile(S, 512, SUBLANES)
    half = ROPE // 2

    def body(z_ref, gq_ref, gkv_ref, cs_ref, w_ref, q_ref, k_ref, v_ref, qn_ref, kvn_ref, rq_ref, rkv_ref):
        z = z_ref[...]
        cs_t = cs_ref[...]
        qn32, rq = _rms_fwd(z[:, :QL], gq_ref[...])
        kvn32, rkv = _rms_fwd(z[:, QL:QL + KVL], gkv_ref[...])
        qn = qn32.astype(BF16)
        kvn = kvn32.astype(BF16)
        qn_ref[...] = qn
        kvn_ref[...] = kvn
        rq_ref[...] = rq
        rkv_ref[...] = rkv
        krope = _rope(z[:, QL + KVL:], cs_t, half).astype(BF16)
        for h in range(H):
            lo, mid, hi = h * HEAD_SLOT, h * HEAD_SLOT + LANES, (h + 1) * HEAD_SLOT
            q = _dot(qn, w_ref[h, KVL:KVL + QL, :])
            kv = _dot(kvn, w_ref[h, 0:KVL, :])
            q_ref[:, lo:mid] = (q[:, :LANES] * SCORE_SCALE).astype(BF16)
            q_ref[:, mid:hi] = (_rope(q[:, LANES:], cs_t, half) * SCORE_SCALE).astype(BF16)
            k_ref[:, lo:mid] = kv[:, :LANES].astype(BF16)
            k_ref[:, mid:hi] = krope
            v_ref[:, h * VDIM:(h + 1) * VDIM] = kv[:, LANES:].astype(BF16)

    return _call(
        body, "mla_prep", (S // tm,),
        [_rows(tm, z_qkr.shape[1]), _whole(gq), _whole(gkv), _rows(tm, 3 * LANES), _whole(c256_w)],
        [_rows(tm, H * HEAD_SLOT), _rows(tm, H * HEAD_SLOT), _rows(tm, H * VDIM), _rows(tm, QL), _rows(tm, KVL),
         _rows(tm, 1), _rows(tm, 1)],
        [_sds((S, H * HEAD_SLOT), BF16), _sds((S, H * HEAD_SLOT), BF16), _sds((S, H * VDIM), BF16),
         _sds((S, QL), BF16), _sds((S, KVL), BF16), _sds((S, 1), F32), _sds((S, 1), F32)],
    )(z_qkr, gq, gkv, cs, c256_w)


def _chunk_mask(rows, cols, diagonal_row):
    shift = CHUNK.bit_length() - 1
    krow = (lax.broadcasted_iota(jnp.int32, (rows, cols), 0) - diagonal_row) >> shift
    qcol = lax.broadcasted_iota(jnp.int32, (rows, cols), 1) >> shift
    return krow <= qcol


def _attn_fwd(q, k, v, H):
    S = q.shape[0]
    t = _tile(S, 512, CHUNK)
    nq = S // t

    def body(q_ref, k_ref, v_ref, o_ref, ot_ref, lse_ref, vt_ref):
        qi = pl.program_id(1)

        @pl.when(qi == 0)
        def _():
            vt_ref[0:VDIM, :] = v_ref[...].T
            vt_ref[VDIM:, :] = jnp.ones((BF16_ROWS, S), BF16)

        qv = q_ref[...]

        def block(start, width, carry, masked):
            m, acc = carry
            off = pl.multiple_of(start * t, t)
            s = _dot_nt(k_ref[pl.ds(off, width * t), :], qv)
            if masked:
                s = jnp.where(_chunk_mask(width * t, t, (width - 1) * t), s, -1e30)
            m_new = jnp.maximum(m, jnp.max(s, axis=0, keepdims=True))
            p = jnp.exp2(s - m_new).astype(BF16)
            acc = jnp.exp2(m - m_new) * acc + _dot(vt_ref[:, pl.ds(off, width * t)], p)
            return m_new, acc

        init = (jnp.full((1, t), -1e30, F32), jnp.zeros((VDIM + BF16_ROWS, t), F32))
        wide = lax.div(qi, ATTN_FWD_WIDTH)
        carry = lax.fori_loop(0, wide, lambda j, c: block(j * ATTN_FWD_WIDTH, ATTN_FWD_WIDTH, c, False), init)
        left = qi - wide * ATTN_FWD_WIDTH
        for extra in range(ATTN_FWD_WIDTH):
            @pl.when(left == extra)
            def _():
                m, acc = block(qi - extra, extra + 1, carry, True)
                l = acc[VDIM:VDIM + 1]
                out = (acc[0:VDIM] * (1.0 / l)).astype(BF16)
                ot_ref[...] = out
                o_ref[...] = out.T
                lse_ref[0] = jnp.broadcast_to(m + jnp.log2(l), (SUBLANES, t))

    return _call(
        body, "attn_fwd", (H, nq),
        [pl.BlockSpec((t, HEAD_SLOT), lambda h, i: (i, h)), pl.BlockSpec((S, HEAD_SLOT), lambda h, i: (0, h)),
         pl.BlockSpec((S, VDIM), lambda h, i: (0, h))],
        [pl.BlockSpec((t, VDIM), lambda h, i: (i, h)), pl.BlockSpec((VDIM, t), lambda h, i: (h, i)),
         pl.BlockSpec((1, SUBLANES, t), lambda h, i: (h, 0, i))],
        [_sds((S, H * VDIM), BF16), _sds((H * VDIM, S), BF16), _sds((H, SUBLANES, S), F32)],
        [pltpu.VMEM((VDIM + BF16_ROWS, S), BF16)],
    )(q, k, v)


def _merge_wo(o, by, z_gg, h, sq_w, c128_w):
    S, D = h.shape
    C = by.shape[1]
    r = sq_w.shape[1] // 3
    tm = _tile(S, 512, SUBLANES)

    def body(o_ref, by_ref, gg_ref, h_ref, wmo_ref, wo_ref, wco_ref, h2_ref, mg_ref, yc_ref, ym_ref):
        ymla = _dot(o_ref[...], wmo_ref[...].reshape(N_DEV * r, D))
        yconv = _dot(by_ref[...], _cat_slots(wco_ref))
        gg = gg_ref[...].astype(F32)
        merged = (_sig(gg[:, :D]) * yconv + _sig(gg[:, D:]) * ymla).astype(BF16)
        mg_ref[...] = merged.T
        yc_ref[...] = yconv.astype(BF16)
        ym_ref[...] = ymla.astype(BF16)
        h2_ref[...] = h_ref[...] + _dot(merged, wo_ref[...].reshape(N_DEV * r, D))

    return _call(
        body, "merge_wo", (S // tm,),
        [_rows(tm, o.shape[1]), _rows(tm, C), _rows(tm, 2 * D), _rows(tm, D), _slab(sq_w, r, 0), _slab(sq_w, r, 1),
         _slab(c128_w, C, 0)],
        [_rows(tm, D), pl.BlockSpec((D, tm), lambda i: (0, i)), _rows(tm, D), _rows(tm, D)],
        [_sds((S, D), F32), _sds((D, S), BF16), _sds((S, D), BF16), _sds((S, D), BF16)],
    )(o, by, z_gg, h, sq_w, sq_w, c128_w)


def _ple_fwd(h, gain, p, sq_w, c128_w, C):
    S, D = h.shape
    P = p.shape[1]
    r = sq_w.shape[1] // 3
    tm = _tile(S, 512, SUBLANES)

    def body(h_ref, gain_ref, p_ref, wpg_ref, wpp_ref, o_ref, pre_ref, pp_ref, n_ref, r_ref):
        x = h_ref[...]
        n32, rstd = _rms_fwd(x, gain_ref[...])
        n = n32.astype(BF16)
        n_ref[...] = n.T
        r_ref[...] = rstd
        pre = _dot(n, wpg_ref[...].reshape(N_DEV * r, D))
        pp = _dot(p_ref[...].astype(BF16), _cat_slots(wpp_ref))
        pre_ref[...] = pre.astype(BF16)
        pp_ref[...] = pp.astype(BF16)
        o_ref[...] = x + _sig(pre) * pp

    return _call(
        body, "ple_fwd", (S // tm,),
        [_rows(tm, D), _whole(gain), _rows(tm, P), _slab(sq_w, r, 2), _slab(c128_w, P, C // P)],
        [_rows(tm, D), _rows(tm, D), _rows(tm, D), pl.BlockSpec((D, tm), lambda i: (0, i)), _rows(tm, 1)],
        [_sds((S, D), F32), _sds((S, D), BF16), _sds((S, D), BF16), _sds((D, S), BF16), _sds((S, 1), F32)],
    )(h, gain, p, sq_w, c128_w)


def _final_loss(h, gain, target):
    S, D = h.shape
    tm = _tile(S, 512, SUBLANES)

    def body(h_ref, gain_ref, t_ref, dh_ref, loss_ref, dg_ref):
        @pl.when(pl.program_id(0) == 0)
        def _():
            loss_ref[...] = jnp.zeros_like(loss_ref)
            dg_ref[...] = jnp.zeros_like(dg_ref)

        x = h_ref[...]
        gain_v = gain_ref[...]
        y, rstd = _rms_fwd(x, gain_v)
        err = y - t_ref[...]
        loss_ref[...] += 0.5 * jnp.sum(jnp.mean(err * err, axis=-1, keepdims=True))
        dx, dgain = _rms_bwd(err * (1.0 / D), x, rstd, gain_v)
        dh_ref[...] = dx
        dg_ref[...] += dgain

    return _call(
        body, "final_loss", (S // tm,),
        [_rows(tm, D), _whole(gain), _rows(tm, D)],
        [_rows(tm, D), pl.BlockSpec((1, LANES), lambda i: (0, 0)), pl.BlockSpec((1, D), lambda i: (0, 0))],
        [_sds((S, D), F32), _sds((1, LANES), F32), _sds((1, D), F32)],
    )(h, gain, target)


def _tn_call(body, name, grid, in_specs, out_spec, out_shape, scratch, operands, prev):
    n = len(operands)
    if prev is None:
        return _call(body, name, grid, in_specs, out_spec, out_shape, scratch)(*operands)
    assert prev.shape == out_shape.shape and prev.dtype == out_shape.dtype

    def wrapped(*refs):
        body(*refs[:n], *refs[n + 1:])

    return _call(wrapped, name, grid, in_specs + [ANY], out_spec, out_shape, scratch, {n: 0})(*operands, prev)


def _tn_slots(xt, dy, prev, rows_total, row_off):
    K, S = xt.shape
    B, _, c = dy.shape
    tk = _tile(K, 1024, BF16_ROWS)

    def body(xt_ref, dy_ref, o_ref):
        o_ref[0] = _dot(xt_ref[...], dy_ref[0]).astype(BF16)

    return _tn_call(
        body, "tn_slots", (K // tk, B),
        [pl.BlockSpec((tk, S), lambda i, b: (i, 0)), pl.BlockSpec((1, S, c), lambda i, b: (b, 0, 0))],
        pl.BlockSpec((1, tk, c), lambda i, b: (b, row_off // tk + i, 0)),
        _sds((B, rows_total, c), BF16), [], [xt, dy], prev)


def _tn_plain(xt, dy):
    K, S = xt.shape
    B, _, c = dy.shape
    tk = _tile(K, 512, BF16_ROWS)
    tn = _tile(c, 1024, LANES)

    def body(xt_ref, dy_ref, o_ref):
        o_ref[0] = _dot(xt_ref[...], dy_ref[0])

    return _call(
        body, "tn_plain", (K // tk, B, c // tn),
        [pl.BlockSpec((tk, S), lambda i, b, j: (i, 0)), pl.BlockSpec((1, S, tn), lambda i, b, j: (b, 0, j))],
        pl.BlockSpec((1, tk, tn), lambda i, b, j: (b, i, j)),
        _sds((B, K, c), F32),
    )(xt, dy)


def _tn_down(at, dh, prev, rows_total, which):
    nb, c, S = at.shape
    D = dh.shape[1]
    r = c // 2
    tn = _tile(D, 512, LANES)

    def body(at_ref, dh_ref, o_ref):
        g = 0.5 * _dot(at_ref[0], dh_ref[...].astype(BF16))
        o_ref[...] = g.astype(BF16).reshape(2, r, tn)

    return _tn_call(
        body, "tn_down", (D // tn, nb),
        [pl.BlockSpec((1, c, S), lambda j, i: (i, 0, 0)), pl.BlockSpec((S, tn), lambda j, i: (0, j))],
        pl.BlockSpec((2, r, tn), lambda j, i: (i, which, j)),
        _sds((N_DEV, rows_total, D), BF16), [], [at, dh], prev)


def _tn_square(xt, dy, prev, rows_total, member):
    K, S = xt.shape
    N = dy.shape[1]
    r = K // N_DEV
    tk = _tile(K, 512, r)
    tn = _tile(N, 512, LANES)

    def body(xt_ref, dy_ref, o_ref):
        g = _dot(xt_ref[...], dy_ref[...].astype(BF16))
        o_ref[...] = g.astype(BF16).reshape(tk // r, r, tn)

    return _tn_call(
        body, "tn_square", (N // tn, K // tk),
        [pl.BlockSpec((tk, S), lambda j, i: (i, 0)), pl.BlockSpec((S, tn), lambda j, i: (0, j))],
        pl.BlockSpec((tk // r, r, tn), lambda j, i: (i, member, j)),
        _sds((N_DEV, rows_total, N), BF16), [], [xt, dy], prev)


def _tn_cols(x, dy, prev, rows_total, row_block):
    S, K = x.shape
    N = dy.shape[1]
    cw = N // N_DEV

    def body(x_ref, dy_ref, o_ref):
        g = _dot(x_ref[...].astype(BF16).T, dy_ref[...])
        for d in range(N_DEV):
            o_ref[d] = g[:, d * cw:(d + 1) * cw].astype(BF16)

    return _tn_call(
        body, "tn_cols", (1,),
        [pl.BlockSpec((S, K), lambda i: (0, 0)), pl.BlockSpec((S, N), lambda i: (0, 0))],
        pl.BlockSpec((N_DEV, K, cw), lambda i: (0, row_block, 0)),
        _sds((N_DEV, rows_total, cw), BF16), [], [x, dy], prev)


def _tn_heads(qn, kvn, dqp, dkv):
    S, QL = qn.shape
    KVL = kvn.shape[1]

    def body(qn_ref, kvn_ref, dq_ref, dkv_ref, o_ref):
        o_ref[0, 0:KVL, :] = _dot(kvn_ref[...].T, dkv_ref[...]).astype(BF16)
        o_ref[0, KVL:KVL + QL, :] = _dot(qn_ref[...].T, dq_ref[...]).astype(BF16)

    head = pl.BlockSpec((S, HEAD_SLOT), lambda h: (0, h))
    return _call(
        body, "tn_heads", (N_DEV,),
        [pl.BlockSpec((S, QL), lambda h: (0, 0)), pl.BlockSpec((S, KVL), lambda h: (0, 0)), head, head],
        pl.BlockSpec((1, KVL + QL, HEAD_SLOT), lambda h: (h, 0, 0)),
        _sds((N_DEV, KVL + QL, HEAD_SLOT), BF16),
    )(qn, kvn, dqp, dkv)


def _ple_bwd(dh, pre, pp, h, rstd, gain, sq_w, after):
    S, D = h.shape
    r = sq_w.shape[1] // 3
    tm = _tile(S, 512, SUBLANES)

    def body(dh_ref, pre_ref, pp_ref, h_ref, r_ref, gain_ref, wpg_ref, *rest):
        o_ref, dpre_ref, dpp_ref, dg_ref = rest[len(after):]

        @pl.when(pl.program_id(0) == 0)
        def _():
            dg_ref[...] = jnp.zeros_like(dg_ref)

        d = dh_ref[...]
        gate = _sig(pre_ref[...].astype(F32))
        dpre = (d * pp_ref[...].astype(F32) * gate * (1.0 - gate)).astype(BF16)
        dpre_ref[...] = dpre
        dpp_ref[...] = (d * gate).astype(BF16)
        dn = _dot_nt(dpre, wpg_ref[...].reshape(N_DEV * r, D))
        dx, dgain = _rms_bwd(dn, h_ref[...], r_ref[...], gain_ref[...])
        o_ref[...] = d + dx
        dg_ref[...] += dgain

    return _call(
        body, "ple_bwd", (S // tm,),
        [_rows(tm, D), _rows(tm, D), _rows(tm, D), _rows(tm, D), _rows(tm, 1), _whole(gain), _slab(sq_w, r, 2)]
        + [ANY] * len(after),
        [_rows(tm, D), _rows(tm, D), _rows(tm, D), pl.BlockSpec((1, D), lambda i: (0, 0))],
        [_sds((S, D), F32), _sds((S, D), BF16), _sds((S, D), BF16), _sds((1, D), F32)],
    )(dh, pre, pp, h, rstd, gain, sq_w, *after)


def _ffn_bwd(dh, jac, gu_w, dn_w, h, rstd, gain, after=()):
    S, D = h.shape
    _, _, c = jac.shape
    nb = N_DEV // 2
    tm = _tile(S, 256, SUBLANES)

    def body(dh_ref, jac_ref, w_ref, wd_ref, h_ref, r_ref, gain_ref, *rest):
        dgu_ref, o_ref, dgain_ref = rest[len(after):]

        @pl.when(pl.program_id(0) == 0)
        def _():
            dgain_ref[...] = jnp.zeros_like(dgain_ref)

        dh_v = dh_ref[...]
        dhb = dh_v.astype(BF16)
        dn = jnp.zeros((tm, D), F32)
        for d in range(nb):
            da = _dot_nt(dhb, _down_weight(wd_ref, d, c))
            dg = (da * jac_ref[d].astype(F32)).astype(BF16)
            du = (da * jac_ref[nb + d].astype(F32)).astype(BF16)
            dgu_ref[d] = dg
            dgu_ref[nb + d] = du
            dn = dn + _dot_nt(dg, w_ref[d]) + _dot_nt(du, w_ref[nb + d])
        dx, dgain = _rms_bwd(dn, h_ref[...], r_ref[...], gain_ref[...])
        o_ref[...] = dh_v + dx
        dgain_ref[...] += dgain

    act = pl.BlockSpec((N_DEV, tm, c), lambda i: (0, i, 0))
    return _call(
        body, "ffn_bwd", (S // tm,),
        [_rows(tm, D), act, _slab(gu_w, D, 0), _slab(dn_w, c // 2, 0), _rows(tm, D), _rows(tm, 1), _whole(gain)]
        + [ANY] * len(after),
        [act, _rows(tm, D), pl.BlockSpec((1, D), lambda i: (0, 0))],
        [_sds((N_DEV, S, c), BF16), _sds((S, D), F32), _sds((1, D), F32)],
    )(dh, jac, gu_w, dn_w, h, rstd, gain, *after)


def _merge_bwd(dh, z_gg, yconv, ymla, o, sq_w, c128_w, C, after):
    S, D = dh.shape
    r = sq_w.shape[1] // 3
    HV = N_DEV * r
    H = HV // VDIM
    tm = _tile(S, 512, SUBLANES)

    def head_rows():
        row = lax.broadcasted_iota(jnp.int32, (SUBLANES * H, HV), 0) >> (SUBLANES.bit_length() - 1)
        col = lax.broadcasted_iota(jnp.int32, (SUBLANES * H, HV), 1) >> (VDIM.bit_length() - 1)
        return jnp.where(row == col, 1.0, 0.0).astype(BF16)

    def body(dh_ref, gg_ref, yc_ref, ym_ref, o_ref, wmo_ref, wo_ref, wco_ref, *rest):
        dgg_ref, dby_ref, do_ref, dyc_ref, dym_ref, dl_ref = rest[len(after):]
        dm = _dot_nt(dh_ref[...].astype(BF16), wo_ref[...].reshape(HV, D))
        gg = gg_ref[...].astype(F32)
        sgc = _sig(gg[:, :D])
        sgm = _sig(gg[:, D:])
        dyc = (dm * sgc).astype(BF16)
        dym = (dm * sgm).astype(BF16)
        dyc_ref[...] = dyc
        dym_ref[...] = dym
        dgg_ref[:, :D] = (dm * yc_ref[...].astype(F32) * sgc * (1.0 - sgc)).astype(BF16)
        dgg_ref[:, D:] = (dm * ym_ref[...].astype(F32) * sgm * (1.0 - sgm)).astype(BF16)
        dby_ref[...] = _dot_nt(dyc, _cat_slots(wco_ref)).astype(BF16)
        do = _dot_nt(dym, wmo_ref[...].reshape(HV, D)).astype(BF16)
        do_ref[...] = do
        prod = do.astype(F32) * o_ref[...].astype(F32)
        hi = prod.astype(BF16)
        lo = (prod - hi.astype(F32)).astype(BF16)
        pick = head_rows()
        dl_ref[...] = _dot_nt(pick, hi) + _dot_nt(pick, lo)

    return _call(
        body, "merge_bwd", (S // tm,),
        [_rows(tm, D), _rows(tm, 2 * D), _rows(tm, D), _rows(tm, D), _rows(tm, HV), _slab(sq_w, r, 0),
         _slab(sq_w, r, 1), _slab(c128_w, C, 0)] + [ANY] * len(after),
        [_rows(tm, 2 * D), _rows(tm, C), _rows(tm, HV), _rows(tm, D), _rows(tm, D),
         pl.BlockSpec((SUBLANES * H, tm), lambda i: (0, i))],
        [_sds((S, 2 * D), BF16), _sds((S, C), BF16), _sds((S, HV), BF16), _sds((S, D), BF16), _sds((S, D), BF16),
         _sds((SUBLANES * H, S), F32)],
    )(dh, z_gg, yconv, ymla, o, sq_w, sq_w, c128_w, *after)


def _conv_bwd(z_bcv, conv_w, dby):
    _, S, C = z_bcv.shape

    def body(z_ref, w_ref, dby_ref, dz_ref, dw_ref):
        w = w_ref[...]
        c = z_ref[1].astype(F32)
        v = z_ref[2].astype(F32)
        d = dby_ref[...].astype(F32)
        zc = c * v
        z1, z2 = _conv_taps(zc)
        y = w[0:1] * z2 + w[1:2] * z1 + w[2:3] * zc
        dz_ref[0] = (d * y).astype(BF16)
        dy = d * z_ref[0].astype(F32)
        rows = lax.broadcasted_iota(jnp.int32, dy.shape, 0)
        dy1 = jnp.where(rows < S - 1, pltpu.roll(dy, S - 1, 0), 0.0)
        dy2 = jnp.where(rows < S - 2, pltpu.roll(dy, S - 2, 0), 0.0)
        dzc = w[2:3] * dy + w[1:2] * dy1 + w[0:1] * dy2
        dz_ref[1] = (dzc * v).astype(BF16)
        dz_ref[2] = (dzc * c).astype(BF16)
        dw_ref[0:1, :] = jnp.sum(dy * z2, axis=0, keepdims=True)
        dw_ref[1:2, :] = jnp.sum(dy * z1, axis=0, keepdims=True)
        dw_ref[2:3, :] = jnp.sum(dy * zc, axis=0, keepdims=True)

    three = pl.BlockSpec((3, S, LANES), lambda j: (0, 0, j))
    wspec = pl.BlockSpec((3, LANES), lambda j: (0, j))
    return _call(
        body, "conv_bwd", (C // LANES,),
        [three, wspec, pl.BlockSpec((S, LANES), lambda j: (0, j))],
        [three, wspec],
        [_sds((3, S, C), BF16), _sds((3, C), F32)],
    )(z_bcv, conv_w, dby)


def _attn_bwd(q, k, v, do, lse, delta, H):
    S = q.shape[0]
    t = _tile(S, 512, CHUNK)
    nk = S // t

    def body(q_ref, k_ref, v_ref, do_ref, lse_ref, dl_ref, dq_ref, dk_ref, dv_ref, dqt_ref):
        kj = pl.program_id(1)

        @pl.when(kj == 0)
        def _():
            dqt_ref[...] = jnp.zeros_like(dqt_ref)

        kv = k_ref[...]
        vv = v_ref[...]
        kt = kv.T

        def block(start, width, carry, masked):
            dk, dv = carry
            off = pl.multiple_of(start * t, t)
            qv = q_ref[pl.ds(off, width * t), :]
            dov = do_ref[pl.ds(off, width * t), :]
            s = _dot_nt(kv, qv)
            if masked:
                s = jnp.where(_chunk_mask(t, width * t, 0), s, -1e30)
            p = jnp.exp2(s - lse_ref[0, 0:1, pl.ds(off, width * t)])
            dp = _dot_nt(vv, dov)
            ds = (p * (dp - dl_ref[0, 0:1, pl.ds(off, width * t)]) * LN2).astype(BF16)
            dqt_ref[:, pl.ds(off, width * t)] += _dot(kt, ds)
            return dk + _dot(ds, qv), dv + _dot(p.astype(BF16), dov)

        init = (jnp.zeros((t, HEAD_SLOT), F32), jnp.zeros((t, VDIM), F32))
        wide = lax.div(nk - 1 - kj, ATTN_BWD_WIDTH)
        left = nk - 1 - kj - wide * ATTN_BWD_WIDTH
        carry = lax.switch(left, [functools.partial(block, kj, extra + 1, init, True)
                                  for extra in range(ATTN_BWD_WIDTH)])
        dk, dv = lax.fori_loop(
            0, wide, lambda j, c: block(kj + 1 + left + j * ATTN_BWD_WIDTH, ATTN_BWD_WIDTH, c, False), carry)
        dk_ref[...] = dk.astype(BF16)
        dv_ref[...] = dv.astype(BF16)

        @pl.when(kj == nk - 1)
        def _():
            dq_ref[...] = (dqt_ref[...] * SCORE_SCALE).T.astype(BF16)

    kspec = lambda w: pl.BlockSpec((t, w), lambda h, j: (j, h))
    qspec = lambda w: pl.BlockSpec((S, w), lambda h, j: (0, h))
    stat = pl.BlockSpec((1, SUBLANES, S), lambda h, j: (h, 0, 0))
    return _call(
        body, "attn_bwd", (H, nk),
        [qspec(HEAD_SLOT), kspec(HEAD_SLOT), kspec(VDIM), qspec(VDIM), stat, stat],
        [qspec(HEAD_SLOT), kspec(HEAD_SLOT), kspec(VDIM)],
        [_sds((S, H * HEAD_SLOT), BF16), _sds((S, H * HEAD_SLOT), BF16), _sds((S, H * VDIM), BF16)],
        [pltpu.VMEM((HEAD_SLOT, S), F32)],
    )(q, k, v, do, lse, delta)


def _mla_prep_bwd(dq, dk, dv, z_qkr, rq, rkv, gq, gkv, cs, c256_w):
    S = z_qkr.shape[0]
    QL, KVL = gq.shape[1], gkv.shape[1]
    H = N_DEV
    tm = _tile(S, 512, SUBLANES)
    half = ROPE // 2

    def body(dq_ref, dk_ref, dv_ref, z_ref, rq_ref, rkv_ref, gq_ref, gkv_ref, cs_ref, w_ref,
             dz_ref, dqp_ref, dkv_ref, dgq_ref, dgkv_ref):
        @pl.when(pl.program_id(0) == 0)
        def _():
            dgq_ref[...] = jnp.zeros_like(dgq_ref)
            dgkv_ref[...] = jnp.zeros_like(dgkv_ref)

        cs_t = cs_ref[...]
        dkr = jnp.zeros((tm, LANES), F32)
        dqn = jnp.zeros((tm, QL), F32)
        dkvn = jnp.zeros((tm, KVL), F32)
        for h in range(H):
            lo, mid, hi = h * HEAD_SLOT, h * HEAD_SLOT + LANES, (h + 1) * HEAD_SLOT
            dqp_ref[:, lo:mid] = dq_ref[:, lo:mid]
            dqp_ref[:, mid:hi] = _unrope(dq_ref[:, mid:hi].astype(F32), cs_t, half).astype(BF16)
            dkv_ref[:, lo:mid] = dk_ref[:, lo:mid]
            dkv_ref[:, mid:hi] = dv_ref[:, h * VDIM:(h + 1) * VDIM]
            dkr = dkr + dk_ref[:, mid:hi].astype(F32)
            dqn = dqn + _dot_nt(dqp_ref[:, lo:hi], w_ref[h, KVL:KVL + QL, :])
            dkvn = dkvn + _dot_nt(dkv_ref[:, lo:hi], w_ref[h, 0:KVL, :])
        z = z_ref[...]
        dqc, dgq = _rms_bwd(dqn, z[:, :QL], rq_ref[...], gq_ref[...])
        dkvc, dgkv = _rms_bwd(dkvn, z[:, QL:QL + KVL], rkv_ref[...], gkv_ref[...])
        dz_ref[:, :QL] = dqc.astype(BF16)
        dz_ref[:, QL:QL + KVL] = dkvc.astype(BF16)
        dz_ref[:, QL + KVL:] = _unrope(dkr, cs_t, half).astype(BF16)
        dgq_ref[...] += dgq
        dgkv_ref[...] += dgkv

    W = z_qkr.shape[1]
    return _call(
        body, "mla_prep_bwd", (S // tm,),
        [_rows(tm, H * HEAD_SLOT), _rows(tm, H * HEAD_SLOT), _rows(tm, H * VDIM), _rows(tm, W), _rows(tm, 1),
         _rows(tm, 1), _whole(gq), _whole(gkv), _rows(tm, 3 * LANES), _whole(c256_w)],
        [_rows(tm, W), _rows(tm, H * HEAD_SLOT), _rows(tm, H * HEAD_SLOT), _whole(gq), _whole(gkv)],
        [_sds((S, W), BF16), _sds((S, H * HEAD_SLOT), BF16), _sds((S, H * HEAD_SLOT), BF16),
         _sds((1, QL), F32), _sds((1, KVL), F32)],
    )(dq, dk, dv, z_qkr, rq, rkv, gq, gkv, cs, c256_w)


def _mix_in_bwd(d_bcv, dz_qkr, dgg, w_bcv, w_qkr, w_gg, h, rstd, gain, dh):
    S, D = h.shape
    C = d_bcv.shape[2]
    tm = _tile(S, 512, SUBLANES)

    def body(db_ref, dq_ref, dgg_ref, wb_ref, wq_ref, wg_ref, h_ref, r_ref, gain_ref, dh_ref, o_ref, dgain_ref):
        @pl.when(pl.program_id(0) == 0)
        def _():
            dgain_ref[...] = jnp.zeros_like(dgain_ref)

        dn = _dot_nt(dq_ref[...], wq_ref[...]) + _dot_nt(dgg_ref[...], wg_ref[...])
        for k in range(3):
            dn = dn + _dot_nt(db_ref[k], wb_ref[k])
        dx, dgain = _rms_bwd(dn, h_ref[...], r_ref[...], gain_ref[...])
        o_ref[...] = dh_ref[...] + dx
        dgain_ref[...] += dgain

    return _call(
        body, "mix_in_bwd", (S // tm,),
        [pl.BlockSpec((3, tm, C), lambda i: (0, i, 0)), _rows(tm, dz_qkr.shape[1]), _rows(tm, dgg.shape[1]),
         _whole(w_bcv), _whole(w_qkr), _whole(w_gg), _rows(tm, D), _rows(tm, 1), _whole(gain), _rows(tm, D)],
        [_rows(tm, D), pl.BlockSpec((1, D), lambda i: (0, 0))],
        [_sds((S, D), F32), _sds((1, D), F32)],
    )(d_bcv, dz_qkr, dgg, w_bcv, w_qkr, w_gg, h, rstd, gain, dh)


def _rope_tables(positions):
    half = ROPE // 2
    inv_freq = ROPE_THETA ** (-jnp.arange(0, ROPE, 2, dtype=F32) / ROPE)
    ang = positions.astype(F32)[:, None] * inv_freq
    cos, sin = jnp.cos(ang), jnp.sin(ang)
    z = jnp.zeros_like(cos)
    pad = jnp.zeros((positions.shape[0], LANES - 2 * half), F32)
    return jnp.concatenate([cos, cos, pad, -sin, z, pad, z, sin, pad], axis=1)


def _grad_rows(w):
    return dict(gu=2 * w["gu1"].shape[1], dn=2 * w["dn1"].shape[1], sq=w["sq"].shape[1], win=w["win"].shape[1],
                c128=w["c128"].shape[1], c256=w["c256"].shape[1])


def _layer_fwd(h0, p_l, cs, w, sm, late):
    C = sm["conv_w"].shape[1]
    QL, KVL = sm["q_norm"].shape[1], sm["kv_norm"].shape[1]
    h1, jac1, at1, n1, r1 = _ffn_fwd(h0, sm["ffn1_norm"], w["gu1"], w["dn1"])
    if late is not None:
        w.update(late(h1))
    w_bcv, w_qkr, w_gg = _win_split(w["win"], C, QL, KVL)
    z_bcv, z_qkr, z_gg, un, rm = _mix_in(h1, sm["mix_norm"], w_bcv, w_qkr, w_gg)
    by = _conv_fwd(z_bcv, sm["conv_w"])
    q, k, v, qn, kvn, rq, rkv = _mla_prep(z_qkr, sm["q_norm"], sm["kv_norm"], cs, w["c256"])
    o, ot, lse = _attn_fwd(q, k, v, N_DEV)
    h2, merged, yconv, ymla = _merge_wo(o, by, z_gg, h1, w["sq"], w["c128"])
    h3, jac2, at2, n2, r2 = _ffn_fwd(h2, sm["ffn2_norm"], w["gu2"], w["dn2"])
    h4, pre, pp, pn, rp = _ple_fwd(h3, sm["ple_norm"], p_l, w["sq"], w["c128"], C)
    saved = dict(h0=h0, jac1=jac1, at1=at1, n1=n1, r1=r1, h1=h1, w_bcv=w_bcv, w_qkr=w_qkr, w_gg=w_gg, z_bcv=z_bcv,
                 z_qkr=z_qkr, z_gg=z_gg, un=un, rm=rm, by=by, q=q, k=k, v=v, qn=qn, kvn=kvn, rq=rq, rkv=rkv, o=o, ot=ot,
                 lse=lse, h2=h2, merged=merged, yconv=yconv, ymla=ymla, jac2=jac2, at2=at2, n2=n2, r2=r2, h3=h3,
                 pre=pre, pp=pp, pn=pn, rp=rp, p=p_l)
    return h4, saved


def _layer_bwd_late(dh4, s, w, sm, after):
    D = dh4.shape[1]
    C = sm["conv_w"].shape[1]
    P = s["p"].shape[1]
    rows = _grad_rows(w)
    small = {}
    dh3, dpre, dpp, small["ple_norm"] = _ple_bwd(dh4, s["pre"], s["pp"], s["h3"], s["rp"], sm["ple_norm"], w["sq"],
                                                 after)
    g_sq = _tn_square(s["pn"], dpre, None, rows["sq"], 2)
    g_c128 = _tn_cols(s["p"], dpp, None, rows["c128"], C // P)

    dgu2, dh2, small["ffn2_norm"] = _ffn_bwd(dh3, s["jac2"], w["gu2"], w["dn2"], s["h2"], s["r2"], sm["ffn2_norm"])
    g_dn = _tn_down(s["at2"], dh3, None, rows["dn"], 1)
    g_gu = _tn_slots(s["n2"], dgu2, None, rows["gu"], D)
    return dh2, dict(gu=g_gu, dn=g_dn, sq=g_sq, c128=g_c128), small


def _layer_bwd_mixer(dh2, part, small, s, cs, w, sm, after):
    C = sm["conv_w"].shape[1]
    rows = _grad_rows(w)
    g_gu, g_dn, g_sq, g_c128 = part["gu"], part["dn"], part["sq"], part["c128"]

    dgg, dby, do, dyc, dym, delta = _merge_bwd(dh2, s["z_gg"], s["yconv"], s["ymla"], s["o"], w["sq"], w["c128"], C,
                                               after)
    g_sq = _tn_square(s["merged"], dh2, g_sq, rows["sq"], 1)
    g_sq = _tn_square(s["ot"], dym, g_sq, rows["sq"], 0)
    g_c128 = _tn_cols(s["by"], dyc, g_c128, rows["c128"], 0)
    d_bcv, small["conv_w"] = _conv_bwd(s["z_bcv"], sm["conv_w"], dby)
    delta = delta.reshape(N_DEV, SUBLANES, delta.shape[1])
    dq, dk, dv = _attn_bwd(s["q"], s["k"], s["v"], do, s["lse"], delta, N_DEV)
    dz_qkr, dqp, dkv, small["q_norm"], small["kv_norm"] = _mla_prep_bwd(
        dq, dk, dv, s["z_qkr"], s["rq"], s["rkv"], sm["q_norm"], sm["kv_norm"], cs, w["c256"])
    g_c256 = _tn_heads(s["qn"], s["kvn"], dqp, dkv)
    un = s["un"]
    g_win = _win_merge(_tn_plain(un, d_bcv), _tn_plain(un, dz_qkr[None])[0], _tn_plain(un, dgg[None])[0],
                       w["win"].shape[2])
    dh1, small["mix_norm"] = _mix_in_bwd(d_bcv, dz_qkr, dgg, s["w_bcv"], s["w_qkr"], s["w_gg"], s["h1"], s["rm"],
                                         sm["mix_norm"], dh2)
    return dh1, dict(gu=g_gu, dn=g_dn, sq=g_sq, win=g_win, c128=g_c128, c256=g_c256), small


def _layer_bwd_first(dh1, part, small, s, w, sm, after):
    rows = _grad_rows(w)
    dgu1, dh0, small["ffn1_norm"] = _ffn_bwd(dh1, s["jac1"], w["gu1"], w["dn1"], s["h0"], s["r1"], sm["ffn1_norm"],
                                             after)
    g_dn = _tn_down(s["at1"], dh1, part["dn"], rows["dn"], 0)
    g_gu = _tn_slots(s["n1"], dgu1, part["gu"], rows["gu"], 0)
    return dh0, dict(part, gu=g_gu, dn=g_dn), small


def _mesh_pos():
    return lax.axis_index("x"), lax.axis_index("y"), lax.axis_index("c")


def _other_chips(x, y):
    return [(1 - x, y), (x, 1 - y), (1 - x, 1 - y)]


def _pack(arrs, flipped, width):
    L = arrs[0].shape[0]
    shapes = [a.shape[:0:-1] if f else a.shape[1:] for a, f in zip(arrs, flipped)]
    R = sum(r for r, _ in shapes)

    def body(*refs):
        o_ref = refs[-1]
        off = 0
        for a_ref, f, (r, c) in zip(refs[:-1], flipped, shapes):
            a = a_ref[0].T if f else a_ref[0]
            o_ref[0, off:off + r, 0:c] = a.astype(BF16)
            if c < width:
                o_ref[0, off:off + r, c:width] = jnp.zeros((r, width - c), BF16)
            off += r

    return _call(
        body, "pack", (L,),
        [pl.BlockSpec((1,) + a.shape[1:], lambda l: (l, 0, 0)) for a in arrs],
        pl.BlockSpec((1, R, width), lambda l: (l, 0, 0)),
        _sds((L, R, width), BF16),
    )(*arrs)


def _handshake(peers):
    barrier = pltpu.get_barrier_semaphore()
    for peer in peers:
        pl.semaphore_signal(barrier, inc=1, device_id=peer, device_id_type=MESH)
    pl.semaphore_wait(barrier, len(peers))


def _sequencer_call(body, name, out_types, sems, collective_id, operands):
    return pl.kernel(
        body, name=name, out_type=out_types,
        mesh=plsc.ScalarSubcoreMesh(axis_name="seq", num_cores=1),
        scratch_types=tuple(pltpu.SemaphoreType.DMA((k,)) for k in sems),
        compiler_params=pltpu.CompilerParams(collective_id=collective_id),
    )(*operands)


def _all_gather(packs, l, after, collective_id):
    n = len(packs)

    def body(*refs):
        ins, outs = refs[:n], refs[n + len(after):2 * n + len(after)]
        send_sems, recv_sems, local_sems = refs[2 * n + len(after):]
        x, y, c = _mesh_pos()
        me, sibling = (x, y, c), (x, y, 1 - c)
        chips = _other_chips(x, y)
        _handshake([sibling] + [(*chip, c) for chip in chips])

        def copy(q, k, block, to, src=None):
            slot = outs[q].at[4 * block[0] + 2 * block[1] + block[2]]
            return pltpu.make_async_remote_copy(
                src_ref=slot if src is None else src, dst_ref=slot,
                send_sem=send_sems.at[7 * q + k], recv_sem=recv_sems.at[7 * q + k], device_id=to, device_id_type=MESH)

        started = []
        for q in range(n):
            src = ins[q].at[l]
            mine = pltpu.make_async_copy(src, outs[q].at[4 * x + 2 * y + c], local_sems.at[q])
            mine.start()
            started.append(mine)
        sends = []
        for q in range(n):
            src = ins[q].at[l]
            sends.append(copy(q, 0, me, sibling, src=src))
            sends += [copy(q, 1 + j, me, (*chip, c), src=src) for j, chip in enumerate(chips)]
        for cp in sends:
            cp.start()
        for q in range(n):
            for j, chip in enumerate(chips):
                copy(q, 1 + j, (*chip, c), me).wait_recv()
                fwd = copy(q, 4 + j, (*chip, c), sibling)
                fwd.start()
                sends.append(fwd)
        for q in range(n):
            copy(q, 0, sibling, me).wait_recv()
            for j, chip in enumerate(chips):
                copy(q, 4 + j, (*chip, 1 - c), me).wait_recv()
        for cp in sends:
            cp.wait_send()
        for mine in started:
            mine.wait()

    return _sequencer_call(
        body, f"all_gather_{collective_id}", [_sds((N_DEV,) + p.shape[1:], p.dtype) for p in packs], (7 * n, 7 * n, n),
        collective_id, list(packs) + list(after))


def _rs_d2d(gs, l, collective_id):
    n = len(gs)

    def body(*refs):
        ins, outs = refs[:n], refs[n:2 * n]
        send_sems, recv_sems = refs[2 * n:]
        x, y, c = _mesh_pos()
        _handshake([(x, y, 1 - c)])
        copies = []
        for q in range(n):
            for j in range(4):
                copies.append(pltpu.make_async_remote_copy(
                    src_ref=ins[q].at[2 * j + (1 - c)], dst_ref=outs[q].at[j], send_sem=send_sems.at[4 * q + j],
                    recv_sem=recv_sems.at[4 * q + j], device_id=(x, y, 1 - c), device_id_type=MESH))
        for cp in copies:
            cp.start()
        for cp in copies:
            cp.wait()

    return _sequencer_call(
        body, f"rs_d2d_{l}", [_sds((4,) + g.shape[1:], g.dtype) for g in gs], (4 * n, 4 * n), collective_id, gs)


def _rs_add_chip(gs, as_, after):
    n = len(gs)
    steps = 4
    tiles = [g.shape[1] // steps for g in gs]

    def chip(k):
        x, y, _ = _mesh_pos()
        return ([(x, y)] + _other_chips(x, y))[k]

    def body(*refs):
        g_refs, a_refs = refs[:4 * n], refs[4 * n:8 * n]
        own_refs, t_refs = refs[8 * n + len(after):9 * n + len(after)], refs[9 * n + len(after):]
        for q in range(n):
            g, a = g_refs[4 * q:4 * q + 4], a_refs[4 * q:4 * q + 4]
            own_refs[q][...] = g[0][0].astype(F32) + a[0][0].astype(F32)
            for k in range(1, 4):
                t_refs[q][k - 1] = (g[k][0].astype(F32) + a[k][0].astype(F32)).astype(BF16)

    def gspec(q, k):
        def index(i):
            px, py = chip(k)
            return 4 * px + 2 * py + lax.axis_index("c"), i, 0
        return pl.BlockSpec((1, tiles[q], gs[q].shape[2]), index)

    def aspec(q, k):
        def index(i):
            px, py = chip(k)
            return 2 * px + py, i, 0
        return pl.BlockSpec((1, tiles[q], gs[q].shape[2]), index)

    in_specs = [gspec(q, k) for q in range(n) for k in range(4)] + [aspec(q, k) for q in range(n) for k in range(4)]
    operands = [g for g in gs for _ in range(4)] + [a for a in as_ for _ in range(4)]
    out_specs = [pl.BlockSpec((tiles[q], gs[q].shape[2]), lambda i: (i, 0)) for q in range(n)]
    out_specs += [pl.BlockSpec((3, tiles[q], gs[q].shape[2]), lambda i: (0, i, 0)) for q in range(n)]
    out_shape = [_sds(g.shape[1:], F32) for g in gs] + [_sds((3,) + g.shape[1:], BF16) for g in gs]
    res = _call(body, "rs_add_chip", (steps,), in_specs + [ANY] * len(after), out_specs, out_shape)(*operands, *after)
    return res[:n], res[n:]


def _rs_ici(ts, l, collective_id):
    n = len(ts)

    def body(*refs):
        ins, outs = refs[:n], refs[n:2 * n]
        send_sems, recv_sems = refs[2 * n:]
        x, y, c = _mesh_pos()
        chips = _other_chips(x, y)
        _handshake([(*chip, c) for chip in chips])
        copies = []
        for q in range(n):
            for k, chip in enumerate(chips):
                copies.append(pltpu.make_async_remote_copy(
                    src_ref=ins[q].at[k], dst_ref=outs[q].at[k], send_sem=send_sems.at[3 * q + k],
                    recv_sem=recv_sems.at[3 * q + k], device_id=(*chip, c), device_id_type=MESH))
        for cp in copies:
            cp.start()
        for cp in copies:
            cp.wait()

    return _sequencer_call(
        body, f"rs_ici_{l}", [_sds(t.shape, t.dtype) for t in ts], (3 * n, 3 * n), collective_id, ts)


def _all_reduce_small(v):
    n, W = v.shape

    def body(v_ref, out_ref, slots, send_sems, recv_sems):
        x, y, c = _mesh_pos()
        me = 4 * x + 2 * y + c
        slots[me] = v_ref[...]
        copies = []
        for k in range(1, N_DEV):
            kx, ky, kc = (k >> 2) & 1, (k >> 1) & 1, k & 1
            peer = (1 - x if kx else x, 1 - y if ky else y, 1 - c if kc else c)
            copies.append(pltpu.make_async_remote_copy(
                src_ref=v_ref, dst_ref=slots.at[me], send_sem=send_sems.at[k - 1], recv_sem=recv_sems.at[k - 1],
                device_id=peer, device_id_type=MESH))
        for cp in copies:
            cp.start()
        for cp in copies:
            cp.wait()
        acc = slots[0]
        for d in range(1, N_DEV):
            acc = acc + slots[d]
        out_ref[...] = acc

    vm = pl.BlockSpec(memory_space=pltpu.VMEM)
    return pl.pallas_call(
        body, name="all_reduce_small",
        out_shape=_sds((n, W), F32),
        in_specs=[vm], out_specs=vm,
        scratch_shapes=[pltpu.VMEM((N_DEV, n, W), F32), pltpu.SemaphoreType.DMA((7,)), pltpu.SemaphoreType.DMA((7,))],
    )(v)


def _adamw_math(w, g, m, v):
    m2 = ADAM_B1 * m + (1.0 - ADAM_B1) * g
    v2 = ADAM_B2 * v + (1.0 - ADAM_B2) * (g * g)
    m_hat = m2 / (1.0 - ADAM_B1 ** ADAM_STEP)
    v_hat = v2 / (1.0 - ADAM_B2 ** ADAM_STEP)
    return -ADAM_LR * (m_hat / (jnp.sqrt(v_hat) + ADAM_EPS) + ADAM_WD * w), m2, v2


def _adamw(w, g, m, v):
    L, r, c = w.shape
    tr = _tile(r, max(SUBLANES, (256 * 1024 // c) // SUBLANES * SUBLANES), SUBLANES)

    def body(w_ref, g_ref, m_ref, v_ref, d_ref, nm_ref, nv_ref):
        d_ref[...], nm_ref[...], nv_ref[...] = _adamw_math(w_ref[...], g_ref[...], m_ref[...], v_ref[...])

    spec = pl.BlockSpec((1, tr, c), lambda l, i: (l, i, 0))
    return _call(body, "adamw", (L, r // tr), [spec] * 4, [spec] * 3, [_sds((L, r, c), F32)] * 3)(w, g, m, v)


def _adamw_reduced(w, m, v, flipped, own, b, row_off, tr, l, prev, after):
    L = w.shape[0]
    c, r = w.shape[1:] if flipped else w.shape[:0:-1]
    W = own.shape[1]
    ob = row_off // tr
    extra = list(prev or ()) + list(after)

    def body(w_ref, m_ref, v_ref, own_ref, b_ref, *rest):
        g_ref, d_ref, nm_ref, nv_ref = rest[len(extra):]
        g = ((own_ref[...] + b_ref[0].astype(F32)) + b_ref[1].astype(F32)) + b_ref[2].astype(F32)
        g = g[:, :c].T if flipped else g[:, :c]
        g_ref[0] = g
        d_ref[0], nm_ref[0], nv_ref[0] = _adamw_math(w_ref[0], g, m_ref[0], v_ref[0])

    spec = pl.BlockSpec((1, c, tr), lambda i: (l, 0, i)) if flipped else pl.BlockSpec((1, tr, c), lambda i: (l, i, 0))
    return _call(
        body, "adamw_reduced", (r // tr,),
        [spec] * 3 + [pl.BlockSpec((tr, W), lambda i: (ob + i, 0)), pl.BlockSpec((3, tr, W), lambda i: (0, ob + i, 0))]
        + [ANY] * len(extra),
        [spec] * 4, [_sds(w.shape, F32)] * 4,
        aliases={5 + k: k for k in range(4)} if prev else None,
    )(w, m, v, own, b, *extra)


_MEMBERS = dict(gu=("ffn1_w_gu", "ffn2_w_gu"), dn=("ffn1_w_down", "ffn2_w_down"),
                sq=("w_mla_out", "w_o", "w_ple_gate"), win=("w_in",), c128=("w_conv_out", "w_ple_proj"),
                c256=("w_ukv", "w_uq"))
_GATHER_MEMBERS = dict(_MEMBERS, gu1=("ffn1_w_gu",), gu2=("ffn2_w_gu",), dn1=("ffn1_w_down",), dn2=("ffn2_w_down",))
GATHER_STAGES = (("gu1", "dn1"), ("win", "c256", "c128", "sq"), ("gu2", "dn2"))
_FLIPPED = ("ffn1_w_gu", "ffn2_w_gu", "w_in", "w_uq")
_SMALL = ("ffn1_norm", "mix_norm", "q_norm", "kv_norm", "ffn2_norm", "ple_norm")
_ORDER = ("ffn1_norm", "ffn1_w_gu", "ffn1_w_down", "mix_norm", "w_in", "conv_w", "w_conv_out", "q_norm", "kv_norm",
          "w_uq", "w_ukv", "w_mla_out", "w_o", "ffn2_norm", "ffn2_w_gu", "ffn2_w_down", "ple_norm", "w_ple_gate",
          "w_ple_proj", "final_norm")


def _class_width(wts, cls):
    return HEAD_SLOT if cls == "c256" else wts[_GATHER_MEMBERS[cls][0]].shape[2]


def _pack_rows(vecs, width):
    flat = jnp.concatenate([a.reshape(-1) for a in vecs])
    n = flat.shape[0]
    rows = -(-n // width)
    rows = -(-rows // SUBLANES) * SUBLANES
    flat = jnp.pad(flat, (0, rows * width - n))
    offs, o = [], 0
    for a in vecs:
        offs.append(o)
        o += a.size
    return flat.reshape(rows, width), offs


def _unpack_rows(packed, vecs, offs):
    flat = packed.reshape(-1)
    return [flat[o:o + a.size].reshape(a.shape) for a, o in zip(vecs, offs)]


def _train(x, p, positions, target, gathered, packs, small_w, final_norm, update):
    cs = _rope_tables(positions)
    L = len(small_w)
    h = x
    saved = []
    def gather(l, names, after, collective_id):
        got = _all_gather([packs[n] for n in names], l, after, collective_id)
        return dict(zip(names, got))

    late = None
    if packs is not None:
        first, mixer, second = GATHER_STAGES
        w0 = gather(0, first, [], 0)
        w0.update(gather(0, mixer, [w0[first[0]]], 1))
        gathered = [w0]
        late = lambda h1: gather(0, second, [h1], 2)
    everything = sum(GATHER_STAGES, ())
    for l in range(L):
        h, s = _layer_fwd(h, p[l], cs, gathered[l], small_w[l], late)
        late = None
        saved.append(s)
        if packs is not None and l + 1 < L:
            gathered.append(gather(l + 1, everything, [s["by"]], 2 + l + 1))
    dh, loss, d_final = _final_loss(h, final_norm, target)
    grads, smalls = [None] * L, [None] * L
    exchanged = None
    landing = None

    def second_stage(after):
        l, gs, as_ = exchanged
        owns, ts = _rs_add_chip(gs, as_, [after])
        return l, owns, _rs_ici(ts, l, 2 * L + 2 + l)

    for l in reversed(range(L)):
        dh, part, small = _layer_bwd_late(dh, saved[l], gathered[l], small_w[l], [])
        pin = []
        if exchanged is not None:
            landing = second_stage(dh)
            pin = [landing[1][0]]
        dh, part, small = _layer_bwd_mixer(dh, part, small, saved[l], cs, gathered[l], small_w[l], pin)
        pin = [update(*landing)] if exchanged is not None else []
        dh, g, smalls[l] = _layer_bwd_first(dh, part, small, saved[l], gathered[l], small_w[l], pin)
        if update is not None:
            gs = [g[cls] for cls in CLASSES]
            exchanged = (l, gs, _rs_d2d(gs, l, L + 2 + l))
        else:
            grads[l] = g
    if update is not None:
        update(*second_stage(dh))
    return loss[0, 0], dh, grads, smalls, d_final


def kernel(x, p, positions, ffn1_norm, ffn1_w_gu, ffn1_w_down, mix_norm, w_in, conv_w, w_conv_out, q_norm, kv_norm, w_uq, w_ukv, w_mla_out, w_o, ffn2_norm, ffn2_w_gu, ffn2_w_down, ple_norm, w_ple_gate, w_ple_proj, final_norm, loss_target, m_ffn1_norm, m_ffn1_w_gu, m_ffn1_w_down, m_mix_norm, m_w_in, m_conv_w, m_w_conv_out, m_q_norm, m_kv_norm, m_w_uq, m_w_ukv, m_w_mla_out, m_w_o, m_ffn2_norm, m_ffn2_w_gu, m_ffn2_w_down, m_ple_norm, m_w_ple_gate, m_w_ple_proj, m_final_norm, v_ffn1_norm, v_ffn1_w_gu, v_ffn1_w_down, v_mix_norm, v_w_in, v_conv_w, v_w_conv_out, v_q_norm, v_kv_norm, v_w_uq, v_w_ukv, v_w_mla_out, v_w_o, v_ffn2_norm, v_ffn2_w_gu, v_ffn2_w_down, v_ple_norm, v_w_ple_gate, v_w_ple_proj, v_final_norm):
    args = dict(locals())
    wts = {n: args[n] for n in _ORDER}
    L = w_in.shape[0]
    dev = 4 * lax.axis_index("x") + 2 * lax.axis_index("y") + lax.axis_index("c")

    view = lambda n, a: jnp.swapaxes(a, 1, 2) if n in _FLIPPED else a
    packs = {cls: _pack([view(n, wts[n]) for n in _GATHER_MEMBERS[cls]], [n in _FLIPPED for n in _GATHER_MEMBERS[cls]],
                        _class_width(wts, cls))
             for stage in GATHER_STAGES for cls in stage}
    cw = conv_w.shape[2]
    conv_full = lax.dynamic_update_slice(jnp.zeros((L, 3, N_DEV * cw), F32), conv_w, (0, 0, dev * cw))
    conv_packed, conv_offs = _pack_rows([conv_full], FLAT_COLS)
    conv_full = _unpack_rows(_all_reduce_small(conv_packed), [conv_full], conv_offs)[0]
    small_w = [dict({n: wts[n][l][None, :] for n in _SMALL}, conv_w=conv_full[l]) for l in range(L)]

    done = {}

    def update(l, owns, bs):
        for q, cls in enumerate(CLASSES):
            off = 0
            rows = [wts[n].shape[1] for n in _MEMBERS[cls]]
            tr = _tile(math.gcd(*rows), 256, BF16_ROWS)
            for n, r in zip(_MEMBERS[cls], rows):
                done[n] = _adamw_reduced(view(n, wts[n]), view(n, args["m_" + n]), view(n, args["v_" + n]),
                                         n in _FLIPPED, owns[q], bs[q], off, tr, l, done.get(n), [])
                off += r
        return done[_MEMBERS[CLASSES[-1]][-1]][0]

    loss_dev, grad_x, _, smalls, d_final = _train(x[0], p[:, 0], positions[0], loss_target[0], None, packs, small_w,
                                                  final_norm[None, :], update)

    small = [jnp.stack([smalls[l][n][0] for l in range(L)]) for n in _SMALL]
    small += [jnp.stack([smalls[l]["conv_w"] for l in range(L)]), d_final[0], loss_dev[None]]
    packed, offs = _pack_rows(small, FLAT_COLS)
    small = _unpack_rows(_all_reduce_small(packed), small, offs)
    grad = dict(zip(_SMALL, small))
    grad["conv_w"] = lax.dynamic_slice(small[len(_SMALL)], (0, 0, dev * cw), (L, 3, cw))
    grad["final_norm"] = small[-2]
    loss = small[-1][0]

    deltas, new_m, new_v = {}, {}, {}
    for n, outs in done.items():
        grad[n], deltas[n], new_m[n], new_v[n] = (view(n, a) for a in outs)
    for n in _SMALL + ("conv_w", "final_norm"):
        w3 = wts[n].reshape((1,) * (3 - wts[n].ndim) + wts[n].shape)
        d, nm, nv = _adamw(w3, grad[n].reshape(w3.shape), args["m_" + n].reshape(w3.shape),
                           args["v_" + n].reshape(w3.shape))
        deltas[n], new_m[n], new_v[n] = (a.reshape(wts[n].shape) for a in (d, nm, nv))
    return (loss, grad_x[None], *[grad[n] for n in _ORDER], *[deltas[n] for n in _ORDER],
            *[new_m[n] for n in _ORDER], *[new_v[n] for n in _ORDER])
```

```python
import functools
import math

import jax
import jax.numpy as jnp
from jax import lax
from jax.experimental import pallas as pl
from jax.experimental.pallas import tpu as pltpu
from jax.experimental.pallas import tpu_sc as plsc

F32 = jnp.float32
BF16 = jnp.bfloat16

CHUNK = 64
NOPE = 128
ROPE = 64
VDIM = 128
ROPE_THETA = 10000.0
EPS = 1e-6
ATTN_SCALE = (NOPE + ROPE) ** -0.5
SCORE_SCALE = ATTN_SCALE * math.log2(math.e)
LN2 = math.log(2.0)
ADAM_LR = 0.001
ADAM_B1 = 0.9
ADAM_B2 = 0.999
ADAM_EPS = 1e-08
ADAM_WD = 0.01
ADAM_STEP = 10

LANES = 128
SUBLANES = 8
BF16_ROWS = 16
V7X_VMEM_BYTES = 64 * 1024 * 1024
VMEM_LIMIT = V7X_VMEM_BYTES * 7 // 8
HEAD_SLOT = 2 * LANES
N_DEV = 8
ATTN_FWD_WIDTH = 4
ATTN_BWD_WIDTH = 4
FLAT_COLS = 1024
CLASSES = ("gu", "dn", "sq", "win", "c128", "c256")

NT = (((1,), (1,)), ((), ()))
MESH = pl.DeviceIdType.MESH
ANY = pl.BlockSpec(memory_space=pl.ANY)


def _dot(a, b):
    return jnp.dot(a, b, preferred_element_type=F32)


def _dot_nt(a, b):
    return lax.dot_general(a, b, NT, preferred_element_type=F32)


def _sig(x):
    return 1.0 / (1.0 + jnp.exp(-x))


def _tile(n, pref, unit):
    if n <= pref:
        return n
    t = (pref // unit) * unit
    while t >= unit:
        if n % t == 0:
            return t
        t -= unit
    return n


def _call(body, name, grid, in_specs, out_specs, out_shape, scratch=(), aliases=None):
    return pl.pallas_call(
        body,
        name=name,
        grid=grid,
        in_specs=in_specs,
        out_specs=out_specs,
        out_shape=out_shape,
        scratch_shapes=list(scratch),
        input_output_aliases=aliases or {},
        compiler_params=pltpu.CompilerParams(
            dimension_semantics=("arbitrary",) * len(grid), vmem_limit_bytes=VMEM_LIMIT
        ),
    )


def _sds(shape, dtype):
    return jax.ShapeDtypeStruct(shape, dtype)


def _rms_fwd(x, gain):
    rstd = lax.rsqrt(jnp.mean(x * x, axis=-1, keepdims=True) + EPS)
    return x * rstd * gain, rstd


def _rms_bwd(dn, x, rstd, gain):
    xhat = x * rstd
    dgy = dn * gain
    dx = rstd * (dgy - xhat * jnp.mean(dgy * xhat, axis=-1, keepdims=True))
    return dx, jnp.sum(dn * xhat, axis=0, keepdims=True)


def _rows(tm, w):
    return pl.BlockSpec((tm, w), lambda i: (i, 0))


def _whole(a):
    nd = a.ndim
    return pl.BlockSpec(a.shape, lambda i: (0,) * nd, pipeline_mode=pl.Buffered(1))


def _slab(buf, rows, index):
    return pl.BlockSpec((N_DEV, rows, buf.shape[2]), lambda i: (0, index, 0), pipeline_mode=pl.Buffered(1))


def _cat_slots(w):
    return jnp.concatenate([w[d] for d in range(N_DEV)], axis=1)


def _down_weight(w_ref, d, c):
    return w_ref[2 * d:2 * d + 2].reshape(c, w_ref.shape[2])


class _FfnWeights:
    def __init__(self, gu_hbm, dn_hbm, gu_ref, dn_ref, sems):
        nb = N_DEV // 2
        self.first = pl.program_id(0) == 0
        self.gu, self.dn = gu_ref, dn_ref
        self.copies = []
        for d in range(nb):
            self.copies.append([
                pltpu.make_async_copy(gu_hbm.at[d], gu_ref.at[d], sems.at[3 * d]),
                pltpu.make_async_copy(gu_hbm.at[nb + d], gu_ref.at[nb + d], sems.at[3 * d + 1]),
                pltpu.make_async_copy(dn_hbm.at[pl.ds(2 * d, 2)], dn_ref.at[pl.ds(2 * d, 2)], sems.at[3 * d + 2])])

        @pl.when(self.first)
        def _():
            for group in self.copies:
                for cp in group:
                    cp.start()

    def use(self, d):
        @pl.when(self.first)
        def _():
            for cp in self.copies[d]:
                cp.wait()

    @staticmethod
    def scratch(gu_w, dn_w):
        return [pltpu.VMEM(gu_w.shape, gu_w.dtype), pltpu.VMEM(dn_w.shape, dn_w.dtype),
                pltpu.SemaphoreType.DMA((3 * N_DEV // 2,))]


def _ffn_fwd(h, gain, gu_w, dn_w):
    S, D = h.shape
    c = gu_w.shape[2]
    tm = _tile(S, 512, SUBLANES)
    nb = N_DEV // 2

    def body(h_ref, gain_ref, w_hbm, wd_hbm, o_ref, jac_ref, at_ref, n_ref, r_ref, w_ref, wd_ref, sems):
        weights = _FfnWeights(w_hbm, wd_hbm, w_ref, wd_ref, sems)
        x = h_ref[...]
        n32, rstd = _rms_fwd(x, gain_ref[...])
        n = n32.astype(BF16)
        n_ref[...] = n.T
        r_ref[...] = rstd
        acc = jnp.zeros((tm, D), F32)
        for d in range(nb):
            weights.use(d)
            g = _dot(n, w_ref[d])
            u = _dot(n, w_ref[nb + d])
            sg = _sig(g)
            silu = g * sg
            a = (silu * u).astype(BF16)
            at_ref[d] = a.T
            jac_ref[d] = (0.5 * u * (sg + silu * (1.0 - sg))).astype(BF16)
            jac_ref[nb + d] = (0.5 * silu).astype(BF16)
            acc = acc + _dot(a, _down_weight(wd_ref, d, c))
        o_ref[...] = x + 0.5 * acc

    return _call(
        body, "ffn_fwd", (S // tm,),
        [_rows(tm, D), _whole(gain), ANY, ANY],
        [_rows(tm, D), pl.BlockSpec((N_DEV, tm, c), lambda i: (0, i, 0)),
         pl.BlockSpec((nb, c, tm), lambda i: (0, 0, i)), pl.BlockSpec((D, tm), lambda i: (0, i)), _rows(tm, 1)],
        [_sds((S, D), F32), _sds((N_DEV, S, c), BF16), _sds((nb, c, S), BF16), _sds((D, S), BF16),
         _sds((S, 1), F32)],
        _FfnWeights.scratch(gu_w, dn_w),
    )(h, gain, gu_w, dn_w)


def _win_segments(C, QL, KVL, D):
    o1, o2 = 3 * C, 3 * C + QL + KVL + ROPE
    return [("bcv", k, k * C, (k + 1) * C) for k in range(3)] + [("qkr", None, o1, o2), ("gg", None, o2, o2 + 2 * D)]


def _win_pieces(segments, cw):
    out = []
    for tgt, lead, a, b in segments:
        for d in range(N_DEV):
            lo, hi = max(a, d * cw), min(b, (d + 1) * cw)
            if lo < hi:
                out.append((tgt, lead, d, (lo - d * cw, hi - d * cw), (lo - a, hi - a)))
    return out


def _win_split(win_w, C, QL, KVL):
    _, D, cw = win_w.shape
    WQ = QL + KVL + LANES
    pieces = _win_pieces(_win_segments(C, QL, KVL, D), cw)
    tr = _tile(D, 256, BF16_ROWS)

    def body(w_ref, bcv_ref, qkr_ref, gg_ref):
        tgt = dict(bcv=bcv_ref, qkr=qkr_ref, gg=gg_ref)
        qkr_ref[:, QL + KVL + ROPE:] = jnp.zeros((tr, LANES - ROPE), BF16)
        for name, lead, d, (s0, s1), (t0, t1) in pieces:
            v = w_ref[d, :, s0:s1]
            if lead is None:
                tgt[name][:, t0:t1] = v
            else:
                tgt[name][lead, :, t0:t1] = v

    return _call(
        body, "win_split", (D // tr,),
        [pl.BlockSpec((N_DEV, tr, cw), lambda i: (0, i, 0))],
        [pl.BlockSpec((3, tr, C), lambda i: (0, i, 0)), _rows(tr, WQ), _rows(tr, 2 * D)],
        [_sds((3, D, C), BF16), _sds((D, WQ), BF16), _sds((D, 2 * D), BF16)],
    )(win_w)


def _win_merge(d_bcv, d_qkr, d_gg, cw):
    _, D, C = d_bcv.shape
    WQ = d_qkr.shape[1]
    QL_KVL = WQ - LANES
    o1 = 3 * C
    segments = [("bcv", k, k * C, (k + 1) * C) for k in range(3)]
    segments += [("qkr", None, o1, o1 + QL_KVL + ROPE), ("gg", None, o1 + QL_KVL + ROPE, o1 + QL_KVL + ROPE + 2 * D)]
    pieces = _win_pieces(segments, cw)
    tr = _tile(D, 256, BF16_ROWS)

    def body(bcv_ref, qkr_ref, gg_ref, o_ref):
        src = dict(bcv=bcv_ref, qkr=qkr_ref, gg=gg_ref)
        for name, lead, d, (s0, s1), (t0, t1) in pieces:
            v = src[name][:, t0:t1] if lead is None else src[name][lead, :, t0:t1]
            o_ref[d, :, s0:s1] = v.astype(BF16)

    return _call(
        body, "win_merge", (D // tr,),
        [pl.BlockSpec((3, tr, C), lambda i: (0, i, 0)), _rows(tr, WQ), _rows(tr, 2 * D)],
        pl.BlockSpec((N_DEV, tr, cw), lambda i: (0, i, 0)),
        _sds((N_DEV, D, cw), BF16),
    )(d_bcv, d_qkr, d_gg)


def _mix_in(h, gain, w_bcv, w_qkr, w_gg):
    S, D = h.shape
    C = w_bcv.shape[2]
    tm = _tile(S, 512, SUBLANES)

    def body(h_ref, gain_ref, w1, w2, w3, o1, o2, o3, n_ref, r_ref):
        n32, rstd = _rms_fwd(h_ref[...], gain_ref[...])
        n = n32.astype(BF16)
        n_ref[...] = n.T
        r_ref[...] = rstd
        for k in range(3):
            o1[k] = _dot(n, w1[k]).astype(BF16)
        o2[...] = _dot(n, w2[...])
        o3[...] = _dot(n, w3[...]).astype(BF16)

    return _call(
        body, "mix_in", (S // tm,),
        [_rows(tm, D), _whole(gain), _whole(w_bcv), _whole(w_qkr), _whole(w_gg)],
        [pl.BlockSpec((3, tm, C), lambda i: (0, i, 0)), _rows(tm, w_qkr.shape[1]), _rows(tm, 2 * D),
         pl.BlockSpec((D, tm), lambda i: (0, i)), _rows(tm, 1)],
        [_sds((3, S, C), BF16), _sds((S, w_qkr.shape[1]), F32), _sds((S, 2 * D), BF16), _sds((D, S), BF16),
         _sds((S, 1), F32)],
    )(h, gain, w_bcv, w_qkr, w_gg)


def _conv_taps(zc):
    rows = lax.broadcasted_iota(jnp.int32, zc.shape, 0)
    z1 = jnp.where(rows >= 1, pltpu.roll(zc, 1, 0), 0.0)
    z2 = jnp.where(rows >= 2, pltpu.roll(zc, 2, 0), 0.0)
    return z1, z2


def _conv_fwd(z_bcv, conv_w):
    _, S, C = z_bcv.shape

    def body(z_ref, w_ref, o_ref):
        w = w_ref[...]
        zc = z_ref[1].astype(F32) * z_ref[2].astype(F32)
        z1, z2 = _conv_taps(zc)
        y = w[0:1] * z2 + w[1:2] * z1 + w[2:3] * zc
        o_ref[...] = (z_ref[0].astype(F32) * y).astype(BF16)

    return _call(
        body, "conv_fwd", (C // LANES,),
        [pl.BlockSpec((3, S, LANES), lambda j: (0, 0, j)), pl.BlockSpec((3, LANES), lambda j: (0, j))],
        pl.BlockSpec((S, LANES), lambda j: (0, j)),
        _sds((S, C), BF16),
    )(z_bcv, conv_w)


def _rope(x, cs, half):
    c, s1, s2 = cs[:, :LANES], cs[:, LANES:2 * LANES], cs[:, 2 * LANES:]
    return x * c + pltpu.roll(x, LANES - half, 1) * s1 + pltpu.roll(x, half, 1) * s2


def _unrope(d, cs, half):
    c, s1, s2 = cs[:, :LANES], cs[:, LANES:2 * LANES], cs[:, 2 * LANES:]
    return d * c + pltpu.roll(d * s1, half, 1) + pltpu.roll(d * s2, LANES - half, 1)


def _mla_prep(z_qkr, gq, gkv, cs, c256_w):
    S = z_qkr.shape[0]
    QL, KVL = gq.shape[1], gkv.shape[1]
    H = N_DEV
    tm = _tile(S, 512, SUBLANES)
    half = ROPE // 2

    def body(z_ref, gq_ref, gkv_ref, cs_ref, w_ref, q_ref, k_ref, v_ref, qn_ref, kvn_ref, rq_ref, rkv_ref):
        z = z_ref[...]
        cs_t = cs_ref[...]
        qn32, rq = _rms_fwd(z[:, :QL], gq_ref[...])
        kvn32, rkv = _rms_fwd(z[:, QL:QL + KVL], gkv_ref[...])
        qn = qn32.astype(BF16)
        kvn = kvn32.astype(BF16)
        qn_ref[...] = qn
        kvn_ref[...] = kvn
        rq_ref[...] = rq
        rkv_ref[...] = rkv
        krope = _rope(z[:, QL + KVL:], cs_t, half).astype(BF16)
        for h in range(H):
            lo, mid, hi = h * HEAD_SLOT, h * HEAD_SLOT + LANES, (h + 1) * HEAD_SLOT
            q = _dot(qn, w_ref[h, KVL:KVL + QL, :])
            kv = _dot(kvn, w_ref[h, 0:KVL, :])
            q_ref[:, lo:mid] = (q[:, :LANES] * SCORE_SCALE).astype(BF16)
            q_ref[:, mid:hi] = (_rope(q[:, LANES:], cs_t, half) * SCORE_SCALE).astype(BF16)
            k_ref[:, lo:mid] = kv[:, :LANES].astype(BF16)
            k_ref[:, mid:hi] = krope
            v_ref[:, h * VDIM:(h + 1) * VDIM] = kv[:, LANES:].astype(BF16)

    return _call(
        body, "mla_prep", (S // tm,),
        [_rows(tm, z_qkr.shape[1]), _whole(gq), _whole(gkv), _rows(tm, 3 * LANES), _whole(c256_w)],
        [_rows(tm, H * HEAD_SLOT), _rows(tm, H * HEAD_SLOT), _rows(tm, H * VDIM), _rows(tm, QL), _rows(tm, KVL),
         _rows(tm, 1), _rows(tm, 1)],
        [_sds((S, H * HEAD_SLOT), BF16), _sds((S, H * HEAD_SLOT), BF16), _sds((S, H * VDIM), BF16),
         _sds((S, QL), BF16), _sds((S, KVL), BF16), _sds((S, 1), F32), _sds((S, 1), F32)],
    )(z_qkr, gq, gkv, cs, c256_w)


def _chunk_mask(rows, cols, diagonal_row):
    shift = CHUNK.bit_length() - 1
    krow = (lax.broadcasted_iota(jnp.int32, (rows, cols), 0) - diagonal_row) >> shift
    qcol = lax.broadcasted_iota(jnp.int32, (rows, cols), 1) >> shift
    return krow <= qcol


def _attn_fwd(q, k, v, H):
    S = q.shape[0]
    t = _tile(S, 512, CHUNK)
    nq = S // t

    def body(q_ref, k_ref, v_ref, o_ref, ot_ref, lse_ref, vt_ref):
        qi = pl.program_id(1)

        @pl.when(qi == 0)
        def _():
            vt_ref[0:VDIM, :] = v_ref[...].T
            vt_ref[VDIM:, :] = jnp.ones((BF16_ROWS, S), BF16)

        qv = q_ref[...]

        def block(start, width, carry, masked):
            m, acc = carry
            off = pl.multiple_of(start * t, t)
            s = _dot_nt(k_ref[pl.ds(off, width * t), :], qv)
            if masked:
                s = jnp.where(_chunk_mask(width * t, t, (width - 1) * t), s, -1e30)
            m_new = jnp.maximum(m, jnp.max(s, axis=0, keepdims=True))
            p = jnp.exp2(s - m_new).astype(BF16)
            acc = jnp.exp2(m - m_new) * acc + _dot(vt_ref[:, pl.ds(off, width * t)], p)
            return m_new, acc

        init = (jnp.full((1, t), -1e30, F32), jnp.zeros((VDIM + BF16_ROWS, t), F32))
        wide = lax.div(qi, ATTN_FWD_WIDTH)
        carry = lax.fori_loop(0, wide, lambda j, c: block(j * ATTN_FWD_WIDTH, ATTN_FWD_WIDTH, c, False), init)
        left = qi - wide * ATTN_FWD_WIDTH
        for extra in range(ATTN_FWD_WIDTH):
            @pl.when(left == extra)
            def _():
                m, acc = block(qi - extra, extra + 1, carry, True)
                l = acc[VDIM:VDIM + 1]
                out = (acc[0:VDIM] * (1.0 / l)).astype(BF16)
                ot_ref[...] = out
                o_ref[...] = out.T
                lse_ref[0] = jnp.broadcast_to(m + jnp.log2(l), (SUBLANES, t))

    return _call(
        body, "attn_fwd", (H, nq),
        [pl.BlockSpec((t, HEAD_SLOT), lambda h, i: (i, h)), pl.BlockSpec((S, HEAD_SLOT), lambda h, i: (0, h)),
         pl.BlockSpec((S, VDIM), lambda h, i: (0, h))],
        [pl.BlockSpec((t, VDIM), lambda h, i: (i, h)), pl.BlockSpec((VDIM, t), lambda h, i: (h, i)),
         pl.BlockSpec((1, SUBLANES, t), lambda h, i: (h, 0, i))],
        [_sds((S, H * VDIM), BF16), _sds((H * VDIM, S), BF16), _sds((H, SUBLANES, S), F32)],
        [pltpu.VMEM((VDIM + BF16_ROWS, S), BF16)],
    )(q, k, v)


def _merge_wo(o, by, z_gg, h, sq_w, c128_w):
    S, D = h.shape
    C = by.shape[1]
    r = sq_w.shape[1] // 3
    tm = _tile(S, 512, SUBLANES)

    def body(o_ref, by_ref, gg_ref, h_ref, wmo_ref, wo_ref, wco_ref, h2_ref, mg_ref, yc_ref, ym_ref):
        ymla = _dot(o_ref[...], wmo_ref[...].reshape(N_DEV * r, D))
        yconv = _dot(by_ref[...], _cat_slots(wco_ref))
        gg = gg_ref[...].astype(F32)
        merged = (_sig(gg[:, :D]) * yconv + _sig(gg[:, D:]) * ymla).astype(BF16)
        mg_ref[...] = merged.T
        yc_ref[...] = yconv.astype(BF16)
        ym_ref[...] = ymla.astype(BF16)
        h2_ref[...] = h_ref[...] + _dot(merged, wo_ref[...].reshape(N_DEV * r, D))

    return _call(
        body, "merge_wo", (S // tm,),
        [_rows(tm, o.shape[1]), _rows(tm, C), _rows(tm, 2 * D), _rows(tm, D), _slab(sq_w, r, 0), _slab(sq_w, r, 1),
         _slab(c128_w, C, 0)],
        [_rows(tm, D), pl.BlockSpec((D, tm), lambda i: (0, i)), _rows(tm, D), _rows(tm, D)],
        [_sds((S, D), F32), _sds((D, S), BF16), _sds((S, D), BF16), _sds((S, D), BF16)],
    )(o, by, z_gg, h, sq_w, sq_w, c128_w)


def _ple_fwd(h, gain, p, sq_w, c128_w, C):
    S, D = h.shape
    P = p.shape[1]
    r = sq_w.shape[1] // 3
    tm = _tile(S, 512, SUBLANES)

    def body(h_ref, gain_ref, p_ref, wpg_ref, wpp_ref, o_ref, pre_ref, pp_ref, n_ref, r_ref):
        x = h_ref[...]
        n32, rstd = _rms_fwd(x, gain_ref[...])
        n = n32.astype(BF16)
        n_ref[...] = n.T
        r_ref[...] = rstd
        pre = _dot(n, wpg_ref[...].reshape(N_DEV * r, D))
        pp = _dot(p_ref[...].astype(BF16), _cat_slots(wpp_ref))
        pre_ref[...] = pre.astype(BF16)
        pp_ref[...] = pp.astype(BF16)
        o_ref[...] = x + _sig(pre) * pp

    return _call(
        body, "ple_fwd", (S // tm,),
        [_rows(tm, D), _whole(gain), _rows(tm, P), _slab(sq_w, r, 2), _slab(c128_w, P, C // P)],
        [_rows(tm, D), _rows(tm, D), _rows(tm, D), pl.BlockSpec((D, tm), lambda i: (0, i)), _rows(tm, 1)],
        [_sds((S, D), F32), _sds((S, D), BF16), _sds((S, D), BF16), _sds((D, S), BF16), _sds((S, 1), F32)],
    )(h, gain, p, sq_w, c128_w)


def _final_loss(h, gain, target):
    S, D = h.shape
    tm = _tile(S, 512, SUBLANES)

    def body(h_ref, gain_ref, t_ref, dh_ref, loss_ref, dg_ref):
        @pl.when(pl.program_id(0) == 0)
        def _():
            loss_ref[...] = jnp.zeros_like(loss_ref)
            dg_ref[...] = jnp.zeros_like(dg_ref)

        x = h_ref[...]
        gain_v = gain_ref[...]
        y, rstd = _rms_fwd(x, gain_v)
        err = y - t_ref[...]
        loss_ref[...] += 0.5 * jnp.sum(jnp.mean(err * err, axis=-1, keepdims=True))
        dx, dgain = _rms_bwd(err * (1.0 / D), x, rstd, gain_v)
        dh_ref[...] = dx
        dg_ref[...] += dgain

    return _call(
        body, "final_loss", (S // tm,),
        [_rows(tm, D), _whole(gain), _rows(tm, D)],
        [_rows(tm, D), pl.BlockSpec((1, LANES), lambda i: (0, 0)), pl.BlockSpec((1, D), lambda i: (0, 0))],
        [_sds((S, D), F32), _sds((1, LANES), F32), _sds((1, D), F32)],
    )(h, gain, target)


def _tn_call(body, name, grid, in_specs, out_spec, out_shape, scratch, operands, prev):
    n = len(operands)
    if prev is None:
        return _call(body, name, grid, in_specs, out_spec, out_shape, scratch)(*operands)
    assert prev.shape == out_shape.shape and prev.dtype == out_shape.dtype

    def wrapped(*refs):
        body(*refs[:n], *refs[n + 1:])

    return _call(wrapped, name, grid, in_specs + [ANY], out_spec, out_shape, scratch, {n: 0})(*operands, prev)


def _tn_slots(xt, dy, prev, rows_total, row_off):
    K, S = xt.shape
    B, _, c = dy.shape
    tk = _tile(K, 1024, BF16_ROWS)

    def body(xt_ref, dy_ref, o_ref):
        o_ref[0] = _dot(xt_ref[...], dy_ref[0]).astype(BF16)

    return _tn_call(
        body, "tn_slots", (K // tk, B),
        [pl.BlockSpec((tk, S), lambda i, b: (i, 0)), pl.BlockSpec((1, S, c), lambda i, b: (b, 0, 0))],
        pl.BlockSpec((1, tk, c), lambda i, b: (b, row_off // tk + i, 0)),
        _sds((B, rows_total, c), BF16), [], [xt, dy], prev)


def _tn_plain(xt, dy):
    K, S = xt.shape
    B, _, c = dy.shape
    tk = _tile(K, 512, BF16_ROWS)
    tn = _tile(c, 1024, LANES)

    def body(xt_ref, dy_ref, o_ref):
        o_ref[0] = _dot(xt_ref[...], dy_ref[0])

    return _call(
        body, "tn_plain", (K // tk, B, c // tn),
        [pl.BlockSpec((tk, S), lambda i, b, j: (i, 0)), pl.BlockSpec((1, S, tn), lambda i, b, j: (b, 0, j))],
        pl.BlockSpec((1, tk, tn), lambda i, b, j: (b, i, j)),
        _sds((B, K, c), F32),
    )(xt, dy)


def _tn_down(at, dh, prev, rows_total, which):
    nb, c, S = at.shape
    D = dh.shape[1]
    r = c // 2
    tn = _tile(D, 512, LANES)

    def body(at_ref, dh_ref, o_ref):
        g = 0.5 * _dot(at_ref[0], dh_ref[...].astype(BF16))
        o_ref[...] = g.astype(BF16).reshape(2, r, tn)

    return _tn_call(
        body, "tn_down", (D // tn, nb),
        [pl.BlockSpec((1, c, S), lambda j, i: (i, 0, 0)), pl.BlockSpec((S, tn), lambda j, i: (0, j))],
        pl.BlockSpec((2, r, tn), lambda j, i: (i, which, j)),
        _sds((N_DEV, rows_total, D), BF16), [], [at, dh], prev)


def _tn_square(xt, dy, prev, rows_total, member):
    K, S = xt.shape
    N = dy.shape[1]
    r = K // N_DEV
    tk = _tile(K, 512, r)
    tn = _tile(N, 512, LANES)

    def body(xt_ref, dy_ref, o_ref):
        g = _dot(xt_ref[...], dy_ref[...].astype(BF16))
        o_ref[...] = g.astype(BF16).reshape(tk // r, r, tn)

    return _tn_call(
        body, "tn_square", (N // tn, K // tk),
        [pl.BlockSpec((tk, S), lambda j, i: (i, 0)), pl.BlockSpec((S, tn), lambda j, i: (0, j))],
        pl.BlockSpec((tk // r, r, tn), lambda j, i: (i, member, j)),
        _sds((N_DEV, rows_total, N), BF16), [], [xt, dy], prev)


def _tn_cols(x, dy, prev, rows_total, row_block):
    S, K = x.shape
    N = dy.shape[1]
    cw = N // N_DEV

    def body(x_ref, dy_ref, o_ref):
        g = _dot(x_ref[...].astype(BF16).T, dy_ref[...])
        for d in range(N_DEV):
            o_ref[d] = g[:, d * cw:(d + 1) * cw].astype(BF16)

    return _tn_call(
        body, "tn_cols", (1,),
        [pl.BlockSpec((S, K), lambda i: (0, 0)), pl.BlockSpec((S, N), lambda i: (0, 0))],
        pl.BlockSpec((N_DEV, K, cw), lambda i: (0, row_block, 0)),
        _sds((N_DEV, rows_total, cw), BF16), [], [x, dy], prev)


def _tn_heads(qn, kvn, dqp, dkv):
    S, QL = qn.shape
    KVL = kvn.shape[1]

    def body(qn_ref, kvn_ref, dq_ref, dkv_ref, o_ref):
        o_ref[0, 0:KVL, :] = _dot(kvn_ref[...].T, dkv_ref[...]).astype(BF16)
        o_ref[0, KVL:KVL + QL, :] = _dot(qn_ref[...].T, dq_ref[...]).astype(BF16)

    head = pl.BlockSpec((S, HEAD_SLOT), lambda h: (0, h))
    return _call(
        body, "tn_heads", (N_DEV,),
        [pl.BlockSpec((S, QL), lambda h: (0, 0)), pl.BlockSpec((S, KVL), lambda h: (0, 0)), head, head],
        pl.BlockSpec((1, KVL + QL, HEAD_SLOT), lambda h: (h, 0, 0)),
        _sds((N_DEV, KVL + QL, HEAD_SLOT), BF16),
    )(qn, kvn, dqp, dkv)


def _ple_bwd(dh, pre, pp, h, rstd, gain, sq_w, after):
    S, D = h.shape
    r = sq_w.shape[1] // 3
    tm = _tile(S, 512, SUBLANES)

    def body(dh_ref, pre_ref, pp_ref, h_ref, r_ref, gain_ref, wpg_ref, *rest):
        o_ref, dpre_ref, dpp_ref, dg_ref = rest[len(after):]

        @pl.when(pl.program_id(0) == 0)
        def _():
            dg_ref[...] = jnp.zeros_like(dg_ref)

        d = dh_ref[...]
        gate = _sig(pre_ref[...].astype(F32))
        dpre = (d * pp_ref[...].astype(F32) * gate * (1.0 - gate)).astype(BF16)
        dpre_ref[...] = dpre
        dpp_ref[...] = (d * gate).astype(BF16)
        dn = _dot_nt(dpre, wpg_ref[...].reshape(N_DEV * r, D))
        dx, dgain = _rms_bwd(dn, h_ref[...], r_ref[...], gain_ref[...])
        o_ref[...] = d + dx
        dg_ref[...] += dgain

    return _call(
        body, "ple_bwd", (S // tm,),
        [_rows(tm, D), _rows(tm, D), _rows(tm, D), _rows(tm, D), _rows(tm, 1), _whole(gain), _slab(sq_w, r, 2)]
        + [ANY] * len(after),
        [_rows(tm, D), _rows(tm, D), _rows(tm, D), pl.BlockSpec((1, D), lambda i: (0, 0))],
        [_sds((S, D), F32), _sds((S, D), BF16), _sds((S, D), BF16), _sds((1, D), F32)],
    )(dh, pre, pp, h, rstd, gain, sq_w, *after)


def _ffn_bwd(dh, jac, gu_w, dn_w, h, rstd, gain, after=()):
    S, D = h.shape
    _, _, c = jac.shape
    nb = N_DEV // 2
    tm = _tile(S, 256, SUBLANES)

    def body(dh_ref, jac_ref, w_hbm, wd_hbm, h_ref, r_ref, gain_ref, *rest):
        dgu_ref, o_ref, dgain_ref, w_ref, wd_ref, sems = rest[len(after):]
        weights = _FfnWeights(w_hbm, wd_hbm, w_ref, wd_ref, sems)

        @pl.when(pl.program_id(0) == 0)
        def _():
            dgain_ref[...] = jnp.zeros_like(dgain_ref)

        dh_v = dh_ref[...]
        dhb = dh_v.astype(BF16)
        dn = jnp.zeros((tm, D), F32)
        for d in range(nb):
            weights.use(d)
            da = _dot_nt(dhb, _down_weight(wd_ref, d, c))
            dg = (da * jac_ref[d].astype(F32)).astype(BF16)
            du = (da * jac_ref[nb + d].astype(F32)).astype(BF16)
            dgu_ref[d] = dg
            dgu_ref[nb + d] = du
            dn = dn + _dot_nt(dg, w_ref[d]) + _dot_nt(du, w_ref[nb + d])
        dx, dgain = _rms_bwd(dn, h_ref[...], r_ref[...], gain_ref[...])
        o_ref[...] = dh_v + dx
        dgain_ref[...] += dgain

    act = pl.BlockSpec((N_DEV, tm, c), lambda i: (0, i, 0))
    return _call(
        body, "ffn_bwd", (S // tm,),
        [_rows(tm, D), act, ANY, ANY, _rows(tm, D), _rows(tm, 1), _whole(gain)] + [ANY] * len(after),
        [act, _rows(tm, D), pl.BlockSpec((1, D), lambda i: (0, 0))],
        [_sds((N_DEV, S, c), BF16), _sds((S, D), F32), _sds((1, D), F32)],
        _FfnWeights.scratch(gu_w, dn_w),
    )(dh, jac, gu_w, dn_w, h, rstd, gain, *after)


def _merge_bwd(dh, z_gg, yconv, ymla, o, sq_w, c128_w, C, after):
    S, D = dh.shape
    r = sq_w.shape[1] // 3
    HV = N_DEV * r
    H = HV // VDIM
    tm = _tile(S, 512, SUBLANES)

    def head_rows():
        row = lax.broadcasted_iota(jnp.int32, (SUBLANES * H, HV), 0) >> (SUBLANES.bit_length() - 1)
        col = lax.broadcasted_iota(jnp.int32, (SUBLANES * H, HV), 1) >> (VDIM.bit_length() - 1)
        return jnp.where(row == col, 1.0, 0.0).astype(BF16)

    def body(dh_ref, gg_ref, yc_ref, ym_ref, o_ref, wmo_ref, wo_ref, wco_ref, *rest):
        dgg_ref, dby_ref, do_ref, dyc_ref, dym_ref, dl_ref = rest[len(after):]
        dm = _dot_nt(dh_ref[...].astype(BF16), wo_ref[...].reshape(HV, D))
        gg = gg_ref[...].astype(F32)
        sgc = _sig(gg[:, :D])
        sgm = _sig(gg[:, D:])
        dyc = (dm * sgc).astype(BF16)
        dym = (dm * sgm).astype(BF16)
        dyc_ref[...] = dyc
        dym_ref[...] = dym
        dgg_ref[:, :D] = (dm * yc_ref[...].astype(F32) * sgc * (1.0 - sgc)).astype(BF16)
        dgg_ref[:, D:] = (dm * ym_ref[...].astype(F32) * sgm * (1.0 - sgm)).astype(BF16)
        dby_ref[...] = _dot_nt(dyc, _cat_slots(wco_ref)).astype(BF16)
        do = _dot_nt(dym, wmo_ref[...].reshape(HV, D)).astype(BF16)
        do_ref[...] = do
        prod = do.astype(F32) * o_ref[...].astype(F32)
        hi = prod.astype(BF16)
        lo = (prod - hi.astype(F32)).astype(BF16)
        pick = head_rows()
        dl_ref[...] = _dot_nt(pick, hi) + _dot_nt(pick, lo)

    return _call(
        body, "merge_bwd", (S // tm,),
        [_rows(tm, D), _rows(tm, 2 * D), _rows(tm, D), _rows(tm, D), _rows(tm, HV), _slab(sq_w, r, 0),
         _slab(sq_w, r, 1), _slab(c128_w, C, 0)] + [ANY] * len(after),
        [_rows(tm, 2 * D), _rows(tm, C), _rows(tm, HV), _rows(tm, D), _rows(tm, D),
         pl.BlockSpec((SUBLANES * H, tm), lambda i: (0, i))],
        [_sds((S, 2 * D), BF16), _sds((S, C), BF16), _sds((S, HV), BF16), _sds((S, D), BF16), _sds((S, D), BF16),
         _sds((SUBLANES * H, S), F32)],
    )(dh, z_gg, yconv, ymla, o, sq_w, sq_w, c128_w, *after)


def _conv_bwd(z_bcv, conv_w, dby):
    _, S, C = z_bcv.shape

    def body(z_ref, w_ref, dby_ref, dz_ref, dw_ref):
        w = w_ref[...]
        c = z_ref[1].astype(F32)
        v = z_ref[2].astype(F32)
        d = dby_ref[...].astype(F32)
        zc = c * v
        z1, z2 = _conv_taps(zc)
        y = w[0:1] * z2 + w[1:2] * z1 + w[2:3] * zc
        dz_ref[0] = (d * y).astype(BF16)
        dy = d * z_ref[0].astype(F32)
        rows = lax.broadcasted_iota(jnp.int32, dy.shape, 0)
        dy1 = jnp.where(rows < S - 1, pltpu.roll(dy, S - 1, 0), 0.0)
        dy2 = jnp.where(rows < S - 2, pltpu.roll(dy, S - 2, 0), 0.0)
        dzc = w[2:3] * dy + w[1:2] * dy1 + w[0:1] * dy2
        dz_ref[1] = (dzc * v).astype(BF16)
        dz_ref[2] = (dzc * c).astype(BF16)
        dw_ref[0:1, :] = jnp.sum(dy * z2, axis=0, keepdims=True)
        dw_ref[1:2, :] = jnp.sum(dy * z1, axis=0, keepdims=True)
        dw_ref[2:3, :] = jnp.sum(dy * zc, axis=0, keepdims=True)

    three = pl.BlockSpec((3, S, LANES), lambda j: (0, 0, j))
    wspec = pl.BlockSpec((3, LANES), lambda j: (0, j))
    return _call(
        body, "conv_bwd", (C // LANES,),
        [three, wspec, pl.BlockSpec((S, LANES), lambda j: (0, j))],
        [three, wspec],
        [_sds((3, S, C), BF16), _sds((3, C), F32)],
    )(z_bcv, conv_w, dby)


def _attn_bwd(q, k, v, do, lse, delta, H):
    S = q.shape[0]
    t = _tile(S, 512, CHUNK)
    nk = S // t

    def body(q_ref, k_ref, v_ref, do_ref, lse_ref, dl_ref, dq_ref, dk_ref, dv_ref, dqt_ref):
        kj = pl.program_id(1)

        @pl.when(kj == 0)
        def _():
            dqt_ref[...] = jnp.zeros_like(dqt_ref)

        kv = k_ref[...]
        vv = v_ref[...]
        kt = kv.T

        def block(start, width, carry, masked):
            dk, dv = carry
            off = pl.multiple_of(start * t, t)
            qv = q_ref[pl.ds(off, width * t), :]
            dov = do_ref[pl.ds(off, width * t), :]
            s = _dot_nt(kv, qv)
            if masked:
                s = jnp.where(_chunk_mask(t, width * t, 0), s, -1e30)
            p = jnp.exp2(s - lse_ref[0, 0:1, pl.ds(off, width * t)])
            dp = _dot_nt(vv, dov)
            ds = (p * (dp - dl_ref[0, 0:1, pl.ds(off, width * t)]) * LN2).astype(BF16)
            dqt_ref[:, pl.ds(off, width * t)] += _dot(kt, ds)
            return dk + _dot(ds, qv), dv + _dot(p.astype(BF16), dov)

        init = (jnp.zeros((t, HEAD_SLOT), F32), jnp.zeros((t, VDIM), F32))
        wide = lax.div(nk - 1 - kj, ATTN_BWD_WIDTH)
        left = nk - 1 - kj - wide * ATTN_BWD_WIDTH
        carry = lax.switch(left, [functools.partial(block, kj, extra + 1, init, True)
                                  for extra in range(ATTN_BWD_WIDTH)])
        dk, dv = lax.fori_loop(
            0, wide, lambda j, c: block(kj + 1 + left + j * ATTN_BWD_WIDTH, ATTN_BWD_WIDTH, c, False), carry)
        dk_ref[...] = dk.astype(BF16)
        dv_ref[...] = dv.astype(BF16)

        @pl.when(kj == nk - 1)
        def _():
            dq_ref[...] = (dqt_ref[...] * SCORE_SCALE).T.astype(BF16)

    kspec = lambda w: pl.BlockSpec((t, w), lambda h, j: (j, h))
    qspec = lambda w: pl.BlockSpec((S, w), lambda h, j: (0, h))
    stat = pl.BlockSpec((1, SUBLANES, S), lambda h, j: (h, 0, 0))
    return _call(
        body, "attn_bwd", (H, nk),
        [qspec(HEAD_SLOT), kspec(HEAD_SLOT), kspec(VDIM), qspec(VDIM), stat, stat],
        [qspec(HEAD_SLOT), kspec(HEAD_SLOT), kspec(VDIM)],
        [_sds((S, H * HEAD_SLOT), BF16), _sds((S, H * HEAD_SLOT), BF16), _sds((S, H * VDIM), BF16)],
        [pltpu.VMEM((HEAD_SLOT, S), F32)],
    )(q, k, v, do, lse, delta)


def _mla_prep_bwd(dq, dk, dv, z_qkr, rq, rkv, gq, gkv, cs, c256_w):
    S = z_qkr.shape[0]
    QL, KVL = gq.shape[1], gkv.shape[1]
    H = N_DEV
    tm = _tile(S, 512, SUBLANES)
    half = ROPE // 2

    def body(dq_ref, dk_ref, dv_ref, z_ref, rq_ref, rkv_ref, gq_ref, gkv_ref, cs_ref, w_ref,
             dz_ref, dqp_ref, dkv_ref, dgq_ref, dgkv_ref):
        @pl.when(pl.program_id(0) == 0)
        def _():
            dgq_ref[...] = jnp.zeros_like(dgq_ref)
            dgkv_ref[...] = jnp.zeros_like(dgkv_ref)

        cs_t = cs_ref[...]
        dkr = jnp.zeros((tm, LANES), F32)
        dqn = jnp.zeros((tm, QL), F32)
        dkvn = jnp.zeros((tm, KVL), F32)
        for h in range(H):
            lo, mid, hi = h * HEAD_SLOT, h * HEAD_SLOT + LANES, (h + 1) * HEAD_SLOT
            dqp_ref[:, lo:mid] = dq_ref[:, lo:mid]
            dqp_ref[:, mid:hi] = _unrope(dq_ref[:, mid:hi].astype(F32), cs_t, half).astype(BF16)
            dkv_ref[:, lo:mid] = dk_ref[:, lo:mid]
            dkv_ref[:, mid:hi] = dv_ref[:, h * VDIM:(h + 1) * VDIM]
            dkr = dkr + dk_ref[:, mid:hi].astype(F32)
            dqn = dqn + _dot_nt(dqp_ref[:, lo:hi], w_ref[h, KVL:KVL + QL, :])
            dkvn = dkvn + _dot_nt(dkv_ref[:, lo:hi], w_ref[h, 0:KVL, :])
        z = z_ref[...]
        dqc, dgq = _rms_bwd(dqn, z[:, :QL], rq_ref[...], gq_ref[...])
        dkvc, dgkv = _rms_bwd(dkvn, z[:, QL:QL + KVL], rkv_ref[...], gkv_ref[...])
        dz_ref[:, :QL] = dqc.astype(BF16)
        dz_ref[:, QL:QL + KVL] = dkvc.astype(BF16)
        dz_ref[:, QL + KVL:] = _unrope(dkr, cs_t, half).astype(BF16)
        dgq_ref[...] += dgq
        dgkv_ref[...] += dgkv

    W = z_qkr.shape[1]
    return _call(
        body, "mla_prep_bwd", (S // tm,),
        [_rows(tm, H * HEAD_SLOT), _rows(tm, H * HEAD_SLOT), _rows(tm, H * VDIM), _rows(tm, W), _rows(tm, 1),
         _rows(tm, 1), _whole(gq), _whole(gkv), _rows(tm, 3 * LANES), _whole(c256_w)],
        [_rows(tm, W), _rows(tm, H * HEAD_SLOT), _rows(tm, H * HEAD_SLOT), _whole(gq), _whole(gkv)],
        [_sds((S, W), BF16), _sds((S, H * HEAD_SLOT), BF16), _sds((S, H * HEAD_SLOT), BF16),
         _sds((1, QL), F32), _sds((1, KVL), F32)],
    )(dq, dk, dv, z_qkr, rq, rkv, gq, gkv, cs, c256_w)


def _mix_in_bwd(d_bcv, dz_qkr, dgg, w_bcv, w_qkr, w_gg, h, rstd, gain, dh):
    S, D = h.shape
    C = d_bcv.shape[2]
    tm = _tile(S, 512, SUBLANES)

    def body(db_ref, dq_ref, dgg_ref, wb_ref, wq_ref, wg_ref, h_ref, r_ref, gain_ref, dh_ref, o_ref, dgain_ref):
        @pl.when(pl.program_id(0) == 0)
        def _():
            dgain_ref[...] = jnp.zeros_like(dgain_ref)

        dn = _dot_nt(dq_ref[...], wq_ref[...]) + _dot_nt(dgg_ref[...], wg_ref[...])
        for k in range(3):
            dn = dn + _dot_nt(db_ref[k], wb_ref[k])
        dx, dgain = _rms_bwd(dn, h_ref[...], r_ref[...], gain_ref[...])
        o_ref[...] = dh_ref[...] + dx
        dgain_ref[...] += dgain

    return _call(
        body, "mix_in_bwd", (S // tm,),
        [pl.BlockSpec((3, tm, C), lambda i: (0, i, 0)), _rows(tm, dz_qkr.shape[1]), _rows(tm, dgg.shape[1]),
         _whole(w_bcv), _whole(w_qkr), _whole(w_gg), _rows(tm, D), _rows(tm, 1), _whole(gain), _rows(tm, D)],
        [_rows(tm, D), pl.BlockSpec((1, D), lambda i: (0, 0))],
        [_sds((S, D), F32), _sds((1, D), F32)],
    )(d_bcv, dz_qkr, dgg, w_bcv, w_qkr, w_gg, h, rstd, gain, dh)


def _rope_tables(positions):
    half = ROPE // 2
    inv_freq = ROPE_THETA ** (-jnp.arange(0, ROPE, 2, dtype=F32) / ROPE)
    ang = positions.astype(F32)[:, None] * inv_freq
    cos, sin = jnp.cos(ang), jnp.sin(ang)
    z = jnp.zeros_like(cos)
    pad = jnp.zeros((positions.shape[0], LANES - 2 * half), F32)
    return jnp.concatenate([cos, cos, pad, -sin, z, pad, z, sin, pad], axis=1)


def _grad_rows(w):
    return dict(gu=2 * w["gu1"].shape[1], dn=2 * w["dn1"].shape[1], sq=w["sq"].shape[1], win=w["win"].shape[1],
                c128=w["c128"].shape[1], c256=w["c256"].shape[1])


def _layer_fwd(h0, p_l, cs, w, sm, late):
    C = sm["conv_w"].shape[1]
    QL, KVL = sm["q_norm"].shape[1], sm["kv_norm"].shape[1]
    h1, jac1, at1, n1, r1 = _ffn_fwd(h0, sm["ffn1_norm"], w["gu1"], w["dn1"])
    if late is not None:
        w.update(late(h1))
    w_bcv, w_qkr, w_gg = _win_split(w["win"], C, QL, KVL)
    z_bcv, z_qkr, z_gg, un, rm = _mix_in(h1, sm["mix_norm"], w_bcv, w_qkr, w_gg)
    by = _conv_fwd(z_bcv, sm["conv_w"])
    q, k, v, qn, kvn, rq, rkv = _mla_prep(z_qkr, sm["q_norm"], sm["kv_norm"], cs, w["c256"])
    o, ot, lse = _attn_fwd(q, k, v, N_DEV)
    h2, merged, yconv, ymla = _merge_wo(o, by, z_gg, h1, w["sq"], w["c128"])
    h3, jac2, at2, n2, r2 = _ffn_fwd(h2, sm["ffn2_norm"], w["gu2"], w["dn2"])
    h4, pre, pp, pn, rp = _ple_fwd(h3, sm["ple_norm"], p_l, w["sq"], w["c128"], C)
    saved = dict(h0=h0, jac1=jac1, at1=at1, n1=n1, r1=r1, h1=h1, w_bcv=w_bcv, w_qkr=w_qkr, w_gg=w_gg, z_bcv=z_bcv,
                 z_qkr=z_qkr, z_gg=z_gg, un=un, rm=rm, by=by, q=q, k=k, v=v, qn=qn, kvn=kvn, rq=rq, rkv=rkv, o=o, ot=ot,
                 lse=lse, h2=h2, merged=merged, yconv=yconv, ymla=ymla, jac2=jac2, at2=at2, n2=n2, r2=r2, h3=h3,
                 pre=pre, pp=pp, pn=pn, rp=rp, p=p_l)
    return h4, saved


def _layer_bwd_late(dh4, s, w, sm, after):
    D = dh4.shape[1]
    C = sm["conv_w"].shape[1]
    P = s["p"].shape[1]
    rows = _grad_rows(w)
    small = {}
    dh3, dpre, dpp, small["ple_norm"] = _ple_bwd(dh4, s["pre"], s["pp"], s["h3"], s["rp"], sm["ple_norm"], w["sq"],
                                                 after)
    g_sq = _tn_square(s["pn"], dpre, None, rows["sq"], 2)
    g_c128 = _tn_cols(s["p"], dpp, None, rows["c128"], C // P)

    dgu2, dh2, small["ffn2_norm"] = _ffn_bwd(dh3, s["jac2"], w["gu2"], w["dn2"], s["h2"], s["r2"], sm["ffn2_norm"])
    g_dn = _tn_down(s["at2"], dh3, None, rows["dn"], 1)
    g_gu = _tn_slots(s["n2"], dgu2, None, rows["gu"], D)
    return dh2, dict(gu=g_gu, dn=g_dn, sq=g_sq, c128=g_c128), small


def _layer_bwd_mixer(dh2, part, small, s, cs, w, sm, after):
    C = sm["conv_w"].shape[1]
    rows = _grad_rows(w)
    g_gu, g_dn, g_sq, g_c128 = part["gu"], part["dn"], part["sq"], part["c128"]

    dgg, dby, do, dyc, dym, delta = _merge_bwd(dh2, s["z_gg"], s["yconv"], s["ymla"], s["o"], w["sq"], w["c128"], C,
                                               after)
    g_sq = _tn_square(s["merged"], dh2, g_sq, rows["sq"], 1)
    g_sq = _tn_square(s["ot"], dym, g_sq, rows["sq"], 0)
    g_c128 = _tn_cols(s["by"], dyc, g_c128, rows["c128"], 0)
    d_bcv, small["conv_w"] = _conv_bwd(s["z_bcv"], sm["conv_w"], dby)
    delta = delta.reshape(N_DEV, SUBLANES, delta.shape[1])
    dq, dk, dv = _attn_bwd(s["q"], s["k"], s["v"], do, s["lse"], delta, N_DEV)
    dz_qkr, dqp, dkv, small["q_norm"], small["kv_norm"] = _mla_prep_bwd(
        dq, dk, dv, s["z_qkr"], s["rq"], s["rkv"], sm["q_norm"], sm["kv_norm"], cs, w["c256"])
    g_c256 = _tn_heads(s["qn"], s["kvn"], dqp, dkv)
    un = s["un"]
    g_win = _win_merge(_tn_plain(un, d_bcv), _tn_plain(un, dz_qkr[None])[0], _tn_plain(un, dgg[None])[0],
                       w["win"].shape[2])
    dh1, small["mix_norm"] = _mix_in_bwd(d_bcv, dz_qkr, dgg, s["w_bcv"], s["w_qkr"], s["w_gg"], s["h1"], s["rm"],
                                         sm["mix_norm"], dh2)
    return dh1, dict(gu=g_gu, dn=g_dn, sq=g_sq, win=g_win, c128=g_c128, c256=g_c256), small


def _layer_bwd_first(dh1, part, small, s, w, sm, after):
    rows = _grad_rows(w)
    dgu1, dh0, small["ffn1_norm"] = _ffn_bwd(dh1, s["jac1"], w["gu1"], w["dn1"], s["h0"], s["r1"], sm["ffn1_norm"],
                                             after)
    g_dn = _tn_down(s["at1"], dh1, part["dn"], rows["dn"], 0)
    g_gu = _tn_slots(s["n1"], dgu1, part["gu"], rows["gu"], 0)
    return dh0, dict(part, gu=g_gu, dn=g_dn), small


def _mesh_pos():
    return lax.axis_index("x"), lax.axis_index("y"), lax.axis_index("c")


def _other_chips(x, y):
    return [(1 - x, y), (x, 1 - y), (1 - x, 1 - y)]


def _pack(arrs, flipped, width):
    L = arrs[0].shape[0]
    shapes = [a.shape[:0:-1] if f else a.shape[1:] for a, f in zip(arrs, flipped)]
    R = sum(r for r, _ in shapes)

    def body(*refs):
        o_ref = refs[-1]
        off = 0
        for a_ref, f, (r, c) in zip(refs[:-1], flipped, shapes):
            a = a_ref[0].T if f else a_ref[0]
            o_ref[0, off:off + r, 0:c] = a.astype(BF16)
            if c < width:
                o_ref[0, off:off + r, c:width] = jnp.zeros((r, width - c), BF16)
            off += r

    return _call(
        body, "pack", (L,),
        [pl.BlockSpec((1,) + a.shape[1:], lambda l: (l, 0, 0)) for a in arrs],
        pl.BlockSpec((1, R, width), lambda l: (l, 0, 0)),
        _sds((L, R, width), BF16),
    )(*arrs)


def _handshake(peers):
    barrier = pltpu.get_barrier_semaphore()
    for peer in peers:
        pl.semaphore_signal(barrier, inc=1, device_id=peer, device_id_type=MESH)
    pl.semaphore_wait(barrier, len(peers))


def _sequencer_call(body, name, out_types, sems, collective_id, operands):
    return pl.kernel(
        body, name=name, out_type=out_types,
        mesh=plsc.ScalarSubcoreMesh(axis_name="seq", num_cores=1),
        scratch_types=tuple(pltpu.SemaphoreType.DMA((k,)) for k in sems),
        compiler_params=pltpu.CompilerParams(collective_id=collective_id),
    )(*operands)


def _all_gather(packs, l, after, collective_id):
    n = len(packs)

    def body(*refs):
        ins, outs = refs[:n], refs[n + len(after):2 * n + len(after)]
        send_sems, recv_sems, local_sems = refs[2 * n + len(after):]
        x, y, c = _mesh_pos()
        me, sibling = (x, y, c), (x, y, 1 - c)
        chips = _other_chips(x, y)
        _handshake([sibling] + [(*chip, c) for chip in chips])

        def copy(q, k, block, to, src=None):
            slot = outs[q].at[4 * block[0] + 2 * block[1] + block[2]]
            return pltpu.make_async_remote_copy(
                src_ref=slot if src is None else src, dst_ref=slot,
                send_sem=send_sems.at[7 * q + k], recv_sem=recv_sems.at[7 * q + k], device_id=to, device_id_type=MESH)

        started = []
        for q in range(n):
            src = ins[q].at[l]
            mine = pltpu.make_async_copy(src, outs[q].at[4 * x + 2 * y + c], local_sems.at[q])
            mine.start()
            started.append(mine)
        sends = []
        for q in range(n):
            src = ins[q].at[l]
            sends.append(copy(q, 0, me, sibling, src=src))
            sends += [copy(q, 1 + j, me, (*chip, c), src=src) for j, chip in enumerate(chips)]
        for cp in sends:
            cp.start()
        for q in range(n):
            for j, chip in enumerate(chips):
                copy(q, 1 + j, (*chip, c), me).wait_recv()
                fwd = copy(q, 4 + j, (*chip, c), sibling)
                fwd.start()
                sends.append(fwd)
        for q in range(n):
            copy(q, 0, sibling, me).wait_recv()
            for j, chip in enumerate(chips):
                copy(q, 4 + j, (*chip, 1 - c), me).wait_recv()
        for cp in sends:
            cp.wait_send()
        for mine in started:
            mine.wait()

    return _sequencer_call(
        body, f"all_gather_{collective_id}", [_sds((N_DEV,) + p.shape[1:], p.dtype) for p in packs], (7 * n, 7 * n, n),
        collective_id, list(packs) + list(after))


def _rs_d2d(gs, l, collective_id):
    n = len(gs)

    def body(*refs):
        ins, outs = refs[:n], refs[n:2 * n]
        send_sems, recv_sems = refs[2 * n:]
        x, y, c = _mesh_pos()
        _handshake([(x, y, 1 - c)])
        copies = []
        for q in range(n):
            for j in range(4):
                copies.append(pltpu.make_async_remote_copy(
                    src_ref=ins[q].at[2 * j + (1 - c)], dst_ref=outs[q].at[j], send_sem=send_sems.at[4 * q + j],
                    recv_sem=recv_sems.at[4 * q + j], device_id=(x, y, 1 - c), device_id_type=MESH))
        for cp in copies:
            cp.start()
        for cp in copies:
            cp.wait()

    return _sequencer_call(
        body, f"rs_d2d_{l}", [_sds((4,) + g.shape[1:], g.dtype) for g in gs], (4 * n, 4 * n), collective_id, gs)


def _rs_add_chip(gs, as_, after):
    n = len(gs)
    steps = 4
    tiles = [g.shape[1] // steps for g in gs]

    def chip(k):
        x, y, _ = _mesh_pos()
        return ([(x, y)] + _other_chips(x, y))[k]

    def body(*refs):
        g_refs, a_refs = refs[:4 * n], refs[4 * n:8 * n]
        own_refs, t_refs = refs[8 * n + len(after):9 * n + len(after)], refs[9 * n + len(after):]
        for q in range(n):
            g, a = g_refs[4 * q:4 * q + 4], a_refs[4 * q:4 * q + 4]
            own_refs[q][...] = g[0][0].astype(F32) + a[0][0].astype(F32)
            for k in range(1, 4):
                t_refs[q][k - 1] = (g[k][0].astype(F32) + a[k][0].astype(F32)).astype(BF16)

    def gspec(q, k):
        def index(i):
            px, py = chip(k)
            return 4 * px + 2 * py + lax.axis_index("c"), i, 0
        return pl.BlockSpec((1, tiles[q], gs[q].shape[2]), index)

    def aspec(q, k):
        def index(i):
            px, py = chip(k)
            return 2 * px + py, i, 0
        return pl.BlockSpec((1, tiles[q], gs[q].shape[2]), index)

    in_specs = [gspec(q, k) for q in range(n) for k in range(4)] + [aspec(q, k) for q in range(n) for k in range(4)]
    operands = [g for g in gs for _ in range(4)] + [a for a in as_ for _ in range(4)]
    out_specs = [pl.BlockSpec((tiles[q], gs[q].shape[2]), lambda i: (i, 0)) for q in range(n)]
    out_specs += [pl.BlockSpec((3, tiles[q], gs[q].shape[2]), lambda i: (0, i, 0)) for q in range(n)]
    out_shape = [_sds(g.shape[1:], F32) for g in gs] + [_sds((3,) + g.shape[1:], BF16) for g in gs]
    res = _call(body, "rs_add_chip", (steps,), in_specs + [ANY] * len(after), out_specs, out_shape)(*operands, *after)
    return res[:n], res[n:]


def _rs_ici(ts, l, collective_id):
    n = len(ts)

    def body(*refs):
        ins, outs = refs[:n], refs[n:2 * n]
        send_sems, recv_sems = refs[2 * n:]
        x, y, c = _mesh_pos()
        chips = _other_chips(x, y)
        _handshake([(*chip, c) for chip in chips])
        copies = []
        for q in range(n):
            for k, chip in enumerate(chips):
                copies.append(pltpu.make_async_remote_copy(
                    src_ref=ins[q].at[k], dst_ref=outs[q].at[k], send_sem=send_sems.at[3 * q + k],
                    recv_sem=recv_sems.at[3 * q + k], device_id=(*chip, c), device_id_type=MESH))
        for cp in copies:
            cp.start()
        for cp in copies:
            cp.wait()

    return _sequencer_call(
        body, f"rs_ici_{l}", [_sds(t.shape, t.dtype) for t in ts], (3 * n, 3 * n), collective_id, ts)


def _all_reduce_small(v):
    n, W = v.shape

    def body(v_ref, out_ref, slots, send_sems, recv_sems):
        x, y, c = _mesh_pos()
        me = 4 * x + 2 * y + c
        slots[me] = v_ref[...]
        copies = []
        for k in range(1, N_DEV):
            kx, ky, kc = (k >> 2) & 1, (k >> 1) & 1, k & 1
            peer = (1 - x if kx else x, 1 - y if ky else y, 1 - c if kc else c)
            copies.append(pltpu.make_async_remote_copy(
                src_ref=v_ref, dst_ref=slots.at[me], send_sem=send_sems.at[k - 1], recv_sem=recv_sems.at[k - 1],
                device_id=peer, device_id_type=MESH))
        for cp in copies:
            cp.start()
        for cp in copies:
            cp.wait()
        acc = slots[0]
        for d in range(1, N_DEV):
            acc = acc + slots[d]
        out_ref[...] = acc

    vm = pl.BlockSpec(memory_space=pltpu.VMEM)
    return pl.pallas_call(
        body, name="all_reduce_small",
        out_shape=_sds((n, W), F32),
        in_specs=[vm], out_specs=vm,
        scratch_shapes=[pltpu.VMEM((N_DEV, n, W), F32), pltpu.SemaphoreType.DMA((7,)), pltpu.SemaphoreType.DMA((7,))],
    )(v)


def _adamw_math(w, g, m, v):
    m2 = ADAM_B1 * m + (1.0 - ADAM_B1) * g
    v2 = ADAM_B2 * v + (1.0 - ADAM_B2) * (g * g)
    m_hat = m2 / (1.0 - ADAM_B1 ** ADAM_STEP)
    v_hat = v2 / (1.0 - ADAM_B2 ** ADAM_STEP)
    return -ADAM_LR * (m_hat / (jnp.sqrt(v_hat) + ADAM_EPS) + ADAM_WD * w), m2, v2


def _adamw(w, g, m, v):
    L, r, c = w.shape
    tr = _tile(r, max(SUBLANES, (256 * 1024 // c) // SUBLANES * SUBLANES), SUBLANES)

    def body(w_ref, g_ref, m_ref, v_ref, d_ref, nm_ref, nv_ref):
        d_ref[...], nm_ref[...], nv_ref[...] = _adamw_math(w_ref[...], g_ref[...], m_ref[...], v_ref[...])

    spec = pl.BlockSpec((1, tr, c), lambda l, i: (l, i, 0))
    return _call(body, "adamw", (L, r // tr), [spec] * 4, [spec] * 3, [_sds((L, r, c), F32)] * 3)(w, g, m, v)


def _adamw_reduced(w, m, v, flipped, own, b, row_off, tr, l, prev, after):
    L = w.shape[0]
    c, r = w.shape[1:] if flipped else w.shape[:0:-1]
    W = own.shape[1]
    ob = row_off // tr
    extra = list(prev or ()) + list(after)

    def body(w_ref, m_ref, v_ref, own_ref, b_ref, *rest):
        g_ref, d_ref, nm_ref, nv_ref = rest[len(extra):]
        g = ((own_ref[...] + b_ref[0].astype(F32)) + b_ref[1].astype(F32)) + b_ref[2].astype(F32)
        g = g[:, :c].T if flipped else g[:, :c]
        g_ref[0] = g
        d_ref[0], nm_ref[0], nv_ref[0] = _adamw_math(w_ref[0], g, m_ref[0], v_ref[0])

    spec = pl.BlockSpec((1, c, tr), lambda i: (l, 0, i)) if flipped else pl.BlockSpec((1, tr, c), lambda i: (l, i, 0))
    return _call(
        body, "adamw_reduced", (r // tr,),
        [spec] * 3 + [pl.BlockSpec((tr, W), lambda i: (ob + i, 0)), pl.BlockSpec((3, tr, W), lambda i: (0, ob + i, 0))]
        + [ANY] * len(extra),
        [spec] * 4, [_sds(w.shape, F32)] * 4,
        aliases={5 + k: k for k in range(4)} if prev else None,
    )(w, m, v, own, b, *extra)


_MEMBERS = dict(gu=("ffn1_w_gu", "ffn2_w_gu"), dn=("ffn1_w_down", "ffn2_w_down"),
                sq=("w_mla_out", "w_o", "w_ple_gate"), win=("w_in",), c128=("w_conv_out", "w_ple_proj"),
                c256=("w_ukv", "w_uq"))
_GATHER_MEMBERS = dict(_MEMBERS, gu1=("ffn1_w_gu",), gu2=("ffn2_w_gu",), dn1=("ffn1_w_down",), dn2=("ffn2_w_down",))
GATHER_STAGES = (("gu1", "dn1"), ("win", "c256", "c128", "sq"), ("gu2", "dn2"))
_FLIPPED = ("ffn1_w_gu", "ffn2_w_gu", "w_in", "w_uq")
_SMALL = ("ffn1_norm", "mix_norm", "q_norm", "kv_norm", "ffn2_norm", "ple_norm")
_ORDER = ("ffn1_norm", "ffn1_w_gu", "ffn1_w_down", "mix_norm", "w_in", "conv_w", "w_conv_out", "q_norm", "kv_norm",
          "w_uq", "w_ukv", "w_mla_out", "w_o", "ffn2_norm", "ffn2_w_gu", "ffn2_w_down", "ple_norm", "w_ple_gate",
          "w_ple_proj", "final_norm")


def _class_width(wts, cls):
    return HEAD_SLOT if cls == "c256" else wts[_GATHER_MEMBERS[cls][0]].shape[2]


def _pack_rows(vecs, width):
    flat = jnp.concatenate([a.reshape(-1) for a in vecs])
    n = flat.shape[0]
    rows = -(-n // width)
    rows = -(-rows // SUBLANES) * SUBLANES
    flat = jnp.pad(flat, (0, rows * width - n))
    offs, o = [], 0
    for a in vecs:
        offs.append(o)
        o += a.size
    return flat.reshape(rows, width), offs


def _unpack_rows(packed, vecs, offs):
    flat = packed.reshape(-1)
    return [flat[o:o + a.size].reshape(a.shape) for a, o in zip(vecs, offs)]


def _train(x, p, positions, target, gathered, packs, small_w, final_norm, update):
    cs = _rope_tables(positions)
    L = len(small_w)
    h = x
    saved = []
    def gather(l, names, after, collective_id):
        got = _all_gather([packs[n] for n in names], l, after, collective_id)
        return dict(zip(names, got))

    late = None
    if packs is not None:
        first, mixer, second = GATHER_STAGES
        w0 = gather(0, first, [], 0)
        w0.update(gather(0, mixer, [w0[first[0]]], 1))
        gathered = [w0]
        late = lambda h1: gather(0, second, [h1], 2)
    everything = sum(GATHER_STAGES, ())
    for l in range(L):
        h, s = _layer_fwd(h, p[l], cs, gathered[l], small_w[l], late)
        late = None
        saved.append(s)
        if packs is not None and l + 1 < L:
            gathered.append(gather(l + 1, everything, [s["by"]], 2 + l + 1))
    dh, loss, d_final = _final_loss(h, final_norm, target)
    grads, smalls = [None] * L, [None] * L
    exchanged = None
    landing = None

    def second_stage(after):
        l, gs, as_ = exchanged
        owns, ts = _rs_add_chip(gs, as_, [after])
        return l, owns, _rs_ici(ts, l, 2 * L + 2 + l)

    for l in reversed(range(L)):
        dh, part, small = _layer_bwd_late(dh, saved[l], gathered[l], small_w[l], [])
        pin = []
        if exchanged is not None:
            landing = second_stage(dh)
            pin = [landing[1][0]]
        dh, part, small = _layer_bwd_mixer(dh, part, small, saved[l], cs, gathered[l], small_w[l], pin)
        pin = [update(*landing)] if exchanged is not None else []
        dh, g, smalls[l] = _layer_bwd_first(dh, part, small, saved[l], gathered[l], small_w[l], pin)
        if update is not None:
            gs = [g[cls] for cls in CLASSES]
            exchanged = (l, gs, _rs_d2d(gs, l, L + 2 + l))
        else:
            grads[l] = g
    if update is not None:
        update(*second_stage(dh))
    return loss[0, 0], dh, grads, smalls, d_final


def kernel(x, p, positions, ffn1_norm, ffn1_w_gu, ffn1_w_down, mix_norm, w_in, conv_w, w_conv_out, q_norm, kv_norm, w_uq, w_ukv, w_mla_out, w_o, ffn2_norm, ffn2_w_gu, ffn2_w_down, ple_norm, w_ple_gate, w_ple_proj, final_norm, loss_target, m_ffn1_norm, m_ffn1_w_gu, m_ffn1_w_down, m_mix_norm, m_w_in, m_conv_w, m_w_conv_out, m_q_norm, m_kv_norm, m_w_uq, m_w_ukv, m_w_mla_out, m_w_o, m_ffn2_norm, m_ffn2_w_gu, m_ffn2_w_down, m_ple_norm, m_w_ple_gate, m_w_ple_proj, m_final_norm, v_ffn1_norm, v_ffn1_w_gu, v_ffn1_w_down, v_mix_norm, v_w_in, v_conv_w, v_w_conv_out, v_q_norm, v_kv_norm, v_w_uq, v_w_ukv, v_w_mla_out, v_w_o, v_ffn2_norm, v_ffn2_w_gu, v_ffn2_w_down, v_ple_norm, v_w_ple_gate, v_w_ple_proj, v_final_norm):
    args = dict(locals())
    wts = {n: args[n] for n in _ORDER}
    L = w_in.shape[0]
    dev = 4 * lax.axis_index("x") + 2 * lax.axis_index("y") + lax.axis_index("c")

    view = lambda n, a: jnp.swapaxes(a, 1, 2) if n in _FLIPPED else a
    packs = {cls: _pack([view(n, wts[n]) for n in _GATHER_MEMBERS[cls]], [n in _FLIPPED for n in _GATHER_MEMBERS[cls]],
                        _class_width(wts, cls))
             for stage in GATHER_STAGES for cls in stage}
    cw = conv_w.shape[2]
    conv_full = lax.dynamic_update_slice(jnp.zeros((L, 3, N_DEV * cw), F32), conv_w, (0, 0, dev * cw))
    conv_packed, conv_offs = _pack_rows([conv_full], FLAT_COLS)
    conv_full = _unpack_rows(_all_reduce_small(conv_packed), [conv_full], conv_offs)[0]
    small_w = [dict({n: wts[n][l][None, :] for n in _SMALL}, conv_w=conv_full[l]) for l in range(L)]

    done = {}

    def update(l, owns, bs):
        for q, cls in enumerate(CLASSES):
            off = 0
            rows = [wts[n].shape[1] for n in _MEMBERS[cls]]
            tr = _tile(math.gcd(*rows), 256, BF16_ROWS)
            for n, r in zip(_MEMBERS[cls], rows):
                done[n] = _adamw_reduced(view(n, wts[n]), view(n, args["m_" + n]), view(n, args["v_" + n]),
                                         n in _FLIPPED, owns[q], bs[q], off, tr, l, done.get(n), [])
                off += r
        return done[_MEMBERS[CLASSES[-1]][-1]][0]

    loss_dev, grad_x, _, smalls, d_final = _train(x[0], p[:, 0], positions[0], loss_target[0], None, packs, small_w,
                                                  final_norm[None, :], update)

    small = [jnp.stack([smalls[l][n][0] for l in range(L)]) for n in _SMALL]
    small += [jnp.stack([smalls[l]["conv_w"] for l in range(L)]), d_final[0], loss_dev[None]]
    packed, offs = _pack_rows(small, FLAT_COLS)
    small = _unpack_rows(_all_reduce_small(packed), small, offs)
    grad = dict(zip(_SMALL, small))
    grad["conv_w"] = lax.dynamic_slice(small[len(_SMALL)], (0, 0, dev * cw), (L, 3, cw))
    grad["final_norm"] = small[-2]
    loss = small[-1][0]

    deltas, new_m, new_v = {}, {}, {}
    for n, outs in done.items():
        grad[n], deltas[n], new_m[n], new_v[n] = (view(n, a) for a in outs)
    for n in _SMALL + ("conv_w", "final_norm"):
        w3 = wts[n].reshape((1,) * (3 - wts[n].ndim) + wts[n].shape)
        d, nm, nv = _adamw(w3, grad[n].reshape(w3.shape), args["m_" + n].reshape(w3.shape),
                           args["v_" + n].reshape(w3.shape))
        deltas[n], new_m[n], new_v[n] = (a.reshape(wts[n].shape) for a in (d, nm, nv))
    return (loss, grad_x[None], *[grad[n] for n in _ORDER], *[deltas[n] for n in _ORDER],
            *[new_m[n] for n in _ORDER], *[new_v[n] for n in _ORDER])
```

```python
import functools
import math

import jax
import jax.numpy as jnp
from jax import lax
from jax.experimental import pallas as pl
from jax.experimental.pallas import tpu as pltpu
from jax.experimental.pallas import tpu_sc as plsc

F32 = jnp.float32
BF16 = jnp.bfloat16

CHUNK = 64
NOPE = 128
ROPE = 64
VDIM = 128
ROPE_THETA = 10000.0
EPS = 1e-6
ATTN_SCALE = (NOPE + ROPE) ** -0.5
SCORE_SCALE = ATTN_SCALE * math.log2(math.e)
LN2 = math.log(2.0)
ADAM_LR = 0.001
ADAM_B1 = 0.9
ADAM_B2 = 0.999
ADAM_EPS = 1e-08
ADAM_WD = 0.01
ADAM_STEP = 10

LANES = 128
SUBLANES = 8
BF16_ROWS = 16
V7X_VMEM_BYTES = 64 * 1024 * 1024
VMEM_LIMIT = V7X_VMEM_BYTES * 7 // 8
HEAD_SLOT = 2 * LANES
N_DEV = 8
ATTN_FWD_WIDTH = 8
ATTN_BWD_WIDTH = 4
FLAT_COLS = 1024
CLASSES = ("gu", "dn", "sq", "win", "c128", "c256")

NT = (((1,), (1,)), ((), ()))
MESH = pl.DeviceIdType.MESH
ANY = pl.BlockSpec(memory_space=pl.ANY)


def _dot(a, b):
    return jnp.dot(a, b, preferred_element_type=F32)


def _dot_nt(a, b):
    return lax.dot_general(a, b, NT, preferred_element_type=F32)


def _sig(x):
    return 1.0 / (1.0 + jnp.exp(-x))


def _tile(n, pref, unit):
    if n <= pref:
        return n
    t = (pref // unit) * unit
    while t >= unit:
        if n % t == 0:
            return t
        t -= unit
    return n


def _call(body, name, grid, in_specs, out_specs, out_shape, scratch=(), aliases=None):
    return pl.pallas_call(
        body,
        name=name,
        grid=grid,
        in_specs=in_specs,
        out_specs=out_specs,
        out_shape=out_shape,
        scratch_shapes=list(scratch),
        input_output_aliases=aliases or {},
        compiler_params=pltpu.CompilerParams(
            dimension_semantics=("arbitrary",) * len(grid), vmem_limit_bytes=VMEM_LIMIT
        ),
    )


def _sds(shape, dtype):
    return jax.ShapeDtypeStruct(shape, dtype)


def _rms_fwd(x, gain):
    rstd = lax.rsqrt(jnp.mean(x * x, axis=-1, keepdims=True) + EPS)
    return x * rstd * gain, rstd


def _rms_bwd(dn, x, rstd, gain):
    xhat = x * rstd
    dgy = dn * gain
    dx = rstd * (dgy - xhat * jnp.mean(dgy * xhat, axis=-1, keepdims=True))
    return dx, jnp.sum(dn * xhat, axis=0, keepdims=True)


def _rows(tm, w):
    return pl.BlockSpec((tm, w), lambda i: (i, 0))


def _whole(a):
    nd = a.ndim
    return pl.BlockSpec(a.shape, lambda i: (0,) * nd, pipeline_mode=pl.Buffered(1))


def _slab(buf, rows, index):
    return pl.BlockSpec((N_DEV, rows, buf.shape[2]), lambda i: (0, index, 0), pipeline_mode=pl.Buffered(1))


def _cat_slots(w):
    return jnp.concatenate([w[d] for d in range(N_DEV)], axis=1)


def _down_weight(w_ref, d, c):
    return w_ref[2 * d:2 * d + 2].reshape(c, w_ref.shape[2])


def _ffn_fwd(h, gain, gu_w, dn_w):
    S, D = h.shape
    c = gu_w.shape[2]
    tm = _tile(S, 512, SUBLANES)
    nb = N_DEV // 2

    def body(h_ref, gain_ref, w_ref, wd_ref, o_ref, jac_ref, at_ref, n_ref, r_ref):
        x = h_ref[...]
        n32, rstd = _rms_fwd(x, gain_ref[...])
        n = n32.astype(BF16)
        n_ref[...] = n.T
        r_ref[...] = rstd
        acc = jnp.zeros((tm, D), F32)
        for d in range(nb):
            g = _dot(n, w_ref[d])
            u = _dot(n, w_ref[nb + d])
            sg = _sig(g)
            silu = g * sg
            a = (silu * u).astype(BF16)
            at_ref[d] = a.T
            jac_ref[d] = (0.5 * u * (sg + silu * (1.0 - sg))).astype(BF16)
            jac_ref[nb + d] = (0.5 * silu).astype(BF16)
            acc = acc + _dot(a, _down_weight(wd_ref, d, c))
        o_ref[...] = x + 0.5 * acc

    return _call(
        body, "ffn_fwd", (S // tm,),
        [_rows(tm, D), _whole(gain), _slab(gu_w, D, 0), _slab(dn_w, c // 2, 0)],
        [_rows(tm, D), pl.BlockSpec((N_DEV, tm, c), lambda i: (0, i, 0)),
         pl.BlockSpec((nb, c, tm), lambda i: (0, 0, i)), pl.BlockSpec((D, tm), lambda i: (0, i)), _rows(tm, 1)],
        [_sds((S, D), F32), _sds((N_DEV, S, c), BF16), _sds((nb, c, S), BF16), _sds((D, S), BF16),
         _sds((S, 1), F32)],
    )(h, gain, gu_w, dn_w)


def _win_segments(C, QL, KVL, D):
    o1, o2 = 3 * C, 3 * C + QL + KVL + ROPE
    return [("bcv", k, k * C, (k + 1) * C) for k in range(3)] + [("qkr", None, o1, o2), ("gg", None, o2, o2 + 2 * D)]


def _win_pieces(segments, cw):
    out = []
    for tgt, lead, a, b in segments:
        for d in range(N_DEV):
            lo, hi = max(a, d * cw), min(b, (d + 1) * cw)
            if lo < hi:
                out.append((tgt, lead, d, (lo - d * cw, hi - d * cw), (lo - a, hi - a)))
    return out


def _win_split(win_w, C, QL, KVL):
    _, D, cw = win_w.shape
    WQ = QL + KVL + LANES
    pieces = _win_pieces(_win_segments(C, QL, KVL, D), cw)
    tr = _tile(D, 256, BF16_ROWS)

    def body(w_ref, bcv_ref, qkr_ref, gg_ref):
        tgt = dict(bcv=bcv_ref, qkr=qkr_ref, gg=gg_ref)
        qkr_ref[:, QL + KVL + ROPE:] = jnp.zeros((tr, LANES - ROPE), BF16)
        for name, lead, d, (s0, s1), (t0, t1) in pieces:
            v = w_ref[d, :, s0:s1]
            if lead is None:
                tgt[name][:, t0:t1] = v
            else:
                tgt[name][lead, :, t0:t1] = v

    return _call(
        body, "win_split", (D // tr,),
        [pl.BlockSpec((N_DEV, tr, cw), lambda i: (0, i, 0))],
        [pl.BlockSpec((3, tr, C), lambda i: (0, i, 0)), _rows(tr, WQ), _rows(tr, 2 * D)],
        [_sds((3, D, C), BF16), _sds((D, WQ), BF16), _sds((D, 2 * D), BF16)],
    )(win_w)


def _win_merge(d_bcv, d_qkr, d_gg, cw):
    _, D, C = d_bcv.shape
    WQ = d_qkr.shape[1]
    QL_KVL = WQ - LANES
    o1 = 3 * C
    segments = [("bcv", k, k * C, (k + 1) * C) for k in range(3)]
    segments += [("qkr", None, o1, o1 + QL_KVL + ROPE), ("gg", None, o1 + QL_KVL + ROPE, o1 + QL_KVL + ROPE + 2 * D)]
    pieces = _win_pieces(segments, cw)
    tr = _tile(D, 256, BF16_ROWS)

    def body(bcv_ref, qkr_ref, gg_ref, o_ref):
        src = dict(bcv=bcv_ref, qkr=qkr_ref, gg=gg_ref)
        for name, lead, d, (s0, s1), (t0, t1) in pieces:
            v = src[name][:, t0:t1] if lead is None else src[name][lead, :, t0:t1]
            o_ref[d, :, s0:s1] = v.astype(BF16)

    return _call(
        body, "win_merge", (D // tr,),
        [pl.BlockSpec((3, tr, C), lambda i: (0, i, 0)), _rows(tr, WQ), _rows(tr, 2 * D)],
        pl.BlockSpec((N_DEV, tr, cw), lambda i: (0, i, 0)),
        _sds((N_DEV, D, cw), BF16),
    )(d_bcv, d_qkr, d_gg)


def _mix_in(h, gain, w_bcv, w_qkr, w_gg):
    S, D = h.shape
    C = w_bcv.shape[2]
    tm = _tile(S, 512, SUBLANES)

    def body(h_ref, gain_ref, w1, w2, w3, o1, o2, o3, n_ref, r_ref):
        n32, rstd = _rms_fwd(h_ref[...], gain_ref[...])
        n = n32.astype(BF16)
        n_ref[...] = n.T
        r_ref[...] = rstd
        for k in range(3):
            o1[k] = _dot(n, w1[k]).astype(BF16)
        o2[...] = _dot(n, w2[...])
        o3[...] = _dot(n, w3[...]).astype(BF16)

    return _call(
        body, "mix_in", (S // tm,),
        [_rows(tm, D), _whole(gain), _whole(w_bcv), _whole(w_qkr), _whole(w_gg)],
        [pl.BlockSpec((3, tm, C), lambda i: (0, i, 0)), _rows(tm, w_qkr.shape[1]), _rows(tm, 2 * D),
         pl.BlockSpec((D, tm), lambda i: (0, i)), _rows(tm, 1)],
        [_sds((3, S, C), BF16), _sds((S, w_qkr.shape[1]), F32), _sds((S, 2 * D), BF16), _sds((D, S), BF16),
         _sds((S, 1), F32)],
    )(h, gain, w_bcv, w_qkr, w_gg)


def _conv_taps(zc):
    rows = lax.broadcasted_iota(jnp.int32, zc.shape, 0)
    z1 = jnp.where(rows >= 1, pltpu.roll(zc, 1, 0), 0.0)
    z2 = jnp.where(rows >= 2, pltpu.roll(zc, 2, 0), 0.0)
    return z1, z2


def _conv_fwd(z_bcv, conv_w):
    _, S, C = z_bcv.shape

    def body(z_ref, w_ref, o_ref):
        w = w_ref[...]
        zc = z_ref[1].astype(F32) * z_ref[2].astype(F32)
        z1, z2 = _conv_taps(zc)
        y = w[0:1] * z2 + w[1:2] * z1 + w[2:3] * zc
        o_ref[...] = (z_ref[0].astype(F32) * y).astype(BF16)

    return _call(
        body, "conv_fwd", (C // LANES,),
        [pl.BlockSpec((3, S, LANES), lambda j: (0, 0, j)), pl.BlockSpec((3, LANES), lambda j: (0, j))],
        pl.BlockSpec((S, LANES), lambda j: (0, j)),
        _sds((S, C), BF16),
    )(z_bcv, conv_w)


def _rope(x, cs, half):
    c, s1, s2 = cs[:, :LANES], cs[:, LANES:2 * LANES], cs[:, 2 * LANES:]
    return x * c + pltpu.roll(x, LANES - half, 1) * s1 + pltpu.roll(x, half, 1) * s2


def _unrope(d, cs, half):
    c, s1, s2 = cs[:, :LANES], cs[:, LANES:2 * LANES], cs[:, 2 * LANES:]
    return d * c + pltpu.roll(d * s1, half, 1) + pltpu.roll(d * s2, LANES - half, 1)


def _mla_prep(z_qkr, gq, gkv, cs, c256_w):
    S = z_qkr.shape[0]
    QL, KVL = gq.shape[1], gkv.shape[1]
    H = N_DEV
    tm = _tile(S, 512, SUBLANES)
    half = ROPE // 2

    def body(z_ref, gq_ref, gkv_ref, cs_ref, w_ref, q_ref, k_ref, v_ref, qn_ref, kvn_ref, rq_ref, rkv_ref):
        z = z_ref[...]
        cs_t = cs_ref[...]
        qn32, rq = _rms_fwd(z[:, :QL], gq_ref[...])
        kvn32, rkv = _rms_fwd(z[:, QL:QL + KVL], gkv_ref[...])
        qn = qn32.astype(BF16)
        kvn = kvn32.astype(BF16)
        qn_ref[...] = qn.T
        kvn_ref[...] = kvn.T
        rq_ref[...] = rq
        rkv_ref[...] = rkv
        krope = _rope(z[:, QL + KVL:], cs_t, half).astype(BF16)
        for h in range(H):
            lo, mid, hi = h * HEAD_SLOT, h * HEAD_SLOT + LANES, (h + 1) * HEAD_SLOT
            q = _dot(qn, w_ref[h, KVL:KVL + QL, :])
            kv = _dot(kvn, w_ref[h, 0:KVL, :])
            q_ref[:, lo:mid] = (q[:, :LANES] * SCORE_SCALE).astype(BF16)
            q_ref[:, mid:hi] = (_rope(q[:, LANES:], cs_t, half) * SCORE_SCALE).astype(BF16)
            k_ref[:, lo:mid] = kv[:, :LANES].astype(BF16)
            k_ref[:, mid:hi] = krope
            v_ref[:, h * VDIM:(h + 1) * VDIM] = kv[:, LANES:].astype(BF16)

    return _call(
        body, "mla_prep", (S // tm,),
        [_rows(tm, z_qkr.shape[1]), _whole(gq), _whole(gkv), _rows(tm, 3 * LANES), _whole(c256_w)],
        [_rows(tm, H * HEAD_SLOT), _rows(tm, H * HEAD_SLOT), _rows(tm, H * VDIM),
         pl.BlockSpec((QL, tm), lambda i: (0, i)), pl.BlockSpec((KVL, tm), lambda i: (0, i)),
         _rows(tm, 1), _rows(tm, 1)],
        [_sds((S, H * HEAD_SLOT), BF16), _sds((S, H * HEAD_SLOT), BF16), _sds((S, H * VDIM), BF16),
         _sds((QL, S), BF16), _sds((KVL, S), BF16), _sds((S, 1), F32), _sds((S, 1), F32)],
    )(z_qkr, gq, gkv, cs, c256_w)


def _chunk_mask(rows, cols, diagonal_row):
    shift = CHUNK.bit_length() - 1
    krow = (lax.broadcasted_iota(jnp.int32, (rows, cols), 0) - diagonal_row) >> shift
    qcol = lax.broadcasted_iota(jnp.int32, (rows, cols), 1) >> shift
    return krow <= qcol


def _attn_fwd(q, k, v, H):
    S = q.shape[0]
    t = _tile(S, 512, CHUNK)
    nq = S // t

    def body(q_ref, k_ref, v_ref, o_ref, ot_ref, lse_ref, vt_ref):
        qi = pl.program_id(1)

        @pl.when(qi == 0)
        def _():
            vt_ref[0:VDIM, :] = v_ref[...].T
            vt_ref[VDIM:, :] = jnp.ones((BF16_ROWS, S), BF16)

        qv = q_ref[...]

        def block(start, width, carry, masked):
            m, acc = carry
            off = pl.multiple_of(start * t, t)
            s = _dot_nt(k_ref[pl.ds(off, width * t), :], qv)
            if masked:
                s = jnp.where(_chunk_mask(width * t, t, (width - 1) * t), s, -1e30)
            m_new = jnp.maximum(m, jnp.max(s, axis=0, keepdims=True))
            p = jnp.exp2(s - m_new).astype(BF16)
            acc = jnp.exp2(m - m_new) * acc + _dot(vt_ref[:, pl.ds(off, width * t)], p)
            return m_new, acc

        init = (jnp.full((1, t), -1e30, F32), jnp.zeros((VDIM + BF16_ROWS, t), F32))
        wide = lax.div(qi, ATTN_FWD_WIDTH)
        carry = lax.fori_loop(0, wide, lambda j, c: block(j * ATTN_FWD_WIDTH, ATTN_FWD_WIDTH, c, False), init)
        left = qi - wide * ATTN_FWD_WIDTH
        for extra in range(ATTN_FWD_WIDTH):
            @pl.when(left == extra)
            def _():
                m, acc = block(qi - extra, extra + 1, carry, True)
                l = acc[VDIM:VDIM + 1]
                out = (acc[0:VDIM] * (1.0 / l)).astype(BF16)
                ot_ref[...] = out
                o_ref[...] = out.T
                lse_ref[0] = jnp.broadcast_to(m + jnp.log2(l), (SUBLANES, t))

    return _call(
        body, "attn_fwd", (H, nq),
        [pl.BlockSpec((t, HEAD_SLOT), lambda h, i: (i, h)), pl.BlockSpec((S, HEAD_SLOT), lambda h, i: (0, h)),
         pl.BlockSpec((S, VDIM), lambda h, i: (0, h))],
        [pl.BlockSpec((t, VDIM), lambda h, i: (i, h)), pl.BlockSpec((VDIM, t), lambda h, i: (h, i)),
         pl.BlockSpec((1, SUBLANES, t), lambda h, i: (h, 0, i))],
        [_sds((S, H * VDIM), BF16), _sds((H * VDIM, S), BF16), _sds((H, SUBLANES, S), F32)],
        [pltpu.VMEM((VDIM + BF16_ROWS, S), BF16)],
    )(q, k, v)


def _merge_wo(o, by, z_gg, h, sq_w, c128_w):
    S, D = h.shape
    C = by.shape[1]
    r = sq_w.shape[1] // 3
    tm = _tile(S, 512, SUBLANES)

    def body(o_ref, by_ref, gg_ref, h_ref, wmo_ref, wo_ref, wco_ref, h2_ref, mg_ref, yc_ref, ym_ref):
        ymla = _dot(o_ref[...], wmo_ref[...].reshape(N_DEV * r, D))
        yconv = _dot(by_ref[...], _cat_slots(wco_ref))
        gg = gg_ref[...].astype(F32)
        merged = (_sig(gg[:, :D]) * yconv + _sig(gg[:, D:]) * ymla).astype(BF16)
        mg_ref[...] = merged.T
        yc_ref[...] = yconv.astype(BF16)
        ym_ref[...] = ymla.astype(BF16)
        h2_ref[...] = h_ref[...] + _dot(merged, wo_ref[...].reshape(N_DEV * r, D))

    return _call(
        body, "merge_wo", (S // tm,),
        [_rows(tm, o.shape[1]), _rows(tm, C), _rows(tm, 2 * D), _rows(tm, D), _slab(sq_w, r, 0), _slab(sq_w, r, 1),
         _slab(c128_w, C, 0)],
        [_rows(tm, D), pl.BlockSpec((D, tm), lambda i: (0, i)), _rows(tm, D), _rows(tm, D)],
        [_sds((S, D), F32), _sds((D, S), BF16), _sds((S, D), BF16), _sds((S, D), BF16)],
    )(o, by, z_gg, h, sq_w, sq_w, c128_w)


def _ple_fwd(h, gain, p, sq_w, c128_w, C):
    S, D = h.shape
    P = p.shape[1]
    r = sq_w.shape[1] // 3
    tm = _tile(S, 512, SUBLANES)

    def body(h_ref, gain_ref, p_ref, wpg_ref, wpp_ref, o_ref, pre_ref, pp_ref, n_ref, r_ref):
        x = h_ref[...]
        n32, rstd = _rms_fwd(x, gain_ref[...])
        n = n32.astype(BF16)
        n_ref[...] = n.T
        r_ref[...] = rstd
        pre = _dot(n, wpg_ref[...].reshape(N_DEV * r, D))
        pp = _dot(p_ref[...].astype(BF16), _cat_slots(wpp_ref))
        pre_ref[...] = pre.astype(BF16)
        pp_ref[...] = pp.astype(BF16)
        o_ref[...] = x + _sig(pre) * pp

    return _call(
        body, "ple_fwd", (S // tm,),
        [_rows(tm, D), _whole(gain), _rows(tm, P), _slab(sq_w, r, 2), _slab(c128_w, P, C // P)],
        [_rows(tm, D), _rows(tm, D), _rows(tm, D), pl.BlockSpec((D, tm), lambda i: (0, i)), _rows(tm, 1)],
        [_sds((S, D), F32), _sds((S, D), BF16), _sds((S, D), BF16), _sds((D, S), BF16), _sds((S, 1), F32)],
    )(h, gain, p, sq_w, c128_w)


def _final_loss(h, gain, target):
    S, D = h.shape
    tm = _tile(S, 512, SUBLANES)

    def body(h_ref, gain_ref, t_ref, dh_ref, loss_ref, dg_ref):
        @pl.when(pl.program_id(0) == 0)
        def _():
            loss_ref[...] = jnp.zeros_like(loss_ref)
            dg_ref[...] = jnp.zeros_like(dg_ref)

        x = h_ref[...]
        gain_v = gain_ref[...]
        y, rstd = _rms_fwd(x, gain_v)
        err = y - t_ref[...]
        loss_ref[...] += 0.5 * jnp.sum(jnp.mean(err * err, axis=-1, keepdims=True))
        dx, dgain = _rms_bwd(err * (1.0 / D), x, rstd, gain_v)
        dh_ref[...] = dx
        dg_ref[...] += dgain

    return _call(
        body, "final_loss", (S // tm,),
        [_rows(tm, D), _whole(gain), _rows(tm, D)],
        [_rows(tm, D), pl.BlockSpec((1, LANES), lambda i: (0, 0)), pl.BlockSpec((1, D), lambda i: (0, 0))],
        [_sds((S, D), F32), _sds((1, LANES), F32), _sds((1, D), F32)],
    )(h, gain, target)


def _tn_call(body, name, grid, in_specs, out_spec, out_shape, scratch, operands, prev):
    n = len(operands)
    if prev is None:
        return _call(body, name, grid, in_specs, out_spec, out_shape, scratch)(*operands)
    assert prev.shape == out_shape.shape and prev.dtype == out_shape.dtype

    def wrapped(*refs):
        body(*refs[:n], *refs[n + 1:])

    return _call(wrapped, name, grid, in_specs + [ANY], out_spec, out_shape, scratch, {n: 0})(*operands, prev)


def _tn_slots(xt, dy, prev, rows_total, row_off):
    K, S = xt.shape
    B, _, c = dy.shape
    tk = _tile(K, 1024, BF16_ROWS)

    def body(xt_ref, dy_ref, o_ref):
        o_ref[0] = _dot(xt_ref[...], dy_ref[0]).astype(BF16)

    return _tn_call(
        body, "tn_slots", (K // tk, B),
        [pl.BlockSpec((tk, S), lambda i, b: (i, 0)), pl.BlockSpec((1, S, c), lambda i, b: (b, 0, 0))],
        pl.BlockSpec((1, tk, c), lambda i, b: (b, row_off // tk + i, 0)),
        _sds((B, rows_total, c), BF16), [], [xt, dy], prev)


def _tn_plain(xt, dy):
    K, S = xt.shape
    B, _, c = dy.shape
    tk = _tile(K, 512, BF16_ROWS)
    tn = _tile(c, 1024, LANES)

    def body(xt_ref, dy_ref, o_ref):
        o_ref[0] = _dot(xt_ref[...], dy_ref[0])

    return _call(
        body, "tn_plain", (K // tk, B, c // tn),
        [pl.BlockSpec((tk, S), lambda i, b, j: (i, 0)), pl.BlockSpec((1, S, tn), lambda i, b, j: (b, 0, j))],
        pl.BlockSpec((1, tk, tn), lambda i, b, j: (b, i, j)),
        _sds((B, K, c), F32),
    )(xt, dy)


def _tn_down(at, dh, prev, rows_total, which):
    nb, c, S = at.shape
    D = dh.shape[1]
    r = c // 2
    tn = _tile(D, 512, LANES)

    def body(at_ref, dh_ref, o_ref):
        g = 0.5 * _dot(at_ref[0], dh_ref[...].astype(BF16))
        o_ref[...] = g.astype(BF16).reshape(2, r, tn)

    return _tn_call(
        body, "tn_down", (D // tn, nb),
        [pl.BlockSpec((1, c, S), lambda j, i: (i, 0, 0)), pl.BlockSpec((S, tn), lambda j, i: (0, j))],
        pl.BlockSpec((2, r, tn), lambda j, i: (i, which, j)),
        _sds((N_DEV, rows_total, D), BF16), [], [at, dh], prev)


def _tn_square(xt, dy, prev, rows_total, member):
    K, S = xt.shape
    N = dy.shape[1]
    r = K // N_DEV
    tk = _tile(K, 512, r)
    tn = _tile(N, 512, LANES)

    def body(xt_ref, dy_ref, o_ref):
        g = _dot(xt_ref[...], dy_ref[...].astype(BF16))
        o_ref[...] = g.astype(BF16).reshape(tk // r, r, tn)

    return _tn_call(
        body, "tn_square", (N // tn, K // tk),
        [pl.BlockSpec((tk, S), lambda j, i: (i, 0)), pl.BlockSpec((S, tn), lambda j, i: (0, j))],
        pl.BlockSpec((tk // r, r, tn), lambda j, i: (i, member, j)),
        _sds((N_DEV, rows_total, N), BF16), [], [xt, dy], prev)


def _tn_cols(x, dy, prev, rows_total, row_block):
    S, K = x.shape
    N = dy.shape[1]
    cw = N // N_DEV

    def body(x_ref, dy_ref, o_ref):
        g = _dot(x_ref[...].astype(BF16).T, dy_ref[...])
        for d in range(N_DEV):
            o_ref[d] = g[:, d * cw:(d + 1) * cw].astype(BF16)

    return _tn_call(
        body, "tn_cols", (1,),
        [pl.BlockSpec((S, K), lambda i: (0, 0)), pl.BlockSpec((S, N), lambda i: (0, 0))],
        pl.BlockSpec((N_DEV, K, cw), lambda i: (0, row_block, 0)),
        _sds((N_DEV, rows_total, cw), BF16), [], [x, dy], prev)


def _tn_heads(qnt, kvnt, dqp, dkv):
    QL, S = qnt.shape
    KVL = kvnt.shape[0]

    def body(qn_ref, kvn_ref, dq_ref, dkv_ref, o_ref):
        o_ref[0, 0:KVL, :] = _dot(kvn_ref[...], dkv_ref[...]).astype(BF16)
        o_ref[0, KVL:KVL + QL, :] = _dot(qn_ref[...], dq_ref[...]).astype(BF16)

    head = pl.BlockSpec((S, HEAD_SLOT), lambda h: (0, h))
    return _call(
        body, "tn_heads", (N_DEV,),
        [pl.BlockSpec((QL, S), lambda h: (0, 0)), pl.BlockSpec((KVL, S), lambda h: (0, 0)), head, head],
        pl.BlockSpec((1, KVL + QL, HEAD_SLOT), lambda h: (h, 0, 0)),
        _sds((N_DEV, KVL + QL, HEAD_SLOT), BF16),
    )(qnt, kvnt, dqp, dkv)


def _ple_bwd(dh, pre, pp, h, rstd, gain, sq_w, after):
    S, D = h.shape
    r = sq_w.shape[1] // 3
    tm = _tile(S, 512, SUBLANES)

    def body(dh_ref, pre_ref, pp_ref, h_ref, r_ref, gain_ref, wpg_ref, *rest):
        o_ref, dpre_ref, dpp_ref, dg_ref = rest[len(after):]

        @pl.when(pl.program_id(0) == 0)
        def _():
            dg_ref[...] = jnp.zeros_like(dg_ref)

        d = dh_ref[...]
        gate = _sig(pre_ref[...].astype(F32))
        dpre = (d * pp_ref[...].astype(F32) * gate * (1.0 - gate)).astype(BF16)
        dpre_ref[...] = dpre
        dpp_ref[...] = (d * gate).astype(BF16)
        dn = _dot_nt(dpre, wpg_ref[...].reshape(N_DEV * r, D))
        dx, dgain = _rms_bwd(dn, h_ref[...], r_ref[...], gain_ref[...])
        o_ref[...] = d + dx
        dg_ref[...] += dgain

    return _call(
        body, "ple_bwd", (S // tm,),
        [_rows(tm, D), _rows(tm, D), _rows(tm, D), _rows(tm, D), _rows(tm, 1), _whole(gain), _slab(sq_w, r, 2)]
        + [ANY] * len(after),
        [_rows(tm, D), _rows(tm, D), _rows(tm, D), pl.BlockSpec((1, D), lambda i: (0, 0))],
        [_sds((S, D), F32), _sds((S, D), BF16), _sds((S, D), BF16), _sds((1, D), F32)],
    )(dh, pre, pp, h, rstd, gain, sq_w, *after)


def _ffn_bwd(dh, jac, gu_w, dn_w, h, rstd, gain, after=()):
    S, D = h.shape
    _, _, c = jac.shape
    nb = N_DEV // 2
    tm = _tile(S, 256, SUBLANES)

    def body(dh_ref, jac_ref, w_ref, wd_ref, h_ref, r_ref, gain_ref, *rest):
        dgu_ref, o_ref, dgain_ref = rest[len(after):]

        @pl.when(pl.program_id(0) == 0)
        def _():
            dgain_ref[...] = jnp.zeros_like(dgain_ref)

        dh_v = dh_ref[...]
        dhb = dh_v.astype(BF16)
        dn = jnp.zeros((tm, D), F32)
        for d in range(nb):
            da = _dot_nt(dhb, _down_weight(wd_ref, d, c))
            dg = (da * jac_ref[d].astype(F32)).astype(BF16)
            du = (da * jac_ref[nb + d].astype(F32)).astype(BF16)
            dgu_ref[d] = dg
            dgu_ref[nb + d] = du
            dn = dn + _dot_nt(dg, w_ref[d]) + _dot_nt(du, w_ref[nb + d])
        dx, dgain = _rms_bwd(dn, h_ref[...], r_ref[...], gain_ref[...])
        o_ref[...] = dh_v + dx
        dgain_ref[...] += dgain

    act = pl.BlockSpec((N_DEV, tm, c), lambda i: (0, i, 0))
    return _call(
        body, "ffn_bwd", (S // tm,),
        [_rows(tm, D), act, _slab(gu_w, D, 0), _slab(dn_w, c // 2, 0), _rows(tm, D), _rows(tm, 1), _whole(gain)]
        + [ANY] * len(after),
        [act, _rows(tm, D), pl.BlockSpec((1, D), lambda i: (0, 0))],
        [_sds((N_DEV, S, c), BF16), _sds((S, D), F32), _sds((1, D), F32)],
    )(dh, jac, gu_w, dn_w, h, rstd, gain, *after)


def _merge_bwd(dh, z_gg, yconv, ymla, o, sq_w, c128_w, C, after):
    S, D = dh.shape
    r = sq_w.shape[1] // 3
    HV = N_DEV * r
    H = HV // VDIM
    tm = _tile(S, 512, SUBLANES)

    def head_rows():
        row = lax.broadcasted_iota(jnp.int32, (SUBLANES * H, HV), 0) >> (SUBLANES.bit_length() - 1)
        col = lax.broadcasted_iota(jnp.int32, (SUBLANES * H, HV), 1) >> (VDIM.bit_length() - 1)
        return jnp.where(row == col, 1.0, 0.0).astype(BF16)

    def body(dh_ref, gg_ref, yc_ref, ym_ref, o_ref, wmo_ref, wo_ref, wco_ref, *rest):
        dgg_ref, dby_ref, do_ref, dyc_ref, dym_ref, dl_ref = rest[len(after):]
        dm = _dot_nt(dh_ref[...].astype(BF16), wo_ref[...].reshape(HV, D))
        gg = gg_ref[...].astype(F32)
        sgc = _sig(gg[:, :D])
        sgm = _sig(gg[:, D:])
        dyc = (dm * sgc).astype(BF16)
        dym = (dm * sgm).astype(BF16)
        dyc_ref[...] = dyc
        dym_ref[...] = dym
        dgg_ref[:, :D] = (dm * yc_ref[...].astype(F32) * sgc * (1.0 - sgc)).astype(BF16)
        dgg_ref[:, D:] = (dm * ym_ref[...].astype(F32) * sgm * (1.0 - sgm)).astype(BF16)
        dby_ref[...] = _dot_nt(dyc, _cat_slots(wco_ref)).astype(BF16)
        do = _dot_nt(dym, wmo_ref[...].reshape(HV, D)).astype(BF16)
        do_ref[...] = do
        prod = do.astype(F32) * o_ref[...].astype(F32)
        hi = prod.astype(BF16)
        lo = (prod - hi.astype(F32)).astype(BF16)
        pick = head_rows()
        dl_ref[...] = _dot_nt(pick, hi) + _dot_nt(pick, lo)

    return _call(
        body, "merge_bwd", (S // tm,),
        [_rows(tm, D), _rows(tm, 2 * D), _rows(tm, D), _rows(tm, D), _rows(tm, HV), _slab(sq_w, r, 0),
         _slab(sq_w, r, 1), _slab(c128_w, C, 0)] + [ANY] * len(after),
        [_rows(tm, 2 * D), _rows(tm, C), _rows(tm, HV), _rows(tm, D), _rows(tm, D),
         pl.BlockSpec((SUBLANES * H, tm), lambda i: (0, i))],
        [_sds((S, 2 * D), BF16), _sds((S, C), BF16), _sds((S, HV), BF16), _sds((S, D), BF16), _sds((S, D), BF16),
         _sds((SUBLANES * H, S), F32)],
    )(dh, z_gg, yconv, ymla, o, sq_w, sq_w, c128_w, *after)


def _conv_bwd(z_bcv, conv_w, dby):
    _, S, C = z_bcv.shape

    def body(z_ref, w_ref, dby_ref, dz_ref, dw_ref):
        w = w_ref[...]
        c = z_ref[1].astype(F32)
        v = z_ref[2].astype(F32)
        d = dby_ref[...].astype(F32)
        zc = c * v
        z1, z2 = _conv_taps(zc)
        y = w[0:1] * z2 + w[1:2] * z1 + w[2:3] * zc
        dz_ref[0] = (d * y).astype(BF16)
        dy = d * z_ref[0].astype(F32)
        rows = lax.broadcasted_iota(jnp.int32, dy.shape, 0)
        dy1 = jnp.where(rows < S - 1, pltpu.roll(dy, S - 1, 0), 0.0)
        dy2 = jnp.where(rows < S - 2, pltpu.roll(dy, S - 2, 0), 0.0)
        dzc = w[2:3] * dy + w[1:2] * dy1 + w[0:1] * dy2
        dz_ref[1] = (dzc * v).astype(BF16)
        dz_ref[2] = (dzc * c).astype(BF16)
        dw_ref[0:1, :] = jnp.sum(dy * z2, axis=0, keepdims=True)
        dw_ref[1:2, :] = jnp.sum(dy * z1, axis=0, keepdims=True)
        dw_ref[2:3, :] = jnp.sum(dy * zc, axis=0, keepdims=True)

    three = pl.BlockSpec((3, S, LANES), lambda j: (0, 0, j))
    wspec = pl.BlockSpec((3, LANES), lambda j: (0, j))
    return _call(
        body, "conv_bwd", (C // LANES,),
        [three, wspec, pl.BlockSpec((S, LANES), lambda j: (0, j))],
        [three, wspec],
        [_sds((3, S, C), BF16), _sds((3, C), F32)],
    )(z_bcv, conv_w, dby)


def _attn_bwd(q, k, v, do, lse, delta, H):
    S = q.shape[0]
    t = _tile(S, 512, CHUNK)
    nk = S // t

    def body(q_ref, k_ref, v_ref, do_ref, lse_ref, dl_ref, dq_ref, dk_ref, dv_ref, dqt_ref):
        kj = pl.program_id(1)

        @pl.when(kj == 0)
        def _():
            dqt_ref[...] = jnp.zeros_like(dqt_ref)

        kv = k_ref[...]
        vv = v_ref[...]
        kt = kv.T

        def block(start, width, carry, masked):
            dk, dv = carry
            off = pl.multiple_of(start * t, t)
            qv = q_ref[pl.ds(off, width * t), :]
            dov = do_ref[pl.ds(off, width * t), :]
            s = _dot_nt(kv, qv)
            if masked:
                s = jnp.where(_chunk_mask(t, width * t, 0), s, -1e30)
            p = jnp.exp2(s - lse_ref[0, 0:1, pl.ds(off, width * t)])
            dp = _dot_nt(vv, dov)
            ds = (p * (dp - dl_ref[0, 0:1, pl.ds(off, width * t)]) * LN2).astype(BF16)
            dqt_ref[:, pl.ds(off, width * t)] += _dot(kt, ds)
            return dk + _dot(ds, qv), dv + _dot(p.astype(BF16), dov)

        init = (jnp.zeros((t, HEAD_SLOT), F32), jnp.zeros((t, VDIM), F32))
        wide = lax.div(nk - 1 - kj, ATTN_BWD_WIDTH)
        left = nk - 1 - kj - wide * ATTN_BWD_WIDTH
        carry = lax.switch(left, [functools.partial(block, kj, extra + 1, init, True)
                                  for extra in range(ATTN_BWD_WIDTH)])
        dk, dv = lax.fori_loop(
            0, wide, lambda j, c: block(kj + 1 + left + j * ATTN_BWD_WIDTH, ATTN_BWD_WIDTH, c, False), carry)
        dk_ref[...] = dk.astype(BF16)
        dv_ref[...] = dv.astype(BF16)

        @pl.when(kj == nk - 1)
        def _():
            dq_ref[...] = (dqt_ref[...] * SCORE_SCALE).T.astype(BF16)

    kspec = lambda w: pl.BlockSpec((t, w), lambda h, j: (j, h))
    qspec = lambda w: pl.BlockSpec((S, w), lambda h, j: (0, h))
    stat = pl.BlockSpec((1, SUBLANES, S), lambda h, j: (h, 0, 0))
    return _call(
        body, "attn_bwd", (H, nk),
        [qspec(HEAD_SLOT), kspec(HEAD_SLOT), kspec(VDIM), qspec(VDIM), stat, stat],
        [qspec(HEAD_SLOT), kspec(HEAD_SLOT), kspec(VDIM)],
        [_sds((S, H * HEAD_SLOT), BF16), _sds((S, H * HEAD_SLOT), BF16), _sds((S, H * VDIM), BF16)],
        [pltpu.VMEM((HEAD_SLOT, S), F32)],
    )(q, k, v, do, lse, delta)


def _mla_prep_bwd(dq, dk, dv, z_qkr, rq, rkv, gq, gkv, cs, c256_w):
    S = z_qkr.shape[0]
    QL, KVL = gq.shape[1], gkv.shape[1]
    H = N_DEV
    tm = _tile(S, 512, SUBLANES)
    half = ROPE // 2

    def body(dq_ref, dk_ref, dv_ref, z_ref, rq_ref, rkv_ref, gq_ref, gkv_ref, cs_ref, w_ref,
             dz_ref, dqp_ref, dkv_ref, dgq_ref, dgkv_ref):
        @pl.when(pl.program_id(0) == 0)
        def _():
            dgq_ref[...] = jnp.zeros_like(dgq_ref)
            dgkv_ref[...] = jnp.zeros_like(dgkv_ref)

        cs_t = cs_ref[...]
        dkr = jnp.zeros((tm, LANES), F32)
        dqn = jnp.zeros((tm, QL), F32)
        dkvn = jnp.zeros((tm, KVL), F32)
        for h in range(H):
            lo, mid, hi = h * HEAD_SLOT, h * HEAD_SLOT + LANES, (h + 1) * HEAD_SLOT
            dqp_ref[:, lo:mid] = dq_ref[:, lo:mid]
            dqp_ref[:, mid:hi] = _unrope(dq_ref[:, mid:hi].astype(F32), cs_t, half).astype(BF16)
            dkv_ref[:, lo:mid] = dk_ref[:, lo:mid]
            dkv_ref[:, mid:hi] = dv_ref[:, h * VDIM:(h + 1) * VDIM]
            dkr = dkr + dk_ref[:, mid:hi].astype(F32)
            dqn = dqn + _dot_nt(dqp_ref[:, lo:hi], w_ref[h, KVL:KVL + QL, :])
            dkvn = dkvn + _dot_nt(dkv_ref[:, lo:hi], w_ref[h, 0:KVL, :])
        z = z_ref[...]
        dqc, dgq = _rms_bwd(dqn, z[:, :QL], rq_ref[...], gq_ref[...])
        dkvc, dgkv = _rms_bwd(dkvn, z[:, QL:QL + KVL], rkv_ref[...], gkv_ref[...])
        dz_ref[:, :QL] = dqc.astype(BF16)
        dz_ref[:, QL:QL + KVL] = dkvc.astype(BF16)
        dz_ref[:, QL + KVL:] = _unrope(dkr, cs_t, half).astype(BF16)
        dgq_ref[...] += dgq
        dgkv_ref[...] += dgkv

    W = z_qkr.shape[1]
    return _call(
        body, "mla_prep_bwd", (S // tm,),
        [_rows(tm, H * HEAD_SLOT), _rows(tm, H * HEAD_SLOT), _rows(tm, H * VDIM), _rows(tm, W), _rows(tm, 1),
         _rows(tm, 1), _whole(gq), _whole(gkv), _rows(tm, 3 * LANES), _whole(c256_w)],
        [_rows(tm, W), _rows(tm, H * HEAD_SLOT), _rows(tm, H * HEAD_SLOT), _whole(gq), _whole(gkv)],
        [_sds((S, W), BF16), _sds((S, H * HEAD_SLOT), BF16), _sds((S, H * HEAD_SLOT), BF16),
         _sds((1, QL), F32), _sds((1, KVL), F32)],
    )(dq, dk, dv, z_qkr, rq, rkv, gq, gkv, cs, c256_w)


def _mix_in_bwd(d_bcv, dz_qkr, dgg, w_bcv, w_qkr, w_gg, h, rstd, gain, dh):
    S, D = h.shape
    C = d_bcv.shape[2]
    tm = _tile(S, 512, SUBLANES)

    def body(db_ref, dq_ref, dgg_ref, wb_ref, wq_ref, wg_ref, h_ref, r_ref, gain_ref, dh_ref, o_ref, dgain_ref):
        @pl.when(pl.program_id(0) == 0)
        def _():
            dgain_ref[...] = jnp.zeros_like(dgain_ref)

        dn = _dot_nt(dq_ref[...], wq_ref[...]) + _dot_nt(dgg_ref[...], wg_ref[...])
        for k in range(3):
            dn = dn + _dot_nt(db_ref[k], wb_ref[k])
        dx, dgain = _rms_bwd(dn, h_ref[...], r_ref[...], gain_ref[...])
        o_ref[...] = dh_ref[...] + dx
        dgain_ref[...] += dgain

    return _call(
        body, "mix_in_bwd", (S // tm,),
        [pl.BlockSpec((3, tm, C), lambda i: (0, i, 0)), _rows(tm, dz_qkr.shape[1]), _rows(tm, dgg.shape[1]),
         _whole(w_bcv), _whole(w_qkr), _whole(w_gg), _rows(tm, D), _rows(tm, 1), _whole(gain), _rows(tm, D)],
        [_rows(tm, D), pl.BlockSpec((1, D), lambda i: (0, 0))],
        [_sds((S, D), F32), _sds((1, D), F32)],
    )(d_bcv, dz_qkr, dgg, w_bcv, w_qkr, w_gg, h, rstd, gain, dh)


def _rope_tables(positions):
    half = ROPE // 2
    inv_freq = ROPE_THETA ** (-jnp.arange(0, ROPE, 2, dtype=F32) / ROPE)
    ang = positions.astype(F32)[:, None] * inv_freq
    cos, sin = jnp.cos(ang), jnp.sin(ang)
    z = jnp.zeros_like(cos)
    pad = jnp.zeros((positions.shape[0], LANES - 2 * half), F32)
    return jnp.concatenate([cos, cos, pad, -sin, z, pad, z, sin, pad], axis=1)


def _grad_rows(w):
    return dict(gu=2 * w["gu1"].shape[1], dn=2 * w["dn1"].shape[1], sq=w["sq"].shape[1], win=w["win"].shape[1],
                c128=w["c128"].shape[1], c256=w["c256"].shape[1])


def _layer_fwd(h0, p_l, cs, w, sm, late):
    C = sm["conv_w"].shape[1]
    QL, KVL = sm["q_norm"].shape[1], sm["kv_norm"].shape[1]
    h1, jac1, at1, n1, r1 = _ffn_fwd(h0, sm["ffn1_norm"], w["gu1"], w["dn1"])
    if late is not None:
        w.update(late(h1))
    w_bcv, w_qkr, w_gg = _win_split(w["win"], C, QL, KVL)
    z_bcv, z_qkr, z_gg, un, rm = _mix_in(h1, sm["mix_norm"], w_bcv, w_qkr, w_gg)
    by = _conv_fwd(z_bcv, sm["conv_w"])
    q, k, v, qn, kvn, rq, rkv = _mla_prep(z_qkr, sm["q_norm"], sm["kv_norm"], cs, w["c256"])
    o, ot, lse = _attn_fwd(q, k, v, N_DEV)
    h2, merged, yconv, ymla = _merge_wo(o, by, z_gg, h1, w["sq"], w["c128"])
    h3, jac2, at2, n2, r2 = _ffn_fwd(h2, sm["ffn2_norm"], w["gu2"], w["dn2"])
    h4, pre, pp, pn, rp = _ple_fwd(h3, sm["ple_norm"], p_l, w["sq"], w["c128"], C)
    saved = dict(h0=h0, jac1=jac1, at1=at1, n1=n1, r1=r1, h1=h1, w_bcv=w_bcv, w_qkr=w_qkr, w_gg=w_gg, z_bcv=z_bcv,
                 z_qkr=z_qkr, z_gg=z_gg, un=un, rm=rm, by=by, q=q, k=k, v=v, qn=qn, kvn=kvn, rq=rq, rkv=rkv, o=o, ot=ot,
                 lse=lse, h2=h2, merged=merged, yconv=yconv, ymla=ymla, jac2=jac2, at2=at2, n2=n2, r2=r2, h3=h3,
                 pre=pre, pp=pp, pn=pn, rp=rp, p=p_l)
    return h4, saved


def _layer_bwd_late(dh4, s, w, sm, after):
    D = dh4.shape[1]
    C = sm["conv_w"].shape[1]
    P = s["p"].shape[1]
    rows = _grad_rows(w)
    small = {}
    dh3, dpre, dpp, small["ple_norm"] = _ple_bwd(dh4, s["pre"], s["pp"], s["h3"], s["rp"], sm["ple_norm"], w["sq"],
                                                 after)
    g_sq = _tn_square(s["pn"], dpre, None, rows["sq"], 2)
    g_c128 = _tn_cols(s["p"], dpp, None, rows["c128"], C // P)

    dgu2, dh2, small["ffn2_norm"] = _ffn_bwd(dh3, s["jac2"], w["gu2"], w["dn2"], s["h2"], s["r2"], sm["ffn2_norm"])
    g_dn = _tn_down(s["at2"], dh3, None, rows["dn"], 1)
    g_gu = _tn_slots(s["n2"], dgu2, None, rows["gu"], D)
    return dh2, dict(gu=g_gu, dn=g_dn, sq=g_sq, c128=g_c128), small


def _layer_bwd_mixer(dh2, part, small, s, cs, w, sm, after):
    C = sm["conv_w"].shape[1]
    rows = _grad_rows(w)
    g_gu, g_dn, g_sq, g_c128 = part["gu"], part["dn"], part["sq"], part["c128"]

    dgg, dby, do, dyc, dym, delta = _merge_bwd(dh2, s["z_gg"], s["yconv"], s["ymla"], s["o"], w["sq"], w["c128"], C,
                                               after)
    g_sq = _tn_square(s["merged"], dh2, g_sq, rows["sq"], 1)
    g_sq = _tn_square(s["ot"], dym, g_sq, rows["sq"], 0)
    g_c128 = _tn_cols(s["by"], dyc, g_c128, rows["c128"], 0)
    d_bcv, small["conv_w"] = _conv_bwd(s["z_bcv"], sm["conv_w"], dby)
    delta = delta.reshape(N_DEV, SUBLANES, delta.shape[1])
    dq, dk, dv = _attn_bwd(s["q"], s["k"], s["v"], do, s["lse"], delta, N_DEV)
    dz_qkr, dqp, dkv, small["q_norm"], small["kv_norm"] = _mla_prep_bwd(
        dq, dk, dv, s["z_qkr"], s["rq"], s["rkv"], sm["q_norm"], sm["kv_norm"], cs, w["c256"])
    g_c256 = _tn_heads(s["qn"], s["kvn"], dqp, dkv)
    un = s["un"]
    g_win = _win_merge(_tn_plain(un, d_bcv), _tn_plain(un, dz_qkr[None])[0], _tn_plain(un, dgg[None])[0],
                       w["win"].shape[2])
    dh1, small["mix_norm"] = _mix_in_bwd(d_bcv, dz_qkr, dgg, s["w_bcv"], s["w_qkr"], s["w_gg"], s["h1"], s["rm"],
                                         sm["mix_norm"], dh2)
    return dh1, dict(gu=g_gu, dn=g_dn, sq=g_sq, win=g_win, c128=g_c128, c256=g_c256), small


def _layer_bwd_first(dh1, part, small, s, w, sm, after):
    rows = _grad_rows(w)
    dgu1, dh0, small["ffn1_norm"] = _ffn_bwd(dh1, s["jac1"], w["gu1"], w["dn1"], s["h0"], s["r1"], sm["ffn1_norm"],
                                             after)
    g_dn = _tn_down(s["at1"], dh1, part["dn"], rows["dn"], 0)
    g_gu = _tn_slots(s["n1"], dgu1, part["gu"], rows["gu"], 0)
    return dh0, dict(part, gu=g_gu, dn=g_dn), small


def _mesh_pos():
    return lax.axis_index("x"), lax.axis_index("y"), lax.axis_index("c")


def _other_chips(x, y):
    return [(1 - x, y), (x, 1 - y), (1 - x, 1 - y)]


def _pack(arrs, flipped, width):
    L = arrs[0].shape[0]
    shapes = [a.shape[:0:-1] if f else a.shape[1:] for a, f in zip(arrs, flipped)]
    R = sum(r for r, _ in shapes)

    def body(*refs):
        o_ref = refs[-1]
        off = 0
        for a_ref, f, (r, c) in zip(refs[:-1], flipped, shapes):
            a = a_ref[0].T if f else a_ref[0]
            o_ref[0, off:off + r, 0:c] = a.astype(BF16)
            if c < width:
                o_ref[0, off:off + r, c:width] = jnp.zeros((r, width - c), BF16)
            off += r

    return _call(
        body, "pack", (L,),
        [pl.BlockSpec((1,) + a.shape[1:], lambda l: (l, 0, 0)) for a in arrs],
        pl.BlockSpec((1, R, width), lambda l: (l, 0, 0)),
        _sds((L, R, width), BF16),
    )(*arrs)


def _handshake(peers):
    barrier = pltpu.get_barrier_semaphore()
    for peer in peers:
        pl.semaphore_signal(barrier, inc=1, device_id=peer, device_id_type=MESH)
    pl.semaphore_wait(barrier, len(peers))


def _sequencer_call(body, name, out_types, sems, collective_id, operands):
    return pl.kernel(
        body, name=name, out_type=out_types,
        mesh=plsc.ScalarSubcoreMesh(axis_name="seq", num_cores=1),
        scratch_types=tuple(pltpu.SemaphoreType.DMA((k,)) for k in sems),
        compiler_params=pltpu.CompilerParams(collective_id=collective_id),
    )(*operands)


def _all_gather(packs, l, after, collective_id):
    n = len(packs)

    def body(*refs):
        ins, outs = refs[:n], refs[n + len(after):2 * n + len(after)]
        send_sems, recv_sems, local_sems = refs[2 * n + len(after):]
        x, y, c = _mesh_pos()
        me, sibling = (x, y, c), (x, y, 1 - c)
        chips = _other_chips(x, y)
        _handshake([sibling] + [(*chip, c) for chip in chips])

        def copy(q, k, block, to, src=None):
            slot = outs[q].at[4 * block[0] + 2 * block[1] + block[2]]
            return pltpu.make_async_remote_copy(
                src_ref=slot if src is None else src, dst_ref=slot,
                send_sem=send_sems.at[7 * q + k], recv_sem=recv_sems.at[7 * q + k], device_id=to, device_id_type=MESH)

        started = []
        for q in range(n):
            src = ins[q].at[l]
            mine = pltpu.make_async_copy(src, outs[q].at[4 * x + 2 * y + c], local_sems.at[q])
            mine.start()
            started.append(mine)
        sends = []
        for q in range(n):
            src = ins[q].at[l]
            sends.append(copy(q, 0, me, sibling, src=src))
            sends += [copy(q, 1 + j, me, (*chip, c), src=src) for j, chip in enumerate(chips)]
        for cp in sends:
            cp.start()
        for q in range(n):
            for j, chip in enumerate(chips):
                copy(q, 1 + j, (*chip, c), me).wait_recv()
                fwd = copy(q, 4 + j, (*chip, c), sibling)
                fwd.start()
                sends.append(fwd)
        for q in range(n):
            copy(q, 0, sibling, me).wait_recv()
            for j, chip in enumerate(chips):
                copy(q, 4 + j, (*chip, 1 - c), me).wait_recv()
        for cp in sends:
            cp.wait_send()
        for mine in started:
            mine.wait()

    return _sequencer_call(
        body, f"all_gather_{collective_id}", [_sds((N_DEV,) + p.shape[1:], p.dtype) for p in packs], (7 * n, 7 * n, n),
        collective_id, list(packs) + list(after))


def _rs_d2d(gs, l, collective_id):
    n = len(gs)

    def body(*refs):
        ins, outs = refs[:n], refs[n:2 * n]
        send_sems, recv_sems = refs[2 * n:]
        x, y, c = _mesh_pos()
        _handshake([(x, y, 1 - c)])
        copies = []
        for q in range(n):
            for j in range(4):
                copies.append(pltpu.make_async_remote_copy(
                    src_ref=ins[q].at[2 * j + (1 - c)], dst_ref=outs[q].at[j], send_sem=send_sems.at[4 * q + j],
                    recv_sem=recv_sems.at[4 * q + j], device_id=(x, y, 1 - c), device_id_type=MESH))
        for cp in copies:
            cp.start()
        for cp in copies:
            cp.wait()

    return _sequencer_call(
        body, f"rs_d2d_{l}", [_sds((4,) + g.shape[1:], g.dtype) for g in gs], (4 * n, 4 * n), collective_id, gs)


def _rs_add_chip(gs, as_, after):
    n = len(gs)
    steps = 4
    tiles = [g.shape[1] // steps for g in gs]

    def chip(k):
        x, y, _ = _mesh_pos()
        return ([(x, y)] + _other_chips(x, y))[k]

    def body(*refs):
        g_refs, a_refs = refs[:4 * n], refs[4 * n:8 * n]
        own_refs, t_refs = refs[8 * n + len(after):9 * n + len(after)], refs[9 * n + len(after):]
        for q in range(n):
            g, a = g_refs[4 * q:4 * q + 4], a_refs[4 * q:4 * q + 4]
            own_refs[q][...] = g[0][0].astype(F32) + a[0][0].astype(F32)
            for k in range(1, 4):
                t_refs[q][k - 1] = (g[k][0].astype(F32) + a[k][0].astype(F32)).astype(BF16)

    def gspec(q, k):
        def index(i):
            px, py = chip(k)
            return 4 * px + 2 * py + lax.axis_index("c"), i, 0
        return pl.BlockSpec((1, tiles[q], gs[q].shape[2]), index)

    def aspec(q, k):
        def index(i):
            px, py = chip(k)
            return 2 * px + py, i, 0
        return pl.BlockSpec((1, tiles[q], gs[q].shape[2]), index)

    in_specs = [gspec(q, k) for q in range(n) for k in range(4)] + [aspec(q, k) for q in range(n) for k in range(4)]
    operands = [g for g in gs for _ in range(4)] + [a for a in as_ for _ in range(4)]
    out_specs = [pl.BlockSpec((tiles[q], gs[q].shape[2]), lambda i: (i, 0)) for q in range(n)]
    out_specs += [pl.BlockSpec((3, tiles[q], gs[q].shape[2]), lambda i: (0, i, 0)) for q in range(n)]
    out_shape = [_sds(g.shape[1:], F32) for g in gs] + [_sds((3,) + g.shape[1:], BF16) for g in gs]
    res = _call(body, "rs_add_chip", (steps,), in_specs + [ANY] * len(after), out_specs, out_shape)(*operands, *after)
    return res[:n], res[n:]


def _rs_ici(ts, l, collective_id):
    n = len(ts)

    def body(*refs):
        ins, outs = refs[:n], refs[n:2 * n]
        send_sems, recv_sems = refs[2 * n:]
        x, y, c = _mesh_pos()
        chips = _other_chips(x, y)
        _handshake([(*chip, c) for chip in chips])
        copies = []
        for q in range(n):
            for k, chip in enumerate(chips):
                copies.append(pltpu.make_async_remote_copy(
                    src_ref=ins[q].at[k], dst_ref=outs[q].at[k], send_sem=send_sems.at[3 * q + k],
                    recv_sem=recv_sems.at[3 * q + k], device_id=(*chip, c), device_id_type=MESH))
        for cp in copies:
            cp.start()
        for cp in copies:
            cp.wait()

    return _sequencer_call(
        body, f"rs_ici_{l}", [_sds(t.shape, t.dtype) for t in ts], (3 * n, 3 * n), collective_id, ts)


def _all_reduce_small(v):
    n, W = v.shape

    def body(v_ref, out_ref, slots, send_sems, recv_sems):
        x, y, c = _mesh_pos()
        me = 4 * x + 2 * y + c
        slots[me] = v_ref[...]
        copies = []
        for k in range(1, N_DEV):
            kx, ky, kc = (k >> 2) & 1, (k >> 1) & 1, k & 1
            peer = (1 - x if kx else x, 1 - y if ky else y, 1 - c if kc else c)
            copies.append(pltpu.make_async_remote_copy(
                src_ref=v_ref, dst_ref=slots.at[me], send_sem=send_sems.at[k - 1], recv_sem=recv_sems.at[k - 1],
                device_id=peer, device_id_type=MESH))
        for cp in copies:
            cp.start()
        for cp in copies:
            cp.wait()
        acc = slots[0]
        for d in range(1, N_DEV):
            acc = acc + slots[d]
        out_ref[...] = acc

    vm = pl.BlockSpec(memory_space=pltpu.VMEM)
    return pl.pallas_call(
        body, name="all_reduce_small",
        out_shape=_sds((n, W), F32),
        in_specs=[vm], out_specs=vm,
        scratch_shapes=[pltpu.VMEM((N_DEV, n, W), F32), pltpu.SemaphoreType.DMA((7,)), pltpu.SemaphoreType.DMA((7,))],
    )(v)


def _adamw_math(w, g, m, v):
    m2 = ADAM_B1 * m + (1.0 - ADAM_B1) * g
    v2 = ADAM_B2 * v + (1.0 - ADAM_B2) * (g * g)
    m_hat = m2 / (1.0 - ADAM_B1 ** ADAM_STEP)
    v_hat = v2 / (1.0 - ADAM_B2 ** ADAM_STEP)
    return -ADAM_LR * (m_hat / (jnp.sqrt(v_hat) + ADAM_EPS) + ADAM_WD * w), m2, v2


def _adamw(w, g, m, v):
    L, r, c = w.shape
    tr = _tile(r, max(SUBLANES, (256 * 1024 // c) // SUBLANES * SUBLANES), SUBLANES)

    def body(w_ref, g_ref, m_ref, v_ref, d_ref, nm_ref, nv_ref):
        d_ref[...], nm_ref[...], nv_ref[...] = _adamw_math(w_ref[...], g_ref[...], m_ref[...], v_ref[...])

    spec = pl.BlockSpec((1, tr, c), lambda l, i: (l, i, 0))
    return _call(body, "adamw", (L, r // tr), [spec] * 4, [spec] * 3, [_sds((L, r, c), F32)] * 3)(w, g, m, v)


def _adamw_reduced(w, m, v, flipped, own, b, row_off, tr, l, prev, after):
    L = w.shape[0]
    c, r = w.shape[1:] if flipped else w.shape[:0:-1]
    W = own.shape[1]
    ob = row_off // tr
    extra = list(prev or ()) + list(after)

    def body(w_ref, m_ref, v_ref, own_ref, b_ref, *rest):
        g_ref, d_ref, nm_ref, nv_ref = rest[len(extra):]
        g = ((own_ref[...] + b_ref[0].astype(F32)) + b_ref[1].astype(F32)) + b_ref[2].astype(F32)
        g = g[:, :c].T if flipped else g[:, :c]
        g_ref[0] = g
        d_ref[0], nm_ref[0], nv_ref[0] = _adamw_math(w_ref[0], g, m_ref[0], v_ref[0])

    spec = pl.BlockSpec((1, c, tr), lambda i: (l, 0, i)) if flipped else pl.BlockSpec((1, tr, c), lambda i: (l, i, 0))
    return _call(
        body, "adamw_reduced", (r // tr,),
        [spec] * 3 + [pl.BlockSpec((tr, W), lambda i: (ob + i, 0)), pl.BlockSpec((3, tr, W), lambda i: (0, ob + i, 0))]
        + [ANY] * len(extra),
        [spec] * 4, [_sds(w.shape, F32)] * 4,
        aliases={5 + k: k for k in range(4)} if prev else None,
    )(w, m, v, own, b, *extra)


_MEMBERS = dict(gu=("ffn1_w_gu", "ffn2_w_gu"), dn=("ffn1_w_down", "ffn2_w_down"),
                sq=("w_mla_out", "w_o", "w_ple_gate"), win=("w_in",), c128=("w_conv_out", "w_ple_proj"),
                c256=("w_ukv", "w_uq"))
_GATHER_MEMBERS = dict(_MEMBERS, gu1=("ffn1_w_gu",), gu2=("ffn2_w_gu",), dn1=("ffn1_w_down",), dn2=("ffn2_w_down",))
GATHER_STAGES = (("gu1", "dn1"), ("win", "c256", "c128", "sq"), ("gu2", "dn2"))
_FLIPPED = ("ffn1_w_gu", "ffn2_w_gu", "w_in", "w_uq")
_SMALL = ("ffn1_norm", "mix_norm", "q_norm", "kv_norm", "ffn2_norm", "ple_norm")
_ORDER = ("ffn1_norm", "ffn1_w_gu", "ffn1_w_down", "mix_norm", "w_in", "conv_w", "w_conv_out", "q_norm", "kv_norm",
          "w_uq", "w_ukv", "w_mla_out", "w_o", "ffn2_norm", "ffn2_w_gu", "ffn2_w_down", "ple_norm", "w_ple_gate",
          "w_ple_proj", "final_norm")


def _class_width(wts, cls):
    return HEAD_SLOT if cls == "c256" else wts[_GATHER_MEMBERS[cls][0]].shape[2]


def _pack_rows(vecs, width):
    flat = jnp.concatenate([a.reshape(-1) for a in vecs])
    n = flat.shape[0]
    rows = -(-n // width)
    rows = -(-rows // SUBLANES) * SUBLANES
    flat = jnp.pad(flat, (0, rows * width - n))
    offs, o = [], 0
    for a in vecs:
        offs.append(o)
        o += a.size
    return flat.reshape(rows, width), offs


def _unpack_rows(packed, vecs, offs):
    flat = packed.reshape(-1)
    return [flat[o:o + a.size].reshape(a.shape) for a, o in zip(vecs, offs)]


def _train(x, p, positions, target, gathered, packs, small_w, final_norm, update):
    cs = _rope_tables(positions)
    L = len(small_w)
    h = x
    saved = []
    def gather(l, names, after, collective_id):
        got = _all_gather([packs[n] for n in names], l, after, collective_id)
        return dict(zip(names, got))

    late = None
    if packs is not None:
        first, mixer, second = GATHER_STAGES
        w0 = gather(0, first, [], 0)
        w0.update(gather(0, mixer, [w0[first[0]]], 1))
        gathered = [w0]
        late = lambda h1: gather(0, second, [h1], 2)
    everything = sum(GATHER_STAGES, ())
    for l in range(L):
        h, s = _layer_fwd(h, p[l], cs, gathered[l], small_w[l], late)
        late = None
        saved.append(s)
        if packs is not None and l + 1 < L:
            gathered.append(gather(l + 1, everything, [s["by"]], 2 + l + 1))
    dh, loss, d_final = _final_loss(h, final_norm, target)
    grads, smalls = [None] * L, [None] * L
    exchanged = None
    landing = None

    def second_stage(after):
        l, gs, as_ = exchanged
        owns, ts = _rs_add_chip(gs, as_, [after])
        return l, owns, _rs_ici(ts, l, 2 * L + 2 + l)

    for l in reversed(range(L)):
        dh, part, small = _layer_bwd_late(dh, saved[l], gathered[l], small_w[l], [])
        pin = []
        if exchanged is not None:
            landing = second_stage(dh)
            pin = [landing[1][0]]
        dh, part, small = _layer_bwd_mixer(dh, part, small, saved[l], cs, gathered[l], small_w[l], pin)
        pin = [update(*landing)] if exchanged is not None else []
        dh, g, smalls[l] = _layer_bwd_first(dh, part, small, saved[l], gathered[l], small_w[l], pin)
        if update is not None:
            gs = [g[cls] for cls in CLASSES]
            exchanged = (l, gs, _rs_d2d(gs, l, L + 2 + l))
        else:
            grads[l] = g
    if update is not None:
        update(*second_stage(dh))
    return loss[0, 0], dh, grads, smalls, d_final


def kernel(x, p, positions, ffn1_norm, ffn1_w_gu, ffn1_w_down, mix_norm, w_in, conv_w, w_conv_out, q_norm, kv_norm, w_uq, w_ukv, w_mla_out, w_o, ffn2_norm, ffn2_w_gu, ffn2_w_down, ple_norm, w_ple_gate, w_ple_proj, final_norm, loss_target, m_ffn1_norm, m_ffn1_w_gu, m_ffn1_w_down, m_mix_norm, m_w_in, m_conv_w, m_w_conv_out, m_q_norm, m_kv_norm, m_w_uq, m_w_ukv, m_w_mla_out, m_w_o, m_ffn2_norm, m_ffn2_w_gu, m_ffn2_w_down, m_ple_norm, m_w_ple_gate, m_w_ple_proj, m_final_norm, v_ffn1_norm, v_ffn1_w_gu, v_ffn1_w_down, v_mix_norm, v_w_in, v_conv_w, v_w_conv_out, v_q_norm, v_kv_norm, v_w_uq, v_w_ukv, v_w_mla_out, v_w_o, v_ffn2_norm, v_ffn2_w_gu, v_ffn2_w_down, v_ple_norm, v_w_ple_gate, v_w_ple_proj, v_final_norm):
    args = dict(locals())
    wts = {n: args[n] for n in _ORDER}
    L = w_in.shape[0]
    dev = 4 * lax.axis_index("x") + 2 * lax.axis_index("y") + lax.axis_index("c")

    view = lambda n, a: jnp.swapaxes(a, 1, 2) if n in _FLIPPED else a
    packs = {cls: _pack([view(n, wts[n]) for n in _GATHER_MEMBERS[cls]], [n in _FLIPPED for n in _GATHER_MEMBERS[cls]],
                        _class_width(wts, cls))
             for stage in GATHER_STAGES for cls in stage}
    cw = conv_w.shape[2]
    conv_full = lax.dynamic_update_slice(jnp.zeros((L, 3, N_DEV * cw), F32), conv_w, (0, 0, dev * cw))
    conv_packed, conv_offs = _pack_rows([conv_full], FLAT_COLS)
    conv_full = _unpack_rows(_all_reduce_small(conv_packed), [conv_full], conv_offs)[0]
    small_w = [dict({n: wts[n][l][None, :] for n in _SMALL}, conv_w=conv_full[l]) for l in range(L)]

    done = {}

    def update(l, owns, bs):
        for q, cls in enumerate(CLASSES):
            off = 0
            rows = [wts[n].shape[1] for n in _MEMBERS[cls]]
            tr = _tile(math.gcd(*rows), 256, BF16_ROWS)
            for n, r in zip(_MEMBERS[cls], rows):
                done[n] = _adamw_reduced(view(n, wts[n]), view(n, args["m_" + n]), view(n, args["v_" + n]),
                                         n in _FLIPPED, owns[q], bs[q], off, tr, l, done.get(n), [])
                off += r
        return done[_MEMBERS[CLASSES[-1]][-1]][0]

    loss_dev, grad_x, _, smalls, d_final = _train(x[0], p[:, 0], positions[0], loss_target[0], None, packs, small_w,
                                                  final_norm[None, :], update)

    small = [jnp.stack([smalls[l][n][0] for l in range(L)]) for n in _SMALL]
    small += [jnp.stack([smalls[l]["conv_w"] for l in range(L)]), d_final[0], loss_dev[None]]
    packed, offs = _pack_rows(small, FLAT_COLS)
    small = _unpack_rows(_all_reduce_small(packed), small, offs)
    grad = dict(zip(_SMALL, small))
    grad["conv_w"] = lax.dynamic_slice(small[len(_SMALL)], (0, 0, dev * cw), (L, 3, cw))
    grad["final_norm"] = small[-2]
    loss = small[-1][0]

    deltas, new_m, new_v = {}, {}, {}
    for n, outs in done.items():
        grad[n], deltas[n], new_m[n], new_v[n] = (view(n, a) for a in outs)
    for n in _SMALL + ("conv_w", "final_norm"):
        w3 = wts[n].reshape((1,) * (3 - wts[n].ndim) + wts[n].shape)
        d, nm, nv = _adamw(w3, grad[n].reshape(w3.shape), args["m_" + n].reshape(w3.shape),
                           args["v_" + n].reshape(w3.shape))
        deltas[n], new_m[n], new_v[n] = (a.reshape(wts[n].shape) for a in (d, nm, nv))
    return (loss, grad_x[None], *[grad[n] for n in _ORDER], *[deltas[n] for n in _ORDER],
            *[new_m[n] for n in _ORDER], *[new_v[n] for n in _ORDER])
```

```python
import functools
import math

import jax
import jax.numpy as jnp
from jax import lax
from jax.experimental import pallas as pl
from jax.experimental.pallas import tpu as pltpu
from jax.experimental.pallas import tpu_sc as plsc

F32 = jnp.float32
BF16 = jnp.bfloat16

CHUNK = 64
NOPE = 128
ROPE = 64
VDIM = 128
ROPE_THETA = 10000.0
EPS = 1e-6
ATTN_SCALE = (NOPE + ROPE) ** -0.5
SCORE_SCALE = ATTN_SCALE * math.log2(math.e)
LN2 = math.log(2.0)
ADAM_LR = 0.001
ADAM_B1 = 0.9
ADAM_B2 = 0.999
ADAM_EPS = 1e-08
ADAM_WD = 0.01
ADAM_STEP = 10

LANES = 128
SUBLANES = 8
BF16_ROWS = 16
V7X_VMEM_BYTES = 64 * 1024 * 1024
VMEM_LIMIT = V7X_VMEM_BYTES * 7 // 8
HEAD_SLOT = 2 * LANES
N_DEV = 8
ATTN_FWD_WIDTH = 8
ATTN_BWD_WIDTH = 4
FLAT_COLS = 1024
CLASSES = ("gu", "dn", "sq", "win", "c128", "c256")

NT = (((1,), (1,)), ((), ()))
MESH = pl.DeviceIdType.MESH
ANY = pl.BlockSpec(memory_space=pl.ANY)


def _dot(a, b):
    return jnp.dot(a, b, preferred_element_type=F32)


def _dot_nt(a, b):
    return lax.dot_general(a, b, NT, preferred_element_type=F32)


def _sig(x):
    return 1.0 / (1.0 + jnp.exp(-x))


def _tile(n, pref, unit):
    if n <= pref:
        return n
    t = (pref // unit) * unit
    while t >= unit:
        if n % t == 0:
            return t
        t -= unit
    return n


def _call(body, name, grid, in_specs, out_specs, out_shape, scratch=(), aliases=None):
    return pl.pallas_call(
        body,
        name=name,
        grid=grid,
        in_specs=in_specs,
        out_specs=out_specs,
        out_shape=out_shape,
        scratch_shapes=list(scratch),
        input_output_aliases=aliases or {},
        compiler_params=pltpu.CompilerParams(
            dimension_semantics=("arbitrary",) * len(grid), vmem_limit_bytes=VMEM_LIMIT
        ),
    )


def _sds(shape, dtype):
    return jax.ShapeDtypeStruct(shape, dtype)


def _rms_fwd(x, gain):
    rstd = lax.rsqrt(jnp.mean(x * x, axis=-1, keepdims=True) + EPS)
    return x * rstd * gain, rstd


def _rms_bwd(dn, x, rstd, gain):
    xhat = x * rstd
    dgy = dn * gain
    dx = rstd * (dgy - xhat * jnp.mean(dgy * xhat, axis=-1, keepdims=True))
    return dx, jnp.sum(dn * xhat, axis=0, keepdims=True)


def _rows(tm, w):
    return pl.BlockSpec((tm, w), lambda i: (i, 0))


def _whole(a):
    nd = a.ndim
    return pl.BlockSpec(a.shape, lambda i: (0,) * nd, pipeline_mode=pl.Buffered(1))


def _slab(buf, rows, index):
    return pl.BlockSpec((N_DEV, rows, buf.shape[2]), lambda i: (0, index, 0), pipeline_mode=pl.Buffered(1))


def _cat_slots(w):
    return jnp.concatenate([w[d] for d in range(N_DEV)], axis=1)


def _down_weight(w_ref, d, c):
    return w_ref[2 * d:2 * d + 2].reshape(c, w_ref.shape[2])


def _ffn_fwd(h, gain, gu_w, dn_w):
    S, D = h.shape
    c = gu_w.shape[2]
    tm = _tile(S, 512, SUBLANES)
    nb = N_DEV // 2

    def body(h_ref, gain_ref, w_ref, wd_ref, o_ref, jac_ref, at_ref, n_ref, r_ref):
        x = h_ref[...]
        n32, rstd = _rms_fwd(x, gain_ref[...])
        n = n32.astype(BF16)
        n_ref[...] = n.T
        r_ref[...] = rstd
        acc = jnp.zeros((tm, D), F32)
        for d in range(nb):
            g = _dot(n, w_ref[d])
            u = _dot(n, w_ref[nb + d])
            sg = _sig(g)
            silu = g * sg
            a = (silu * u).astype(BF16)
            at_ref[d] = a.T
            jac_ref[d] = (0.5 * u * (sg + silu * (1.0 - sg))).astype(BF16)
            jac_ref[nb + d] = (0.5 * silu).astype(BF16)
            acc = acc + _dot(a, _down_weight(wd_ref, d, c))
        o_ref[...] = x + 0.5 * acc

    return _call(
        body, "ffn_fwd", (S // tm,),
        [_rows(tm, D), _whole(gain), _slab(gu_w, D, 0), _slab(dn_w, c // 2, 0)],
        [_rows(tm, D), pl.BlockSpec((N_DEV, tm, c), lambda i: (0, i, 0)),
         pl.BlockSpec((nb, c, tm), lambda i: (0, 0, i)), pl.BlockSpec((D, tm), lambda i: (0, i)), _rows(tm, 1)],
        [_sds((S, D), F32), _sds((N_DEV, S, c), BF16), _sds((nb, c, S), BF16), _sds((D, S), BF16),
         _sds((S, 1), F32)],
    )(h, gain, gu_w, dn_w)


def _win_segments(C, QL, KVL, D):
    o1, o2 = 3 * C, 3 * C + QL + KVL + ROPE
    return [("bcv", k, k * C, (k + 1) * C) for k in range(3)] + [("qkr", None, o1, o2), ("gg", None, o2, o2 + 2 * D)]


def _win_pieces(segments, cw):
    out = []
    for tgt, lead, a, b in segments:
        for d in range(N_DEV):
            lo, hi = max(a, d * cw), min(b, (d + 1) * cw)
            if lo < hi:
                out.append((tgt, lead, d, (lo - d * cw, hi - d * cw), (lo - a, hi - a)))
    return out


def _win_split(win_w, C, QL, KVL):
    _, D, cw = win_w.shape
    WQ = QL + KVL + LANES
    pieces = _win_pieces(_win_segments(C, QL, KVL, D), cw)
    tr = _tile(D, 256, BF16_ROWS)

    def body(w_ref, bcv_ref, qkr_ref, gg_ref):
        tgt = dict(bcv=bcv_ref, qkr=qkr_ref, gg=gg_ref)
        qkr_ref[:, QL + KVL + ROPE:] = jnp.zeros((tr, LANES - ROPE), BF16)
        for name, lead, d, (s0, s1), (t0, t1) in pieces:
            v = w_ref[d, :, s0:s1]
            if lead is None:
                tgt[name][:, t0:t1] = v
            else:
                tgt[name][lead, :, t0:t1] = v

    return _call(
        body, "win_split", (D // tr,),
        [pl.BlockSpec((N_DEV, tr, cw), lambda i: (0, i, 0))],
        [pl.BlockSpec((3, tr, C), lambda i: (0, i, 0)), _rows(tr, WQ), _rows(tr, 2 * D)],
        [_sds((3, D, C), BF16), _sds((D, WQ), BF16), _sds((D, 2 * D), BF16)],
    )(win_w)


def _win_merge(d_bcv, d_qkr, d_gg, cw):
    _, D, C = d_bcv.shape
    WQ = d_qkr.shape[1]
    QL_KVL = WQ - LANES
    o1 = 3 * C
    segments = [("bcv", k, k * C, (k + 1) * C) for k in range(3)]
    segments += [("qkr", None, o1, o1 + QL_KVL + ROPE), ("gg", None, o1 + QL_KVL + ROPE, o1 + QL_KVL + ROPE + 2 * D)]
    pieces = _win_pieces(segments, cw)
    tr = _tile(D, 256, BF16_ROWS)

    def body(bcv_ref, qkr_ref, gg_ref, o_ref):
        src = dict(bcv=bcv_ref, qkr=qkr_ref, gg=gg_ref)
        for name, lead, d, (s0, s1), (t0, t1) in pieces:
            v = src[name][:, t0:t1] if lead is None else src[name][lead, :, t0:t1]
            o_ref[d, :, s0:s1] = v.astype(BF16)

    return _call(
        body, "win_merge", (D // tr,),
        [pl.BlockSpec((3, tr, C), lambda i: (0, i, 0)), _rows(tr, WQ), _rows(tr, 2 * D)],
        pl.BlockSpec((N_DEV, tr, cw), lambda i: (0, i, 0)),
        _sds((N_DEV, D, cw), BF16),
    )(d_bcv, d_qkr, d_gg)


def _mix_in(h, gain, w_bcv, w_qkr, w_gg):
    S, D = h.shape
    C = w_bcv.shape[2]
    tm = _tile(S, 512, SUBLANES)

    def body(h_ref, gain_ref, w1, w2, w3, o1, o2, o3, n_ref, r_ref):
        n32, rstd = _rms_fwd(h_ref[...], gain_ref[...])
        n = n32.astype(BF16)
        n_ref[...] = n.T
        r_ref[...] = rstd
        for k in range(3):
            o1[k] = _dot(n, w1[k]).astype(BF16)
        o2[...] = _dot(n, w2[...])
        o3[...] = _dot(n, w3[...]).astype(BF16)

    return _call(
        body, "mix_in", (S // tm,),
        [_rows(tm, D), _whole(gain), _whole(w_bcv), _whole(w_qkr), _whole(w_gg)],
        [pl.BlockSpec((3, tm, C), lambda i: (0, i, 0)), _rows(tm, w_qkr.shape[1]), _rows(tm, 2 * D),
         pl.BlockSpec((D, tm), lambda i: (0, i)), _rows(tm, 1)],
        [_sds((3, S, C), BF16), _sds((S, w_qkr.shape[1]), F32), _sds((S, 2 * D), BF16), _sds((D, S), BF16),
         _sds((S, 1), F32)],
    )(h, gain, w_bcv, w_qkr, w_gg)


def _conv_taps(zc):
    rows = lax.broadcasted_iota(jnp.int32, zc.shape, 0)
    z1 = jnp.where(rows >= 1, pltpu.roll(zc, 1, 0), 0.0)
    z2 = jnp.where(rows >= 2, pltpu.roll(zc, 2, 0), 0.0)
    return z1, z2


def _conv_fwd(z_bcv, conv_w):
    _, S, C = z_bcv.shape

    def body(z_ref, w_ref, o_ref):
        w = w_ref[...]
        zc = z_ref[1].astype(F32) * z_ref[2].astype(F32)
        z1, z2 = _conv_taps(zc)
        y = w[0:1] * z2 + w[1:2] * z1 + w[2:3] * zc
        o_ref[...] = (z_ref[0].astype(F32) * y).astype(BF16)

    return _call(
        body, "conv_fwd", (C // LANES,),
        [pl.BlockSpec((3, S, LANES), lambda j: (0, 0, j)), pl.BlockSpec((3, LANES), lambda j: (0, j))],
        pl.BlockSpec((S, LANES), lambda j: (0, j)),
        _sds((S, C), BF16),
    )(z_bcv, conv_w)


def _rope(x, cs, half):
    c, s1, s2 = cs[:, :LANES], cs[:, LANES:2 * LANES], cs[:, 2 * LANES:]
    return x * c + pltpu.roll(x, LANES - half, 1) * s1 + pltpu.roll(x, half, 1) * s2


def _unrope(d, cs, half):
    c, s1, s2 = cs[:, :LANES], cs[:, LANES:2 * LANES], cs[:, 2 * LANES:]
    return d * c + pltpu.roll(d * s1, half, 1) + pltpu.roll(d * s2, LANES - half, 1)


def _mla_prep(z_qkr, gq, gkv, cs, c256_w):
    S = z_qkr.shape[0]
    QL, KVL = gq.shape[1], gkv.shape[1]
    H = N_DEV
    tm = _tile(S, 512, SUBLANES)
    half = ROPE // 2

    def body(z_ref, gq_ref, gkv_ref, cs_ref, w_ref, q_ref, k_ref, v_ref, qn_ref, kvn_ref, rq_ref, rkv_ref):
        z = z_ref[...]
        cs_t = cs_ref[...]
        qn32, rq = _rms_fwd(z[:, :QL], gq_ref[...])
        kvn32, rkv = _rms_fwd(z[:, QL:QL + KVL], gkv_ref[...])
        qn = qn32.astype(BF16)
        kvn = kvn32.astype(BF16)
        qn_ref[...] = qn.T
        kvn_ref[...] = kvn.T
        rq_ref[...] = rq
        rkv_ref[...] = rkv
        krope = _rope(z[:, QL + KVL:], cs_t, half).astype(BF16)
        for h in range(H):
            lo, mid, hi = h * HEAD_SLOT, h * HEAD_SLOT + LANES, (h + 1) * HEAD_SLOT
            q = _dot(qn, w_ref[h, KVL:KVL + QL, :])
            kv = _dot(kvn, w_ref[h, 0:KVL, :])
            q_ref[:, lo:mid] = (q[:, :LANES] * SCORE_SCALE).astype(BF16)
            q_ref[:, mid:hi] = (_rope(q[:, LANES:], cs_t, half) * SCORE_SCALE).astype(BF16)
            k_ref[:, lo:mid] = kv[:, :LANES].astype(BF16)
            k_ref[:, mid:hi] = krope
            v_ref[:, h * VDIM:(h + 1) * VDIM] = kv[:, LANES:].astype(BF16)

    return _call(
        body, "mla_prep", (S // tm,),
        [_rows(tm, z_qkr.shape[1]), _whole(gq), _whole(gkv), _rows(tm, 3 * LANES), _whole(c256_w)],
        [_rows(tm, H * HEAD_SLOT), _rows(tm, H * HEAD_SLOT), _rows(tm, H * VDIM),
         pl.BlockSpec((QL, tm), lambda i: (0, i)), pl.BlockSpec((KVL, tm), lambda i: (0, i)),
         _rows(tm, 1), _rows(tm, 1)],
        [_sds((S, H * HEAD_SLOT), BF16), _sds((S, H * HEAD_SLOT), BF16), _sds((S, H * VDIM), BF16),
         _sds((QL, S), BF16), _sds((KVL, S), BF16), _sds((S, 1), F32), _sds((S, 1), F32)],
    )(z_qkr, gq, gkv, cs, c256_w)


def _chunk_mask(rows, cols, diagonal_row):
    shift = CHUNK.bit_length() - 1
    krow = (lax.broadcasted_iota(jnp.int32, (rows, cols), 0) - diagonal_row) >> shift
    qcol = lax.broadcasted_iota(jnp.int32, (rows, cols), 1) >> shift
    return krow <= qcol


def _attn_fwd(q, k, v, H):
    S = q.shape[0]
    t = _tile(S, 512, CHUNK)
    nq = S // t

    def body(q_ref, k_ref, v_ref, o_ref, ot_ref, lse_ref, vt_ref):
        qi = pl.program_id(1)

        @pl.when(qi == 0)
        def _():
            vt_ref[0:VDIM, :] = v_ref[...].T
            vt_ref[VDIM:, :] = jnp.ones((BF16_ROWS, S), BF16)

        qv = q_ref[...]

        def block(start, width, carry, masked):
            m, acc = carry
            off = pl.multiple_of(start * t, t)
            s = _dot_nt(k_ref[pl.ds(off, width * t), :], qv)
            if masked:
                s = jnp.where(_chunk_mask(width * t, t, (width - 1) * t), s, -1e30)
            m_new = jnp.maximum(m, jnp.max(s, axis=0, keepdims=True))
            p = jnp.exp2(s - m_new).astype(BF16)
            acc = jnp.exp2(m - m_new) * acc + _dot(vt_ref[:, pl.ds(off, width * t)], p)
            return m_new, acc

        init = (jnp.full((1, t), -1e30, F32), jnp.zeros((VDIM + BF16_ROWS, t), F32))
        wide = lax.div(qi, ATTN_FWD_WIDTH)
        carry = lax.fori_loop(0, wide, lambda j, c: block(j * ATTN_FWD_WIDTH, ATTN_FWD_WIDTH, c, False), init)
        left = qi - wide * ATTN_FWD_WIDTH
        for extra in range(ATTN_FWD_WIDTH):
            @pl.when(left == extra)
            def _():
                m, acc = block(qi - extra, extra + 1, carry, True)
                l = acc[VDIM:VDIM + 1]
                out = (acc[0:VDIM] * (1.0 / l)).astype(BF16)
                ot_ref[...] = out
                o_ref[...] = out.T
                lse_ref[0] = jnp.broadcast_to(m + jnp.log2(l), (SUBLANES, t))

    return _call(
        body, "attn_fwd", (H, nq),
        [pl.BlockSpec((t, HEAD_SLOT), lambda h, i: (i, h)), pl.BlockSpec((S, HEAD_SLOT), lambda h, i: (0, h)),
         pl.BlockSpec((S, VDIM), lambda h, i: (0, h))],
        [pl.BlockSpec((t, VDIM), lambda h, i: (i, h)), pl.BlockSpec((VDIM, t), lambda h, i: (h, i)),
         pl.BlockSpec((1, SUBLANES, t), lambda h, i: (h, 0, i))],
        [_sds((S, H * VDIM), BF16), _sds((H * VDIM, S), BF16), _sds((H, SUBLANES, S), F32)],
        [pltpu.VMEM((VDIM + BF16_ROWS, S), BF16)],
    )(q, k, v)


def _merge_wo(o, by, z_gg, h, sq_w, c128_w):
    S, D = h.shape
    C = by.shape[1]
    r = sq_w.shape[1] // 3
    tm = _tile(S, 512, SUBLANES)

    def body(o_ref, by_ref, gg_ref, h_ref, wmo_ref, wo_ref, wco_ref, h2_ref, mg_ref, yc_ref, ym_ref):
        ymla = _dot(o_ref[...], wmo_ref[...].reshape(N_DEV * r, D))
        yconv = _dot(by_ref[...], _cat_slots(wco_ref))
        gg = gg_ref[...].astype(F32)
        merged = (_sig(gg[:, :D]) * yconv + _sig(gg[:, D:]) * ymla).astype(BF16)
        mg_ref[...] = merged.T
        yc_ref[...] = yconv.astype(BF16)
        ym_ref[...] = ymla.astype(BF16)
        h2_ref[...] = h_ref[...] + _dot(merged, wo_ref[...].reshape(N_DEV * r, D))

    return _call(
        body, "merge_wo", (S // tm,),
        [_rows(tm, o.shape[1]), _rows(tm, C), _rows(tm, 2 * D), _rows(tm, D), _slab(sq_w, r, 0), _slab(sq_w, r, 1),
         _slab(c128_w, C, 0)],
        [_rows(tm, D), pl.BlockSpec((D, tm), lambda i: (0, i)), _rows(tm, D), _rows(tm, D)],
        [_sds((S, D), F32), _sds((D, S), BF16), _sds((S, D), BF16), _sds((S, D), BF16)],
    )(o, by, z_gg, h, sq_w, sq_w, c128_w)


def _ple_fwd(h, gain, p, sq_w, c128_w, C):
    S, D = h.shape
    P = p.shape[1]
    r = sq_w.shape[1] // 3
    tm = _tile(S, 512, SUBLANES)

    def body(h_ref, gain_ref, p_ref, wpg_ref, wpp_ref, o_ref, pre_ref, pp_ref, n_ref, r_ref):
        x = h_ref[...]
        n32, rstd = _rms_fwd(x, gain_ref[...])
        n = n32.astype(BF16)
        n_ref[...] = n.T
        r_ref[...] = rstd
        pre = _dot(n, wpg_ref[...].reshape(N_DEV * r, D))
        pp = _dot(p_ref[...].astype(BF16), _cat_slots(wpp_ref))
        pre_ref[...] = pre.astype(BF16)
        pp_ref[...] = pp.astype(BF16)
        o_ref[...] = x + _sig(pre) * pp

    return _call(
        body, "ple_fwd", (S // tm,),
        [_rows(tm, D), _whole(gain), _rows(tm, P), _slab(sq_w, r, 2), _slab(c128_w, P, C // P)],
        [_rows(tm, D), _rows(tm, D), _rows(tm, D), pl.BlockSpec((D, tm), lambda i: (0, i)), _rows(tm, 1)],
        [_sds((S, D), F32), _sds((S, D), BF16), _sds((S, D), BF16), _sds((D, S), BF16), _sds((S, 1), F32)],
    )(h, gain, p, sq_w, c128_w)


def _final_loss(h, gain, target):
    S, D = h.shape
    tm = _tile(S, 512, SUBLANES)

    def body(h_ref, gain_ref, t_ref, dh_ref, loss_ref, dg_ref):
        @pl.when(pl.program_id(0) == 0)
        def _():
            loss_ref[...] = jnp.zeros_like(loss_ref)
            dg_ref[...] = jnp.zeros_like(dg_ref)

        x = h_ref[...]
        gain_v = gain_ref[...]
        y, rstd = _rms_fwd(x, gain_v)
        err = y - t_ref[...]
        loss_ref[...] += 0.5 * jnp.sum(jnp.mean(err * err, axis=-1, keepdims=True))
        dx, dgain = _rms_bwd(err * (1.0 / D), x, rstd, gain_v)
        dh_ref[...] = dx
        dg_ref[...] += dgain

    return _call(
        body, "final_loss", (S // tm,),
        [_rows(tm, D), _whole(gain), _rows(tm, D)],
        [_rows(tm, D), pl.BlockSpec((1, LANES), lambda i: (0, 0)), pl.BlockSpec((1, D), lambda i: (0, 0))],
        [_sds((S, D), F32), _sds((1, LANES), F32), _sds((1, D), F32)],
    )(h, gain, target)


def _tn_call(body, name, grid, in_specs, out_spec, out_shape, scratch, operands, prev):
    n = len(operands)
    if prev is None:
        return _call(body, name, grid, in_specs, out_spec, out_shape, scratch)(*operands)
    assert prev.shape == out_shape.shape and prev.dtype == out_shape.dtype

    def wrapped(*refs):
        body(*refs[:n], *refs[n + 1:])

    return _call(wrapped, name, grid, in_specs + [ANY], out_spec, out_shape, scratch, {n: 0})(*operands, prev)


def _tn_slots(xt, dy, prev, rows_total, row_off):
    K, S = xt.shape
    B, _, c = dy.shape
    tk = _tile(K, 1024, BF16_ROWS)

    def body(xt_ref, dy_ref, o_ref):
        o_ref[0] = _dot(xt_ref[...], dy_ref[0]).astype(BF16)

    return _tn_call(
        body, "tn_slots", (K // tk, B),
        [pl.BlockSpec((tk, S), lambda i, b: (i, 0)), pl.BlockSpec((1, S, c), lambda i, b: (b, 0, 0))],
        pl.BlockSpec((1, tk, c), lambda i, b: (b, row_off // tk + i, 0)),
        _sds((B, rows_total, c), BF16), [], [xt, dy], prev)


def _tn_plain(xt, dy):
    K, S = xt.shape
    B, _, c = dy.shape
    tk = _tile(K, 512, BF16_ROWS)
    tn = _tile(c, 1024, LANES)

    def body(xt_ref, dy_ref, o_ref):
        o_ref[0] = _dot(xt_ref[...], dy_ref[0])

    return _call(
        body, "tn_plain", (K // tk, B, c // tn),
        [pl.BlockSpec((tk, S), lambda i, b, j: (i, 0)), pl.BlockSpec((1, S, tn), lambda i, b, j: (b, 0, j))],
        pl.BlockSpec((1, tk, tn), lambda i, b, j: (b, i, j)),
        _sds((B, K, c), F32),
    )(xt, dy)


def _tn_down(at, dh, prev, rows_total, which):
    nb, c, S = at.shape
    D = dh.shape[1]
    r = c // 2
    tn = _tile(D, 512, LANES)

    def body(at_ref, dh_ref, o_ref):
        g = 0.5 * _dot(at_ref[0], dh_ref[...].astype(BF16))
        o_ref[...] = g.astype(BF16).reshape(2, r, tn)

    return _tn_call(
        body, "tn_down", (D // tn, nb),
        [pl.BlockSpec((1, c, S), lambda j, i: (i, 0, 0)), pl.BlockSpec((S, tn), lambda j, i: (0, j))],
        pl.BlockSpec((2, r, tn), lambda j, i: (i, which, j)),
        _sds((N_DEV, rows_total, D), BF16), [], [at, dh], prev)


def _tn_square(xt, dy, prev, rows_total, member):
    K, S = xt.shape
    N = dy.shape[1]
    r = K // N_DEV
    tk = _tile(K, 512, r)
    tn = _tile(N, 512, LANES)

    def body(xt_ref, dy_ref, o_ref):
        g = _dot(xt_ref[...], dy_ref[...].astype(BF16))
        o_ref[...] = g.astype(BF16).reshape(tk // r, r, tn)

    return _tn_call(
        body, "tn_square", (N // tn, K // tk),
        [pl.BlockSpec((tk, S), lambda j, i: (i, 0)), pl.BlockSpec((S, tn), lambda j, i: (0, j))],
        pl.BlockSpec((tk // r, r, tn), lambda j, i: (i, member, j)),
        _sds((N_DEV, rows_total, N), BF16), [], [xt, dy], prev)


def _tn_cols(x, dy, prev, rows_total, row_block):
    S, K = x.shape
    N = dy.shape[1]
    cw = N // N_DEV

    def body(x_ref, dy_ref, o_ref):
        g = _dot(x_ref[...].astype(BF16).T, dy_ref[...])
        for d in range(N_DEV):
            o_ref[d] = g[:, d * cw:(d + 1) * cw].astype(BF16)

    return _tn_call(
        body, "tn_cols", (1,),
        [pl.BlockSpec((S, K), lambda i: (0, 0)), pl.BlockSpec((S, N), lambda i: (0, 0))],
        pl.BlockSpec((N_DEV, K, cw), lambda i: (0, row_block, 0)),
        _sds((N_DEV, rows_total, cw), BF16), [], [x, dy], prev)


def _tn_heads(qnt, kvnt, dqp, dkv):
    QL, S = qnt.shape
    KVL = kvnt.shape[0]

    def body(qn_ref, kvn_ref, dq_ref, dkv_ref, o_ref):
        o_ref[0, 0:KVL, :] = _dot(kvn_ref[...], dkv_ref[...]).astype(BF16)
        o_ref[0, KVL:KVL + QL, :] = _dot(qn_ref[...], dq_ref[...]).astype(BF16)

    head = pl.BlockSpec((S, HEAD_SLOT), lambda h: (0, h))
    return _call(
        body, "tn_heads", (N_DEV,),
        [pl.BlockSpec((QL, S), lambda h: (0, 0)), pl.BlockSpec((KVL, S), lambda h: (0, 0)), head, head],
        pl.BlockSpec((1, KVL + QL, HEAD_SLOT), lambda h: (h, 0, 0)),
        _sds((N_DEV, KVL + QL, HEAD_SLOT), BF16),
    )(qnt, kvnt, dqp, dkv)


def _ple_bwd(dh, pre, pp, h, rstd, gain, sq_w, after):
    S, D = h.shape
    r = sq_w.shape[1] // 3
    tm = _tile(S, 512, SUBLANES)

    def body(dh_ref, pre_ref, pp_ref, h_ref, r_ref, gain_ref, wpg_ref, *rest):
        o_ref, dpre_ref, dpp_ref, dg_ref = rest[len(after):]

        @pl.when(pl.program_id(0) == 0)
        def _():
            dg_ref[...] = jnp.zeros_like(dg_ref)

        d = dh_ref[...]
        gate = _sig(pre_ref[...].astype(F32))
        dpre = (d * pp_ref[...].astype(F32) * gate * (1.0 - gate)).astype(BF16)
        dpre_ref[...] = dpre
        dpp_ref[...] = (d * gate).astype(BF16)
        dn = _dot_nt(dpre, wpg_ref[...].reshape(N_DEV * r, D))
        dx, dgain = _rms_bwd(dn, h_ref[...], r_ref[...], gain_ref[...])
        o_ref[...] = d + dx
        dg_ref[...] += dgain

    return _call(
        body, "ple_bwd", (S // tm,),
        [_rows(tm, D), _rows(tm, D), _rows(tm, D), _rows(tm, D), _rows(tm, 1), _whole(gain), _slab(sq_w, r, 2)]
        + [ANY] * len(after),
        [_rows(tm, D), _rows(tm, D), _rows(tm, D), pl.BlockSpec((1, D), lambda i: (0, 0))],
        [_sds((S, D), F32), _sds((S, D), BF16), _sds((S, D), BF16), _sds((1, D), F32)],
    )(dh, pre, pp, h, rstd, gain, sq_w, *after)


def _ffn_bwd(dh, jac, gu_w, dn_w, h, rstd, gain, after=()):
    S, D = h.shape
    _, _, c = jac.shape
    nb = N_DEV // 2
    tm = _tile(S, 256, SUBLANES)

    def body(dh_ref, jac_ref, w_ref, wd_ref, h_ref, r_ref, gain_ref, *rest):
        dgu_ref, o_ref, dgain_ref = rest[len(after):]

        @pl.when(pl.program_id(0) == 0)
        def _():
            dgain_ref[...] = jnp.zeros_like(dgain_ref)

        dh_v = dh_ref[...]
        dhb = dh_v.astype(BF16)
        dn = jnp.zeros((tm, D), F32)
        for d in range(nb):
            da = _dot_nt(dhb, _down_weight(wd_ref, d, c))
            dg = (da * jac_ref[d].astype(F32)).astype(BF16)
            du = (da * jac_ref[nb + d].astype(F32)).astype(BF16)
            dgu_ref[d] = dg
            dgu_ref[nb + d] = du
            dn = dn + _dot_nt(dg, w_ref[d]) + _dot_nt(du, w_ref[nb + d])
        dx, dgain = _rms_bwd(dn, h_ref[...], r_ref[...], gain_ref[...])
        o_ref[...] = dh_v + dx
        dgain_ref[...] += dgain

    act = pl.BlockSpec((N_DEV, tm, c), lambda i: (0, i, 0))
    return _call(
        body, "ffn_bwd", (S // tm,),
        [_rows(tm, D), act, _slab(gu_w, D, 0), _slab(dn_w, c // 2, 0), _rows(tm, D), _rows(tm, 1), _whole(gain)]
        + [ANY] * len(after),
        [act, _rows(tm, D), pl.BlockSpec((1, D), lambda i: (0, 0))],
        [_sds((N_DEV, S, c), BF16), _sds((S, D), F32), _sds((1, D), F32)],
    )(dh, jac, gu_w, dn_w, h, rstd, gain, *after)


def _merge_bwd(dh, z_gg, yconv, ymla, o, sq_w, c128_w, C, after):
    S, D = dh.shape
    r = sq_w.shape[1] // 3
    HV = N_DEV * r
    H = HV // VDIM
    tm = _tile(S, 512, SUBLANES)

    def head_rows():
        row = lax.broadcasted_iota(jnp.int32, (SUBLANES * H, HV), 0) >> (SUBLANES.bit_length() - 1)
        col = lax.broadcasted_iota(jnp.int32, (SUBLANES * H, HV), 1) >> (VDIM.bit_length() - 1)
        return jnp.where(row == col, 1.0, 0.0).astype(BF16)

    def body(dh_ref, gg_ref, yc_ref, ym_ref, o_ref, wmo_ref, wo_ref, wco_ref, *rest):
        dgg_ref, dby_ref, do_ref, dyc_ref, dym_ref, dl_ref = rest[len(after):]
        dm = _dot_nt(dh_ref[...].astype(BF16), wo_ref[...].reshape(HV, D))
        gg = gg_ref[...].astype(F32)
        sgc = _sig(gg[:, :D])
        sgm = _sig(gg[:, D:])
        dyc = (dm * sgc).astype(BF16)
        dym = (dm * sgm).astype(BF16)
        dyc_ref[...] = dyc
        dym_ref[...] = dym
        dgg_ref[:, :D] = (dm * yc_ref[...].astype(F32) * sgc * (1.0 - sgc)).astype(BF16)
        dgg_ref[:, D:] = (dm * ym_ref[...].astype(F32) * sgm * (1.0 - sgm)).astype(BF16)
        dby_ref[...] = _dot_nt(dyc, _cat_slots(wco_ref)).astype(BF16)
        do = _dot_nt(dym, wmo_ref[...].reshape(HV, D)).astype(BF16)
        do_ref[...] = do
        prod = do.astype(F32) * o_ref[...].astype(F32)
        hi = prod.astype(BF16)
        lo = (prod - hi.astype(F32)).astype(BF16)
        pick = head_rows()
        dl_ref[...] = _dot_nt(pick, hi) + _dot_nt(pick, lo)

    return _call(
        body, "merge_bwd", (S // tm,),
        [_rows(tm, D), _rows(tm, 2 * D), _rows(tm, D), _rows(tm, D), _rows(tm, HV), _slab(sq_w, r, 0),
         _slab(sq_w, r, 1), _slab(c128_w, C, 0)] + [ANY] * len(after),
        [_rows(tm, 2 * D), _rows(tm, C), _rows(tm, HV), _rows(tm, D), _rows(tm, D),
         pl.BlockSpec((SUBLANES * H, tm), lambda i: (0, i))],
        [_sds((S, 2 * D), BF16), _sds((S, C), BF16), _sds((S, HV), BF16), _sds((S, D), BF16), _sds((S, D), BF16),
         _sds((SUBLANES * H, S), F32)],
    )(dh, z_gg, yconv, ymla, o, sq_w, sq_w, c128_w, *after)


def _conv_bwd(z_bcv, conv_w, dby):
    _, S, C = z_bcv.shape

    def body(z_ref, w_ref, dby_ref, dz_ref, dw_ref):
        w = w_ref[...]
        c = z_ref[1].astype(F32)
        v = z_ref[2].astype(F32)
        d = dby_ref[...].astype(F32)
        zc = c * v
        z1, z2 = _conv_taps(zc)
        y = w[0:1] * z2 + w[1:2] * z1 + w[2:3] * zc
        dz_ref[0] = (d * y).astype(BF16)
        dy = d * z_ref[0].astype(F32)
        rows = lax.broadcasted_iota(jnp.int32, dy.shape, 0)
        dy1 = jnp.where(rows < S - 1, pltpu.roll(dy, S - 1, 0), 0.0)
        dy2 = jnp.where(rows < S - 2, pltpu.roll(dy, S - 2, 0), 0.0)
        dzc = w[2:3] * dy + w[1:2] * dy1 + w[0:1] * dy2
        dz_ref[1] = (dzc * v).astype(BF16)
        dz_ref[2] = (dzc * c).astype(BF16)
        dw_ref[0:1, :] = jnp.sum(dy * z2, axis=0, keepdims=True)
        dw_ref[1:2, :] = jnp.sum(dy * z1, axis=0, keepdims=True)
        dw_ref[2:3, :] = jnp.sum(dy * zc, axis=0, keepdims=True)

    three = pl.BlockSpec((3, S, LANES), lambda j: (0, 0, j))
    wspec = pl.BlockSpec((3, LANES), lambda j: (0, j))
    return _call(
        body, "conv_bwd", (C // LANES,),
        [three, wspec, pl.BlockSpec((S, LANES), lambda j: (0, j))],
        [three, wspec],
        [_sds((3, S, C), BF16), _sds((3, C), F32)],
    )(z_bcv, conv_w, dby)


def _attn_bwd(q, k, v, do, lse, delta, H):
    S = q.shape[0]
    t = _tile(S, 512, CHUNK)
    nk = S // t

    def body(q_ref, k_ref, v_ref, do_ref, lse_ref, dl_ref, dq_ref, dk_ref, dv_ref, dqt_ref):
        kj = pl.program_id(1)

        @pl.when(kj == 0)
        def _():
            dqt_ref[...] = jnp.zeros_like(dqt_ref)

        kv = k_ref[...]
        vv = v_ref[...]
        kt = kv.T

        def block(start, width, carry, masked):
            dk, dv = carry
            off = pl.multiple_of(start * t, t)
            qv = q_ref[pl.ds(off, width * t), :]
            dov = do_ref[pl.ds(off, width * t), :]
            s = _dot_nt(kv, qv)
            if masked:
                s = jnp.where(_chunk_mask(t, width * t, 0), s, -1e30)
            p = jnp.exp2(s - lse_ref[0, 0:1, pl.ds(off, width * t)])
            dp = _dot_nt(vv, dov)
            ds = (p * (dp - dl_ref[0, 0:1, pl.ds(off, width * t)]) * LN2).astype(BF16)
            dqt_ref[:, pl.ds(off, width * t)] += _dot(kt, ds)
            return dk + _dot(ds, qv), dv + _dot(p.astype(BF16), dov)

        init = (jnp.zeros((t, HEAD_SLOT), F32), jnp.zeros((t, VDIM), F32))
        wide = lax.div(nk - 1 - kj, ATTN_BWD_WIDTH)
        left = nk - 1 - kj - wide * ATTN_BWD_WIDTH
        carry = lax.switch(left, [functools.partial(block, kj, extra + 1, init, True)
                                  for extra in range(ATTN_BWD_WIDTH)])
        dk, dv = lax.fori_loop(
            0, wide, lambda j, c: block(kj + 1 + left + j * ATTN_BWD_WIDTH, ATTN_BWD_WIDTH, c, False), carry)
        dk_ref[...] = dk.astype(BF16)
        dv_ref[...] = dv.astype(BF16)

        @pl.when(kj == nk - 1)
        def _():
            dq_ref[...] = (dqt_ref[...] * SCORE_SCALE).T.astype(BF16)

    kspec = lambda w: pl.BlockSpec((t, w), lambda h, j: (j, h))
    qspec = lambda w: pl.BlockSpec((S, w), lambda h, j: (0, h))
    stat = pl.BlockSpec((1, SUBLANES, S), lambda h, j: (h, 0, 0))
    return _call(
        body, "attn_bwd", (H, nk),
        [qspec(HEAD_SLOT), kspec(HEAD_SLOT), kspec(VDIM), qspec(VDIM), stat, stat],
        [qspec(HEAD_SLOT), kspec(HEAD_SLOT), kspec(VDIM)],
        [_sds((S, H * HEAD_SLOT), BF16), _sds((S, H * HEAD_SLOT), BF16), _sds((S, H * VDIM), BF16)],
        [pltpu.VMEM((HEAD_SLOT, S), F32)],
    )(q, k, v, do, lse, delta)


def _mla_prep_bwd(dq, dk, dv, z_qkr, rq, rkv, gq, gkv, cs, c256_w):
    S = z_qkr.shape[0]
    QL, KVL = gq.shape[1], gkv.shape[1]
    H = N_DEV
    tm = _tile(S, 512, SUBLANES)
    half = ROPE // 2

    def body(dq_ref, dk_ref, dv_ref, z_ref, rq_ref, rkv_ref, gq_ref, gkv_ref, cs_ref, w_ref,
             dz_ref, dqp_ref, dkv_ref, dgq_ref, dgkv_ref):
        @pl.when(pl.program_id(0) == 0)
        def _():
            dgq_ref[...] = jnp.zeros_like(dgq_ref)
            dgkv_ref[...] = jnp.zeros_like(dgkv_ref)

        cs_t = cs_ref[...]
        dkr = jnp.zeros((tm, LANES), F32)
        dqn = jnp.zeros((tm, QL), F32)
        dkvn = jnp.zeros((tm, KVL), F32)
        for h in range(H):
            lo, mid, hi = h * HEAD_SLOT, h * HEAD_SLOT + LANES, (h + 1) * HEAD_SLOT
            dqp_ref[:, lo:mid] = dq_ref[:, lo:mid]
            dqp_ref[:, mid:hi] = _unrope(dq_ref[:, mid:hi].astype(F32), cs_t, half).astype(BF16)
            dkv_ref[:, lo:mid] = dk_ref[:, lo:mid]
            dkv_ref[:, mid:hi] = dv_ref[:, h * VDIM:(h + 1) * VDIM]
            dkr = dkr + dk_ref[:, mid:hi].astype(F32)
            dqn = dqn + _dot_nt(dqp_ref[:, lo:hi], w_ref[h, KVL:KVL + QL, :])
            dkvn = dkvn + _dot_nt(dkv_ref[:, lo:hi], w_ref[h, 0:KVL, :])
        z = z_ref[...]
        dqc, dgq = _rms_bwd(dqn, z[:, :QL], rq_ref[...], gq_ref[...])
        dkvc, dgkv = _rms_bwd(dkvn, z[:, QL:QL + KVL], rkv_ref[...], gkv_ref[...])
        dz_ref[:, :QL] = dqc.astype(BF16)
        dz_ref[:, QL:QL + KVL] = dkvc.astype(BF16)
        dz_ref[:, QL + KVL:] = _unrope(dkr, cs_t, half).astype(BF16)
        dgq_ref[...] += dgq
        dgkv_ref[...] += dgkv

    W = z_qkr.shape[1]
    return _call(
        body, "mla_prep_bwd", (S // tm,),
        [_rows(tm, H * HEAD_SLOT), _rows(tm, H * HEAD_SLOT), _rows(tm, H * VDIM), _rows(tm, W), _rows(tm, 1),
         _rows(tm, 1), _whole(gq), _whole(gkv), _rows(tm, 3 * LANES), _whole(c256_w)],
        [_rows(tm, W), _rows(tm, H * HEAD_SLOT), _rows(tm, H * HEAD_SLOT), _whole(gq), _whole(gkv)],
        [_sds((S, W), BF16), _sds((S, H * HEAD_SLOT), BF16), _sds((S, H * HEAD_SLOT), BF16),
         _sds((1, QL), F32), _sds((1, KVL), F32)],
    )(dq, dk, dv, z_qkr, rq, rkv, gq, gkv, cs, c256_w)


def _mix_in_bwd(d_bcv, dz_qkr, dgg, w_bcv, w_qkr, w_gg, h, rstd, gain, dh):
    S, D = h.shape
    C = d_bcv.shape[2]
    tm = _tile(S, 512, SUBLANES)

    def body(db_ref, dq_ref, dgg_ref, wb_ref, wq_ref, wg_ref, h_ref, r_ref, gain_ref, dh_ref, o_ref, dgain_ref):
        @pl.when(pl.program_id(0) == 0)
        def _():
            dgain_ref[...] = jnp.zeros_like(dgain_ref)

        dn = _dot_nt(dq_ref[...], wq_ref[...]) + _dot_nt(dgg_ref[...], wg_ref[...])
        for k in range(3):
            dn = dn + _dot_nt(db_ref[k], wb_ref[k])
        dx, dgain = _rms_bwd(dn, h_ref[...], r_ref[...], gain_ref[...])
        o_ref[...] = dh_ref[...] + dx
        dgain_ref[...] += dgain

    return _call(
        body, "mix_in_bwd", (S // tm,),
        [pl.BlockSpec((3, tm, C), lambda i: (0, i, 0)), _rows(tm, dz_qkr.shape[1]), _rows(tm, dgg.shape[1]),
         _whole(w_bcv), _whole(w_qkr), _whole(w_gg), _rows(tm, D), _rows(tm, 1), _whole(gain), _rows(tm, D)],
        [_rows(tm, D), pl.BlockSpec((1, D), lambda i: (0, 0))],
        [_sds((S, D), F32), _sds((1, D), F32)],
    )(d_bcv, dz_qkr, dgg, w_bcv, w_qkr, w_gg, h, rstd, gain, dh)


def _rope_tables(positions):
    half = ROPE // 2
    inv_freq = ROPE_THETA ** (-jnp.arange(0, ROPE, 2, dtype=F32) / ROPE)
    ang = positions.astype(F32)[:, None] * inv_freq
    cos, sin = jnp.cos(ang), jnp.sin(ang)
    z = jnp.zeros_like(cos)
    pad = jnp.zeros((positions.shape[0], LANES - 2 * half), F32)
    return jnp.concatenate([cos, cos, pad, -sin, z, pad, z, sin, pad], axis=1)


def _grad_rows(w):
    return dict(gu=2 * w["gu1"].shape[1], dn=2 * w["dn1"].shape[1], sq=w["sq"].shape[1], win=w["win"].shape[1],
                c128=w["c128"].shape[1], c256=w["c256"].shape[1])


def _layer_fwd(h0, p_l, cs, w, sm, late):
    C = sm["conv_w"].shape[1]
    QL, KVL = sm["q_norm"].shape[1], sm["kv_norm"].shape[1]
    h1, jac1, at1, n1, r1 = _ffn_fwd(h0, sm["ffn1_norm"], w["gu1"], w["dn1"])
    if late is not None:
        w.update(late(h1))
    w_bcv, w_qkr, w_gg = _win_split(w["win"], C, QL, KVL)
    z_bcv, z_qkr, z_gg, un, rm = _mix_in(h1, sm["mix_norm"], w_bcv, w_qkr, w_gg)
    by = _conv_fwd(z_bcv, sm["conv_w"])
    q, k, v, qn, kvn, rq, rkv = _mla_prep(z_qkr, sm["q_norm"], sm["kv_norm"], cs, w["c256"])
    o, ot, lse = _attn_fwd(q, k, v, N_DEV)
    h2, merged, yconv, ymla = _merge_wo(o, by, z_gg, h1, w["sq"], w["c128"])
    h3, jac2, at2, n2, r2 = _ffn_fwd(h2, sm["ffn2_norm"], w["gu2"], w["dn2"])
    h4, pre, pp, pn, rp = _ple_fwd(h3, sm["ple_norm"], p_l, w["sq"], w["c128"], C)
    saved = dict(h0=h0, jac1=jac1, at1=at1, n1=n1, r1=r1, h1=h1, w_bcv=w_bcv, w_qkr=w_qkr, w_gg=w_gg, z_bcv=z_bcv,
                 z_qkr=z_qkr, z_gg=z_gg, un=un, rm=rm, by=by, q=q, k=k, v=v, qn=qn, kvn=kvn, rq=rq, rkv=rkv, o=o, ot=ot,
                 lse=lse, h2=h2, merged=merged, yconv=yconv, ymla=ymla, jac2=jac2, at2=at2, n2=n2, r2=r2, h3=h3,
                 pre=pre, pp=pp, pn=pn, rp=rp, p=p_l)
    return h4, saved


def _layer_bwd_late(dh4, s, w, sm, after):
    D = dh4.shape[1]
    C = sm["conv_w"].shape[1]
    P = s["p"].shape[1]
    rows = _grad_rows(w)
    small = {}
    dh3, dpre, dpp, small["ple_norm"] = _ple_bwd(dh4, s["pre"], s["pp"], s["h3"], s["rp"], sm["ple_norm"], w["sq"],
                                                 after)
    g_sq = _tn_square(s["pn"], dpre, None, rows["sq"], 2)
    g_c128 = _tn_cols(s["p"], dpp, None, rows["c128"], C // P)

    dgu2, dh2, small["ffn2_norm"] = _ffn_bwd(dh3, s["jac2"], w["gu2"], w["dn2"], s["h2"], s["r2"], sm["ffn2_norm"])
    g_dn = _tn_down(s["at2"], dh3, None, rows["dn"], 1)
    g_gu = _tn_slots(s["n2"], dgu2, None, rows["gu"], D)
    return dh2, dict(gu=g_gu, dn=g_dn, sq=g_sq, c128=g_c128), small


def _layer_bwd_mixer(dh2, part, small, s, cs, w, sm, after):
    C = sm["conv_w"].shape[1]
    rows = _grad_rows(w)
    g_gu, g_dn, g_sq, g_c128 = part["gu"], part["dn"], part["sq"], part["c128"]

    dgg, dby, do, dyc, dym, delta = _merge_bwd(dh2, s["z_gg"], s["yconv"], s["ymla"], s["o"], w["sq"], w["c128"], C,
                                               after)
    g_sq = _tn_square(s["merged"], dh2, g_sq, rows["sq"], 1)
    g_sq = _tn_square(s["ot"], dym, g_sq, rows["sq"], 0)
    g_c128 = _tn_cols(s["by"], dyc, g_c128, rows["c128"], 0)
    d_bcv, small["conv_w"] = _conv_bwd(s["z_bcv"], sm["conv_w"], dby)
    delta = delta.reshape(N_DEV, SUBLANES, delta.shape[1])
    dq, dk, dv = _attn_bwd(s["q"], s["k"], s["v"], do, s["lse"], delta, N_DEV)
    dz_qkr, dqp, dkv, small["q_norm"], small["kv_norm"] = _mla_prep_bwd(
        dq, dk, dv, s["z_qkr"], s["rq"], s["rkv"], sm["q_norm"], sm["kv_norm"], cs, w["c256"])
    g_c256 = _tn_heads(s["qn"], s["kvn"], dqp, dkv)
    un = s["un"]
    g_win = _win_merge(_tn_plain(un, d_bcv), _tn_plain(un, dz_qkr[None])[0], _tn_plain(un, dgg[None])[0],
                       w["win"].shape[2])
    dh1, small["mix_norm"] = _mix_in_bwd(d_bcv, dz_qkr, dgg, s["w_bcv"], s["w_qkr"], s["w_gg"], s["h1"], s["rm"],
                                         sm["mix_norm"], dh2)
    return dh1, dict(gu=g_gu, dn=g_dn, sq=g_sq, win=g_win, c128=g_c128, c256=g_c256), small


def _layer_bwd_first(dh1, part, small, s, w, sm, after):
    rows = _grad_rows(w)
    dgu1, dh0, small["ffn1_norm"] = _ffn_bwd(dh1, s["jac1"], w["gu1"], w["dn1"], s["h0"], s["r1"], sm["ffn1_norm"],
                                             after)
    g_dn = _tn_down(s["at1"], dh1, part["dn"], rows["dn"], 0)
    g_gu = _tn_slots(s["n1"], dgu1, part["gu"], rows["gu"], 0)
    return dh0, dict(part, gu=g_gu, dn=g_dn), small


def _mesh_pos():
    return lax.axis_index("x"), lax.axis_index("y"), lax.axis_index("c")


def _other_chips(x, y):
    return [(1 - x, y), (x, 1 - y), (1 - x, 1 - y)]


def _pack(arrs, flipped, width):
    L = arrs[0].shape[0]
    shapes = [a.shape[:0:-1] if f else a.shape[1:] for a, f in zip(arrs, flipped)]
    R = sum(r for r, _ in shapes)

    def body(*refs):
        o_ref = refs[-1]
        off = 0
        for a_ref, f, (r, c) in zip(refs[:-1], flipped, shapes):
            a = a_ref[0].T if f else a_ref[0]
            o_ref[0, off:off + r, 0:c] = a.astype(BF16)
            if c < width:
                o_ref[0, off:off + r, c:width] = jnp.zeros((r, width - c), BF16)
            off += r

    return _call(
        body, "pack", (L,),
        [pl.BlockSpec((1,) + a.shape[1:], lambda l: (l, 0, 0)) for a in arrs],
        pl.BlockSpec((1, R, width), lambda l: (l, 0, 0)),
        _sds((L, R, width), BF16),
    )(*arrs)


def _handshake(peers):
    barrier = pltpu.get_barrier_semaphore()
    for peer in peers:
        pl.semaphore_signal(barrier, inc=1, device_id=peer, device_id_type=MESH)
    pl.semaphore_wait(barrier, len(peers))


def _sequencer_call(body, name, out_types, sems, collective_id, operands):
    return pl.kernel(
        body, name=name, out_type=out_types,
        mesh=plsc.ScalarSubcoreMesh(axis_name="seq", num_cores=1),
        scratch_types=tuple(pltpu.SemaphoreType.DMA((k,)) for k in sems),
        compiler_params=pltpu.CompilerParams(collective_id=collective_id),
    )(*operands)


def _all_gather(packs, l, after, collective_id):
    n = len(packs)

    def body(*refs):
        ins, outs = refs[:n], refs[n + len(after):2 * n + len(after)]
        send_sems, recv_sems, local_sems = refs[2 * n + len(after):]
        x, y, c = _mesh_pos()
        me, sibling = (x, y, c), (x, y, 1 - c)
        chips = _other_chips(x, y)
        _handshake([sibling] + [(*chip, c) for chip in chips])

        def copy(q, k, block, to, src=None):
            slot = outs[q].at[4 * block[0] + 2 * block[1] + block[2]]
            return pltpu.make_async_remote_copy(
                src_ref=slot if src is None else src, dst_ref=slot,
                send_sem=send_sems.at[7 * q + k], recv_sem=recv_sems.at[7 * q + k], device_id=to, device_id_type=MESH)

        started = []
        for q in range(n):
            src = ins[q].at[l]
            mine = pltpu.make_async_copy(src, outs[q].at[4 * x + 2 * y + c], local_sems.at[q])
            mine.start()
            started.append(mine)
        sends = []
        for q in range(n):
            src = ins[q].at[l]
            sends.append(copy(q, 0, me, sibling, src=src))
            sends += [copy(q, 1 + j, me, (*chip, c), src=src) for j, chip in enumerate(chips)]
        for cp in sends:
            cp.start()
        for q in range(n):
            for j, chip in enumerate(chips):
                copy(q, 1 + j, (*chip, c), me).wait_recv()
                fwd = copy(q, 4 + j, (*chip, c), sibling)
                fwd.start()
                sends.append(fwd)
        for q in range(n):
            copy(q, 0, sibling, me).wait_recv()
            for j, chip in enumerate(chips):
                copy(q, 4 + j, (*chip, 1 - c), me).wait_recv()
        for cp in sends:
            cp.wait_send()
        for mine in started:
            mine.wait()

    return _sequencer_call(
        body, f"all_gather_{collective_id}", [_sds((N_DEV,) + p.shape[1:], p.dtype) for p in packs], (7 * n, 7 * n, n),
        collective_id, list(packs) + list(after))


def _rs_d2d(gs, l, collective_id):
    n = len(gs)

    def body(*refs):
        ins, outs = refs[:n], refs[n:2 * n]
        send_sems, recv_sems = refs[2 * n:]
        x, y, c = _mesh_pos()
        _handshake([(x, y, 1 - c)])
        copies = []
        for q in range(n):
            for j in range(4):
                copies.append(pltpu.make_async_remote_copy(
                    src_ref=ins[q].at[2 * j + (1 - c)], dst_ref=outs[q].at[j], send_sem=send_sems.at[4 * q + j],
                    recv_sem=recv_sems.at[4 * q + j], device_id=(x, y, 1 - c), device_id_type=MESH))
        for cp in copies:
            cp.start()
        for cp in copies:
            cp.wait()

    return _sequencer_call(
        body, f"rs_d2d_{l}", [_sds((4,) + g.shape[1:], g.dtype) for g in gs], (4 * n, 4 * n), collective_id, gs)


def _rs_add_chip(gs, as_, after):
    n = len(gs)
    steps = 4
    tiles = [g.shape[1] // steps for g in gs]

    def chip(k):
        x, y, _ = _mesh_pos()
        return ([(x, y)] + _other_chips(x, y))[k]

    def body(*refs):
        g_refs, a_refs = refs[:4 * n], refs[4 * n:8 * n]
        own_refs, t_refs = refs[8 * n + len(after):9 * n + len(after)], refs[9 * n + len(after):]
        for q in range(n):
            g, a = g_refs[4 * q:4 * q + 4], a_refs[4 * q:4 * q + 4]
            own_refs[q][...] = g[0][0].astype(F32) + a[0][0].astype(F32)
            for k in range(1, 4):
                t_refs[q][k - 1] = (g[k][0].astype(F32) + a[k][0].astype(F32)).astype(BF16)

    def gspec(q, k):
        def index(i):
            px, py = chip(k)
            return 4 * px + 2 * py + lax.axis_index("c"), i, 0
        return pl.BlockSpec((1, tiles[q], gs[q].shape[2]), index)

    def aspec(q, k):
        def index(i):
            px, py = chip(k)
            return 2 * px + py, i, 0
        return pl.BlockSpec((1, tiles[q], gs[q].shape[2]), index)

    in_specs = [gspec(q, k) for q in range(n) for k in range(4)] + [aspec(q, k) for q in range(n) for k in range(4)]
    operands = [g for g in gs for _ in range(4)] + [a for a in as_ for _ in range(4)]
    out_specs = [pl.BlockSpec((tiles[q], gs[q].shape[2]), lambda i: (i, 0)) for q in range(n)]
    out_specs += [pl.BlockSpec((3, tiles[q], gs[q].shape[2]), lambda i: (0, i, 0)) for q in range(n)]
    out_shape = [_sds(g.shape[1:], F32) for g in gs] + [_sds((3,) + g.shape[1:], BF16) for g in gs]
    res = _call(body, "rs_add_chip", (steps,), in_specs + [ANY] * len(after), out_specs, out_shape)(*operands, *after)
    return res[:n], res[n:]


def _rs_ici(ts, l, collective_id):
    n = len(ts)

    def body(*refs):
        ins, outs = refs[:n], refs[n:2 * n]
        send_sems, recv_sems = refs[2 * n:]
        x, y, c = _mesh_pos()
        chips = _other_chips(x, y)
        _handshake([(*chip, c) for chip in chips])
        copies = []
        for q in range(n):
            for k, chip in enumerate(chips):
                copies.append(pltpu.make_async_remote_copy(
                    src_ref=ins[q].at[k], dst_ref=outs[q].at[k], send_sem=send_sems.at[3 * q + k],
                    recv_sem=recv_sems.at[3 * q + k], device_id=(*chip, c), device_id_type=MESH))
        for cp in copies:
            cp.start()
        for cp in copies:
            cp.wait()

    return _sequencer_call(
        body, f"rs_ici_{l}", [_sds(t.shape, t.dtype) for t in ts], (3 * n, 3 * n), collective_id, ts)


def _all_reduce_small(v):
    n, W = v.shape

    def body(v_ref, out_ref, slots, send_sems, recv_sems):
        x, y, c = _mesh_pos()
        me = 4 * x + 2 * y + c
        slots[me] = v_ref[...]
        copies = []
        for k in range(1, N_DEV):
            kx, ky, kc = (k >> 2) & 1, (k >> 1) & 1, k & 1
            peer = (1 - x if kx else x, 1 - y if ky else y, 1 - c if kc else c)
            copies.append(pltpu.make_async_remote_copy(
                src_ref=v_ref, dst_ref=slots.at[me], send_sem=send_sems.at[k - 1], recv_sem=recv_sems.at[k - 1],
                device_id=peer, device_id_type=MESH))
        for cp in copies:
            cp.start()
        for cp in copies:
            cp.wait()
        acc = slots[0]
        for d in range(1, N_DEV):
            acc = acc + slots[d]
        out_ref[...] = acc

    vm = pl.BlockSpec(memory_space=pltpu.VMEM)
    return pl.pallas_call(
        body, name="all_reduce_small",
        out_shape=_sds((n, W), F32),
        in_specs=[vm], out_specs=vm,
        scratch_shapes=[pltpu.VMEM((N_DEV, n, W), F32), pltpu.SemaphoreType.DMA((7,)), pltpu.SemaphoreType.DMA((7,))],
    )(v)


def _adamw_math(w, g, m, v):
    m2 = ADAM_B1 * m + (1.0 - ADAM_B1) * g
    v2 = ADAM_B2 * v + (1.0 - ADAM_B2) * (g * g)
    m_hat = m2 / (1.0 - ADAM_B1 ** ADAM_STEP)
    v_hat = v2 / (1.0 - ADAM_B2 ** ADAM_STEP)
    return -ADAM_LR * (m_hat / (jnp.sqrt(v_hat) + ADAM_EPS) + ADAM_WD * w), m2, v2


def _adamw(w, g, m, v):
    L, r, c = w.shape
    tr = _tile(r, max(SUBLANES, (256 * 1024 // c) // SUBLANES * SUBLANES), SUBLANES)

    def body(w_ref, g_ref, m_ref, v_ref, d_ref, nm_ref, nv_ref):
        d_ref[...], nm_ref[...], nv_ref[...] = _adamw_math(w_ref[...], g_ref[...], m_ref[...], v_ref[...])

    spec = pl.BlockSpec((1, tr, c), lambda l, i: (l, i, 0))
    return _call(body, "adamw", (L, r // tr), [spec] * 4, [spec] * 3, [_sds((L, r, c), F32)] * 3)(w, g, m, v)


def _adamw_reduced(w, m, v, flipped, own, b, row_off, tr, l, prev, after):
    L = w.shape[0]
    c, r = w.shape[1:] if flipped else w.shape[:0:-1]
    W = own.shape[1]
    ob = row_off // tr
    extra = list(prev or ()) + list(after)

    def body(w_ref, m_ref, v_ref, own_ref, b_ref, *rest):
        g_ref, d_ref, nm_ref, nv_ref = rest[len(extra):]
        g = ((own_ref[...] + b_ref[0].astype(F32)) + b_ref[1].astype(F32)) + b_ref[2].astype(F32)
        g = g[:, :c].T if flipped else g[:, :c]
        g_ref[0] = g
        d_ref[0], nm_ref[0], nv_ref[0] = _adamw_math(w_ref[0], g, m_ref[0], v_ref[0])

    spec = pl.BlockSpec((1, c, tr), lambda i: (l, 0, i)) if flipped else pl.BlockSpec((1, tr, c), lambda i: (l, i, 0))
    return _call(
        body, "adamw_reduced", (r // tr,),
        [spec] * 3 + [pl.BlockSpec((tr, W), lambda i: (ob + i, 0)), pl.BlockSpec((3, tr, W), lambda i: (0, ob + i, 0))]
        + [ANY] * len(extra),
        [spec] * 4, [_sds(w.shape, F32)] * 4,
        aliases={5 + k: k for k in range(4)} if prev else None,
    )(w, m, v, own, b, *extra)


_MEMBERS = dict(gu=("ffn1_w_gu", "ffn2_w_gu"), dn=("ffn1_w_down", "ffn2_w_down"),
                sq=("w_mla_out", "w_o", "w_ple_gate"), win=("w_in",), c128=("w_conv_out", "w_ple_proj"),
                c256=("w_ukv", "w_uq"))
_GATHER_MEMBERS = dict(_MEMBERS, gu1=("ffn1_w_gu",), gu2=("ffn2_w_gu",), dn1=("ffn1_w_down",), dn2=("ffn2_w_down",))
GATHER_STAGES = (("gu1", "dn1"), ("win", "c256", "c128", "sq"), ("gu2", "dn2"))
_FLIPPED = ("ffn1_w_gu", "ffn2_w_gu", "w_in", "w_uq")
_SMALL = ("ffn1_norm", "mix_norm", "q_norm", "kv_norm", "ffn2_norm", "ple_norm")
_ORDER = ("ffn1_norm", "ffn1_w_gu", "ffn1_w_down", "mix_norm", "w_in", "conv_w", "w_conv_out", "q_norm", "kv_norm",
          "w_uq", "w_ukv", "w_mla_out", "w_o", "ffn2_norm", "ffn2_w_gu", "ffn2_w_down", "ple_norm", "w_ple_gate",
          "w_ple_proj", "final_norm")


def _class_width(wts, cls):
    return HEAD_SLOT if cls == "c256" else wts[_GATHER_MEMBERS[cls][0]].shape[2]


def _pack_rows(vecs, width):
    flat = jnp.concatenate([a.reshape(-1) for a in vecs])
    n = flat.shape[0]
    rows = -(-n // width)
    rows = -(-rows // SUBLANES) * SUBLANES
    flat = jnp.pad(flat, (0, rows * width - n))
    offs, o = [], 0
    for a in vecs:
        offs.append(o)
        o += a.size
    return flat.reshape(rows, width), offs


def _unpack_rows(packed, vecs, offs):
    flat = packed.reshape(-1)
    return [flat[o:o + a.size].reshape(a.shape) for a, o in zip(vecs, offs)]


def _train(x, p, positions, target, gathered, packs, small_w, final_norm, update, reduce_small=None):
    cs = _rope_tables(positions)
    L = len(small_w)
    h = x
    saved = []
    def gather(l, names, after, collective_id):
        got = _all_gather([packs[n] for n in names], l, after, collective_id)
        return dict(zip(names, got))

    late = None
    if packs is not None:
        first, mixer, second = GATHER_STAGES
        w0 = gather(0, first, [], 0)
        w0.update(gather(0, mixer, [w0[first[0]]], 1))
        gathered = [w0]
        late = lambda h1: gather(0, second, [h1], 2)
    everything = sum(GATHER_STAGES, ())
    for l in range(L):
        h, s = _layer_fwd(h, p[l], cs, gathered[l], small_w[l], late)
        late = None
        saved.append(s)
        if packs is not None and l + 1 < L:
            gathered.append(gather(l + 1, everything, [s["by"]], 2 + l + 1))
    dh, loss, d_final = _final_loss(h, final_norm, target)
    grads, smalls = [None] * L, [None] * L
    exchanged = None
    landing = None

    def second_stage(after):
        l, gs, as_ = exchanged
        owns, ts = _rs_add_chip(gs, as_, [after])
        return l, owns, _rs_ici(ts, l, 2 * L + 2 + l)

    for l in reversed(range(L)):
        dh, part, small = _layer_bwd_late(dh, saved[l], gathered[l], small_w[l], [])
        pin = []
        if exchanged is not None:
            landing = second_stage(dh)
            pin = [landing[1][0]]
        dh, part, small = _layer_bwd_mixer(dh, part, small, saved[l], cs, gathered[l], small_w[l], pin)
        pin = [update(*landing)] if exchanged is not None else []
        dh, g, smalls[l] = _layer_bwd_first(dh, part, small, saved[l], gathered[l], small_w[l], pin)
        if update is not None:
            gs = [g[cls] for cls in CLASSES]
            exchanged = (l, gs, _rs_d2d(gs, l, L + 2 + l))
        else:
            grads[l] = g
    if update is not None:
        update(*second_stage(reduce_small(smalls, d_final, loss[0, 0])))
    return loss[0, 0], dh, grads, smalls, d_final


def kernel(x, p, positions, ffn1_norm, ffn1_w_gu, ffn1_w_down, mix_norm, w_in, conv_w, w_conv_out, q_norm, kv_norm, w_uq, w_ukv, w_mla_out, w_o, ffn2_norm, ffn2_w_gu, ffn2_w_down, ple_norm, w_ple_gate, w_ple_proj, final_norm, loss_target, m_ffn1_norm, m_ffn1_w_gu, m_ffn1_w_down, m_mix_norm, m_w_in, m_conv_w, m_w_conv_out, m_q_norm, m_kv_norm, m_w_uq, m_w_ukv, m_w_mla_out, m_w_o, m_ffn2_norm, m_ffn2_w_gu, m_ffn2_w_down, m_ple_norm, m_w_ple_gate, m_w_ple_proj, m_final_norm, v_ffn1_norm, v_ffn1_w_gu, v_ffn1_w_down, v_mix_norm, v_w_in, v_conv_w, v_w_conv_out, v_q_norm, v_kv_norm, v_w_uq, v_w_ukv, v_w_mla_out, v_w_o, v_ffn2_norm, v_ffn2_w_gu, v_ffn2_w_down, v_ple_norm, v_w_ple_gate, v_w_ple_proj, v_final_norm):
    args = dict(locals())
    wts = {n: args[n] for n in _ORDER}
    L = w_in.shape[0]
    dev = 4 * lax.axis_index("x") + 2 * lax.axis_index("y") + lax.axis_index("c")

    view = lambda n, a: jnp.swapaxes(a, 1, 2) if n in _FLIPPED else a
    packs = {cls: _pack([view(n, wts[n]) for n in _GATHER_MEMBERS[cls]], [n in _FLIPPED for n in _GATHER_MEMBERS[cls]],
                        _class_width(wts, cls))
             for stage in GATHER_STAGES for cls in stage}
    cw = conv_w.shape[2]
    conv_full = lax.dynamic_update_slice(jnp.zeros((L, 3, N_DEV * cw), F32), conv_w, (0, 0, dev * cw))
    conv_packed, conv_offs = _pack_rows([conv_full], FLAT_COLS)
    conv_full = _unpack_rows(_all_reduce_small(conv_packed), [conv_full], conv_offs)[0]
    small_w = [dict({n: wts[n][l][None, :] for n in _SMALL}, conv_w=conv_full[l]) for l in range(L)]

    done = {}

    def update(l, owns, bs):
        for q, cls in enumerate(CLASSES):
            off = 0
            rows = [wts[n].shape[1] for n in _MEMBERS[cls]]
            tr = _tile(math.gcd(*rows), 256, BF16_ROWS)
            for n, r in zip(_MEMBERS[cls], rows):
                done[n] = _adamw_reduced(view(n, wts[n]), view(n, args["m_" + n]), view(n, args["v_" + n]),
                                         n in _FLIPPED, owns[q], bs[q], off, tr, l, done.get(n), [])
                off += r
        return done[_MEMBERS[CLASSES[-1]][-1]][0]

    reduced = {}

    def reduce_small(smalls, d_final, loss_dev):
        small = [jnp.stack([smalls[l][n][0] for l in range(L)]) for n in _SMALL]
        small += [jnp.stack([smalls[l]["conv_w"] for l in range(L)]), d_final[0], loss_dev[None]]
        packed, offs = _pack_rows(small, FLAT_COLS)
        total = _all_reduce_small(packed)
        reduced["small"] = _unpack_rows(total, small, offs)
        return total

    _, grad_x, _, _, _ = _train(x[0], p[:, 0], positions[0], loss_target[0], None, packs, small_w,
                                final_norm[None, :], update, reduce_small)

    small = reduced["small"]
    grad = dict(zip(_SMALL, small))
    grad["conv_w"] = lax.dynamic_slice(small[len(_SMALL)], (0, 0, dev * cw), (L, 3, cw))
    grad["final_norm"] = small[-2]
    loss = small[-1][0]

    deltas, new_m, new_v = {}, {}, {}
    for n, outs in done.items():
        grad[n], deltas[n], new_m[n], new_v[n] = (view(n, a) for a in outs)
    for n in _SMALL + ("conv_w", "final_norm"):
        w3 = wts[n].reshape((1,) * (3 - wts[n].ndim) + wts[n].shape)
        d, nm, nv = _adamw(w3, grad[n].reshape(w3.shape), args["m_" + n].reshape(w3.shape),
                           args["v_" + n].reshape(w3.shape))
        deltas[n], new_m[n], new_v[n] = (a.reshape(wts[n].shape) for a in (d, nm, nv))
    return (loss, grad_x[None], *[grad[n] for n in _ORDER], *[deltas[n] for n in _ORDER],
            *[new_m[n] for n in _ORDER], *[new_v[n] for n in _ORDER])
```

```python
import functools
import math

import jax
import jax.numpy as jnp
from jax import lax
from jax.experimental import pallas as pl
from jax.experimental.pallas import tpu as pltpu
from jax.experimental.pallas import tpu_sc as plsc

F32 = jnp.float32
BF16 = jnp.bfloat16

CHUNK = 64
NOPE = 128
ROPE = 64
VDIM = 128
ROPE_THETA = 10000.0
EPS = 1e-6
ATTN_SCALE = (NOPE + ROPE) ** -0.5
SCORE_SCALE = ATTN_SCALE * math.log2(math.e)
LN2 = math.log(2.0)
ADAM_LR = 0.001
ADAM_B1 = 0.9
ADAM_B2 = 0.999
ADAM_EPS = 1e-08
ADAM_WD = 0.01
ADAM_STEP = 10

LANES = 128
SUBLANES = 8
BF16_ROWS = 16
V7X_VMEM_BYTES = 64 * 1024 * 1024
VMEM_LIMIT = V7X_VMEM_BYTES * 7 // 8
HEAD_SLOT = 2 * LANES
N_DEV = 8
ATTN_FWD_WIDTH = 8
ATTN_BWD_WIDTH = 8
FLAT_COLS = 1024
CLASSES = ("gu", "dn", "sq", "win", "c128", "c256")

NT = (((1,), (1,)), ((), ()))
MESH = pl.DeviceIdType.MESH
ANY = pl.BlockSpec(memory_space=pl.ANY)


def _dot(a, b):
    return jnp.dot(a, b, preferred_element_type=F32)


def _dot_nt(a, b):
    return lax.dot_general(a, b, NT, preferred_element_type=F32)


def _sig(x):
    return 1.0 / (1.0 + jnp.exp(-x))


def _tile(n, pref, unit):
    if n <= pref:
        return n
    t = (pref // unit) * unit
    while t >= unit:
        if n % t == 0:
            return t
        t -= unit
    return n


def _call(body, name, grid, in_specs, out_specs, out_shape, scratch=(), aliases=None):
    return pl.pallas_call(
        body,
        name=name,
        grid=grid,
        in_specs=in_specs,
        out_specs=out_specs,
        out_shape=out_shape,
        scratch_shapes=list(scratch),
        input_output_aliases=aliases or {},
        compiler_params=pltpu.CompilerParams(
            dimension_semantics=("arbitrary",) * len(grid), vmem_limit_bytes=VMEM_LIMIT
        ),
    )


def _sds(shape, dtype):
    return jax.ShapeDtypeStruct(shape, dtype)


def _rms_fwd(x, gain):
    rstd = lax.rsqrt(jnp.mean(x * x, axis=-1, keepdims=True) + EPS)
    return x * rstd * gain, rstd


def _rms_bwd(dn, x, rstd, gain):
    xhat = x * rstd
    dgy = dn * gain
    dx = rstd * (dgy - xhat * jnp.mean(dgy * xhat, axis=-1, keepdims=True))
    return dx, jnp.sum(dn * xhat, axis=0, keepdims=True)


def _rows(tm, w):
    return pl.BlockSpec((tm, w), lambda i: (i, 0))


def _whole(a):
    nd = a.ndim
    return pl.BlockSpec(a.shape, lambda i: (0,) * nd, pipeline_mode=pl.Buffered(1))


def _slab(buf, rows, index):
    return pl.BlockSpec((N_DEV, rows, buf.shape[2]), lambda i: (0, index, 0), pipeline_mode=pl.Buffered(1))


def _cat_slots(w):
    return jnp.concatenate([w[d] for d in range(N_DEV)], axis=1)


def _down_weight(w_ref, d, c):
    return w_ref[2 * d:2 * d + 2].reshape(c, w_ref.shape[2])


def _ffn_fwd(h, gain, gu_w, dn_w):
    S, D = h.shape
    c = gu_w.shape[2]
    tm = _tile(S, 512, SUBLANES)
    nb = N_DEV // 2

    def body(h_ref, gain_ref, w_ref, wd_ref, o_ref, jac_ref, at_ref, n_ref, r_ref):
        x = h_ref[...]
        n32, rstd = _rms_fwd(x, gain_ref[...])
        n = n32.astype(BF16)
        n_ref[...] = n.T
        r_ref[...] = rstd
        acc = jnp.zeros((tm, D), F32)
        for d in range(nb):
            g = _dot(n, w_ref[d])
            u = _dot(n, w_ref[nb + d])
            sg = _sig(g)
            silu = g * sg
            a = (silu * u).astype(BF16)
            at_ref[d] = a.T
            jac_ref[d] = (0.5 * u * (sg + silu * (1.0 - sg))).astype(BF16)
            jac_ref[nb + d] = (0.5 * silu).astype(BF16)
            acc = acc + _dot(a, _down_weight(wd_ref, d, c))
        o_ref[...] = x + 0.5 * acc

    return _call(
        body, "ffn_fwd", (S // tm,),
        [_rows(tm, D), _whole(gain), _slab(gu_w, D, 0), _slab(dn_w, c // 2, 0)],
        [_rows(tm, D), pl.BlockSpec((N_DEV, tm, c), lambda i: (0, i, 0)),
         pl.BlockSpec((nb, c, tm), lambda i: (0, 0, i)), pl.BlockSpec((D, tm), lambda i: (0, i)), _rows(tm, 1)],
        [_sds((S, D), F32), _sds((N_DEV, S, c), BF16), _sds((nb, c, S), BF16), _sds((D, S), BF16),
         _sds((S, 1), F32)],
    )(h, gain, gu_w, dn_w)


def _win_segments(C, QL, KVL, D):
    o1, o2 = 3 * C, 3 * C + QL + KVL + ROPE
    return [("bcv", k, k * C, (k + 1) * C) for k in range(3)] + [("qkr", None, o1, o2), ("gg", None, o2, o2 + 2 * D)]


def _win_pieces(segments, cw):
    out = []
    for tgt, lead, a, b in segments:
        for d in range(N_DEV):
            lo, hi = max(a, d * cw), min(b, (d + 1) * cw)
            if lo < hi:
                out.append((tgt, lead, d, (lo - d * cw, hi - d * cw), (lo - a, hi - a)))
    return out


def _win_split(win_w, C, QL, KVL):
    _, D, cw = win_w.shape
    WQ = QL + KVL + LANES
    pieces = _win_pieces(_win_segments(C, QL, KVL, D), cw)
    tr = _tile(D, 256, BF16_ROWS)

    def body(w_ref, bcv_ref, qkr_ref, gg_ref):
        tgt = dict(bcv=bcv_ref, qkr=qkr_ref, gg=gg_ref)
        qkr_ref[:, QL + KVL + ROPE:] = jnp.zeros((tr, LANES - ROPE), BF16)
        for name, lead, d, (s0, s1), (t0, t1) in pieces:
            v = w_ref[d, :, s0:s1]
            if lead is None:
                tgt[name][:, t0:t1] = v
            else:
                tgt[name][lead, :, t0:t1] = v

    return _call(
        body, "win_split", (D // tr,),
        [pl.BlockSpec((N_DEV, tr, cw), lambda i: (0, i, 0))],
        [pl.BlockSpec((3, tr, C), lambda i: (0, i, 0)), _rows(tr, WQ), _rows(tr, 2 * D)],
        [_sds((3, D, C), BF16), _sds((D, WQ), BF16), _sds((D, 2 * D), BF16)],
    )(win_w)


def _win_merge(d_bcv, d_qkr, d_gg, cw):
    _, D, C = d_bcv.shape
    WQ = d_qkr.shape[1]
    QL_KVL = WQ - LANES
    o1 = 3 * C
    segments = [("bcv", k, k * C, (k + 1) * C) for k in range(3)]
    segments += [("qkr", None, o1, o1 + QL_KVL + ROPE), ("gg", None, o1 + QL_KVL + ROPE, o1 + QL_KVL + ROPE + 2 * D)]
    pieces = _win_pieces(segments, cw)
    tr = _tile(D, 256, BF16_ROWS)

    def body(bcv_ref, qkr_ref, gg_ref, o_ref):
        src = dict(bcv=bcv_ref, qkr=qkr_ref, gg=gg_ref)
        for name, lead, d, (s0, s1), (t0, t1) in pieces:
            v = src[name][:, t0:t1] if lead is None else src[name][lead, :, t0:t1]
            o_ref[d, :, s0:s1] = v.astype(BF16)

    return _call(
        body, "win_merge", (D // tr,),
        [pl.BlockSpec((3, tr, C), lambda i: (0, i, 0)), _rows(tr, WQ), _rows(tr, 2 * D)],
        pl.BlockSpec((N_DEV, tr, cw), lambda i: (0, i, 0)),
        _sds((N_DEV, D, cw), BF16),
    )(d_bcv, d_qkr, d_gg)


def _mix_in(h, gain, w_bcv, w_qkr, w_gg):
    S, D = h.shape
    C = w_bcv.shape[2]
    tm = _tile(S, 512, SUBLANES)

    def body(h_ref, gain_ref, w1, w2, w3, o1, o2, o3, n_ref, r_ref):
        n32, rstd = _rms_fwd(h_ref[...], gain_ref[...])
        n = n32.astype(BF16)
        n_ref[...] = n.T
        r_ref[...] = rstd
        for k in range(3):
            o1[k] = _dot(n, w1[k]).astype(BF16)
        o2[...] = _dot(n, w2[...])
        o3[...] = _dot(n, w3[...]).astype(BF16)

    return _call(
        body, "mix_in", (S // tm,),
        [_rows(tm, D), _whole(gain), _whole(w_bcv), _whole(w_qkr), _whole(w_gg)],
        [pl.BlockSpec((3, tm, C), lambda i: (0, i, 0)), _rows(tm, w_qkr.shape[1]), _rows(tm, 2 * D),
         pl.BlockSpec((D, tm), lambda i: (0, i)), _rows(tm, 1)],
        [_sds((3, S, C), BF16), _sds((S, w_qkr.shape[1]), F32), _sds((S, 2 * D), BF16), _sds((D, S), BF16),
         _sds((S, 1), F32)],
    )(h, gain, w_bcv, w_qkr, w_gg)


def _conv_taps(zc):
    rows = lax.broadcasted_iota(jnp.int32, zc.shape, 0)
    z1 = jnp.where(rows >= 1, pltpu.roll(zc, 1, 0), 0.0)
    z2 = jnp.where(rows >= 2, pltpu.roll(zc, 2, 0), 0.0)
    return z1, z2


def _conv_fwd(z_bcv, conv_w):
    _, S, C = z_bcv.shape

    def body(z_ref, w_ref, o_ref):
        w = w_ref[...]
        zc = z_ref[1].astype(F32) * z_ref[2].astype(F32)
        z1, z2 = _conv_taps(zc)
        y = w[0:1] * z2 + w[1:2] * z1 + w[2:3] * zc
        o_ref[...] = (z_ref[0].astype(F32) * y).astype(BF16)

    return _call(
        body, "conv_fwd", (C // LANES,),
        [pl.BlockSpec((3, S, LANES), lambda j: (0, 0, j)), pl.BlockSpec((3, LANES), lambda j: (0, j))],
        pl.BlockSpec((S, LANES), lambda j: (0, j)),
        _sds((S, C), BF16),
    )(z_bcv, conv_w)


def _rope(x, cs, half):
    c, s1, s2 = cs[:, :LANES], cs[:, LANES:2 * LANES], cs[:, 2 * LANES:]
    return x * c + pltpu.roll(x, LANES - half, 1) * s1 + pltpu.roll(x, half, 1) * s2


def _unrope(d, cs, half):
    c, s1, s2 = cs[:, :LANES], cs[:, LANES:2 * LANES], cs[:, 2 * LANES:]
    return d * c + pltpu.roll(d * s1, half, 1) + pltpu.roll(d * s2, LANES - half, 1)


def _mla_prep(z_qkr, gq, gkv, cs, c256_w):
    S = z_qkr.shape[0]
    QL, KVL = gq.shape[1], gkv.shape[1]
    H = N_DEV
    tm = _tile(S, 512, SUBLANES)
    half = ROPE // 2

    def body(z_ref, gq_ref, gkv_ref, cs_ref, w_ref, q_ref, k_ref, v_ref, qn_ref, kvn_ref, rq_ref, rkv_ref):
        z = z_ref[...]
        cs_t = cs_ref[...]
        qn32, rq = _rms_fwd(z[:, :QL], gq_ref[...])
        kvn32, rkv = _rms_fwd(z[:, QL:QL + KVL], gkv_ref[...])
        qn = qn32.astype(BF16)
        kvn = kvn32.astype(BF16)
        qn_ref[...] = qn.T
        kvn_ref[...] = kvn.T
        rq_ref[...] = rq
        rkv_ref[...] = rkv
        krope = _rope(z[:, QL + KVL:], cs_t, half).astype(BF16)
        for h in range(H):
            lo, mid, hi = h * HEAD_SLOT, h * HEAD_SLOT + LANES, (h + 1) * HEAD_SLOT
            q = _dot(qn, w_ref[h, KVL:KVL + QL, :])
            kv = _dot(kvn, w_ref[h, 0:KVL, :])
            q_ref[:, lo:mid] = (q[:, :LANES] * SCORE_SCALE).astype(BF16)
            q_ref[:, mid:hi] = (_rope(q[:, LANES:], cs_t, half) * SCORE_SCALE).astype(BF16)
            k_ref[:, lo:mid] = kv[:, :LANES].astype(BF16)
            k_ref[:, mid:hi] = krope
            v_ref[:, h * VDIM:(h + 1) * VDIM] = kv[:, LANES:].astype(BF16)

    return _call(
        body, "mla_prep", (S // tm,),
        [_rows(tm, z_qkr.shape[1]), _whole(gq), _whole(gkv), _rows(tm, 3 * LANES), _whole(c256_w)],
        [_rows(tm, H * HEAD_SLOT), _rows(tm, H * HEAD_SLOT), _rows(tm, H * VDIM),
         pl.BlockSpec((QL, tm), lambda i: (0, i)), pl.BlockSpec((KVL, tm), lambda i: (0, i)),
         _rows(tm, 1), _rows(tm, 1)],
        [_sds((S, H * HEAD_SLOT), BF16), _sds((S, H * HEAD_SLOT), BF16), _sds((S, H * VDIM), BF16),
         _sds((QL, S), BF16), _sds((KVL, S), BF16), _sds((S, 1), F32), _sds((S, 1), F32)],
    )(z_qkr, gq, gkv, cs, c256_w)


def _chunk_mask(rows, cols, diagonal_row):
    shift = CHUNK.bit_length() - 1
    krow = (lax.broadcasted_iota(jnp.int32, (rows, cols), 0) - diagonal_row) >> shift
    qcol = lax.broadcasted_iota(jnp.int32, (rows, cols), 1) >> shift
    return krow <= qcol


def _attn_fwd(q, k, v, H):
    S = q.shape[0]
    t = _tile(S, 512, CHUNK)
    nq = S // t

    def body(q_ref, k_ref, v_ref, o_ref, ot_ref, lse_ref, vt_ref):
        qi = pl.program_id(1)

        @pl.when(qi == 0)
        def _():
            vt_ref[0:VDIM, :] = v_ref[...].T
            vt_ref[VDIM:, :] = jnp.ones((BF16_ROWS, S), BF16)

        qv = q_ref[...]

        def block(start, width, carry, masked):
            m, acc = carry
            off = pl.multiple_of(start * t, t)
            s = _dot_nt(k_ref[pl.ds(off, width * t), :], qv)
            if masked:
                s = jnp.where(_chunk_mask(width * t, t, (width - 1) * t), s, -1e30)
            m_new = jnp.maximum(m, jnp.max(s, axis=0, keepdims=True))
            p = jnp.exp2(s - m_new).astype(BF16)
            acc = jnp.exp2(m - m_new) * acc + _dot(vt_ref[:, pl.ds(off, width * t)], p)
            return m_new, acc

        init = (jnp.full((1, t), -1e30, F32), jnp.zeros((VDIM + BF16_ROWS, t), F32))
        wide = lax.div(qi, ATTN_FWD_WIDTH)
        carry = lax.fori_loop(0, wide, lambda j, c: block(j * ATTN_FWD_WIDTH, ATTN_FWD_WIDTH, c, False), init)
        left = qi - wide * ATTN_FWD_WIDTH
        for extra in range(ATTN_FWD_WIDTH):
            @pl.when(left == extra)
            def _():
                m, acc = block(qi - extra, extra + 1, carry, True)
                l = acc[VDIM:VDIM + 1]
                out = (acc[0:VDIM] * (1.0 / l)).astype(BF16)
                ot_ref[...] = out
                o_ref[...] = out.T
                lse_ref[0] = jnp.broadcast_to(m + jnp.log2(l), (SUBLANES, t))

    return _call(
        body, "attn_fwd", (H, nq),
        [pl.BlockSpec((t, HEAD_SLOT), lambda h, i: (i, h)), pl.BlockSpec((S, HEAD_SLOT), lambda h, i: (0, h)),
         pl.BlockSpec((S, VDIM), lambda h, i: (0, h))],
        [pl.BlockSpec((t, VDIM), lambda h, i: (i, h)), pl.BlockSpec((VDIM, t), lambda h, i: (h, i)),
         pl.BlockSpec((1, SUBLANES, t), lambda h, i: (h, 0, i))],
        [_sds((S, H * VDIM), BF16), _sds((H * VDIM, S), BF16), _sds((H, SUBLANES, S), F32)],
        [pltpu.VMEM((VDIM + BF16_ROWS, S), BF16)],
    )(q, k, v)


def _merge_wo(o, by, z_gg, h, sq_w, c128_w):
    S, D = h.shape
    C = by.shape[1]
    r = sq_w.shape[1] // 3
    tm = _tile(S, 512, SUBLANES)

    def body(o_ref, by_ref, gg_ref, h_ref, wmo_ref, wo_ref, wco_ref, h2_ref, mg_ref, yc_ref, ym_ref):
        ymla = _dot(o_ref[...], wmo_ref[...].reshape(N_DEV * r, D))
        yconv = _dot(by_ref[...], _cat_slots(wco_ref))
        gg = gg_ref[...].astype(F32)
        merged = (_sig(gg[:, :D]) * yconv + _sig(gg[:, D:]) * ymla).astype(BF16)
        mg_ref[...] = merged.T
        yc_ref[...] = yconv.astype(BF16)
        ym_ref[...] = ymla.astype(BF16)
        h2_ref[...] = h_ref[...] + _dot(merged, wo_ref[...].reshape(N_DEV * r, D))

    return _call(
        body, "merge_wo", (S // tm,),
        [_rows(tm, o.shape[1]), _rows(tm, C), _rows(tm, 2 * D), _rows(tm, D), _slab(sq_w, r, 0), _slab(sq_w, r, 1),
         _slab(c128_w, C, 0)],
        [_rows(tm, D), pl.BlockSpec((D, tm), lambda i: (0, i)), _rows(tm, D), _rows(tm, D)],
        [_sds((S, D), F32), _sds((D, S), BF16), _sds((S, D), BF16), _sds((S, D), BF16)],
    )(o, by, z_gg, h, sq_w, sq_w, c128_w)


def _ple_fwd(h, gain, p, sq_w, c128_w, C):
    S, D = h.shape
    P = p.shape[1]
    r = sq_w.shape[1] // 3
    tm = _tile(S, 512, SUBLANES)

    def body(h_ref, gain_ref, p_ref, wpg_ref, wpp_ref, o_ref, pre_ref, pp_ref, n_ref, r_ref):
        x = h_ref[...]
        n32, rstd = _rms_fwd(x, gain_ref[...])
        n = n32.astype(BF16)
        n_ref[...] = n.T
        r_ref[...] = rstd
        pre = _dot(n, wpg_ref[...].reshape(N_DEV * r, D))
        pp = _dot(p_ref[...].astype(BF16), _cat_slots(wpp_ref))
        pre_ref[...] = pre.astype(BF16)
        pp_ref[...] = pp.astype(BF16)
        o_ref[...] = x + _sig(pre) * pp

    return _call(
        body, "ple_fwd", (S // tm,),
        [_rows(tm, D), _whole(gain), _rows(tm, P), _slab(sq_w, r, 2), _slab(c128_w, P, C // P)],
        [_rows(tm, D), _rows(tm, D), _rows(tm, D), pl.BlockSpec((D, tm), lambda i: (0, i)), _rows(tm, 1)],
        [_sds((S, D), F32), _sds((S, D), BF16), _sds((S, D), BF16), _sds((D, S), BF16), _sds((S, 1), F32)],
    )(h, gain, p, sq_w, c128_w)


def _final_loss(h, gain, target):
    S, D = h.shape
    tm = _tile(S, 512, SUBLANES)

    def body(h_ref, gain_ref, t_ref, dh_ref, loss_ref, dg_ref):
        @pl.when(pl.program_id(0) == 0)
        def _():
            loss_ref[...] = jnp.zeros_like(loss_ref)
            dg_ref[...] = jnp.zeros_like(dg_ref)

        x = h_ref[...]
        gain_v = gain_ref[...]
        y, rstd = _rms_fwd(x, gain_v)
        err = y - t_ref[...]
        loss_ref[...] += 0.5 * jnp.sum(jnp.mean(err * err, axis=-1, keepdims=True))
        dx, dgain = _rms_bwd(err * (1.0 / D), x, rstd, gain_v)
        dh_ref[...] = dx
        dg_ref[...] += dgain

    return _call(
        body, "final_loss", (S // tm,),
        [_rows(tm, D), _whole(gain), _rows(tm, D)],
        [_rows(tm, D), pl.BlockSpec((1, LANES), lambda i: (0, 0)), pl.BlockSpec((1, D), lambda i: (0, 0))],
        [_sds((S, D), F32), _sds((1, LANES), F32), _sds((1, D), F32)],
    )(h, gain, target)


def _tn_call(body, name, grid, in_specs, out_spec, out_shape, scratch, operands, prev):
    n = len(operands)
    if prev is None:
        return _call(body, name, grid, in_specs, out_spec, out_shape, scratch)(*operands)
    assert prev.shape == out_shape.shape and prev.dtype == out_shape.dtype

    def wrapped(*refs):
        body(*refs[:n], *refs[n + 1:])

    return _call(wrapped, name, grid, in_specs + [ANY], out_spec, out_shape, scratch, {n: 0})(*operands, prev)


def _tn_slots(xt, dy, prev, rows_total, row_off):
    K, S = xt.shape
    B, _, c = dy.shape
    tk = _tile(K, 1024, BF16_ROWS)

    def body(xt_ref, dy_ref, o_ref):
        o_ref[0] = _dot(xt_ref[...], dy_ref[0]).astype(BF16)

    return _tn_call(
        body, "tn_slots", (K // tk, B),
        [pl.BlockSpec((tk, S), lambda i, b: (i, 0)), pl.BlockSpec((1, S, c), lambda i, b: (b, 0, 0))],
        pl.BlockSpec((1, tk, c), lambda i, b: (b, row_off // tk + i, 0)),
        _sds((B, rows_total, c), BF16), [], [xt, dy], prev)


def _tn_plain(xt, dy):
    K, S = xt.shape
    B, _, c = dy.shape
    tk = _tile(K, 512, BF16_ROWS)
    tn = _tile(c, 1024, LANES)

    def body(xt_ref, dy_ref, o_ref):
        o_ref[0] = _dot(xt_ref[...], dy_ref[0])

    return _call(
        body, "tn_plain", (K // tk, B, c // tn),
        [pl.BlockSpec((tk, S), lambda i, b, j: (i, 0)), pl.BlockSpec((1, S, tn), lambda i, b, j: (b, 0, j))],
        pl.BlockSpec((1, tk, tn), lambda i, b, j: (b, i, j)),
        _sds((B, K, c), F32),
    )(xt, dy)


def _tn_down(at, dh, prev, rows_total, which):
    nb, c, S = at.shape
    D = dh.shape[1]
    r = c // 2
    tn = _tile(D, 512, LANES)

    def body(at_ref, dh_ref, o_ref):
        g = 0.5 * _dot(at_ref[0], dh_ref[...].astype(BF16))
        o_ref[...] = g.astype(BF16).reshape(2, r, tn)

    return _tn_call(
        body, "tn_down", (D // tn, nb),
        [pl.BlockSpec((1, c, S), lambda j, i: (i, 0, 0)), pl.BlockSpec((S, tn), lambda j, i: (0, j))],
        pl.BlockSpec((2, r, tn), lambda j, i: (i, which, j)),
        _sds((N_DEV, rows_total, D), BF16), [], [at, dh], prev)


def _tn_square(xt, dy, prev, rows_total, member):
    K, S = xt.shape
    N = dy.shape[1]
    r = K // N_DEV
    tk = _tile(K, 512, r)
    tn = _tile(N, 512, LANES)

    def body(xt_ref, dy_ref, o_ref):
        g = _dot(xt_ref[...], dy_ref[...].astype(BF16))
        o_ref[...] = g.astype(BF16).reshape(tk // r, r, tn)

    return _tn_call(
        body, "tn_square", (N // tn, K // tk),
        [pl.BlockSpec((tk, S), lambda j, i: (i, 0)), pl.BlockSpec((S, tn), lambda j, i: (0, j))],
        pl.BlockSpec((tk // r, r, tn), lambda j, i: (i, member, j)),
        _sds((N_DEV, rows_total, N), BF16), [], [xt, dy], prev)


def _tn_cols(x, dy, prev, rows_total, row_block):
    S, K = x.shape
    N = dy.shape[1]
    cw = N // N_DEV

    def body(x_ref, dy_ref, o_ref):
        g = _dot(x_ref[...].astype(BF16).T, dy_ref[...])
        for d in range(N_DEV):
            o_ref[d] = g[:, d * cw:(d + 1) * cw].astype(BF16)

    return _tn_call(
        body, "tn_cols", (1,),
        [pl.BlockSpec((S, K), lambda i: (0, 0)), pl.BlockSpec((S, N), lambda i: (0, 0))],
        pl.BlockSpec((N_DEV, K, cw), lambda i: (0, row_block, 0)),
        _sds((N_DEV, rows_total, cw), BF16), [], [x, dy], prev)


def _tn_heads(qnt, kvnt, dqp, dkv):
    QL, S = qnt.shape
    KVL = kvnt.shape[0]

    def body(qn_ref, kvn_ref, dq_ref, dkv_ref, o_ref):
        o_ref[0, 0:KVL, :] = _dot(kvn_ref[...], dkv_ref[...]).astype(BF16)
        o_ref[0, KVL:KVL + QL, :] = _dot(qn_ref[...], dq_ref[...]).astype(BF16)

    head = pl.BlockSpec((S, HEAD_SLOT), lambda h: (0, h))
    return _call(
        body, "tn_heads", (N_DEV,),
        [pl.BlockSpec((QL, S), lambda h: (0, 0)), pl.BlockSpec((KVL, S), lambda h: (0, 0)), head, head],
        pl.BlockSpec((1, KVL + QL, HEAD_SLOT), lambda h: (h, 0, 0)),
        _sds((N_DEV, KVL + QL, HEAD_SLOT), BF16),
    )(qnt, kvnt, dqp, dkv)


def _ple_bwd(dh, pre, pp, h, rstd, gain, sq_w, after):
    S, D = h.shape
    r = sq_w.shape[1] // 3
    tm = _tile(S, 512, SUBLANES)

    def body(dh_ref, pre_ref, pp_ref, h_ref, r_ref, gain_ref, wpg_ref, *rest):
        o_ref, dpre_ref, dpp_ref, dg_ref = rest[len(after):]

        @pl.when(pl.program_id(0) == 0)
        def _():
            dg_ref[...] = jnp.zeros_like(dg_ref)

        d = dh_ref[...]
        gate = _sig(pre_ref[...].astype(F32))
        dpre = (d * pp_ref[...].astype(F32) * gate * (1.0 - gate)).astype(BF16)
        dpre_ref[...] = dpre
        dpp_ref[...] = (d * gate).astype(BF16)
        dn = _dot_nt(dpre, wpg_ref[...].reshape(N_DEV * r, D))
        dx, dgain = _rms_bwd(dn, h_ref[...], r_ref[...], gain_ref[...])
        o_ref[...] = d + dx
        dg_ref[...] += dgain

    return _call(
        body, "ple_bwd", (S // tm,),
        [_rows(tm, D), _rows(tm, D), _rows(tm, D), _rows(tm, D), _rows(tm, 1), _whole(gain), _slab(sq_w, r, 2)]
        + [ANY] * len(after),
        [_rows(tm, D), _rows(tm, D), _rows(tm, D), pl.BlockSpec((1, D), lambda i: (0, 0))],
        [_sds((S, D), F32), _sds((S, D), BF16), _sds((S, D), BF16), _sds((1, D), F32)],
    )(dh, pre, pp, h, rstd, gain, sq_w, *after)


def _ffn_bwd(dh, jac, gu_w, dn_w, h, rstd, gain, after=()):
    S, D = h.shape
    _, _, c = jac.shape
    nb = N_DEV // 2
    tm = _tile(S, 256, SUBLANES)

    def body(dh_ref, jac_ref, w_ref, wd_ref, h_ref, r_ref, gain_ref, *rest):
        dgu_ref, o_ref, dgain_ref = rest[len(after):]

        @pl.when(pl.program_id(0) == 0)
        def _():
            dgain_ref[...] = jnp.zeros_like(dgain_ref)

        dh_v = dh_ref[...]
        dhb = dh_v.astype(BF16)
        dn = jnp.zeros((tm, D), F32)
        for d in range(nb):
            da = _dot_nt(dhb, _down_weight(wd_ref, d, c))
            dg = (da * jac_ref[d].astype(F32)).astype(BF16)
            du = (da * jac_ref[nb + d].astype(F32)).astype(BF16)
            dgu_ref[d] = dg
            dgu_ref[nb + d] = du
            dn = dn + _dot_nt(dg, w_ref[d]) + _dot_nt(du, w_ref[nb + d])
        dx, dgain = _rms_bwd(dn, h_ref[...], r_ref[...], gain_ref[...])
        o_ref[...] = dh_v + dx
        dgain_ref[...] += dgain

    act = pl.BlockSpec((N_DEV, tm, c), lambda i: (0, i, 0))
    return _call(
        body, "ffn_bwd", (S // tm,),
        [_rows(tm, D), act, _slab(gu_w, D, 0), _slab(dn_w, c // 2, 0), _rows(tm, D), _rows(tm, 1), _whole(gain)]
        + [ANY] * len(after),
        [act, _rows(tm, D), pl.BlockSpec((1, D), lambda i: (0, 0))],
        [_sds((N_DEV, S, c), BF16), _sds((S, D), F32), _sds((1, D), F32)],
    )(dh, jac, gu_w, dn_w, h, rstd, gain, *after)


def _merge_bwd(dh, z_gg, yconv, ymla, o, sq_w, c128_w, C, after):
    S, D = dh.shape
    r = sq_w.shape[1] // 3
    HV = N_DEV * r
    H = HV // VDIM
    tm = _tile(S, 512, SUBLANES)

    def head_rows():
        row = lax.broadcasted_iota(jnp.int32, (SUBLANES * H, HV), 0) >> (SUBLANES.bit_length() - 1)
        col = lax.broadcasted_iota(jnp.int32, (SUBLANES * H, HV), 1) >> (VDIM.bit_length() - 1)
        return jnp.where(row == col, 1.0, 0.0).astype(BF16)

    def body(dh_ref, gg_ref, yc_ref, ym_ref, o_ref, wmo_ref, wo_ref, wco_ref, *rest):
        dgg_ref, dby_ref, do_ref, dyc_ref, dym_ref, dl_ref = rest[len(after):]
        dm = _dot_nt(dh_ref[...].astype(BF16), wo_ref[...].reshape(HV, D))
        gg = gg_ref[...].astype(F32)
        sgc = _sig(gg[:, :D])
        sgm = _sig(gg[:, D:])
        dyc = (dm * sgc).astype(BF16)
        dym = (dm * sgm).astype(BF16)
        dyc_ref[...] = dyc
        dym_ref[...] = dym
        dgg_ref[:, :D] = (dm * yc_ref[...].astype(F32) * sgc * (1.0 - sgc)).astype(BF16)
        dgg_ref[:, D:] = (dm * ym_ref[...].astype(F32) * sgm * (1.0 - sgm)).astype(BF16)
        dby_ref[...] = _dot_nt(dyc, _cat_slots(wco_ref)).astype(BF16)
        do = _dot_nt(dym, wmo_ref[...].reshape(HV, D)).astype(BF16)
        do_ref[...] = do
        prod = do.astype(F32) * o_ref[...].astype(F32)
        hi = prod.astype(BF16)
        lo = (prod - hi.astype(F32)).astype(BF16)
        pick = head_rows()
        dl_ref[...] = _dot_nt(pick, hi) + _dot_nt(pick, lo)

    return _call(
        body, "merge_bwd", (S // tm,),
        [_rows(tm, D), _rows(tm, 2 * D), _rows(tm, D), _rows(tm, D), _rows(tm, HV), _slab(sq_w, r, 0),
         _slab(sq_w, r, 1), _slab(c128_w, C, 0)] + [ANY] * len(after),
        [_rows(tm, 2 * D), _rows(tm, C), _rows(tm, HV), _rows(tm, D), _rows(tm, D),
         pl.BlockSpec((SUBLANES * H, tm), lambda i: (0, i))],
        [_sds((S, 2 * D), BF16), _sds((S, C), BF16), _sds((S, HV), BF16), _sds((S, D), BF16), _sds((S, D), BF16),
         _sds((SUBLANES * H, S), F32)],
    )(dh, z_gg, yconv, ymla, o, sq_w, sq_w, c128_w, *after)


def _conv_bwd(z_bcv, conv_w, dby):
    _, S, C = z_bcv.shape

    def body(z_ref, w_ref, dby_ref, dz_ref, dw_ref):
        w = w_ref[...]
        c = z_ref[1].astype(F32)
        v = z_ref[2].astype(F32)
        d = dby_ref[...].astype(F32)
        zc = c * v
        z1, z2 = _conv_taps(zc)
        y = w[0:1] * z2 + w[1:2] * z1 + w[2:3] * zc
        dz_ref[0] = (d * y).astype(BF16)
        dy = d * z_ref[0].astype(F32)
        rows = lax.broadcasted_iota(jnp.int32, dy.shape, 0)
        dy1 = jnp.where(rows < S - 1, pltpu.roll(dy, S - 1, 0), 0.0)
        dy2 = jnp.where(rows < S - 2, pltpu.roll(dy, S - 2, 0), 0.0)
        dzc = w[2:3] * dy + w[1:2] * dy1 + w[0:1] * dy2
        dz_ref[1] = (dzc * v).astype(BF16)
        dz_ref[2] = (dzc * c).astype(BF16)
        dw_ref[0:1, :] = jnp.sum(dy * z2, axis=0, keepdims=True)
        dw_ref[1:2, :] = jnp.sum(dy * z1, axis=0, keepdims=True)
        dw_ref[2:3, :] = jnp.sum(dy * zc, axis=0, keepdims=True)

    three = pl.BlockSpec((3, S, LANES), lambda j: (0, 0, j))
    wspec = pl.BlockSpec((3, LANES), lambda j: (0, j))
    return _call(
        body, "conv_bwd", (C // LANES,),
        [three, wspec, pl.BlockSpec((S, LANES), lambda j: (0, j))],
        [three, wspec],
        [_sds((3, S, C), BF16), _sds((3, C), F32)],
    )(z_bcv, conv_w, dby)


def _attn_bwd(q, k, v, do, lse, delta, H):
    S = q.shape[0]
    t = _tile(S, 512, CHUNK)
    nk = S // t

    def body(q_ref, k_ref, v_ref, do_ref, lse_ref, dl_ref, dq_ref, dk_ref, dv_ref, dqt_ref):
        kj = pl.program_id(1)

        @pl.when(kj == 0)
        def _():
            dqt_ref[...] = jnp.zeros_like(dqt_ref)

        kv = k_ref[...]
        vv = v_ref[...]
        kt = kv.T

        def block(start, width, carry, masked):
            dk, dv = carry
            off = pl.multiple_of(start * t, t)
            qv = q_ref[pl.ds(off, width * t), :]
            dov = do_ref[pl.ds(off, width * t), :]
            s = _dot_nt(kv, qv)
            if masked:
                s = jnp.where(_chunk_mask(t, width * t, 0), s, -1e30)
            p = jnp.exp2(s - lse_ref[0, 0:1, pl.ds(off, width * t)])
            dp = _dot_nt(vv, dov)
            ds = (p * (dp - dl_ref[0, 0:1, pl.ds(off, width * t)]) * LN2).astype(BF16)
            dqt_ref[:, pl.ds(off, width * t)] += _dot(kt, ds)
            return dk + _dot(ds, qv), dv + _dot(p.astype(BF16), dov)

        init = (jnp.zeros((t, HEAD_SLOT), F32), jnp.zeros((t, VDIM), F32))
        wide = lax.div(nk - 1 - kj, ATTN_BWD_WIDTH)
        left = nk - 1 - kj - wide * ATTN_BWD_WIDTH
        carry = lax.switch(left, [functools.partial(block, kj, extra + 1, init, True)
                                  for extra in range(ATTN_BWD_WIDTH)])
        dk, dv = lax.fori_loop(
            0, wide, lambda j, c: block(kj + 1 + left + j * ATTN_BWD_WIDTH, ATTN_BWD_WIDTH, c, False), carry)
        dk_ref[...] = dk.astype(BF16)
        dv_ref[...] = dv.astype(BF16)

        @pl.when(kj == nk - 1)
        def _():
            dq_ref[...] = (dqt_ref[...] * SCORE_SCALE).T.astype(BF16)

    kspec = lambda w: pl.BlockSpec((t, w), lambda h, j: (j, h))
    qspec = lambda w: pl.BlockSpec((S, w), lambda h, j: (0, h))
    stat = pl.BlockSpec((1, SUBLANES, S), lambda h, j: (h, 0, 0))
    return _call(
        body, "attn_bwd", (H, nk),
        [qspec(HEAD_SLOT), kspec(HEAD_SLOT), kspec(VDIM), qspec(VDIM), stat, stat],
        [qspec(HEAD_SLOT), kspec(HEAD_SLOT), kspec(VDIM)],
        [_sds((S, H * HEAD_SLOT), BF16), _sds((S, H * HEAD_SLOT), BF16), _sds((S, H * VDIM), BF16)],
        [pltpu.VMEM((HEAD_SLOT, S), F32)],
    )(q, k, v, do, lse, delta)


def _mla_prep_bwd(dq, dk, dv, z_qkr, rq, rkv, gq, gkv, cs, c256_w):
    S = z_qkr.shape[0]
    QL, KVL = gq.shape[1], gkv.shape[1]
    H = N_DEV
    tm = _tile(S, 512, SUBLANES)
    half = ROPE // 2

    def body(dq_ref, dk_ref, dv_ref, z_ref, rq_ref, rkv_ref, gq_ref, gkv_ref, cs_ref, w_ref,
             dz_ref, dqp_ref, dkv_ref, dgq_ref, dgkv_ref):
        @pl.when(pl.program_id(0) == 0)
        def _():
            dgq_ref[...] = jnp.zeros_like(dgq_ref)
            dgkv_ref[...] = jnp.zeros_like(dgkv_ref)

        cs_t = cs_ref[...]
        dkr = jnp.zeros((tm, LANES), F32)
        dqn = jnp.zeros((tm, QL), F32)
        dkvn = jnp.zeros((tm, KVL), F32)
        for h in range(H):
            lo, mid, hi = h * HEAD_SLOT, h * HEAD_SLOT + LANES, (h + 1) * HEAD_SLOT
            dqp_ref[:, lo:mid] = dq_ref[:, lo:mid]
            dqp_ref[:, mid:hi] = _unrope(dq_ref[:, mid:hi].astype(F32), cs_t, half).astype(BF16)
            dkv_ref[:, lo:mid] = dk_ref[:, lo:mid]
            dkv_ref[:, mid:hi] = dv_ref[:, h * VDIM:(h + 1) * VDIM]
            dkr = dkr + dk_ref[:, mid:hi].astype(F32)
            dqn = dqn + _dot_nt(dqp_ref[:, lo:hi], w_ref[h, KVL:KVL + QL, :])
            dkvn = dkvn + _dot_nt(dkv_ref[:, lo:hi], w_ref[h, 0:KVL, :])
        z = z_ref[...]
        dqc, dgq = _rms_bwd(dqn, z[:, :QL], rq_ref[...], gq_ref[...])
        dkvc, dgkv = _rms_bwd(dkvn, z[:, QL:QL + KVL], rkv_ref[...], gkv_ref[...])
        dz_ref[:, :QL] = dqc.astype(BF16)
        dz_ref[:, QL:QL + KVL] = dkvc.astype(BF16)
        dz_ref[:, QL + KVL:] = _unrope(dkr, cs_t, half).astype(BF16)
        dgq_ref[...] += dgq
        dgkv_ref[...] += dgkv

    W = z_qkr.shape[1]
    return _call(
        body, "mla_prep_bwd", (S // tm,),
        [_rows(tm, H * HEAD_SLOT), _rows(tm, H * HEAD_SLOT), _rows(tm, H * VDIM), _rows(tm, W), _rows(tm, 1),
         _rows(tm, 1), _whole(gq), _whole(gkv), _rows(tm, 3 * LANES), _whole(c256_w)],
        [_rows(tm, W), _rows(tm, H * HEAD_SLOT), _rows(tm, H * HEAD_SLOT), _whole(gq), _whole(gkv)],
        [_sds((S, W), BF16), _sds((S, H * HEAD_SLOT), BF16), _sds((S, H * HEAD_SLOT), BF16),
         _sds((1, QL), F32), _sds((1, KVL), F32)],
    )(dq, dk, dv, z_qkr, rq, rkv, gq, gkv, cs, c256_w)


def _mix_in_bwd(d_bcv, dz_qkr, dgg, w_bcv, w_qkr, w_gg, h, rstd, gain, dh):
    S, D = h.shape
    C = d_bcv.shape[2]
    tm = _tile(S, 512, SUBLANES)

    def body(db_ref, dq_ref, dgg_ref, wb_ref, wq_ref, wg_ref, h_ref, r_ref, gain_ref, dh_ref, o_ref, dgain_ref):
        @pl.when(pl.program_id(0) == 0)
        def _():
            dgain_ref[...] = jnp.zeros_like(dgain_ref)

        dn = _dot_nt(dq_ref[...], wq_ref[...]) + _dot_nt(dgg_ref[...], wg_ref[...])
        for k in range(3):
            dn = dn + _dot_nt(db_ref[k], wb_ref[k])
        dx, dgain = _rms_bwd(dn, h_ref[...], r_ref[...], gain_ref[...])
        o_ref[...] = dh_ref[...] + dx
        dgain_ref[...] += dgain

    return _call(
        body, "mix_in_bwd", (S // tm,),
        [pl.BlockSpec((3, tm, C), lambda i: (0, i, 0)), _rows(tm, dz_qkr.shape[1]), _rows(tm, dgg.shape[1]),
         _whole(w_bcv), _whole(w_qkr), _whole(w_gg), _rows(tm, D), _rows(tm, 1), _whole(gain), _rows(tm, D)],
        [_rows(tm, D), pl.BlockSpec((1, D), lambda i: (0, 0))],
        [_sds((S, D), F32), _sds((1, D), F32)],
    )(d_bcv, dz_qkr, dgg, w_bcv, w_qkr, w_gg, h, rstd, gain, dh)


def _rope_tables(positions):
    half = ROPE // 2
    inv_freq = ROPE_THETA ** (-jnp.arange(0, ROPE, 2, dtype=F32) / ROPE)
    ang = positions.astype(F32)[:, None] * inv_freq
    cos, sin = jnp.cos(ang), jnp.sin(ang)
    z = jnp.zeros_like(cos)
    pad = jnp.zeros((positions.shape[0], LANES - 2 * half), F32)
    return jnp.concatenate([cos, cos, pad, -sin, z, pad, z, sin, pad], axis=1)


def _grad_rows(w):
    return dict(gu=2 * w["gu1"].shape[1], dn=2 * w["dn1"].shape[1], sq=w["sq"].shape[1], win=w["win"].shape[1],
                c128=w["c128"].shape[1], c256=w["c256"].shape[1])


def _layer_fwd(h0, p_l, cs, w, sm, late):
    C = sm["conv_w"].shape[1]
    QL, KVL = sm["q_norm"].shape[1], sm["kv_norm"].shape[1]
    h1, jac1, at1, n1, r1 = _ffn_fwd(h0, sm["ffn1_norm"], w["gu1"], w["dn1"])
    if late is not None:
        w.update(late(h1))
    w_bcv, w_qkr, w_gg = _win_split(w["win"], C, QL, KVL)
    z_bcv, z_qkr, z_gg, un, rm = _mix_in(h1, sm["mix_norm"], w_bcv, w_qkr, w_gg)
    by = _conv_fwd(z_bcv, sm["conv_w"])
    q, k, v, qn, kvn, rq, rkv = _mla_prep(z_qkr, sm["q_norm"], sm["kv_norm"], cs, w["c256"])
    o, ot, lse = _attn_fwd(q, k, v, N_DEV)
    h2, merged, yconv, ymla = _merge_wo(o, by, z_gg, h1, w["sq"], w["c128"])
    h3, jac2, at2, n2, r2 = _ffn_fwd(h2, sm["ffn2_norm"], w["gu2"], w["dn2"])
    h4, pre, pp, pn, rp = _ple_fwd(h3, sm["ple_norm"], p_l, w["sq"], w["c128"], C)
    saved = dict(h0=h0, jac1=jac1, at1=at1, n1=n1, r1=r1, h1=h1, w_bcv=w_bcv, w_qkr=w_qkr, w_gg=w_gg, z_bcv=z_bcv,
                 z_qkr=z_qkr, z_gg=z_gg, un=un, rm=rm, by=by, q=q, k=k, v=v, qn=qn, kvn=kvn, rq=rq, rkv=rkv, o=o, ot=ot,
                 lse=lse, h2=h2, merged=merged, yconv=yconv, ymla=ymla, jac2=jac2, at2=at2, n2=n2, r2=r2, h3=h3,
                 pre=pre, pp=pp, pn=pn, rp=rp, p=p_l)
    return h4, saved


def _layer_bwd_late(dh4, s, w, sm, after):
    D = dh4.shape[1]
    C = sm["conv_w"].shape[1]
    P = s["p"].shape[1]
    rows = _grad_rows(w)
    small = {}
    dh3, dpre, dpp, small["ple_norm"] = _ple_bwd(dh4, s["pre"], s["pp"], s["h3"], s["rp"], sm["ple_norm"], w["sq"],
                                                 after)
    g_sq = _tn_square(s["pn"], dpre, None, rows["sq"], 2)
    g_c128 = _tn_cols(s["p"], dpp, None, rows["c128"], C // P)

    dgu2, dh2, small["ffn2_norm"] = _ffn_bwd(dh3, s["jac2"], w["gu2"], w["dn2"], s["h2"], s["r2"], sm["ffn2_norm"])
    g_dn = _tn_down(s["at2"], dh3, None, rows["dn"], 1)
    g_gu = _tn_slots(s["n2"], dgu2, None, rows["gu"], D)
    return dh2, dict(gu=g_gu, dn=g_dn, sq=g_sq, c128=g_c128), small


def _layer_bwd_mixer(dh2, part, small, s, cs, w, sm, after):
    C = sm["conv_w"].shape[1]
    rows = _grad_rows(w)
    g_gu, g_dn, g_sq, g_c128 = part["gu"], part["dn"], part["sq"], part["c128"]

    dgg, dby, do, dyc, dym, delta = _merge_bwd(dh2, s["z_gg"], s["yconv"], s["ymla"], s["o"], w["sq"], w["c128"], C,
                                               after)
    g_sq = _tn_square(s["merged"], dh2, g_sq, rows["sq"], 1)
    g_sq = _tn_square(s["ot"], dym, g_sq, rows["sq"], 0)
    g_c128 = _tn_cols(s["by"], dyc, g_c128, rows["c128"], 0)
    d_bcv, small["conv_w"] = _conv_bwd(s["z_bcv"], sm["conv_w"], dby)
    delta = delta.reshape(N_DEV, SUBLANES, delta.shape[1])
    dq, dk, dv = _attn_bwd(s["q"], s["k"], s["v"], do, s["lse"], delta, N_DEV)
    dz_qkr, dqp, dkv, small["q_norm"], small["kv_norm"] = _mla_prep_bwd(
        dq, dk, dv, s["z_qkr"], s["rq"], s["rkv"], sm["q_norm"], sm["kv_norm"], cs, w["c256"])
    g_c256 = _tn_heads(s["qn"], s["kvn"], dqp, dkv)
    un = s["un"]
    g_win = _win_merge(_tn_plain(un, d_bcv), _tn_plain(un, dz_qkr[None])[0], _tn_plain(un, dgg[None])[0],
                       w["win"].shape[2])
    dh1, small["mix_norm"] = _mix_in_bwd(d_bcv, dz_qkr, dgg, s["w_bcv"], s["w_qkr"], s["w_gg"], s["h1"], s["rm"],
                                         sm["mix_norm"], dh2)
    return dh1, dict(gu=g_gu, dn=g_dn, sq=g_sq, win=g_win, c128=g_c128, c256=g_c256), small


def _layer_bwd_first(dh1, part, small, s, w, sm, after):
    rows = _grad_rows(w)
    dgu1, dh0, small["ffn1_norm"] = _ffn_bwd(dh1, s["jac1"], w["gu1"], w["dn1"], s["h0"], s["r1"], sm["ffn1_norm"],
                                             after)
    g_dn = _tn_down(s["at1"], dh1, part["dn"], rows["dn"], 0)
    g_gu = _tn_slots(s["n1"], dgu1, part["gu"], rows["gu"], 0)
    return dh0, dict(part, gu=g_gu, dn=g_dn), small


def _mesh_pos():
    return lax.axis_index("x"), lax.axis_index("y"), lax.axis_index("c")


def _other_chips(x, y):
    return [(1 - x, y), (x, 1 - y), (1 - x, 1 - y)]


def _pack(arrs, flipped, width):
    L = arrs[0].shape[0]
    shapes = [a.shape[:0:-1] if f else a.shape[1:] for a, f in zip(arrs, flipped)]
    R = sum(r for r, _ in shapes)

    def body(*refs):
        o_ref = refs[-1]
        off = 0
        for a_ref, f, (r, c) in zip(refs[:-1], flipped, shapes):
            a = a_ref[0].T if f else a_ref[0]
            o_ref[0, off:off + r, 0:c] = a.astype(BF16)
            if c < width:
                o_ref[0, off:off + r, c:width] = jnp.zeros((r, width - c), BF16)
            off += r

    return _call(
        body, "pack", (L,),
        [pl.BlockSpec((1,) + a.shape[1:], lambda l: (l, 0, 0)) for a in arrs],
        pl.BlockSpec((1, R, width), lambda l: (l, 0, 0)),
        _sds((L, R, width), BF16),
    )(*arrs)


def _handshake(peers):
    barrier = pltpu.get_barrier_semaphore()
    for peer in peers:
        pl.semaphore_signal(barrier, inc=1, device_id=peer, device_id_type=MESH)
    pl.semaphore_wait(barrier, len(peers))


def _sequencer_call(body, name, out_types, sems, collective_id, operands):
    return pl.kernel(
        body, name=name, out_type=out_types,
        mesh=plsc.ScalarSubcoreMesh(axis_name="seq", num_cores=1),
        scratch_types=tuple(pltpu.SemaphoreType.DMA((k,)) for k in sems),
        compiler_params=pltpu.CompilerParams(collective_id=collective_id),
    )(*operands)


def _all_gather(packs, l, after, collective_id):
    n = len(packs)

    def body(*refs):
        ins, outs = refs[:n], refs[n + len(after):2 * n + len(after)]
        send_sems, recv_sems, local_sems = refs[2 * n + len(after):]
        x, y, c = _mesh_pos()
        me, sibling = (x, y, c), (x, y, 1 - c)
        chips = _other_chips(x, y)
        _handshake([sibling] + [(*chip, c) for chip in chips])

        def copy(q, k, block, to, src=None):
            slot = outs[q].at[4 * block[0] + 2 * block[1] + block[2]]
            return pltpu.make_async_remote_copy(
                src_ref=slot if src is None else src, dst_ref=slot,
                send_sem=send_sems.at[7 * q + k], recv_sem=recv_sems.at[7 * q + k], device_id=to, device_id_type=MESH)

        started = []
        for q in range(n):
            src = ins[q].at[l]
            mine = pltpu.make_async_copy(src, outs[q].at[4 * x + 2 * y + c], local_sems.at[q])
            mine.start()
            started.append(mine)
        sends = []
        for q in range(n):
            src = ins[q].at[l]
            sends.append(copy(q, 0, me, sibling, src=src))
            sends += [copy(q, 1 + j, me, (*chip, c), src=src) for j, chip in enumerate(chips)]
        for cp in sends:
            cp.start()
        for q in range(n):
            for j, chip in enumerate(chips):
                copy(q, 1 + j, (*chip, c), me).wait_recv()
                fwd = copy(q, 4 + j, (*chip, c), sibling)
                fwd.start()
                sends.append(fwd)
        for q in range(n):
            copy(q, 0, sibling, me).wait_recv()
            for j, chip in enumerate(chips):
                copy(q, 4 + j, (*chip, 1 - c), me).wait_recv()
        for cp in sends:
            cp.wait_send()
        for mine in started:
            mine.wait()

    return _sequencer_call(
        body, f"all_gather_{collective_id}", [_sds((N_DEV,) + p.shape[1:], p.dtype) for p in packs], (7 * n, 7 * n, n),
        collective_id, list(packs) + list(after))


def _rs_d2d(gs, l, collective_id):
    n = len(gs)

    def body(*refs):
        ins, outs = refs[:n], refs[n:2 * n]
        send_sems, recv_sems = refs[2 * n:]
        x, y, c = _mesh_pos()
        _handshake([(x, y, 1 - c)])
        copies = []
        for q in range(n):
            for j in range(4):
                copies.append(pltpu.make_async_remote_copy(
                    src_ref=ins[q].at[2 * j + (1 - c)], dst_ref=outs[q].at[j], send_sem=send_sems.at[4 * q + j],
                    recv_sem=recv_sems.at[4 * q + j], device_id=(x, y, 1 - c), device_id_type=MESH))
        for cp in copies:
            cp.start()
        for cp in copies:
            cp.wait()

    return _sequencer_call(
        body, f"rs_d2d_{l}", [_sds((4,) + g.shape[1:], g.dtype) for g in gs], (4 * n, 4 * n), collective_id, gs)


def _rs_add_chip(gs, as_, after):
    n = len(gs)
    steps = 4
    tiles = [g.shape[1] // steps for g in gs]

    def chip(k):
        x, y, _ = _mesh_pos()
        return ([(x, y)] + _other_chips(x, y))[k]

    def body(*refs):
        g_refs, a_refs = refs[:4 * n], refs[4 * n:8 * n]
        own_refs, t_refs = refs[8 * n + len(after):9 * n + len(after)], refs[9 * n + len(after):]
        for q in range(n):
            g, a = g_refs[4 * q:4 * q + 4], a_refs[4 * q:4 * q + 4]
            own_refs[q][...] = g[0][0].astype(F32) + a[0][0].astype(F32)
            for k in range(1, 4):
                t_refs[q][k - 1] = (g[k][0].astype(F32) + a[k][0].astype(F32)).astype(BF16)

    def gspec(q, k):
        def index(i):
            px, py = chip(k)
            return 4 * px + 2 * py + lax.axis_index("c"), i, 0
        return pl.BlockSpec((1, tiles[q], gs[q].shape[2]), index)

    def aspec(q, k):
        def index(i):
            px, py = chip(k)
            return 2 * px + py, i, 0
        return pl.BlockSpec((1, tiles[q], gs[q].shape[2]), index)

    in_specs = [gspec(q, k) for q in range(n) for k in range(4)] + [aspec(q, k) for q in range(n) for k in range(4)]
    operands = [g for g in gs for _ in range(4)] + [a for a in as_ for _ in range(4)]
    out_specs = [pl.BlockSpec((tiles[q], gs[q].shape[2]), lambda i: (i, 0)) for q in range(n)]
    out_specs += [pl.BlockSpec((3, tiles[q], gs[q].shape[2]), lambda i: (0, i, 0)) for q in range(n)]
    out_shape = [_sds(g.shape[1:], F32) for g in gs] + [_sds((3,) + g.shape[1:], BF16) for g in gs]
    res = _call(body, "rs_add_chip", (steps,), in_specs + [ANY] * len(after), out_specs, out_shape)(*operands, *after)
    return res[:n], res[n:]


def _rs_ici(ts, l, collective_id):
    n = len(ts)

    def body(*refs):
        ins, outs = refs[:n], refs[n:2 * n]
        send_sems, recv_sems = refs[2 * n:]
        x, y, c = _mesh_pos()
        chips = _other_chips(x, y)
        _handshake([(*chip, c) for chip in chips])
        copies = []
        for q in range(n):
            for k, chip in enumerate(chips):
                copies.append(pltpu.make_async_remote_copy(
                    src_ref=ins[q].at[k], dst_ref=outs[q].at[k], send_sem=send_sems.at[3 * q + k],
                    recv_sem=recv_sems.at[3 * q + k], device_id=(*chip, c), device_id_type=MESH))
        for cp in copies:
            cp.start()
        for cp in copies:
            cp.wait()

    return _sequencer_call(
        body, f"rs_ici_{l}", [_sds(t.shape, t.dtype) for t in ts], (3 * n, 3 * n), collective_id, ts)


def _all_reduce_small(v):
    n, W = v.shape

    def body(v_ref, out_ref, slots, send_sems, recv_sems):
        x, y, c = _mesh_pos()
        me = 4 * x + 2 * y + c
        slots[me] = v_ref[...]
        copies = []
        for k in range(1, N_DEV):
            kx, ky, kc = (k >> 2) & 1, (k >> 1) & 1, k & 1
            peer = (1 - x if kx else x, 1 - y if ky else y, 1 - c if kc else c)
            copies.append(pltpu.make_async_remote_copy(
                src_ref=v_ref, dst_ref=slots.at[me], send_sem=send_sems.at[k - 1], recv_sem=recv_sems.at[k - 1],
                device_id=peer, device_id_type=MESH))
        for cp in copies:
            cp.start()
        for cp in copies:
            cp.wait()
        acc = slots[0]
        for d in range(1, N_DEV):
            acc = acc + slots[d]
        out_ref[...] = acc

    vm = pl.BlockSpec(memory_space=pltpu.VMEM)
    return pl.pallas_call(
        body, name="all_reduce_small",
        out_shape=_sds((n, W), F32),
        in_specs=[vm], out_specs=vm,
        scratch_shapes=[pltpu.VMEM((N_DEV, n, W), F32), pltpu.SemaphoreType.DMA((7,)), pltpu.SemaphoreType.DMA((7,))],
    )(v)


def _adamw_math(w, g, m, v):
    m2 = ADAM_B1 * m + (1.0 - ADAM_B1) * g
    v2 = ADAM_B2 * v + (1.0 - ADAM_B2) * (g * g)
    m_hat = m2 / (1.0 - ADAM_B1 ** ADAM_STEP)
    v_hat = v2 / (1.0 - ADAM_B2 ** ADAM_STEP)
    return -ADAM_LR * (m_hat / (jnp.sqrt(v_hat) + ADAM_EPS) + ADAM_WD * w), m2, v2


def _adamw(w, g, m, v):
    L, r, c = w.shape
    tr = _tile(r, max(SUBLANES, (256 * 1024 // c) // SUBLANES * SUBLANES), SUBLANES)

    def body(w_ref, g_ref, m_ref, v_ref, d_ref, nm_ref, nv_ref):
        d_ref[...], nm_ref[...], nv_ref[...] = _adamw_math(w_ref[...], g_ref[...], m_ref[...], v_ref[...])

    spec = pl.BlockSpec((1, tr, c), lambda l, i: (l, i, 0))
    return _call(body, "adamw", (L, r // tr), [spec] * 4, [spec] * 3, [_sds((L, r, c), F32)] * 3)(w, g, m, v)


def _adamw_reduced(w, m, v, flipped, own, b, row_off, tr, l, prev, after):
    L = w.shape[0]
    c, r = w.shape[1:] if flipped else w.shape[:0:-1]
    W = own.shape[1]
    ob = row_off // tr
    extra = list(prev or ()) + list(after)

    def body(w_ref, m_ref, v_ref, own_ref, b_ref, *rest):
        g_ref, d_ref, nm_ref, nv_ref = rest[len(extra):]
        g = ((own_ref[...] + b_ref[0].astype(F32)) + b_ref[1].astype(F32)) + b_ref[2].astype(F32)
        g = g[:, :c].T if flipped else g[:, :c]
        g_ref[0] = g
        d_ref[0], nm_ref[0], nv_ref[0] = _adamw_math(w_ref[0], g, m_ref[0], v_ref[0])

    spec = pl.BlockSpec((1, c, tr), lambda i: (l, 0, i)) if flipped else pl.BlockSpec((1, tr, c), lambda i: (l, i, 0))
    return _call(
        body, "adamw_reduced", (r // tr,),
        [spec] * 3 + [pl.BlockSpec((tr, W), lambda i: (ob + i, 0)), pl.BlockSpec((3, tr, W), lambda i: (0, ob + i, 0))]
        + [ANY] * len(extra),
        [spec] * 4, [_sds(w.shape, F32)] * 4,
        aliases={5 + k: k for k in range(4)} if prev else None,
    )(w, m, v, own, b, *extra)


_MEMBERS = dict(gu=("ffn1_w_gu", "ffn2_w_gu"), dn=("ffn1_w_down", "ffn2_w_down"),
                sq=("w_mla_out", "w_o", "w_ple_gate"), win=("w_in",), c128=("w_conv_out", "w_ple_proj"),
                c256=("w_ukv", "w_uq"))
_GATHER_MEMBERS = dict(_MEMBERS, gu1=("ffn1_w_gu",), gu2=("ffn2_w_gu",), dn1=("ffn1_w_down",), dn2=("ffn2_w_down",))
GATHER_STAGES = (("gu1", "dn1"), ("win", "c256", "c128", "sq"), ("gu2", "dn2"))
_FLIPPED = ("ffn1_w_gu", "ffn2_w_gu", "w_in", "w_uq")
_SMALL = ("ffn1_norm", "mix_norm", "q_norm", "kv_norm", "ffn2_norm", "ple_norm")
_ORDER = ("ffn1_norm", "ffn1_w_gu", "ffn1_w_down", "mix_norm", "w_in", "conv_w", "w_conv_out", "q_norm", "kv_norm",
          "w_uq", "w_ukv", "w_mla_out", "w_o", "ffn2_norm", "ffn2_w_gu", "ffn2_w_down", "ple_norm", "w_ple_gate",
          "w_ple_proj", "final_norm")


def _class_width(wts, cls):
    return HEAD_SLOT if cls == "c256" else wts[_GATHER_MEMBERS[cls][0]].shape[2]


def _pack_rows(vecs, width):
    flat = jnp.concatenate([a.reshape(-1) for a in vecs])
    n = flat.shape[0]
    rows = -(-n // width)
    rows = -(-rows // SUBLANES) * SUBLANES
    flat = jnp.pad(flat, (0, rows * width - n))
    offs, o = [], 0
    for a in vecs:
        offs.append(o)
        o += a.size
    return flat.reshape(rows, width), offs


def _unpack_rows(packed, vecs, offs):
    flat = packed.reshape(-1)
    return [flat[o:o + a.size].reshape(a.shape) for a, o in zip(vecs, offs)]


def _train(x, p, positions, target, gathered, packs, small_w, final_norm, update, reduce_small=None):
    cs = _rope_tables(positions)
    L = len(small_w)
    h = x
    saved = []
    def gather(l, names, after, collective_id):
        got = _all_gather([packs[n] for n in names], l, after, collective_id)
        return dict(zip(names, got))

    late = None
    if packs is not None:
        first, mixer, second = GATHER_STAGES
        w0 = gather(0, first, [], 0)
        w0.update(gather(0, mixer, [w0[first[0]]], 1))
        gathered = [w0]
        late = lambda h1: gather(0, second, [h1], 2)
    everything = sum(GATHER_STAGES, ())
    for l in range(L):
        h, s = _layer_fwd(h, p[l], cs, gathered[l], small_w[l], late)
        late = None
        saved.append(s)
        if packs is not None and l + 1 < L:
            gathered.append(gather(l + 1, everything, [s["by"]], 2 + l + 1))
    dh, loss, d_final = _final_loss(h, final_norm, target)
    grads, smalls = [None] * L, [None] * L
    exchanged = None
    landing = None

    def second_stage(after):
        l, gs, as_ = exchanged
        owns, ts = _rs_add_chip(gs, as_, [after])
        return l, owns, _rs_ici(ts, l, 2 * L + 2 + l)

    for l in reversed(range(L)):
        dh, part, small = _layer_bwd_late(dh, saved[l], gathered[l], small_w[l], [])
        pin = []
        if exchanged is not None:
            landing = second_stage(dh)
            pin = [landing[1][0]]
        dh, part, small = _layer_bwd_mixer(dh, part, small, saved[l], cs, gathered[l], small_w[l], pin)
        pin = [update(*landing)] if exchanged is not None else []
        dh, g, smalls[l] = _layer_bwd_first(dh, part, small, saved[l], gathered[l], small_w[l], pin)
        if update is not None:
            gs = [g[cls] for cls in CLASSES]
            exchanged = (l, gs, _rs_d2d(gs, l, L + 2 + l))
        else:
            grads[l] = g
    if update is not None:
        update(*second_stage(reduce_small(smalls, d_final, loss[0, 0])))
    return loss[0, 0], dh, grads, smalls, d_final


def kernel(x, p, positions, ffn1_norm, ffn1_w_gu, ffn1_w_down, mix_norm, w_in, conv_w, w_conv_out, q_norm, kv_norm, w_uq, w_ukv, w_mla_out, w_o, ffn2_norm, ffn2_w_gu, ffn2_w_down, ple_norm, w_ple_gate, w_ple_proj, final_norm, loss_target, m_ffn1_norm, m_ffn1_w_gu, m_ffn1_w_down, m_mix_norm, m_w_in, m_conv_w, m_w_conv_out, m_q_norm, m_kv_norm, m_w_uq, m_w_ukv, m_w_mla_out, m_w_o, m_ffn2_norm, m_ffn2_w_gu, m_ffn2_w_down, m_ple_norm, m_w_ple_gate, m_w_ple_proj, m_final_norm, v_ffn1_norm, v_ffn1_w_gu, v_ffn1_w_down, v_mix_norm, v_w_in, v_conv_w, v_w_conv_out, v_q_norm, v_kv_norm, v_w_uq, v_w_ukv, v_w_mla_out, v_w_o, v_ffn2_norm, v_ffn2_w_gu, v_ffn2_w_down, v_ple_norm, v_w_ple_gate, v_w_ple_proj, v_final_norm):
    args = dict(locals())
    wts = {n: args[n] for n in _ORDER}
    L = w_in.shape[0]
    dev = 4 * lax.axis_index("x") + 2 * lax.axis_index("y") + lax.axis_index("c")

    view = lambda n, a: jnp.swapaxes(a, 1, 2) if n in _FLIPPED else a
    packs = {cls: _pack([view(n, wts[n]) for n in _GATHER_MEMBERS[cls]], [n in _FLIPPED for n in _GATHER_MEMBERS[cls]],
                        _class_width(wts, cls))
             for stage in GATHER_STAGES for cls in stage}
    cw = conv_w.shape[2]
    conv_full = lax.dynamic_update_slice(jnp.zeros((L, 3, N_DEV * cw), F32), conv_w, (0, 0, dev * cw))
    conv_packed, conv_offs = _pack_rows([conv_full], FLAT_COLS)
    conv_full = _unpack_rows(_all_reduce_small(conv_packed), [conv_full], conv_offs)[0]
    small_w = [dict({n: wts[n][l][None, :] for n in _SMALL}, conv_w=conv_full[l]) for l in range(L)]

    done = {}

    def update(l, owns, bs):
        for q, cls in enumerate(CLASSES):
            off = 0
            rows = [wts[n].shape[1] for n in _MEMBERS[cls]]
            tr = _tile(math.gcd(*rows), 256, BF16_ROWS)
            for n, r in zip(_MEMBERS[cls], rows):
                done[n] = _adamw_reduced(view(n, wts[n]), view(n, args["m_" + n]), view(n, args["v_" + n]),
                                         n in _FLIPPED, owns[q], bs[q], off, tr, l, done.get(n), [])
                off += r
        return done[_MEMBERS[CLASSES[-1]][-1]][0]

    reduced = {}

    def reduce_small(smalls, d_final, loss_dev):
        small = [jnp.stack([smalls[l][n][0] for l in range(L)]) for n in _SMALL]
        small += [jnp.stack([smalls[l]["conv_w"] for l in range(L)]), d_final[0], loss_dev[None]]
        packed, offs = _pack_rows(small, FLAT_COLS)
        total = _all_reduce_small(packed)
        reduced["small"] = _unpack_rows(total, small, offs)
        return total

    _, grad_x, _, _, _ = _train(x[0], p[:, 0], positions[0], loss_target[0], None, packs, small_w,
                                final_norm[None, :], update, reduce_small)

    small = reduced["small"]
    grad = dict(zip(_SMALL, small))
    grad["conv_w"] = lax.dynamic_slice(small[len(_SMALL)], (0, 0, dev * cw), (L, 3, cw))
    grad["final_norm"] = small[-2]
    loss = small[-1][0]

    deltas, new_m, new_v = {}, {}, {}
    for n, outs in done.items():
        grad[n], deltas[n], new_m[n], new_v[n] = (view(n, a) for a in outs)
    for n in _SMALL + ("conv_w", "final_norm"):
        w3 = wts[n].reshape((1,) * (3 - wts[n].ndim) + wts[n].shape)
        d, nm, nv = _adamw(w3, grad[n].reshape(w3.shape), args["m_" + n].reshape(w3.shape),
                           args["v_" + n].reshape(w3.shape))
        deltas[n], new_m[n], new_v[n] = (a.reshape(wts[n].shape) for a in (d, nm, nv))
    return (loss, grad_x[None], *[grad[n] for n in _ORDER], *[deltas[n] for n in _ORDER],
            *[new_m[n] for n in _ORDER], *[new_v[n] for n in _ORDER])
```

```python
import functools
import math

import jax
import jax.numpy as jnp
from jax import lax
from jax.experimental import pallas as pl
from jax.experimental.pallas import tpu as pltpu
from jax.experimental.pallas import tpu_sc as plsc

F32 = jnp.float32
BF16 = jnp.bfloat16

CHUNK = 64
NOPE = 128
ROPE = 64
VDIM = 128
ROPE_THETA = 10000.0
EPS = 1e-6
ATTN_SCALE = (NOPE + ROPE) ** -0.5
SCORE_SCALE = ATTN_SCALE * math.log2(math.e)
LN2 = math.log(2.0)
ADAM_LR = 0.001
ADAM_B1 = 0.9
ADAM_B2 = 0.999
ADAM_EPS = 1e-08
ADAM_WD = 0.01
ADAM_STEP = 10

LANES = 128
SUBLANES = 8
BF16_ROWS = 16
V7X_VMEM_BYTES = 64 * 1024 * 1024
VMEM_LIMIT = V7X_VMEM_BYTES * 7 // 8
HEAD_SLOT = 2 * LANES
N_DEV = 8
ATTN_FWD_WIDTH = 8
ATTN_BWD_WIDTH = 4
FLAT_COLS = 1024
CLASSES = ("gu", "dn", "sq", "win", "c128", "c256")

NT = (((1,), (1,)), ((), ()))
MESH = pl.DeviceIdType.MESH
ANY = pl.BlockSpec(memory_space=pl.ANY)


def _dot(a, b):
    return jnp.dot(a, b, preferred_element_type=F32)


def _dot_nt(a, b):
    return lax.dot_general(a, b, NT, preferred_element_type=F32)


def _sig(x):
    return 1.0 / (1.0 + jnp.exp(-x))


def _tile(n, pref, unit):
    if n <= pref:
        return n
    t = (pref // unit) * unit
    while t >= unit:
        if n % t == 0:
            return t
        t -= unit
    return n


def _call(body, name, grid, in_specs, out_specs, out_shape, scratch=(), aliases=None):
    return pl.pallas_call(
        body,
        name=name,
        grid=grid,
        in_specs=in_specs,
        out_specs=out_specs,
        out_shape=out_shape,
        scratch_shapes=list(scratch),
        input_output_aliases=aliases or {},
        compiler_params=pltpu.CompilerParams(
            dimension_semantics=("arbitrary",) * len(grid), vmem_limit_bytes=VMEM_LIMIT
        ),
    )


def _sds(shape, dtype):
    return jax.ShapeDtypeStruct(shape, dtype)


def _rms_fwd(x, gain):
    rstd = lax.rsqrt(jnp.mean(x * x, axis=-1, keepdims=True) + EPS)
    return x * rstd * gain, rstd


def _rms_bwd(dn, x, rstd, gain):
    xhat = x * rstd
    dgy = dn * gain
    dx = rstd * (dgy - xhat * jnp.mean(dgy * xhat, axis=-1, keepdims=True))
    return dx, jnp.sum(dn * xhat, axis=0, keepdims=True)


def _rows(tm, w):
    return pl.BlockSpec((tm, w), lambda i: (i, 0))


def _whole(a):
    nd = a.ndim
    return pl.BlockSpec(a.shape, lambda i: (0,) * nd, pipeline_mode=pl.Buffered(1))


def _slab(buf, rows, index):
    return pl.BlockSpec((N_DEV, rows, buf.shape[2]), lambda i: (0, index, 0), pipeline_mode=pl.Buffered(1))


def _cat_slots(w):
    return jnp.concatenate([w[d] for d in range(N_DEV)], axis=1)


def _down_weight(w_ref, d, c):
    return w_ref[2 * d:2 * d + 2].reshape(c, w_ref.shape[2])


def _ffn_fwd(h, gain, gu_w, dn_w):
    S, D = h.shape
    c = gu_w.shape[2]
    tm = _tile(S, 256, SUBLANES)
    nb = N_DEV // 2

    def body(h_ref, gain_ref, w_ref, wd_ref, o_ref, jac_ref, at_ref, n_ref, r_ref):
        x = h_ref[...]
        n32, rstd = _rms_fwd(x, gain_ref[...])
        n = n32.astype(BF16)
        n_ref[...] = n.T
        r_ref[...] = rstd
        acc = jnp.zeros((tm, D), F32)
        for d in range(nb):
            g = _dot(n, w_ref[d])
            u = _dot(n, w_ref[nb + d])
            sg = _sig(g)
            silu = g * sg
            a = (silu * u).astype(BF16)
            at_ref[d] = a.T
            jac_ref[d] = (0.5 * u * (sg + silu * (1.0 - sg))).astype(BF16)
            jac_ref[nb + d] = (0.5 * silu).astype(BF16)
            acc = acc + _dot(a, _down_weight(wd_ref, d, c))
        o_ref[...] = x + 0.5 * acc

    return _call(
        body, "ffn_fwd", (S // tm,),
        [_rows(tm, D), _whole(gain), _slab(gu_w, D, 0), _slab(dn_w, c // 2, 0)],
        [_rows(tm, D), pl.BlockSpec((N_DEV, tm, c), lambda i: (0, i, 0)),
         pl.BlockSpec((nb, c, tm), lambda i: (0, 0, i)), pl.BlockSpec((D, tm), lambda i: (0, i)), _rows(tm, 1)],
        [_sds((S, D), F32), _sds((N_DEV, S, c), BF16), _sds((nb, c, S), BF16), _sds((D, S), BF16),
         _sds((S, 1), F32)],
    )(h, gain, gu_w, dn_w)


def _win_segments(C, QL, KVL, D):
    o1, o2 = 3 * C, 3 * C + QL + KVL + ROPE
    return [("bcv", k, k * C, (k + 1) * C) for k in range(3)] + [("qkr", None, o1, o2), ("gg", None, o2, o2 + 2 * D)]


def _win_pieces(segments, cw):
    out = []
    for tgt, lead, a, b in segments:
        for d in range(N_DEV):
            lo, hi = max(a, d * cw), min(b, (d + 1) * cw)
            if lo < hi:
                out.append((tgt, lead, d, (lo - d * cw, hi - d * cw), (lo - a, hi - a)))
    return out


def _win_split(win_w, C, QL, KVL):
    _, D, cw = win_w.shape
    WQ = QL + KVL + LANES
    pieces = _win_pieces(_win_segments(C, QL, KVL, D), cw)
    tr = _tile(D, 256, BF16_ROWS)

    def body(w_ref, bcv_ref, qkr_ref, gg_ref):
        tgt = dict(bcv=bcv_ref, qkr=qkr_ref, gg=gg_ref)
        qkr_ref[:, QL + KVL + ROPE:] = jnp.zeros((tr, LANES - ROPE), BF16)
        for name, lead, d, (s0, s1), (t0, t1) in pieces:
            v = w_ref[d, :, s0:s1]
            if lead is None:
                tgt[name][:, t0:t1] = v
            else:
                tgt[name][lead, :, t0:t1] = v

    return _call(
        body, "win_split", (D // tr,),
        [pl.BlockSpec((N_DEV, tr, cw), lambda i: (0, i, 0))],
        [pl.BlockSpec((3, tr, C), lambda i: (0, i, 0)), _rows(tr, WQ), _rows(tr, 2 * D)],
        [_sds((3, D, C), BF16), _sds((D, WQ), BF16), _sds((D, 2 * D), BF16)],
    )(win_w)


def _win_merge(d_bcv, d_qkr, d_gg, cw):
    _, D, C = d_bcv.shape
    WQ = d_qkr.shape[1]
    QL_KVL = WQ - LANES
    o1 = 3 * C
    segments = [("bcv", k, k * C, (k + 1) * C) for k in range(3)]
    segments += [("qkr", None, o1, o1 + QL_KVL + ROPE), ("gg", None, o1 + QL_KVL + ROPE, o1 + QL_KVL + ROPE + 2 * D)]
    pieces = _win_pieces(segments, cw)
    tr = _tile(D, 256, BF16_ROWS)

    def body(bcv_ref, qkr_ref, gg_ref, o_ref):
        src = dict(bcv=bcv_ref, qkr=qkr_ref, gg=gg_ref)
        for name, lead, d, (s0, s1), (t0, t1) in pieces:
            v = src[name][:, t0:t1] if lead is None else src[name][lead, :, t0:t1]
            o_ref[d, :, s0:s1] = v.astype(BF16)

    return _call(
        body, "win_merge", (D // tr,),
        [pl.BlockSpec((3, tr, C), lambda i: (0, i, 0)), _rows(tr, WQ), _rows(tr, 2 * D)],
        pl.BlockSpec((N_DEV, tr, cw), lambda i: (0, i, 0)),
        _sds((N_DEV, D, cw), BF16),
    )(d_bcv, d_qkr, d_gg)


def _mix_in(h, gain, w_bcv, w_qkr, w_gg):
    S, D = h.shape
    C = w_bcv.shape[2]
    tm = _tile(S, 512, SUBLANES)

    def body(h_ref, gain_ref, w1, w2, w3, o1, o2, o3, n_ref, r_ref):
        n32, rstd = _rms_fwd(h_ref[...], gain_ref[...])
        n = n32.astype(BF16)
        n_ref[...] = n.T
        r_ref[...] = rstd
        for k in range(3):
            o1[k] = _dot(n, w1[k]).astype(BF16)
        o2[...] = _dot(n, w2[...])
        o3[...] = _dot(n, w3[...]).astype(BF16)

    return _call(
        body, "mix_in", (S // tm,),
        [_rows(tm, D), _whole(gain), _whole(w_bcv), _whole(w_qkr), _whole(w_gg)],
        [pl.BlockSpec((3, tm, C), lambda i: (0, i, 0)), _rows(tm, w_qkr.shape[1]), _rows(tm, 2 * D),
         pl.BlockSpec((D, tm), lambda i: (0, i)), _rows(tm, 1)],
        [_sds((3, S, C), BF16), _sds((S, w_qkr.shape[1]), F32), _sds((S, 2 * D), BF16), _sds((D, S), BF16),
         _sds((S, 1), F32)],
    )(h, gain, w_bcv, w_qkr, w_gg)


def _conv_taps(zc):
    rows = lax.broadcasted_iota(jnp.int32, zc.shape, 0)
    z1 = jnp.where(rows >= 1, pltpu.roll(zc, 1, 0), 0.0)
    z2 = jnp.where(rows >= 2, pltpu.roll(zc, 2, 0), 0.0)
    return z1, z2


def _conv_fwd(z_bcv, conv_w):
    _, S, C = z_bcv.shape

    def body(z_ref, w_ref, o_ref):
        w = w_ref[...]
        zc = z_ref[1].astype(F32) * z_ref[2].astype(F32)
        z1, z2 = _conv_taps(zc)
        y = w[0:1] * z2 + w[1:2] * z1 + w[2:3] * zc
        o_ref[...] = (z_ref[0].astype(F32) * y).astype(BF16)

    return _call(
        body, "conv_fwd", (C // LANES,),
        [pl.BlockSpec((3, S, LANES), lambda j: (0, 0, j)), pl.BlockSpec((3, LANES), lambda j: (0, j))],
        pl.BlockSpec((S, LANES), lambda j: (0, j)),
        _sds((S, C), BF16),
    )(z_bcv, conv_w)


def _rope(x, cs, half):
    c, s1, s2 = cs[:, :LANES], cs[:, LANES:2 * LANES], cs[:, 2 * LANES:]
    return x * c + pltpu.roll(x, LANES - half, 1) * s1 + pltpu.roll(x, half, 1) * s2


def _unrope(d, cs, half):
    c, s1, s2 = cs[:, :LANES], cs[:, LANES:2 * LANES], cs[:, 2 * LANES:]
    return d * c + pltpu.roll(d * s1, half, 1) + pltpu.roll(d * s2, LANES - half, 1)


def _mla_prep(z_qkr, gq, gkv, cs, c256_w):
    S = z_qkr.shape[0]
    QL, KVL = gq.shape[1], gkv.shape[1]
    H = N_DEV
    tm = _tile(S, 512, SUBLANES)
    half = ROPE // 2

    def body(z_ref, gq_ref, gkv_ref, cs_ref, w_ref, q_ref, k_ref, v_ref, qn_ref, kvn_ref, rq_ref, rkv_ref):
        z = z_ref[...]
        cs_t = cs_ref[...]
        qn32, rq = _rms_fwd(z[:, :QL], gq_ref[...])
        kvn32, rkv = _rms_fwd(z[:, QL:QL + KVL], gkv_ref[...])
        qn = qn32.astype(BF16)
        kvn = kvn32.astype(BF16)
        qn_ref[...] = qn.T
        kvn_ref[...] = kvn.T
        rq_ref[...] = rq
        rkv_ref[...] = rkv
        krope = _rope(z[:, QL + KVL:], cs_t, half).astype(BF16)
        for h in range(H):
            lo, mid, hi = h * HEAD_SLOT, h * HEAD_SLOT + LANES, (h + 1) * HEAD_SLOT
            q = _dot(qn, w_ref[h, KVL:KVL + QL, :])
            kv = _dot(kvn, w_ref[h, 0:KVL, :])
            q_ref[:, lo:mid] = (q[:, :LANES] * SCORE_SCALE).astype(BF16)
            q_ref[:, mid:hi] = (_rope(q[:, LANES:], cs_t, half) * SCORE_SCALE).astype(BF16)
            k_ref[:, lo:mid] = kv[:, :LANES].astype(BF16)
            k_ref[:, mid:hi] = krope
            v_ref[:, h * VDIM:(h + 1) * VDIM] = kv[:, LANES:].astype(BF16)

    return _call(
        body, "mla_prep", (S // tm,),
        [_rows(tm, z_qkr.shape[1]), _whole(gq), _whole(gkv), _rows(tm, 3 * LANES), _whole(c256_w)],
        [_rows(tm, H * HEAD_SLOT), _rows(tm, H * HEAD_SLOT), _rows(tm, H * VDIM),
         pl.BlockSpec((QL, tm), lambda i: (0, i)), pl.BlockSpec((KVL, tm), lambda i: (0, i)),
         _rows(tm, 1), _rows(tm, 1)],
        [_sds((S, H * HEAD_SLOT), BF16), _sds((S, H * HEAD_SLOT), BF16), _sds((S, H * VDIM), BF16),
         _sds((QL, S), BF16), _sds((KVL, S), BF16), _sds((S, 1), F32), _sds((S, 1), F32)],
    )(z_qkr, gq, gkv, cs, c256_w)


def _chunk_mask(rows, cols, diagonal_row):
    shift = CHUNK.bit_length() - 1
    krow = (lax.broadcasted_iota(jnp.int32, (rows, cols), 0) - diagonal_row) >> shift
    qcol = lax.broadcasted_iota(jnp.int32, (rows, cols), 1) >> shift
    return krow <= qcol


def _attn_fwd(q, k, v, H):
    S = q.shape[0]
    t = _tile(S, 512, CHUNK)
    nq = S // t

    def body(q_ref, k_ref, v_ref, o_ref, ot_ref, lse_ref, vt_ref):
        qi = pl.program_id(1)

        @pl.when(qi == 0)
        def _():
            vt_ref[0:VDIM, :] = v_ref[...].T
            vt_ref[VDIM:, :] = jnp.ones((BF16_ROWS, S), BF16)

        qv = q_ref[...]

        def block(start, width, carry, masked):
            m, acc = carry
            off = pl.multiple_of(start * t, t)
            s = _dot_nt(k_ref[pl.ds(off, width * t), :], qv)
            if masked:
                s = jnp.where(_chunk_mask(width * t, t, (width - 1) * t), s, -1e30)
            m_new = jnp.maximum(m, jnp.max(s, axis=0, keepdims=True))
            p = jnp.exp2(s - m_new).astype(BF16)
            acc = jnp.exp2(m - m_new) * acc + _dot(vt_ref[:, pl.ds(off, width * t)], p)
            return m_new, acc

        init = (jnp.full((1, t), -1e30, F32), jnp.zeros((VDIM + BF16_ROWS, t), F32))
        wide = lax.div(qi, ATTN_FWD_WIDTH)
        carry = lax.fori_loop(0, wide, lambda j, c: block(j * ATTN_FWD_WIDTH, ATTN_FWD_WIDTH, c, False), init)
        left = qi - wide * ATTN_FWD_WIDTH
        for extra in range(ATTN_FWD_WIDTH):
            @pl.when(left == extra)
            def _():
                m, acc = block(qi - extra, extra + 1, carry, True)
                l = acc[VDIM:VDIM + 1]
                out = (acc[0:VDIM] * (1.0 / l)).astype(BF16)
                ot_ref[...] = out
                o_ref[...] = out.T
                lse_ref[0] = jnp.broadcast_to(m + jnp.log2(l), (SUBLANES, t))

    return _call(
        body, "attn_fwd", (H, nq),
        [pl.BlockSpec((t, HEAD_SLOT), lambda h, i: (i, h)), pl.BlockSpec((S, HEAD_SLOT), lambda h, i: (0, h)),
         pl.BlockSpec((S, VDIM), lambda h, i: (0, h))],
        [pl.BlockSpec((t, VDIM), lambda h, i: (i, h)), pl.BlockSpec((VDIM, t), lambda h, i: (h, i)),
         pl.BlockSpec((1, SUBLANES, t), lambda h, i: (h, 0, i))],
        [_sds((S, H * VDIM), BF16), _sds((H * VDIM, S), BF16), _sds((H, SUBLANES, S), F32)],
        [pltpu.VMEM((VDIM + BF16_ROWS, S), BF16)],
    )(q, k, v)


def _merge_wo(o, by, z_gg, h, sq_w, c128_w):
    S, D = h.shape
    C = by.shape[1]
    r = sq_w.shape[1] // 3
    tm = _tile(S, 512, SUBLANES)

    def body(o_ref, by_ref, gg_ref, h_ref, wmo_ref, wo_ref, wco_ref, h2_ref, mg_ref, yc_ref, ym_ref):
        ymla = _dot(o_ref[...], wmo_ref[...].reshape(N_DEV * r, D))
        yconv = _dot(by_ref[...], _cat_slots(wco_ref))
        gg = gg_ref[...].astype(F32)
        merged = (_sig(gg[:, :D]) * yconv + _sig(gg[:, D:]) * ymla).astype(BF16)
        mg_ref[...] = merged.T
        yc_ref[...] = yconv.astype(BF16)
        ym_ref[...] = ymla.astype(BF16)
        h2_ref[...] = h_ref[...] + _dot(merged, wo_ref[...].reshape(N_DEV * r, D))

    return _call(
        body, "merge_wo", (S // tm,),
        [_rows(tm, o.shape[1]), _rows(tm, C), _rows(tm, 2 * D), _rows(tm, D), _slab(sq_w, r, 0), _slab(sq_w, r, 1),
         _slab(c128_w, C, 0)],
        [_rows(tm, D), pl.BlockSpec((D, tm), lambda i: (0, i)), _rows(tm, D), _rows(tm, D)],
        [_sds((S, D), F32), _sds((D, S), BF16), _sds((S, D), BF16), _sds((S, D), BF16)],
    )(o, by, z_gg, h, sq_w, sq_w, c128_w)


def _ple_fwd(h, gain, p, sq_w, c128_w, C):
    S, D = h.shape
    P = p.shape[1]
    r = sq_w.shape[1] // 3
    tm = _tile(S, 512, SUBLANES)

    def body(h_ref, gain_ref, p_ref, wpg_ref, wpp_ref, o_ref, pre_ref, pp_ref, n_ref, r_ref):
        x = h_ref[...]
        n32, rstd = _rms_fwd(x, gain_ref[...])
        n = n32.astype(BF16)
        n_ref[...] = n.T
        r_ref[...] = rstd
        pre = _dot(n, wpg_ref[...].reshape(N_DEV * r, D))
        pp = _dot(p_ref[...].astype(BF16), _cat_slots(wpp_ref))
        pre_ref[...] = pre.astype(BF16)
        pp_ref[...] = pp.astype(BF16)
        o_ref[...] = x + _sig(pre) * pp

    return _call(
        body, "ple_fwd", (S // tm,),
        [_rows(tm, D), _whole(gain), _rows(tm, P), _slab(sq_w, r, 2), _slab(c128_w, P, C // P)],
        [_rows(tm, D), _rows(tm, D), _rows(tm, D), pl.BlockSpec((D, tm), lambda i: (0, i)), _rows(tm, 1)],
        [_sds((S, D), F32), _sds((S, D), BF16), _sds((S, D), BF16), _sds((D, S), BF16), _sds((S, 1), F32)],
    )(h, gain, p, sq_w, c128_w)


def _final_loss(h, gain, target):
    S, D = h.shape
    tm = _tile(S, 512, SUBLANES)

    def body(h_ref, gain_ref, t_ref, dh_ref, loss_ref, dg_ref):
        @pl.when(pl.program_id(0) == 0)
        def _():
            loss_ref[...] = jnp.zeros_like(loss_ref)
            dg_ref[...] = jnp.zeros_like(dg_ref)

        x = h_ref[...]
        gain_v = gain_ref[...]
        y, rstd = _rms_fwd(x, gain_v)
        err = y - t_ref[...]
        loss_ref[...] += 0.5 * jnp.sum(jnp.mean(err * err, axis=-1, keepdims=True))
        dx, dgain = _rms_bwd(err * (1.0 / D), x, rstd, gain_v)
        dh_ref[...] = dx
        dg_ref[...] += dgain

    return _call(
        body, "final_loss", (S // tm,),
        [_rows(tm, D), _whole(gain), _rows(tm, D)],
        [_rows(tm, D), pl.BlockSpec((1, LANES), lambda i: (0, 0)), pl.BlockSpec((1, D), lambda i: (0, 0))],
        [_sds((S, D), F32), _sds((1, LANES), F32), _sds((1, D), F32)],
    )(h, gain, target)


def _tn_call(body, name, grid, in_specs, out_spec, out_shape, scratch, operands, prev):
    n = len(operands)
    if prev is None:
        return _call(body, name, grid, in_specs, out_spec, out_shape, scratch)(*operands)
    assert prev.shape == out_shape.shape and prev.dtype == out_shape.dtype

    def wrapped(*refs):
        body(*refs[:n], *refs[n + 1:])

    return _call(wrapped, name, grid, in_specs + [ANY], out_spec, out_shape, scratch, {n: 0})(*operands, prev)


def _tn_slots(xt, dy, prev, rows_total, row_off):
    K, S = xt.shape
    B, _, c = dy.shape
    tk = _tile(K, 1024, BF16_ROWS)

    def body(xt_ref, dy_ref, o_ref):
        o_ref[0] = _dot(xt_ref[...], dy_ref[0]).astype(BF16)

    return _tn_call(
        body, "tn_slots", (K // tk, B),
        [pl.BlockSpec((tk, S), lambda i, b: (i, 0)), pl.BlockSpec((1, S, c), lambda i, b: (b, 0, 0))],
        pl.BlockSpec((1, tk, c), lambda i, b: (b, row_off // tk + i, 0)),
        _sds((B, rows_total, c), BF16), [], [xt, dy], prev)


def _tn_plain(xt, dy):
    K, S = xt.shape
    B, _, c = dy.shape
    tk = _tile(K, 512, BF16_ROWS)
    tn = _tile(c, 1024, LANES)

    def body(xt_ref, dy_ref, o_ref):
        o_ref[0] = _dot(xt_ref[...], dy_ref[0])

    return _call(
        body, "tn_plain", (K // tk, B, c // tn),
        [pl.BlockSpec((tk, S), lambda i, b, j: (i, 0)), pl.BlockSpec((1, S, tn), lambda i, b, j: (b, 0, j))],
        pl.BlockSpec((1, tk, tn), lambda i, b, j: (b, i, j)),
        _sds((B, K, c), F32),
    )(xt, dy)


def _tn_down(at, dh, prev, rows_total, which):
    nb, c, S = at.shape
    D = dh.shape[1]
    r = c // 2
    tn = _tile(D, 512, LANES)

    def body(at_ref, dh_ref, o_ref):
        g = 0.5 * _dot(at_ref[0], dh_ref[...].astype(BF16))
        o_ref[...] = g.astype(BF16).reshape(2, r, tn)

    return _tn_call(
        body, "tn_down", (D // tn, nb),
        [pl.BlockSpec((1, c, S), lambda j, i: (i, 0, 0)), pl.BlockSpec((S, tn), lambda j, i: (0, j))],
        pl.BlockSpec((2, r, tn), lambda j, i: (i, which, j)),
        _sds((N_DEV, rows_total, D), BF16), [], [at, dh], prev)


def _tn_square(xt, dy, prev, rows_total, member):
    K, S = xt.shape
    N = dy.shape[1]
    r = K // N_DEV
    tk = _tile(K, 512, r)
    tn = _tile(N, 512, LANES)

    def body(xt_ref, dy_ref, o_ref):
        g = _dot(xt_ref[...], dy_ref[...].astype(BF16))
        o_ref[...] = g.astype(BF16).reshape(tk // r, r, tn)

    return _tn_call(
        body, "tn_square", (N // tn, K // tk),
        [pl.BlockSpec((tk, S), lambda j, i: (i, 0)), pl.BlockSpec((S, tn), lambda j, i: (0, j))],
        pl.BlockSpec((tk // r, r, tn), lambda j, i: (i, member, j)),
        _sds((N_DEV, rows_total, N), BF16), [], [xt, dy], prev)


def _tn_cols(x, dy, prev, rows_total, row_block):
    S, K = x.shape
    N = dy.shape[1]
    cw = N // N_DEV

    def body(x_ref, dy_ref, o_ref):
        g = _dot(x_ref[...].astype(BF16).T, dy_ref[...])
        for d in range(N_DEV):
            o_ref[d] = g[:, d * cw:(d + 1) * cw].astype(BF16)

    return _tn_call(
        body, "tn_cols", (1,),
        [pl.BlockSpec((S, K), lambda i: (0, 0)), pl.BlockSpec((S, N), lambda i: (0, 0))],
        pl.BlockSpec((N_DEV, K, cw), lambda i: (0, row_block, 0)),
        _sds((N_DEV, rows_total, cw), BF16), [], [x, dy], prev)


def _tn_heads(qnt, kvnt, dqp, dkv):
    QL, S = qnt.shape
    KVL = kvnt.shape[0]

    def body(qn_ref, kvn_ref, dq_ref, dkv_ref, o_ref):
        o_ref[0, 0:KVL, :] = _dot(kvn_ref[...], dkv_ref[...]).astype(BF16)
        o_ref[0, KVL:KVL + QL, :] = _dot(qn_ref[...], dq_ref[...]).astype(BF16)

    head = pl.BlockSpec((S, HEAD_SLOT), lambda h: (0, h))
    return _call(
        body, "tn_heads", (N_DEV,),
        [pl.BlockSpec((QL, S), lambda h: (0, 0)), pl.BlockSpec((KVL, S), lambda h: (0, 0)), head, head],
        pl.BlockSpec((1, KVL + QL, HEAD_SLOT), lambda h: (h, 0, 0)),
        _sds((N_DEV, KVL + QL, HEAD_SLOT), BF16),
    )(qnt, kvnt, dqp, dkv)


def _ple_bwd(dh, pre, pp, h, rstd, gain, sq_w, after):
    S, D = h.shape
    r = sq_w.shape[1] // 3
    tm = _tile(S, 512, SUBLANES)

    def body(dh_ref, pre_ref, pp_ref, h_ref, r_ref, gain_ref, wpg_ref, *rest):
        o_ref, dpre_ref, dpp_ref, dg_ref = rest[len(after):]

        @pl.when(pl.program_id(0) == 0)
        def _():
            dg_ref[...] = jnp.zeros_like(dg_ref)

        d = dh_ref[...]
        gate = _sig(pre_ref[...].astype(F32))
        dpre = (d * pp_ref[...].astype(F32) * gate * (1.0 - gate)).astype(BF16)
        dpre_ref[...] = dpre
        dpp_ref[...] = (d * gate).astype(BF16)
        dn = _dot_nt(dpre, wpg_ref[...].reshape(N_DEV * r, D))
        dx, dgain = _rms_bwd(dn, h_ref[...], r_ref[...], gain_ref[...])
        o_ref[...] = d + dx
        dg_ref[...] += dgain

    return _call(
        body, "ple_bwd", (S // tm,),
        [_rows(tm, D), _rows(tm, D), _rows(tm, D), _rows(tm, D), _rows(tm, 1), _whole(gain), _slab(sq_w, r, 2)]
        + [ANY] * len(after),
        [_rows(tm, D), _rows(tm, D), _rows(tm, D), pl.BlockSpec((1, D), lambda i: (0, 0))],
        [_sds((S, D), F32), _sds((S, D), BF16), _sds((S, D), BF16), _sds((1, D), F32)],
    )(dh, pre, pp, h, rstd, gain, sq_w, *after)


def _ffn_bwd(dh, jac, gu_w, dn_w, h, rstd, gain, after=()):
    S, D = h.shape
    _, _, c = jac.shape
    nb = N_DEV // 2
    tm = _tile(S, 256, SUBLANES)

    def body(dh_ref, jac_ref, w_ref, wd_ref, h_ref, r_ref, gain_ref, *rest):
        dgu_ref, o_ref, dgain_ref = rest[len(after):]

        @pl.when(pl.program_id(0) == 0)
        def _():
            dgain_ref[...] = jnp.zeros_like(dgain_ref)

        dh_v = dh_ref[...]
        dhb = dh_v.astype(BF16)
        dn = jnp.zeros((tm, D), F32)
        for d in range(nb):
            da = _dot_nt(dhb, _down_weight(wd_ref, d, c))
            dg = (da * jac_ref[d].astype(F32)).astype(BF16)
            du = (da * jac_ref[nb + d].astype(F32)).astype(BF16)
            dgu_ref[d] = dg
            dgu_ref[nb + d] = du
            dn = dn + _dot_nt(dg, w_ref[d]) + _dot_nt(du, w_ref[nb + d])
        dx, dgain = _rms_bwd(dn, h_ref[...], r_ref[...], gain_ref[...])
        o_ref[...] = dh_v + dx
        dgain_ref[...] += dgain

    act = pl.BlockSpec((N_DEV, tm, c), lambda i: (0, i, 0))
    return _call(
        body, "ffn_bwd", (S // tm,),
        [_rows(tm, D), act, _slab(gu_w, D, 0), _slab(dn_w, c // 2, 0), _rows(tm, D), _rows(tm, 1), _whole(gain)]
        + [ANY] * len(after),
        [act, _rows(tm, D), pl.BlockSpec((1, D), lambda i: (0, 0))],
        [_sds((N_DEV, S, c), BF16), _sds((S, D), F32), _sds((1, D), F32)],
    )(dh, jac, gu_w, dn_w, h, rstd, gain, *after)


def _merge_bwd(dh, z_gg, yconv, ymla, o, sq_w, c128_w, C, after):
    S, D = dh.shape
    r = sq_w.shape[1] // 3
    HV = N_DEV * r
    H = HV // VDIM
    tm = _tile(S, 512, SUBLANES)

    def head_rows():
        row = lax.broadcasted_iota(jnp.int32, (SUBLANES * H, HV), 0) >> (SUBLANES.bit_length() - 1)
        col = lax.broadcasted_iota(jnp.int32, (SUBLANES * H, HV), 1) >> (VDIM.bit_length() - 1)
        return jnp.where(row == col, 1.0, 0.0).astype(BF16)

    def body(dh_ref, gg_ref, yc_ref, ym_ref, o_ref, wmo_ref, wo_ref, wco_ref, *rest):
        dgg_ref, dby_ref, do_ref, dyc_ref, dym_ref, dl_ref = rest[len(after):]
        dm = _dot_nt(dh_ref[...].astype(BF16), wo_ref[...].reshape(HV, D))
        gg = gg_ref[...].astype(F32)
        sgc = _sig(gg[:, :D])
        sgm = _sig(gg[:, D:])
        dyc = (dm * sgc).astype(BF16)
        dym = (dm * sgm).astype(BF16)
        dyc_ref[...] = dyc
        dym_ref[...] = dym
        dgg_ref[:, :D] = (dm * yc_ref[...].astype(F32) * sgc * (1.0 - sgc)).astype(BF16)
        dgg_ref[:, D:] = (dm * ym_ref[...].astype(F32) * sgm * (1.0 - sgm)).astype(BF16)
        dby_ref[...] = _dot_nt(dyc, _cat_slots(wco_ref)).astype(BF16)
        do = _dot_nt(dym, wmo_ref[...].reshape(HV, D)).astype(BF16)
        do_ref[...] = do
        prod = do.astype(F32) * o_ref[...].astype(F32)
        hi = prod.astype(BF16)
        lo = (prod - hi.astype(F32)).astype(BF16)
        pick = head_rows()
        dl_ref[...] = _dot_nt(pick, hi) + _dot_nt(pick, lo)

    return _call(
        body, "merge_bwd", (S // tm,),
        [_rows(tm, D), _rows(tm, 2 * D), _rows(tm, D), _rows(tm, D), _rows(tm, HV), _slab(sq_w, r, 0),
         _slab(sq_w, r, 1), _slab(c128_w, C, 0)] + [ANY] * len(after),
        [_rows(tm, 2 * D), _rows(tm, C), _rows(tm, HV), _rows(tm, D), _rows(tm, D),
         pl.BlockSpec((SUBLANES * H, tm), lambda i: (0, i))],
        [_sds((S, 2 * D), BF16), _sds((S, C), BF16), _sds((S, HV), BF16), _sds((S, D), BF16), _sds((S, D), BF16),
         _sds((SUBLANES * H, S), F32)],
    )(dh, z_gg, yconv, ymla, o, sq_w, sq_w, c128_w, *after)


def _conv_bwd(z_bcv, conv_w, dby):
    _, S, C = z_bcv.shape

    def body(z_ref, w_ref, dby_ref, dz_ref, dw_ref):
        w = w_ref[...]
        c = z_ref[1].astype(F32)
        v = z_ref[2].astype(F32)
        d = dby_ref[...].astype(F32)
        zc = c * v
        z1, z2 = _conv_taps(zc)
        y = w[0:1] * z2 + w[1:2] * z1 + w[2:3] * zc
        dz_ref[0] = (d * y).astype(BF16)
        dy = d * z_ref[0].astype(F32)
        rows = lax.broadcasted_iota(jnp.int32, dy.shape, 0)
        dy1 = jnp.where(rows < S - 1, pltpu.roll(dy, S - 1, 0), 0.0)
        dy2 = jnp.where(rows < S - 2, pltpu.roll(dy, S - 2, 0), 0.0)
        dzc = w[2:3] * dy + w[1:2] * dy1 + w[0:1] * dy2
        dz_ref[1] = (dzc * v).astype(BF16)
        dz_ref[2] = (dzc * c).astype(BF16)
        dw_ref[0:1, :] = jnp.sum(dy * z2, axis=0, keepdims=True)
        dw_ref[1:2, :] = jnp.sum(dy * z1, axis=0, keepdims=True)
        dw_ref[2:3, :] = jnp.sum(dy * zc, axis=0, keepdims=True)

    three = pl.BlockSpec((3, S, LANES), lambda j: (0, 0, j))
    wspec = pl.BlockSpec((3, LANES), lambda j: (0, j))
    return _call(
        body, "conv_bwd", (C // LANES,),
        [three, wspec, pl.BlockSpec((S, LANES), lambda j: (0, j))],
        [three, wspec],
        [_sds((3, S, C), BF16), _sds((3, C), F32)],
    )(z_bcv, conv_w, dby)


def _attn_bwd(q, k, v, do, lse, delta, H):
    S = q.shape[0]
    t = _tile(S, 512, CHUNK)
    nk = S // t

    def body(q_ref, k_ref, v_ref, do_ref, lse_ref, dl_ref, dq_ref, dk_ref, dv_ref, dqt_ref):
        kj = pl.program_id(1)

        @pl.when(kj == 0)
        def _():
            dqt_ref[...] = jnp.zeros_like(dqt_ref)

        kv = k_ref[...]
        vv = v_ref[...]
        kt = kv.T

        def block(start, width, carry, masked):
            dk, dv = carry
            off = pl.multiple_of(start * t, t)
            qv = q_ref[pl.ds(off, width * t), :]
            dov = do_ref[pl.ds(off, width * t), :]
            s = _dot_nt(kv, qv)
            if masked:
                s = jnp.where(_chunk_mask(t, width * t, 0), s, -1e30)
            p = jnp.exp2(s - lse_ref[0, 0:1, pl.ds(off, width * t)])
            dp = _dot_nt(vv, dov)
            ds = (p * (dp - dl_ref[0, 0:1, pl.ds(off, width * t)]) * LN2).astype(BF16)
            dqt_ref[:, pl.ds(off, width * t)] += _dot(kt, ds)
            return dk + _dot(ds, qv), dv + _dot(p.astype(BF16), dov)

        init = (jnp.zeros((t, HEAD_SLOT), F32), jnp.zeros((t, VDIM), F32))
        wide = lax.div(nk - 1 - kj, ATTN_BWD_WIDTH)
        left = nk - 1 - kj - wide * ATTN_BWD_WIDTH
        carry = lax.switch(left, [functools.partial(block, kj, extra + 1, init, True)
                                  for extra in range(ATTN_BWD_WIDTH)])
        dk, dv = lax.fori_loop(
            0, wide, lambda j, c: block(kj + 1 + left + j * ATTN_BWD_WIDTH, ATTN_BWD_WIDTH, c, False), carry)
        dk_ref[...] = dk.astype(BF16)
        dv_ref[...] = dv.astype(BF16)

        @pl.when(kj == nk - 1)
        def _():
            dq_ref[...] = (dqt_ref[...] * SCORE_SCALE).T.astype(BF16)

    kspec = lambda w: pl.BlockSpec((t, w), lambda h, j: (j, h))
    qspec = lambda w: pl.BlockSpec((S, w), lambda h, j: (0, h))
    stat = pl.BlockSpec((1, SUBLANES, S), lambda h, j: (h, 0, 0))
    return _call(
        body, "attn_bwd", (H, nk),
        [qspec(HEAD_SLOT), kspec(HEAD_SLOT), kspec(VDIM), qspec(VDIM), stat, stat],
        [qspec(HEAD_SLOT), kspec(HEAD_SLOT), kspec(VDIM)],
        [_sds((S, H * HEAD_SLOT), BF16), _sds((S, H * HEAD_SLOT), BF16), _sds((S, H * VDIM), BF16)],
        [pltpu.VMEM((HEAD_SLOT, S), F32)],
    )(q, k, v, do, lse, delta)


def _mla_prep_bwd(dq, dk, dv, z_qkr, rq, rkv, gq, gkv, cs, c256_w):
    S = z_qkr.shape[0]
    QL, KVL = gq.shape[1], gkv.shape[1]
    H = N_DEV
    tm = _tile(S, 512, SUBLANES)
    half = ROPE // 2

    def body(dq_ref, dk_ref, dv_ref, z_ref, rq_ref, rkv_ref, gq_ref, gkv_ref, cs_ref, w_ref,
             dz_ref, dqp_ref, dkv_ref, dgq_ref, dgkv_ref):
        @pl.when(pl.program_id(0) == 0)
        def _():
            dgq_ref[...] = jnp.zeros_like(dgq_ref)
            dgkv_ref[...] = jnp.zeros_like(dgkv_ref)

        cs_t = cs_ref[...]
        dkr = jnp.zeros((tm, LANES), F32)
        dqn = jnp.zeros((tm, QL), F32)
        dkvn = jnp.zeros((tm, KVL), F32)
        for h in range(H):
            lo, mid, hi = h * HEAD_SLOT, h * HEAD_SLOT + LANES, (h + 1) * HEAD_SLOT
            dqp_ref[:, lo:mid] = dq_ref[:, lo:mid]
            dqp_ref[:, mid:hi] = _unrope(dq_ref[:, mid:hi].astype(F32), cs_t, half).astype(BF16)
            dkv_ref[:, lo:mid] = dk_ref[:, lo:mid]
            dkv_ref[:, mid:hi] = dv_ref[:, h * VDIM:(h + 1) * VDIM]
            dkr = dkr + dk_ref[:, mid:hi].astype(F32)
            dqn = dqn + _dot_nt(dqp_ref[:, lo:hi], w_ref[h, KVL:KVL + QL, :])
            dkvn = dkvn + _dot_nt(dkv_ref[:, lo:hi], w_ref[h, 0:KVL, :])
        z = z_ref[...]
        dqc, dgq = _rms_bwd(dqn, z[:, :QL], rq_ref[...], gq_ref[...])
        dkvc, dgkv = _rms_bwd(dkvn, z[:, QL:QL + KVL], rkv_ref[...], gkv_ref[...])
        dz_ref[:, :QL] = dqc.astype(BF16)
        dz_ref[:, QL:QL + KVL] = dkvc.astype(BF16)
        dz_ref[:, QL + KVL:] = _unrope(dkr, cs_t, half).astype(BF16)
        dgq_ref[...] += dgq
        dgkv_ref[...] += dgkv

    W = z_qkr.shape[1]
    return _call(
        body, "mla_prep_bwd", (S // tm,),
        [_rows(tm, H * HEAD_SLOT), _rows(tm, H * HEAD_SLOT), _rows(tm, H * VDIM), _rows(tm, W), _rows(tm, 1),
         _rows(tm, 1), _whole(gq), _whole(gkv), _rows(tm, 3 * LANES), _whole(c256_w)],
        [_rows(tm, W), _rows(tm, H * HEAD_SLOT), _rows(tm, H * HEAD_SLOT), _whole(gq), _whole(gkv)],
        [_sds((S, W), BF16), _sds((S, H * HEAD_SLOT), BF16), _sds((S, H * HEAD_SLOT), BF16),
         _sds((1, QL), F32), _sds((1, KVL), F32)],
    )(dq, dk, dv, z_qkr, rq, rkv, gq, gkv, cs, c256_w)


def _mix_in_bwd(d_bcv, dz_qkr, dgg, w_bcv, w_qkr, w_gg, h, rstd, gain, dh):
    S, D = h.shape
    C = d_bcv.shape[2]
    tm = _tile(S, 512, SUBLANES)

    def body(db_ref, dq_ref, dgg_ref, wb_ref, wq_ref, wg_ref, h_ref, r_ref, gain_ref, dh_ref, o_ref, dgain_ref):
        @pl.when(pl.program_id(0) == 0)
        def _():
            dgain_ref[...] = jnp.zeros_like(dgain_ref)

        dn = _dot_nt(dq_ref[...], wq_ref[...]) + _dot_nt(dgg_ref[...], wg_ref[...])
        for k in range(3):
            dn = dn + _dot_nt(db_ref[k], wb_ref[k])
        dx, dgain = _rms_bwd(dn, h_ref[...], r_ref[...], gain_ref[...])
        o_ref[...] = dh_ref[...] + dx
        dgain_ref[...] += dgain

    return _call(
        body, "mix_in_bwd", (S // tm,),
        [pl.BlockSpec((3, tm, C), lambda i: (0, i, 0)), _rows(tm, dz_qkr.shape[1]), _rows(tm, dgg.shape[1]),
         _whole(w_bcv), _whole(w_qkr), _whole(w_gg), _rows(tm, D), _rows(tm, 1), _whole(gain), _rows(tm, D)],
        [_rows(tm, D), pl.BlockSpec((1, D), lambda i: (0, 0))],
        [_sds((S, D), F32), _sds((1, D), F32)],
    )(d_bcv, dz_qkr, dgg, w_bcv, w_qkr, w_gg, h, rstd, gain, dh)


def _rope_tables(positions):
    half = ROPE // 2
    inv_freq = ROPE_THETA ** (-jnp.arange(0, ROPE, 2, dtype=F32) / ROPE)
    ang = positions.astype(F32)[:, None] * inv_freq
    cos, sin = jnp.cos(ang), jnp.sin(ang)
    z = jnp.zeros_like(cos)
    pad = jnp.zeros((positions.shape[0], LANES - 2 * half), F32)
    return jnp.concatenate([cos, cos, pad, -sin, z, pad, z, sin, pad], axis=1)


def _grad_rows(w):
    return dict(gu=2 * w["gu1"].shape[1], dn=2 * w["dn1"].shape[1], sq=w["sq"].shape[1], win=w["win"].shape[1],
                c128=w["c128"].shape[1], c256=w["c256"].shape[1])


def _layer_fwd(h0, p_l, cs, w, sm, late):
    C = sm["conv_w"].shape[1]
    QL, KVL = sm["q_norm"].shape[1], sm["kv_norm"].shape[1]
    h1, jac1, at1, n1, r1 = _ffn_fwd(h0, sm["ffn1_norm"], w["gu1"], w["dn1"])
    if late is not None:
        w.update(late(h1))
    w_bcv, w_qkr, w_gg = _win_split(w["win"], C, QL, KVL)
    z_bcv, z_qkr, z_gg, un, rm = _mix_in(h1, sm["mix_norm"], w_bcv, w_qkr, w_gg)
    by = _conv_fwd(z_bcv, sm["conv_w"])
    q, k, v, qn, kvn, rq, rkv = _mla_prep(z_qkr, sm["q_norm"], sm["kv_norm"], cs, w["c256"])
    o, ot, lse = _attn_fwd(q, k, v, N_DEV)
    h2, merged, yconv, ymla = _merge_wo(o, by, z_gg, h1, w["sq"], w["c128"])
    h3, jac2, at2, n2, r2 = _ffn_fwd(h2, sm["ffn2_norm"], w["gu2"], w["dn2"])
    h4, pre, pp, pn, rp = _ple_fwd(h3, sm["ple_norm"], p_l, w["sq"], w["c128"], C)
    saved = dict(h0=h0, jac1=jac1, at1=at1, n1=n1, r1=r1, h1=h1, w_bcv=w_bcv, w_qkr=w_qkr, w_gg=w_gg, z_bcv=z_bcv,
                 z_qkr=z_qkr, z_gg=z_gg, un=un, rm=rm, by=by, q=q, k=k, v=v, qn=qn, kvn=kvn, rq=rq, rkv=rkv, o=o, ot=ot,
                 lse=lse, h2=h2, merged=merged, yconv=yconv, ymla=ymla, jac2=jac2, at2=at2, n2=n2, r2=r2, h3=h3,
                 pre=pre, pp=pp, pn=pn, rp=rp, p=p_l)
    return h4, saved


def _layer_bwd_late(dh4, s, w, sm, after):
    D = dh4.shape[1]
    C = sm["conv_w"].shape[1]
    P = s["p"].shape[1]
    rows = _grad_rows(w)
    small = {}
    dh3, dpre, dpp, small["ple_norm"] = _ple_bwd(dh4, s["pre"], s["pp"], s["h3"], s["rp"], sm["ple_norm"], w["sq"],
                                                 after)
    g_sq = _tn_square(s["pn"], dpre, None, rows["sq"], 2)
    g_c128 = _tn_cols(s["p"], dpp, None, rows["c128"], C // P)

    dgu2, dh2, small["ffn2_norm"] = _ffn_bwd(dh3, s["jac2"], w["gu2"], w["dn2"], s["h2"], s["r2"], sm["ffn2_norm"])
    g_dn = _tn_down(s["at2"], dh3, None, rows["dn"], 1)
    g_gu = _tn_slots(s["n2"], dgu2, None, rows["gu"], D)
    return dh2, dict(gu=g_gu, dn=g_dn, sq=g_sq, c128=g_c128), small


def _layer_bwd_mixer(dh2, part, small, s, cs, w, sm, after):
    C = sm["conv_w"].shape[1]
    rows = _grad_rows(w)
    g_gu, g_dn, g_sq, g_c128 = part["gu"], part["dn"], part["sq"], part["c128"]

    dgg, dby, do, dyc, dym, delta = _merge_bwd(dh2, s["z_gg"], s["yconv"], s["ymla"], s["o"], w["sq"], w["c128"], C,
                                               after)
    g_sq = _tn_square(s["merged"], dh2, g_sq, rows["sq"], 1)
    g_sq = _tn_square(s["ot"], dym, g_sq, rows["sq"], 0)
    g_c128 = _tn_cols(s["by"], dyc, g_c128, rows["c128"], 0)
    d_bcv, small["conv_w"] = _conv_bwd(s["z_bcv"], sm["conv_w"], dby)
    delta = delta.reshape(N_DEV, SUBLANES, delta.shape[1])
    dq, dk, dv = _attn_bwd(s["q"], s["k"], s["v"], do, s["lse"], delta, N_DEV)
    dz_qkr, dqp, dkv, small["q_norm"], small["kv_norm"] = _mla_prep_bwd(
        dq, dk, dv, s["z_qkr"], s["rq"], s["rkv"], sm["q_norm"], sm["kv_norm"], cs, w["c256"])
    g_c256 = _tn_heads(s["qn"], s["kvn"], dqp, dkv)
    un = s["un"]
    g_win = _win_merge(_tn_plain(un, d_bcv), _tn_plain(un, dz_qkr[None])[0], _tn_plain(un, dgg[None])[0],
                       w["win"].shape[2])
    dh1, small["mix_norm"] = _mix_in_bwd(d_bcv, dz_qkr, dgg, s["w_bcv"], s["w_qkr"], s["w_gg"], s["h1"], s["rm"],
                                         sm["mix_norm"], dh2)
    return dh1, dict(gu=g_gu, dn=g_dn, sq=g_sq, win=g_win, c128=g_c128, c256=g_c256), small


def _layer_bwd_first(dh1, part, small, s, w, sm, after):
    rows = _grad_rows(w)
    dgu1, dh0, small["ffn1_norm"] = _ffn_bwd(dh1, s["jac1"], w["gu1"], w["dn1"], s["h0"], s["r1"], sm["ffn1_norm"],
                                             after)
    g_dn = _tn_down(s["at1"], dh1, part["dn"], rows["dn"], 0)
    g_gu = _tn_slots(s["n1"], dgu1, part["gu"], rows["gu"], 0)
    return dh0, dict(part, gu=g_gu, dn=g_dn), small


def _mesh_pos():
    return lax.axis_index("x"), lax.axis_index("y"), lax.axis_index("c")


def _other_chips(x, y):
    return [(1 - x, y), (x, 1 - y), (1 - x, 1 - y)]


def _pack(arrs, flipped, width):
    L = arrs[0].shape[0]
    shapes = [a.shape[:0:-1] if f else a.shape[1:] for a, f in zip(arrs, flipped)]
    R = sum(r for r, _ in shapes)

    def body(*refs):
        o_ref = refs[-1]
        off = 0
        for a_ref, f, (r, c) in zip(refs[:-1], flipped, shapes):
            a = a_ref[0].T if f else a_ref[0]
            o_ref[0, off:off + r, 0:c] = a.astype(BF16)
            if c < width:
                o_ref[0, off:off + r, c:width] = jnp.zeros((r, width - c), BF16)
            off += r

    return _call(
        body, "pack", (L,),
        [pl.BlockSpec((1,) + a.shape[1:], lambda l: (l, 0, 0)) for a in arrs],
        pl.BlockSpec((1, R, width), lambda l: (l, 0, 0)),
        _sds((L, R, width), BF16),
    )(*arrs)


def _handshake(peers):
    barrier = pltpu.get_barrier_semaphore()
    for peer in peers:
        pl.semaphore_signal(barrier, inc=1, device_id=peer, device_id_type=MESH)
    pl.semaphore_wait(barrier, len(peers))


def _sequencer_call(body, name, out_types, sems, collective_id, operands):
    return pl.kernel(
        body, name=name, out_type=out_types,
        mesh=plsc.ScalarSubcoreMesh(axis_name="seq", num_cores=1),
        scratch_types=tuple(pltpu.SemaphoreType.DMA((k,)) for k in sems),
        compiler_params=pltpu.CompilerParams(collective_id=collective_id),
    )(*operands)


def _all_gather(packs, l, after, collective_id):
    n = len(packs)

    def body(*refs):
        ins, outs = refs[:n], refs[n + len(after):2 * n + len(after)]
        send_sems, recv_sems, local_sems = refs[2 * n + len(after):]
        x, y, c = _mesh_pos()
        me, sibling = (x, y, c), (x, y, 1 - c)
        chips = _other_chips(x, y)
        _handshake([sibling] + [(*chip, c) for chip in chips])

        def copy(q, k, block, to, src=None):
            slot = outs[q].at[4 * block[0] + 2 * block[1] + block[2]]
            return pltpu.make_async_remote_copy(
                src_ref=slot if src is None else src, dst_ref=slot,
                send_sem=send_sems.at[7 * q + k], recv_sem=recv_sems.at[7 * q + k], device_id=to, device_id_type=MESH)

        started = []
        for q in range(n):
            src = ins[q].at[l]
            mine = pltpu.make_async_copy(src, outs[q].at[4 * x + 2 * y + c], local_sems.at[q])
            mine.start()
            started.append(mine)
        sends = []
        for q in range(n):
            src = ins[q].at[l]
            sends.append(copy(q, 0, me, sibling, src=src))
            sends += [copy(q, 1 + j, me, (*chip, c), src=src) for j, chip in enumerate(chips)]
        for cp in sends:
            cp.start()
        for q in range(n):
            for j, chip in enumerate(chips):
                copy(q, 1 + j, (*chip, c), me).wait_recv()
                fwd = copy(q, 4 + j, (*chip, c), sibling)
                fwd.start()
                sends.append(fwd)
        for q in range(n):
            copy(q, 0, sibling, me).wait_recv()
            for j, chip in enumerate(chips):
                copy(q, 4 + j, (*chip, 1 - c), me).wait_recv()
        for cp in sends:
            cp.wait_send()
        for mine in started:
            mine.wait()

    return _sequencer_call(
        body, f"all_gather_{collective_id}", [_sds((N_DEV,) + p.shape[1:], p.dtype) for p in packs], (7 * n, 7 * n, n),
        collective_id, list(packs) + list(after))


def _rs_d2d(gs, l, collective_id):
    n = len(gs)

    def body(*refs):
        ins, outs = refs[:n], refs[n:2 * n]
        send_sems, recv_sems = refs[2 * n:]
        x, y, c = _mesh_pos()
        _handshake([(x, y, 1 - c)])
        copies = []
        for q in range(n):
            for j in range(4):
                copies.append(pltpu.make_async_remote_copy(
                    src_ref=ins[q].at[2 * j + (1 - c)], dst_ref=outs[q].at[j], send_sem=send_sems.at[4 * q + j],
                    recv_sem=recv_sems.at[4 * q + j], device_id=(x, y, 1 - c), device_id_type=MESH))
        for cp in copies:
            cp.start()
        for cp in copies:
            cp.wait()

    return _sequencer_call(
        body, f"rs_d2d_{l}", [_sds((4,) + g.shape[1:], g.dtype) for g in gs], (4 * n, 4 * n), collective_id, gs)


def _rs_add_chip(gs, as_, after):
    n = len(gs)
    steps = 4
    tiles = [g.shape[1] // steps for g in gs]

    def chip(k):
        x, y, _ = _mesh_pos()
        return ([(x, y)] + _other_chips(x, y))[k]

    def body(*refs):
        g_refs, a_refs = refs[:4 * n], refs[4 * n:8 * n]
        own_refs, t_refs = refs[8 * n + len(after):9 * n + len(after)], refs[9 * n + len(after):]
        for q in range(n):
            g, a = g_refs[4 * q:4 * q + 4], a_refs[4 * q:4 * q + 4]
            own_refs[q][...] = g[0][0].astype(F32) + a[0][0].astype(F32)
            for k in range(1, 4):
                t_refs[q][k - 1] = (g[k][0].astype(F32) + a[k][0].astype(F32)).astype(BF16)

    def gspec(q, k):
        def index(i):
            px, py = chip(k)
            return 4 * px + 2 * py + lax.axis_index("c"), i, 0
        return pl.BlockSpec((1, tiles[q], gs[q].shape[2]), index)

    def aspec(q, k):
        def index(i):
            px, py = chip(k)
            return 2 * px + py, i, 0
        return pl.BlockSpec((1, tiles[q], gs[q].shape[2]), index)

    in_specs = [gspec(q, k) for q in range(n) for k in range(4)] + [aspec(q, k) for q in range(n) for k in range(4)]
    operands = [g for g in gs for _ in range(4)] + [a for a in as_ for _ in range(4)]
    out_specs = [pl.BlockSpec((tiles[q], gs[q].shape[2]), lambda i: (i, 0)) for q in range(n)]
    out_specs += [pl.BlockSpec((3, tiles[q], gs[q].shape[2]), lambda i: (0, i, 0)) for q in range(n)]
    out_shape = [_sds(g.shape[1:], F32) for g in gs] + [_sds((3,) + g.shape[1:], BF16) for g in gs]
    res = _call(body, "rs_add_chip", (steps,), in_specs + [ANY] * len(after), out_specs, out_shape)(*operands, *after)
    return res[:n], res[n:]


def _rs_ici(ts, l, collective_id):
    n = len(ts)

    def body(*refs):
        ins, outs = refs[:n], refs[n:2 * n]
        send_sems, recv_sems = refs[2 * n:]
        x, y, c = _mesh_pos()
        chips = _other_chips(x, y)
        _handshake([(*chip, c) for chip in chips])
        copies = []
        for q in range(n):
            for k, chip in enumerate(chips):
                copies.append(pltpu.make_async_remote_copy(
                    src_ref=ins[q].at[k], dst_ref=outs[q].at[k], send_sem=send_sems.at[3 * q + k],
                    recv_sem=recv_sems.at[3 * q + k], device_id=(*chip, c), device_id_type=MESH))
        for cp in copies:
            cp.start()
        for cp in copies:
            cp.wait()

    return _sequencer_call(
        body, f"rs_ici_{l}", [_sds(t.shape, t.dtype) for t in ts], (3 * n, 3 * n), collective_id, ts)


def _all_reduce_small(v):
    n, W = v.shape

    def body(v_ref, out_ref, slots, send_sems, recv_sems):
        x, y, c = _mesh_pos()
        me = 4 * x + 2 * y + c
        slots[me] = v_ref[...]
        copies = []
        for k in range(1, N_DEV):
            kx, ky, kc = (k >> 2) & 1, (k >> 1) & 1, k & 1
            peer = (1 - x if kx else x, 1 - y if ky else y, 1 - c if kc else c)
            copies.append(pltpu.make_async_remote_copy(
                src_ref=v_ref, dst_ref=slots.at[me], send_sem=send_sems.at[k - 1], recv_sem=recv_sems.at[k - 1],
                device_id=peer, device_id_type=MESH))
        for cp in copies:
            cp.start()
        for cp in copies:
            cp.wait()
        acc = slots[0]
        for d in range(1, N_DEV):
            acc = acc + slots[d]
        out_ref[...] = acc

    vm = pl.BlockSpec(memory_space=pltpu.VMEM)
    return pl.pallas_call(
        body, name="all_reduce_small",
        out_shape=_sds((n, W), F32),
        in_specs=[vm], out_specs=vm,
        scratch_shapes=[pltpu.VMEM((N_DEV, n, W), F32), pltpu.SemaphoreType.DMA((7,)), pltpu.SemaphoreType.DMA((7,))],
    )(v)


def _adamw_math(w, g, m, v):
    m2 = ADAM_B1 * m + (1.0 - ADAM_B1) * g
    v2 = ADAM_B2 * v + (1.0 - ADAM_B2) * (g * g)
    m_hat = m2 / (1.0 - ADAM_B1 ** ADAM_STEP)
    v_hat = v2 / (1.0 - ADAM_B2 ** ADAM_STEP)
    return -ADAM_LR * (m_hat / (jnp.sqrt(v_hat) + ADAM_EPS) + ADAM_WD * w), m2, v2


def _adamw(w, g, m, v):
    L, r, c = w.shape
    tr = _tile(r, max(SUBLANES, (256 * 1024 // c) // SUBLANES * SUBLANES), SUBLANES)

    def body(w_ref, g_ref, m_ref, v_ref, d_ref, nm_ref, nv_ref):
        d_ref[...], nm_ref[...], nv_ref[...] = _adamw_math(w_ref[...], g_ref[...], m_ref[...], v_ref[...])

    spec = pl.BlockSpec((1, tr, c), lambda l, i: (l, i, 0))
    return _call(body, "adamw", (L, r // tr), [spec] * 4, [spec] * 3, [_sds((L, r, c), F32)] * 3)(w, g, m, v)


def _adamw_reduced(w, m, v, flipped, own, b, row_off, tr, l, prev, after):
    L = w.shape[0]
    c, r = w.shape[1:] if flipped else w.shape[:0:-1]
    W = own.shape[1]
    ob = row_off // tr
    extra = list(prev or ()) + list(after)

    def body(w_ref, m_ref, v_ref, own_ref, b_ref, *rest):
        g_ref, d_ref, nm_ref, nv_ref = rest[len(extra):]
        g = ((own_ref[...] + b_ref[0].astype(F32)) + b_ref[1].astype(F32)) + b_ref[2].astype(F32)
        g = g[:, :c].T if flipped else g[:, :c]
        g_ref[0] = g
        d_ref[0], nm_ref[0], nv_ref[0] = _adamw_math(w_ref[0], g, m_ref[0], v_ref[0])

    spec = pl.BlockSpec((1, c, tr), lambda i: (l, 0, i)) if flipped else pl.BlockSpec((1, tr, c), lambda i: (l, i, 0))
    return _call(
        body, "adamw_reduced", (r // tr,),
        [spec] * 3 + [pl.BlockSpec((tr, W), lambda i: (ob + i, 0)), pl.BlockSpec((3, tr, W), lambda i: (0, ob + i, 0))]
        + [ANY] * len(extra),
        [spec] * 4, [_sds(w.shape, F32)] * 4,
        aliases={5 + k: k for k in range(4)} if prev else None,
    )(w, m, v, own, b, *extra)


_MEMBERS = dict(gu=("ffn1_w_gu", "ffn2_w_gu"), dn=("ffn1_w_down", "ffn2_w_down"),
                sq=("w_mla_out", "w_o", "w_ple_gate"), win=("w_in",), c128=("w_conv_out", "w_ple_proj"),
                c256=("w_ukv", "w_uq"))
_GATHER_MEMBERS = dict(_MEMBERS, gu1=("ffn1_w_gu",), gu2=("ffn2_w_gu",), dn1=("ffn1_w_down",), dn2=("ffn2_w_down",))
GATHER_STAGES = (("gu1", "dn1"), ("win", "c256", "c128", "sq"), ("gu2", "dn2"))
_FLIPPED = ("ffn1_w_gu", "ffn2_w_gu", "w_in", "w_uq")
_SMALL = ("ffn1_norm", "mix_norm", "q_norm", "kv_norm", "ffn2_norm", "ple_norm")
_ORDER = ("ffn1_norm", "ffn1_w_gu", "ffn1_w_down", "mix_norm", "w_in", "conv_w", "w_conv_out", "q_norm", "kv_norm",
          "w_uq", "w_ukv", "w_mla_out", "w_o", "ffn2_norm", "ffn2_w_gu", "ffn2_w_down", "ple_norm", "w_ple_gate",
          "w_ple_proj", "final_norm")


def _class_width(wts, cls):
    return HEAD_SLOT if cls == "c256" else wts[_GATHER_MEMBERS[cls][0]].shape[2]


def _pack_rows(vecs, width):
    flat = jnp.concatenate([a.reshape(-1) for a in vecs])
    n = flat.shape[0]
    rows = -(-n // width)
    rows = -(-rows // SUBLANES) * SUBLANES
    flat = jnp.pad(flat, (0, rows * width - n))
    offs, o = [], 0
    for a in vecs:
        offs.append(o)
        o += a.size
    return flat.reshape(rows, width), offs


def _unpack_rows(packed, vecs, offs):
    flat = packed.reshape(-1)
    return [flat[o:o + a.size].reshape(a.shape) for a, o in zip(vecs, offs)]


def _train(x, p, positions, target, gathered, packs, small_w, final_norm, update, reduce_small=None):
    cs = _rope_tables(positions)
    L = len(small_w)
    h = x
    saved = []
    def gather(l, names, after, collective_id):
        got = _all_gather([packs[n] for n in names], l, after, collective_id)
        return dict(zip(names, got))

    late = None
    if packs is not None:
        first, mixer, second = GATHER_STAGES
        w0 = gather(0, first, [], 0)
        w0.update(gather(0, mixer, [w0[first[0]]], 1))
        gathered = [w0]
        late = lambda h1: gather(0, second, [h1], 2)
    everything = sum(GATHER_STAGES, ())
    for l in range(L):
        h, s = _layer_fwd(h, p[l], cs, gathered[l], small_w[l], late)
        late = None
        saved.append(s)
        if packs is not None and l + 1 < L:
            gathered.append(gather(l + 1, everything, [s["by"]], 2 + l + 1))
    dh, loss, d_final = _final_loss(h, final_norm, target)
    grads, smalls = [None] * L, [None] * L
    exchanged = None
    landing = None

    def second_stage(after):
        l, gs, as_ = exchanged
        owns, ts = _rs_add_chip(gs, as_, [after])
        return l, owns, _rs_ici(ts, l, 2 * L + 2 + l)

    for l in reversed(range(L)):
        dh, part, small = _layer_bwd_late(dh, saved[l], gathered[l], small_w[l], [])
        pin = []
        if exchanged is not None:
            landing = second_stage(dh)
            pin = [landing[1][0]]
        dh, part, small = _layer_bwd_mixer(dh, part, small, saved[l], cs, gathered[l], small_w[l], pin)
        pin = [update(*landing)] if exchanged is not None else []
        dh, g, smalls[l] = _layer_bwd_first(dh, part, small, saved[l], gathered[l], small_w[l], pin)
        if update is not None:
            gs = [g[cls] for cls in CLASSES]
            exchanged = (l, gs, _rs_d2d(gs, l, L + 2 + l))
        else:
            grads[l] = g
    if update is not None:
        update(*second_stage(reduce_small(smalls, d_final, loss[0, 0])))
    return loss[0, 0], dh, grads, smalls, d_final


def kernel(x, p, positions, ffn1_norm, ffn1_w_gu, ffn1_w_down, mix_norm, w_in, conv_w, w_conv_out, q_norm, kv_norm, w_uq, w_ukv, w_mla_out, w_o, ffn2_norm, ffn2_w_gu, ffn2_w_down, ple_norm, w_ple_gate, w_ple_proj, final_norm, loss_target, m_ffn1_norm, m_ffn1_w_gu, m_ffn1_w_down, m_mix_norm, m_w_in, m_conv_w, m_w_conv_out, m_q_norm, m_kv_norm, m_w_uq, m_w_ukv, m_w_mla_out, m_w_o, m_ffn2_norm, m_ffn2_w_gu, m_ffn2_w_down, m_ple_norm, m_w_ple_gate, m_w_ple_proj, m_final_norm, v_ffn1_norm, v_ffn1_w_gu, v_ffn1_w_down, v_mix_norm, v_w_in, v_conv_w, v_w_conv_out, v_q_norm, v_kv_norm, v_w_uq, v_w_ukv, v_w_mla_out, v_w_o, v_ffn2_norm, v_ffn2_w_gu, v_ffn2_w_down, v_ple_norm, v_w_ple_gate, v_w_ple_proj, v_final_norm):
    args = dict(locals())
    wts = {n: args[n] for n in _ORDER}
    L = w_in.shape[0]
    dev = 4 * lax.axis_index("x") + 2 * lax.axis_index("y") + lax.axis_index("c")

    view = lambda n, a: jnp.swapaxes(a, 1, 2) if n in _FLIPPED else a
    packs = {cls: _pack([view(n, wts[n]) for n in _GATHER_MEMBERS[cls]], [n in _FLIPPED for n in _GATHER_MEMBERS[cls]],
                        _class_width(wts, cls))
             for stage in GATHER_STAGES for cls in stage}
    cw = conv_w.shape[2]
    conv_full = lax.dynamic_update_slice(jnp.zeros((L, 3, N_DEV * cw), F32), conv_w, (0, 0, dev * cw))
    conv_packed, conv_offs = _pack_rows([conv_full], FLAT_COLS)
    conv_full = _unpack_rows(_all_reduce_small(conv_packed), [conv_full], conv_offs)[0]
    small_w = [dict({n: wts[n][l][None, :] for n in _SMALL}, conv_w=conv_full[l]) for l in range(L)]

    done = {}

    def update(l, owns, bs):
        for q, cls in enumerate(CLASSES):
            off = 0
            rows = [wts[n].shape[1] for n in _MEMBERS[cls]]
            tr = _tile(math.gcd(*rows), 256, BF16_ROWS)
            for n, r in zip(_MEMBERS[cls], rows):
                done[n] = _adamw_reduced(view(n, wts[n]), view(n, args["m_" + n]), view(n, args["v_" + n]),
                                         n in _FLIPPED, owns[q], bs[q], off, tr, l, done.get(n), [])
                off += r
        return done[_MEMBERS[CLASSES[-1]][-1]][0]

    reduced = {}

    def reduce_small(smalls, d_final, loss_dev):
        small = [jnp.stack([smalls[l][n][0] for l in range(L)]) for n in _SMALL]
        small += [jnp.stack([smalls[l]["conv_w"] for l in range(L)]), d_final[0], loss_dev[None]]
        packed, offs = _pack_rows(small, FLAT_COLS)
        total = _all_reduce_small(packed)
        reduced["small"] = _unpack_rows(total, small, offs)
        return total

    _, grad_x, _, _, _ = _train(x[0], p[:, 0], positions[0], loss_target[0], None, packs, small_w,
                                final_norm[None, :], update, reduce_small)

    small = reduced["small"]
    grad = dict(zip(_SMALL, small))
    grad["conv_w"] = lax.dynamic_slice(small[len(_SMALL)], (0, 0, dev * cw), (L, 3, cw))
    grad["final_norm"] = small[-2]
    loss = small[-1][0]

    deltas, new_m, new_v = {}, {}, {}
    for n, outs in done.items():
        grad[n], deltas[n], new_m[n], new_v[n] = (view(n, a) for a in outs)
    for n in _SMALL + ("conv_w", "final_norm"):
        w3 = wts[n].reshape((1,) * (3 - wts[n].ndim) + wts[n].shape)
        d, nm, nv = _adamw(w3, grad[n].reshape(w3.shape), args["m_" + n].reshape(w3.shape),
                           args["v_" + n].reshape(w3.shape))
        deltas[n], new_m[n], new_v[n] = (a.reshape(wts[n].shape) for a in (d, nm, nv))
    return (loss, grad_x[None], *[grad[n] for n in _ORDER], *[deltas[n] for n in _ORDER],
            *[new_m[n] for n in _ORDER], *[new_v[n] for n in _ORDER])
```

```python
import functools
import math

import jax
import jax.numpy as jnp
from jax import lax
from jax.experimental import pallas as pl
from jax.experimental.pallas import tpu as pltpu
from jax.experimental.pallas import tpu_sc as plsc

F32 = jnp.float32
BF16 = jnp.bfloat16

CHUNK = 64
NOPE = 128
ROPE = 64
VDIM = 128
ROPE_THETA = 10000.0
EPS = 1e-6
ATTN_SCALE = (NOPE + ROPE) ** -0.5
SCORE_SCALE = ATTN_SCALE * math.log2(math.e)
LN2 = math.log(2.0)
ADAM_LR = 0.001
ADAM_B1 = 0.9
ADAM_B2 = 0.999
ADAM_EPS = 1e-08
ADAM_WD = 0.01
ADAM_STEP = 10

LANES = 128
SUBLANES = 8
BF16_ROWS = 16
V7X_VMEM_BYTES = 64 * 1024 * 1024
VMEM_LIMIT = V7X_VMEM_BYTES * 7 // 8
HEAD_SLOT = 2 * LANES
N_DEV = 8
ATTN_FWD_WIDTH = 8
ATTN_BWD_WIDTH = 4
FLAT_COLS = 1024
CLASSES = ("gu", "dn", "sq", "win", "c128", "c256")

NT = (((1,), (1,)), ((), ()))
MESH = pl.DeviceIdType.MESH
ANY = pl.BlockSpec(memory_space=pl.ANY)


def _dot(a, b):
    return jnp.dot(a, b, preferred_element_type=F32)


def _dot_nt(a, b):
    return lax.dot_general(a, b, NT, preferred_element_type=F32)


def _sig(x):
    return 1.0 / (1.0 + jnp.exp(-x))


def _tile(n, pref, unit):
    if n <= pref:
        return n
    t = (pref // unit) * unit
    while t >= unit:
        if n % t == 0:
            return t
        t -= unit
    return n


def _call(body, name, grid, in_specs, out_specs, out_shape, scratch=(), aliases=None):
    return pl.pallas_call(
        body,
        name=name,
        grid=grid,
        in_specs=in_specs,
        out_specs=out_specs,
        out_shape=out_shape,
        scratch_shapes=list(scratch),
        input_output_aliases=aliases or {},
        compiler_params=pltpu.CompilerParams(
            dimension_semantics=("arbitrary",) * len(grid), vmem_limit_bytes=VMEM_LIMIT
        ),
    )


def _sds(shape, dtype):
    return jax.ShapeDtypeStruct(shape, dtype)


def _rms_fwd(x, gain):
    rstd = lax.rsqrt(jnp.mean(x * x, axis=-1, keepdims=True) + EPS)
    return x * rstd * gain, rstd


def _rms_bwd(dn, x, rstd, gain):
    xhat = x * rstd
    dgy = dn * gain
    dx = rstd * (dgy - xhat * jnp.mean(dgy * xhat, axis=-1, keepdims=True))
    return dx, jnp.sum(dn * xhat, axis=0, keepdims=True)


def _rows(tm, w):
    return pl.BlockSpec((tm, w), lambda i: (i, 0))


def _whole(a):
    nd = a.ndim
    return pl.BlockSpec(a.shape, lambda i: (0,) * nd, pipeline_mode=pl.Buffered(1))


def _slab(buf, rows, index):
    return pl.BlockSpec((N_DEV, rows, buf.shape[2]), lambda i: (0, index, 0), pipeline_mode=pl.Buffered(1))


def _cat_slots(w):
    return jnp.concatenate([w[d] for d in range(N_DEV)], axis=1)


def _down_weight(w_ref, d, c):
    return w_ref[2 * d:2 * d + 2].reshape(c, w_ref.shape[2])


def _ffn_fwd(h, gain, gu_w, dn_w):
    S, D = h.shape
    c = gu_w.shape[2]
    tm = _tile(S, 256, SUBLANES)
    nb = N_DEV // 2

    def body(h_ref, gain_ref, w_ref, wd_ref, o_ref, jac_ref, at_ref, n_ref, r_ref):
        x = h_ref[...]
        n32, rstd = _rms_fwd(x, gain_ref[...])
        n = n32.astype(BF16)
        n_ref[...] = n.T
        r_ref[...] = rstd
        acc = jnp.zeros((tm, D), F32)
        for d in range(nb):
            g = _dot(n, w_ref[d])
            u = _dot(n, w_ref[nb + d])
            sg = _sig(g)
            silu = g * sg
            a = (silu * u).astype(BF16)
            at_ref[d] = a.T
            jac_ref[d] = (0.5 * u * (sg + silu * (1.0 - sg))).astype(BF16)
            jac_ref[nb + d] = (0.5 * silu).astype(BF16)
            acc = acc + _dot(a, _down_weight(wd_ref, d, c))
        o_ref[...] = x + 0.5 * acc

    return _call(
        body, "ffn_fwd", (S // tm,),
        [_rows(tm, D), _whole(gain), _slab(gu_w, D, 0), _slab(dn_w, c // 2, 0)],
        [_rows(tm, D), pl.BlockSpec((N_DEV, tm, c), lambda i: (0, i, 0)),
         pl.BlockSpec((nb, c, tm), lambda i: (0, 0, i)), pl.BlockSpec((D, tm), lambda i: (0, i)), _rows(tm, 1)],
        [_sds((S, D), F32), _sds((N_DEV, S, c), BF16), _sds((nb, c, S), BF16), _sds((D, S), BF16),
         _sds((S, 1), F32)],
    )(h, gain, gu_w, dn_w)


def _win_segments(C, QL, KVL, D):
    o1, o2 = 3 * C, 3 * C + QL + KVL + ROPE
    return [("bcv", k, k * C, (k + 1) * C) for k in range(3)] + [("qkr", None, o1, o2), ("gg", None, o2, o2 + 2 * D)]


def _win_pieces(segments, cw):
    out = []
    for tgt, lead, a, b in segments:
        for d in range(N_DEV):
            lo, hi = max(a, d * cw), min(b, (d + 1) * cw)
            if lo < hi:
                out.append((tgt, lead, d, (lo - d * cw, hi - d * cw), (lo - a, hi - a)))
    return out


def _win_split(win_w, C, QL, KVL):
    _, D, cw = win_w.shape
    WQ = QL + KVL + LANES
    pieces = _win_pieces(_win_segments(C, QL, KVL, D), cw)
    tr = _tile(D, 256, BF16_ROWS)

    def body(w_ref, bcv_ref, qkr_ref, gg_ref):
        tgt = dict(bcv=bcv_ref, qkr=qkr_ref, gg=gg_ref)
        qkr_ref[:, QL + KVL + ROPE:] = jnp.zeros((tr, LANES - ROPE), BF16)
        for name, lead, d, (s0, s1), (t0, t1) in pieces:
            v = w_ref[d, :, s0:s1]
            if lead is None:
                tgt[name][:, t0:t1] = v
            else:
                tgt[name][lead, :, t0:t1] = v

    return _call(
        body, "win_split", (D // tr,),
        [pl.BlockSpec((N_DEV, tr, cw), lambda i: (0, i, 0))],
        [pl.BlockSpec((3, tr, C), lambda i: (0, i, 0)), _rows(tr, WQ), _rows(tr, 2 * D)],
        [_sds((3, D, C), BF16), _sds((D, WQ), BF16), _sds((D, 2 * D), BF16)],
    )(win_w)


def _win_merge(d_bcv, d_qkr, d_gg, cw):
    _, D, C = d_bcv.shape
    WQ = d_qkr.shape[1]
    QL_KVL = WQ - LANES
    o1 = 3 * C
    segments = [("bcv", k, k * C, (k + 1) * C) for k in range(3)]
    segments += [("qkr", None, o1, o1 + QL_KVL + ROPE), ("gg", None, o1 + QL_KVL + ROPE, o1 + QL_KVL + ROPE + 2 * D)]
    pieces = _win_pieces(segments, cw)
    tr = _tile(D, 256, BF16_ROWS)

    def body(bcv_ref, qkr_ref, gg_ref, o_ref):
        src = dict(bcv=bcv_ref, qkr=qkr_ref, gg=gg_ref)
        for name, lead, d, (s0, s1), (t0, t1) in pieces:
            v = src[name][:, t0:t1] if lead is None else src[name][lead, :, t0:t1]
            o_ref[d, :, s0:s1] = v.astype(BF16)

    return _call(
        body, "win_merge", (D // tr,),
        [pl.BlockSpec((3, tr, C), lambda i: (0, i, 0)), _rows(tr, WQ), _rows(tr, 2 * D)],
        pl.BlockSpec((N_DEV, tr, cw), lambda i: (0, i, 0)),
        _sds((N_DEV, D, cw), BF16),
    )(d_bcv, d_qkr, d_gg)


def _mix_in(h, gain, w_bcv, w_qkr, w_gg):
    S, D = h.shape
    C = w_bcv.shape[2]
    tm = _tile(S, 512, SUBLANES)

    def body(h_ref, gain_ref, w1, w2, w3, o1, o2, o3, n_ref, r_ref):
        n32, rstd = _rms_fwd(h_ref[...], gain_ref[...])
        n = n32.astype(BF16)
        n_ref[...] = n.T
        r_ref[...] = rstd
        for k in range(3):
            o1[k] = _dot(n, w1[k]).astype(BF16)
        o2[...] = _dot(n, w2[...])
        o3[...] = _dot(n, w3[...]).astype(BF16)

    return _call(
        body, "mix_in", (S // tm,),
        [_rows(tm, D), _whole(gain), _whole(w_bcv), _whole(w_qkr), _whole(w_gg)],
        [pl.BlockSpec((3, tm, C), lambda i: (0, i, 0)), _rows(tm, w_qkr.shape[1]), _rows(tm, 2 * D),
         pl.BlockSpec((D, tm), lambda i: (0, i)), _rows(tm, 1)],
        [_sds((3, S, C), BF16), _sds((S, w_qkr.shape[1]), F32), _sds((S, 2 * D), BF16), _sds((D, S), BF16),
         _sds((S, 1), F32)],
    )(h, gain, w_bcv, w_qkr, w_gg)


def _conv_taps(zc):
    rows = lax.broadcasted_iota(jnp.int32, zc.shape, 0)
    z1 = jnp.where(rows >= 1, pltpu.roll(zc, 1, 0), 0.0)
    z2 = jnp.where(rows >= 2, pltpu.roll(zc, 2, 0), 0.0)
    return z1, z2


def _conv_fwd(z_bcv, conv_w):
    _, S, C = z_bcv.shape

    def body(z_ref, w_ref, o_ref):
        w = w_ref[...]
        zc = z_ref[1].astype(F32) * z_ref[2].astype(F32)
        z1, z2 = _conv_taps(zc)
        y = w[0:1] * z2 + w[1:2] * z1 + w[2:3] * zc
        o_ref[...] = (z_ref[0].astype(F32) * y).astype(BF16)

    return _call(
        body, "conv_fwd", (C // LANES,),
        [pl.BlockSpec((3, S, LANES), lambda j: (0, 0, j)), pl.BlockSpec((3, LANES), lambda j: (0, j))],
        pl.BlockSpec((S, LANES), lambda j: (0, j)),
        _sds((S, C), BF16),
    )(z_bcv, conv_w)


def _rope(x, cs, half):
    c, s1, s2 = cs[:, :LANES], cs[:, LANES:2 * LANES], cs[:, 2 * LANES:]
    return x * c + pltpu.roll(x, LANES - half, 1) * s1 + pltpu.roll(x, half, 1) * s2


def _unrope(d, cs, half):
    c, s1, s2 = cs[:, :LANES], cs[:, LANES:2 * LANES], cs[:, 2 * LANES:]
    return d * c + pltpu.roll(d * s1, half, 1) + pltpu.roll(d * s2, LANES - half, 1)


def _mla_prep(z_qkr, gq, gkv, cs, c256_w):
    S = z_qkr.shape[0]
    QL, KVL = gq.shape[1], gkv.shape[1]
    H = N_DEV
    tm = _tile(S, 512, SUBLANES)
    half = ROPE // 2

    def body(z_ref, gq_ref, gkv_ref, cs_ref, w_ref, q_ref, k_ref, v_ref, qn_ref, kvn_ref, rq_ref, rkv_ref):
        z = z_ref[...]
        cs_t = cs_ref[...]
        qn32, rq = _rms_fwd(z[:, :QL], gq_ref[...])
        kvn32, rkv = _rms_fwd(z[:, QL:QL + KVL], gkv_ref[...])
        qn = qn32.astype(BF16)
        kvn = kvn32.astype(BF16)
        qn_ref[...] = qn.T
        kvn_ref[...] = kvn.T
        rq_ref[...] = rq
        rkv_ref[...] = rkv
        krope = _rope(z[:, QL + KVL:], cs_t, half).astype(BF16)
        for h in range(H):
            lo, mid, hi = h * HEAD_SLOT, h * HEAD_SLOT + LANES, (h + 1) * HEAD_SLOT
            q = _dot(qn, w_ref[h, KVL:KVL + QL, :])
            kv = _dot(kvn, w_ref[h, 0:KVL, :])
            q_ref[:, lo:mid] = (q[:, :LANES] * SCORE_SCALE).astype(BF16)
            q_ref[:, mid:hi] = (_rope(q[:, LANES:], cs_t, half) * SCORE_SCALE).astype(BF16)
            k_ref[:, lo:mid] = kv[:, :LANES].astype(BF16)
            k_ref[:, mid:hi] = krope
            v_ref[:, h * VDIM:(h + 1) * VDIM] = kv[:, LANES:].astype(BF16)

    return _call(
        body, "mla_prep", (S // tm,),
        [_rows(tm, z_qkr.shape[1]), _whole(gq), _whole(gkv), _rows(tm, 3 * LANES), _whole(c256_w)],
        [_rows(tm, H * HEAD_SLOT), _rows(tm, H * HEAD_SLOT), _rows(tm, H * VDIM),
         pl.BlockSpec((QL, tm), lambda i: (0, i)), pl.BlockSpec((KVL, tm), lambda i: (0, i)),
         _rows(tm, 1), _rows(tm, 1)],
        [_sds((S, H * HEAD_SLOT), BF16), _sds((S, H * HEAD_SLOT), BF16), _sds((S, H * VDIM), BF16),
         _sds((QL, S), BF16), _sds((KVL, S), BF16), _sds((S, 1), F32), _sds((S, 1), F32)],
    )(z_qkr, gq, gkv, cs, c256_w)


def _chunk_mask(rows, cols, diagonal_row):
    shift = CHUNK.bit_length() - 1
    krow = (lax.broadcasted_iota(jnp.int32, (rows, cols), 0) - diagonal_row) >> shift
    qcol = lax.broadcasted_iota(jnp.int32, (rows, cols), 1) >> shift
    return krow <= qcol


def _attn_fwd(q, k, v, H):
    S = q.shape[0]
    t = _tile(S, 512, CHUNK)
    nq = S // t

    def body(q_ref, k_ref, v_ref, o_ref, ot_ref, lse_ref, vt_ref):
        qi = pl.program_id(1)

        @pl.when(qi == 0)
        def _():
            vt_ref[0:VDIM, :] = v_ref[...].T
            vt_ref[VDIM:, :] = jnp.ones((BF16_ROWS, S), BF16)

        qv = q_ref[...]

        def block(start, width, carry, masked):
            m, acc = carry
            off = pl.multiple_of(start * t, t)
            s = _dot_nt(k_ref[pl.ds(off, width * t), :], qv)
            if masked:
                s = jnp.where(_chunk_mask(width * t, t, (width - 1) * t), s, -1e30)
            m_new = jnp.maximum(m, jnp.max(s, axis=0, keepdims=True))
            p = jnp.exp2(s - m_new).astype(BF16)
            acc = jnp.exp2(m - m_new) * acc + _dot(vt_ref[:, pl.ds(off, width * t)], p)
            return m_new, acc

        init = (jnp.full((1, t), -1e30, F32), jnp.zeros((VDIM + BF16_ROWS, t), F32))
        wide = lax.div(qi, ATTN_FWD_WIDTH)
        carry = lax.fori_loop(0, wide, lambda j, c: block(j * ATTN_FWD_WIDTH, ATTN_FWD_WIDTH, c, False), init)
        left = qi - wide * ATTN_FWD_WIDTH
        for extra in range(ATTN_FWD_WIDTH):
            @pl.when(left == extra)
            def _():
                m, acc = block(qi - extra, extra + 1, carry, True)
                l = acc[VDIM:VDIM + 1]
                out = (acc[0:VDIM] * (1.0 / l)).astype(BF16)
                ot_ref[...] = out
                o_ref[...] = out.T
                lse_ref[0] = jnp.broadcast_to(m + jnp.log2(l), (SUBLANES, t))

    return _call(
        body, "attn_fwd", (H, nq),
        [pl.BlockSpec((t, HEAD_SLOT), lambda h, i: (i, h)), pl.BlockSpec((S, HEAD_SLOT), lambda h, i: (0, h)),
         pl.BlockSpec((S, VDIM), lambda h, i: (0, h))],
        [pl.BlockSpec((t, VDIM), lambda h, i: (i, h)), pl.BlockSpec((VDIM, t), lambda h, i: (h, i)),
         pl.BlockSpec((1, SUBLANES, t), lambda h, i: (h, 0, i))],
        [_sds((S, H * VDIM), BF16), _sds((H * VDIM, S), BF16), _sds((H, SUBLANES, S), F32)],
        [pltpu.VMEM((VDIM + BF16_ROWS, S), BF16)],
    )(q, k, v)


def _merge_wo(o, by, z_gg, h, sq_w, c128_w):
    S, D = h.shape
    C = by.shape[1]
    r = sq_w.shape[1] // 3
    tm = _tile(S, 512, SUBLANES)

    def body(o_ref, by_ref, gg_ref, h_ref, wmo_ref, wo_ref, wco_ref, h2_ref, mg_ref, yc_ref, ym_ref):
        ymla = _dot(o_ref[...], wmo_ref[...].reshape(N_DEV * r, D))
        yconv = _dot(by_ref[...], _cat_slots(wco_ref))
        gg = gg_ref[...].astype(F32)
        merged = (_sig(gg[:, :D]) * yconv + _sig(gg[:, D:]) * ymla).astype(BF16)
        mg_ref[...] = merged.T
        yc_ref[...] = yconv.astype(BF16)
        ym_ref[...] = ymla.astype(BF16)
        h2_ref[...] = h_ref[...] + _dot(merged, wo_ref[...].reshape(N_DEV * r, D))

    return _call(
        body, "merge_wo", (S // tm,),
        [_rows(tm, o.shape[1]), _rows(tm, C), _rows(tm, 2 * D), _rows(tm, D), _slab(sq_w, r, 0), _slab(sq_w, r, 1),
         _slab(c128_w, C, 0)],
        [_rows(tm, D), pl.BlockSpec((D, tm), lambda i: (0, i)), _rows(tm, D), _rows(tm, D)],
        [_sds((S, D), F32), _sds((D, S), BF16), _sds((S, D), BF16), _sds((S, D), BF16)],
    )(o, by, z_gg, h, sq_w, sq_w, c128_w)


def _ple_fwd(h, gain, p, sq_w, c128_w, C):
    S, D = h.shape
    P = p.shape[1]
    r = sq_w.shape[1] // 3
    tm = _tile(S, 512, SUBLANES)

    def body(h_ref, gain_ref, p_ref, wpg_ref, wpp_ref, o_ref, pre_ref, pp_ref, n_ref, r_ref):
        x = h_ref[...]
        n32, rstd = _rms_fwd(x, gain_ref[...])
        n = n32.astype(BF16)
        n_ref[...] = n.T
        r_ref[...] = rstd
        pre = _dot(n, wpg_ref[...].reshape(N_DEV * r, D))
        pp = _dot(p_ref[...].astype(BF16), _cat_slots(wpp_ref))
        pre_ref[...] = pre.astype(BF16)
        pp_ref[...] = pp.astype(BF16)
        o_ref[...] = x + _sig(pre) * pp

    return _call(
        body, "ple_fwd", (S // tm,),
        [_rows(tm, D), _whole(gain), _rows(tm, P), _slab(sq_w, r, 2), _slab(c128_w, P, C // P)],
        [_rows(tm, D), _rows(tm, D), _rows(tm, D), pl.BlockSpec((D, tm), lambda i: (0, i)), _rows(tm, 1)],
        [_sds((S, D), F32), _sds((S, D), BF16), _sds((S, D), BF16), _sds((D, S), BF16), _sds((S, 1), F32)],
    )(h, gain, p, sq_w, c128_w)


def _final_loss(h, gain, target):
    S, D = h.shape
    tm = _tile(S, 512, SUBLANES)

    def body(h_ref, gain_ref, t_ref, dh_ref, loss_ref, dg_ref):
        @pl.when(pl.program_id(0) == 0)
        def _():
            loss_ref[...] = jnp.zeros_like(loss_ref)
            dg_ref[...] = jnp.zeros_like(dg_ref)

        x = h_ref[...]
        gain_v = gain_ref[...]
        y, rstd = _rms_fwd(x, gain_v)
        err = y - t_ref[...]
        loss_ref[...] += 0.5 * jnp.sum(jnp.mean(err * err, axis=-1, keepdims=True))
        dx, dgain = _rms_bwd(err * (1.0 / D), x, rstd, gain_v)
        dh_ref[...] = dx
        dg_ref[...] += dgain

    return _call(
        body, "final_loss", (S // tm,),
        [_rows(tm, D), _whole(gain), _rows(tm, D)],
        [_rows(tm, D), pl.BlockSpec((1, LANES), lambda i: (0, 0)), pl.BlockSpec((1, D), lambda i: (0, 0))],
        [_sds((S, D), F32), _sds((1, LANES), F32), _sds((1, D), F32)],
    )(h, gain, target)


def _tn_call(body, name, grid, in_specs, out_spec, out_shape, scratch, operands, prev):
    n = len(operands)
    if prev is None:
        return _call(body, name, grid, in_specs, out_spec, out_shape, scratch)(*operands)
    assert prev.shape == out_shape.shape and prev.dtype == out_shape.dtype

    def wrapped(*refs):
        body(*refs[:n], *refs[n + 1:])

    return _call(wrapped, name, grid, in_specs + [ANY], out_spec, out_shape, scratch, {n: 0})(*operands, prev)


def _tn_slots(xt, dy, prev, rows_total, row_off):
    K, S = xt.shape
    B, _, c = dy.shape
    tk = _tile(K, 1024, BF16_ROWS)

    def body(xt_ref, dy_ref, o_ref):
        o_ref[0] = _dot(xt_ref[...], dy_ref[0]).astype(BF16)

    return _tn_call(
        body, "tn_slots", (K // tk, B),
        [pl.BlockSpec((tk, S), lambda i, b: (i, 0)), pl.BlockSpec((1, S, c), lambda i, b: (b, 0, 0))],
        pl.BlockSpec((1, tk, c), lambda i, b: (b, row_off // tk + i, 0)),
        _sds((B, rows_total, c), BF16), [], [xt, dy], prev)


def _tn_plain(xt, dy):
    K, S = xt.shape
    B, _, c = dy.shape
    tk = _tile(K, 512, BF16_ROWS)
    tn = _tile(c, 1024, LANES)

    def body(xt_ref, dy_ref, o_ref):
        o_ref[0] = _dot(xt_ref[...], dy_ref[0])

    return _call(
        body, "tn_plain", (K // tk, B, c // tn),
        [pl.BlockSpec((tk, S), lambda i, b, j: (i, 0)), pl.BlockSpec((1, S, tn), lambda i, b, j: (b, 0, j))],
        pl.BlockSpec((1, tk, tn), lambda i, b, j: (b, i, j)),
        _sds((B, K, c), F32),
    )(xt, dy)


def _tn_down(at, dh, prev, rows_total, which):
    nb, c, S = at.shape
    D = dh.shape[1]
    r = c // 2
    tn = _tile(D, 512, LANES)

    def body(at_ref, dh_ref, o_ref):
        g = 0.5 * _dot(at_ref[0], dh_ref[...].astype(BF16))
        o_ref[...] = g.astype(BF16).reshape(2, r, tn)

    return _tn_call(
        body, "tn_down", (D // tn, nb),
        [pl.BlockSpec((1, c, S), lambda j, i: (i, 0, 0)), pl.BlockSpec((S, tn), lambda j, i: (0, j))],
        pl.BlockSpec((2, r, tn), lambda j, i: (i, which, j)),
        _sds((N_DEV, rows_total, D), BF16), [], [at, dh], prev)


def _tn_square(xt, dy, prev, rows_total, member):
    K, S = xt.shape
    N = dy.shape[1]
    r = K // N_DEV
    tk = _tile(K, 512, r)
    tn = _tile(N, 512, LANES)

    def body(xt_ref, dy_ref, o_ref):
        g = _dot(xt_ref[...], dy_ref[...].astype(BF16))
        o_ref[...] = g.astype(BF16).reshape(tk // r, r, tn)

    return _tn_call(
        body, "tn_square", (N // tn, K // tk),
        [pl.BlockSpec((tk, S), lambda j, i: (i, 0)), pl.BlockSpec((S, tn), lambda j, i: (0, j))],
        pl.BlockSpec((tk // r, r, tn), lambda j, i: (i, member, j)),
        _sds((N_DEV, rows_total, N), BF16), [], [xt, dy], prev)


def _tn_cols(x, dy, prev, rows_total, row_block):
    S, K = x.shape
    N = dy.shape[1]
    cw = N // N_DEV

    def body(x_ref, dy_ref, o_ref):
        g = _dot(x_ref[...].astype(BF16).T, dy_ref[...])
        for d in range(N_DEV):
            o_ref[d] = g[:, d * cw:(d + 1) * cw].astype(BF16)

    return _tn_call(
        body, "tn_cols", (1,),
        [pl.BlockSpec((S, K), lambda i: (0, 0)), pl.BlockSpec((S, N), lambda i: (0, 0))],
        pl.BlockSpec((N_DEV, K, cw), lambda i: (0, row_block, 0)),
        _sds((N_DEV, rows_total, cw), BF16), [], [x, dy], prev)


def _tn_heads(qnt, kvnt, dqp, dkv):
    QL, S = qnt.shape
    KVL = kvnt.shape[0]

    def body(qn_ref, kvn_ref, dq_ref, dkv_ref, o_ref):
        o_ref[0, 0:KVL, :] = _dot(kvn_ref[...], dkv_ref[...]).astype(BF16)
        o_ref[0, KVL:KVL + QL, :] = _dot(qn_ref[...], dq_ref[...]).astype(BF16)

    head = pl.BlockSpec((S, HEAD_SLOT), lambda h: (0, h))
    return _call(
        body, "tn_heads", (N_DEV,),
        [pl.BlockSpec((QL, S), lambda h: (0, 0)), pl.BlockSpec((KVL, S), lambda h: (0, 0)), head, head],
        pl.BlockSpec((1, KVL + QL, HEAD_SLOT), lambda h: (h, 0, 0)),
        _sds((N_DEV, KVL + QL, HEAD_SLOT), BF16),
    )(qnt, kvnt, dqp, dkv)


def _ple_bwd(dh, pre, pp, h, rstd, gain, sq_w, after):
    S, D = h.shape
    r = sq_w.shape[1] // 3
    tm = _tile(S, 512, SUBLANES)

    def body(dh_ref, pre_ref, pp_ref, h_ref, r_ref, gain_ref, wpg_ref, *rest):
        o_ref, dpre_ref, dpp_ref, dg_ref = rest[len(after):]

        @pl.when(pl.program_id(0) == 0)
        def _():
            dg_ref[...] = jnp.zeros_like(dg_ref)

        d = dh_ref[...]
        gate = _sig(pre_ref[...].astype(F32))
        dpre = (d * pp_ref[...].astype(F32) * gate * (1.0 - gate)).astype(BF16)
        dpre_ref[...] = dpre
        dpp_ref[...] = (d * gate).astype(BF16)
        dn = _dot_nt(dpre, wpg_ref[...].reshape(N_DEV * r, D))
        dx, dgain = _rms_bwd(dn, h_ref[...], r_ref[...], gain_ref[...])
        o_ref[...] = d + dx
        dg_ref[...] += dgain

    return _call(
        body, "ple_bwd", (S // tm,),
        [_rows(tm, D), _rows(tm, D), _rows(tm, D), _rows(tm, D), _rows(tm, 1), _whole(gain), _slab(sq_w, r, 2)]
        + [ANY] * len(after),
        [_rows(tm, D), _rows(tm, D), _rows(tm, D), pl.BlockSpec((1, D), lambda i: (0, 0))],
        [_sds((S, D), F32), _sds((S, D), BF16), _sds((S, D), BF16), _sds((1, D), F32)],
    )(dh, pre, pp, h, rstd, gain, sq_w, *after)


def _ffn_bwd(dh, jac, gu_w, dn_w, h, rstd, gain, after=()):
    S, D = h.shape
    _, _, c = jac.shape
    nb = N_DEV // 2
    tm = _tile(S, 256, SUBLANES)

    def body(dh_ref, jac_ref, w_ref, wd_ref, h_ref, r_ref, gain_ref, *rest):
        dgu_ref, o_ref, dgain_ref = rest[len(after):]

        @pl.when(pl.program_id(0) == 0)
        def _():
            dgain_ref[...] = jnp.zeros_like(dgain_ref)

        dh_v = dh_ref[...]
        dhb = dh_v.astype(BF16)
        dn = jnp.zeros((tm, D), F32)
        for d in range(nb):
            da = _dot_nt(dhb, _down_weight(wd_ref, d, c))
            dg = (da * jac_ref[d].astype(F32)).astype(BF16)
            du = (da * jac_ref[nb + d].astype(F32)).astype(BF16)
            dgu_ref[d] = dg
            dgu_ref[nb + d] = du
            dn = dn + _dot_nt(dg, w_ref[d]) + _dot_nt(du, w_ref[nb + d])
        dx, dgain = _rms_bwd(dn, h_ref[...], r_ref[...], gain_ref[...])
        o_ref[...] = dh_v + dx
        dgain_ref[...] += dgain

    act = pl.BlockSpec((N_DEV, tm, c), lambda i: (0, i, 0))
    return _call(
        body, "ffn_bwd", (S // tm,),
        [_rows(tm, D), act, _slab(gu_w, D, 0), _slab(dn_w, c // 2, 0), _rows(tm, D), _rows(tm, 1), _whole(gain)]
        + [ANY] * len(after),
        [act, _rows(tm, D), pl.BlockSpec((1, D), lambda i: (0, 0))],
        [_sds((N_DEV, S, c), BF16), _sds((S, D), F32), _sds((1, D), F32)],
    )(dh, jac, gu_w, dn_w, h, rstd, gain, *after)


def _merge_bwd(dh, z_gg, yconv, ymla, o, sq_w, c128_w, C, after):
    S, D = dh.shape
    r = sq_w.shape[1] // 3
    HV = N_DEV * r
    H = HV // VDIM
    tm = _tile(S, 512, SUBLANES)

    def head_rows():
        row = lax.broadcasted_iota(jnp.int32, (SUBLANES * H, HV), 0) >> (SUBLANES.bit_length() - 1)
        col = lax.broadcasted_iota(jnp.int32, (SUBLANES * H, HV), 1) >> (VDIM.bit_length() - 1)
        return jnp.where(row == col, 1.0, 0.0).astype(BF16)

    def body(dh_ref, gg_ref, yc_ref, ym_ref, o_ref, wmo_ref, wo_ref, wco_ref, *rest):
        dgg_ref, dby_ref, do_ref, dyc_ref, dym_ref, dl_ref = rest[len(after):]
        dm = _dot_nt(dh_ref[...].astype(BF16), wo_ref[...].reshape(HV, D))
        gg = gg_ref[...].astype(F32)
        sgc = _sig(gg[:, :D])
        sgm = _sig(gg[:, D:])
        dyc = (dm * sgc).astype(BF16)
        dym = (dm * sgm).astype(BF16)
        dyc_ref[...] = dyc
        dym_ref[...] = dym
        dgg_ref[:, :D] = (dm * yc_ref[...].astype(F32) * sgc * (1.0 - sgc)).astype(BF16)
        dgg_ref[:, D:] = (dm * ym_ref[...].astype(F32) * sgm * (1.0 - sgm)).astype(BF16)
        dby_ref[...] = _dot_nt(dyc, _cat_slots(wco_ref)).astype(BF16)
        do = _dot_nt(dym, wmo_ref[...].reshape(HV, D)).astype(BF16)
        do_ref[...] = do
        prod = do.astype(F32) * o_ref[...].astype(F32)
        hi = prod.astype(BF16)
        lo = (prod - hi.astype(F32)).astype(BF16)
        pick = head_rows()
        dl_ref[...] = _dot_nt(pick, hi) + _dot_nt(pick, lo)

    return _call(
        body, "merge_bwd", (S // tm,),
        [_rows(tm, D), _rows(tm, 2 * D), _rows(tm, D), _rows(tm, D), _rows(tm, HV), _slab(sq_w, r, 0),
         _slab(sq_w, r, 1), _slab(c128_w, C, 0)] + [ANY] * len(after),
        [_rows(tm, 2 * D), _rows(tm, C), _rows(tm, HV), _rows(tm, D), _rows(tm, D),
         pl.BlockSpec((SUBLANES * H, tm), lambda i: (0, i))],
        [_sds((S, 2 * D), BF16), _sds((S, C), BF16), _sds((S, HV), BF16), _sds((S, D), BF16), _sds((S, D), BF16),
         _sds((SUBLANES * H, S), F32)],
    )(dh, z_gg, yconv, ymla, o, sq_w, sq_w, c128_w, *after)


def _conv_bwd(z_bcv, conv_w, dby):
    _, S, C = z_bcv.shape

    def body(z_ref, w_ref, dby_ref, dz_ref, dw_ref):
        w = w_ref[...]
        c = z_ref[1].astype(F32)
        v = z_ref[2].astype(F32)
        d = dby_ref[...].astype(F32)
        zc = c * v
        z1, z2 = _conv_taps(zc)
        y = w[0:1] * z2 + w[1:2] * z1 + w[2:3] * zc
        dz_ref[0] = (d * y).astype(BF16)
        dy = d * z_ref[0].astype(F32)
        rows = lax.broadcasted_iota(jnp.int32, dy.shape, 0)
        dy1 = jnp.where(rows < S - 1, pltpu.roll(dy, S - 1, 0), 0.0)
        dy2 = jnp.where(rows < S - 2, pltpu.roll(dy, S - 2, 0), 0.0)
        dzc = w[2:3] * dy + w[1:2] * dy1 + w[0:1] * dy2
        dz_ref[1] = (dzc * v).astype(BF16)
        dz_ref[2] = (dzc * c).astype(BF16)
        dw_ref[0:1, :] = jnp.sum(dy * z2, axis=0, keepdims=True)
        dw_ref[1:2, :] = jnp.sum(dy * z1, axis=0, keepdims=True)
        dw_ref[2:3, :] = jnp.sum(dy * zc, axis=0, keepdims=True)

    three = pl.BlockSpec((3, S, LANES), lambda j: (0, 0, j))
    wspec = pl.BlockSpec((3, LANES), lambda j: (0, j))
    return _call(
        body, "conv_bwd", (C // LANES,),
        [three, wspec, pl.BlockSpec((S, LANES), lambda j: (0, j))],
        [three, wspec],
        [_sds((3, S, C), BF16), _sds((3, C), F32)],
    )(z_bcv, conv_w, dby)


def _attn_bwd(q, k, v, do, lse, delta, H):
    S = q.shape[0]
    t = _tile(S, 512, CHUNK)
    nk = S // t

    def body(q_ref, k_ref, v_ref, do_ref, lse_ref, dl_ref, dq_ref, dk_ref, dv_ref, dqt_ref):
        kj = pl.program_id(1)

        @pl.when(kj == 0)
        def _():
            dqt_ref[...] = jnp.zeros_like(dqt_ref)

        kv = k_ref[...]
        vv = v_ref[...]
        kt = kv.T

        def block(start, width, carry, masked):
            dk, dv = carry
            off = pl.multiple_of(start * t, t)
            qv = q_ref[pl.ds(off, width * t), :]
            dov = do_ref[pl.ds(off, width * t), :]
            s = _dot_nt(kv, qv)
            if masked:
                s = jnp.where(_chunk_mask(t, width * t, 0), s, -1e30)
            p = jnp.exp2(s - lse_ref[0, 0:1, pl.ds(off, width * t)])
            dp = _dot_nt(vv, dov)
            ds = (p * (dp - dl_ref[0, 0:1, pl.ds(off, width * t)]) * LN2).astype(BF16)
            dqt_ref[:, pl.ds(off, width * t)] += _dot(kt, ds)
            return dk + _dot(ds, qv), dv + _dot(p.astype(BF16), dov)

        init = (jnp.zeros((t, HEAD_SLOT), F32), jnp.zeros((t, VDIM), F32))
        wide = lax.div(nk - 1 - kj, ATTN_BWD_WIDTH)
        left = nk - 1 - kj - wide * ATTN_BWD_WIDTH
        carry = lax.switch(left, [functools.partial(block, kj, extra + 1, init, True)
                                  for extra in range(ATTN_BWD_WIDTH)])
        dk, dv = lax.fori_loop(
            0, wide, lambda j, c: block(kj + 1 + left + j * ATTN_BWD_WIDTH, ATTN_BWD_WIDTH, c, False), carry)
        dk_ref[...] = dk.astype(BF16)
        dv_ref[...] = dv.astype(BF16)

        @pl.when(kj == nk - 1)
        def _():
            dq_ref[...] = (dqt_ref[...] * SCORE_SCALE).T.astype(BF16)

    kspec = lambda w: pl.BlockSpec((t, w), lambda h, j: (j, h))
    qspec = lambda w: pl.BlockSpec((S, w), lambda h, j: (0, h))
    stat = pl.BlockSpec((1, SUBLANES, S), lambda h, j: (h, 0, 0))
    return _call(
        body, "attn_bwd", (H, nk),
        [qspec(HEAD_SLOT), kspec(HEAD_SLOT), kspec(VDIM), qspec(VDIM), stat, stat],
        [qspec(HEAD_SLOT), kspec(HEAD_SLOT), kspec(VDIM)],
        [_sds((S, H * HEAD_SLOT), BF16), _sds((S, H * HEAD_SLOT), BF16), _sds((S, H * VDIM), BF16)],
        [pltpu.VMEM((HEAD_SLOT, S), F32)],
    )(q, k, v, do, lse, delta)


def _mla_prep_bwd(dq, dk, dv, z_qkr, rq, rkv, gq, gkv, cs, c256_w):
    S = z_qkr.shape[0]
    QL, KVL = gq.shape[1], gkv.shape[1]
    H = N_DEV
    tm = _tile(S, 512, SUBLANES)
    half = ROPE // 2

    def body(dq_ref, dk_ref, dv_ref, z_ref, rq_ref, rkv_ref, gq_ref, gkv_ref, cs_ref, w_ref,
             dz_ref, dqp_ref, dkv_ref, dgq_ref, dgkv_ref):
        @pl.when(pl.program_id(0) == 0)
        def _():
            dgq_ref[...] = jnp.zeros_like(dgq_ref)
            dgkv_ref[...] = jnp.zeros_like(dgkv_ref)

        cs_t = cs_ref[...]
        dkr = jnp.zeros((tm, LANES), F32)
        dqn = jnp.zeros((tm, QL), F32)
        dkvn = jnp.zeros((tm, KVL), F32)
        for h in range(H):
            lo, mid, hi = h * HEAD_SLOT, h * HEAD_SLOT + LANES, (h + 1) * HEAD_SLOT
            dqp_ref[:, lo:mid] = dq_ref[:, lo:mid]
            dqp_ref[:, mid:hi] = _unrope(dq_ref[:, mid:hi].astype(F32), cs_t, half).astype(BF16)
            dkv_ref[:, lo:mid] = dk_ref[:, lo:mid]
            dkv_ref[:, mid:hi] = dv_ref[:, h * VDIM:(h + 1) * VDIM]
            dkr = dkr + dk_ref[:, mid:hi].astype(F32)
            dqn = dqn + _dot_nt(dqp_ref[:, lo:hi], w_ref[h, KVL:KVL + QL, :])
            dkvn = dkvn + _dot_nt(dkv_ref[:, lo:hi], w_ref[h, 0:KVL, :])
        z = z_ref[...]
        dqc, dgq = _rms_bwd(dqn, z[:, :QL], rq_ref[...], gq_ref[...])
        dkvc, dgkv = _rms_bwd(dkvn, z[:, QL:QL + KVL], rkv_ref[...], gkv_ref[...])
        dz_ref[:, :QL] = dqc.astype(BF16)
        dz_ref[:, QL:QL + KVL] = dkvc.astype(BF16)
        dz_ref[:, QL + KVL:] = _unrope(dkr, cs_t, half).astype(BF16)
        dgq_ref[...] += dgq
        dgkv_ref[...] += dgkv

    W = z_qkr.shape[1]
    return _call(
        body, "mla_prep_bwd", (S // tm,),
        [_rows(tm, H * HEAD_SLOT), _rows(tm, H * HEAD_SLOT), _rows(tm, H * VDIM), _rows(tm, W), _rows(tm, 1),
         _rows(tm, 1), _whole(gq), _whole(gkv), _rows(tm, 3 * LANES), _whole(c256_w)],
        [_rows(tm, W), _rows(tm, H * HEAD_SLOT), _rows(tm, H * HEAD_SLOT), _whole(gq), _whole(gkv)],
        [_sds((S, W), BF16), _sds((S, H * HEAD_SLOT), BF16), _sds((S, H * HEAD_SLOT), BF16),
         _sds((1, QL), F32), _sds((1, KVL), F32)],
    )(dq, dk, dv, z_qkr, rq, rkv, gq, gkv, cs, c256_w)


def _mix_in_bwd(d_bcv, dz_qkr, dgg, w_bcv, w_qkr, w_gg, h, rstd, gain, dh):
    S, D = h.shape
    C = d_bcv.shape[2]
    tm = _tile(S, 512, SUBLANES)

    def body(db_ref, dq_ref, dgg_ref, wb_ref, wq_ref, wg_ref, h_ref, r_ref, gain_ref, dh_ref, o_ref, dgain_ref):
        @pl.when(pl.program_id(0) == 0)
        def _():
            dgain_ref[...] = jnp.zeros_like(dgain_ref)

        dn = _dot_nt(dq_ref[...], wq_ref[...]) + _dot_nt(dgg_ref[...], wg_ref[...])
        for k in range(3):
            dn = dn + _dot_nt(db_ref[k], wb_ref[k])
        dx, dgain = _rms_bwd(dn, h_ref[...], r_ref[...], gain_ref[...])
        o_ref[...] = dh_ref[...] + dx
        dgain_ref[...] += dgain

    return _call(
        body, "mix_in_bwd", (S // tm,),
        [pl.BlockSpec((3, tm, C), lambda i: (0, i, 0)), _rows(tm, dz_qkr.shape[1]), _rows(tm, dgg.shape[1]),
         _whole(w_bcv), _whole(w_qkr), _whole(w_gg), _rows(tm, D), _rows(tm, 1), _whole(gain), _rows(tm, D)],
        [_rows(tm, D), pl.BlockSpec((1, D), lambda i: (0, 0))],
        [_sds((S, D), F32), _sds((1, D), F32)],
    )(d_bcv, dz_qkr, dgg, w_bcv, w_qkr, w_gg, h, rstd, gain, dh)


def _rope_tables(positions):
    half = ROPE // 2
    inv_freq = ROPE_THETA ** (-jnp.arange(0, ROPE, 2, dtype=F32) / ROPE)
    ang = positions.astype(F32)[:, None] * inv_freq
    cos, sin = jnp.cos(ang), jnp.sin(ang)
    z = jnp.zeros_like(cos)
    pad = jnp.zeros((positions.shape[0], LANES - 2 * half), F32)
    return jnp.concatenate([cos, cos, pad, -sin, z, pad, z, sin, pad], axis=1)


def _grad_rows(w):
    return dict(gu=2 * w["gu1"].shape[1], dn=2 * w["dn1"].shape[1], sq=w["sq"].shape[1], win=w["win"].shape[1],
                c128=w["c128"].shape[1], c256=w["c256"].shape[1])


def _layer_fwd(h0, p_l, cs, w, sm, late):
    C = sm["conv_w"].shape[1]
    QL, KVL = sm["q_norm"].shape[1], sm["kv_norm"].shape[1]
    h1, jac1, at1, n1, r1 = _ffn_fwd(h0, sm["ffn1_norm"], w["gu1"], w["dn1"])
    if late is not None:
        w.update(late(h1))
    w_bcv, w_qkr, w_gg = _win_split(w["win"], C, QL, KVL)
    z_bcv, z_qkr, z_gg, un, rm = _mix_in(h1, sm["mix_norm"], w_bcv, w_qkr, w_gg)
    by = _conv_fwd(z_bcv, sm["conv_w"])
    q, k, v, qn, kvn, rq, rkv = _mla_prep(z_qkr, sm["q_norm"], sm["kv_norm"], cs, w["c256"])
    o, ot, lse = _attn_fwd(q, k, v, N_DEV)
    h2, merged, yconv, ymla = _merge_wo(o, by, z_gg, h1, w["sq"], w["c128"])
    h3, jac2, at2, n2, r2 = _ffn_fwd(h2, sm["ffn2_norm"], w["gu2"], w["dn2"])
    h4, pre, pp, pn, rp = _ple_fwd(h3, sm["ple_norm"], p_l, w["sq"], w["c128"], C)
    saved = dict(h0=h0, jac1=jac1, at1=at1, n1=n1, r1=r1, h1=h1, w_bcv=w_bcv, w_qkr=w_qkr, w_gg=w_gg, z_bcv=z_bcv,
                 z_qkr=z_qkr, z_gg=z_gg, un=un, rm=rm, by=by, q=q, k=k, v=v, qn=qn, kvn=kvn, rq=rq, rkv=rkv, o=o, ot=ot,
                 lse=lse, h2=h2, merged=merged, yconv=yconv, ymla=ymla, jac2=jac2, at2=at2, n2=n2, r2=r2, h3=h3,
                 pre=pre, pp=pp, pn=pn, rp=rp, p=p_l)
    return h4, saved


def _layer_bwd_late(dh4, s, w, sm, after):
    D = dh4.shape[1]
    C = sm["conv_w"].shape[1]
    P = s["p"].shape[1]
    rows = _grad_rows(w)
    small = {}
    dh3, dpre, dpp, small["ple_norm"] = _ple_bwd(dh4, s["pre"], s["pp"], s["h3"], s["rp"], sm["ple_norm"], w["sq"],
                                                 after)
    g_sq = _tn_square(s["pn"], dpre, None, rows["sq"], 2)
    g_c128 = _tn_cols(s["p"], dpp, None, rows["c128"], C // P)

    dgu2, dh2, small["ffn2_norm"] = _ffn_bwd(dh3, s["jac2"], w["gu2"], w["dn2"], s["h2"], s["r2"], sm["ffn2_norm"])
    g_dn = _tn_down(s["at2"], dh3, None, rows["dn"], 1)
    g_gu = _tn_slots(s["n2"], dgu2, None, rows["gu"], D)
    return dh2, dict(gu=g_gu, dn=g_dn, sq=g_sq, c128=g_c128), small


def _layer_bwd_mixer(dh2, part, small, s, cs, w, sm, after):
    C = sm["conv_w"].shape[1]
    rows = _grad_rows(w)
    g_gu, g_dn, g_sq, g_c128 = part["gu"], part["dn"], part["sq"], part["c128"]

    dgg, dby, do, dyc, dym, delta = _merge_bwd(dh2, s["z_gg"], s["yconv"], s["ymla"], s["o"], w["sq"], w["c128"], C,
                                               after)
    g_sq = _tn_square(s["merged"], dh2, g_sq, rows["sq"], 1)
    g_sq = _tn_square(s["ot"], dym, g_sq, rows["sq"], 0)
    g_c128 = _tn_cols(s["by"], dyc, g_c128, rows["c128"], 0)
    d_bcv, small["conv_w"] = _conv_bwd(s["z_bcv"], sm["conv_w"], dby)
    delta = delta.reshape(N_DEV, SUBLANES, delta.shape[1])
    dq, dk, dv = _attn_bwd(s["q"], s["k"], s["v"], do, s["lse"], delta, N_DEV)
    dz_qkr, dqp, dkv, small["q_norm"], small["kv_norm"] = _mla_prep_bwd(
        dq, dk, dv, s["z_qkr"], s["rq"], s["rkv"], sm["q_norm"], sm["kv_norm"], cs, w["c256"])
    g_c256 = _tn_heads(s["qn"], s["kvn"], dqp, dkv)
    un = s["un"]
    g_win = _win_merge(_tn_plain(un, d_bcv), _tn_plain(un, dz_qkr[None])[0], _tn_plain(un, dgg[None])[0],
                       w["win"].shape[2])
    dh1, small["mix_norm"] = _mix_in_bwd(d_bcv, dz_qkr, dgg, s["w_bcv"], s["w_qkr"], s["w_gg"], s["h1"], s["rm"],
                                         sm["mix_norm"], dh2)
    return dh1, dict(gu=g_gu, dn=g_dn, sq=g_sq, win=g_win, c128=g_c128, c256=g_c256), small


def _layer_bwd_first(dh1, part, small, s, w, sm, after):
    rows = _grad_rows(w)
    dgu1, dh0, small["ffn1_norm"] = _ffn_bwd(dh1, s["jac1"], w["gu1"], w["dn1"], s["h0"], s["r1"], sm["ffn1_norm"],
                                             after)
    g_dn = _tn_down(s["at1"], dh1, part["dn"], rows["dn"], 0)
    g_gu = _tn_slots(s["n1"], dgu1, part["gu"], rows["gu"], 0)
    return dh0, dict(part, gu=g_gu, dn=g_dn), small


def _mesh_pos():
    return lax.axis_index("x"), lax.axis_index("y"), lax.axis_index("c")


def _other_chips(x, y):
    return [(1 - x, y), (x, 1 - y), (1 - x, 1 - y)]


def _pack(arrs, flipped, width):
    L = arrs[0].shape[0]
    shapes = [a.shape[:0:-1] if f else a.shape[1:] for a, f in zip(arrs, flipped)]
    R = sum(r for r, _ in shapes)

    def body(*refs):
        o_ref = refs[-1]
        off = 0
        for a_ref, f, (r, c) in zip(refs[:-1], flipped, shapes):
            a = a_ref[0].T if f else a_ref[0]
            o_ref[0, off:off + r, 0:c] = a.astype(BF16)
            if c < width:
                o_ref[0, off:off + r, c:width] = jnp.zeros((r, width - c), BF16)
            off += r

    return _call(
        body, "pack", (L,),
        [pl.BlockSpec((1,) + a.shape[1:], lambda l: (l, 0, 0)) for a in arrs],
        pl.BlockSpec((1, R, width), lambda l: (l, 0, 0)),
        _sds((L, R, width), BF16),
    )(*arrs)


def _handshake(peers):
    barrier = pltpu.get_barrier_semaphore()
    for peer in peers:
        pl.semaphore_signal(barrier, inc=1, device_id=peer, device_id_type=MESH)
    pl.semaphore_wait(barrier, len(peers))


def _sequencer_call(body, name, out_types, sems, collective_id, operands):
    return pl.kernel(
        body, name=name, out_type=out_types,
        mesh=plsc.ScalarSubcoreMesh(axis_name="seq", num_cores=1),
        scratch_types=tuple(pltpu.SemaphoreType.DMA((k,)) for k in sems),
        compiler_params=pltpu.CompilerParams(collective_id=collective_id),
    )(*operands)


def _all_gather(packs, l, after, collective_id):
    n = len(packs)

    def body(*refs):
        ins, outs = refs[:n], refs[n + len(after):2 * n + len(after)]
        send_sems, recv_sems, local_sems = refs[2 * n + len(after):]
        x, y, c = _mesh_pos()
        me, sibling = (x, y, c), (x, y, 1 - c)
        chips = _other_chips(x, y)
        _handshake([sibling] + [(*chip, c) for chip in chips])

        def copy(q, k, block, to, src=None):
            slot = outs[q].at[4 * block[0] + 2 * block[1] + block[2]]
            return pltpu.make_async_remote_copy(
                src_ref=slot if src is None else src, dst_ref=slot,
                send_sem=send_sems.at[7 * q + k], recv_sem=recv_sems.at[7 * q + k], device_id=to, device_id_type=MESH)

        started = []
        for q in range(n):
            src = ins[q].at[l]
            mine = pltpu.make_async_copy(src, outs[q].at[4 * x + 2 * y + c], local_sems.at[q])
            mine.start()
            started.append(mine)
        sends = []
        for q in range(n):
            src = ins[q].at[l]
            sends.append(copy(q, 0, me, sibling, src=src))
            sends += [copy(q, 1 + j, me, (*chip, c), src=src) for j, chip in enumerate(chips)]
        for cp in sends:
            cp.start()
        for q in range(n):
            for j, chip in enumerate(chips):
                copy(q, 1 + j, (*chip, c), me).wait_recv()
                fwd = copy(q, 4 + j, (*chip, c), sibling)
                fwd.start()
                sends.append(fwd)
        for q in range(n):
            copy(q, 0, sibling, me).wait_recv()
            for j, chip in enumerate(chips):
                copy(q, 4 + j, (*chip, 1 - c), me).wait_recv()
        for cp in sends:
            cp.wait_send()
        for mine in started:
            mine.wait()

    return _sequencer_call(
        body, f"all_gather_{collective_id}", [_sds((N_DEV,) + p.shape[1:], p.dtype) for p in packs], (7 * n, 7 * n, n),
        collective_id, list(packs) + list(after))


def _rs_d2d(gs, l, collective_id):
    n = len(gs)

    def body(*refs):
        ins, outs = refs[:n], refs[n:2 * n]
        send_sems, recv_sems = refs[2 * n:]
        x, y, c = _mesh_pos()
        _handshake([(x, y, 1 - c)])
        copies = []
        for q in range(n):
            for j in range(4):
                copies.append(pltpu.make_async_remote_copy(
                    src_ref=ins[q].at[2 * j + (1 - c)], dst_ref=outs[q].at[j], send_sem=send_sems.at[4 * q + j],
                    recv_sem=recv_sems.at[4 * q + j], device_id=(x, y, 1 - c), device_id_type=MESH))
        for cp in copies:
            cp.start()
        for cp in copies:
            cp.wait()

    return _sequencer_call(
        body, f"rs_d2d_{l}", [_sds((4,) + g.shape[1:], g.dtype) for g in gs], (4 * n, 4 * n), collective_id, gs)


def _rs_add_chip(gs, as_, after):
    n = len(gs)
    steps = 4
    tiles = [g.shape[1] // steps for g in gs]

    def chip(k):
        x, y, _ = _mesh_pos()
        return _other_chips(x, y)[k]

    def body(*refs):
        g_refs, a_refs = refs[:3 * n], refs[3 * n:6 * n]
        t_refs = refs[6 * n + len(after):]
        for q in range(n):
            g, a = g_refs[3 * q:3 * q + 3], a_refs[3 * q:3 * q + 3]
            for k in range(3):
                t_refs[q][k] = (g[k][0].astype(F32) + a[k][0].astype(F32)).astype(BF16)

    def gspec(q, k):
        def index(i):
            px, py = chip(k)
            return 4 * px + 2 * py + lax.axis_index("c"), i, 0
        return pl.BlockSpec((1, tiles[q], gs[q].shape[2]), index)

    def aspec(q, k):
        def index(i):
            px, py = chip(k)
            return 2 * px + py, i, 0
        return pl.BlockSpec((1, tiles[q], gs[q].shape[2]), index)

    in_specs = [gspec(q, k) for q in range(n) for k in range(3)] + [aspec(q, k) for q in range(n) for k in range(3)]
    operands = [g for g in gs for _ in range(3)] + [a for a in as_ for _ in range(3)]
    out_specs = [pl.BlockSpec((3, tiles[q], gs[q].shape[2]), lambda i: (0, i, 0)) for q in range(n)]
    out_shape = [_sds((3,) + g.shape[1:], BF16) for g in gs]
    return _call(body, "rs_add_chip", (steps,), in_specs + [ANY] * len(after), out_specs, out_shape)(*operands, *after)


def _rs_ici(ts, l, collective_id):
    n = len(ts)

    def body(*refs):
        ins, outs = refs[:n], refs[n:2 * n]
        send_sems, recv_sems = refs[2 * n:]
        x, y, c = _mesh_pos()
        chips = _other_chips(x, y)
        _handshake([(*chip, c) for chip in chips])
        copies = []
        for q in range(n):
            for k, chip in enumerate(chips):
                copies.append(pltpu.make_async_remote_copy(
                    src_ref=ins[q].at[k], dst_ref=outs[q].at[k], send_sem=send_sems.at[3 * q + k],
                    recv_sem=recv_sems.at[3 * q + k], device_id=(*chip, c), device_id_type=MESH))
        for cp in copies:
            cp.start()
        for cp in copies:
            cp.wait()

    return _sequencer_call(
        body, f"rs_ici_{l}", [_sds(t.shape, t.dtype) for t in ts], (3 * n, 3 * n), collective_id, ts)


def _all_reduce_small(v):
    n, W = v.shape

    def body(v_ref, out_ref, slots, send_sems, recv_sems):
        x, y, c = _mesh_pos()
        me = 4 * x + 2 * y + c
        slots[me] = v_ref[...]
        copies = []
        for k in range(1, N_DEV):
            kx, ky, kc = (k >> 2) & 1, (k >> 1) & 1, k & 1
            peer = (1 - x if kx else x, 1 - y if ky else y, 1 - c if kc else c)
            copies.append(pltpu.make_async_remote_copy(
                src_ref=v_ref, dst_ref=slots.at[me], send_sem=send_sems.at[k - 1], recv_sem=recv_sems.at[k - 1],
                device_id=peer, device_id_type=MESH))
        for cp in copies:
            cp.start()
        for cp in copies:
            cp.wait()
        acc = slots[0]
        for d in range(1, N_DEV):
            acc = acc + slots[d]
        out_ref[...] = acc

    vm = pl.BlockSpec(memory_space=pltpu.VMEM)
    return pl.pallas_call(
        body, name="all_reduce_small",
        out_shape=_sds((n, W), F32),
        in_specs=[vm], out_specs=vm,
        scratch_shapes=[pltpu.VMEM((N_DEV, n, W), F32), pltpu.SemaphoreType.DMA((7,)), pltpu.SemaphoreType.DMA((7,))],
    )(v)


def _adamw_math(w, g, m, v):
    m2 = ADAM_B1 * m + (1.0 - ADAM_B1) * g
    v2 = ADAM_B2 * v + (1.0 - ADAM_B2) * (g * g)
    m_hat = m2 / (1.0 - ADAM_B1 ** ADAM_STEP)
    v_hat = v2 / (1.0 - ADAM_B2 ** ADAM_STEP)
    return -ADAM_LR * (m_hat / (jnp.sqrt(v_hat) + ADAM_EPS) + ADAM_WD * w), m2, v2


def _adamw(w, g, m, v):
    L, r, c = w.shape
    tr = _tile(r, max(SUBLANES, (256 * 1024 // c) // SUBLANES * SUBLANES), SUBLANES)

    def body(w_ref, g_ref, m_ref, v_ref, d_ref, nm_ref, nv_ref):
        d_ref[...], nm_ref[...], nv_ref[...] = _adamw_math(w_ref[...], g_ref[...], m_ref[...], v_ref[...])

    spec = pl.BlockSpec((1, tr, c), lambda l, i: (l, i, 0))
    return _call(body, "adamw", (L, r // tr), [spec] * 4, [spec] * 3, [_sds((L, r, c), F32)] * 3)(w, g, m, v)


def _adamw_reduced(w, m, v, flipped, gc, a, b, row_off, tr, l, prev, after):
    L = w.shape[0]
    c, r = w.shape[1:] if flipped else w.shape[:0:-1]
    W = gc.shape[2]
    ob = row_off // tr
    extra = list(prev or ()) + list(after)

    def own_slot(i):
        x, y, cc = _mesh_pos()
        return 4 * x + 2 * y + cc, ob + i, 0

    def own_chip(i):
        x, y, _ = _mesh_pos()
        return 2 * x + y, ob + i, 0

    def body(w_ref, m_ref, v_ref, gc_ref, a_ref, b_ref, *rest):
        g_ref, d_ref, nm_ref, nv_ref = rest[len(extra):]
        own = gc_ref[0].astype(F32) + a_ref[0].astype(F32)
        g = ((own + b_ref[0].astype(F32)) + b_ref[1].astype(F32)) + b_ref[2].astype(F32)
        g = g[:, :c].T if flipped else g[:, :c]
        g_ref[0] = g
        d_ref[0], nm_ref[0], nv_ref[0] = _adamw_math(w_ref[0], g, m_ref[0], v_ref[0])

    spec = pl.BlockSpec((1, c, tr), lambda i: (l, 0, i)) if flipped else pl.BlockSpec((1, tr, c), lambda i: (l, i, 0))
    return _call(
        body, "adamw_reduced", (r // tr,),
        [spec] * 3 + [pl.BlockSpec((1, tr, W), own_slot), pl.BlockSpec((1, tr, W), own_chip),
                      pl.BlockSpec((3, tr, W), lambda i: (0, ob + i, 0))]
        + [ANY] * len(extra),
        [spec] * 4, [_sds(w.shape, F32)] * 4,
        aliases={6 + k: k for k in range(4)} if prev else None,
    )(w, m, v, gc, a, b, *extra)


_MEMBERS = dict(gu=("ffn1_w_gu", "ffn2_w_gu"), dn=("ffn1_w_down", "ffn2_w_down"),
                sq=("w_mla_out", "w_o", "w_ple_gate"), win=("w_in",), c128=("w_conv_out", "w_ple_proj"),
                c256=("w_ukv", "w_uq"))
_GATHER_MEMBERS = dict(_MEMBERS, gu1=("ffn1_w_gu",), gu2=("ffn2_w_gu",), dn1=("ffn1_w_down",), dn2=("ffn2_w_down",))
GATHER_STAGES = (("gu1", "dn1"), ("win", "c256", "c128", "sq"), ("gu2", "dn2"))
_FLIPPED = ("ffn1_w_gu", "ffn2_w_gu", "w_in", "w_uq")
_SMALL = ("ffn1_norm", "mix_norm", "q_norm", "kv_norm", "ffn2_norm", "ple_norm")
_ORDER = ("ffn1_norm", "ffn1_w_gu", "ffn1_w_down", "mix_norm", "w_in", "conv_w", "w_conv_out", "q_norm", "kv_norm",
          "w_uq", "w_ukv", "w_mla_out", "w_o", "ffn2_norm", "ffn2_w_gu", "ffn2_w_down", "ple_norm", "w_ple_gate",
          "w_ple_proj", "final_norm")


def _class_width(wts, cls):
    return HEAD_SLOT if cls == "c256" else wts[_GATHER_MEMBERS[cls][0]].shape[2]


def _pack_rows(vecs, width):
    flat = jnp.concatenate([a.reshape(-1) for a in vecs])
    n = flat.shape[0]
    rows = -(-n // width)
    rows = -(-rows // SUBLANES) * SUBLANES
    flat = jnp.pad(flat, (0, rows * width - n))
    offs, o = [], 0
    for a in vecs:
        offs.append(o)
        o += a.size
    return flat.reshape(rows, width), offs


def _unpack_rows(packed, vecs, offs):
    flat = packed.reshape(-1)
    return [flat[o:o + a.size].reshape(a.shape) for a, o in zip(vecs, offs)]


def _train(x, p, positions, target, gathered, packs, small_w, final_norm, update, reduce_small=None):
    cs = _rope_tables(positions)
    L = len(small_w)
    h = x
    saved = []
    def gather(l, names, after, collective_id):
        got = _all_gather([packs[n] for n in names], l, after, collective_id)
        return dict(zip(names, got))

    late = None
    if packs is not None:
        first, mixer, second = GATHER_STAGES
        w0 = gather(0, first, [], 0)
        w0.update(gather(0, mixer, [w0[first[0]]], 1))
        gathered = [w0]
        late = lambda h1: gather(0, second, [h1], 2)
    everything = sum(GATHER_STAGES, ())
    for l in range(L):
        h, s = _layer_fwd(h, p[l], cs, gathered[l], small_w[l], late)
        late = None
        saved.append(s)
        if packs is not None and l + 1 < L:
            gathered.append(gather(l + 1, everything, [s["by"]], 2 + l + 1))
    dh, loss, d_final = _final_loss(h, final_norm, target)
    grads, smalls = [None] * L, [None] * L
    exchanged = None
    landing = None

    def second_stage(after):
        l, gs, as_ = exchanged
        ts = _rs_add_chip(gs, as_, [after])
        return l, gs, as_, _rs_ici(ts, l, 2 * L + 2 + l), ts[0]

    for l in reversed(range(L)):
        dh, part, small = _layer_bwd_late(dh, saved[l], gathered[l], small_w[l], [])
        pin = []
        if exchanged is not None:
            landing = second_stage(dh)
            pin = [landing[4]]
        dh, part, small = _layer_bwd_mixer(dh, part, small, saved[l], cs, gathered[l], small_w[l], pin)
        pin = [update(*landing[:4])] if exchanged is not None else []
        dh, g, smalls[l] = _layer_bwd_first(dh, part, small, saved[l], gathered[l], small_w[l], pin)
        if update is not None:
            gs = [g[cls] for cls in CLASSES]
            exchanged = (l, gs, _rs_d2d(gs, l, L + 2 + l))
        else:
            grads[l] = g
    if update is not None:
        update(*second_stage(reduce_small(smalls, d_final, loss[0, 0]))[:4])
    return loss[0, 0], dh, grads, smalls, d_final


def kernel(x, p, positions, ffn1_norm, ffn1_w_gu, ffn1_w_down, mix_norm, w_in, conv_w, w_conv_out, q_norm, kv_norm, w_uq, w_ukv, w_mla_out, w_o, ffn2_norm, ffn2_w_gu, ffn2_w_down, ple_norm, w_ple_gate, w_ple_proj, final_norm, loss_target, m_ffn1_norm, m_ffn1_w_gu, m_ffn1_w_down, m_mix_norm, m_w_in, m_conv_w, m_w_conv_out, m_q_norm, m_kv_norm, m_w_uq, m_w_ukv, m_w_mla_out, m_w_o, m_ffn2_norm, m_ffn2_w_gu, m_ffn2_w_down, m_ple_norm, m_w_ple_gate, m_w_ple_proj, m_final_norm, v_ffn1_norm, v_ffn1_w_gu, v_ffn1_w_down, v_mix_norm, v_w_in, v_conv_w, v_w_conv_out, v_q_norm, v_kv_norm, v_w_uq, v_w_ukv, v_w_mla_out, v_w_o, v_ffn2_norm, v_ffn2_w_gu, v_ffn2_w_down, v_ple_norm, v_w_ple_gate, v_w_ple_proj, v_final_norm):
    args = dict(locals())
    wts = {n: args[n] for n in _ORDER}
    L = w_in.shape[0]
    dev = 4 * lax.axis_index("x") + 2 * lax.axis_index("y") + lax.axis_index("c")

    view = lambda n, a: jnp.swapaxes(a, 1, 2) if n in _FLIPPED else a
    packs = {cls: _pack([view(n, wts[n]) for n in _GATHER_MEMBERS[cls]], [n in _FLIPPED for n in _GATHER_MEMBERS[cls]],
                        _class_width(wts, cls))
             for stage in GATHER_STAGES for cls in stage}
    cw = conv_w.shape[2]
    conv_full = lax.dynamic_update_slice(jnp.zeros((L, 3, N_DEV * cw), F32), conv_w, (0, 0, dev * cw))
    conv_packed, conv_offs = _pack_rows([conv_full], FLAT_COLS)
    conv_full = _unpack_rows(_all_reduce_small(conv_packed), [conv_full], conv_offs)[0]
    small_w = [dict({n: wts[n][l][None, :] for n in _SMALL}, conv_w=conv_full[l]) for l in range(L)]

    done = {}

    def update(l, gs, as_, bs):
        for q, cls in enumerate(CLASSES):
            off = 0
            rows = [wts[n].shape[1] for n in _MEMBERS[cls]]
            tr = _tile(math.gcd(*rows), 256, BF16_ROWS)
            for n, r in zip(_MEMBERS[cls], rows):
                done[n] = _adamw_reduced(view(n, wts[n]), view(n, args["m_" + n]), view(n, args["v_" + n]),
                                         n in _FLIPPED, gs[q], as_[q], bs[q], off, tr, l, done.get(n), [])
                off += r
        return done[_MEMBERS[CLASSES[-1]][-1]][0]

    reduced = {}

    def reduce_small(smalls, d_final, loss_dev):
        small = [jnp.stack([smalls[l][n][0] for l in range(L)]) for n in _SMALL]
        small += [jnp.stack([smalls[l]["conv_w"] for l in range(L)]), d_final[0], loss_dev[None]]
        packed, offs = _pack_rows(small, FLAT_COLS)
        total = _all_reduce_small(packed)
        reduced["small"] = _unpack_rows(total, small, offs)
        return total

    _, grad_x, _, _, _ = _train(x[0], p[:, 0], positions[0], loss_target[0], None, packs, small_w,
                                final_norm[None, :], update, reduce_small)

    small = reduced["small"]
    grad = dict(zip(_SMALL, small))
    grad["conv_w"] = lax.dynamic_slice(small[len(_SMALL)], (0, 0, dev * cw), (L, 3, cw))
    grad["final_norm"] = small[-2]
    loss = small[-1][0]

    deltas, new_m, new_v = {}, {}, {}
    for n, outs in done.items():
        grad[n], deltas[n], new_m[n], new_v[n] = (view(n, a) for a in outs)
    for n in _SMALL + ("conv_w", "final_norm"):
        w3 = wts[n].reshape((1,) * (3 - wts[n].ndim) + wts[n].shape)
        d, nm, nv = _adamw(w3, grad[n].reshape(w3.shape), args["m_" + n].reshape(w3.shape),
                           args["v_" + n].reshape(w3.shape))
        deltas[n], new_m[n], new_v[n] = (a.reshape(wts[n].shape) for a in (d, nm, nv))
    return (loss, grad_x[None], *[grad[n] for n in _ORDER], *[deltas[n] for n in _ORDER],
            *[new_m[n] for n in _ORDER], *[new_v[n] for n in _ORDER])
```
